```python
import math
import jax
import jax.numpy as jnp
from jax import lax
import numpy as np

D_MODEL = 2048
BATCH = 8
SEQ = 8192
DEPTH = 4

GDN_HEADS = 8
GDN_HEAD_DIM = 128
GDN_WIDTH = GDN_HEADS * GDN_HEAD_DIM
S5_GROUP_SIZE = 16
S5_GROUPS = 48
S5_STATE = 64
S5_WIDTH = S5_GROUPS * S5_GROUP_SIZE
M2_HEADS = 16
M2_HEAD_DIM = 64
M2_WIDTH = M2_HEADS * M2_HEAD_DIM
M2_GROUPS = 4
M2_STATE = 128
M2_CONV_DIM = M2_WIDTH + 2 * M2_GROUPS * M2_STATE
CONV_K = 4
CHUNK = 64
N_BRANCHES = 3
NORM_EPS = 1e-6
IN_SPLITS = (3 * GDN_WIDTH, GDN_WIDTH, GDN_HEADS, GDN_HEADS,
             S5_WIDTH, S5_WIDTH,
             M2_WIDTH, M2_CONV_DIM, M2_HEADS,
             N_BRANCHES * D_MODEL)
IN_DIM = 4 * GDN_WIDTH + 2 * GDN_HEADS + 2 * S5_WIDTH + M2_WIDTH + M2_CONV_DIM + M2_HEADS + N_BRANCHES * D_MODEL

kernel_name = 'hybrid_gdn_s5_ssd_gated_merge'


def rms_norm(x, w):
    x32 = x.astype(jnp.float32)
    y = x32 * lax.rsqrt(jnp.mean(x32 * x32, axis=-1, keepdims=True) + NORM_EPS)
    return (y * w.astype(jnp.float32)).astype(x.dtype)


def _rms_f32(x, w):
    return x * lax.rsqrt(jnp.mean(x * x, axis=-1, keepdims=True) + NORM_EPS) * w


def _l2norm(x):
    return x * lax.rsqrt(jnp.sum(x * x, axis=-1, keepdims=True) + NORM_EPS)


def causal_depthwise_conv(x, w, b=None):
    k, c = w.shape
    y = lax.conv_general_dilated(x, w[:, None, :].astype(x.dtype), window_strides=(1,),
                                 padding=[(k - 1, 0)], dimension_numbers=('NWC', 'WIO', 'NWC'),
                                 feature_group_count=c)
    if b is not None:
        y = y + b.astype(x.dtype)
    return y


def gated_delta_rule_chunked(q, k, v, g, beta):
    bsz, seqlen, heads, dk = q.shape
    dv = v.shape[-1]
    n = seqlen // CHUNK

    def to_chunks(t):
        return t.reshape(bsz, n, CHUNK, heads, -1).transpose(0, 1, 3, 2, 4)

    q = to_chunks(q) * (dk ** -0.5)
    k = to_chunks(k)
    v = to_chunks(v)
    beta = beta.reshape(bsz, n, CHUNK, heads).transpose(0, 1, 3, 2)
    g_cum = jnp.cumsum(g.reshape(bsz, n, CHUNK, heads).transpose(0, 1, 3, 2), axis=-1)
    causal = jnp.tril(jnp.ones((CHUNK, CHUNK), dtype=bool))
    strict = jnp.tril(jnp.ones((CHUNK, CHUNK), dtype=bool), k=-1)
    decay = jnp.exp(jnp.where(causal, g_cum[..., :, None] - g_cum[..., None, :], -jnp.inf))
    k_beta = k * beta[..., None]
    a_mat = jnp.where(strict, jnp.einsum('bnhid,bnhjd->bnhij', k_beta, k) * decay, 0.0)
    eye = jnp.eye(CHUNK, dtype=q.dtype)
    t_inv = lax.linalg.triangular_solve(a_mat + eye, jnp.broadcast_to(eye, a_mat.shape),
                                        left_side=True, lower=True)
    u = t_inv @ (v * beta[..., None])
    w = t_inv @ (k_beta * jnp.exp(g_cum)[..., None])
    qk = jnp.einsum('bnhid,bnhjd->bnhij', q, k) * decay
    q_dec = q * jnp.exp(g_cum)[..., None]
    g_last = g_cum[..., -1]
    k_tail = k * jnp.exp(g_last[..., None] - g_cum)[..., None]

    def step(state, inp):
        q_i, qk_i, u_i, w_i, kt_i, gl_i = inp
        v_new = u_i - jnp.einsum('bhck,bhkv->bhcv', w_i, state)
        out = jnp.einsum('bhck,bhkv->bhcv', q_i, state) + jnp.einsum('bhij,bhjv->bhiv', qk_i, v_new)
        state = state * jnp.exp(gl_i)[..., None, None] + jnp.einsum('bhck,bhcv->bhkv', kt_i, v_new)
        return state, out

    init = jnp.zeros((bsz, heads, dk, dv), q.dtype)
    xs = tuple(jnp.moveaxis(t, 1, 0) for t in (q_dec, qk, u, w, k_tail, g_last))
    _, out = lax.scan(step, init, xs)
    return out.transpose(1, 0, 3, 2, 4).reshape(bsz, seqlen, heads, dv)


def gdn_branch(qkv, z, b_raw, a_raw, conv_w, a_log, dt_bias, norm_w):
    bsz, seqlen, _ = qkv.shape
    f32 = jnp.float32
    qkv = jax.nn.silu(causal_depthwise_conv(qkv.astype(f32), conv_w.astype(f32)))
    q, k, v = jnp.split(qkv, 3, axis=-1)
    shp = (bsz, seqlen, GDN_HEADS, GDN_HEAD_DIM)
    q = _l2norm(q.reshape(shp))
    k = _l2norm(k.reshape(shp))
    v = v.reshape(shp)
    beta = jax.nn.sigmoid(b_raw.astype(f32))
    g = -jnp.exp(a_log.astype(f32)) * jax.nn.softplus(a_raw.astype(f32) + dt_bias.astype(f32))
    o = gated_delta_rule_chunked(q, k, v, g, beta)
    o = _rms_f32(o, norm_w.astype(f32)) * jax.nn.silu(z.astype(f32).reshape(shp))
    return o.reshape(bsz, seqlen, GDN_WIDTH)


def _complex_affine_combine(left, right):
    a1r, a1i, b1r, b1i = left
    a2r, a2i, b2r, b2i = right
    return (a2r * a1r - a2i * a1i,
            a2r * a1i + a2i * a1r,
            a2r * b1r - a2i * b1i + b2r,
            a2r * b1i + a2i * b1r + b2i)


def s5_branch(u, gate, lam_re, lam_im, log_step, b_re, b_im, c_re, c_im, d_skip, glu_w, glu_b):
    bsz, seqlen, _ = u.shape
    f32 = jnp.float32
    u = u.astype(f32).reshape(bsz, seqlen, S5_GROUPS, S5_GROUP_SIZE)
    lam_re = jnp.minimum(lam_re.astype(f32), -1e-4)
    lam_im = lam_im.astype(f32)
    step = jnp.exp(log_step.astype(f32))[:, None]
    mag = jnp.exp(lam_re * step)
    ab_re = mag * jnp.cos(lam_im * step)
    ab_im = mag * jnp.sin(lam_im * step)
    den = lam_re * lam_re + lam_im * lam_im
    f_re = ((ab_re - 1.0) * lam_re + ab_im * lam_im) / den
    f_im = (ab_im * lam_re - (ab_re - 1.0) * lam_im) / den
    b_re = b_re.astype(f32)
    b_im = b_im.astype(f32)
    bb_re = f_re[..., None] * b_re - f_im[..., None] * b_im
    bb_im = f_re[..., None] * b_im + f_im[..., None] * b_re
    bu_re = jnp.einsum('blgh,gph->blgp', u, bb_re)
    bu_im = jnp.einsum('blgh,gph->blgp', u, bb_im)
    a_re = jnp.broadcast_to(ab_re, (seqlen,) + ab_re.shape)
    a_im = jnp.broadcast_to(ab_im, (seqlen,) + ab_im.shape)

    def scan_one(br, bi):
        return lax.associative_scan(_complex_affine_combine, (a_re, a_im, br, bi), axis=0)[2:]

    s_re, s_im = jax.vmap(scan_one)(bu_re, bu_im)
    y = (jnp.einsum('blgp,ghp->blgh', s_re, c_re.astype(f32))
         - jnp.einsum('blgp,ghp->blgh', s_im, c_im.astype(f32))
         + d_skip.astype(f32) * u)
    y = jax.nn.gelu(y.reshape(bsz, seqlen, S5_WIDTH))
    y = y * jax.nn.sigmoid(y @ glu_w.astype(f32) + glu_b.astype(f32))
    return y * jax.nn.silu(gate.astype(f32))


def ssd_chunked(x, a, bm, cm):
    bsz, seqlen, heads, hd = x.shape
    groups, dstate = bm.shape[2], bm.shape[3]
    rep = heads // groups
    n = seqlen // CHUNK
    x = x.reshape(bsz, n, CHUNK, groups, rep, hd)
    bm = bm.reshape(bsz, n, CHUNK, groups, dstate)
    cm = cm.reshape(bsz, n, CHUNK, groups, dstate)
    a_cum = jnp.cumsum(a.reshape(bsz, n, CHUNK, groups, rep).transpose(0, 1, 3, 4, 2), axis=-1)
    causal = jnp.tril(jnp.ones((CHUNK, CHUNK), dtype=bool))
    seg = jnp.exp(jnp.where(causal, a_cum[..., :, None] - a_cum[..., None, :], -jnp.inf))
    scores = jnp.einsum('bclgn,bcsgn->bcgls', cm, bm)
    y_diag = jnp.einsum('bcgls,bcgrls,bcsgrp->bclgrp', scores, seg, x)
    decay_states = jnp.exp(a_cum[..., -1:] - a_cum)
    states = jnp.einsum('bclgn,bcgrl,bclgrp->bcgrpn', bm, decay_states, x)
    chunk_decay = jnp.exp(a_cum[..., -1])

    def step(s, inp):
        st, dec = inp
        return s * dec[..., None, None] + st, s

    init = jnp.zeros((bsz, groups, rep, hd, dstate), x.dtype)
    _, s_prev = lax.scan(step, init, (jnp.moveaxis(states, 1, 0), jnp.moveaxis(chunk_decay, 1, 0)))
    s_prev = jnp.moveaxis(s_prev, 0, 1)
    y_off = jnp.einsum('bclgn,bcgrpn,bcgrl->bclgrp', cm, s_prev, jnp.exp(a_cum))
    return (y_diag + y_off).reshape(bsz, seqlen, heads, hd)


def mamba2_branch(z, xbc, dt_raw, conv_w, conv_b, a_log, dt_bias, d_skip, norm_w):
    bsz, seqlen, _ = z.shape
    f32 = jnp.float32
    xbc = jax.nn.silu(causal_depthwise_conv(xbc.astype(f32), conv_w.astype(f32), conv_b.astype(f32)))
    xs, bm, cm = jnp.split(xbc, [M2_WIDTH, M2_WIDTH + M2_GROUPS * M2_STATE], axis=-1)
    xs = xs.reshape(bsz, seqlen, M2_HEADS, M2_HEAD_DIM)
    bm = bm.reshape(bsz, seqlen, M2_GROUPS, M2_STATE)
    cm = cm.reshape(bsz, seqlen, M2_GROUPS, M2_STATE)
    dt = jax.nn.softplus(dt_raw.astype(f32) + dt_bias.astype(f32))
    a = -jnp.exp(a_log.astype(f32))
    y = ssd_chunked(xs * dt[..., None], a * dt, bm, cm) + d_skip.astype(f32)[:, None] * xs
    y = y.reshape(bsz, seqlen, M2_WIDTH) * jax.nn.silu(z.astype(f32))
    y = _rms_f32(y.reshape(bsz, seqlen, M2_GROUPS, M2_WIDTH // M2_GROUPS), 1.0)
    return y.reshape(bsz, seqlen, M2_WIDTH) * norm_w.astype(f32)


def hybrid_layer(x, norm_w, w_in, gdn_conv_w, gdn_a_log, gdn_dt_bias, gdn_norm_w,
                 s5_lam_re, s5_lam_im, s5_log_step, s5_b_re, s5_b_im, s5_c_re, s5_c_im, s5_d,
                 s5_glu_w, s5_glu_b, m2_conv_w, m2_conv_b, m2_a_log, m2_dt_bias, m2_d, m2_norm_w,
                 proj_a, proj_b, proj_c, w_out):
    h = rms_norm(x, norm_w)
    proj = h @ w_in
    idx = [int(i) for i in np.cumsum(IN_SPLITS)[:-1]]
    (qkv, a_z, a_beta, a_decay, s_u, s_gate, c_z, c_xbc, c_dt, merge) = jnp.split(proj, idx, axis=-1)
    y_a = gdn_branch(qkv, a_z, a_beta, a_decay, gdn_conv_w, gdn_a_log, gdn_dt_bias, gdn_norm_w)
    y_b = s5_branch(s_u, s_gate, s5_lam_re, s5_lam_im, s5_log_step, s5_b_re, s5_b_im,
                    s5_c_re, s5_c_im, s5_d, s5_glu_w, s5_glu_b)
    y_c = mamba2_branch(c_z, c_xbc, c_dt, m2_conv_w, m2_conv_b, m2_a_log, m2_dt_bias, m2_d, m2_norm_w)
    g_a, g_b, g_c = jnp.split(jax.nn.sigmoid(merge), N_BRANCHES, axis=-1)
    dt = x.dtype
    merged = (g_a * (y_a.astype(dt) @ proj_a)
              + g_b * (y_b.astype(dt) @ proj_b)
              + g_c * (y_c.astype(dt) @ proj_c))
    return x + merged @ w_out


def _inv_softplus(y):
    return y + jnp.log(-jnp.expm1(-y))


def _log_uniform(key, shape, lo, hi):
    return jnp.exp(jax.random.uniform(key, shape, jnp.float32, math.log(lo), math.log(hi)))


def _fwd_setup_inputs(seed: int = 0) -> dict:
    key = jax.random.key(seed)
    ks = jax.random.split(key, 32)
    f32 = jnp.float32

    def nrm(k, shape, scale):
        return scale * jax.random.normal(k, shape, f32)

    x = jax.random.normal(ks[0], (BATCH, SEQ, D_MODEL), f32)
    norm_w = 1.0 + nrm(ks[1], (DEPTH, D_MODEL), 0.02)
    w_in = nrm(ks[2], (DEPTH, D_MODEL, IN_DIM), D_MODEL ** -0.5)
    gdn_conv_w = nrm(ks[3], (DEPTH, CONV_K, 3 * GDN_WIDTH), CONV_K ** -0.5)
    gdn_a_log = jnp.log(jax.random.uniform(ks[4], (DEPTH, GDN_HEADS), f32, 1.0, 16.0))
    gdn_dt_bias = _inv_softplus(_log_uniform(ks[5], (DEPTH, GDN_HEADS), 1e-3, 1e-1))
    gdn_norm_w = 1.0 + nrm(ks[6], (DEPTH, GDN_HEAD_DIM), 0.02)
    n_idx = jnp.arange(S5_STATE, dtype=f32)
    s5_lam_re = -0.5 + nrm(ks[7], (DEPTH, S5_GROUPS, S5_STATE), 1e-3)
    s5_lam_im = math.pi * n_idx + nrm(ks[8], (DEPTH, S5_GROUPS, S5_STATE), 1e-3)
    s5_log_step = jax.random.uniform(ks[9], (DEPTH, S5_GROUPS), f32, math.log(1e-3), math.log(1e-1))
    s5_b_re = nrm(ks[10], (DEPTH, S5_GROUPS, S5_STATE, S5_GROUP_SIZE), S5_GROUP_SIZE ** -0.5)
    s5_b_im = nrm(ks[11], (DEPTH, S5_GROUPS, S5_STATE, S5_GROUP_SIZE), S5_GROUP_SIZE ** -0.5)
    s5_c_re = nrm(ks[12], (DEPTH, S5_GROUPS, S5_GROUP_SIZE, S5_STATE), 0.5)
    s5_c_im = nrm(ks[13], (DEPTH, S5_GROUPS, S5_GROUP_SIZE, S5_STATE), 0.5)
    s5_d = nrm(ks[14], (DEPTH, S5_GROUPS, S5_GROUP_SIZE), 1.0)
    s5_glu_w = nrm(ks[15], (DEPTH, S5_WIDTH, S5_WIDTH), S5_WIDTH ** -0.5)
    s5_glu_b = nrm(ks[16], (DEPTH, S5_WIDTH), 0.01)
    m2_conv_w = nrm(ks[17], (DEPTH, CONV_K, M2_CONV_DIM), CONV_K ** -0.5)
    m2_conv_b = nrm(ks[18], (DEPTH, M2_CONV_DIM), 0.01)
    m2_a_log = jnp.log(jax.random.uniform(ks[19], (DEPTH, M2_HEADS), f32, 1.0, 16.0))
    m2_dt_bias = _inv_softplus(_log_uniform(ks[20], (DEPTH, M2_HEADS), 1e-3, 1e-1))
    m2_d = 1.0 + nrm(ks[21], (DEPTH, M2_HEADS), 0.1)
    m2_norm_w = 1.0 + nrm(ks[22], (DEPTH, M2_WIDTH), 0.02)
    proj_a = nrm(ks[23], (DEPTH, GDN_WIDTH, D_MODEL), GDN_WIDTH ** -0.5)
    proj_b = nrm(ks[24], (DEPTH, S5_WIDTH, D_MODEL), S5_WIDTH ** -0.5)
    proj_c = nrm(ks[25], (DEPTH, M2_WIDTH, D_MODEL), M2_WIDTH ** -0.5)
    w_out = nrm(ks[26], (DEPTH, D_MODEL, D_MODEL), D_MODEL ** -0.5)
    final_norm_w = 1.0 + nrm(ks[27], (D_MODEL,), 0.02)
    return {'x': x, 'norm_w': norm_w, 'w_in': w_in,
            'gdn_conv_w': gdn_conv_w, 'gdn_a_log': gdn_a_log, 'gdn_dt_bias': gdn_dt_bias, 'gdn_norm_w': gdn_norm_w,
            's5_lam_re': s5_lam_re, 's5_lam_im': s5_lam_im, 's5_log_step': s5_log_step,
            's5_b_re': s5_b_re, 's5_b_im': s5_b_im, 's5_c_re': s5_c_re, 's5_c_im': s5_c_im, 's5_d': s5_d,
            's5_glu_w': s5_glu_w, 's5_glu_b': s5_glu_b,
            'm2_conv_w': m2_conv_w, 'm2_conv_b': m2_conv_b, 'm2_a_log': m2_a_log, 'm2_dt_bias': m2_dt_bias,
            'm2_d': m2_d, 'm2_norm_w': m2_norm_w,
            'proj_a': proj_a, 'proj_b': proj_b, 'proj_c': proj_c, 'w_out': w_out,
            'final_norm_w': final_norm_w}


def _fwd_reference(x, norm_w, w_in, gdn_conv_w, gdn_a_log, gdn_dt_bias, gdn_norm_w,
              s5_lam_re, s5_lam_im, s5_log_step, s5_b_re, s5_b_im, s5_c_re, s5_c_im, s5_d,
              s5_glu_w, s5_glu_b, m2_conv_w, m2_conv_b, m2_a_log, m2_dt_bias, m2_d, m2_norm_w,
              proj_a, proj_b, proj_c, w_out, final_norm_w):
    for i in range(DEPTH):
        x = hybrid_layer(x, norm_w[i], w_in[i], gdn_conv_w[i], gdn_a_log[i], gdn_dt_bias[i], gdn_norm_w[i],
                         s5_lam_re[i], s5_lam_im[i], s5_log_step[i], s5_b_re[i], s5_b_im[i],
                         s5_c_re[i], s5_c_im[i], s5_d[i], s5_glu_w[i], s5_glu_b[i],
                         m2_conv_w[i], m2_conv_b[i], m2_a_log[i], m2_dt_bias[i], m2_d[i], m2_norm_w[i],
                         proj_a[i], proj_b[i], proj_c[i], w_out[i])
    return rms_norm(x, final_norm_w)


import jax as _jax
import jax.numpy as _jnp

TWIN_FORMAT = 'train_step'
FWD_PARAMS = ['x', 'norm_w', 'w_in', 'gdn_conv_w', 'gdn_a_log', 'gdn_dt_bias', 'gdn_norm_w', 's5_lam_re', 's5_lam_im', 's5_log_step', 's5_b_re', 's5_b_im', 's5_c_re', 's5_c_im', 's5_d', 's5_glu_w', 's5_glu_b', 'm2_conv_w', 'm2_conv_b', 'm2_a_log', 'm2_dt_bias', 'm2_d', 'm2_norm_w', 'proj_a', 'proj_b', 'proj_c', 'w_out', 'final_norm_w']
TWIN_WEIGHTS = ['norm_w', 'w_in', 'gdn_conv_w', 'gdn_a_log', 'gdn_dt_bias', 'gdn_norm_w', 's5_lam_re', 's5_lam_im', 's5_log_step', 's5_b_re', 's5_b_im', 's5_c_re', 's5_c_im', 's5_d', 's5_glu_w', 's5_glu_b', 'm2_conv_w', 'm2_conv_b', 'm2_a_log', 'm2_dt_bias', 'm2_d', 'm2_norm_w', 'proj_a', 'proj_b', 'proj_c', 'w_out', 'final_norm_w']
TWIN_DIFF_INPUT = 'x'
TWIN_INPUTS = ['x', 'norm_w', 'w_in', 'gdn_conv_w', 'gdn_a_log', 'gdn_dt_bias', 'gdn_norm_w', 's5_lam_re', 's5_lam_im', 's5_log_step', 's5_b_re', 's5_b_im', 's5_c_re', 's5_c_im', 's5_d', 's5_glu_w', 's5_glu_b', 'm2_conv_w', 'm2_conv_b', 'm2_a_log', 'm2_dt_bias', 'm2_d', 'm2_norm_w', 'proj_a', 'proj_b', 'proj_c', 'w_out', 'final_norm_w', 'loss_target', 'm_norm_w', 'm_w_in', 'm_gdn_conv_w', 'm_gdn_a_log', 'm_gdn_dt_bias', 'm_gdn_norm_w', 'm_s5_lam_re', 'm_s5_lam_im', 'm_s5_log_step', 'm_s5_b_re', 'm_s5_b_im', 'm_s5_c_re', 'm_s5_c_im', 'm_s5_d', 'm_s5_glu_w', 'm_s5_glu_b', 'm_m2_conv_w', 'm_m2_conv_b', 'm_m2_a_log', 'm_m2_dt_bias', 'm_m2_d', 'm_m2_norm_w', 'm_proj_a', 'm_proj_b', 'm_proj_c', 'm_w_out', 'm_final_norm_w', 'v_norm_w', 'v_w_in', 'v_gdn_conv_w', 'v_gdn_a_log', 'v_gdn_dt_bias', 'v_gdn_norm_w', 'v_s5_lam_re', 'v_s5_lam_im', 'v_s5_log_step', 'v_s5_b_re', 'v_s5_b_im', 'v_s5_c_re', 'v_s5_c_im', 'v_s5_d', 'v_s5_glu_w', 'v_s5_glu_b', 'v_m2_conv_w', 'v_m2_conv_b', 'v_m2_a_log', 'v_m2_dt_bias', 'v_m2_d', 'v_m2_norm_w', 'v_proj_a', 'v_proj_b', 'v_proj_c', 'v_w_out', 'v_final_norm_w']
TWIN_OUTPUTS = ['loss', 'grad_x', 'grad_norm_w', 'grad_w_in', 'grad_gdn_conv_w', 'grad_gdn_a_log', 'grad_gdn_dt_bias', 'grad_gdn_norm_w', 'grad_s5_lam_re', 'grad_s5_lam_im', 'grad_s5_log_step', 'grad_s5_b_re', 'grad_s5_b_im', 'grad_s5_c_re', 'grad_s5_c_im', 'grad_s5_d', 'grad_s5_glu_w', 'grad_s5_glu_b', 'grad_m2_conv_w', 'grad_m2_conv_b', 'grad_m2_a_log', 'grad_m2_dt_bias', 'grad_m2_d', 'grad_m2_norm_w', 'grad_proj_a', 'grad_proj_b', 'grad_proj_c', 'grad_w_out', 'grad_final_norm_w', 'delta_norm_w', 'delta_w_in', 'delta_gdn_conv_w', 'delta_gdn_a_log', 'delta_gdn_dt_bias', 'delta_gdn_norm_w', 'delta_s5_lam_re', 'delta_s5_lam_im', 'delta_s5_log_step', 'delta_s5_b_re', 'delta_s5_b_im', 'delta_s5_c_re', 'delta_s5_c_im', 'delta_s5_d', 'delta_s5_glu_w', 'delta_s5_glu_b', 'delta_m2_conv_w', 'delta_m2_conv_b', 'delta_m2_a_log', 'delta_m2_dt_bias', 'delta_m2_d', 'delta_m2_norm_w', 'delta_proj_a', 'delta_proj_b', 'delta_proj_c', 'delta_w_out', 'delta_final_norm_w', 'new_m_norm_w', 'new_m_w_in', 'new_m_gdn_conv_w', 'new_m_gdn_a_log', 'new_m_gdn_dt_bias', 'new_m_gdn_norm_w', 'new_m_s5_lam_re', 'new_m_s5_lam_im', 'new_m_s5_log_step', 'new_m_s5_b_re', 'new_m_s5_b_im', 'new_m_s5_c_re', 'new_m_s5_c_im', 'new_m_s5_d', 'new_m_s5_glu_w', 'new_m_s5_glu_b', 'new_m_m2_conv_w', 'new_m_m2_conv_b', 'new_m_m2_a_log', 'new_m_m2_dt_bias', 'new_m_m2_d', 'new_m_m2_norm_w', 'new_m_proj_a', 'new_m_proj_b', 'new_m_proj_c', 'new_m_w_out', 'new_m_final_norm_w', 'new_v_norm_w', 'new_v_w_in', 'new_v_gdn_conv_w', 'new_v_gdn_a_log', 'new_v_gdn_dt_bias', 'new_v_gdn_norm_w', 'new_v_s5_lam_re', 'new_v_s5_lam_im', 'new_v_s5_log_step', 'new_v_s5_b_re', 'new_v_s5_b_im', 'new_v_s5_c_re', 'new_v_s5_c_im', 'new_v_s5_d', 'new_v_s5_glu_w', 'new_v_s5_glu_b', 'new_v_m2_conv_w', 'new_v_m2_conv_b', 'new_v_m2_a_log', 'new_v_m2_dt_bias', 'new_v_m2_d', 'new_v_m2_norm_w', 'new_v_proj_a', 'new_v_proj_b', 'new_v_proj_c', 'new_v_w_out', 'new_v_final_norm_w']
TWIN_LEAF_KINDS = {'loss': 'loss', 'grad_x': 'grad_x', 'grad_norm_w': 'grad_w', 'grad_w_in': 'grad_w', 'grad_gdn_conv_w': 'grad_w', 'grad_gdn_a_log': 'grad_w', 'grad_gdn_dt_bias': 'grad_w', 'grad_gdn_norm_w': 'grad_w', 'grad_s5_lam_re': 'grad_w', 'grad_s5_lam_im': 'grad_w', 'grad_s5_log_step': 'grad_w', 'grad_s5_b_re': 'grad_w', 'grad_s5_b_im': 'grad_w', 'grad_s5_c_re': 'grad_w', 'grad_s5_c_im': 'grad_w', 'grad_s5_d': 'grad_w', 'grad_s5_glu_w': 'grad_w', 'grad_s5_glu_b': 'grad_w', 'grad_m2_conv_w': 'grad_w', 'grad_m2_conv_b': 'grad_w', 'grad_m2_a_log': 'grad_w', 'grad_m2_dt_bias': 'grad_w', 'grad_m2_d': 'grad_w', 'grad_m2_norm_w': 'grad_w', 'grad_proj_a': 'grad_w', 'grad_proj_b': 'grad_w', 'grad_proj_c': 'grad_w', 'grad_w_out': 'grad_w', 'grad_final_norm_w': 'grad_w', 'delta_norm_w': 'delta_w', 'delta_w_in': 'delta_w', 'delta_gdn_conv_w': 'delta_w', 'delta_gdn_a_log': 'delta_w', 'delta_gdn_dt_bias': 'delta_w', 'delta_gdn_norm_w': 'delta_w', 'delta_s5_lam_re': 'delta_w', 'delta_s5_lam_im': 'delta_w', 'delta_s5_log_step': 'delta_w', 'delta_s5_b_re': 'delta_w', 'delta_s5_b_im': 'delta_w', 'delta_s5_c_re': 'delta_w', 'delta_s5_c_im': 'delta_w', 'delta_s5_d': 'delta_w', 'delta_s5_glu_w': 'delta_w', 'delta_s5_glu_b': 'delta_w', 'delta_m2_conv_w': 'delta_w', 'delta_m2_conv_b': 'delta_w', 'delta_m2_a_log': 'delta_w', 'delta_m2_dt_bias': 'delta_w', 'delta_m2_d': 'delta_w', 'delta_m2_norm_w': 'delta_w', 'delta_proj_a': 'delta_w', 'delta_proj_b': 'delta_w', 'delta_proj_c': 'delta_w', 'delta_w_out': 'delta_w', 'delta_final_norm_w': 'delta_w', 'new_m_norm_w': 'new_m', 'new_m_w_in': 'new_m', 'new_m_gdn_conv_w': 'new_m', 'new_m_gdn_a_log': 'new_m', 'new_m_gdn_dt_bias': 'new_m', 'new_m_gdn_norm_w': 'new_m', 'new_m_s5_lam_re': 'new_m', 'new_m_s5_lam_im': 'new_m', 'new_m_s5_log_step': 'new_m', 'new_m_s5_b_re': 'new_m', 'new_m_s5_b_im': 'new_m', 'new_m_s5_c_re': 'new_m', 'new_m_s5_c_im': 'new_m', 'new_m_s5_d': 'new_m', 'new_m_s5_glu_w': 'new_m', 'new_m_s5_glu_b': 'new_m', 'new_m_m2_conv_w': 'new_m', 'new_m_m2_conv_b': 'new_m', 'new_m_m2_a_log': 'new_m', 'new_m_m2_dt_bias': 'new_m', 'new_m_m2_d': 'new_m', 'new_m_m2_norm_w': 'new_m', 'new_m_proj_a': 'new_m', 'new_m_proj_b': 'new_m', 'new_m_proj_c': 'new_m', 'new_m_w_out': 'new_m', 'new_m_final_norm_w': 'new_m', 'new_v_norm_w': 'new_v', 'new_v_w_in': 'new_v', 'new_v_gdn_conv_w': 'new_v', 'new_v_gdn_a_log': 'new_v', 'new_v_gdn_dt_bias': 'new_v', 'new_v_gdn_norm_w': 'new_v', 'new_v_s5_lam_re': 'new_v', 'new_v_s5_lam_im': 'new_v', 'new_v_s5_log_step': 'new_v', 'new_v_s5_b_re': 'new_v', 'new_v_s5_b_im': 'new_v', 'new_v_s5_c_re': 'new_v', 'new_v_s5_c_im': 'new_v', 'new_v_s5_d': 'new_v', 'new_v_s5_glu_w': 'new_v', 'new_v_s5_glu_b': 'new_v', 'new_v_m2_conv_w': 'new_v', 'new_v_m2_conv_b': 'new_v', 'new_v_m2_a_log': 'new_v', 'new_v_m2_dt_bias': 'new_v', 'new_v_m2_d': 'new_v', 'new_v_m2_norm_w': 'new_v', 'new_v_proj_a': 'new_v', 'new_v_proj_b': 'new_v', 'new_v_proj_c': 'new_v', 'new_v_w_out': 'new_v', 'new_v_final_norm_w': 'new_v'}


def _forward(args):
    return _fwd_reference(*[args[k] for k in FWD_PARAMS])


def _output_shape():
    def fwd():
        inp = _fwd_setup_inputs(0)
        return _fwd_reference(*[inp[k] for k in FWD_PARAMS])
    out = _jax.eval_shape(fwd)
    return out.shape, out.dtype

N_MICROBATCH = 1
ADAM_LR = 0.001
ADAM_B1 = 0.9
ADAM_B2 = 0.999
ADAM_EPS = 1e-08
ADAM_WD = 0.01
ADAM_STEP = 10
PER_EXAMPLE_BATCH_AXIS = {'x': 0, 'loss_target': 0}
SHARED_INPUTS = []
_WEIGHT_DTYPES = {'norm_w': _jnp.float32, 'w_in': _jnp.float32, 'gdn_conv_w': _jnp.float32, 'gdn_a_log': _jnp.float32, 'gdn_dt_bias': _jnp.float32, 'gdn_norm_w': _jnp.float32, 's5_lam_re': _jnp.float32, 's5_lam_im': _jnp.float32, 's5_log_step': _jnp.float32, 's5_b_re': _jnp.float32, 's5_b_im': _jnp.float32, 's5_c_re': _jnp.float32, 's5_c_im': _jnp.float32, 's5_d': _jnp.float32, 's5_glu_w': _jnp.float32, 's5_glu_b': _jnp.float32, 'm2_conv_w': _jnp.float32, 'm2_conv_b': _jnp.float32, 'm2_a_log': _jnp.float32, 'm2_dt_bias': _jnp.float32, 'm2_d': _jnp.float32, 'm2_norm_w': _jnp.float32, 'proj_a': _jnp.float32, 'proj_b': _jnp.float32, 'proj_c': _jnp.float32, 'w_out': _jnp.float32, 'final_norm_w': _jnp.float32}
MOMENT_SCALE = {'norm_w': 1.198459e-01, 'w_in': 4.376090e-02, 'gdn_conv_w': 4.034291e-02, 'gdn_a_log': 2.315541e-01, 'gdn_dt_bias': 2.217546e-01, 'gdn_norm_w': 1.524228e-01, 's5_lam_re': 1.061241e-02, 's5_lam_im': 1.145389e-02, 's5_log_step': 1.019408e+01, 's5_b_re': 4.746373e-03, 's5_b_im': 4.651475e-03, 's5_c_re': 2.378830e-03, 's5_c_im': 2.325983e-03, 's5_d': 2.394249e-02, 's5_glu_w': 8.104472e-03, 's5_glu_b': 1.069112e-02, 'm2_conv_w': 6.713986e-02, 'm2_conv_b': 9.329387e-02, 'm2_a_log': 2.951688e-01, 'm2_dt_bias': 2.857763e-01, 'm2_d': 4.889157e-01, 'm2_norm_w': 8.822635e-02, 'proj_a': 3.648291e-02, 'proj_b': 1.532127e-02, 'proj_c': 6.245925e-02, 'w_out': 7.400873e-02, 'final_norm_w': 3.197310e+01}


def _to_microbatches(a, axis):
    t = _jnp.moveaxis(a, axis, 0)
    t = t.reshape((N_MICROBATCH, t.shape[0] // N_MICROBATCH) + t.shape[1:])
    return _jnp.moveaxis(t, 1, axis + 1)


def setup_inputs(seed: int = 0) -> dict:
    inp = _fwd_setup_inputs(seed)
    key = _jax.random.fold_in(_jax.random.key(seed), 7919)
    shape, _ = _output_shape()
    out = dict(inp)
    out["loss_target"] = _jax.random.normal(_jax.random.fold_in(key, 0), shape, _jnp.float32)
    for i, name in enumerate(TWIN_WEIGHTS):
        w = inp[name].astype(_jnp.float32)
        if MOMENT_SCALE is None:
            s = _jnp.sqrt(_jnp.mean(_jnp.square(w)) + 1e-30)
        else:
            s = MOMENT_SCALE[name]
        km, kv = _jax.random.split(_jax.random.fold_in(key, i + 1))
        out[name] = w
        out["m_" + name] = s * _jax.random.normal(km, w.shape, _jnp.float32)
        out["v_" + name] = (s * s) * _jax.random.uniform(kv, w.shape, _jnp.float32, 0.5, 1.5)
    if N_MICROBATCH > 1:
        for name, axis in PER_EXAMPLE_BATCH_AXIS.items():
            out[name] = _to_microbatches(out[name], axis)
    return {'x': out['x'], 'norm_w': out['norm_w'], 'w_in': out['w_in'], 'gdn_conv_w': out['gdn_conv_w'], 'gdn_a_log': out['gdn_a_log'], 'gdn_dt_bias': out['gdn_dt_bias'], 'gdn_norm_w': out['gdn_norm_w'], 's5_lam_re': out['s5_lam_re'], 's5_lam_im': out['s5_lam_im'], 's5_log_step': out['s5_log_step'], 's5_b_re': out['s5_b_re'], 's5_b_im': out['s5_b_im'], 's5_c_re': out['s5_c_re'], 's5_c_im': out['s5_c_im'], 's5_d': out['s5_d'], 's5_glu_w': out['s5_glu_w'], 's5_glu_b': out['s5_glu_b'], 'm2_conv_w': out['m2_conv_w'], 'm2_conv_b': out['m2_conv_b'], 'm2_a_log': out['m2_a_log'], 'm2_dt_bias': out['m2_dt_bias'], 'm2_d': out['m2_d'], 'm2_norm_w': out['m2_norm_w'], 'proj_a': out['proj_a'], 'proj_b': out['proj_b'], 'proj_c': out['proj_c'], 'w_out': out['w_out'], 'final_norm_w': out['final_norm_w'], 'loss_target': out['loss_target'], 'm_norm_w': out['m_norm_w'], 'm_w_in': out['m_w_in'], 'm_gdn_conv_w': out['m_gdn_conv_w'], 'm_gdn_a_log': out['m_gdn_a_log'], 'm_gdn_dt_bias': out['m_gdn_dt_bias'], 'm_gdn_norm_w': out['m_gdn_norm_w'], 'm_s5_lam_re': out['m_s5_lam_re'], 'm_s5_lam_im': out['m_s5_lam_im'], 'm_s5_log_step': out['m_s5_log_step'], 'm_s5_b_re': out['m_s5_b_re'], 'm_s5_b_im': out['m_s5_b_im'], 'm_s5_c_re': out['m_s5_c_re'], 'm_s5_c_im': out['m_s5_c_im'], 'm_s5_d': out['m_s5_d'], 'm_s5_glu_w': out['m_s5_glu_w'], 'm_s5_glu_b': out['m_s5_glu_b'], 'm_m2_conv_w': out['m_m2_conv_w'], 'm_m2_conv_b': out['m_m2_conv_b'], 'm_m2_a_log': out['m_m2_a_log'], 'm_m2_dt_bias': out['m_m2_dt_bias'], 'm_m2_d': out['m_m2_d'], 'm_m2_norm_w': out['m_m2_norm_w'], 'm_proj_a': out['m_proj_a'], 'm_proj_b': out['m_proj_b'], 'm_proj_c': out['m_proj_c'], 'm_w_out': out['m_w_out'], 'm_final_norm_w': out['m_final_norm_w'], 'v_norm_w': out['v_norm_w'], 'v_w_in': out['v_w_in'], 'v_gdn_conv_w': out['v_gdn_conv_w'], 'v_gdn_a_log': out['v_gdn_a_log'], 'v_gdn_dt_bias': out['v_gdn_dt_bias'], 'v_gdn_norm_w': out['v_gdn_norm_w'], 'v_s5_lam_re': out['v_s5_lam_re'], 'v_s5_lam_im': out['v_s5_lam_im'], 'v_s5_log_step': out['v_s5_log_step'], 'v_s5_b_re': out['v_s5_b_re'], 'v_s5_b_im': out['v_s5_b_im'], 'v_s5_c_re': out['v_s5_c_re'], 'v_s5_c_im': out['v_s5_c_im'], 'v_s5_d': out['v_s5_d'], 'v_s5_glu_w': out['v_s5_glu_w'], 'v_s5_glu_b': out['v_s5_glu_b'], 'v_m2_conv_w': out['v_m2_conv_w'], 'v_m2_conv_b': out['v_m2_conv_b'], 'v_m2_a_log': out['v_m2_a_log'], 'v_m2_dt_bias': out['v_m2_dt_bias'], 'v_m2_d': out['v_m2_d'], 'v_m2_norm_w': out['v_m2_norm_w'], 'v_proj_a': out['v_proj_a'], 'v_proj_b': out['v_proj_b'], 'v_proj_c': out['v_proj_c'], 'v_w_out': out['v_w_out'], 'v_final_norm_w': out['v_final_norm_w']}


def _loss(weights, diff, rest, loss_target):
    with _jax.named_scope("forward"):
        args = {**rest, TWIN_DIFF_INPUT: diff, **{k: w.astype(_WEIGHT_DTYPES[k]) for k, w in weights.items()}}
        y = _forward(args)
    with _jax.named_scope("loss_head"):
        err = _jnp.square(y.astype(_jnp.float32) - loss_target)
        return 0.5 * _jnp.sum(_jnp.mean(err, axis=-1)) if err.ndim else 0.5 * err


def _adamw(w, g, m, v):
    m = ADAM_B1 * m + (1.0 - ADAM_B1) * g
    v = ADAM_B2 * v + (1.0 - ADAM_B2) * _jnp.square(g)
    m_hat = m / (1.0 - ADAM_B1 ** ADAM_STEP)
    v_hat = v / (1.0 - ADAM_B2 ** ADAM_STEP)
    delta = -ADAM_LR * (m_hat / (_jnp.sqrt(v_hat) + ADAM_EPS) + ADAM_WD * w)
    return delta, m, v


def reference(x, norm_w, w_in, gdn_conv_w, gdn_a_log, gdn_dt_bias, gdn_norm_w, s5_lam_re, s5_lam_im, s5_log_step, s5_b_re, s5_b_im, s5_c_re, s5_c_im, s5_d, s5_glu_w, s5_glu_b, m2_conv_w, m2_conv_b, m2_a_log, m2_dt_bias, m2_d, m2_norm_w, proj_a, proj_b, proj_c, w_out, final_norm_w, loss_target, m_norm_w, m_w_in, m_gdn_conv_w, m_gdn_a_log, m_gdn_dt_bias, m_gdn_norm_w, m_s5_lam_re, m_s5_lam_im, m_s5_log_step, m_s5_b_re, m_s5_b_im, m_s5_c_re, m_s5_c_im, m_s5_d, m_s5_glu_w, m_s5_glu_b, m_m2_conv_w, m_m2_conv_b, m_m2_a_log, m_m2_dt_bias, m_m2_d, m_m2_norm_w, m_proj_a, m_proj_b, m_proj_c, m_w_out, m_final_norm_w, v_norm_w, v_w_in, v_gdn_conv_w, v_gdn_a_log, v_gdn_dt_bias, v_gdn_norm_w, v_s5_lam_re, v_s5_lam_im, v_s5_log_step, v_s5_b_re, v_s5_b_im, v_s5_c_re, v_s5_c_im, v_s5_d, v_s5_glu_w, v_s5_glu_b, v_m2_conv_w, v_m2_conv_b, v_m2_a_log, v_m2_dt_bias, v_m2_d, v_m2_norm_w, v_proj_a, v_proj_b, v_proj_c, v_w_out, v_final_norm_w):
    given = dict(x=x, norm_w=norm_w, w_in=w_in, gdn_conv_w=gdn_conv_w, gdn_a_log=gdn_a_log, gdn_dt_bias=gdn_dt_bias, gdn_norm_w=gdn_norm_w, s5_lam_re=s5_lam_re, s5_lam_im=s5_lam_im, s5_log_step=s5_log_step, s5_b_re=s5_b_re, s5_b_im=s5_b_im, s5_c_re=s5_c_re, s5_c_im=s5_c_im, s5_d=s5_d, s5_glu_w=s5_glu_w, s5_glu_b=s5_glu_b, m2_conv_w=m2_conv_w, m2_conv_b=m2_conv_b, m2_a_log=m2_a_log, m2_dt_bias=m2_dt_bias, m2_d=m2_d, m2_norm_w=m2_norm_w, proj_a=proj_a, proj_b=proj_b, proj_c=proj_c, w_out=w_out, final_norm_w=final_norm_w, loss_target=loss_target, m_norm_w=m_norm_w, m_w_in=m_w_in, m_gdn_conv_w=m_gdn_conv_w, m_gdn_a_log=m_gdn_a_log, m_gdn_dt_bias=m_gdn_dt_bias, m_gdn_norm_w=m_gdn_norm_w, m_s5_lam_re=m_s5_lam_re, m_s5_lam_im=m_s5_lam_im, m_s5_log_step=m_s5_log_step, m_s5_b_re=m_s5_b_re, m_s5_b_im=m_s5_b_im, m_s5_c_re=m_s5_c_re, m_s5_c_im=m_s5_c_im, m_s5_d=m_s5_d, m_s5_glu_w=m_s5_glu_w, m_s5_glu_b=m_s5_glu_b, m_m2_conv_w=m_m2_conv_w, m_m2_conv_b=m_m2_conv_b, m_m2_a_log=m_m2_a_log, m_m2_dt_bias=m_m2_dt_bias, m_m2_d=m_m2_d, m_m2_norm_w=m_m2_norm_w, m_proj_a=m_proj_a, m_proj_b=m_proj_b, m_proj_c=m_proj_c, m_w_out=m_w_out, m_final_norm_w=m_final_norm_w, v_norm_w=v_norm_w, v_w_in=v_w_in, v_gdn_conv_w=v_gdn_conv_w, v_gdn_a_log=v_gdn_a_log, v_gdn_dt_bias=v_gdn_dt_bias, v_gdn_norm_w=v_gdn_norm_w, v_s5_lam_re=v_s5_lam_re, v_s5_lam_im=v_s5_lam_im, v_s5_log_step=v_s5_log_step, v_s5_b_re=v_s5_b_re, v_s5_b_im=v_s5_b_im, v_s5_c_re=v_s5_c_re, v_s5_c_im=v_s5_c_im, v_s5_d=v_s5_d, v_s5_glu_w=v_s5_glu_w, v_s5_glu_b=v_s5_glu_b, v_m2_conv_w=v_m2_conv_w, v_m2_conv_b=v_m2_conv_b, v_m2_a_log=v_m2_a_log, v_m2_dt_bias=v_m2_dt_bias, v_m2_d=v_m2_d, v_m2_norm_w=v_m2_norm_w, v_proj_a=v_proj_a, v_proj_b=v_proj_b, v_proj_c=v_proj_c, v_w_out=v_w_out, v_final_norm_w=v_final_norm_w)
    weights = {n: given[n] for n in TWIN_WEIGHTS}
    shared = {n: given[n] for n in SHARED_INPUTS}
    per_example = {n: given[n] for n in ['x']}
    grad_fn = _jax.value_and_grad(_loss, argnums=(0, 1))

    def one_microbatch(ex, loss_target):
        ex = dict(ex)
        diff = ex.pop(TWIN_DIFF_INPUT)
        return grad_fn(weights, diff, {**shared, **ex}, loss_target)

    if N_MICROBATCH == 1:
        loss, (grad_w, grad_x) = one_microbatch(per_example, given["loss_target"])
    else:
        def body(carry, xs):
            loss_sum, grad_sum = carry
            l_k, (gw_k, gx_k) = one_microbatch(xs[0], xs[1])
            with _jax.named_scope("update"):
                return (loss_sum + l_k, _jax.tree.map(_jnp.add, grad_sum, gw_k)), gx_k

        init = (_jnp.zeros((), _jnp.float32), _jax.tree.map(_jnp.zeros_like, weights))
        (loss, grad_w), grad_x = _jax.lax.scan(body, init, (per_example, given["loss_target"]))
    with _jax.named_scope("update"):
        delta_w, new_m, new_v = {}, {}, {}
        for n in TWIN_WEIGHTS:
            delta_w[n], new_m[n], new_v[n] = _adamw(weights[n], grad_w[n], given["m_" + n], given["v_" + n])
    return (loss, grad_x, *[grad_w[n] for n in TWIN_WEIGHTS], *[delta_w[n] for n in TWIN_WEIGHTS],
            *[new_m[n] for n in TWIN_WEIGHTS], *[new_v[n] for n in TWIN_WEIGHTS])
```

```python
import functools
import math

import jax
import jax.numpy as jnp
import numpy as np
from jax import lax
from jax.experimental import pallas as pl
from jax.experimental.pallas import tpu as pltpu

F32 = jnp.float32
BF16 = jnp.bfloat16
HI = lax.Precision.HIGHEST
MESH = pl.DeviceIdType.MESH

CHUNK = 64
CONV_K = 4
NORM_EPS = 1e-6
GDN_HEAD_DIM = 128
M2_HEAD_DIM = 64
M2_STATE = 128
M2_GROUPS = 4
S5_GROUP_TILE = 8
ADAM_LR = 0.001
ADAM_B1 = 0.9
ADAM_B2 = 0.999
ADAM_EPS = 1e-08
ADAM_WD = 0.01
ADAM_STEP = 10
LANES = 128
SUBLANES = 8
VMEM_LIMIT_BYTES = 56 * 1024 * 1024

WEIGHT_NAMES = ['norm_w', 'w_in', 'gdn_conv_w', 'gdn_a_log', 'gdn_dt_bias', 'gdn_norm_w', 's5_lam_re', 's5_lam_im',
                's5_log_step', 's5_b_re', 's5_b_im', 's5_c_re', 's5_c_im', 's5_d', 's5_glu_w', 's5_glu_b',
                'm2_conv_w', 'm2_conv_b', 'm2_a_log', 'm2_dt_bias', 'm2_d', 'm2_norm_w', 'proj_a', 'proj_b',
                'proj_c', 'w_out', 'final_norm_w']
BIG = {'w_in': 2, 'proj_a': 2, 'proj_b': 2, 'proj_c': 2, 'w_out': 1, 's5_glu_w': 1}
SHARDED_SMALL = {'gdn_conv_w': 2, 'm2_conv_w': 2}


def _call(body, *, name, out_shape, grid=None, in_specs=None, out_specs=None, scratch_shapes=(), semantics=None,
          num_scalar_prefetch=None):
    params = dict(vmem_limit_bytes=VMEM_LIMIT_BYTES)
    if semantics is not None:
        params['dimension_semantics'] = semantics
    kw = dict(name=name, out_shape=out_shape, compiler_params=pltpu.CompilerParams(**params))
    if num_scalar_prefetch is not None:
        kw['grid_spec'] = pltpu.PrefetchScalarGridSpec(num_scalar_prefetch=num_scalar_prefetch, grid=grid,
                                                       in_specs=in_specs, out_specs=out_specs,
                                                       scratch_shapes=scratch_shapes)
    else:
        if grid is not None:
            kw['grid'] = grid
        if in_specs is not None:
            kw['in_specs'] = in_specs
        if out_specs is not None:
            kw['out_specs'] = out_specs
        if scratch_shapes:
            kw['scratch_shapes'] = scratch_shapes
    return pl.pallas_call(body, **kw)


def _tile(n, target, unit):
    if n <= target:
        return n
    t = (target // unit) * unit
    while t >= unit:
        if n % t == 0:
            return t
        t -= unit
    raise ValueError(f"no tile for {n} (unit {unit}, target {target})")


def _sds(shape, dtype):
    return jax.ShapeDtypeStruct(tuple(shape), dtype)


def _sigmoid(x):
    return jax.nn.sigmoid(x)


def _silu(x):
    return x * jax.nn.sigmoid(x)


def _softplus(x):
    return jnp.maximum(x, 0.0) + jnp.log(1.0 + jnp.exp(-jnp.abs(x)))


def _gelu_tanh(x):
    return 0.5 * x * (1.0 + jnp.tanh(math.sqrt(2.0 / math.pi) * (x + 0.044715 * (x * x * x))))


def _dot(a, b, dims, prec=None):
    return lax.dot_general(a, b, (dims, ((), ())), precision=prec, preferred_element_type=F32)


def _nn(a, b, prec=None):
    return _dot(a, b, ((1,), (0,)), prec)


def _nt(a, b, prec=None):
    return _dot(a, b, ((1,), (1,)), prec)


def _tn(a, b, prec=None):
    return _dot(a, b, ((0,), (0,)), prec)


def _matmul(a, b, mode, out_dtype, name, tm=1024, tn=1024, tk=1024, add=None):
    if mode == 'nn':
        (m, k), (k2, n) = a.shape, b.shape
    elif mode == 'nt':
        (m, k), (n, k2) = a.shape, b.shape
    else:
        (k, m), (k2, n) = a.shape, b.shape
    assert k == k2, (a.shape, b.shape, mode)
    tm = _tile(m, tm, LANES if mode == 'tn' else SUBLANES)
    tn = _tile(n, tn, LANES)
    tk = _tile(k, tk, LANES if mode != 'tn' else SUBLANES * 2)
    nk = k // tk
    dims = {'nn': ((1,), (0,)), 'nt': ((1,), (1,)), 'tn': ((0,), (0,))}[mode]

    def body(*refs):
        a_ref, b_ref = refs[:2]
        o_ref, acc_ref = refs[-2:]
        kk = pl.program_id(2)

        @pl.when(kk == 0)
        def _():
            acc_ref[...] = jnp.zeros_like(acc_ref)

        acc_ref[...] += _dot(a_ref[...].astype(BF16), b_ref[...].astype(BF16), dims)

        @pl.when(kk == nk - 1)
        def _():
            res = acc_ref[...]
            if add is not None:
                res = res + refs[2][...].astype(F32)
            o_ref[...] = res.astype(o_ref.dtype)

    a_spec = pl.BlockSpec((tk, tm), lambda i, j, kk: (kk, i)) if mode == 'tn' else pl.BlockSpec((tm, tk), lambda i, j, kk: (i, kk))
    b_spec = pl.BlockSpec((tn, tk), lambda i, j, kk: (j, kk)) if mode == 'nt' else pl.BlockSpec((tk, tn), lambda i, j, kk: (kk, j))
    o_spec = pl.BlockSpec((tm, tn), lambda i, j, kk: (i, j))
    ops = (a, b) if add is None else (a, b, add)
    return _call(body, name=name, out_shape=_sds((m, n), out_dtype), grid=(m // tm, n // tn, nk),
                 in_specs=[a_spec, b_spec] + ([] if add is None else [o_spec]), out_specs=o_spec,
                 scratch_shapes=[pltpu.VMEM((tm, tn), F32)], semantics=("parallel", "parallel", "arbitrary"))(*ops)


def _matmul_bd(a, b, mode, out_dtype, name, tm=1024):
    if mode in ('nn', 'nt'):
        t = a.shape[0]
        jn, ka, nb = b.shape
        tm = _tile(t, tm, SUBLANES)
        win, wout = (ka, nb) if mode == 'nn' else (nb, ka)
        assert a.shape[1] == jn * win

        def body(a_ref, b_ref, o_ref):
            if mode == 'nn':
                o_ref[...] = _nn(a_ref[...].astype(BF16), b_ref[...].astype(BF16)).astype(o_ref.dtype)
            else:
                o_ref[...] = _nt(a_ref[...].astype(BF16), b_ref[...].astype(BF16)).astype(o_ref.dtype)

        return _call(body, name=name, out_shape=_sds((t, jn * wout), out_dtype), grid=(t // tm, jn),
                     in_specs=[pl.BlockSpec((tm, win), lambda i, j: (i, j)),
                               pl.BlockSpec((None, ka, nb), lambda i, j: (j, 0, 0))],
                     out_specs=pl.BlockSpec((tm, wout), lambda i, j: (i, j)),
                     semantics=("parallel", "parallel"))(a, b)
    t = a.shape[0]
    ka, nb = mode[1], mode[2]
    jn = a.shape[1] // ka
    assert b.shape[1] == jn * nb
    tk = _tile(t, tm, SUBLANES * 2)
    nk = t // tk

    def body_tn(a_ref, b_ref, o_ref):
        @pl.when(pl.program_id(1) == 0)
        def _():
            o_ref[...] = jnp.zeros_like(o_ref)

        o_ref[...] += _tn(a_ref[...].astype(BF16), b_ref[...].astype(BF16))

    return _call(body_tn, name=name, out_shape=_sds((jn, ka, nb), F32), grid=(jn, nk),
                 in_specs=[pl.BlockSpec((tk, ka), lambda j, kk: (kk, j)), pl.BlockSpec((tk, nb), lambda j, kk: (kk, j))],
                 out_specs=pl.BlockSpec((None, ka, nb), lambda j, kk: (j, 0, 0)),
                 semantics=("parallel", "arbitrary"))(a, b)


def _rowwise(fn, rows, params, out_widths, out_dtypes, name, tb=256):
    t = rows[0].shape[0]
    tb = _tile(t, tb, SUBLANES * 2)
    nr, npar = len(rows), len(params)

    def body(*refs):
        ins = [r[...].astype(F32) for r in refs[:nr + npar]]
        outs = fn(*ins)
        for o_ref, o in zip(refs[nr + npar:], outs):
            o_ref[...] = o.astype(o_ref.dtype)

    in_specs = [pl.BlockSpec((tb, r.shape[1]), lambda i: (i, 0)) for r in rows]
    in_specs += [pl.BlockSpec(p.shape, lambda i: (0, 0)) for p in params]
    out_shape = [_sds((t, w), d) for w, d in zip(out_widths, out_dtypes)]
    out_specs = [pl.BlockSpec((tb, w), lambda i: (i, 0)) for w in out_widths]
    return _call(body, name=name, out_shape=out_shape, grid=(t // tb,), in_specs=in_specs, out_specs=out_specs,
                 semantics=("parallel",))(*rows, *params)


def _rowwise_bwd(fn, rows, params, cts, row_grad_dtypes, name, tb=256, addend=None):
    t = rows[0].shape[0]
    tb = _tile(t, tb, SUBLANES * 2)
    nr, npar, nc = len(rows), len(params), len(cts)
    keep = [i for i, d in enumerate(row_grad_dtypes) if d is not None]
    nadd = 0 if addend is None else 1

    def body(*refs):
        ins = [r[...].astype(F32) for r in refs[:nr + npar]]
        ct = [r[...].astype(F32) for r in refs[nr + npar:nr + npar + nc]]
        _, vjp = jax.vjp(fn, *ins)
        grads = vjp(tuple(ct))
        out_refs = refs[nr + npar + nc + nadd:]
        for o_ref, i in zip(out_refs[:len(keep)], keep):
            g = grads[i]
            if nadd and i == 0:
                g = g + refs[nr + npar + nc][...].astype(F32)
            o_ref[...] = g.astype(o_ref.dtype)

        @pl.when(pl.program_id(0) == 0)
        def _():
            for o_ref in out_refs[len(keep):]:
                o_ref[...] = jnp.zeros_like(o_ref)

        for o_ref, g in zip(out_refs[len(keep):], grads[nr:]):
            o_ref[...] += g

    in_specs = [pl.BlockSpec((tb, r.shape[1]), lambda i: (i, 0)) for r in rows]
    in_specs += [pl.BlockSpec(p.shape, lambda i: (0, 0)) for p in params]
    in_specs += [pl.BlockSpec((tb, c.shape[1]), lambda i: (i, 0)) for c in cts]
    extra = []
    if nadd:
        in_specs += [pl.BlockSpec((tb, addend.shape[1]), lambda i: (i, 0))]
        extra = [addend]
    out_shape = [_sds(rows[i].shape, row_grad_dtypes[i]) for i in keep] + [_sds(p.shape, F32) for p in params]
    out_specs = [pl.BlockSpec((tb, rows[i].shape[1]), lambda i_: (i_, 0)) for i in keep]
    out_specs += [pl.BlockSpec(p.shape, lambda i: (0, 0)) for p in params]
    return _call(body, name=name, out_shape=out_shape, grid=(t // tb,), in_specs=in_specs, out_specs=out_specs,
                 semantics=("arbitrary",))(*rows, *params, *cts, *extra)


def _chunk_masks(c):
    row = lax.broadcasted_iota(jnp.int32, (c, c), 0)
    col = lax.broadcasted_iota(jnp.int32, (c, c), 1)
    causal = row >= col
    strict = row > col
    return causal, strict, causal.astype(F32), (row > col).astype(F32), (row == col).astype(F32)


def _lane_pick(blk, idx):
    lane = lax.broadcasted_iota(jnp.int32, blk.shape, 1)
    return jnp.sum(jnp.where(lane == idx, blk, 0.0), axis=1, keepdims=True)


def _gdn_head(q, k, v, z, braw, araw, alog, dtb, nw, s_in):
    c, d = q.shape
    causal, strict, lower, upper_t, eye = _chunk_masks(c)
    qn = q * lax.rsqrt(jnp.sum(q * q, axis=-1, keepdims=True) + NORM_EPS) * (d ** -0.5)
    kn = k * lax.rsqrt(jnp.sum(k * k, axis=-1, keepdims=True) + NORM_EPS)
    beta = _sigmoid(braw)
    g = -jnp.exp(alog) * _softplus(araw + dtb)
    dlog = _nn(lower, g * upper_t, HI)
    dm = jnp.where(causal, jnp.exp(dlog), 0.0)
    gc = _nn(lower, jnp.broadcast_to(g, (c, d)), HI)
    gl = jnp.sum(jnp.broadcast_to(g, (c, d)), axis=0, keepdims=True)
    eg = jnp.exp(gc)
    kb = kn * beta
    a_mat = jnp.where(strict, _nt(kb, kn) * dm, 0.0)
    x = -a_mat
    t_inv = eye + x
    p = x
    for _ in range(int(math.log2(c)) - 1):
        p = _nn(p, p, HI)
        t_inv = t_inv + _nn(t_inv, p, HI)
    r = beta * (v - eg * _nn(kn, s_in))
    v_new = _nn(t_inv, r)
    qk = _nt(qn, kn) * dm
    out = eg * _nn(qn, s_in) + _nn(qk, v_new)
    k_tail = kn * jnp.exp(gl - gc)
    s_out = s_in * jnp.exp(gl) + _tn(k_tail, v_new)
    y = out * lax.rsqrt(jnp.mean(out * out, axis=-1, keepdims=True) + NORM_EPS) * nw * _silu(z)
    return y, s_out


def _gdn_chunks_fwd(sqkv, z, braw, araw, alog, dtb, nw, hb=2):
    t, w3 = sqkv.shape
    w = w3 // 3
    d = GDN_HEAD_DIM
    h = w // d
    hb = min(hb, h)
    hg = h // hb
    c = CHUNK
    nc = t // c

    def body(q_ref, k_ref, v_ref, z_ref, b_ref, a_ref, alog_ref, dtb_ref, nw_ref, y_ref, ssave_ref, s_ref):
        @pl.when(pl.program_id(1) == 0)
        def _():
            s_ref[...] = jnp.zeros_like(s_ref)

        for i in range(hb):
            head = pl.program_id(0) * hb + i
            sl = slice(i * d, (i + 1) * d)
            s_in = s_ref[i]
            ssave_ref[i] = s_in
            y, s_out = _gdn_head(q_ref[:, sl], k_ref[:, sl], v_ref[:, sl], z_ref[:, sl],
                                 _lane_pick(b_ref[...], head), _lane_pick(a_ref[...], head),
                                 _lane_pick(alog_ref[...], head), _lane_pick(dtb_ref[...], head), nw_ref[...], s_in)
            y_ref[:, sl] = y
            s_ref[i] = s_out

    blk = (c, hb * d)
    in_specs = [pl.BlockSpec(blk, lambda g, n: (n, g)), pl.BlockSpec(blk, lambda g, n: (n, hg + g)),
                pl.BlockSpec(blk, lambda g, n: (n, 2 * hg + g)), pl.BlockSpec(blk, lambda g, n: (n, g)),
                pl.BlockSpec((c, h), lambda g, n: (n, 0)), pl.BlockSpec((c, h), lambda g, n: (n, 0)),
                pl.BlockSpec((1, h), lambda g, n: (0, 0)), pl.BlockSpec((1, h), lambda g, n: (0, 0)),
                pl.BlockSpec((1, d), lambda g, n: (0, 0))]
    out_shape = [_sds((t, w), F32), _sds((hg, nc, hb, d, d), F32)]
    out_specs = [pl.BlockSpec(blk, lambda g, n: (n, g)),
                 pl.BlockSpec((None, None, hb, d, d), lambda g, n: (g, n, 0, 0, 0))]
    return _call(body, name="gdn_chunks_fwd", out_shape=out_shape, grid=(hg, nc), in_specs=in_specs,
                 out_specs=out_specs, scratch_shapes=[pltpu.VMEM((hb, d, d), F32)],
                 semantics=("parallel", "arbitrary"))(sqkv, sqkv, sqkv, z, braw, araw, alog, dtb, nw)


def _gdn_chunks_bwd(sqkv, z, braw, araw, alog, dtb, nw, ssave, dy, hb=2):
    t, w3 = sqkv.shape
    w = w3 // 3
    d = GDN_HEAD_DIM
    h = w // d
    hb = min(hb, h)
    hg = h // hb
    c = CHUNK
    nc = t // c

    def body(q_ref, k_ref, v_ref, z_ref, b_ref, a_ref, alog_ref, dtb_ref, nw_ref, ssave_ref, dy_ref,
             dq_ref, dk_ref, dv_ref, dz_ref, db_ref, da_ref, dalog_ref, ddtb_ref, dnw_ref, ds_ref):
        first = jnp.logical_and(pl.program_id(0) == 0, pl.program_id(1) == 0)

        @pl.when(pl.program_id(1) == 0)
        def _():
            ds_ref[...] = jnp.zeros_like(ds_ref)

        @pl.when(first)
        def _():
            dalog_ref[...] = jnp.zeros_like(dalog_ref)
            ddtb_ref[...] = jnp.zeros_like(ddtb_ref)
            dnw_ref[...] = jnp.zeros_like(dnw_ref)

        lane_h = lax.broadcasted_iota(jnp.int32, (1, h), 1)
        db_acc = jnp.zeros((c, h), F32)
        da_acc = jnp.zeros((c, h), F32)
        for i in range(hb):
            head = pl.program_id(0) * hb + i
            sl = slice(i * d, (i + 1) * d)
            args = (q_ref[:, sl], k_ref[:, sl], v_ref[:, sl], z_ref[:, sl],
                    _lane_pick(b_ref[...], head), _lane_pick(a_ref[...], head),
                    _lane_pick(alog_ref[...], head), _lane_pick(dtb_ref[...], head), nw_ref[...], ssave_ref[i])
            _, vjp = jax.vjp(_gdn_head, *args)
            dq, dk, dv, dz, db, da, dalog, ddtb, dnw, ds_in = vjp((dy_ref[:, sl], ds_ref[i]))
            dq_ref[:, sl] = dq
            dk_ref[:, sl] = dk
            dv_ref[:, sl] = dv
            dz_ref[:, sl] = dz.astype(dz_ref.dtype)
            onehot = (lane_h == head).astype(F32)
            db_acc = db_acc + db * onehot
            da_acc = da_acc + da * onehot
            dalog_ref[...] += dalog * onehot
            ddtb_ref[...] += ddtb * onehot
            dnw_ref[...] += dnw
            ds_ref[i] = ds_in
        db_ref[...] = db_acc
        da_ref[...] = da_acc

    blk = (c, hb * d)
    rev = lambda n: nc - 1 - n
    in_specs = [pl.BlockSpec(blk, lambda g, n: (rev(n), g)), pl.BlockSpec(blk, lambda g, n: (rev(n), hg + g)),
                pl.BlockSpec(blk, lambda g, n: (rev(n), 2 * hg + g)), pl.BlockSpec(blk, lambda g, n: (rev(n), g)),
                pl.BlockSpec((c, h), lambda g, n: (rev(n), 0)), pl.BlockSpec((c, h), lambda g, n: (rev(n), 0)),
                pl.BlockSpec((1, h), lambda g, n: (0, 0)), pl.BlockSpec((1, h), lambda g, n: (0, 0)),
                pl.BlockSpec((1, d), lambda g, n: (0, 0)),
                pl.BlockSpec((None, None, hb, d, d), lambda g, n: (g, rev(n), 0, 0, 0)),
                pl.BlockSpec(blk, lambda g, n: (rev(n), g))]
    out_shape = [_sds((t, w), F32), _sds((t, w), F32), _sds((t, w), F32), _sds((t, w), BF16),
                 _sds((hg, t, h), F32), _sds((hg, t, h), F32), _sds((1, h), F32), _sds((1, h), F32), _sds((1, d), F32)]
    out_specs = [pl.BlockSpec(blk, lambda g, n: (rev(n), g))] * 4
    out_specs += [pl.BlockSpec((None, c, h), lambda g, n: (g, rev(n), 0))] * 2
    out_specs += [pl.BlockSpec((1, h), lambda g, n: (0, 0)), pl.BlockSpec((1, h), lambda g, n: (0, 0)),
                  pl.BlockSpec((1, d), lambda g, n: (0, 0))]
    return _call(body, name="gdn_chunks_bwd", out_shape=out_shape, grid=(hg, nc), in_specs=in_specs,
                 out_specs=out_specs, scratch_shapes=[pltpu.VMEM((hb, d, d), F32)],
                 semantics=("arbitrary", "arbitrary"))(sqkv, sqkv, sqkv, z, braw, araw, alog, dtb, nw, ssave, dy)


def _m2_group(xs, z, bm, cm, dtraws, alogs, dtbs, dsks, nw, st):
    c, gw = xs.shape
    rep = len(dtraws)
    causal, _, lower, upper_t, _ = _chunk_masks(c)
    lane_head = lax.broadcasted_iota(jnp.int32, (1, gw), 1) // M2_HEAD_DIM

    def expand(cols):
        res = jnp.broadcast_to(cols[-1], (cols[-1].shape[0], gw))
        for i in reversed(range(rep - 1)):
            res = jnp.where(lane_head == i, cols[i], res)
        return res

    dts = [_softplus(dtraws[i] + dtbs[i]) for i in range(rep)]
    adts = [-jnp.exp(alogs[i]) * dts[i] for i in range(rep)]
    dt_l, adt_l, dsk_l = expand(dts), expand(adts), expand(dsks)
    xdt = xs * dt_l
    acum = _nn(lower, adt_l, HI)
    alast = jnp.sum(adt_l, axis=0, keepdims=True)
    scores = _nt(cm, bm)
    y = jnp.exp(acum) * _nn(cm, st) + dsk_l * xs
    for i in range(rep):
        seg = jnp.where(causal, jnp.exp(_nn(lower, adts[i] * upper_t, HI)), 0.0)
        y = y + _nn(scores * seg, jnp.where(lane_head == i, xdt, 0.0))
    st_out = st * jnp.exp(alast) + _tn(bm, xdt * jnp.exp(alast - acum))
    y2 = y * _silu(z)
    out = y2 * lax.rsqrt(jnp.mean(y2 * y2, axis=-1, keepdims=True) + NORM_EPS) * nw
    return out, st_out


def _m2_dims(sxbc, z):
    t = sxbc.shape[0]
    w2 = z.shape[1]
    g = M2_GROUPS
    n = M2_STATE
    assert sxbc.shape[1] == w2 + 2 * g * n
    gw = w2 // g
    return t, w2, g, n, gw, gw // M2_HEAD_DIM, t // CHUNK


def _m2_args(refs, g, rep, st):
    xs_ref, bm_ref, cm_ref, z_ref, dt_ref, alog_ref, dtb_ref, dsk_ref, nw_ref = refs
    heads = [g * rep + i for i in range(rep)]
    return (xs_ref[...], z_ref[...], bm_ref[...], cm_ref[...],
            [_lane_pick(dt_ref[...], hd) for hd in heads], [_lane_pick(alog_ref[...], hd) for hd in heads],
            [_lane_pick(dtb_ref[...], hd) for hd in heads], [_lane_pick(dsk_ref[...], hd) for hd in heads],
            nw_ref[...], st)


def _m2_chunks_fwd(sxbc, z, dtraw, alog, dtb, dsk, nw):
    t, w2, g, n, gw, rep, nc = _m2_dims(sxbc, z)
    hm = dtraw.shape[1]
    c = CHUNK

    def body(xs_ref, bm_ref, cm_ref, z_ref, dt_ref, alog_ref, dtb_ref, dsk_ref, nw_ref, y_ref, ssave_ref, st_ref):
        @pl.when(pl.program_id(1) == 0)
        def _():
            st_ref[...] = jnp.zeros_like(st_ref)

        st = st_ref[...]
        ssave_ref[...] = st
        y, st_out = _m2_group(*_m2_args((xs_ref, bm_ref, cm_ref, z_ref, dt_ref, alog_ref, dtb_ref, dsk_ref, nw_ref),
                                        pl.program_id(0), rep, st))
        y_ref[...] = y
        st_ref[...] = st_out

    in_specs = [pl.BlockSpec((c, gw), lambda gi, k: (k, gi)),
                pl.BlockSpec((c, n), lambda gi, k: (k, w2 // n + gi)),
                pl.BlockSpec((c, n), lambda gi, k: (k, w2 // n + g + gi)),
                pl.BlockSpec((c, gw), lambda gi, k: (k, gi)),
                pl.BlockSpec((c, hm), lambda gi, k: (k, 0)),
                pl.BlockSpec((1, hm), lambda gi, k: (0, 0)), pl.BlockSpec((1, hm), lambda gi, k: (0, 0)),
                pl.BlockSpec((1, hm), lambda gi, k: (0, 0)), pl.BlockSpec((1, gw), lambda gi, k: (0, gi))]
    out_shape = [_sds((t, w2), F32), _sds((g, nc, n, gw), F32)]
    out_specs = [pl.BlockSpec((c, gw), lambda gi, k: (k, gi)),
                 pl.BlockSpec((None, None, n, gw), lambda gi, k: (gi, k, 0, 0))]
    return _call(body, name="m2_chunks_fwd", out_shape=out_shape, grid=(g, nc), in_specs=in_specs,
                 out_specs=out_specs, scratch_shapes=[pltpu.VMEM((n, gw), F32)],
                 semantics=("parallel", "arbitrary"))(sxbc, sxbc, sxbc, z, dtraw, alog, dtb, dsk, nw)


def _m2_chunks_bwd(sxbc, z, dtraw, alog, dtb, dsk, nw, ssave, dy):
    t, w2, g, n, gw, rep, nc = _m2_dims(sxbc, z)
    hm = dtraw.shape[1]
    c = CHUNK

    def body(xs_ref, bm_ref, cm_ref, z_ref, dt_ref, alog_ref, dtb_ref, dsk_ref, nw_ref, ssave_ref, dy_ref,
             dxs_ref, dbm_ref, dcm_ref, dz_ref, ddt_ref, dalog_ref, ddtb_ref, ddsk_ref, dnw_ref, dst_ref):
        gi = pl.program_id(0)

        @pl.when(pl.program_id(1) == 0)
        def _():
            dst_ref[...] = jnp.zeros_like(dst_ref)
            dnw_ref[...] = jnp.zeros_like(dnw_ref)

        @pl.when(jnp.logical_and(gi == 0, pl.program_id(1) == 0))
        def _():
            dalog_ref[...] = jnp.zeros_like(dalog_ref)
            ddtb_ref[...] = jnp.zeros_like(ddtb_ref)
            ddsk_ref[...] = jnp.zeros_like(ddsk_ref)

        args = _m2_args((xs_ref, bm_ref, cm_ref, z_ref, dt_ref, alog_ref, dtb_ref, dsk_ref, nw_ref), gi, rep,
                        ssave_ref[...])
        _, vjp = jax.vjp(_m2_group, *args)
        dxs, dz, dbm, dcm, ddts, dalogs, ddtbs, ddsks, dnw, dst = vjp((dy_ref[...], dst_ref[...]))
        dxs_ref[...] = dxs
        dbm_ref[...] = dbm
        dcm_ref[...] = dcm
        dz_ref[...] = dz.astype(dz_ref.dtype)
        dnw_ref[...] += dnw
        dst_ref[...] = dst
        lane_h = lax.broadcasted_iota(jnp.int32, (1, hm), 1)
        ddt = jnp.zeros((c, hm), F32)
        for i in range(rep):
            onehot = (lane_h == gi * rep + i).astype(F32)
            ddt = ddt + ddts[i] * onehot
            dalog_ref[...] += dalogs[i] * onehot
            ddtb_ref[...] += ddtbs[i] * onehot
            ddsk_ref[...] += ddsks[i] * onehot
        ddt_ref[...] = ddt

    rev = lambda k: nc - 1 - k
    in_specs = [pl.BlockSpec((c, gw), lambda gi, k: (rev(k), gi)),
                pl.BlockSpec((c, n), lambda gi, k: (rev(k), w2 // n + gi)),
                pl.BlockSpec((c, n), lambda gi, k: (rev(k), w2 // n + g + gi)),
                pl.BlockSpec((c, gw), lambda gi, k: (rev(k), gi)),
                pl.BlockSpec((c, hm), lambda gi, k: (rev(k), 0)),
                pl.BlockSpec((1, hm), lambda gi, k: (0, 0)), pl.BlockSpec((1, hm), lambda gi, k: (0, 0)),
                pl.BlockSpec((1, hm), lambda gi, k: (0, 0)), pl.BlockSpec((1, gw), lambda gi, k: (0, gi)),
                pl.BlockSpec((None, None, n, gw), lambda gi, k: (gi, rev(k), 0, 0)),
                pl.BlockSpec((c, gw), lambda gi, k: (rev(k), gi))]
    out_shape = [_sds((t, w2), F32), _sds((t, g * n), F32), _sds((t, g * n), F32), _sds((t, w2), BF16),
                 _sds((g, t, hm), F32), _sds((1, hm), F32), _sds((1, hm), F32), _sds((1, hm), F32),
                 _sds((1, w2), F32)]
    out_specs = [pl.BlockSpec((c, gw), lambda gi, k: (rev(k), gi)),
                 pl.BlockSpec((c, n), lambda gi, k: (rev(k), gi)), pl.BlockSpec((c, n), lambda gi, k: (rev(k), gi)),
                 pl.BlockSpec((c, gw), lambda gi, k: (rev(k), gi)),
                 pl.BlockSpec((None, c, hm), lambda gi, k: (gi, rev(k), 0)),
                 pl.BlockSpec((1, hm), lambda gi, k: (0, 0)), pl.BlockSpec((1, hm), lambda gi, k: (0, 0)),
                 pl.BlockSpec((1, hm), lambda gi, k: (0, 0)), pl.BlockSpec((1, gw), lambda gi, k: (0, gi))]
    return _call(body, name="m2_chunks_bwd", out_shape=out_shape, grid=(g, nc), in_specs=in_specs,
                 out_specs=out_specs, scratch_shapes=[pltpu.VMEM((n, gw), F32)],
                 semantics=("arbitrary", "arbitrary"))(sxbc, sxbc, sxbc, z, dtraw, alog, dtb, dsk, nw, ssave, dy)


def _s5_scan(bu, a_l, cw, reverse, name, tb=256):
    t, wtot = bu.shape
    jn = wtot // cw
    half = cw // 2
    tb = _tile(t, tb, SUBLANES)
    nb = t // tb

    def body(bu_ref, a_ref, s_ref, st_ref):
        @pl.when(pl.program_id(1) == 0)
        def _():
            st_ref[...] = jnp.zeros_like(st_ref)

        are = a_ref[:, :half]
        aim = -a_ref[:, half:] if reverse else a_ref[:, half:]

        def step(k, carry):
            sre, sim = carry
            r = tb - 1 - k if reverse else k
            nre = are * sre - aim * sim + bu_ref[pl.ds(r, 1), pl.ds(0, half)]
            nim = are * sim + aim * sre + bu_ref[pl.ds(r, 1), pl.ds(half, half)]
            s_ref[pl.ds(r, 1), pl.ds(0, half)] = nre
            s_ref[pl.ds(r, 1), pl.ds(half, half)] = nim
            return nre, nim

        sre, sim = lax.fori_loop(0, tb, step, (st_ref[:, :half], st_ref[:, half:]), unroll=8)
        st_ref[:, :half] = sre
        st_ref[:, half:] = sim

    rb = (lambda i: nb - 1 - i) if reverse else (lambda i: i)
    return _call(body, name=name, out_shape=_sds((t, wtot), F32), grid=(jn, nb),
                 in_specs=[pl.BlockSpec((tb, cw), lambda j, i: (rb(i), j)), pl.BlockSpec((1, cw), lambda j, i: (0, j))],
                 out_specs=pl.BlockSpec((tb, cw), lambda j, i: (rb(i), j)),
                 scratch_shapes=[pltpu.VMEM((1, cw), F32)], semantics=("parallel", "arbitrary"))(bu, a_l)


def _s5_da(ds, s, cw, tb=256):
    t, wtot = ds.shape
    jn = wtot // cw
    half = cw // 2
    tb = _tile(t, tb, SUBLANES)
    nb = t // tb
    hb = tb // SUBLANES

    def body(ds_ref, s_ref, halo_ref, o_ref):
        i = pl.program_id(1)

        @pl.when(i == 0)
        def _():
            o_ref[...] = jnp.zeros_like(o_ref)

        prev = jnp.where(i == 0, 0.0, halo_ref[SUBLANES - 1:SUBLANES, :])
        row = lax.broadcasted_iota(jnp.int32, (tb, cw), 0)
        sh = jnp.where(row == 0, prev, pltpu.roll(s_ref[...], 1, 0))
        d = ds_ref[...]
        dre, dim, sre, sim = d[:, :half], d[:, half:], sh[:, :half], sh[:, half:]
        o_ref[:, :half] += jnp.sum(dre * sre + dim * sim, axis=0, keepdims=True)
        o_ref[:, half:] += jnp.sum(dim * sre - dre * sim, axis=0, keepdims=True)

    return _call(body, name="s5_da", out_shape=_sds((1, wtot), F32), grid=(jn, nb),
                 in_specs=[pl.BlockSpec((tb, cw), lambda j, i: (i, j)), pl.BlockSpec((tb, cw), lambda j, i: (i, j)),
                           pl.BlockSpec((SUBLANES, cw), lambda j, i: (jnp.maximum(i * hb - 1, 0), j))],
                 out_specs=pl.BlockSpec((1, cw), lambda j, i: (0, j)),
                 semantics=("parallel", "arbitrary"))(ds, s, s)


def _conv_rows(t, tb):
    tb = _tile(t, tb, SUBLANES * 2)
    return tb, t // tb, tb // SUBLANES


def _conv_pre(x_ref, halo_ref, w_ref, b_ref, xe_ref, tb):
    i = pl.program_id(0)
    xe_ref[0:SUBLANES, :] = jnp.where(i == 0, 0.0, halo_ref[...])
    xe_ref[SUBLANES:, :] = x_ref[...]
    c = b_ref[...] + w_ref[0:1, :] * xe_ref[pl.ds(SUBLANES - CONV_K + 1, tb), :]
    for k in range(1, CONV_K):
        c = c + w_ref[k:k + 1, :] * xe_ref[pl.ds(SUBLANES - CONV_K + 1 + k, tb), :]
    return c


def _conv_fwd(x, w, b, name, tb=256):
    t, cwid = x.shape
    tb, nb, hb = _conv_rows(t, tb)

    def body(x_ref, halo_ref, w_ref, b_ref, o_ref, xe_ref):
        o_ref[...] = _silu(_conv_pre(x_ref, halo_ref, w_ref, b_ref, xe_ref, tb))

    return _call(body, name=name, out_shape=_sds((t, cwid), F32), grid=(nb,),
                 in_specs=[pl.BlockSpec((tb, cwid), lambda i: (i, 0)),
                           pl.BlockSpec((SUBLANES, cwid), lambda i: (jnp.maximum(i * hb - 1, 0), 0)),
                           pl.BlockSpec((CONV_K, cwid), lambda i: (0, 0)), pl.BlockSpec((1, cwid), lambda i: (0, 0))],
                 out_specs=pl.BlockSpec((tb, cwid), lambda i: (i, 0)),
                 scratch_shapes=[pltpu.VMEM((tb + SUBLANES, cwid), F32)], semantics=("arbitrary",))(x, x, w, b)


def _conv_bwd_act(x, w, b, ds, name, tb=256):
    t, cwid = x.shape
    tb, nb, hb = _conv_rows(t, tb)

    def body(x_ref, halo_ref, w_ref, b_ref, ds_ref, o_ref, xe_ref):
        c = _conv_pre(x_ref, halo_ref, w_ref, b_ref, xe_ref, tb)
        sg = _sigmoid(c)
        o_ref[...] = ds_ref[...] * (sg * (1.0 + c * (1.0 - sg)))

    return _call(body, name=name, out_shape=_sds((t, cwid), F32), grid=(nb,),
                 in_specs=[pl.BlockSpec((tb, cwid), lambda i: (i, 0)),
                           pl.BlockSpec((SUBLANES, cwid), lambda i: (jnp.maximum(i * hb - 1, 0), 0)),
                           pl.BlockSpec((CONV_K, cwid), lambda i: (0, 0)), pl.BlockSpec((1, cwid), lambda i: (0, 0)),
                           pl.BlockSpec((tb, cwid), lambda i: (i, 0))],
                 out_specs=pl.BlockSpec((tb, cwid), lambda i: (i, 0)),
                 scratch_shapes=[pltpu.VMEM((tb + SUBLANES, cwid), F32)], semantics=("arbitrary",))(x, x, w, b, ds)


def _conv_bwd_taps(dc, x, w, name, tb=256):
    t, cwid = x.shape
    tb, nb, hb = _conv_rows(t, tb)

    def body(dc_ref, nxt_ref, x_ref, halo_ref, w_ref, dx_ref, dw_ref, db_ref, dce_ref, xe_ref):
        i = pl.program_id(0)

        @pl.when(i == 0)
        def _():
            dw_ref[...] = jnp.zeros_like(dw_ref)
            db_ref[...] = jnp.zeros_like(db_ref)

        dc = dc_ref[...]
        dce_ref[0:tb, :] = dc
        dce_ref[tb:, :] = jnp.where(i == nb - 1, 0.0, nxt_ref[...])
        xe_ref[0:SUBLANES, :] = jnp.where(i == 0, 0.0, halo_ref[...])
        xe_ref[SUBLANES:, :] = x_ref[...]
        dx = w_ref[CONV_K - 1:CONV_K, :] * dc
        for k in range(CONV_K - 1):
            dx = dx + w_ref[k:k + 1, :] * dce_ref[pl.ds(CONV_K - 1 - k, tb), :]
        dx_ref[...] = dx.astype(dx_ref.dtype)
        for k in range(CONV_K):
            dw_ref[k:k + 1, :] += jnp.sum(dc * xe_ref[pl.ds(SUBLANES - CONV_K + 1 + k, tb), :], axis=0, keepdims=True)
        db_ref[...] += jnp.sum(dc, axis=0, keepdims=True)

    return _call(body, name=name, out_shape=[_sds((t, cwid), BF16), _sds((CONV_K, cwid), F32), _sds((1, cwid), F32)],
                 grid=(nb,),
                 in_specs=[pl.BlockSpec((tb, cwid), lambda i: (i, 0)),
                           pl.BlockSpec((SUBLANES, cwid), lambda i: (jnp.minimum((i + 1) * hb, nb * hb - 1), 0)),
                           pl.BlockSpec((tb, cwid), lambda i: (i, 0)),
                           pl.BlockSpec((SUBLANES, cwid), lambda i: (jnp.maximum(i * hb - 1, 0), 0)),
                           pl.BlockSpec((CONV_K, cwid), lambda i: (0, 0))],
                 out_specs=[pl.BlockSpec((tb, cwid), lambda i: (i, 0)), pl.BlockSpec((CONV_K, cwid), lambda i: (0, 0)),
                            pl.BlockSpec((1, cwid), lambda i: (0, 0))],
                 scratch_shapes=[pltpu.VMEM((tb + SUBLANES, cwid), F32), pltpu.VMEM((tb + SUBLANES, cwid), F32)],
                 semantics=("arbitrary",))(dc, dc, x, x, w)


def _f_rms(x, w):
    return (x * lax.rsqrt(jnp.mean(x * x, axis=-1, keepdims=True) + NORM_EPS) * w,)


def _f_s5_post1(ymm, u, d_l):
    return (_gelu_tanh(ymm + d_l * u),)


def _f_s5_post2(yg, tt, gate, b):
    return (yg * _sigmoid(tt + b) * _silu(gate),)


def _f_merge(ma, mb, mc, pa, pb, pc):
    return (_sigmoid(ma) * pa + _sigmoid(mb) * pb + _sigmoid(mc) * pc,)


def _loss_and_grad(x, tgt, fw, tb=256):
    t, dm = x.shape
    tb = _tile(t, tb, SUBLANES * 2)

    def f(xb, wb, tb_):
        y = _f_rms(xb, wb)[0]
        e = y - tb_
        return 0.5 * jnp.sum(jnp.mean(e * e, axis=-1, keepdims=True), axis=0, keepdims=True)

    def body(x_ref, t_ref, w_ref, loss_ref, dx_ref, dw_ref):
        @pl.when(pl.program_id(0) == 0)
        def _():
            loss_ref[...] = jnp.zeros_like(loss_ref)
            dw_ref[...] = jnp.zeros_like(dw_ref)

        tgt_b = t_ref[...]
        val, vjp = jax.vjp(lambda a, b: f(a, b, tgt_b), x_ref[...], w_ref[...])
        dxb, dwb = vjp(jnp.ones((1, 1), F32))
        loss_ref[...] += val
        dx_ref[...] = dxb
        dw_ref[...] += dwb

    return _call(body, name="loss_and_grad", out_shape=[_sds((1, 1), F32), _sds((t, dm), F32), _sds((1, dm), F32)],
                 grid=(t // tb,),
                 in_specs=[pl.BlockSpec((tb, dm), lambda i: (i, 0)), pl.BlockSpec((tb, dm), lambda i: (i, 0)),
                           pl.BlockSpec((1, dm), lambda i: (0, 0))],
                 out_specs=[pl.BlockSpec((1, 1), lambda i: (0, 0)), pl.BlockSpec((tb, dm), lambda i: (i, 0)),
                            pl.BlockSpec((1, dm), lambda i: (0, 0))],
                 semantics=("arbitrary",))(x, tgt, fw)


FLAT_W = 1024


def _sum_parts(parts, name, tb=256):
    n, r, wd = parts.shape
    tb = _tile(r, tb, SUBLANES)

    def body(p_ref, o_ref):
        acc = p_ref[0]
        for k in range(1, n):
            acc = acc + p_ref[k]
        o_ref[...] = acc

    return _call(body, name=name, out_shape=_sds((r, wd), F32), grid=(r // tb,),
                 in_specs=[pl.BlockSpec((n, tb, wd), lambda i: (0, i, 0))],
                 out_specs=pl.BlockSpec((tb, wd), lambda i: (i, 0)), semantics=("parallel",))(parts)


def _add_my_half(g2, recv, c_idx, tb=256):
    _, p, r, wd = g2.shape
    tb = _tile(r, tb, SUBLANES)

    def body(c_ref, g_ref, r_ref, o_ref):
        o_ref[...] = g_ref[...] + r_ref[...]

    return _call(body, name="add_my_half", out_shape=_sds((p, r, wd), F32), grid=(p, r // tb),
                 in_specs=[pl.BlockSpec((None, None, tb, wd), lambda j, i, c_ref: (c_ref[0], j, i, 0)),
                           pl.BlockSpec((None, tb, wd), lambda j, i, c_ref: (j, i, 0))],
                 out_specs=pl.BlockSpec((None, tb, wd), lambda j, i, c_ref: (j, i, 0)),
                 semantics=("parallel", "parallel"), num_scalar_prefetch=1)(c_idx, g2, recv)


def _adamw(w, g, m, v, name, tb=256):
    r, wd = w.shape
    tb = _tile(r, tb, SUBLANES)

    def body(w_ref, g_ref, m_ref, v_ref, d_ref, nm_ref, nv_ref):
        gg = g_ref[...]
        nm = ADAM_B1 * m_ref[...] + (1.0 - ADAM_B1) * gg
        nv = ADAM_B2 * v_ref[...] + (1.0 - ADAM_B2) * (gg * gg)
        m_hat = nm / (1.0 - ADAM_B1 ** ADAM_STEP)
        v_hat = nv / (1.0 - ADAM_B2 ** ADAM_STEP)
        d_ref[...] = -ADAM_LR * (m_hat / (jnp.sqrt(v_hat) + ADAM_EPS) + ADAM_WD * w_ref[...])
        nm_ref[...] = nm
        nv_ref[...] = nv

    spec = pl.BlockSpec((tb, wd), lambda i: (i, 0))
    return _call(body, name=name, out_shape=[_sds((r, wd), F32)] * 3, grid=(r // tb,), in_specs=[spec] * 4,
                 out_specs=[spec] * 3, semantics=("parallel",))(w, g, m, v)


def _here():
    return lax.axis_index("x"), lax.axis_index("y"), lax.axis_index("c")


def _comm_call(body, name, out_shape, n_sems, operands):
    anyspec = pl.BlockSpec(memory_space=pl.ANY)
    outs = out_shape if isinstance(out_shape, (list, tuple)) else [out_shape]
    return _call(body, name=name, out_shape=out_shape, in_specs=[anyspec] * len(operands),
                 out_specs=[anyspec] * len(outs) if isinstance(out_shape, (list, tuple)) else anyspec,
                 scratch_shapes=[pltpu.SemaphoreType.DMA((n_sems,)), pltpu.SemaphoreType.DMA((n_sems,)),
                                 pltpu.SemaphoreType.DMA(())])(*operands)


def _gather_chips(x, name):
    def body(x_ref, o_ref, send_sems, recv_sems, local_sem):
        xi, yi, ci = _here()
        chips = [(1 - xi, yi), (xi, 1 - yi), (1 - xi, 1 - yi)]
        mine = pltpu.make_async_copy(x_ref, o_ref.at[2 * xi + yi], local_sem)
        mine.start()

        def copy(k, slot, to):
            return pltpu.make_async_remote_copy(src_ref=x_ref, dst_ref=o_ref.at[slot], send_sem=send_sems.at[k],
                                                recv_sem=recv_sems.at[k], device_id=to, device_id_type=MESH)

        sends = [copy(k, 2 * xi + yi, (px, py, ci)) for k, (px, py) in enumerate(chips)]
        for cp in sends:
            cp.start()
        for k, (px, py) in enumerate(chips):
            copy(k, 2 * px + py, (px, py, ci)).wait_recv()
        for cp in sends:
            cp.wait_send()
        mine.wait()

    return _comm_call(body, name, _sds((4,) + x.shape, x.dtype), 3, (x,))


def _gather_all(x, name):
    def body(x_ref, o_ref, send_sems, recv_sems, local_sem):
        xi, yi, ci = _here()
        me = 4 * xi + 2 * yi + ci
        flips = [(fx, fy, fc) for fx in (0, 1) for fy in (0, 1) for fc in (0, 1)][1:]
        peers = [((1 - xi) if fx else xi, (1 - yi) if fy else yi, (1 - ci) if fc else ci) for fx, fy, fc in flips]
        mine = pltpu.make_async_copy(x_ref, o_ref.at[me], local_sem)
        mine.start()

        def copy(k, slot, to):
            return pltpu.make_async_remote_copy(src_ref=x_ref, dst_ref=o_ref.at[slot], send_sem=send_sems.at[k],
                                                recv_sem=recv_sems.at[k], device_id=to, device_id_type=MESH)

        sends = [copy(k, me, p) for k, p in enumerate(peers)]
        for cp in sends:
            cp.start()
        for k, (px, py, pc) in enumerate(peers):
            copy(k, 4 * px + 2 * py + pc, (px, py, pc)).wait_recv()
        for cp in sends:
            cp.wait_send()
        mine.wait()

    return _comm_call(body, name, _sds((8,) + x.shape, x.dtype), 7, (x,))


def _swap_sibling_half(g2):
    def body(g_ref, o_ref, send_sems, recv_sems, local_sem):
        xi, yi, ci = _here()
        cp = pltpu.make_async_remote_copy(src_ref=g_ref.at[1 - ci], dst_ref=o_ref, send_sem=send_sems.at[0],
                                          recv_sem=recv_sems.at[0], device_id=(xi, yi, 1 - ci), device_id_type=MESH)
        cp.start()
        cp.wait()

    return _comm_call(body, "swap_sibling_half", _sds(g2.shape[1:], g2.dtype), 1, (g2,))


def _scatter_chips(gp):
    def body(g_ref, o_ref, send_sems, recv_sems, local_sem):
        xi, yi, ci = _here()
        me = 2 * xi + yi
        chips = [(1 - xi, yi), (xi, 1 - yi), (1 - xi, 1 - yi)]
        mine = pltpu.make_async_copy(g_ref.at[me], o_ref.at[me], local_sem)
        mine.start()

        def copy(k, src_slot, dst_slot, to):
            return pltpu.make_async_remote_copy(src_ref=g_ref.at[src_slot], dst_ref=o_ref.at[dst_slot],
                                                send_sem=send_sems.at[k], recv_sem=recv_sems.at[k], device_id=to,
                                                device_id_type=MESH)

        sends = [copy(k, 2 * px + py, me, (px, py, ci)) for k, (px, py) in enumerate(chips)]
        for cp in sends:
            cp.start()
        for k, (px, py) in enumerate(chips):
            copy(k, me, 2 * px + py, (px, py, ci)).wait_recv()
        for cp in sends:
            cp.wait_send()
        mine.wait()

    return _comm_call(body, "scatter_chips", _sds(gp.shape, gp.dtype), 3, (gp,))


def _share_sibling(r):
    def body(r_ref, o_ref, send_sems, recv_sems, local_sem):
        xi, yi, ci = _here()
        mine = pltpu.make_async_copy(r_ref, o_ref.at[ci], local_sem)
        mine.start()
        cp = pltpu.make_async_remote_copy(src_ref=r_ref, dst_ref=o_ref.at[ci], send_sem=send_sems.at[0],
                                          recv_sem=recv_sems.at[0], device_id=(xi, yi, 1 - ci), device_id_type=MESH)
        cp.start()
        pltpu.make_async_remote_copy(src_ref=r_ref, dst_ref=o_ref.at[1 - ci], send_sem=send_sems.at[0],
                                     recv_sem=recv_sems.at[0], device_id=(xi, yi, 1 - ci),
                                     device_id_type=MESH).wait_recv()
        cp.wait_send()
        mine.wait()

    return _comm_call(body, "share_sibling", _sds((2,) + r.shape, r.dtype), 1, (r,))


def _flat_pack(arrs, dtype, row_mult):
    flat = jnp.concatenate([a.astype(dtype).reshape(-1) for a in arrs])
    unit = FLAT_W * row_mult
    npad = -(-flat.shape[0] // unit) * unit
    return jnp.pad(flat, (0, npad - flat.shape[0])).reshape(npad // FLAT_W, FLAT_W)


def _flat_unpack(flat2d, shapes):
    flat = flat2d.reshape(-1)
    outs, off = [], 0
    for s in shapes:
        size = int(np.prod(s))
        outs.append(flat[off:off + size].reshape(s))
        off += size
    return outs


def _split_cols(a, widths):
    outs, off = [], 0
    for wd in widths:
        outs.append(lax.slice_in_dim(a, off, off + wd, axis=1))
        off += wd
    return outs


def _s5_params(lam_re, lam_im, log_step, b_re, b_im, c_re, c_im, d_skip):
    g, p = lam_re.shape
    hs = b_re.shape[2]
    gt = S5_GROUP_TILE
    jn = g // gt
    lam_re = jnp.minimum(lam_re, -1e-4)
    step = jnp.exp(log_step)[:, None]
    mag = jnp.exp(lam_re * step)
    ab_re = mag * jnp.cos(lam_im * step)
    ab_im = mag * jnp.sin(lam_im * step)
    den = lam_re * lam_re + lam_im * lam_im
    f_re = ((ab_re - 1.0) * lam_re + ab_im * lam_im) / den
    f_im = (ab_im * lam_re - (ab_re - 1.0) * lam_im) / den
    bb_re = f_re[..., None] * b_re - f_im[..., None] * b_im
    bb_im = f_re[..., None] * b_im + f_im[..., None] * b_re
    a_l = jnp.concatenate([ab_re.reshape(jn, gt * p), ab_im.reshape(jn, gt * p)], axis=1).reshape(1, jn * 2 * gt * p)
    eye = jnp.eye(gt, dtype=F32)

    def blockdiag(m):
        return jnp.einsum('jahp,ab->jahbp', m.reshape(jn, gt, hs, p), eye).reshape(jn, gt * hs, gt * p)

    b_blk = jnp.concatenate([blockdiag(bb_re.transpose(0, 2, 1)), blockdiag(bb_im.transpose(0, 2, 1))], axis=2)
    c_blk = jnp.concatenate([blockdiag(c_re), blockdiag(-c_im)], axis=2)
    return a_l, b_blk, c_blk, d_skip.reshape(1, g * hs)


def _layer_dims(p):
    d_model = p['w_out'].shape[1]
    wa = p['proj_a'].shape[0]
    h = p['gdn_a_log'].shape[0]
    wb = p['proj_b'].shape[0]
    wc = p['proj_c'].shape[0]
    hm = p['m2_a_log'].shape[0]
    cdim = p['m2_conv_w'].shape[1]
    splits = (3 * wa, wa, h, h, wb, wb, wc, cdim, hm, d_model, d_model, d_model)
    n_in = sum(splits)
    return splits, n_in, -(-n_in // LANES) * LANES


def _layer_fwd(x, p):
    splits, n_in, n_pad = _layer_dims(p)
    sv = {'x': x}
    h = _rowwise(_f_rms, [x], [p['norm_w'][None]], [x.shape[1]], [BF16], "rms_fwd")[0]
    w_in = jnp.pad(p['w_in'], ((0, 0), (0, n_pad - n_in)))
    proj = _matmul(h, w_in, 'nn', F32, "in_proj", tn=1152)
    qkv, az, braw, araw, su, sgate, cz, cxbc, cdt, ma, mb, mc = _split_cols(proj, splits)
    sv.update(h=h, w_in=w_in, qkv=qkv, az=az, braw=braw, araw=araw, su=su, sgate=sgate, cz=cz, cxbc=cxbc, cdt=cdt,
              ma=ma, mb=mb, mc=mc)
    gb0 = jnp.zeros((1, qkv.shape[1]), F32)
    sqkv = _conv_fwd(qkv, p['gdn_conv_w'], gb0, "gdn_conv_fwd")
    ya, ssa = _gdn_chunks_fwd(sqkv, az, braw, araw, p['gdn_a_log'][None], p['gdn_dt_bias'][None], p['gdn_norm_w'][None])
    sv.update(sqkv=sqkv, ssa=ssa, ya=ya)
    s5_in = tuple(p[k] for k in ('s5_lam_re', 's5_lam_im', 's5_log_step', 's5_b_re', 's5_b_im', 's5_c_re', 's5_c_im',
                                 's5_d'))
    (a_l, b_blk, c_blk, d_l), s5_vjp = jax.vjp(_s5_params, *s5_in)
    cw = b_blk.shape[2]
    bu = _matmul_bd(su, b_blk, 'nn', F32, "s5_bu")
    s = _s5_scan(bu, a_l, cw, False, "s5_scan_fwd")
    ymm = _matmul_bd(s, c_blk, 'nt', F32, "s5_out")
    yg = _rowwise(_f_s5_post1, [ymm, su], [d_l], [su.shape[1]], [F32], "s5_post1_fwd")[0]
    tt = _matmul(yg, p['s5_glu_w'], 'nn', F32, "s5_glu")
    yb = _rowwise(_f_s5_post2, [yg, tt, sgate], [p['s5_glu_b'][None]], [su.shape[1]], [F32], "s5_post2_fwd")[0]
    sv.update(a_l=a_l, b_blk=b_blk, c_blk=c_blk, d_l=d_l, s5_vjp=s5_vjp, s=s, ymm=ymm, yg=yg, tt=tt, yb=yb, cw=cw)
    sxbc = _conv_fwd(cxbc, p['m2_conv_w'], p['m2_conv_b'][None], "m2_conv_fwd")
    yc, ssc = _m2_chunks_fwd(sxbc, cz, cdt, p['m2_a_log'][None], p['m2_dt_bias'][None], p['m2_d'][None],
                             p['m2_norm_w'][None])
    sv.update(sxbc=sxbc, ssc=ssc, yc=yc)
    pa = _matmul(ya, p['proj_a'], 'nn', F32, "proj_a")
    pb = _matmul(yb, p['proj_b'], 'nn', F32, "proj_b")
    pc = _matmul(yc, p['proj_c'], 'nn', F32, "proj_c")
    merged = _rowwise(_f_merge, [ma, mb, mc, pa, pb, pc], [], [x.shape[1]], [BF16], "merge_fwd", tb=128)[0]
    x_next = _matmul(merged, p['w_out'], 'nn', F32, "out_proj", add=x)
    sv.update(pa=pa, pb=pb, pc=pc, merged=merged)
    return x_next, sv


def _layer_bwd(dx_out, p, sv):
    splits, n_in, n_pad = _layer_dims(p)
    g = {}
    dmerged = _matmul(dx_out, p['w_out'], 'nt', F32, "out_proj_dx")
    g['w_out'] = _matmul(sv['merged'], dx_out, 'tn', F32, "out_proj_dw")
    dma, dmb, dmc, dpa, dpb, dpc = _rowwise_bwd(
        _f_merge, [sv['ma'], sv['mb'], sv['mc'], sv['pa'], sv['pb'], sv['pc']], [], [dmerged], [BF16] * 6,
        "merge_bwd", tb=128)
    dya = _matmul(dpa, p['proj_a'], 'nt', F32, "proj_a_dx")
    dyb = _matmul(dpb, p['proj_b'], 'nt', F32, "proj_b_dx")
    dyc = _matmul(dpc, p['proj_c'], 'nt', F32, "proj_c_dx")
    g['proj_a'] = _matmul(sv['ya'], dpa, 'tn', F32, "proj_a_dw")
    g['proj_b'] = _matmul(sv['yb'], dpb, 'tn', F32, "proj_b_dw")
    g['proj_c'] = _matmul(sv['yc'], dpc, 'tn', F32, "proj_c_dw")
    alog, dtb, gnw = p['gdn_a_log'][None], p['gdn_dt_bias'][None], p['gdn_norm_w'][None]
    dq, dk, dv, daz, db3, da3, dalog, ddtb, dgnw = _gdn_chunks_bwd(sv['sqkv'], sv['az'], sv['braw'], sv['araw'], alog,
                                                                   dtb, gnw, sv['ssa'], dya)
    gb0 = jnp.zeros((1, sv['qkv'].shape[1]), F32)
    dcq = _conv_bwd_act(sv['qkv'], p['gdn_conv_w'], gb0, jnp.concatenate([dq, dk, dv], axis=1), "gdn_conv_bwd_act")
    dqkv, g['gdn_conv_w'], _ = _conv_bwd_taps(dcq, sv['qkv'], p['gdn_conv_w'], "gdn_conv_bwd_taps")
    dbraw, daraw = jnp.sum(db3, axis=0), jnp.sum(da3, axis=0)
    g.update(gdn_a_log=dalog[0], gdn_dt_bias=ddtb[0], gdn_norm_w=dgnw[0])
    dxs, dbm, dcm, dcz, ddt3, dmalog, dmdtb, dmdsk, dmnw = _m2_chunks_bwd(
        sv['sxbc'], sv['cz'], sv['cdt'], p['m2_a_log'][None], p['m2_dt_bias'][None], p['m2_d'][None],
        p['m2_norm_w'][None], sv['ssc'], dyc)
    dcx = _conv_bwd_act(sv['cxbc'], p['m2_conv_w'], p['m2_conv_b'][None], jnp.concatenate([dxs, dbm, dcm], axis=1),
                        "m2_conv_bwd_act")
    dcxbc, g['m2_conv_w'], dconvb = _conv_bwd_taps(dcx, sv['cxbc'], p['m2_conv_w'], "m2_conv_bwd_taps")
    dcdt = jnp.sum(ddt3, axis=0)
    g.update(m2_conv_b=dconvb[0], m2_a_log=dmalog[0], m2_dt_bias=dmdtb[0], m2_d=dmdsk[0], m2_norm_w=dmnw[0])
    dyg1, dtt, dsgate, dglub = _rowwise_bwd(_f_s5_post2, [sv['yg'], sv['tt'], sv['sgate']], [p['s5_glu_b'][None]],
                                            [dyb], [F32, BF16, BF16], "s5_post2_bwd")
    dyg = _matmul(dtt, p['s5_glu_w'], 'nt', F32, "s5_glu_dx", add=dyg1)
    g['s5_glu_w'] = _matmul(sv['yg'], dtt, 'tn', F32, "s5_glu_dw")
    g['s5_glu_b'] = dglub[0]
    dymm, dsu1, dd_l = _rowwise_bwd(_f_s5_post1, [sv['ymm'], sv['su']], [sv['d_l']], [dyg], [BF16, F32],
                                    "s5_post1_bwd")
    gy = _matmul_bd(dymm, sv['c_blk'], 'nn', F32, "s5_out_dx")
    ds = _s5_scan(gy, sv['a_l'], sv['cw'], True, "s5_scan_bwd")
    da_l = _s5_da(ds, sv['s'], sv['cw'])
    dsu = (dsu1 + _matmul_bd(ds, sv['b_blk'], 'nt', F32, "s5_bu_dx")).astype(BF16)
    ka = sv['b_blk'].shape[1]
    db_blk = _matmul_bd(sv['su'], ds, ('tn', ka, sv['cw']), F32, "s5_bu_dw")
    dc_blk = _matmul_bd(dymm, sv['s'], ('tn', ka, sv['cw']), F32, "s5_out_dw")
    for k, v in zip(('s5_lam_re', 's5_lam_im', 's5_log_step', 's5_b_re', 's5_b_im', 's5_c_re', 's5_c_im', 's5_d'),
                    sv['s5_vjp']((da_l, db_blk, dc_blk, dd_l))):
        g[k] = v
    dproj = jnp.concatenate([dqkv, daz, dbraw.astype(BF16), daraw.astype(BF16), dsu, dsgate, dcz, dcxbc,
                             dcdt.astype(BF16), dma, dmb, dmc, jnp.zeros((dqkv.shape[0], n_pad - n_in), BF16)], axis=1)
    dh = _matmul(dproj, sv['w_in'], 'nt', F32, "in_proj_dx", tk=1152)
    g['w_in'] = lax.slice_in_dim(_matmul(sv['h'], dproj, 'tn', F32, "in_proj_dw", tn=1152), 0, n_in, axis=1)
    dx, dnw = _rowwise_bwd(_f_rms, [sv['x']], [p['norm_w'][None]], [dh], [F32], "rms_bwd", addend=dx_out)
    g['norm_w'] = dnw[0]
    return dx, g


INPUT_NAMES = (['x'] + WEIGHT_NAMES + ['loss_target'] + ['m_' + n for n in WEIGHT_NAMES]
               + ['v_' + n for n in WEIGHT_NAMES])


def _step(d):
    xi, yi, ci = _here()
    me = 2 * xi + yi
    depth = d['norm_w'].shape[0]
    big, ssm = list(BIG), list(SHARDED_SMALL)
    wg = _gather_chips(_flat_pack([d[n] for n in big], BF16, 32), "gather_weights")
    cg = _gather_chips(_flat_pack([d[n] for n in ssm], F32, 8), "gather_conv_weights")
    full = {}
    for names, gathered, axes in ((big, wg, BIG), (ssm, cg, SHARDED_SMALL)):
        parts = [_flat_unpack(gathered[j], [d[n].shape for n in names]) for j in range(4)]
        for i, n in enumerate(names):
            full[n] = jnp.concatenate([parts[j][i] for j in range(4)], axis=axes[n])
    layer_names = [n for n in WEIGHT_NAMES if n != 'final_norm_w']

    def layer_params(l):
        return {n: (full[n][l] if n in full else d[n][l]) for n in layer_names}

    x = d['x'][0]
    saved = []
    for l in range(depth):
        x, sv = _layer_fwd(x, layer_params(l))
        saved.append(sv)
    loss11, dx, dfw = _loss_and_grad(x, d['loss_target'][0], d['final_norm_w'][None])
    loss = lax.psum(loss11[0, 0], ("x", "y", "c"))
    grads = [None] * depth
    for l in reversed(range(depth)):
        dx, grads[l] = _layer_bwd(dx, layer_params(l), saved[l])
    gfull = {n: jnp.stack([grads[l][n] for l in range(depth)]) for n in layer_names}
    gfull['final_norm_w'] = dfw[0]
    nsh = 4

    def shard(a, axis, j):
        wd = a.shape[axis] // nsh
        return lax.slice_in_dim(a, j * wd, (j + 1) * wd, axis=axis)

    gpk = jnp.stack([_flat_pack([shard(gfull[n], BIG[n], j) for n in big], F32, 32) for j in range(nsh)])
    rows = gpk.shape[1]
    g2 = gpk.reshape(nsh, 2, rows // 2, FLAT_W).transpose(1, 0, 2, 3)
    pair = _add_my_half(g2, _swap_sibling_half(g2), jnp.reshape(ci, (1,)).astype(jnp.int32))
    mine = _sum_parts(_scatter_chips(pair), "sum_chips")
    gbig = _share_sibling(mine).reshape(rows, FLAT_W)
    wpk, mpk, vpk = (_flat_pack([d[pre + n] for n in big], F32, 32) for pre in ('', 'm_', 'v_'))
    dl, nm, nv = _adamw(wpk, gbig, mpk, vpk, "adamw_large")
    shapes = [d[n].shape for n in big]
    out = {}
    for key, arr in (('grad_', gbig), ('delta_', dl), ('new_m_', nm), ('new_v_', nv)):
        for n, a in zip(big, _flat_unpack(arr, shapes)):
            out[key + n] = a
    small = [n for n in WEIGHT_NAMES if n not in BIG]
    sshapes = [gfull[n].shape for n in small]
    gsm = _sum_parts(_gather_all(_flat_pack([gfull[n] for n in small], F32, 8), "gather_small_grads"), "sum_devices")
    gs = dict(zip(small, _flat_unpack(gsm, sshapes)))
    for n in ssm:
        wd = d[n].shape[SHARDED_SMALL[n]]
        gs[n] = lax.dynamic_slice_in_dim(gs[n], me * wd, wd, axis=SHARDED_SMALL[n])
    lshapes = [d[n].shape for n in small]
    wps, gps, mps, vps = (_flat_pack(arrs, F32, 8) for arrs in (
        [d[n] for n in small], [gs[n] for n in small], [d['m_' + n] for n in small], [d['v_' + n] for n in small]))
    dl, nm, nv = _adamw(wps, gps, mps, vps, "adamw_small")
    for key, arr in (('grad_', gps), ('delta_', dl), ('new_m_', nm), ('new_v_', nv)):
        for n, a in zip(small, _flat_unpack(arr, lshapes)):
            out[key + n] = a
    res = [loss, dx[None]]
    for key in ('grad_', 'delta_', 'new_m_', 'new_v_'):
        res += [out[key + n] for n in WEIGHT_NAMES]
    return tuple(res)


def kernel(x, norm_w, w_in, gdn_conv_w, gdn_a_log, gdn_dt_bias, gdn_norm_w, s5_lam_re, s5_lam_im, s5_log_step, s5_b_re, s5_b_im, s5_c_re, s5_c_im, s5_d, s5_glu_w, s5_glu_b, m2_conv_w, m2_conv_b, m2_a_log, m2_dt_bias, m2_d, m2_norm_w, proj_a, proj_b, proj_c, w_out, final_norm_w, loss_target, m_norm_w, m_w_in, m_gdn_conv_w, m_gdn_a_log, m_gdn_dt_bias, m_gdn_norm_w, m_s5_lam_re, m_s5_lam_im, m_s5_log_step, m_s5_b_re, m_s5_b_im, m_s5_c_re, m_s5_c_im, m_s5_d, m_s5_glu_w, m_s5_glu_b, m_m2_conv_w, m_m2_conv_b, m_m2_a_log, m_m2_dt_bias, m_m2_d, m_m2_norm_w, m_proj_a, m_proj_b, m_proj_c, m_w_out, m_final_norm_w, v_norm_w, v_w_in, v_gdn_conv_w, v_gdn_a_log, v_gdn_dt_bias, v_gdn_norm_w, v_s5_lam_re, v_s5_lam_im, v_s5_log_step, v_s5_b_re, v_s5_b_im, v_s5_c_re, v_s5_c_im, v_s5_d, v_s5_glu_w, v_s5_glu_b, v_m2_conv_w, v_m2_conv_b, v_m2_a_log, v_m2_dt_bias, v_m2_d, v_m2_norm_w, v_proj_a, v_proj_b, v_proj_c, v_w_out, v_final_norm_w):
    args = (x, norm_w, w_in, gdn_conv_w, gdn_a_log, gdn_dt_bias, gdn_norm_w, s5_lam_re, s5_lam_im, s5_log_step, s5_b_re, s5_b_im, s5_c_re, s5_c_im, s5_d, s5_glu_w, s5_glu_b, m2_conv_w, m2_conv_b, m2_a_log, m2_dt_bias, m2_d, m2_norm_w, proj_a, proj_b, proj_c, w_out, final_norm_w, loss_target, m_norm_w, m_w_in, m_gdn_conv_w, m_gdn_a_log, m_gdn_dt_bias, m_gdn_norm_w, m_s5_lam_re, m_s5_lam_im, m_s5_log_step, m_s5_b_re, m_s5_b_im, m_s5_c_re, m_s5_c_im, m_s5_d, m_s5_glu_w, m_s5_glu_b, m_m2_conv_w, m_m2_conv_b, m_m2_a_log, m_m2_dt_bias, m_m2_d, m_m2_norm_w, m_proj_a, m_proj_b, m_proj_c, m_w_out, m_final_norm_w, v_norm_w, v_w_in, v_gdn_conv_w, v_gdn_a_log, v_gdn_dt_bias, v_gdn_norm_w, v_s5_lam_re, v_s5_lam_im, v_s5_log_step, v_s5_b_re, v_s5_b_im, v_s5_c_re, v_s5_c_im, v_s5_d, v_s5_glu_w, v_s5_glu_b, v_m2_conv_w, v_m2_conv_b, v_m2_a_log, v_m2_dt_bias, v_m2_d, v_m2_norm_w, v_proj_a, v_proj_b, v_proj_c, v_w_out, v_final_norm_w)
    return _step(dict(zip(INPUT_NAMES, args)))
```

```python
import functools
import math

import jax
import jax.numpy as jnp
import numpy as np
from jax import lax
from jax.experimental import pallas as pl
from jax.experimental.pallas import tpu as pltpu

F32 = jnp.float32
BF16 = jnp.bfloat16
HI = lax.Precision.HIGHEST
MESH = pl.DeviceIdType.MESH

CHUNK = 64
CONV_K = 4
NORM_EPS = 1e-6
GDN_HEAD_DIM = 128
M2_HEAD_DIM = 64
M2_STATE = 128
M2_GROUPS = 4
S5_GROUP_TILE = 8
ADAM_LR = 0.001
ADAM_B1 = 0.9
ADAM_B2 = 0.999
ADAM_EPS = 1e-08
ADAM_WD = 0.01
ADAM_STEP = 10
LANES = 128
SUBLANES = 8
VMEM_LIMIT_BYTES = 56 * 1024 * 1024

WEIGHT_NAMES = ['norm_w', 'w_in', 'gdn_conv_w', 'gdn_a_log', 'gdn_dt_bias', 'gdn_norm_w', 's5_lam_re', 's5_lam_im',
                's5_log_step', 's5_b_re', 's5_b_im', 's5_c_re', 's5_c_im', 's5_d', 's5_glu_w', 's5_glu_b',
                'm2_conv_w', 'm2_conv_b', 'm2_a_log', 'm2_dt_bias', 'm2_d', 'm2_norm_w', 'proj_a', 'proj_b',
                'proj_c', 'w_out', 'final_norm_w']
BIG = {'w_in': 2, 'proj_a': 2, 'proj_b': 2, 'proj_c': 2, 'w_out': 1, 's5_glu_w': 1}
SHARDED_SMALL = {'gdn_conv_w': 2, 'm2_conv_w': 2}


def _call(body, *, name, out_shape, grid=None, in_specs=None, out_specs=None, scratch_shapes=(), semantics=None,
          num_scalar_prefetch=None):
    params = dict(vmem_limit_bytes=VMEM_LIMIT_BYTES)
    if semantics is not None:
        params['dimension_semantics'] = semantics
    kw = dict(name=name, out_shape=out_shape, compiler_params=pltpu.CompilerParams(**params))
    if num_scalar_prefetch is not None:
        kw['grid_spec'] = pltpu.PrefetchScalarGridSpec(num_scalar_prefetch=num_scalar_prefetch, grid=grid,
                                                       in_specs=in_specs, out_specs=out_specs,
                                                       scratch_shapes=scratch_shapes)
    else:
        if grid is not None:
            kw['grid'] = grid
        if in_specs is not None:
            kw['in_specs'] = in_specs
        if out_specs is not None:
            kw['out_specs'] = out_specs
        if scratch_shapes:
            kw['scratch_shapes'] = scratch_shapes
    return pl.pallas_call(body, **kw)


def _tile(n, target, unit):
    if n <= target:
        return n
    t = (target // unit) * unit
    while t >= unit:
        if n % t == 0:
            return t
        t -= unit
    raise ValueError(f"no tile for {n} (unit {unit}, target {target})")


def _sds(shape, dtype):
    return jax.ShapeDtypeStruct(tuple(shape), dtype)


def _sigmoid(x):
    return jax.nn.sigmoid(x)


def _silu(x):
    return x * jax.nn.sigmoid(x)


def _softplus(x):
    return jnp.maximum(x, 0.0) + jnp.log(1.0 + jnp.exp(-jnp.abs(x)))


def _gelu_tanh(x):
    return 0.5 * x * (1.0 + jnp.tanh(math.sqrt(2.0 / math.pi) * (x + 0.044715 * (x * x * x))))


def _dot(a, b, dims, prec=None):
    return lax.dot_general(a, b, (dims, ((), ())), precision=prec, preferred_element_type=F32)


def _nn(a, b, prec=None):
    return _dot(a, b, ((1,), (0,)), prec)


def _nt(a, b, prec=None):
    return _dot(a, b, ((1,), (1,)), prec)


def _tn(a, b, prec=None):
    return _dot(a, b, ((0,), (0,)), prec)


def _matmul(a, b, mode, out_dtype, name, tm=1024, tn=1024, tk=1024, add=None):
    if mode == 'nn':
        (m, k), (k2, n) = a.shape, b.shape
    elif mode == 'nt':
        (m, k), (n, k2) = a.shape, b.shape
    else:
        (k, m), (k2, n) = a.shape, b.shape
    assert k == k2, (a.shape, b.shape, mode)
    tm = _tile(m, tm, LANES if mode == 'tn' else SUBLANES)
    tn = _tile(n, tn, LANES)
    tk = _tile(k, tk, LANES if mode != 'tn' else SUBLANES * 2)
    nk = k // tk
    dims = {'nn': ((1,), (0,)), 'nt': ((1,), (1,)), 'tn': ((0,), (0,))}[mode]

    def body(*refs):
        a_ref, b_ref = refs[:2]
        o_ref, acc_ref = refs[-2:]
        kk = pl.program_id(2)

        @pl.when(kk == 0)
        def _():
            acc_ref[...] = jnp.zeros_like(acc_ref)

        acc_ref[...] += _dot(a_ref[...].astype(BF16), b_ref[...].astype(BF16), dims)

        @pl.when(kk == nk - 1)
        def _():
            res = acc_ref[...]
            if add is not None:
                res = res + refs[2][...].astype(F32)
            o_ref[...] = res.astype(o_ref.dtype)

    a_spec = pl.BlockSpec((tk, tm), lambda i, j, kk: (kk, i)) if mode == 'tn' else pl.BlockSpec((tm, tk), lambda i, j, kk: (i, kk))
    b_spec = pl.BlockSpec((tn, tk), lambda i, j, kk: (j, kk)) if mode == 'nt' else pl.BlockSpec((tk, tn), lambda i, j, kk: (kk, j))
    o_spec = pl.BlockSpec((tm, tn), lambda i, j, kk: (i, j))
    ops = (a, b) if add is None else (a, b, add)
    return _call(body, name=name, out_shape=_sds((m, n), out_dtype), grid=(m // tm, n // tn, nk),
                 in_specs=[a_spec, b_spec] + ([] if add is None else [o_spec]), out_specs=o_spec,
                 scratch_shapes=[pltpu.VMEM((tm, tn), F32)], semantics=("parallel", "parallel", "arbitrary"))(*ops)


def _matmul_bd(a, b, mode, out_dtype, name, tm=1024):
    if mode in ('nn', 'nt'):
        t = a.shape[0]
        jn, ka, nb = b.shape
        tm = _tile(t, tm, SUBLANES)
        win, wout = (ka, nb) if mode == 'nn' else (nb, ka)
        assert a.shape[1] == jn * win

        def body(a_ref, b_ref, o_ref):
            if mode == 'nn':
                o_ref[...] = _nn(a_ref[...].astype(BF16), b_ref[...].astype(BF16)).astype(o_ref.dtype)
            else:
                o_ref[...] = _nt(a_ref[...].astype(BF16), b_ref[...].astype(BF16)).astype(o_ref.dtype)

        return _call(body, name=name, out_shape=_sds((t, jn * wout), out_dtype), grid=(t // tm, jn),
                     in_specs=[pl.BlockSpec((tm, win), lambda i, j: (i, j)),
                               pl.BlockSpec((None, ka, nb), lambda i, j: (j, 0, 0))],
                     out_specs=pl.BlockSpec((tm, wout), lambda i, j: (i, j)),
                     semantics=("parallel", "parallel"))(a, b)
    t = a.shape[0]
    ka, nb = mode[1], mode[2]
    jn = a.shape[1] // ka
    assert b.shape[1] == jn * nb
    tk = _tile(t, tm, SUBLANES * 2)
    nk = t // tk

    def body_tn(a_ref, b_ref, o_ref):
        @pl.when(pl.program_id(1) == 0)
        def _():
            o_ref[...] = jnp.zeros_like(o_ref)

        o_ref[...] += _tn(a_ref[...].astype(BF16), b_ref[...].astype(BF16))

    return _call(body_tn, name=name, out_shape=_sds((jn, ka, nb), F32), grid=(jn, nk),
                 in_specs=[pl.BlockSpec((tk, ka), lambda j, kk: (kk, j)), pl.BlockSpec((tk, nb), lambda j, kk: (kk, j))],
                 out_specs=pl.BlockSpec((None, ka, nb), lambda j, kk: (j, 0, 0)),
                 semantics=("parallel", "arbitrary"))(a, b)


def _rowwise(fn, rows, params, out_widths, out_dtypes, name, tb=256):
    t = rows[0].shape[0]
    tb = _tile(t, tb, SUBLANES * 2)
    nr, npar = len(rows), len(params)

    def body(*refs):
        ins = [r[...].astype(F32) for r in refs[:nr + npar]]
        outs = fn(*ins)
        for o_ref, o in zip(refs[nr + npar:], outs):
            o_ref[...] = o.astype(o_ref.dtype)

    in_specs = [pl.BlockSpec((tb, r.shape[1]), lambda i: (i, 0)) for r in rows]
    in_specs += [pl.BlockSpec(p.shape, lambda i: (0, 0)) for p in params]
    out_shape = [_sds((t, w), d) for w, d in zip(out_widths, out_dtypes)]
    out_specs = [pl.BlockSpec((tb, w), lambda i: (i, 0)) for w in out_widths]
    return _call(body, name=name, out_shape=out_shape, grid=(t // tb,), in_specs=in_specs, out_specs=out_specs,
                 semantics=("parallel",))(*rows, *params)


def _rowwise_bwd(fn, rows, params, cts, row_grad_dtypes, name, tb=256, addend=None):
    t = rows[0].shape[0]
    tb = _tile(t, tb, SUBLANES * 2)
    nr, npar, nc = len(rows), len(params), len(cts)
    keep = [i for i, d in enumerate(row_grad_dtypes) if d is not None]
    nadd = 0 if addend is None else 1

    def body(*refs):
        ins = [r[...].astype(F32) for r in refs[:nr + npar]]
        ct = [r[...].astype(F32) for r in refs[nr + npar:nr + npar + nc]]
        _, vjp = jax.vjp(fn, *ins)
        grads = vjp(tuple(ct))
        out_refs = refs[nr + npar + nc + nadd:]
        for o_ref, i in zip(out_refs[:len(keep)], keep):
            g = grads[i]
            if nadd and i == 0:
                g = g + refs[nr + npar + nc][...].astype(F32)
            o_ref[...] = g.astype(o_ref.dtype)

        @pl.when(pl.program_id(0) == 0)
        def _():
            for o_ref in out_refs[len(keep):]:
                o_ref[...] = jnp.zeros_like(o_ref)

        for o_ref, g in zip(out_refs[len(keep):], grads[nr:]):
            o_ref[...] += g

    in_specs = [pl.BlockSpec((tb, r.shape[1]), lambda i: (i, 0)) for r in rows]
    in_specs += [pl.BlockSpec(p.shape, lambda i: (0, 0)) for p in params]
    in_specs += [pl.BlockSpec((tb, c.shape[1]), lambda i: (i, 0)) for c in cts]
    extra = []
    if nadd:
        in_specs += [pl.BlockSpec((tb, addend.shape[1]), lambda i: (i, 0))]
        extra = [addend]
    out_shape = [_sds(rows[i].shape, row_grad_dtypes[i]) for i in keep] + [_sds(p.shape, F32) for p in params]
    out_specs = [pl.BlockSpec((tb, rows[i].shape[1]), lambda i_: (i_, 0)) for i in keep]
    out_specs += [pl.BlockSpec(p.shape, lambda i: (0, 0)) for p in params]
    return _call(body, name=name, out_shape=out_shape, grid=(t // tb,), in_specs=in_specs, out_specs=out_specs,
                 semantics=("arbitrary",))(*rows, *params, *cts, *extra)


def _chunk_masks(c):
    row = lax.broadcasted_iota(jnp.int32, (c, c), 0)
    col = lax.broadcasted_iota(jnp.int32, (c, c), 1)
    causal = row >= col
    strict = row > col
    return causal, strict, causal.astype(F32), (row > col).astype(F32), (row == col).astype(F32)


def _lane_pick(blk, idx):
    lane = lax.broadcasted_iota(jnp.int32, blk.shape, 1)
    return jnp.sum(jnp.where(lane == idx, blk, 0.0), axis=1, keepdims=True)


def _bdot(a, b, ca, cb, prec=None):
    return lax.dot_general(a, b, (((ca,), (cb,)), ((0,), (0,))), precision=prec, preferred_element_type=F32)


def _bnn(a, b, prec=None):
    return _bdot(a, b, 2, 1, prec)


def _bnt(a, b, prec=None):
    return _bdot(a, b, 2, 2, prec)


def _btn(a, b, prec=None):
    return _bdot(a, b, 1, 1, prec)


def _unit_lower_inverse(a_mat, eye):
    x = -a_mat
    t_inv = eye + x
    p = x
    for _ in range(int(math.log2(a_mat.shape[-1])) - 1):
        p = _bnn(p, p, HI)
        t_inv = t_inv + _bnn(t_inv, p, HI)
    return t_inv


@jax.custom_vjp
def _saved_inverse(a_mat, t_saved):
    return t_saved


def _saved_inverse_fwd(a_mat, t_saved):
    return t_saved, t_saved


def _saved_inverse_bwd(t_inv, ct):
    return -_bnt(_btn(t_inv, ct, HI), t_inv, HI), jnp.zeros_like(t_inv)


_saved_inverse.defvjp(_saved_inverse_fwd, _saved_inverse_bwd)


def _gdn_heads(q, k, v, z, braw, araw, alog, dtb, nw, s_in, t_saved=None):
    b, c, d = q.shape
    causal, strict, lower, upper_t, eye = _chunk_masks(c)
    lower_b = jnp.broadcast_to(lower[None], (b, c, c))
    qn = q * lax.rsqrt(jnp.sum(q * q, axis=-1, keepdims=True) + NORM_EPS) * (d ** -0.5)
    kn = k * lax.rsqrt(jnp.sum(k * k, axis=-1, keepdims=True) + NORM_EPS)
    beta = _sigmoid(braw)
    g = -jnp.exp(alog) * _softplus(araw + dtb)
    dlog = _bnn(lower_b, g * upper_t[None], HI)
    dm = jnp.where(causal[None], jnp.exp(dlog), 0.0)
    g_lanes = jnp.broadcast_to(g, (b, c, d))
    gc = _bnn(lower_b, g_lanes, HI)
    gl = jnp.sum(g_lanes, axis=1, keepdims=True)
    eg = jnp.exp(gc)
    kb = kn * beta
    a_mat = jnp.where(strict[None], _bnt(kb, kn) * dm, 0.0)
    t_inv = _unit_lower_inverse(a_mat, eye[None]) if t_saved is None else _saved_inverse(a_mat, t_saved)
    r = beta * (v - eg * _bnn(kn, s_in))
    v_new = _bnn(t_inv, r)
    qk = _bnt(qn, kn) * dm
    out = eg * _bnn(qn, s_in) + _bnn(qk, v_new)
    k_tail = kn * jnp.exp(gl - gc)
    s_out = s_in * jnp.exp(gl) + _btn(k_tail, v_new)
    y = out * lax.rsqrt(jnp.mean(out * out, axis=-1, keepdims=True) + NORM_EPS) * nw[None] * _silu(z)
    if t_saved is None:
        return y, s_out, t_inv
    return y, s_out


def _gdn_stack(refs, hb, d, h, first_head):
    q_ref, k_ref, v_ref, z_ref, b_ref, a_ref, alog_ref, dtb_ref = refs
    sls = [slice(i * d, (i + 1) * d) for i in range(hb)]
    heads = [first_head + i for i in range(hb)]
    wide = [jnp.stack([r[:, sl] for sl in sls]) for r in (q_ref, k_ref, v_ref, z_ref)]
    cols = [jnp.stack([_lane_pick(r[...], hd) for hd in heads]) for r in (b_ref, a_ref, alog_ref, dtb_ref)]
    return wide + cols


GDN_HEADS_PER_STEP = 8


def _gdn_chunks_fwd(sqkv, z, braw, araw, alog, dtb, nw, hb=GDN_HEADS_PER_STEP):
    t, w3 = sqkv.shape
    w = w3 // 3
    d = GDN_HEAD_DIM
    h = w // d
    hb = min(hb, h)
    hg = h // hb
    c = CHUNK
    nc = t // c

    def body(q_ref, k_ref, v_ref, z_ref, b_ref, a_ref, alog_ref, dtb_ref, nw_ref, y_ref, ssave_ref, tsave_ref,
             s_ref):
        @pl.when(pl.program_id(1) == 0)
        def _():
            s_ref[...] = jnp.zeros_like(s_ref)

        s_in = s_ref[...]
        ssave_ref[...] = s_in
        args = _gdn_stack((q_ref, k_ref, v_ref, z_ref, b_ref, a_ref, alog_ref, dtb_ref), hb, d, h,
                          pl.program_id(0) * hb)
        y, s_out, t_inv = _gdn_heads(*args, nw_ref[...], s_in)
        for i in range(hb):
            y_ref[:, i * d:(i + 1) * d] = y[i]
        s_ref[...] = s_out
        tsave_ref[...] = t_inv

    blk = (c, hb * d)
    in_specs = [pl.BlockSpec(blk, lambda g, n: (n, g)), pl.BlockSpec(blk, lambda g, n: (n, hg + g)),
                pl.BlockSpec(blk, lambda g, n: (n, 2 * hg + g)), pl.BlockSpec(blk, lambda g, n: (n, g)),
                pl.BlockSpec((c, h), lambda g, n: (n, 0)), pl.BlockSpec((c, h), lambda g, n: (n, 0)),
                pl.BlockSpec((1, h), lambda g, n: (0, 0)), pl.BlockSpec((1, h), lambda g, n: (0, 0)),
                pl.BlockSpec((1, d), lambda g, n: (0, 0))]
    out_shape = [_sds((t, w), F32), _sds((hg, nc, hb, d, d), F32), _sds((hg, nc, hb, c, c), F32)]
    out_specs = [pl.BlockSpec(blk, lambda g, n: (n, g)),
                 pl.BlockSpec((None, None, hb, d, d), lambda g, n: (g, n, 0, 0, 0)),
                 pl.BlockSpec((None, None, hb, c, c), lambda g, n: (g, n, 0, 0, 0))]
    return _call(body, name="gdn_chunks_fwd", out_shape=out_shape, grid=(hg, nc), in_specs=in_specs,
                 out_specs=out_specs, scratch_shapes=[pltpu.VMEM((hb, d, d), F32)],
                 semantics=("parallel", "arbitrary"))(sqkv, sqkv, sqkv, z, braw, araw, alog, dtb, nw)


def _gdn_chunks_bwd(sqkv, z, braw, araw, alog, dtb, nw, ssave, tsave, dy, hb=GDN_HEADS_PER_STEP):
    t, w3 = sqkv.shape
    w = w3 // 3
    d = GDN_HEAD_DIM
    h = w // d
    hb = min(hb, h)
    hg = h // hb
    c = CHUNK
    nc = t // c

    def body(q_ref, k_ref, v_ref, z_ref, b_ref, a_ref, alog_ref, dtb_ref, nw_ref, ssave_ref, tsave_ref, dy_ref,
             dq_ref, dk_ref, dv_ref, dz_ref, db_ref, da_ref, dalog_ref, ddtb_ref, dnw_ref, ds_ref):
        first = jnp.logical_and(pl.program_id(0) == 0, pl.program_id(1) == 0)

        @pl.when(pl.program_id(1) == 0)
        def _():
            ds_ref[...] = jnp.zeros_like(ds_ref)

        @pl.when(first)
        def _():
            dalog_ref[...] = jnp.zeros_like(dalog_ref)
            ddtb_ref[...] = jnp.zeros_like(ddtb_ref)
            dnw_ref[...] = jnp.zeros_like(dnw_ref)

        lane_h = lax.broadcasted_iota(jnp.int32, (1, h), 1)
        db_acc = jnp.zeros((c, h), F32)
        da_acc = jnp.zeros((c, h), F32)
        args = _gdn_stack((q_ref, k_ref, v_ref, z_ref, b_ref, a_ref, alog_ref, dtb_ref), hb, d, h,
                          pl.program_id(0) * hb)
        t_saved = tsave_ref[...]
        _, vjp = jax.vjp(lambda *a: _gdn_heads(*a, t_saved=t_saved), *args, nw_ref[...], ssave_ref[...])
        dyb = jnp.stack([dy_ref[:, i * d:(i + 1) * d] for i in range(hb)])
        dq, dk, dv, dz, db, da, dalog, ddtb, dnw, ds_in = vjp((dyb, ds_ref[...]))
        for i in range(hb):
            sl = slice(i * d, (i + 1) * d)
            dq_ref[:, sl] = dq[i]
            dk_ref[:, sl] = dk[i]
            dv_ref[:, sl] = dv[i]
            dz_ref[:, sl] = dz[i].astype(dz_ref.dtype)
            onehot = (lane_h == pl.program_id(0) * hb + i).astype(F32)
            db_acc = db_acc + db[i] * onehot
            da_acc = da_acc + da[i] * onehot
            dalog_ref[...] += dalog[i] * onehot
            ddtb_ref[...] += ddtb[i] * onehot
        dnw_ref[...] += dnw
        ds_ref[...] = ds_in
        db_ref[...] = db_acc
        da_ref[...] = da_acc

    blk = (c, hb * d)
    rev = lambda n: nc - 1 - n
    in_specs = [pl.BlockSpec(blk, lambda g, n: (rev(n), g)), pl.BlockSpec(blk, lambda g, n: (rev(n), hg + g)),
                pl.BlockSpec(blk, lambda g, n: (rev(n), 2 * hg + g)), pl.BlockSpec(blk, lambda g, n: (rev(n), g)),
                pl.BlockSpec((c, h), lambda g, n: (rev(n), 0)), pl.BlockSpec((c, h), lambda g, n: (rev(n), 0)),
                pl.BlockSpec((1, h), lambda g, n: (0, 0)), pl.BlockSpec((1, h), lambda g, n: (0, 0)),
                pl.BlockSpec((1, d), lambda g, n: (0, 0)),
                pl.BlockSpec((None, None, hb, d, d), lambda g, n: (g, rev(n), 0, 0, 0)),
                pl.BlockSpec((None, None, hb, c, c), lambda g, n: (g, rev(n), 0, 0, 0)),
                pl.BlockSpec(blk, lambda g, n: (rev(n), g))]
    out_shape = [_sds((t, w), F32), _sds((t, w), F32), _sds((t, w), F32), _sds((t, w), BF16),
                 _sds((hg, t, h), F32), _sds((hg, t, h), F32), _sds((1, h), F32), _sds((1, h), F32), _sds((1, d), F32)]
    out_specs = [pl.BlockSpec(blk, lambda g, n: (rev(n), g))] * 4
    out_specs += [pl.BlockSpec((None, c, h), lambda g, n: (g, rev(n), 0))] * 2
    out_specs += [pl.BlockSpec((1, h), lambda g, n: (0, 0)), pl.BlockSpec((1, h), lambda g, n: (0, 0)),
                  pl.BlockSpec((1, d), lambda g, n: (0, 0))]
    return _call(body, name="gdn_chunks_bwd", out_shape=out_shape, grid=(hg, nc), in_specs=in_specs,
                 out_specs=out_specs, scratch_shapes=[pltpu.VMEM((hb, d, d), F32)],
                 semantics=("arbitrary", "arbitrary"))(sqkv, sqkv, sqkv, z, braw, araw, alog, dtb, nw, ssave, tsave,
                                                       dy)


def _m2_groups(xs, z, bm, cm, dtraws, alogs, dtbs, dsks, nw, st):
    g, c, gw = xs.shape
    rep = len(dtraws)
    causal, _, lower, upper_t, _ = _chunk_masks(c)
    lower_b = jnp.broadcast_to(lower[None], (g, c, c))
    lane_head = lax.broadcasted_iota(jnp.int32, (1, 1, gw), 2) // M2_HEAD_DIM

    def expand(cols):
        res = jnp.broadcast_to(cols[-1], (g, cols[-1].shape[1], gw))
        for i in reversed(range(rep - 1)):
            res = jnp.where(lane_head == i, cols[i], res)
        return res

    dts = [_softplus(dtraws[i] + dtbs[i]) for i in range(rep)]
    adts = [-jnp.exp(alogs[i]) * dts[i] for i in range(rep)]
    dt_l, adt_l, dsk_l = expand(dts), expand(adts), expand(dsks)
    xdt = xs * dt_l
    acum = _bnn(lower_b, adt_l, HI)
    alast = jnp.sum(adt_l, axis=1, keepdims=True)
    scores = _bnt(cm, bm)
    y = jnp.exp(acum) * _bnn(cm, st) + dsk_l * xs
    for i in range(rep):
        seg = jnp.where(causal[None], jnp.exp(_bnn(lower_b, adts[i] * upper_t[None], HI)), 0.0)
        y = y + _bnn(scores * seg, jnp.where(lane_head == i, xdt, 0.0))
    st_out = st * jnp.exp(alast) + _btn(bm, xdt * jnp.exp(alast - acum))
    y2 = y * _silu(z)
    out = y2 * lax.rsqrt(jnp.mean(y2 * y2, axis=-1, keepdims=True) + NORM_EPS) * nw
    return out, st_out


def _m2_dims(sxbc, z):
    t = sxbc.shape[0]
    w2 = z.shape[1]
    g = M2_GROUPS
    n = M2_STATE
    assert sxbc.shape[1] == w2 + 2 * g * n
    gw = w2 // g
    return t, w2, g, n, gw, gw // M2_HEAD_DIM, t // CHUNK


def _m2_args(refs, g, n, gw, rep, st):
    sx_ref, z_ref, dt_ref, alog_ref, dtb_ref, dsk_ref, nw_ref = refs
    w2 = g * gw
    xs = jnp.stack([sx_ref[:, i * gw:(i + 1) * gw] for i in range(g)])
    z = jnp.stack([z_ref[:, i * gw:(i + 1) * gw] for i in range(g)])
    bm = jnp.stack([sx_ref[:, w2 + i * n:w2 + (i + 1) * n] for i in range(g)])
    cm = jnp.stack([sx_ref[:, w2 + (g + i) * n:w2 + (g + i + 1) * n] for i in range(g)])
    nw = jnp.stack([nw_ref[:, i * gw:(i + 1) * gw] for i in range(g)])

    def cols(ref):
        blk = ref[...]
        return [jnp.stack([_lane_pick(blk, gi * rep + i) for gi in range(g)]) for i in range(rep)]

    return xs, z, bm, cm, cols(dt_ref), cols(alog_ref), cols(dtb_ref), cols(dsk_ref), nw, st


def _m2_chunks_fwd(sxbc, z, dtraw, alog, dtb, dsk, nw):
    t, w2, g, n, gw, rep, nc = _m2_dims(sxbc, z)
    hm = dtraw.shape[1]
    c = CHUNK
    wx = sxbc.shape[1]

    def body(sx_ref, z_ref, dt_ref, alog_ref, dtb_ref, dsk_ref, nw_ref, y_ref, ssave_ref, st_ref):
        @pl.when(pl.program_id(0) == 0)
        def _():
            st_ref[...] = jnp.zeros_like(st_ref)

        st = st_ref[...]
        ssave_ref[...] = st
        y, st_out = _m2_groups(*_m2_args((sx_ref, z_ref, dt_ref, alog_ref, dtb_ref, dsk_ref, nw_ref), g, n, gw, rep, st))
        for i in range(g):
            y_ref[:, i * gw:(i + 1) * gw] = y[i]
        st_ref[...] = st_out

    in_specs = [pl.BlockSpec((c, wx), lambda k: (k, 0)), pl.BlockSpec((c, w2), lambda k: (k, 0)),
                pl.BlockSpec((c, hm), lambda k: (k, 0)),
                pl.BlockSpec((1, hm), lambda k: (0, 0)), pl.BlockSpec((1, hm), lambda k: (0, 0)),
                pl.BlockSpec((1, hm), lambda k: (0, 0)), pl.BlockSpec((1, w2), lambda k: (0, 0))]
    out_shape = [_sds((t, w2), F32), _sds((nc, g, n, gw), F32)]
    out_specs = [pl.BlockSpec((c, w2), lambda k: (k, 0)), pl.BlockSpec((None, g, n, gw), lambda k: (k, 0, 0, 0))]
    return _call(body, name="m2_chunks_fwd", out_shape=out_shape, grid=(nc,), in_specs=in_specs,
                 out_specs=out_specs, scratch_shapes=[pltpu.VMEM((g, n, gw), F32)],
                 semantics=("arbitrary",))(sxbc, z, dtraw, alog, dtb, dsk, nw)


def _m2_chunks_bwd(sxbc, z, dtraw, alog, dtb, dsk, nw, ssave, dy):
    t, w2, g, n, gw, rep, nc = _m2_dims(sxbc, z)
    hm = dtraw.shape[1]
    c = CHUNK

    wx = sxbc.shape[1]

    def body(sx_ref, z_ref, dt_ref, alog_ref, dtb_ref, dsk_ref, nw_ref, ssave_ref, dy_ref,
             dsx_ref, dz_ref, ddt_ref, dalog_ref, ddtb_ref, ddsk_ref, dnw_ref, dst_ref):
        @pl.when(pl.program_id(0) == 0)
        def _():
            dst_ref[...] = jnp.zeros_like(dst_ref)
            dnw_ref[...] = jnp.zeros_like(dnw_ref)
            dalog_ref[...] = jnp.zeros_like(dalog_ref)
            ddtb_ref[...] = jnp.zeros_like(ddtb_ref)
            ddsk_ref[...] = jnp.zeros_like(ddsk_ref)

        args = _m2_args((sx_ref, z_ref, dt_ref, alog_ref, dtb_ref, dsk_ref, nw_ref), g, n, gw, rep, ssave_ref[...])
        _, vjp = jax.vjp(_m2_groups, *args)
        dyb = jnp.stack([dy_ref[:, i * gw:(i + 1) * gw] for i in range(g)])
        dxs, dz, dbm, dcm, ddts, dalogs, ddtbs, ddsks, dnw, dst = vjp((dyb, dst_ref[...]))
        dst_ref[...] = dst
        lane_h = lax.broadcasted_iota(jnp.int32, (1, hm), 1)
        ddt = jnp.zeros((c, hm), F32)
        for gi in range(g):
            dsx_ref[:, gi * gw:(gi + 1) * gw] = dxs[gi]
            dsx_ref[:, w2 + gi * n:w2 + (gi + 1) * n] = dbm[gi]
            dsx_ref[:, w2 + (g + gi) * n:w2 + (g + gi + 1) * n] = dcm[gi]
            dz_ref[:, gi * gw:(gi + 1) * gw] = dz[gi].astype(dz_ref.dtype)
            dnw_ref[:, gi * gw:(gi + 1) * gw] += dnw[gi]
            for i in range(rep):
                onehot = (lane_h == gi * rep + i).astype(F32)
                ddt = ddt + ddts[i][gi] * onehot
                dalog_ref[...] += dalogs[i][gi] * onehot
                ddtb_ref[...] += ddtbs[i][gi] * onehot
                ddsk_ref[...] += ddsks[i][gi] * onehot
        ddt_ref[...] = ddt

    rev = lambda k: nc - 1 - k
    in_specs = [pl.BlockSpec((c, wx), lambda k: (rev(k), 0)), pl.BlockSpec((c, w2), lambda k: (rev(k), 0)),
                pl.BlockSpec((c, hm), lambda k: (rev(k), 0)),
                pl.BlockSpec((1, hm), lambda k: (0, 0)), pl.BlockSpec((1, hm), lambda k: (0, 0)),
                pl.BlockSpec((1, hm), lambda k: (0, 0)), pl.BlockSpec((1, w2), lambda k: (0, 0)),
                pl.BlockSpec((None, g, n, gw), lambda k: (rev(k), 0, 0, 0)),
                pl.BlockSpec((c, w2), lambda k: (rev(k), 0))]
    out_shape = [_sds((t, wx), F32), _sds((t, w2), BF16), _sds((t, hm), F32), _sds((1, hm), F32), _sds((1, hm), F32),
                 _sds((1, hm), F32), _sds((1, w2), F32)]
    out_specs = [pl.BlockSpec((c, wx), lambda k: (rev(k), 0)), pl.BlockSpec((c, w2), lambda k: (rev(k), 0)),
                 pl.BlockSpec((c, hm), lambda k: (rev(k), 0)),
                 pl.BlockSpec((1, hm), lambda k: (0, 0)), pl.BlockSpec((1, hm), lambda k: (0, 0)),
                 pl.BlockSpec((1, hm), lambda k: (0, 0)), pl.BlockSpec((1, w2), lambda k: (0, 0))]
    return _call(body, name="m2_chunks_bwd", out_shape=out_shape, grid=(nc,), in_specs=in_specs,
                 out_specs=out_specs, scratch_shapes=[pltpu.VMEM((g, n, gw), F32)],
                 semantics=("arbitrary",))(sxbc, z, dtraw, alog, dtb, dsk, nw, ssave, dy)


def _s5_scan(bu, a_l, cw, reverse, name, tb=256):
    t, wtot = bu.shape
    jn = wtot // cw
    half = cw // 2
    tb = _tile(t, tb, SUBLANES)
    nb = t // tb

    def body(bu_ref, a_ref, s_ref, st_ref):
        @pl.when(pl.program_id(1) == 0)
        def _():
            st_ref[...] = jnp.zeros_like(st_ref)

        are = a_ref[:, :half]
        aim = -a_ref[:, half:] if reverse else a_ref[:, half:]

        def step(k, carry):
            sre, sim = carry
            r = tb - 1 - k if reverse else k
            nre = are * sre - aim * sim + bu_ref[pl.ds(r, 1), pl.ds(0, half)]
            nim = are * sim + aim * sre + bu_ref[pl.ds(r, 1), pl.ds(half, half)]
            s_ref[pl.ds(r, 1), pl.ds(0, half)] = nre
            s_ref[pl.ds(r, 1), pl.ds(half, half)] = nim
            return nre, nim

        sre, sim = lax.fori_loop(0, tb, step, (st_ref[:, :half], st_ref[:, half:]), unroll=8)
        st_ref[:, :half] = sre
        st_ref[:, half:] = sim

    rb = (lambda i: nb - 1 - i) if reverse else (lambda i: i)
    return _call(body, name=name, out_shape=_sds((t, wtot), F32), grid=(jn, nb),
                 in_specs=[pl.BlockSpec((tb, cw), lambda j, i: (rb(i), j)), pl.BlockSpec((1, cw), lambda j, i: (0, j))],
                 out_specs=pl.BlockSpec((tb, cw), lambda j, i: (rb(i), j)),
                 scratch_shapes=[pltpu.VMEM((1, cw), F32)], semantics=("parallel", "arbitrary"))(bu, a_l)


def _s5_da(ds, s, cw, tb=256):
    t, wtot = ds.shape
    jn = wtot // cw
    half = cw // 2
    tb = _tile(t, tb, SUBLANES)
    nb = t // tb
    hb = tb // SUBLANES

    def body(ds_ref, s_ref, halo_ref, o_ref):
        i = pl.program_id(1)

        @pl.when(i == 0)
        def _():
            o_ref[...] = jnp.zeros_like(o_ref)

        prev = jnp.where(i == 0, 0.0, halo_ref[SUBLANES - 1:SUBLANES, :])
        row = lax.broadcasted_iota(jnp.int32, (tb, cw), 0)
        sh = jnp.where(row == 0, prev, pltpu.roll(s_ref[...], 1, 0))
        d = ds_ref[...]
        dre, dim, sre, sim = d[:, :half], d[:, half:], sh[:, :half], sh[:, half:]
        o_ref[:, :half] += jnp.sum(dre * sre + dim * sim, axis=0, keepdims=True)
        o_ref[:, half:] += jnp.sum(dim * sre - dre * sim, axis=0, keepdims=True)

    return _call(body, name="s5_da", out_shape=_sds((1, wtot), F32), grid=(jn, nb),
                 in_specs=[pl.BlockSpec((tb, cw), lambda j, i: (i, j)), pl.BlockSpec((tb, cw), lambda j, i: (i, j)),
                           pl.BlockSpec((SUBLANES, cw), lambda j, i: (jnp.maximum(i * hb - 1, 0), j))],
                 out_specs=pl.BlockSpec((1, cw), lambda j, i: (0, j)),
                 semantics=("parallel", "arbitrary"))(ds, s, s)


def _conv_rows(t, tb):
    tb = _tile(t, tb, SUBLANES * 2)
    return tb, t // tb, tb // SUBLANES


def _conv_pre(x_ref, halo_ref, w_ref, b_ref, xe_ref, tb):
    i = pl.program_id(0)
    xe_ref[0:SUBLANES, :] = jnp.where(i == 0, 0.0, halo_ref[...])
    xe_ref[SUBLANES:, :] = x_ref[...]
    c = b_ref[...] + w_ref[0:1, :] * xe_ref[pl.ds(SUBLANES - CONV_K + 1, tb), :]
    for k in range(1, CONV_K):
        c = c + w_ref[k:k + 1, :] * xe_ref[pl.ds(SUBLANES - CONV_K + 1 + k, tb), :]
    return c


def _conv_fwd(x, w, b, name, tb=256):
    t, cwid = x.shape
    tb, nb, hb = _conv_rows(t, tb)

    def body(x_ref, halo_ref, w_ref, b_ref, o_ref, xe_ref):
        o_ref[...] = _silu(_conv_pre(x_ref, halo_ref, w_ref, b_ref, xe_ref, tb))

    return _call(body, name=name, out_shape=_sds((t, cwid), F32), grid=(nb,),
                 in_specs=[pl.BlockSpec((tb, cwid), lambda i: (i, 0)),
                           pl.BlockSpec((SUBLANES, cwid), lambda i: (jnp.maximum(i * hb - 1, 0), 0)),
                           pl.BlockSpec((CONV_K, cwid), lambda i: (0, 0)), pl.BlockSpec((1, cwid), lambda i: (0, 0))],
                 out_specs=pl.BlockSpec((tb, cwid), lambda i: (i, 0)),
                 scratch_shapes=[pltpu.VMEM((tb + SUBLANES, cwid), F32)], semantics=("arbitrary",))(x, x, w, b)


def _conv_bwd_act(x, w, b, ds, name, tb=256):
    t, cwid = x.shape
    tb, nb, hb = _conv_rows(t, tb)

    def body(x_ref, halo_ref, w_ref, b_ref, ds_ref, o_ref, xe_ref):
        c = _conv_pre(x_ref, halo_ref, w_ref, b_ref, xe_ref, tb)
        sg = _sigmoid(c)
        o_ref[...] = ds_ref[...] * (sg * (1.0 + c * (1.0 - sg)))

    return _call(body, name=name, out_shape=_sds((t, cwid), F32), grid=(nb,),
                 in_specs=[pl.BlockSpec((tb, cwid), lambda i: (i, 0)),
                           pl.BlockSpec((SUBLANES, cwid), lambda i: (jnp.maximum(i * hb - 1, 0), 0)),
                           pl.BlockSpec((CONV_K, cwid), lambda i: (0, 0)), pl.BlockSpec((1, cwid), lambda i: (0, 0)),
                           pl.BlockSpec((tb, cwid), lambda i: (i, 0))],
                 out_specs=pl.BlockSpec((tb, cwid), lambda i: (i, 0)),
                 scratch_shapes=[pltpu.VMEM((tb + SUBLANES, cwid), F32)], semantics=("arbitrary",))(x, x, w, b, ds)


def _conv_bwd_taps(dc, x, w, name, tb=256):
    t, cwid = x.shape
    tb, nb, hb = _conv_rows(t, tb)

    def body(dc_ref, nxt_ref, x_ref, halo_ref, w_ref, dx_ref, dw_ref, db_ref, dce_ref, xe_ref):
        i = pl.program_id(0)

        @pl.when(i == 0)
        def _():
            dw_ref[...] = jnp.zeros_like(dw_ref)
            db_ref[...] = jnp.zeros_like(db_ref)

        dc = dc_ref[...]
        dce_ref[0:tb, :] = dc
        dce_ref[tb:, :] = jnp.where(i == nb - 1, 0.0, nxt_ref[...])
        xe_ref[0:SUBLANES, :] = jnp.where(i == 0, 0.0, halo_ref[...])
        xe_ref[SUBLANES:, :] = x_ref[...]
        dx = w_ref[CONV_K - 1:CONV_K, :] * dc
        for k in range(CONV_K - 1):
            dx = dx + w_ref[k:k + 1, :] * dce_ref[pl.ds(CONV_K - 1 - k, tb), :]
        dx_ref[...] = dx.astype(dx_ref.dtype)
        for k in range(CONV_K):
            dw_ref[k:k + 1, :] += jnp.sum(dc * xe_ref[pl.ds(SUBLANES - CONV_K + 1 + k, tb), :], axis=0, keepdims=True)
        db_ref[...] += jnp.sum(dc, axis=0, keepdims=True)

    return _call(body, name=name, out_shape=[_sds((t, cwid), BF16), _sds((CONV_K, cwid), F32), _sds((1, cwid), F32)],
                 grid=(nb,),
                 in_specs=[pl.BlockSpec((tb, cwid), lambda i: (i, 0)),
                           pl.BlockSpec((SUBLANES, cwid), lambda i: (jnp.minimum((i + 1) * hb, nb * hb - 1), 0)),
                           pl.BlockSpec((tb, cwid), lambda i: (i, 0)),
                           pl.BlockSpec((SUBLANES, cwid), lambda i: (jnp.maximum(i * hb - 1, 0), 0)),
                           pl.BlockSpec((CONV_K, cwid), lambda i: (0, 0))],
                 out_specs=[pl.BlockSpec((tb, cwid), lambda i: (i, 0)), pl.BlockSpec((CONV_K, cwid), lambda i: (0, 0)),
                            pl.BlockSpec((1, cwid), lambda i: (0, 0))],
                 scratch_shapes=[pltpu.VMEM((tb + SUBLANES, cwid), F32), pltpu.VMEM((tb + SUBLANES, cwid), F32)],
                 semantics=("arbitrary",))(dc, dc, x, x, w)


def _f_rms(x, w):
    return (x * lax.rsqrt(jnp.mean(x * x, axis=-1, keepdims=True) + NORM_EPS) * w,)


def _f_s5_post1(ymm, u, d_l):
    return (_gelu_tanh(ymm + d_l * u),)


def _f_s5_post2(yg, tt, gate, b):
    return (yg * _sigmoid(tt + b) * _silu(gate),)


def _f_merge(ma, mb, mc, pa, pb, pc):
    return (_sigmoid(ma) * pa + _sigmoid(mb) * pb + _sigmoid(mc) * pc,)


def _loss_and_grad(x, tgt, fw, tb=256):
    t, dm = x.shape
    tb = _tile(t, tb, SUBLANES * 2)

    def f(xb, wb, tb_):
        y = _f_rms(xb, wb)[0]
        e = y - tb_
        return 0.5 * jnp.sum(jnp.mean(e * e, axis=-1, keepdims=True), axis=0, keepdims=True)

    def body(x_ref, t_ref, w_ref, loss_ref, dx_ref, dw_ref):
        @pl.when(pl.program_id(0) == 0)
        def _():
            loss_ref[...] = jnp.zeros_like(loss_ref)
            dw_ref[...] = jnp.zeros_like(dw_ref)

        tgt_b = t_ref[...]
        val, vjp = jax.vjp(lambda a, b: f(a, b, tgt_b), x_ref[...], w_ref[...])
        dxb, dwb = vjp(jnp.ones((1, 1), F32))
        loss_ref[...] += val
        dx_ref[...] = dxb
        dw_ref[...] += dwb

    return _call(body, name="loss_and_grad", out_shape=[_sds((1, 1), F32), _sds((t, dm), F32), _sds((1, dm), F32)],
                 grid=(t // tb,),
                 in_specs=[pl.BlockSpec((tb, dm), lambda i: (i, 0)), pl.BlockSpec((tb, dm), lambda i: (i, 0)),
                           pl.BlockSpec((1, dm), lambda i: (0, 0))],
                 out_specs=[pl.BlockSpec((1, 1), lambda i: (0, 0)), pl.BlockSpec((tb, dm), lambda i: (i, 0)),
                            pl.BlockSpec((1, dm), lambda i: (0, 0))],
                 semantics=("arbitrary",))(x, tgt, fw)


FLAT_W = 1024


def _sum_parts(parts, name, tb=256):
    n, r, wd = parts.shape
    tb = _tile(r, tb, SUBLANES)

    def body(p_ref, o_ref):
        acc = p_ref[0]
        for k in range(1, n):
            acc = acc + p_ref[k]
        o_ref[...] = acc

    return _call(body, name=name, out_shape=_sds((r, wd), F32), grid=(r // tb,),
                 in_specs=[pl.BlockSpec((n, tb, wd), lambda i: (0, i, 0))],
                 out_specs=pl.BlockSpec((tb, wd), lambda i: (i, 0)), semantics=("parallel",))(parts)


BLOCK_BYTES = 1 << 20


def _rows_per_block(r, wd):
    return _tile(r, max(SUBLANES * 2, BLOCK_BYTES // (4 * wd) // (SUBLANES * 2) * (SUBLANES * 2)), SUBLANES * 2)


def _add_my_half(g4, recv, c_idx, name):
    p, _, r, wd = g4.shape
    tb = _rows_per_block(r, wd)

    def body(c_ref, g_ref, r_ref, o_ref, ob_ref):
        s = g_ref[...] + r_ref[...]
        o_ref[...] = s
        ob_ref[...] = s.astype(BF16)

    spec = pl.BlockSpec((None, tb, wd), lambda j, i, c_ref: (j, i, 0))
    return _call(body, name=name, out_shape=[_sds((p, r, wd), F32), _sds((p, r, wd), BF16)], grid=(p, r // tb),
                 in_specs=[pl.BlockSpec((None, None, tb, wd), lambda j, i, c_ref: (j, c_ref[0], i, 0)), spec],
                 out_specs=[spec, spec], semantics=("parallel", "parallel"), num_scalar_prefetch=1)(c_idx, g4, recv)


def _sum_chips(own, got, me_idx, name):
    p, r, wd = own.shape
    tb = _rows_per_block(r, wd)

    def body(me_ref, own_ref, got_ref, o_ref):
        me = me_ref[0]
        acc = None
        for k in range(p):
            part = jnp.where(me == k, own_ref[...], got_ref[k].astype(F32))
            acc = part if acc is None else acc + part
        o_ref[...] = acc

    return _call(body, name=name, out_shape=_sds((r, wd), F32), grid=(r // tb,),
                 in_specs=[pl.BlockSpec((None, tb, wd), lambda i, me_ref: (me_ref[0], i, 0)),
                           pl.BlockSpec((p, tb, wd), lambda i, me_ref: (0, i, 0))],
                 out_specs=pl.BlockSpec((tb, wd), lambda i, me_ref: (i, 0)),
                 semantics=("parallel",), num_scalar_prefetch=1)(me_idx, own, got)


def _adamw(w, g, m, v, name):
    r, wd = w.shape
    tb = _rows_per_block(r, wd)

    def body(w_ref, g_ref, m_ref, v_ref, d_ref, nm_ref, nv_ref):
        gg = g_ref[...]
        nm = ADAM_B1 * m_ref[...] + (1.0 - ADAM_B1) * gg
        nv = ADAM_B2 * v_ref[...] + (1.0 - ADAM_B2) * (gg * gg)
        m_hat = nm / (1.0 - ADAM_B1 ** ADAM_STEP)
        v_hat = nv / (1.0 - ADAM_B2 ** ADAM_STEP)
        d_ref[...] = -ADAM_LR * (m_hat / (jnp.sqrt(v_hat) + ADAM_EPS) + ADAM_WD * w_ref[...])
        nm_ref[...] = nm
        nv_ref[...] = nv

    spec = pl.BlockSpec((tb, wd), lambda i: (i, 0))
    return _call(body, name=name, out_shape=[_sds((r, wd), F32)] * 3, grid=(r // tb,), in_specs=[spec] * 4,
                 out_specs=[spec] * 3, semantics=("parallel",))(w, g, m, v)


def _here():
    return lax.axis_index("x"), lax.axis_index("y"), lax.axis_index("c")


def _comm_call(body, name, out_shape, n_sems, operands):
    anyspec = pl.BlockSpec(memory_space=pl.ANY)
    outs = out_shape if isinstance(out_shape, (list, tuple)) else [out_shape]
    return _call(body, name=name, out_shape=out_shape, in_specs=[anyspec] * len(operands),
                 out_specs=[anyspec] * len(outs) if isinstance(out_shape, (list, tuple)) else anyspec,
                 scratch_shapes=[pltpu.SemaphoreType.DMA((n_sems,)), pltpu.SemaphoreType.DMA((n_sems,)),
                                 pltpu.SemaphoreType.DMA(())])(*operands)


def _gather_chips(x, name):
    def body(x_ref, o_ref, send_sems, recv_sems, local_sem):
        xi, yi, ci = _here()
        chips = [(1 - xi, yi), (xi, 1 - yi), (1 - xi, 1 - yi)]
        mine = pltpu.make_async_copy(x_ref, o_ref.at[2 * xi + yi], local_sem)
        mine.start()

        def copy(k, slot, to):
            return pltpu.make_async_remote_copy(src_ref=x_ref, dst_ref=o_ref.at[slot], send_sem=send_sems.at[k],
                                                recv_sem=recv_sems.at[k], device_id=to, device_id_type=MESH)

        sends = [copy(k, 2 * xi + yi, (px, py, ci)) for k, (px, py) in enumerate(chips)]
        for cp in sends:
            cp.start()
        for k, (px, py) in enumerate(chips):
            copy(k, 2 * px + py, (px, py, ci)).wait_recv()
        for cp in sends:
            cp.wait_send()
        mine.wait()

    return _comm_call(body, name, _sds((4,) + x.shape, x.dtype), 3, (x,))


def _gather_all(x, name):
    def body(x_ref, o_ref, send_sems, recv_sems, local_sem):
        xi, yi, ci = _here()
        me = 4 * xi + 2 * yi + ci
        flips = [(fx, fy, fc) for fx in (0, 1) for fy in (0, 1) for fc in (0, 1)][1:]
        peers = [((1 - xi) if fx else xi, (1 - yi) if fy else yi, (1 - ci) if fc else ci) for fx, fy, fc in flips]
        mine = pltpu.make_async_copy(x_ref, o_ref.at[me], local_sem)
        mine.start()

        def copy(k, slot, to):
            return pltpu.make_async_remote_copy(src_ref=x_ref, dst_ref=o_ref.at[slot], send_sem=send_sems.at[k],
                                                recv_sem=recv_sems.at[k], device_id=to, device_id_type=MESH)

        sends = [copy(k, me, p) for k, p in enumerate(peers)]
        for cp in sends:
            cp.start()
        for k, (px, py, pc) in enumerate(peers):
            copy(k, 4 * px + 2 * py + pc, (px, py, pc)).wait_recv()
        for cp in sends:
            cp.wait_send()
        mine.wait()

    return _comm_call(body, name, _sds((8,) + x.shape, x.dtype), 7, (x,))


def _multi_comm_call(body, name, out_shapes, n_sems, operands):
    anyspec = pl.BlockSpec(memory_space=pl.ANY)
    nin = len(operands)

    def flat_body(*refs):
        body(refs[:nin], refs[nin:nin + len(out_shapes)], refs[-2], refs[-1])

    return _call(flat_body, name=name, out_shape=list(out_shapes), in_specs=[anyspec] * nin,
                 out_specs=[anyspec] * len(out_shapes),
                 scratch_shapes=[pltpu.SemaphoreType.DMA((n_sems,)), pltpu.SemaphoreType.DMA((n_sems,))])(*operands)


def _remote(src, dst, send_sems, recv_sems, k, to):
    return pltpu.make_async_remote_copy(src_ref=src, dst_ref=dst, send_sem=send_sems.at[k], recv_sem=recv_sems.at[k],
                                        device_id=to, device_id_type=MESH)


def _gather_chips_split(xs, name):
    nw = len(xs)

    def body(x_refs, o_refs, send_sems, recv_sems):
        xi, yi, ci = _here()
        me = 2 * xi + yi
        sib = (xi, yi, 1 - ci)
        chips = [(1 - xi, yi), (xi, 1 - yi), (1 - xi, 1 - yi)]
        sends = []
        for i in range(nw):
            for k, (px, py) in enumerate(chips):
                sends.append(_remote(x_refs[i].at[ci], o_refs[i].at[me, ci], send_sems, recv_sems, 3 * i + k,
                                     (px, py, ci)))
        for cp in sends:
            cp.start()
        passed = []
        for i in range(nw):
            for k, (px, py) in enumerate(chips):
                landed = o_refs[i].at[2 * px + py, ci]
                _remote(landed, landed, send_sems, recv_sems, 3 * i + k, (px, py, ci)).wait_recv()
                fwd = _remote(landed, landed, send_sems, recv_sems, 3 * (nw + i) + k, sib)
                fwd.start()
                passed.append(fwd)
        for i in range(nw):
            for k, (px, py) in enumerate(chips):
                other = o_refs[i].at[2 * px + py, 1 - ci]
                _remote(other, other, send_sems, recv_sems, 3 * (nw + i) + k, sib).wait_recv()
        for cp in sends + passed:
            cp.wait_send()

    return _multi_comm_call(body, name, [_sds((4,) + x.shape, x.dtype) for x in xs], 6 * nw, xs)


def _swap_sibling_half(gs):
    def body(g_refs, o_refs, send_sems, recv_sems):
        xi, yi, ci = _here()
        cps = [_remote(g.at[:, 1 - ci], o, send_sems, recv_sems, i, (xi, yi, 1 - ci))
               for i, (g, o) in enumerate(zip(g_refs, o_refs))]
        for cp in cps:
            cp.start()
        for cp in cps:
            cp.wait()

    return _multi_comm_call(body, "swap_sibling_half", [_sds((g.shape[0],) + g.shape[2:], g.dtype) for g in gs],
                            len(gs), gs)


def _scatter_chips(gps):
    nw = len(gps)

    def body(g_refs, o_refs, send_sems, recv_sems):
        xi, yi, ci = _here()
        me = 2 * xi + yi
        chips = [(1 - xi, yi), (xi, 1 - yi), (1 - xi, 1 - yi)]
        sends = [_remote(g_refs[i].at[2 * px + py], o_refs[i].at[me], send_sems, recv_sems, 3 * i + k, (px, py, ci))
                 for i in range(nw) for k, (px, py) in enumerate(chips)]
        for cp in sends:
            cp.start()
        for i in range(nw):
            for k, (px, py) in enumerate(chips):
                slot = o_refs[i].at[2 * px + py]
                _remote(slot, slot, send_sems, recv_sems, 3 * i + k, (px, py, ci)).wait_recv()
        for cp in sends:
            cp.wait_send()

    return _multi_comm_call(body, "scatter_chips", [_sds(g.shape, g.dtype) for g in gps], 3 * nw, gps)


def _share_sibling(rs):
    def body(r_refs, o_refs, send_sems, recv_sems):
        xi, yi, ci = _here()
        cps = [_remote(r, o, send_sems, recv_sems, i, (xi, yi, 1 - ci)) for i, (r, o) in enumerate(zip(r_refs, o_refs))]
        for cp in cps:
            cp.start()
        for cp in cps:
            cp.wait()

    return _multi_comm_call(body, "share_sibling", [_sds(r.shape, r.dtype) for r in rs], len(rs), rs)


def _flat_pack(arrs, dtype, row_mult):
    flat = jnp.concatenate([a.astype(dtype).reshape(-1) for a in arrs])
    unit = FLAT_W * row_mult
    npad = -(-flat.shape[0] // unit) * unit
    return jnp.pad(flat, (0, npad - flat.shape[0])).reshape(npad // FLAT_W, FLAT_W)


def _flat_unpack(flat2d, shapes):
    flat = flat2d.reshape(-1)
    outs, off = [], 0
    for s in shapes:
        size = int(np.prod(s))
        outs.append(flat[off:off + size].reshape(s))
        off += size
    return outs


def _split_cols(a, widths):
    outs, off = [], 0
    for wd in widths:
        outs.append(lax.slice_in_dim(a, off, off + wd, axis=1))
        off += wd
    return outs


def _s5_params(lam_re, lam_im, log_step, b_re, b_im, c_re, c_im, d_skip):
    g, p = lam_re.shape
    hs = b_re.shape[2]
    gt = S5_GROUP_TILE
    jn = g // gt
    lam_re = jnp.minimum(lam_re, -1e-4)
    step = jnp.exp(log_step)[:, None]
    mag = jnp.exp(lam_re * step)
    ab_re = mag * jnp.cos(lam_im * step)
    ab_im = mag * jnp.sin(lam_im * step)
    den = lam_re * lam_re + lam_im * lam_im
    f_re = ((ab_re - 1.0) * lam_re + ab_im * lam_im) / den
    f_im = (ab_im * lam_re - (ab_re - 1.0) * lam_im) / den
    bb_re = f_re[..., None] * b_re - f_im[..., None] * b_im
    bb_im = f_re[..., None] * b_im + f_im[..., None] * b_re
    a_l = jnp.concatenate([ab_re.reshape(jn, gt * p), ab_im.reshape(jn, gt * p)], axis=1).reshape(1, jn * 2 * gt * p)
    eye = jnp.eye(gt, dtype=F32)

    def blockdiag(m):
        return jnp.einsum('jahp,ab->jahbp', m.reshape(jn, gt, hs, p), eye).reshape(jn, gt * hs, gt * p)

    b_blk = jnp.concatenate([blockdiag(bb_re.transpose(0, 2, 1)), blockdiag(bb_im.transpose(0, 2, 1))], axis=2)
    c_blk = jnp.concatenate([blockdiag(c_re), blockdiag(-c_im)], axis=2)
    return a_l, b_blk, c_blk, d_skip.reshape(1, g * hs)


def _layer_dims(p):
    d_model = p['w_out'].shape[1]
    wa = p['proj_a'].shape[0]
    h = p['gdn_a_log'].shape[0]
    wb = p['proj_b'].shape[0]
    wc = p['proj_c'].shape[0]
    hm = p['m2_a_log'].shape[0]
    cdim = p['m2_conv_w'].shape[1]
    splits = (3 * wa, wa, h, h, wb, wb, wc, cdim, hm, d_model, d_model, d_model)
    n_in = sum(splits)
    return splits, n_in, -(-n_in // LANES) * LANES


def _layer_fwd(x, p):
    splits, n_in, n_pad = _layer_dims(p)
    sv = {'x': x}
    h = _rowwise(_f_rms, [x], [p['norm_w'][None]], [x.shape[1]], [BF16], "rms_fwd")[0]
    w_in = jnp.pad(p['w_in'], ((0, 0), (0, n_pad - n_in)))
    proj = _matmul(h, w_in, 'nn', F32, "in_proj", tn=1152)
    qkv, az, braw, araw, su, sgate, cz, cxbc, cdt, ma, mb, mc = _split_cols(proj, splits)
    sv.update(h=h, w_in=w_in, qkv=qkv, az=az, braw=braw, araw=araw, su=su, sgate=sgate, cz=cz, cxbc=cxbc, cdt=cdt,
              ma=ma, mb=mb, mc=mc)
    gb0 = jnp.zeros((1, qkv.shape[1]), F32)
    sqkv = _conv_fwd(qkv, p['gdn_conv_w'], gb0, "gdn_conv_fwd")
    ya, ssa, tsa = _gdn_chunks_fwd(sqkv, az, braw, araw, p['gdn_a_log'][None], p['gdn_dt_bias'][None],
                                   p['gdn_norm_w'][None])
    sv.update(sqkv=sqkv, ssa=ssa, tsa=tsa, ya=ya)
    s5_in = tuple(p[k] for k in ('s5_lam_re', 's5_lam_im', 's5_log_step', 's5_b_re', 's5_b_im', 's5_c_re', 's5_c_im',
                                 's5_d'))
    (a_l, b_blk, c_blk, d_l), s5_vjp = jax.vjp(_s5_params, *s5_in)
    cw = b_blk.shape[2]
    bu = _matmul_bd(su, b_blk, 'nn', F32, "s5_bu")
    s = _s5_scan(bu, a_l, cw, False, "s5_scan_fwd")
    ymm = _matmul_bd(s, c_blk, 'nt', F32, "s5_out")
    yg = _rowwise(_f_s5_post1, [ymm, su], [d_l], [su.shape[1]], [F32], "s5_post1_fwd")[0]
    tt = _matmul(yg, p['s5_glu_w'], 'nn', F32, "s5_glu")
    yb = _rowwise(_f_s5_post2, [yg, tt, sgate], [p['s5_glu_b'][None]], [su.shape[1]], [F32], "s5_post2_fwd")[0]
    sv.update(a_l=a_l, b_blk=b_blk, c_blk=c_blk, d_l=d_l, s5_vjp=s5_vjp, s=s, ymm=ymm, yg=yg, tt=tt, yb=yb, cw=cw)
    sxbc = _conv_fwd(cxbc, p['m2_conv_w'], p['m2_conv_b'][None], "m2_conv_fwd")
    yc, ssc = _m2_chunks_fwd(sxbc, cz, cdt, p['m2_a_log'][None], p['m2_dt_bias'][None], p['m2_d'][None],
                             p['m2_norm_w'][None])
    sv.update(sxbc=sxbc, ssc=ssc, yc=yc)
    pa = _matmul(ya, p['proj_a'], 'nn', F32, "proj_a")
    pb = _matmul(yb, p['proj_b'], 'nn', F32, "proj_b")
    pc = _matmul(yc, p['proj_c'], 'nn', F32, "proj_c")
    merged = _rowwise(_f_merge, [ma, mb, mc, pa, pb, pc], [], [x.shape[1]], [BF16], "merge_fwd", tb=128)[0]
    x_next = _matmul(merged, p['w_out'], 'nn', F32, "out_proj", add=x)
    sv.update(pa=pa, pb=pb, pc=pc, merged=merged)
    return x_next, sv


def _layer_bwd(dx_out, p, sv):
    splits, n_in, n_pad = _layer_dims(p)
    g = {}
    dmerged = _matmul(dx_out, p['w_out'], 'nt', F32, "out_proj_dx")
    g['w_out'] = _matmul(sv['merged'], dx_out, 'tn', F32, "out_proj_dw")
    dma, dmb, dmc, dpa, dpb, dpc = _rowwise_bwd(
        _f_merge, [sv['ma'], sv['mb'], sv['mc'], sv['pa'], sv['pb'], sv['pc']], [], [dmerged], [BF16] * 6,
        "merge_bwd", tb=128)
    dya = _matmul(dpa, p['proj_a'], 'nt', F32, "proj_a_dx")
    dyb = _matmul(dpb, p['proj_b'], 'nt', F32, "proj_b_dx")
    dyc = _matmul(dpc, p['proj_c'], 'nt', F32, "proj_c_dx")
    g['proj_a'] = _matmul(sv['ya'], dpa, 'tn', F32, "proj_a_dw")
    g['proj_b'] = _matmul(sv['yb'], dpb, 'tn', F32, "proj_b_dw")
    g['proj_c'] = _matmul(sv['yc'], dpc, 'tn', F32, "proj_c_dw")
    alog, dtb, gnw = p['gdn_a_log'][None], p['gdn_dt_bias'][None], p['gdn_norm_w'][None]
    dq, dk, dv, daz, db3, da3, dalog, ddtb, dgnw = _gdn_chunks_bwd(sv['sqkv'], sv['az'], sv['braw'], sv['araw'], alog,
                                                                   dtb, gnw, sv['ssa'], sv['tsa'], dya)
    gb0 = jnp.zeros((1, sv['qkv'].shape[1]), F32)
    dcq = _conv_bwd_act(sv['qkv'], p['gdn_conv_w'], gb0, jnp.concatenate([dq, dk, dv], axis=1), "gdn_conv_bwd_act")
    dqkv, g['gdn_conv_w'], _ = _conv_bwd_taps(dcq, sv['qkv'], p['gdn_conv_w'], "gdn_conv_bwd_taps")
    dbraw, daraw = jnp.sum(db3, axis=0), jnp.sum(da3, axis=0)
    g.update(gdn_a_log=dalog[0], gdn_dt_bias=ddtb[0], gdn_norm_w=dgnw[0])
    dsx, dcz, dcdt, dmalog, dmdtb, dmdsk, dmnw = _m2_chunks_bwd(
        sv['sxbc'], sv['cz'], sv['cdt'], p['m2_a_log'][None], p['m2_dt_bias'][None], p['m2_d'][None],
        p['m2_norm_w'][None], sv['ssc'], dyc)
    dcx = _conv_bwd_act(sv['cxbc'], p['m2_conv_w'], p['m2_conv_b'][None], dsx, "m2_conv_bwd_act")
    dcxbc, g['m2_conv_w'], dconvb = _conv_bwd_taps(dcx, sv['cxbc'], p['m2_conv_w'], "m2_conv_bwd_taps")
    g.update(m2_conv_b=dconvb[0], m2_a_log=dmalog[0], m2_dt_bias=dmdtb[0], m2_d=dmdsk[0], m2_norm_w=dmnw[0])
    dyg1, dtt, dsgate, dglub = _rowwise_bwd(_f_s5_post2, [sv['yg'], sv['tt'], sv['sgate']], [p['s5_glu_b'][None]],
                                            [dyb], [F32, BF16, BF16], "s5_post2_bwd")
    dyg = _matmul(dtt, p['s5_glu_w'], 'nt', F32, "s5_glu_dx", add=dyg1)
    g['s5_glu_w'] = _matmul(sv['yg'], dtt, 'tn', F32, "s5_glu_dw")
    g['s5_glu_b'] = dglub[0]
    dymm, dsu1, dd_l = _rowwise_bwd(_f_s5_post1, [sv['ymm'], sv['su']], [sv['d_l']], [dyg], [BF16, F32],
                                    "s5_post1_bwd")
    gy = _matmul_bd(dymm, sv['c_blk'], 'nn', F32, "s5_out_dx")
    ds = _s5_scan(gy, sv['a_l'], sv['cw'], True, "s5_scan_bwd")
    da_l = _s5_da(ds, sv['s'], sv['cw'])
    dsu = (dsu1 + _matmul_bd(ds, sv['b_blk'], 'nt', F32, "s5_bu_dx")).astype(BF16)
    ka = sv['b_blk'].shape[1]
    db_blk = _matmul_bd(sv['su'], ds, ('tn', ka, sv['cw']), F32, "s5_bu_dw")
    dc_blk = _matmul_bd(dymm, sv['s'], ('tn', ka, sv['cw']), F32, "s5_out_dw")
    for k, v in zip(('s5_lam_re', 's5_lam_im', 's5_log_step', 's5_b_re', 's5_b_im', 's5_c_re', 's5_c_im', 's5_d'),
                    sv['s5_vjp']((da_l, db_blk, dc_blk, dd_l))):
        g[k] = v
    dproj = jnp.concatenate([dqkv, daz, dbraw.astype(BF16), daraw.astype(BF16), dsu, dsgate, dcz, dcxbc,
                             dcdt.astype(BF16), dma, dmb, dmc, jnp.zeros((dqkv.shape[0], n_pad - n_in), BF16)], axis=1)
    dh = _matmul(dproj, sv['w_in'], 'nt', F32, "in_proj_dx", tk=1152)
    g['w_in'] = lax.slice_in_dim(_matmul(sv['h'], dproj, 'tn', F32, "in_proj_dw", tn=1152), 0, n_in, axis=1)
    dx, dnw = _rowwise_bwd(_f_rms, [sv['x']], [p['norm_w'][None]], [dh], [F32], "rms_bwd", addend=dx_out)
    g['norm_w'] = dnw[0]
    return dx, g


INPUT_NAMES = (['x'] + WEIGHT_NAMES + ['loss_target'] + ['m_' + n for n in WEIGHT_NAMES]
               + ['v_' + n for n in WEIGHT_NAMES])


def _step(d):
    xi, yi, ci = _here()
    me = 2 * xi + yi
    depth = d['norm_w'].shape[0]
    big, ssm = list(BIG), list(SHARDED_SMALL)
    nsh = 4
    full = {}
    halves = [d[n].astype(BF16).reshape(2, -1, d[n].shape[-1]) for n in big]
    for n, hv, got in zip(big, halves, _gather_chips_split(halves, "gather_weights")):
        got = lax.dynamic_update_slice(got, hv[None], (me, 0, 0, 0)).reshape((nsh,) + d[n].shape)
        full[n] = jnp.concatenate([got[j] for j in range(nsh)], axis=BIG[n])
    cg = _gather_chips(_flat_pack([d[n] for n in ssm], F32, 8), "gather_conv_weights")
    parts = [_flat_unpack(cg[j], [d[n].shape for n in ssm]) for j in range(nsh)]
    for i, n in enumerate(ssm):
        full[n] = jnp.concatenate([parts[j][i] for j in range(nsh)], axis=SHARDED_SMALL[n])
    layer_names = [n for n in WEIGHT_NAMES if n != 'final_norm_w']

    def layer_params(l):
        return {n: (full[n][l] if n in full else d[n][l]) for n in layer_names}

    x = d['x'][0]
    saved = []
    for l in range(depth):
        x, sv = _layer_fwd(x, layer_params(l))
        saved.append(sv)
    loss11, dx, dfw = _loss_and_grad(x, d['loss_target'][0], d['final_norm_w'][None])
    loss = lax.psum(loss11[0, 0], ("x", "y", "c"))
    grads = [None] * depth
    for l in reversed(range(depth)):
        dx, grads[l] = _layer_bwd(dx, layer_params(l), saved[l])
    gfull = {n: jnp.stack([grads[l][n] for l in range(depth)]) for n in layer_names}
    gfull['final_norm_w'] = dfw[0]
    def shard(a, axis, j):
        wd = a.shape[axis] // nsh
        return lax.slice_in_dim(a, j * wd, (j + 1) * wd, axis=axis)

    c_idx = jnp.reshape(ci, (1,)).astype(jnp.int32)
    me_idx = jnp.reshape(me, (1,)).astype(jnp.int32)
    g4 = [jnp.stack([shard(gfull[n], BIG[n], j) for j in range(nsh)]).reshape(nsh, 2, -1, d[n].shape[-1])
          for n in big]
    pairs = [_add_my_half(g, r, c_idx, "add_my_half_" + n) for n, g, r in zip(big, g4, _swap_sibling_half(g4))]
    got = _scatter_chips([pb for _, pb in pairs])
    mine = [_sum_chips(pf, gt, me_idx, "sum_chips_" + n) for n, (pf, _), gt in zip(big, pairs, got)]
    theirs = _share_sibling(mine)
    out = {}
    for n, mn, th in zip(big, mine, theirs):
        both = jnp.where(ci == 0, jnp.stack([mn, th]), jnp.stack([th, mn]))
        w2, m2, v2 = (d[pre + n].reshape(both.shape[0] * both.shape[1], both.shape[2]) for pre in ('', 'm_', 'v_'))
        g2 = both.reshape(w2.shape)
        dl, nm, nv = _adamw(w2, g2, m2, v2, "adamw_" + n)
        for key, arr in (('grad_', g2), ('delta_', dl), ('new_m_', nm), ('new_v_', nv)):
            out[key + n] = arr.reshape(d[n].shape)
    small = [n for n in WEIGHT_NAMES if n not in BIG]
    sshapes = [gfull[n].shape for n in small]
    gsm = _sum_parts(_gather_all(_flat_pack([gfull[n] for n in small], F32, 8), "gather_small_grads"), "sum_devices")
    gs = dict(zip(small, _flat_unpack(gsm, sshapes)))
    for n in ssm:
        wd = d[n].shape[SHARDED_SMALL[n]]
        gs[n] = lax.dynamic_slice_in_dim(gs[n], me * wd, wd, axis=SHARDED_SMALL[n])
    lshapes = [d[n].shape for n in small]
    wps, gps, mps, vps = (_flat_pack(arrs, F32, 16) for arrs in (
        [d[n] for n in small], [gs[n] for n in small], [d['m_' + n] for n in small], [d['v_' + n] for n in small]))
    dl, nm, nv = _adamw(wps, gps, mps, vps, "adamw_small")
    for key, arr in (('grad_', gps), ('delta_', dl), ('new_m_', nm), ('new_v_', nv)):
        for n, a in zip(small, _flat_unpack(arr, lshapes)):
            out[key + n] = a
    res = [loss, dx[None]]
    for key in ('grad_', 'delta_', 'new_m_', 'new_v_'):
        res += [out[key + n] for n in WEIGHT_NAMES]
    return tuple(res)


def kernel(x, norm_w, w_in, gdn_conv_w, gdn_a_log, gdn_dt_bias, gdn_norm_w, s5_lam_re, s5_lam_im, s5_log_step, s5_b_re, s5_b_im, s5_c_re, s5_c_im, s5_d, s5_glu_w, s5_glu_b, m2_conv_w, m2_conv_b, m2_a_log, m2_dt_bias, m2_d, m2_norm_w, proj_a, proj_b, proj_c, w_out, final_norm_w, loss_target, m_norm_w, m_w_in, m_gdn_conv_w, m_gdn_a_log, m_gdn_dt_bias, m_gdn_norm_w, m_s5_lam_re, m_s5_lam_im, m_s5_log_step, m_s5_b_re, m_s5_b_im, m_s5_c_re, m_s5_c_im, m_s5_d, m_s5_glu_w, m_s5_glu_b, m_m2_conv_w, m_m2_conv_b, m_m2_a_log, m_m2_dt_bias, m_m2_d, m_m2_norm_w, m_proj_a, m_proj_b, m_proj_c, m_w_out, m_final_norm_w, v_norm_w, v_w_in, v_gdn_conv_w, v_gdn_a_log, v_gdn_dt_bias, v_gdn_norm_w, v_s5_lam_re, v_s5_lam_im, v_s5_log_step, v_s5_b_re, v_s5_b_im, v_s5_c_re, v_s5_c_im, v_s5_d, v_s5_glu_w, v_s5_glu_b, v_m2_conv_w, v_m2_conv_b, v_m2_a_log, v_m2_dt_bias, v_m2_d, v_m2_norm_w, v_proj_a, v_proj_b, v_proj_c, v_w_out, v_final_norm_w):
    args = (x, norm_w, w_in, gdn_conv_w, gdn_a_log, gdn_dt_bias, gdn_norm_w, s5_lam_re, s5_lam_im, s5_log_step, s5_b_re, s5_b_im, s5_c_re, s5_c_im, s5_d, s5_glu_w, s5_glu_b, m2_conv_w, m2_conv_b, m2_a_log, m2_dt_bias, m2_d, m2_norm_w, proj_a, proj_b, proj_c, w_out, final_norm_w, loss_target, m_norm_w, m_w_in, m_gdn_conv_w, m_gdn_a_log, m_gdn_dt_bias, m_gdn_norm_w, m_s5_lam_re, m_s5_lam_im, m_s5_log_step, m_s5_b_re, m_s5_b_im, m_s5_c_re, m_s5_c_im, m_s5_d, m_s5_glu_w, m_s5_glu_b, m_m2_conv_w, m_m2_conv_b, m_m2_a_log, m_m2_dt_bias, m_m2_d, m_m2_norm_w, m_proj_a, m_proj_b, m_proj_c, m_w_out, m_final_norm_w, v_norm_w, v_w_in, v_gdn_conv_w, v_gdn_a_log, v_gdn_dt_bias, v_gdn_norm_w, v_s5_lam_re, v_s5_lam_im, v_s5_log_step, v_s5_b_re, v_s5_b_im, v_s5_c_re, v_s5_c_im, v_s5_d, v_s5_glu_w, v_s5_glu_b, v_m2_conv_w, v_m2_conv_b, v_m2_a_log, v_m2_dt_bias, v_m2_d, v_m2_norm_w, v_proj_a, v_proj_b, v_proj_c, v_w_out, v_final_norm_w)
    return _step(dict(zip(INPUT_NAMES, args)))
```

```python
import functools
import math

import jax
import jax.numpy as jnp
import numpy as np
from jax import lax
from jax.experimental import pallas as pl
from jax.experimental.pallas import tpu as pltpu

F32 = jnp.float32
BF16 = jnp.bfloat16
HI = lax.Precision.HIGH
MESH = pl.DeviceIdType.MESH

CHUNK = 64
CONV_K = 4
NORM_EPS = 1e-6
GDN_HEAD_DIM = 128
M2_HEAD_DIM = 64
M2_STATE = 128
M2_GROUPS = 4
S5_GROUP_TILE = 8
ADAM_LR = 0.001
ADAM_B1 = 0.9
ADAM_B2 = 0.999
ADAM_EPS = 1e-08
ADAM_WD = 0.01
ADAM_STEP = 10
LANES = 128
SUBLANES = 8
VMEM_LIMIT_BYTES = 56 * 1024 * 1024

WEIGHT_NAMES = ['norm_w', 'w_in', 'gdn_conv_w', 'gdn_a_log', 'gdn_dt_bias', 'gdn_norm_w', 's5_lam_re', 's5_lam_im',
                's5_log_step', 's5_b_re', 's5_b_im', 's5_c_re', 's5_c_im', 's5_d', 's5_glu_w', 's5_glu_b',
                'm2_conv_w', 'm2_conv_b', 'm2_a_log', 'm2_dt_bias', 'm2_d', 'm2_norm_w', 'proj_a', 'proj_b',
                'proj_c', 'w_out', 'final_norm_w']
BIG = {'w_in': 2, 'proj_a': 2, 'proj_b': 2, 'proj_c': 2, 'w_out': 1, 's5_glu_w': 1}
SHARDED_SMALL = {'gdn_conv_w': 2, 'm2_conv_w': 2}


def _call(body, *, name, out_shape, grid=None, in_specs=None, out_specs=None, scratch_shapes=(), semantics=None,
          num_scalar_prefetch=None):
    params = dict(vmem_limit_bytes=VMEM_LIMIT_BYTES)
    if semantics is not None:
        params['dimension_semantics'] = semantics
    kw = dict(name=name, out_shape=out_shape, compiler_params=pltpu.CompilerParams(**params))
    if num_scalar_prefetch is not None:
        kw['grid_spec'] = pltpu.PrefetchScalarGridSpec(num_scalar_prefetch=num_scalar_prefetch, grid=grid,
                                                       in_specs=in_specs, out_specs=out_specs,
                                                       scratch_shapes=scratch_shapes)
    else:
        if grid is not None:
            kw['grid'] = grid
        if in_specs is not None:
            kw['in_specs'] = in_specs
        if out_specs is not None:
            kw['out_specs'] = out_specs
        if scratch_shapes:
            kw['scratch_shapes'] = scratch_shapes
    return pl.pallas_call(body, **kw)


def _tile(n, target, unit):
    if n <= target:
        return n
    t = (target // unit) * unit
    while t >= unit:
        if n % t == 0:
            return t
        t -= unit
    raise ValueError(f"no tile for {n} (unit {unit}, target {target})")


def _sds(shape, dtype):
    return jax.ShapeDtypeStruct(tuple(shape), dtype)


def _sigmoid(x):
    return jax.nn.sigmoid(x)


def _silu(x):
    return x * jax.nn.sigmoid(x)


def _softplus(x):
    return jnp.maximum(x, 0.0) + jnp.log(1.0 + jnp.exp(-jnp.abs(x)))


def _gelu_tanh(x):
    return 0.5 * x * (1.0 + jnp.tanh(math.sqrt(2.0 / math.pi) * (x + 0.044715 * (x * x * x))))


def _dot(a, b, dims, prec=None):
    return lax.dot_general(a, b, (dims, ((), ())), precision=prec, preferred_element_type=F32)


def _nn(a, b, prec=None):
    return _dot(a, b, ((1,), (0,)), prec)


def _nt(a, b, prec=None):
    return _dot(a, b, ((1,), (1,)), prec)


def _tn(a, b, prec=None):
    return _dot(a, b, ((0,), (0,)), prec)


IN_PROJ_TILE = 1664
MATMUL_TILES = {'nn': (1024, 1024, 2048), 'nt': (1024, 1024, 2048), 'tn': (1024, 1024, 2048)}


def _matmul(a, b, mode, out_dtype, name, tm=None, tn=None, tk=None, add=None):
    tm, tn, tk = (t if t is not None else dflt for t, dflt in zip((tm, tn, tk), MATMUL_TILES[mode]))
    if mode == 'nn':
        (m, k), (k2, n) = a.shape, b.shape
    elif mode == 'nt':
        (m, k), (n, k2) = a.shape, b.shape
    else:
        (k, m), (k2, n) = a.shape, b.shape
    assert k == k2, (a.shape, b.shape, mode)
    tm = _tile(m, tm, LANES if mode == 'tn' else SUBLANES)
    tn = _tile(n, tn, LANES)
    tk = _tile(k, tk, LANES if mode != 'tn' else SUBLANES * 2)
    nk = k // tk
    dims = {'nn': ((1,), (0,)), 'nt': ((1,), (1,)), 'tn': ((0,), (0,))}[mode]

    def body(*refs):
        a_ref, b_ref = refs[:2]
        o_ref, acc_ref = refs[-2:]
        kk = pl.program_id(2)

        @pl.when(kk == 0)
        def _():
            acc_ref[...] = jnp.zeros_like(acc_ref)

        acc_ref[...] += _dot(a_ref[...].astype(BF16), b_ref[...].astype(BF16), dims)

        @pl.when(kk == nk - 1)
        def _():
            res = acc_ref[...]
            if add is not None:
                res = res + refs[2][...].astype(F32)
            o_ref[...] = res.astype(o_ref.dtype)

    a_spec = pl.BlockSpec((tk, tm), lambda i, j, kk: (kk, i)) if mode == 'tn' else pl.BlockSpec((tm, tk), lambda i, j, kk: (i, kk))
    b_spec = pl.BlockSpec((tn, tk), lambda i, j, kk: (j, kk)) if mode == 'nt' else pl.BlockSpec((tk, tn), lambda i, j, kk: (kk, j))
    o_spec = pl.BlockSpec((tm, tn), lambda i, j, kk: (i, j))
    ops = (a, b) if add is None else (a, b, add)
    return _call(body, name=name, out_shape=_sds((m, n), out_dtype), grid=(m // tm, n // tn, nk),
                 in_specs=[a_spec, b_spec] + ([] if add is None else [o_spec]), out_specs=o_spec,
                 scratch_shapes=[pltpu.VMEM((tm, tn), F32)], semantics=("parallel", "parallel", "arbitrary"))(*ops)


def _matmul_bd(a, b, mode, out_dtype, name, tm=1024):
    if mode in ('nn', 'nt'):
        t = a.shape[0]
        jn, ka, nb = b.shape
        tm = _tile(t, tm, SUBLANES)
        win, wout = (ka, nb) if mode == 'nn' else (nb, ka)
        assert a.shape[1] == jn * win

        def body(a_ref, b_ref, o_ref):
            if mode == 'nn':
                o_ref[...] = _nn(a_ref[...].astype(BF16), b_ref[...].astype(BF16)).astype(o_ref.dtype)
            else:
                o_ref[...] = _nt(a_ref[...].astype(BF16), b_ref[...].astype(BF16)).astype(o_ref.dtype)

        return _call(body, name=name, out_shape=_sds((t, jn * wout), out_dtype), grid=(t // tm, jn),
                     in_specs=[pl.BlockSpec((tm, win), lambda i, j: (i, j)),
                               pl.BlockSpec((None, ka, nb), lambda i, j: (j, 0, 0))],
                     out_specs=pl.BlockSpec((tm, wout), lambda i, j: (i, j)),
                     semantics=("parallel", "parallel"))(a, b)
    t = a.shape[0]
    ka, nb = mode[1], mode[2]
    jn = a.shape[1] // ka
    assert b.shape[1] == jn * nb
    tk = _tile(t, tm, SUBLANES * 2)
    nk = t // tk

    def body_tn(a_ref, b_ref, o_ref):
        @pl.when(pl.program_id(1) == 0)
        def _():
            o_ref[...] = jnp.zeros_like(o_ref)

        o_ref[...] += _tn(a_ref[...].astype(BF16), b_ref[...].astype(BF16))

    return _call(body_tn, name=name, out_shape=_sds((jn, ka, nb), F32), grid=(jn, nk),
                 in_specs=[pl.BlockSpec((tk, ka), lambda j, kk: (kk, j)), pl.BlockSpec((tk, nb), lambda j, kk: (kk, j))],
                 out_specs=pl.BlockSpec((None, ka, nb), lambda j, kk: (j, 0, 0)),
                 semantics=("parallel", "arbitrary"))(a, b)


def _rowwise(fn, rows, params, out_widths, out_dtypes, name, tb=256):
    t = rows[0].shape[0]
    tb = _tile(t, tb, SUBLANES * 2)
    nr, npar = len(rows), len(params)

    def body(*refs):
        ins = [r[...].astype(F32) for r in refs[:nr + npar]]
        outs = fn(*ins)
        for o_ref, o in zip(refs[nr + npar:], outs):
            o_ref[...] = o.astype(o_ref.dtype)

    in_specs = [pl.BlockSpec((tb, r.shape[1]), lambda i: (i, 0)) for r in rows]
    in_specs += [pl.BlockSpec(p.shape, lambda i: (0, 0)) for p in params]
    out_shape = [_sds((t, w), d) for w, d in zip(out_widths, out_dtypes)]
    out_specs = [pl.BlockSpec((tb, w), lambda i: (i, 0)) for w in out_widths]
    return _call(body, name=name, out_shape=out_shape, grid=(t // tb,), in_specs=in_specs, out_specs=out_specs,
                 semantics=("parallel",))(*rows, *params)


def _rowwise_bwd(fn, rows, params, cts, row_grad_dtypes, name, tb=256, addend=None):
    t = rows[0].shape[0]
    tb = _tile(t, tb, SUBLANES * 2)
    nr, npar, nc = len(rows), len(params), len(cts)
    keep = [i for i, d in enumerate(row_grad_dtypes) if d is not None]
    nadd = 0 if addend is None else 1

    def body(*refs):
        ins = [r[...].astype(F32) for r in refs[:nr + npar]]
        ct = [r[...].astype(F32) for r in refs[nr + npar:nr + npar + nc]]
        _, vjp = jax.vjp(fn, *ins)
        grads = vjp(tuple(ct))
        out_refs = refs[nr + npar + nc + nadd:]
        for o_ref, i in zip(out_refs[:len(keep)], keep):
            g = grads[i]
            if nadd and i == 0:
                g = g + refs[nr + npar + nc][...].astype(F32)
            o_ref[...] = g.astype(o_ref.dtype)

        @pl.when(pl.program_id(0) == 0)
        def _():
            for o_ref in out_refs[len(keep):]:
                o_ref[...] = jnp.zeros_like(o_ref)

        for o_ref, g in zip(out_refs[len(keep):], grads[nr:]):
            o_ref[...] += g

    in_specs = [pl.BlockSpec((tb, r.shape[1]), lambda i: (i, 0)) for r in rows]
    in_specs += [pl.BlockSpec(p.shape, lambda i: (0, 0)) for p in params]
    in_specs += [pl.BlockSpec((tb, c.shape[1]), lambda i: (i, 0)) for c in cts]
    extra = []
    if nadd:
        in_specs += [pl.BlockSpec((tb, addend.shape[1]), lambda i: (i, 0))]
        extra = [addend]
    out_shape = [_sds(rows[i].shape, row_grad_dtypes[i]) for i in keep] + [_sds(p.shape, F32) for p in params]
    out_specs = [pl.BlockSpec((tb, rows[i].shape[1]), lambda i_: (i_, 0)) for i in keep]
    out_specs += [pl.BlockSpec(p.shape, lambda i: (0, 0)) for p in params]
    return _call(body, name=name, out_shape=out_shape, grid=(t // tb,), in_specs=in_specs, out_specs=out_specs,
                 semantics=("arbitrary",))(*rows, *params, *cts, *extra)


def _chunk_masks(c):
    row = lax.broadcasted_iota(jnp.int32, (c, c), 0)
    col = lax.broadcasted_iota(jnp.int32, (c, c), 1)
    causal = row >= col
    strict = row > col
    return causal, strict, causal.astype(F32), (row > col).astype(F32), (row == col).astype(F32)


def _lane_pick(blk, idx):
    lane = lax.broadcasted_iota(jnp.int32, blk.shape, 1)
    return jnp.sum(jnp.where(lane == idx, blk, 0.0), axis=1, keepdims=True)


def _bdot(a, b, ca, cb, prec=None):
    return lax.dot_general(a, b, (((ca,), (cb,)), ((0,), (0,))), precision=prec, preferred_element_type=F32)


def _bnn(a, b, prec=None):
    return _bdot(a, b, 2, 1, prec)


def _bnt(a, b, prec=None):
    return _bdot(a, b, 2, 2, prec)


def _btn(a, b, prec=None):
    return _bdot(a, b, 1, 1, prec)


def _unit_lower_inverse(a_mat, eye):
    x = -a_mat
    t_inv = eye + x
    p = x
    for _ in range(int(math.log2(a_mat.shape[-1])) - 1):
        p = _bnn(p, p, HI)
        t_inv = t_inv + _bnn(t_inv, p, HI)
    return t_inv


@jax.custom_vjp
def _saved_inverse(a_mat, t_saved):
    return t_saved


def _saved_inverse_fwd(a_mat, t_saved):
    return t_saved, t_saved


def _saved_inverse_bwd(t_inv, ct):
    return -_bnt(_btn(t_inv, ct, HI), t_inv, HI), jnp.zeros_like(t_inv)


_saved_inverse.defvjp(_saved_inverse_fwd, _saved_inverse_bwd)


def _gdn_heads(q, k, v, z, braw, araw, alog, dtb, nw, s_in, t_saved=None):
    b, c, d = q.shape
    causal, strict, lower, upper_t, eye = _chunk_masks(c)
    lower_b = jnp.broadcast_to(lower[None], (b, c, c))
    qn = q * lax.rsqrt(jnp.sum(q * q, axis=-1, keepdims=True) + NORM_EPS) * (d ** -0.5)
    kn = k * lax.rsqrt(jnp.sum(k * k, axis=-1, keepdims=True) + NORM_EPS)
    beta = _sigmoid(braw)
    g = -jnp.exp(alog) * _softplus(araw + dtb)
    dlog = _bnn(lower_b, g * upper_t[None], HI)
    dm = jnp.where(causal[None], jnp.exp(dlog), 0.0)
    g_lanes = jnp.broadcast_to(g, (b, c, d))
    gc = _bnn(lower_b, g_lanes, HI)
    gl = jnp.sum(g_lanes, axis=1, keepdims=True)
    eg = jnp.exp(gc)
    kb = kn * beta
    a_mat = jnp.where(strict[None], _bnt(kb, kn) * dm, 0.0)
    t_inv = _unit_lower_inverse(a_mat, eye[None]) if t_saved is None else _saved_inverse(a_mat, t_saved)
    r = beta * (v - eg * _bnn(kn, s_in))
    v_new = _bnn(t_inv, r)
    qk = _bnt(qn, kn) * dm
    out = eg * _bnn(qn, s_in) + _bnn(qk, v_new)
    k_tail = kn * jnp.exp(gl - gc)
    s_out = s_in * jnp.exp(gl) + _btn(k_tail, v_new)
    y = out * lax.rsqrt(jnp.mean(out * out, axis=-1, keepdims=True) + NORM_EPS) * nw[None] * _silu(z)
    if t_saved is None:
        return y, s_out, t_inv
    return y, s_out


def _gdn_stack(refs, hb, d, h, first_head):
    q_ref, k_ref, v_ref, z_ref, b_ref, a_ref, alog_ref, dtb_ref = refs
    sls = [slice(i * d, (i + 1) * d) for i in range(hb)]
    heads = [first_head + i for i in range(hb)]
    wide = [jnp.stack([r[:, sl] for sl in sls]) for r in (q_ref, k_ref, v_ref, z_ref)]
    cols = [jnp.stack([_lane_pick(r[...], hd) for hd in heads]) for r in (b_ref, a_ref, alog_ref, dtb_ref)]
    return wide + cols


GDN_HEADS_PER_STEP = 8


def _gdn_chunks_fwd(sqkv, z, braw, araw, alog, dtb, nw, hb=GDN_HEADS_PER_STEP):
    t, w3 = sqkv.shape
    w = w3 // 3
    d = GDN_HEAD_DIM
    h = w // d
    hb = min(hb, h)
    hg = h // hb
    c = CHUNK
    nc = t // c

    def body(q_ref, k_ref, v_ref, z_ref, b_ref, a_ref, alog_ref, dtb_ref, nw_ref, y_ref, ssave_ref, tsave_ref,
             s_ref):
        @pl.when(pl.program_id(1) == 0)
        def _():
            s_ref[...] = jnp.zeros_like(s_ref)

        s_in = s_ref[...]
        ssave_ref[...] = s_in
        args = _gdn_stack((q_ref, k_ref, v_ref, z_ref, b_ref, a_ref, alog_ref, dtb_ref), hb, d, h,
                          pl.program_id(0) * hb)
        y, s_out, t_inv = _gdn_heads(*args, nw_ref[...], s_in)
        for i in range(hb):
            y_ref[:, i * d:(i + 1) * d] = y[i]
        s_ref[...] = s_out
        tsave_ref[...] = t_inv

    blk = (c, hb * d)
    in_specs = [pl.BlockSpec(blk, lambda g, n: (n, g)), pl.BlockSpec(blk, lambda g, n: (n, hg + g)),
                pl.BlockSpec(blk, lambda g, n: (n, 2 * hg + g)), pl.BlockSpec(blk, lambda g, n: (n, g)),
                pl.BlockSpec((c, h), lambda g, n: (n, 0)), pl.BlockSpec((c, h), lambda g, n: (n, 0)),
                pl.BlockSpec((1, h), lambda g, n: (0, 0)), pl.BlockSpec((1, h), lambda g, n: (0, 0)),
                pl.BlockSpec((1, d), lambda g, n: (0, 0))]
    out_shape = [_sds((t, w), F32), _sds((hg, nc, hb, d, d), F32), _sds((hg, nc, hb, c, c), F32)]
    out_specs = [pl.BlockSpec(blk, lambda g, n: (n, g)),
                 pl.BlockSpec((None, None, hb, d, d), lambda g, n: (g, n, 0, 0, 0)),
                 pl.BlockSpec((None, None, hb, c, c), lambda g, n: (g, n, 0, 0, 0))]
    return _call(body, name="gdn_chunks_fwd", out_shape=out_shape, grid=(hg, nc), in_specs=in_specs,
                 out_specs=out_specs, scratch_shapes=[pltpu.VMEM((hb, d, d), F32)],
                 semantics=("parallel", "arbitrary"))(sqkv, sqkv, sqkv, z, braw, araw, alog, dtb, nw)


def _gdn_chunks_bwd(sqkv, z, braw, araw, alog, dtb, nw, ssave, tsave, dy, hb=GDN_HEADS_PER_STEP):
    t, w3 = sqkv.shape
    w = w3 // 3
    d = GDN_HEAD_DIM
    h = w // d
    hb = min(hb, h)
    hg = h // hb
    c = CHUNK
    nc = t // c

    def body(q_ref, k_ref, v_ref, z_ref, b_ref, a_ref, alog_ref, dtb_ref, nw_ref, ssave_ref, tsave_ref, dy_ref,
             dq_ref, dk_ref, dv_ref, dz_ref, db_ref, da_ref, dalog_ref, ddtb_ref, dnw_ref, ds_ref):
        first = jnp.logical_and(pl.program_id(0) == 0, pl.program_id(1) == 0)

        @pl.when(pl.program_id(1) == 0)
        def _():
            ds_ref[...] = jnp.zeros_like(ds_ref)

        @pl.when(first)
        def _():
            dalog_ref[...] = jnp.zeros_like(dalog_ref)
            ddtb_ref[...] = jnp.zeros_like(ddtb_ref)
            dnw_ref[...] = jnp.zeros_like(dnw_ref)

        lane_h = lax.broadcasted_iota(jnp.int32, (1, h), 1)
        db_acc = jnp.zeros((c, h), F32)
        da_acc = jnp.zeros((c, h), F32)
        args = _gdn_stack((q_ref, k_ref, v_ref, z_ref, b_ref, a_ref, alog_ref, dtb_ref), hb, d, h,
                          pl.program_id(0) * hb)
        t_saved = tsave_ref[...]
        _, vjp = jax.vjp(lambda *a: _gdn_heads(*a, t_saved=t_saved), *args, nw_ref[...], ssave_ref[...])
        dyb = jnp.stack([dy_ref[:, i * d:(i + 1) * d] for i in range(hb)])
        dq, dk, dv, dz, db, da, dalog, ddtb, dnw, ds_in = vjp((dyb, ds_ref[...]))
        for i in range(hb):
            sl = slice(i * d, (i + 1) * d)
            dq_ref[:, sl] = dq[i]
            dk_ref[:, sl] = dk[i]
            dv_ref[:, sl] = dv[i]
            dz_ref[:, sl] = dz[i].astype(dz_ref.dtype)
            onehot = (lane_h == pl.program_id(0) * hb + i).astype(F32)
            db_acc = db_acc + db[i] * onehot
            da_acc = da_acc + da[i] * onehot
            dalog_ref[...] += dalog[i] * onehot
            ddtb_ref[...] += ddtb[i] * onehot
        dnw_ref[...] += dnw
        ds_ref[...] = ds_in
        db_ref[...] = db_acc
        da_ref[...] = da_acc

    blk = (c, hb * d)
    rev = lambda n: nc - 1 - n
    in_specs = [pl.BlockSpec(blk, lambda g, n: (rev(n), g)), pl.BlockSpec(blk, lambda g, n: (rev(n), hg + g)),
                pl.BlockSpec(blk, lambda g, n: (rev(n), 2 * hg + g)), pl.BlockSpec(blk, lambda g, n: (rev(n), g)),
                pl.BlockSpec((c, h), lambda g, n: (rev(n), 0)), pl.BlockSpec((c, h), lambda g, n: (rev(n), 0)),
                pl.BlockSpec((1, h), lambda g, n: (0, 0)), pl.BlockSpec((1, h), lambda g, n: (0, 0)),
                pl.BlockSpec((1, d), lambda g, n: (0, 0)),
                pl.BlockSpec((None, None, hb, d, d), lambda g, n: (g, rev(n), 0, 0, 0)),
                pl.BlockSpec((None, None, hb, c, c), lambda g, n: (g, rev(n), 0, 0, 0)),
                pl.BlockSpec(blk, lambda g, n: (rev(n), g))]
    out_shape = [_sds((t, w), F32), _sds((t, w), F32), _sds((t, w), F32), _sds((t, w), BF16),
                 _sds((hg, t, h), F32), _sds((hg, t, h), F32), _sds((1, h), F32), _sds((1, h), F32), _sds((1, d), F32)]
    out_specs = [pl.BlockSpec(blk, lambda g, n: (rev(n), g))] * 4
    out_specs += [pl.BlockSpec((None, c, h), lambda g, n: (g, rev(n), 0))] * 2
    out_specs += [pl.BlockSpec((1, h), lambda g, n: (0, 0)), pl.BlockSpec((1, h), lambda g, n: (0, 0)),
                  pl.BlockSpec((1, d), lambda g, n: (0, 0))]
    return _call(body, name="gdn_chunks_bwd", out_shape=out_shape, grid=(hg, nc), in_specs=in_specs,
                 out_specs=out_specs, scratch_shapes=[pltpu.VMEM((hb, d, d), F32)],
                 semantics=("arbitrary", "arbitrary"))(sqkv, sqkv, sqkv, z, braw, araw, alog, dtb, nw, ssave, tsave,
                                                       dy)


def _m2_groups(xs, z, bm, cm, dtraws, alogs, dtbs, dsks, nw, st):
    g, c, gw = xs.shape
    rep = len(dtraws)
    causal, _, lower, upper_t, _ = _chunk_masks(c)
    lower_b = jnp.broadcast_to(lower[None], (g, c, c))
    lane_head = lax.broadcasted_iota(jnp.int32, (1, 1, gw), 2) // M2_HEAD_DIM

    def expand(cols):
        res = jnp.broadcast_to(cols[-1], (g, cols[-1].shape[1], gw))
        for i in reversed(range(rep - 1)):
            res = jnp.where(lane_head == i, cols[i], res)
        return res

    dts = [_softplus(dtraws[i] + dtbs[i]) for i in range(rep)]
    adts = [-jnp.exp(alogs[i]) * dts[i] for i in range(rep)]
    dt_l, adt_l, dsk_l = expand(dts), expand(adts), expand(dsks)
    xdt = xs * dt_l
    acum = _bnn(lower_b, adt_l, HI)
    alast = jnp.sum(adt_l, axis=1, keepdims=True)
    scores = _bnt(cm, bm)
    y = jnp.exp(acum) * _bnn(cm, st) + dsk_l * xs
    for i in range(rep):
        seg = jnp.where(causal[None], jnp.exp(_bnn(lower_b, adts[i] * upper_t[None], HI)), 0.0)
        y = y + _bnn(scores * seg, jnp.where(lane_head == i, xdt, 0.0))
    st_out = st * jnp.exp(alast) + _btn(bm, xdt * jnp.exp(alast - acum))
    y2 = y * _silu(z)
    out = y2 * lax.rsqrt(jnp.mean(y2 * y2, axis=-1, keepdims=True) + NORM_EPS) * nw
    return out, st_out


def _m2_dims(sxbc, z):
    t = sxbc.shape[0]
    w2 = z.shape[1]
    g = M2_GROUPS
    n = M2_STATE
    assert sxbc.shape[1] == w2 + 2 * g * n
    gw = w2 // g
    return t, w2, g, n, gw, gw // M2_HEAD_DIM, t // CHUNK


def _m2_args(refs, g, n, gw, rep, st):
    sx_ref, z_ref, dt_ref, alog_ref, dtb_ref, dsk_ref, nw_ref = refs
    w2 = g * gw
    xs = jnp.stack([sx_ref[:, i * gw:(i + 1) * gw] for i in range(g)])
    z = jnp.stack([z_ref[:, i * gw:(i + 1) * gw] for i in range(g)])
    bm = jnp.stack([sx_ref[:, w2 + i * n:w2 + (i + 1) * n] for i in range(g)])
    cm = jnp.stack([sx_ref[:, w2 + (g + i) * n:w2 + (g + i + 1) * n] for i in range(g)])
    nw = jnp.stack([nw_ref[:, i * gw:(i + 1) * gw] for i in range(g)])

    def cols(ref):
        blk = ref[...]
        return [jnp.stack([_lane_pick(blk, gi * rep + i) for gi in range(g)]) for i in range(rep)]

    return xs, z, bm, cm, cols(dt_ref), cols(alog_ref), cols(dtb_ref), cols(dsk_ref), nw, st


def _m2_chunks_fwd(sxbc, z, dtraw, alog, dtb, dsk, nw):
    t, w2, g, n, gw, rep, nc = _m2_dims(sxbc, z)
    hm = dtraw.shape[1]
    c = CHUNK
    wx = sxbc.shape[1]

    def body(sx_ref, z_ref, dt_ref, alog_ref, dtb_ref, dsk_ref, nw_ref, y_ref, ssave_ref, st_ref):
        @pl.when(pl.program_id(0) == 0)
        def _():
            st_ref[...] = jnp.zeros_like(st_ref)

        st = st_ref[...]
        ssave_ref[...] = st
        y, st_out = _m2_groups(*_m2_args((sx_ref, z_ref, dt_ref, alog_ref, dtb_ref, dsk_ref, nw_ref), g, n, gw, rep, st))
        for i in range(g):
            y_ref[:, i * gw:(i + 1) * gw] = y[i]
        st_ref[...] = st_out

    in_specs = [pl.BlockSpec((c, wx), lambda k: (k, 0)), pl.BlockSpec((c, w2), lambda k: (k, 0)),
                pl.BlockSpec((c, hm), lambda k: (k, 0)),
                pl.BlockSpec((1, hm), lambda k: (0, 0)), pl.BlockSpec((1, hm), lambda k: (0, 0)),
                pl.BlockSpec((1, hm), lambda k: (0, 0)), pl.BlockSpec((1, w2), lambda k: (0, 0))]
    out_shape = [_sds((t, w2), F32), _sds((nc, g, n, gw), F32)]
    out_specs = [pl.BlockSpec((c, w2), lambda k: (k, 0)), pl.BlockSpec((None, g, n, gw), lambda k: (k, 0, 0, 0))]
    return _call(body, name="m2_chunks_fwd", out_shape=out_shape, grid=(nc,), in_specs=in_specs,
                 out_specs=out_specs, scratch_shapes=[pltpu.VMEM((g, n, gw), F32)],
                 semantics=("arbitrary",))(sxbc, z, dtraw, alog, dtb, dsk, nw)


def _m2_chunks_bwd(sxbc, z, dtraw, alog, dtb, dsk, nw, ssave, dy):
    t, w2, g, n, gw, rep, nc = _m2_dims(sxbc, z)
    hm = dtraw.shape[1]
    c = CHUNK

    wx = sxbc.shape[1]

    def body(sx_ref, z_ref, dt_ref, alog_ref, dtb_ref, dsk_ref, nw_ref, ssave_ref, dy_ref,
             dsx_ref, dz_ref, ddt_ref, dalog_ref, ddtb_ref, ddsk_ref, dnw_ref, dst_ref):
        @pl.when(pl.program_id(0) == 0)
        def _():
            dst_ref[...] = jnp.zeros_like(dst_ref)
            dnw_ref[...] = jnp.zeros_like(dnw_ref)
            dalog_ref[...] = jnp.zeros_like(dalog_ref)
            ddtb_ref[...] = jnp.zeros_like(ddtb_ref)
            ddsk_ref[...] = jnp.zeros_like(ddsk_ref)

        args = _m2_args((sx_ref, z_ref, dt_ref, alog_ref, dtb_ref, dsk_ref, nw_ref), g, n, gw, rep, ssave_ref[...])
        _, vjp = jax.vjp(_m2_groups, *args)
        dyb = jnp.stack([dy_ref[:, i * gw:(i + 1) * gw] for i in range(g)])
        dxs, dz, dbm, dcm, ddts, dalogs, ddtbs, ddsks, dnw, dst = vjp((dyb, dst_ref[...]))
        dst_ref[...] = dst
        lane_h = lax.broadcasted_iota(jnp.int32, (1, hm), 1)
        ddt = jnp.zeros((c, hm), F32)
        for gi in range(g):
            dsx_ref[:, gi * gw:(gi + 1) * gw] = dxs[gi]
            dsx_ref[:, w2 + gi * n:w2 + (gi + 1) * n] = dbm[gi]
            dsx_ref[:, w2 + (g + gi) * n:w2 + (g + gi + 1) * n] = dcm[gi]
            dz_ref[:, gi * gw:(gi + 1) * gw] = dz[gi].astype(dz_ref.dtype)
            dnw_ref[:, gi * gw:(gi + 1) * gw] += dnw[gi]
            for i in range(rep):
                onehot = (lane_h == gi * rep + i).astype(F32)
                ddt = ddt + ddts[i][gi] * onehot
                dalog_ref[...] += dalogs[i][gi] * onehot
                ddtb_ref[...] += ddtbs[i][gi] * onehot
                ddsk_ref[...] += ddsks[i][gi] * onehot
        ddt_ref[...] = ddt

    rev = lambda k: nc - 1 - k
    in_specs = [pl.BlockSpec((c, wx), lambda k: (rev(k), 0)), pl.BlockSpec((c, w2), lambda k: (rev(k), 0)),
                pl.BlockSpec((c, hm), lambda k: (rev(k), 0)),
                pl.BlockSpec((1, hm), lambda k: (0, 0)), pl.BlockSpec((1, hm), lambda k: (0, 0)),
                pl.BlockSpec((1, hm), lambda k: (0, 0)), pl.BlockSpec((1, w2), lambda k: (0, 0)),
                pl.BlockSpec((None, g, n, gw), lambda k: (rev(k), 0, 0, 0)),
                pl.BlockSpec((c, w2), lambda k: (rev(k), 0))]
    out_shape = [_sds((t, wx), F32), _sds((t, w2), BF16), _sds((t, hm), F32), _sds((1, hm), F32), _sds((1, hm), F32),
                 _sds((1, hm), F32), _sds((1, w2), F32)]
    out_specs = [pl.BlockSpec((c, wx), lambda k: (rev(k), 0)), pl.BlockSpec((c, w2), lambda k: (rev(k), 0)),
                 pl.BlockSpec((c, hm), lambda k: (rev(k), 0)),
                 pl.BlockSpec((1, hm), lambda k: (0, 0)), pl.BlockSpec((1, hm), lambda k: (0, 0)),
                 pl.BlockSpec((1, hm), lambda k: (0, 0)), pl.BlockSpec((1, w2), lambda k: (0, 0))]
    return _call(body, name="m2_chunks_bwd", out_shape=out_shape, grid=(nc,), in_specs=in_specs,
                 out_specs=out_specs, scratch_shapes=[pltpu.VMEM((g, n, gw), F32)],
                 semantics=("arbitrary",))(sxbc, z, dtraw, alog, dtb, dsk, nw, ssave, dy)


def _s5_scan(bu, a_l, cw, reverse, name, tb=256):
    t, wtot = bu.shape
    jn = wtot // cw
    half = cw // 2
    tb = _tile(t, tb, SUBLANES)
    nb = t // tb

    def body(bu_ref, a_ref, s_ref, st_ref):
        @pl.when(pl.program_id(1) == 0)
        def _():
            st_ref[...] = jnp.zeros_like(st_ref)

        are = a_ref[:, :half]
        aim = -a_ref[:, half:] if reverse else a_ref[:, half:]

        def step(k, carry):
            sre, sim = carry
            r = tb - 1 - k if reverse else k
            nre = are * sre - aim * sim + bu_ref[pl.ds(r, 1), pl.ds(0, half)]
            nim = are * sim + aim * sre + bu_ref[pl.ds(r, 1), pl.ds(half, half)]
            s_ref[pl.ds(r, 1), pl.ds(0, half)] = nre
            s_ref[pl.ds(r, 1), pl.ds(half, half)] = nim
            return nre, nim

        sre, sim = lax.fori_loop(0, tb, step, (st_ref[:, :half], st_ref[:, half:]), unroll=8)
        st_ref[:, :half] = sre
        st_ref[:, half:] = sim

    rb = (lambda i: nb - 1 - i) if reverse else (lambda i: i)
    return _call(body, name=name, out_shape=_sds((t, wtot), F32), grid=(jn, nb),
                 in_specs=[pl.BlockSpec((tb, cw), lambda j, i: (rb(i), j)), pl.BlockSpec((1, cw), lambda j, i: (0, j))],
                 out_specs=pl.BlockSpec((tb, cw), lambda j, i: (rb(i), j)),
                 scratch_shapes=[pltpu.VMEM((1, cw), F32)], semantics=("parallel", "arbitrary"))(bu, a_l)


def _s5_da(ds, s, cw, tb=256):
    t, wtot = ds.shape
    jn = wtot // cw
    half = cw // 2
    tb = _tile(t, tb, SUBLANES)
    nb = t // tb
    hb = tb // SUBLANES

    def body(ds_ref, s_ref, halo_ref, o_ref):
        i = pl.program_id(1)

        @pl.when(i == 0)
        def _():
            o_ref[...] = jnp.zeros_like(o_ref)

        prev = jnp.where(i == 0, 0.0, halo_ref[SUBLANES - 1:SUBLANES, :])
        row = lax.broadcasted_iota(jnp.int32, (tb, cw), 0)
        sh = jnp.where(row == 0, prev, pltpu.roll(s_ref[...], 1, 0))
        d = ds_ref[...]
        dre, dim, sre, sim = d[:, :half], d[:, half:], sh[:, :half], sh[:, half:]
        o_ref[:, :half] += jnp.sum(dre * sre + dim * sim, axis=0, keepdims=True)
        o_ref[:, half:] += jnp.sum(dim * sre - dre * sim, axis=0, keepdims=True)

    return _call(body, name="s5_da", out_shape=_sds((1, wtot), F32), grid=(jn, nb),
                 in_specs=[pl.BlockSpec((tb, cw), lambda j, i: (i, j)), pl.BlockSpec((tb, cw), lambda j, i: (i, j)),
                           pl.BlockSpec((SUBLANES, cw), lambda j, i: (jnp.maximum(i * hb - 1, 0), j))],
                 out_specs=pl.BlockSpec((1, cw), lambda j, i: (0, j)),
                 semantics=("parallel", "arbitrary"))(ds, s, s)


def _conv_rows(t, tb):
    tb = _tile(t, tb, SUBLANES * 2)
    return tb, t // tb, tb // SUBLANES


def _shift_down(x, above, s):
    n = x.shape[0]
    y = pltpu.roll(x, s, 0)
    row = lax.broadcasted_iota(jnp.int32, above.shape, 0)
    head = jnp.where(row < s, pltpu.roll(above, s, 0), y[:SUBLANES])
    return head if n == SUBLANES else jnp.concatenate([head, y[SUBLANES:]], axis=0)


def _shift_up(x, below, s):
    n = x.shape[0]
    y = pltpu.roll(x, n - s, 0)
    row = lax.broadcasted_iota(jnp.int32, below.shape, 0)
    tail = jnp.where(row >= SUBLANES - s, pltpu.roll(below, SUBLANES - s, 0), y[n - SUBLANES:])
    return tail if n == SUBLANES else jnp.concatenate([y[:n - SUBLANES], tail], axis=0)


def _conv_taps(x, above, w_ref, b_ref):
    xs = [x] + [_shift_down(x, above, s) for s in range(1, CONV_K)]
    c = b_ref[...] + w_ref[CONV_K - 1:CONV_K, :] * x
    for s in range(1, CONV_K):
        c = c + w_ref[CONV_K - 1 - s:CONV_K - s, :] * xs[s]
    return c, xs


def _conv_fwd(x, w, b, name, tb=256):
    t, cwid = x.shape
    tb, nb, hb = _conv_rows(t, tb)

    def body(x_ref, halo_ref, w_ref, b_ref, o_ref):
        above = jnp.where(pl.program_id(0) == 0, 0.0, halo_ref[...])
        o_ref[...] = _silu(_conv_taps(x_ref[...], above, w_ref, b_ref)[0])

    return _call(body, name=name, out_shape=_sds((t, cwid), F32), grid=(nb,),
                 in_specs=[pl.BlockSpec((tb, cwid), lambda i: (i, 0)),
                           pl.BlockSpec((SUBLANES, cwid), lambda i: (jnp.maximum(i * hb - 1, 0), 0)),
                           pl.BlockSpec((CONV_K, cwid), lambda i: (0, 0)), pl.BlockSpec((1, cwid), lambda i: (0, 0))],
                 out_specs=pl.BlockSpec((tb, cwid), lambda i: (i, 0)), semantics=("parallel",))(x, x, w, b)


def _dsilu(c):
    sg = _sigmoid(c)
    return sg * (1.0 + c * (1.0 - sg))


def _conv_bwd(x, w, b, ds, name, tb=256):
    t, cwid = x.shape
    tb, nb, hb = _conv_rows(t, tb)

    def body(x_ref, halo_ref, xn_ref, ds_ref, dsn_ref, w_ref, b_ref, dx_ref, dw_ref, db_ref):
        i = pl.program_id(0)

        @pl.when(i == 0)
        def _():
            dw_ref[...] = jnp.zeros_like(dw_ref)
            db_ref[...] = jnp.zeros_like(db_ref)

        x = x_ref[...]
        above = jnp.where(i == 0, 0.0, halo_ref[...])
        c, xs = _conv_taps(x, above, w_ref, b_ref)
        dc = ds_ref[...] * _dsilu(c)
        cn, _ = _conv_taps(xn_ref[...], x[tb - SUBLANES:], w_ref, b_ref)
        dcn = jnp.where(i == nb - 1, 0.0, dsn_ref[...] * _dsilu(cn))
        dx = w_ref[CONV_K - 1:CONV_K, :] * dc
        for s in range(1, CONV_K):
            dx = dx + w_ref[CONV_K - 1 - s:CONV_K - s, :] * _shift_up(dc, dcn, s)
        dx_ref[...] = dx.astype(dx_ref.dtype)
        for s in range(CONV_K):
            dw_ref[CONV_K - 1 - s:CONV_K - s, :] += jnp.sum(dc * xs[s], axis=0, keepdims=True)
        db_ref[...] += jnp.sum(dc, axis=0, keepdims=True)

    blk = pl.BlockSpec((tb, cwid), lambda i: (i, 0))
    prev = pl.BlockSpec((SUBLANES, cwid), lambda i: (jnp.maximum(i * hb - 1, 0), 0))
    nxt = pl.BlockSpec((SUBLANES, cwid), lambda i: (jnp.minimum((i + 1) * hb, nb * hb - 1), 0))
    return _call(body, name=name, out_shape=[_sds((t, cwid), BF16), _sds((CONV_K, cwid), F32), _sds((1, cwid), F32)],
                 grid=(nb,),
                 in_specs=[blk, prev, nxt, blk, nxt, pl.BlockSpec((CONV_K, cwid), lambda i: (0, 0)),
                           pl.BlockSpec((1, cwid), lambda i: (0, 0))],
                 out_specs=[blk, pl.BlockSpec((CONV_K, cwid), lambda i: (0, 0)), pl.BlockSpec((1, cwid), lambda i: (0, 0))],
                 semantics=("arbitrary",))(x, x, x, ds, ds, w, b)


def _f_rms(x, w):
    return (x * lax.rsqrt(jnp.mean(x * x, axis=-1, keepdims=True) + NORM_EPS) * w,)


def _f_s5_post1(ymm, u, d_l):
    return (_gelu_tanh(ymm + d_l * u),)


def _f_s5_post2(yg, tt, gate, b):
    return (yg * _sigmoid(tt + b) * _silu(gate),)


def _f_merge(ma, mb, mc, pa, pb, pc):
    return (_sigmoid(ma) * pa + _sigmoid(mb) * pb + _sigmoid(mc) * pc,)


def _loss_and_grad(x, tgt, fw, tb=256):
    t, dm = x.shape
    tb = _tile(t, tb, SUBLANES * 2)

    def f(xb, wb, tb_):
        y = _f_rms(xb, wb)[0]
        e = y - tb_
        return 0.5 * jnp.sum(jnp.mean(e * e, axis=-1, keepdims=True), axis=0, keepdims=True)

    def body(x_ref, t_ref, w_ref, loss_ref, dx_ref, dw_ref):
        @pl.when(pl.program_id(0) == 0)
        def _():
            loss_ref[...] = jnp.zeros_like(loss_ref)
            dw_ref[...] = jnp.zeros_like(dw_ref)

        tgt_b = t_ref[...]
        val, vjp = jax.vjp(lambda a, b: f(a, b, tgt_b), x_ref[...], w_ref[...])
        dxb, dwb = vjp(jnp.ones((1, 1), F32))
        loss_ref[...] += val
        dx_ref[...] = dxb
        dw_ref[...] += dwb

    return _call(body, name="loss_and_grad", out_shape=[_sds((1, 1), F32), _sds((t, dm), F32), _sds((1, dm), F32)],
                 grid=(t // tb,),
                 in_specs=[pl.BlockSpec((tb, dm), lambda i: (i, 0)), pl.BlockSpec((tb, dm), lambda i: (i, 0)),
                           pl.BlockSpec((1, dm), lambda i: (0, 0))],
                 out_specs=[pl.BlockSpec((1, 1), lambda i: (0, 0)), pl.BlockSpec((tb, dm), lambda i: (i, 0)),
                            pl.BlockSpec((1, dm), lambda i: (0, 0))],
                 semantics=("arbitrary",))(x, tgt, fw)


FLAT_W = 1024


def _sum_parts(parts, name, tb=256):
    n, r, wd = parts.shape
    tb = _tile(r, tb, SUBLANES)

    def body(p_ref, o_ref):
        acc = p_ref[0]
        for k in range(1, n):
            acc = acc + p_ref[k]
        o_ref[...] = acc

    return _call(body, name=name, out_shape=_sds((r, wd), F32), grid=(r // tb,),
                 in_specs=[pl.BlockSpec((n, tb, wd), lambda i: (0, i, 0))],
                 out_specs=pl.BlockSpec((tb, wd), lambda i: (i, 0)), semantics=("parallel",))(parts)


BLOCK_BYTES = 1 << 20


def _rows_per_block(r, wd):
    return _tile(r, max(SUBLANES * 2, BLOCK_BYTES // (4 * wd) // (SUBLANES * 2) * (SUBLANES * 2)), SUBLANES * 2)


def _add_my_half(g4, recv, c_idx, name):
    p, _, r, wd = g4.shape
    tb = _rows_per_block(r, wd)

    def body(c_ref, g_ref, r_ref, o_ref, ob_ref):
        s = g_ref[...] + r_ref[...]
        o_ref[...] = s
        ob_ref[...] = s.astype(BF16)

    spec = pl.BlockSpec((None, tb, wd), lambda j, i, c_ref: (j, i, 0))
    return _call(body, name=name, out_shape=[_sds((p, r, wd), F32), _sds((p, r, wd), BF16)], grid=(p, r // tb),
                 in_specs=[pl.BlockSpec((None, None, tb, wd), lambda j, i, c_ref: (j, c_ref[0], i, 0)), spec],
                 out_specs=[spec, spec], semantics=("parallel", "parallel"), num_scalar_prefetch=1)(c_idx, g4, recv)


def _sum_chips(own, got, me_idx, name):
    p, r, wd = own.shape
    tb = _rows_per_block(r, wd)

    def body(me_ref, own_ref, got_ref, o_ref):
        me = me_ref[0]
        acc = None
        for k in range(p):
            part = jnp.where(me == k, own_ref[...], got_ref[k].astype(F32))
            acc = part if acc is None else acc + part
        o_ref[...] = acc

    return _call(body, name=name, out_shape=_sds((r, wd), F32), grid=(r // tb,),
                 in_specs=[pl.BlockSpec((None, tb, wd), lambda i, me_ref: (me_ref[0], i, 0)),
                           pl.BlockSpec((p, tb, wd), lambda i, me_ref: (0, i, 0))],
                 out_specs=pl.BlockSpec((tb, wd), lambda i, me_ref: (i, 0)),
                 semantics=("parallel",), num_scalar_prefetch=1)(me_idx, own, got)


def _adamw(w, g, m, v, name):
    r, wd = w.shape
    tb = _rows_per_block(r, wd)

    def body(w_ref, g_ref, m_ref, v_ref, d_ref, nm_ref, nv_ref):
        gg = g_ref[...]
        nm = ADAM_B1 * m_ref[...] + (1.0 - ADAM_B1) * gg
        nv = ADAM_B2 * v_ref[...] + (1.0 - ADAM_B2) * (gg * gg)
        m_hat = nm / (1.0 - ADAM_B1 ** ADAM_STEP)
        v_hat = nv / (1.0 - ADAM_B2 ** ADAM_STEP)
        d_ref[...] = -ADAM_LR * (m_hat / (jnp.sqrt(v_hat) + ADAM_EPS) + ADAM_WD * w_ref[...])
        nm_ref[...] = nm
        nv_ref[...] = nv

    spec = pl.BlockSpec((tb, wd), lambda i: (i, 0))
    return _call(body, name=name, out_shape=[_sds((r, wd), F32)] * 3, grid=(r // tb,), in_specs=[spec] * 4,
                 out_specs=[spec] * 3, semantics=("parallel",))(w, g, m, v)


def _here():
    return lax.axis_index("x"), lax.axis_index("y"), lax.axis_index("c")


def _comm_call(body, name, out_shape, n_sems, operands):
    anyspec = pl.BlockSpec(memory_space=pl.ANY)
    outs = out_shape if isinstance(out_shape, (list, tuple)) else [out_shape]
    return _call(body, name=name, out_shape=out_shape, in_specs=[anyspec] * len(operands),
                 out_specs=[anyspec] * len(outs) if isinstance(out_shape, (list, tuple)) else anyspec,
                 scratch_shapes=[pltpu.SemaphoreType.DMA((n_sems,)), pltpu.SemaphoreType.DMA((n_sems,)),
                                 pltpu.SemaphoreType.DMA(())])(*operands)


def _gather_chips(x, name):
    def body(x_ref, o_ref, send_sems, recv_sems, local_sem):
        xi, yi, ci = _here()
        chips = [(1 - xi, yi), (xi, 1 - yi), (1 - xi, 1 - yi)]
        mine = pltpu.make_async_copy(x_ref, o_ref.at[2 * xi + yi], local_sem)
        mine.start()

        def copy(k, slot, to):
            return pltpu.make_async_remote_copy(src_ref=x_ref, dst_ref=o_ref.at[slot], send_sem=send_sems.at[k],
                                                recv_sem=recv_sems.at[k], device_id=to, device_id_type=MESH)

        sends = [copy(k, 2 * xi + yi, (px, py, ci)) for k, (px, py) in enumerate(chips)]
        for cp in sends:
            cp.start()
        for k, (px, py) in enumerate(chips):
            copy(k, 2 * px + py, (px, py, ci)).wait_recv()
        for cp in sends:
            cp.wait_send()
        mine.wait()

    return _comm_call(body, name, _sds((4,) + x.shape, x.dtype), 3, (x,))


def _gather_all(x, name):
    def body(x_ref, o_ref, send_sems, recv_sems, local_sem):
        xi, yi, ci = _here()
        me = 4 * xi + 2 * yi + ci
        flips = [(fx, fy, fc) for fx in (0, 1) for fy in (0, 1) for fc in (0, 1)][1:]
        peers = [((1 - xi) if fx else xi, (1 - yi) if fy else yi, (1 - ci) if fc else ci) for fx, fy, fc in flips]
        mine = pltpu.make_async_copy(x_ref, o_ref.at[me], local_sem)
        mine.start()

        def copy(k, slot, to):
            return pltpu.make_async_remote_copy(src_ref=x_ref, dst_ref=o_ref.at[slot], send_sem=send_sems.at[k],
                                                recv_sem=recv_sems.at[k], device_id=to, device_id_type=MESH)

        sends = [copy(k, me, p) for k, p in enumerate(peers)]
        for cp in sends:
            cp.start()
        for k, (px, py, pc) in enumerate(peers):
            copy(k, 4 * px + 2 * py + pc, (px, py, pc)).wait_recv()
        for cp in sends:
            cp.wait_send()
        mine.wait()

    return _comm_call(body, name, _sds((8,) + x.shape, x.dtype), 7, (x,))


def _multi_comm_call(body, name, out_shapes, n_sems, operands):
    anyspec = pl.BlockSpec(memory_space=pl.ANY)
    nin = len(operands)

    def flat_body(*refs):
        body(refs[:nin], refs[nin:nin + len(out_shapes)], refs[-2], refs[-1])

    return _call(flat_body, name=name, out_shape=list(out_shapes), in_specs=[anyspec] * nin,
                 out_specs=[anyspec] * len(out_shapes),
                 scratch_shapes=[pltpu.SemaphoreType.DMA((n_sems,)), pltpu.SemaphoreType.DMA((n_sems,))])(*operands)


def _remote(src, dst, send_sems, recv_sems, k, to):
    return pltpu.make_async_remote_copy(src_ref=src, dst_ref=dst, send_sem=send_sems.at[k], recv_sem=recv_sems.at[k],
                                        device_id=to, device_id_type=MESH)


def _gather_chips_split(xs, name):
    nw = len(xs)

    def body(x_refs, o_refs, send_sems, recv_sems):
        xi, yi, ci = _here()
        me = 2 * xi + yi
        sib = (xi, yi, 1 - ci)
        chips = [(1 - xi, yi), (xi, 1 - yi), (1 - xi, 1 - yi)]
        sends = []
        for i in range(nw):
            for k, (px, py) in enumerate(chips):
                sends.append(_remote(x_refs[i].at[ci], o_refs[i].at[me, ci], send_sems, recv_sems, 3 * i + k,
                                     (px, py, ci)))
        for cp in sends:
            cp.start()
        passed = []
        for i in range(nw):
            for k, (px, py) in enumerate(chips):
                landed = o_refs[i].at[2 * px + py, ci]
                _remote(landed, landed, send_sems, recv_sems, 3 * i + k, (px, py, ci)).wait_recv()
                fwd = _remote(landed, landed, send_sems, recv_sems, 3 * (nw + i) + k, sib)
                fwd.start()
                passed.append(fwd)
        for i in range(nw):
            for k, (px, py) in enumerate(chips):
                other = o_refs[i].at[2 * px + py, 1 - ci]
                _remote(other, other, send_sems, recv_sems, 3 * (nw + i) + k, sib).wait_recv()
        for cp in sends + passed:
            cp.wait_send()

    return _multi_comm_call(body, name, [_sds((4,) + x.shape, x.dtype) for x in xs], 6 * nw, xs)


def _swap_sibling_half(gs):
    def body(g_refs, o_refs, send_sems, recv_sems):
        xi, yi, ci = _here()
        cps = [_remote(g.at[:, 1 - ci], o, send_sems, recv_sems, i, (xi, yi, 1 - ci))
               for i, (g, o) in enumerate(zip(g_refs, o_refs))]
        for cp in cps:
            cp.start()
        for cp in cps:
            cp.wait()

    return _multi_comm_call(body, "swap_sibling_half", [_sds((g.shape[0],) + g.shape[2:], g.dtype) for g in gs],
                            len(gs), gs)


def _scatter_chips(gps):
    nw = len(gps)

    def body(g_refs, o_refs, send_sems, recv_sems):
        xi, yi, ci = _here()
        me = 2 * xi + yi
        chips = [(1 - xi, yi), (xi, 1 - yi), (1 - xi, 1 - yi)]
        sends = [_remote(g_refs[i].at[2 * px + py], o_refs[i].at[me], send_sems, recv_sems, 3 * i + k, (px, py, ci))
                 for i in range(nw) for k, (px, py) in enumerate(chips)]
        for cp in sends:
            cp.start()
        for i in range(nw):
            for k, (px, py) in enumerate(chips):
                slot = o_refs[i].at[2 * px + py]
                _remote(slot, slot, send_sems, recv_sems, 3 * i + k, (px, py, ci)).wait_recv()
        for cp in sends:
            cp.wait_send()

    return _multi_comm_call(body, "scatter_chips", [_sds(g.shape, g.dtype) for g in gps], 3 * nw, gps)


def _share_sibling(rs):
    def body(r_refs, o_refs, send_sems, recv_sems):
        xi, yi, ci = _here()
        cps = [_remote(r, o, send_sems, recv_sems, i, (xi, yi, 1 - ci)) for i, (r, o) in enumerate(zip(r_refs, o_refs))]
        for cp in cps:
            cp.start()
        for cp in cps:
            cp.wait()

    return _multi_comm_call(body, "share_sibling", [_sds(r.shape, r.dtype) for r in rs], len(rs), rs)


def _flat_pack(arrs, dtype, row_mult):
    flat = jnp.concatenate([a.astype(dtype).reshape(-1) for a in arrs])
    unit = FLAT_W * row_mult
    npad = -(-flat.shape[0] // unit) * unit
    return jnp.pad(flat, (0, npad - flat.shape[0])).reshape(npad // FLAT_W, FLAT_W)


def _flat_unpack(flat2d, shapes):
    flat = flat2d.reshape(-1)
    outs, off = [], 0
    for s in shapes:
        size = int(np.prod(s))
        outs.append(flat[off:off + size].reshape(s))
        off += size
    return outs


def _split_cols(a, widths):
    outs, off = [], 0
    for wd in widths:
        outs.append(lax.slice_in_dim(a, off, off + wd, axis=1))
        off += wd
    return outs


def _s5_params(lam_re, lam_im, log_step, b_re, b_im, c_re, c_im, d_skip):
    g, p = lam_re.shape
    hs = b_re.shape[2]
    gt = S5_GROUP_TILE
    jn = g // gt
    lam_re = jnp.minimum(lam_re, -1e-4)
    step = jnp.exp(log_step)[:, None]
    mag = jnp.exp(lam_re * step)
    ab_re = mag * jnp.cos(lam_im * step)
    ab_im = mag * jnp.sin(lam_im * step)
    den = lam_re * lam_re + lam_im * lam_im
    f_re = ((ab_re - 1.0) * lam_re + ab_im * lam_im) / den
    f_im = (ab_im * lam_re - (ab_re - 1.0) * lam_im) / den
    bb_re = f_re[..., None] * b_re - f_im[..., None] * b_im
    bb_im = f_re[..., None] * b_im + f_im[..., None] * b_re
    a_l = jnp.concatenate([ab_re.reshape(jn, gt * p), ab_im.reshape(jn, gt * p)], axis=1).reshape(1, jn * 2 * gt * p)
    eye = jnp.eye(gt, dtype=F32)

    def blockdiag(m):
        return jnp.einsum('jahp,ab->jahbp', m.reshape(jn, gt, hs, p), eye).reshape(jn, gt * hs, gt * p)

    b_blk = jnp.concatenate([blockdiag(bb_re.transpose(0, 2, 1)), blockdiag(bb_im.transpose(0, 2, 1))], axis=2)
    c_blk = jnp.concatenate([blockdiag(c_re), blockdiag(-c_im)], axis=2)
    return a_l, b_blk, c_blk, d_skip.reshape(1, g * hs)


def _layer_dims(p):
    d_model = p['w_out'].shape[1]
    wa = p['proj_a'].shape[0]
    h = p['gdn_a_log'].shape[0]
    wb = p['proj_b'].shape[0]
    wc = p['proj_c'].shape[0]
    hm = p['m2_a_log'].shape[0]
    cdim = p['m2_conv_w'].shape[1]
    width = dict(zip(PROJ_ORDER, (3 * wa, wa, h, h, wb, wb, wc, cdim, hm, d_model, d_model, d_model)))
    n_in = sum(width.values())
    return width, n_in, -(-n_in // LANES) * LANES


PROJ_ORDER = ('qkv', 'az', 'braw', 'araw', 'su', 'sgate', 'cz', 'cxbc', 'cdt', 'ma', 'mb', 'mc')
WORK_ORDER = ('qkv', 'az', 'su', 'sgate', 'cz', 'cxbc', 'ma', 'mb', 'mc', 'braw', 'araw', 'cdt')


def _reorder_cols(a, width, src_order, dst_order, n_out):
    off, o = {}, 0
    for n in src_order:
        off[n] = o
        o += width[n]
    parts = [lax.slice_in_dim(a, off[n], off[n] + width[n], axis=a.ndim - 1) for n in dst_order]
    used = sum(width[n] for n in dst_order)
    if n_out > used:
        parts.append(jnp.zeros(a.shape[:-1] + (n_out - used,), a.dtype))
    return jnp.concatenate(parts, axis=a.ndim - 1)


def _layer_fwd(x, p):
    width, n_in, n_pad = _layer_dims(p)
    sv = {'x': x}
    h = _rowwise(_f_rms, [x], [p['norm_w'][None]], [x.shape[1]], [BF16], "rms_fwd")[0]
    w_in = _reorder_cols(p['w_in'], width, PROJ_ORDER, WORK_ORDER, n_pad)
    proj = _matmul(h, w_in, 'nn', F32, "in_proj", tn=IN_PROJ_TILE)
    qkv, az, su, sgate, cz, cxbc, ma, mb, mc, braw, araw, cdt = _split_cols(proj, [width[n] for n in WORK_ORDER])
    sv.update(h=h, w_in=w_in, qkv=qkv, az=az, braw=braw, araw=araw, su=su, sgate=sgate, cz=cz, cxbc=cxbc, cdt=cdt,
              ma=ma, mb=mb, mc=mc)
    gb0 = jnp.zeros((1, qkv.shape[1]), F32)
    sqkv = _conv_fwd(qkv, p['gdn_conv_w'], gb0, "gdn_conv_fwd")
    ya, ssa, tsa = _gdn_chunks_fwd(sqkv, az, braw, araw, p['gdn_a_log'][None], p['gdn_dt_bias'][None],
                                   p['gdn_norm_w'][None])
    sv.update(sqkv=sqkv, ssa=ssa, tsa=tsa, ya=ya)
    s5_in = tuple(p[k] for k in ('s5_lam_re', 's5_lam_im', 's5_log_step', 's5_b_re', 's5_b_im', 's5_c_re', 's5_c_im',
                                 's5_d'))
    (a_l, b_blk, c_blk, d_l), s5_vjp = jax.vjp(_s5_params, *s5_in)
    cw = b_blk.shape[2]
    bu = _matmul_bd(su, b_blk, 'nn', F32, "s5_bu")
    s = _s5_scan(bu, a_l, cw, False, "s5_scan_fwd")
    ymm = _matmul_bd(s, c_blk, 'nt', F32, "s5_out")
    yg = _rowwise(_f_s5_post1, [ymm, su], [d_l], [su.shape[1]], [F32], "s5_post1_fwd")[0]
    tt = _matmul(yg, p['s5_glu_w'], 'nn', F32, "s5_glu")
    yb = _rowwise(_f_s5_post2, [yg, tt, sgate], [p['s5_glu_b'][None]], [su.shape[1]], [F32], "s5_post2_fwd")[0]
    sv.update(a_l=a_l, b_blk=b_blk, c_blk=c_blk, d_l=d_l, s5_vjp=s5_vjp, s=s, ymm=ymm, yg=yg, tt=tt, yb=yb, cw=cw)
    sxbc = _conv_fwd(cxbc, p['m2_conv_w'], p['m2_conv_b'][None], "m2_conv_fwd")
    yc, ssc = _m2_chunks_fwd(sxbc, cz, cdt, p['m2_a_log'][None], p['m2_dt_bias'][None], p['m2_d'][None],
                             p['m2_norm_w'][None])
    sv.update(sxbc=sxbc, ssc=ssc, yc=yc)
    pa = _matmul(ya, p['proj_a'], 'nn', F32, "proj_a")
    pb = _matmul(yb, p['proj_b'], 'nn', F32, "proj_b")
    pc = _matmul(yc, p['proj_c'], 'nn', F32, "proj_c")
    merged = _rowwise(_f_merge, [ma, mb, mc, pa, pb, pc], [], [x.shape[1]], [BF16], "merge_fwd", tb=128)[0]
    x_next = _matmul(merged, p['w_out'], 'nn', F32, "out_proj", add=x)
    sv.update(pa=pa, pb=pb, pc=pc, merged=merged)
    return x_next, sv


def _layer_bwd(dx_out, p, sv):
    width, n_in, n_pad = _layer_dims(p)
    g = {}
    dmerged = _matmul(dx_out, p['w_out'], 'nt', F32, "out_proj_dx")
    g['w_out'] = _matmul(sv['merged'], dx_out, 'tn', F32, "out_proj_dw")
    dma, dmb, dmc, dpa, dpb, dpc = _rowwise_bwd(
        _f_merge, [sv['ma'], sv['mb'], sv['mc'], sv['pa'], sv['pb'], sv['pc']], [], [dmerged], [BF16] * 6,
        "merge_bwd", tb=128)
    dya = _matmul(dpa, p['proj_a'], 'nt', F32, "proj_a_dx")
    dyb = _matmul(dpb, p['proj_b'], 'nt', F32, "proj_b_dx")
    dyc = _matmul(dpc, p['proj_c'], 'nt', F32, "proj_c_dx")
    g['proj_a'] = _matmul(sv['ya'], dpa, 'tn', F32, "proj_a_dw")
    g['proj_b'] = _matmul(sv['yb'], dpb, 'tn', F32, "proj_b_dw")
    g['proj_c'] = _matmul(sv['yc'], dpc, 'tn', F32, "proj_c_dw")
    alog, dtb, gnw = p['gdn_a_log'][None], p['gdn_dt_bias'][None], p['gdn_norm_w'][None]
    dq, dk, dv, daz, db3, da3, dalog, ddtb, dgnw = _gdn_chunks_bwd(sv['sqkv'], sv['az'], sv['braw'], sv['araw'], alog,
                                                                   dtb, gnw, sv['ssa'], sv['tsa'], dya)
    gb0 = jnp.zeros((1, sv['qkv'].shape[1]), F32)
    dqkv, g['gdn_conv_w'], _ = _conv_bwd(sv['qkv'], p['gdn_conv_w'], gb0, jnp.concatenate([dq, dk, dv], axis=1),
                                         "gdn_conv_bwd")
    dbraw, daraw = jnp.sum(db3, axis=0), jnp.sum(da3, axis=0)
    g.update(gdn_a_log=dalog[0], gdn_dt_bias=ddtb[0], gdn_norm_w=dgnw[0])
    dsx, dcz, dcdt, dmalog, dmdtb, dmdsk, dmnw = _m2_chunks_bwd(
        sv['sxbc'], sv['cz'], sv['cdt'], p['m2_a_log'][None], p['m2_dt_bias'][None], p['m2_d'][None],
        p['m2_norm_w'][None], sv['ssc'], dyc)
    dcxbc, g['m2_conv_w'], dconvb = _conv_bwd(sv['cxbc'], p['m2_conv_w'], p['m2_conv_b'][None], dsx, "m2_conv_bwd")
    g.update(m2_conv_b=dconvb[0], m2_a_log=dmalog[0], m2_dt_bias=dmdtb[0], m2_d=dmdsk[0], m2_norm_w=dmnw[0])
    dyg1, dtt, dsgate, dglub = _rowwise_bwd(_f_s5_post2, [sv['yg'], sv['tt'], sv['sgate']], [p['s5_glu_b'][None]],
                                            [dyb], [F32, BF16, BF16], "s5_post2_bwd")
    dyg = _matmul(dtt, p['s5_glu_w'], 'nt', F32, "s5_glu_dx", add=dyg1)
    g['s5_glu_w'] = _matmul(sv['yg'], dtt, 'tn', F32, "s5_glu_dw")
    g['s5_glu_b'] = dglub[0]
    dymm, dsu1, dd_l = _rowwise_bwd(_f_s5_post1, [sv['ymm'], sv['su']], [sv['d_l']], [dyg], [BF16, F32],
                                    "s5_post1_bwd")
    gy = _matmul_bd(dymm, sv['c_blk'], 'nn', F32, "s5_out_dx")
    ds = _s5_scan(gy, sv['a_l'], sv['cw'], True, "s5_scan_bwd")
    da_l = _s5_da(ds, sv['s'], sv['cw'])
    dsu = (dsu1 + _matmul_bd(ds, sv['b_blk'], 'nt', F32, "s5_bu_dx")).astype(BF16)
    ka = sv['b_blk'].shape[1]
    db_blk = _matmul_bd(sv['su'], ds, ('tn', ka, sv['cw']), F32, "s5_bu_dw")
    dc_blk = _matmul_bd(dymm, sv['s'], ('tn', ka, sv['cw']), F32, "s5_out_dw")
    for k, v in zip(('s5_lam_re', 's5_lam_im', 's5_log_step', 's5_b_re', 's5_b_im', 's5_c_re', 's5_c_im', 's5_d'),
                    sv['s5_vjp']((da_l, db_blk, dc_blk, dd_l))):
        g[k] = v
    dproj = jnp.concatenate([dqkv, daz, dsu, dsgate, dcz, dcxbc, dma, dmb, dmc, dbraw.astype(BF16), daraw.astype(BF16),
                             dcdt.astype(BF16), jnp.zeros((dqkv.shape[0], n_pad - n_in), BF16)], axis=1)
    dh = _matmul(dproj, sv['w_in'], 'nt', F32, "in_proj_dx", tm=2048, tk=IN_PROJ_TILE)
    g['w_in'] = _reorder_cols(_matmul(sv['h'], dproj, 'tn', F32, "in_proj_dw", tn=IN_PROJ_TILE), width, WORK_ORDER,
                              PROJ_ORDER, n_in)
    dx, dnw = _rowwise_bwd(_f_rms, [sv['x']], [p['norm_w'][None]], [dh], [F32], "rms_bwd", addend=dx_out)
    g['norm_w'] = dnw[0]
    return dx, g


INPUT_NAMES = (['x'] + WEIGHT_NAMES + ['loss_target'] + ['m_' + n for n in WEIGHT_NAMES]
               + ['v_' + n for n in WEIGHT_NAMES])


def _step(d):
    xi, yi, ci = _here()
    me = 2 * xi + yi
    depth = d['norm_w'].shape[0]
    big, ssm = list(BIG), list(SHARDED_SMALL)
    nsh = 4
    full, gathered = {}, {}
    halves = [d[n].astype(BF16).reshape(2, -1, d[n].shape[-1]) for n in big]
    for n, hv, got in zip(big, halves, _gather_chips_split(halves, "gather_weights")):
        gathered[n] = lax.dynamic_update_slice(got, hv[None], (me, 0, 0, 0)).reshape((nsh,) + d[n].shape)
    cg = _gather_chips(_flat_pack([d[n] for n in ssm], F32, 8), "gather_conv_weights")
    parts = [_flat_unpack(cg[j], [d[n].shape for n in ssm]) for j in range(nsh)]
    for i, n in enumerate(ssm):
        full[n] = jnp.concatenate([parts[j][i] for j in range(nsh)], axis=SHARDED_SMALL[n])
    layer_names = [n for n in WEIGHT_NAMES if n != 'final_norm_w']

    def layer_params(l):
        p = {n: (full[n][l] if n in full else d[n][l]) for n in layer_names if n not in BIG}
        for n in big:
            p[n] = jnp.concatenate([gathered[n][j, l] for j in range(nsh)], axis=BIG[n] - 1)
        return p

    x = d['x'][0]
    saved = []
    for l in range(depth):
        x, sv = _layer_fwd(x, layer_params(l))
        saved.append(sv)
    loss11, dx, dfw = _loss_and_grad(x, d['loss_target'][0], d['final_norm_w'][None])
    loss = lax.psum(loss11[0, 0], ("x", "y", "c"))
    grads = [None] * depth
    for l in reversed(range(depth)):
        dx, grads[l] = _layer_bwd(dx, layer_params(l), saved[l])
    gfull = {n: jnp.stack([grads[l][n] for l in range(depth)]) for n in layer_names if n not in BIG}
    gfull['final_norm_w'] = dfw[0]
    def shard(a, axis, j):
        wd = a.shape[axis] // nsh
        return lax.slice_in_dim(a, j * wd, (j + 1) * wd, axis=axis)

    c_idx = jnp.reshape(ci, (1,)).astype(jnp.int32)
    me_idx = jnp.reshape(me, (1,)).astype(jnp.int32)
    g4 = [jnp.stack([jnp.stack([shard(grads[l][n], BIG[n] - 1, j) for l in range(depth)]) for j in range(nsh)])
          .reshape(nsh, 2, -1, d[n].shape[-1]) for n in big]
    pairs = [_add_my_half(g, r, c_idx, "add_my_half_" + n) for n, g, r in zip(big, g4, _swap_sibling_half(g4))]
    got = _scatter_chips([pb for _, pb in pairs])
    mine = [_sum_chips(pf, gt, me_idx, "sum_chips_" + n) for n, (pf, _), gt in zip(big, pairs, got)]
    theirs = _share_sibling(mine)
    out = {}
    for n, mn, th in zip(big, mine, theirs):
        both = jnp.where(ci == 0, jnp.stack([mn, th]), jnp.stack([th, mn]))
        w2, m2, v2 = (d[pre + n].reshape(both.shape[0] * both.shape[1], both.shape[2]) for pre in ('', 'm_', 'v_'))
        g2 = both.reshape(w2.shape)
        dl, nm, nv = _adamw(w2, g2, m2, v2, "adamw_" + n)
        for key, arr in (('grad_', g2), ('delta_', dl), ('new_m_', nm), ('new_v_', nv)):
            out[key + n] = arr.reshape(d[n].shape)
    small = [n for n in WEIGHT_NAMES if n not in BIG]
    sshapes = [gfull[n].shape for n in small]
    gsm = _sum_parts(_gather_all(_flat_pack([gfull[n] for n in small], F32, 8), "gather_small_grads"), "sum_devices")
    gs = dict(zip(small, _flat_unpack(gsm, sshapes)))
    for n in ssm:
        wd = d[n].shape[SHARDED_SMALL[n]]
        gs[n] = lax.dynamic_slice_in_dim(gs[n], me * wd, wd, axis=SHARDED_SMALL[n])
    lshapes = [d[n].shape for n in small]
    wps, gps, mps, vps = (_flat_pack(arrs, F32, 16) for arrs in (
        [d[n] for n in small], [gs[n] for n in small], [d['m_' + n] for n in small], [d['v_' + n] for n in small]))
    dl, nm, nv = _adamw(wps, gps, mps, vps, "adamw_small")
    for key, arr in (('grad_', gps), ('delta_', dl), ('new_m_', nm), ('new_v_', nv)):
        for n, a in zip(small, _flat_unpack(arr, lshapes)):
            out[key + n] = a
    res = [loss, dx[None]]
    for key in ('grad_', 'delta_', 'new_m_', 'new_v_'):
        res += [out[key + n] for n in WEIGHT_NAMES]
    return tuple(res)


def kernel(x, norm_w, w_in, gdn_conv_w, gdn_a_log, gdn_dt_bias, gdn_norm_w, s5_lam_re, s5_lam_im, s5_log_step, s5_b_re, s5_b_im, s5_c_re, s5_c_im, s5_d, s5_glu_w, s5_glu_b, m2_conv_w, m2_conv_b, m2_a_log, m2_dt_bias, m2_d, m2_norm_w, proj_a, proj_b, proj_c, w_out, final_norm_w, loss_target, m_norm_w, m_w_in, m_gdn_conv_w, m_gdn_a_log, m_gdn_dt_bias, m_gdn_norm_w, m_s5_lam_re, m_s5_lam_im, m_s5_log_step, m_s5_b_re, m_s5_b_im, m_s5_c_re, m_s5_c_im, m_s5_d, m_s5_glu_w, m_s5_glu_b, m_m2_conv_w, m_m2_conv_b, m_m2_a_log, m_m2_dt_bias, m_m2_d, m_m2_norm_w, m_proj_a, m_proj_b, m_proj_c, m_w_out, m_final_norm_w, v_norm_w, v_w_in, v_gdn_conv_w, v_gdn_a_log, v_gdn_dt_bias, v_gdn_norm_w, v_s5_lam_re, v_s5_lam_im, v_s5_log_step, v_s5_b_re, v_s5_b_im, v_s5_c_re, v_s5_c_im, v_s5_d, v_s5_glu_w, v_s5_glu_b, v_m2_conv_w, v_m2_conv_b, v_m2_a_log, v_m2_dt_bias, v_m2_d, v_m2_norm_w, v_proj_a, v_proj_b, v_proj_c, v_w_out, v_final_norm_w):
    args = (x, norm_w, w_in, gdn_conv_w, gdn_a_log, gdn_dt_bias, gdn_norm_w, s5_lam_re, s5_lam_im, s5_log_step, s5_b_re, s5_b_im, s5_c_re, s5_c_im, s5_d, s5_glu_w, s5_glu_b, m2_conv_w, m2_conv_b, m2_a_log, m2_dt_bias, m2_d, m2_norm_w, proj_a, proj_b, proj_c, w_out, final_norm_w, loss_target, m_norm_w, m_w_in, m_gdn_conv_w, m_gdn_a_log, m_gdn_dt_bias, m_gdn_norm_w, m_s5_lam_re, m_s5_lam_im, m_s5_log_step, m_s5_b_re, m_s5_b_im, m_s5_c_re, m_s5_c_im, m_s5_d, m_s5_glu_w, m_s5_glu_b, m_m2_conv_w, m_m2_conv_b, m_m2_a_log, m_m2_dt_bias, m_m2_d, m_m2_norm_w, m_proj_a, m_proj_b, m_proj_c, m_w_out, m_final_norm_w, v_norm_w, v_w_in, v_gdn_conv_w, v_gdn_a_log, v_gdn_dt_bias, v_gdn_norm_w, v_s5_lam_re, v_s5_lam_im, v_s5_log_step, v_s5_b_re, v_s5_b_im, v_s5_c_re, v_s5_c_im, v_s5_d, v_s5_glu_w, v_s5_glu_b, v_m2_conv_w, v_m2_conv_b, v_m2_a_log, v_m2_dt_bias, v_m2_d, v_m2_norm_w, v_proj_a, v_proj_b, v_proj_c, v_w_out, v_final_norm_w)
    return _step(dict(zip(INPUT_NAMES, args)))
```

```python
import functools
import math
from typing import NamedTuple

import jax
import jax.numpy as jnp
import numpy as np
from jax import lax
from jax.experimental import pallas as pl
from jax.experimental.pallas import tpu as pltpu

F32 = jnp.float32
BF16 = jnp.bfloat16
HI = lax.Precision.HIGH
MESH = pl.DeviceIdType.MESH

CHUNK = 64
CONV_K = 4
NORM_EPS = 1e-6
GDN_HEAD_DIM = 128
M2_HEAD_DIM = 64
M2_STATE = 128
M2_GROUPS = 4
S5_GROUP_TILE = 8
ADAM_LR = 0.001
ADAM_B1 = 0.9
ADAM_B2 = 0.999
ADAM_EPS = 1e-08
ADAM_WD = 0.01
ADAM_STEP = 10
LANES = 128
SUBLANES = 8
VMEM_LIMIT_BYTES = 56 * 1024 * 1024

WEIGHT_NAMES = ['norm_w', 'w_in', 'gdn_conv_w', 'gdn_a_log', 'gdn_dt_bias', 'gdn_norm_w', 's5_lam_re', 's5_lam_im',
                's5_log_step', 's5_b_re', 's5_b_im', 's5_c_re', 's5_c_im', 's5_d', 's5_glu_w', 's5_glu_b',
                'm2_conv_w', 'm2_conv_b', 'm2_a_log', 'm2_dt_bias', 'm2_d', 'm2_norm_w', 'proj_a', 'proj_b',
                'proj_c', 'w_out', 'final_norm_w']
BIG = {'w_in': 2, 'proj_a': 2, 'proj_b': 2, 'proj_c': 2, 'w_out': 1, 's5_glu_w': 1}
SHARDED_SMALL = {'gdn_conv_w': 2, 'm2_conv_w': 2}


def _call(body, *, name, out_shape, grid=None, in_specs=None, out_specs=None, scratch_shapes=(), semantics=None,
          num_scalar_prefetch=None):
    params = dict(vmem_limit_bytes=VMEM_LIMIT_BYTES)
    if semantics is not None:
        params['dimension_semantics'] = semantics
    kw = dict(name=name, out_shape=out_shape, compiler_params=pltpu.CompilerParams(**params))
    if num_scalar_prefetch is not None:
        kw['grid_spec'] = pltpu.PrefetchScalarGridSpec(num_scalar_prefetch=num_scalar_prefetch, grid=grid,
                                                       in_specs=in_specs, out_specs=out_specs,
                                                       scratch_shapes=scratch_shapes)
    else:
        if grid is not None:
            kw['grid'] = grid
        if in_specs is not None:
            kw['in_specs'] = in_specs
        if out_specs is not None:
            kw['out_specs'] = out_specs
        if scratch_shapes:
            kw['scratch_shapes'] = scratch_shapes
    return pl.pallas_call(body, **kw)


def _tile(n, target, unit):
    if n <= target:
        return n
    t = (target // unit) * unit
    while t >= unit:
        if n % t == 0:
            return t
        t -= unit
    raise ValueError(f"no tile for {n} (unit {unit}, target {target})")


def _sds(shape, dtype):
    return jax.ShapeDtypeStruct(tuple(shape), dtype)


def _sigmoid(x):
    return jax.nn.sigmoid(x)


def _silu(x):
    return x * jax.nn.sigmoid(x)


def _softplus(x):
    return jnp.maximum(x, 0.0) + jnp.log(1.0 + jnp.exp(-jnp.abs(x)))


def _gelu_tanh(x):
    return 0.5 * x * (1.0 + jnp.tanh(math.sqrt(2.0 / math.pi) * (x + 0.044715 * (x * x * x))))


def _dot(a, b, dims, prec=None):
    return lax.dot_general(a, b, (dims, ((), ())), precision=prec, preferred_element_type=F32)


def _nn(a, b, prec=None):
    return _dot(a, b, ((1,), (0,)), prec)


def _nt(a, b, prec=None):
    return _dot(a, b, ((1,), (1,)), prec)


def _tn(a, b, prec=None):
    return _dot(a, b, ((0,), (0,)), prec)


IN_PROJ_TILE = 1664
MATMUL_TILES = {'nn': (1024, 1024, 2048), 'nt': (1024, 1024, 2048), 'tn': (1024, 1024, 2048)}


def _matmul(a, b, mode, out_dtype, name, tm=None, tn=None, tk=None, add=None):
    tm, tn, tk = (t if t is not None else dflt for t, dflt in zip((tm, tn, tk), MATMUL_TILES[mode]))
    if mode == 'nn':
        (m, k), (k2, n) = a.shape, b.shape
    elif mode == 'nt':
        (m, k), (n, k2) = a.shape, b.shape
    else:
        (k, m), (k2, n) = a.shape, b.shape
    assert k == k2, (a.shape, b.shape, mode)
    tm = _tile(m, tm, LANES if mode == 'tn' else SUBLANES)
    tn = _tile(n, tn, LANES)
    tk = _tile(k, tk, LANES if mode != 'tn' else SUBLANES * 2)
    nk = k // tk
    dims = {'nn': ((1,), (0,)), 'nt': ((1,), (1,)), 'tn': ((0,), (0,))}[mode]

    def body(*refs):
        a_ref, b_ref = refs[:2]
        o_ref, acc_ref = refs[-2:]
        kk = pl.program_id(2)

        @pl.when(kk == 0)
        def _():
            acc_ref[...] = jnp.zeros_like(acc_ref)

        acc_ref[...] += _dot(a_ref[...].astype(BF16), b_ref[...].astype(BF16), dims)

        @pl.when(kk == nk - 1)
        def _():
            res = acc_ref[...]
            if add is not None:
                res = res + refs[2][...].astype(F32)
            o_ref[...] = res.astype(o_ref.dtype)

    a_spec = pl.BlockSpec((tk, tm), lambda i, j, kk: (kk, i)) if mode == 'tn' else pl.BlockSpec((tm, tk), lambda i, j, kk: (i, kk))
    b_spec = pl.BlockSpec((tn, tk), lambda i, j, kk: (j, kk)) if mode == 'nt' else pl.BlockSpec((tk, tn), lambda i, j, kk: (kk, j))
    o_spec = pl.BlockSpec((tm, tn), lambda i, j, kk: (i, j))
    ops = (a, b) if add is None else (a, b, add)
    return _call(body, name=name, out_shape=_sds((m, n), out_dtype), grid=(m // tm, n // tn, nk),
                 in_specs=[a_spec, b_spec] + ([] if add is None else [o_spec]), out_specs=o_spec,
                 scratch_shapes=[pltpu.VMEM((tm, tn), F32)], semantics=("parallel", "parallel", "arbitrary"))(*ops)


class Win(NamedTuple):
    arr: jax.Array
    off: int
    width: int

    @property
    def shape(self):
        return (self.arr.shape[0], self.width)


def _win(x):
    return x if isinstance(x, Win) else Win(x, 0, x.shape[1])


def _col_tile(wins, target):
    ct = (min(target, min(w.width for w in wins)) // LANES) * LANES
    while ct > LANES and any(w.width % ct or w.off % ct for w in wins):
        ct -= LANES
    assert all(w.width % ct == 0 and w.off % ct == 0 for w in wins), [(w.off, w.width) for w in wins]
    return ct


def _wspec(rows, ct, w, row_first=True):
    base = w.off // ct
    if row_first:
        return pl.BlockSpec((rows, ct), lambda i, j: (i, base + j))
    return pl.BlockSpec((rows, ct), lambda j, i: (i, base + j))


def _matmul_bd(a, b, mode, out_dtype, name, tm=1024, add=None):
    a = _win(a)
    if mode in ('nn', 'nt'):
        t = a.shape[0]
        jn, ka, nb = b.shape
        tm = _tile(t, tm, SUBLANES)
        win, wout = (ka, nb) if mode == 'nn' else (nb, ka)
        assert a.shape[1] == jn * win and a.off % win == 0
        abase = a.off // win

        def body_add(a_ref, b_ref, c_ref, o_ref):
            o_ref[...] = (_nt(a_ref[...].astype(BF16), b_ref[...].astype(BF16)) + c_ref[...].astype(F32)).astype(o_ref.dtype)

        if add is not None:
            assert mode == 'nt'
            return _call(body_add, name=name, out_shape=_sds((t, jn * wout), out_dtype), grid=(t // tm, jn),
                         in_specs=[pl.BlockSpec((tm, win), lambda i, j: (i, abase + j)),
                                   pl.BlockSpec((None, ka, nb), lambda i, j: (j, 0, 0)),
                                   pl.BlockSpec((tm, wout), lambda i, j: (i, j))],
                         out_specs=pl.BlockSpec((tm, wout), lambda i, j: (i, j)),
                         semantics=("parallel", "parallel"))(a.arr, b, add)

        def body(a_ref, b_ref, o_ref):
            if mode == 'nn':
                o_ref[...] = _nn(a_ref[...].astype(BF16), b_ref[...].astype(BF16)).astype(o_ref.dtype)
            else:
                o_ref[...] = _nt(a_ref[...].astype(BF16), b_ref[...].astype(BF16)).astype(o_ref.dtype)

        return _call(body, name=name, out_shape=_sds((t, jn * wout), out_dtype), grid=(t // tm, jn),
                     in_specs=[pl.BlockSpec((tm, win), lambda i, j: (i, abase + j)),
                               pl.BlockSpec((None, ka, nb), lambda i, j: (j, 0, 0))],
                     out_specs=pl.BlockSpec((tm, wout), lambda i, j: (i, j)),
                     semantics=("parallel", "parallel"))(a.arr, b)
    t = a.shape[0]
    ka, nb = mode[1], mode[2]
    jn = a.shape[1] // ka
    assert b.shape[1] == jn * nb and a.off % ka == 0
    abase = a.off // ka
    tk = _tile(t, tm, SUBLANES * 2)
    nk = t // tk

    def body_tn(a_ref, b_ref, o_ref):
        @pl.when(pl.program_id(1) == 0)
        def _():
            o_ref[...] = jnp.zeros_like(o_ref)

        o_ref[...] += _tn(a_ref[...].astype(BF16), b_ref[...].astype(BF16))

    return _call(body_tn, name=name, out_shape=_sds((jn, ka, nb), F32), grid=(jn, nk),
                 in_specs=[pl.BlockSpec((tk, ka), lambda j, kk: (kk, abase + j)),
                           pl.BlockSpec((tk, nb), lambda j, kk: (kk, j))],
                 out_specs=pl.BlockSpec((None, ka, nb), lambda j, kk: (j, 0, 0)),
                 semantics=("parallel", "arbitrary"))(a.arr, b)


def _rowwise_tiles(rows, tb, col_tile):
    rows = [_win(r) for r in rows]
    t = rows[0].shape[0]
    tb = _tile(t, tb, SUBLANES * 2)
    if col_tile is None:
        assert all(r.off % r.width == 0 for r in rows)
        return rows, tb, None, 1
    ct = _col_tile(rows, col_tile)
    return rows, tb, ct, rows[0].width // ct


def _rowwise(fn, rows, params, out_widths, out_dtypes, name, tb=256, col_tile=None):
    rows, tb, ct, ncol = _rowwise_tiles(rows, tb, col_tile)
    t = rows[0].shape[0]
    nr, npar = len(rows), len(params)

    def body(*refs):
        ins = [r[...].astype(F32) for r in refs[:nr + npar]]
        outs = fn(*ins)
        for o_ref, o in zip(refs[nr + npar:], outs):
            o_ref[...] = o.astype(o_ref.dtype)

    in_specs = [_wspec(tb, ct or r.width, r) for r in rows]
    in_specs += [pl.BlockSpec((1, ct or p.shape[1]), lambda i, j: (0, j)) for p in params]
    out_shape = [_sds((t, w), d) for w, d in zip(out_widths, out_dtypes)]
    out_specs = [pl.BlockSpec((tb, ct or w), lambda i, j: (i, j)) for w in out_widths]
    return _call(body, name=name, out_shape=out_shape, grid=(t // tb, ncol), in_specs=in_specs, out_specs=out_specs,
                 semantics=("parallel", "parallel"))(*[r.arr for r in rows], *params)


def _rowwise_bwd(fn, rows, params, cts, row_grad_dtypes, name, tb=256, addend=None, col_tile=None):
    rows, tb, ct, ncol = _rowwise_tiles(rows, tb, col_tile)
    t = rows[0].shape[0]
    nr, npar, nc = len(rows), len(params), len(cts)
    keep = [i for i, d in enumerate(row_grad_dtypes) if d is not None]
    nadd = 0 if addend is None else 1

    def body(*refs):
        ins = [r[...].astype(F32) for r in refs[:nr + npar]]
        ct = [r[...].astype(F32) for r in refs[nr + npar:nr + npar + nc]]
        _, vjp = jax.vjp(fn, *ins)
        grads = vjp(tuple(ct))
        out_refs = refs[nr + npar + nc + nadd:]
        for o_ref, i in zip(out_refs[:len(keep)], keep):
            g = grads[i]
            if nadd and i == 0:
                g = g + refs[nr + npar + nc][...].astype(F32)
            o_ref[...] = g.astype(o_ref.dtype)

        @pl.when(pl.program_id(1) == 0)
        def _():
            for o_ref in out_refs[len(keep):]:
                o_ref[...] = jnp.zeros_like(o_ref)

        for o_ref, g in zip(out_refs[len(keep):], grads[nr:]):
            o_ref[...] += g

    def plain(w):
        return pl.BlockSpec((tb, ct or w), lambda j, i: (i, j))

    in_specs = [_wspec(tb, ct or r.width, r, row_first=False) for r in rows]
    in_specs += [pl.BlockSpec((1, ct or p.shape[1]), lambda j, i: (0, j)) for p in params]
    in_specs += [plain(c.shape[1]) for c in cts]
    extra = []
    if nadd:
        in_specs += [plain(addend.shape[1])]
        extra = [addend]
    out_shape = [_sds(rows[i].shape, row_grad_dtypes[i]) for i in keep] + [_sds(p.shape, F32) for p in params]
    out_specs = [plain(rows[i].width) for i in keep]
    out_specs += [pl.BlockSpec((1, ct or p.shape[1]), lambda j, i: (0, j)) for p in params]
    return _call(body, name=name, out_shape=out_shape, grid=(ncol, t // tb), in_specs=in_specs, out_specs=out_specs,
                 semantics=("parallel", "arbitrary"))(*[r.arr for r in rows], *params, *cts, *extra)


def _chunk_masks(c):
    row = lax.broadcasted_iota(jnp.int32, (c, c), 0)
    col = lax.broadcasted_iota(jnp.int32, (c, c), 1)
    causal = row >= col
    strict = row > col
    return causal, strict, causal.astype(F32), (row > col).astype(F32), (row == col).astype(F32)


def _lane_pick(blk, idx):
    lane = lax.broadcasted_iota(jnp.int32, blk.shape, 1)
    return jnp.sum(jnp.where(lane == idx, blk, 0.0), axis=1, keepdims=True)


def _bdot(a, b, ca, cb, prec=None):
    return lax.dot_general(a, b, (((ca,), (cb,)), ((0,), (0,))), precision=prec, preferred_element_type=F32)


def _bnn(a, b, prec=None):
    return _bdot(a, b, 2, 1, prec)


def _bnt(a, b, prec=None):
    return _bdot(a, b, 2, 2, prec)


def _btn(a, b, prec=None):
    return _bdot(a, b, 1, 1, prec)


def _unit_lower_inverse(a_mat, eye):
    x = -a_mat
    t_inv = eye + x
    p = x
    for _ in range(int(math.log2(a_mat.shape[-1])) - 1):
        p = _bnn(p, p, HI)
        t_inv = t_inv + _bnn(t_inv, p, HI)
    return t_inv


@jax.custom_vjp
def _saved_inverse(a_mat, t_saved):
    return t_saved


def _saved_inverse_fwd(a_mat, t_saved):
    return t_saved, t_saved


def _saved_inverse_bwd(t_inv, ct):
    return -_bnt(_btn(t_inv, ct, HI), t_inv, HI), jnp.zeros_like(t_inv)


_saved_inverse.defvjp(_saved_inverse_fwd, _saved_inverse_bwd)


def _gdn_heads(q, k, v, z, braw, araw, alog, dtb, nw, s_in, t_saved=None):
    b, c, d = q.shape
    causal, strict, lower, upper_t, eye = _chunk_masks(c)
    lower_b = jnp.broadcast_to(lower[None], (b, c, c))
    qn = q * lax.rsqrt(jnp.sum(q * q, axis=-1, keepdims=True) + NORM_EPS) * (d ** -0.5)
    kn = k * lax.rsqrt(jnp.sum(k * k, axis=-1, keepdims=True) + NORM_EPS)
    beta = _sigmoid(braw)
    g = -jnp.exp(alog) * _softplus(araw + dtb)
    dlog = _bnn(lower_b, g * upper_t[None], HI)
    dm = jnp.where(causal[None], jnp.exp(dlog), 0.0)
    g_lanes = jnp.broadcast_to(g, (b, c, d))
    gc = _bnn(lower_b, g_lanes, HI)
    gl = jnp.sum(g_lanes, axis=1, keepdims=True)
    eg = jnp.exp(gc)
    kb = kn * beta
    a_mat = jnp.where(strict[None], _bnt(kb, kn) * dm, 0.0)
    t_inv = _unit_lower_inverse(a_mat, eye[None]) if t_saved is None else _saved_inverse(a_mat, t_saved)
    r = beta * (v - eg * _bnn(kn, s_in))
    v_new = _bnn(t_inv, r)
    qk = _bnt(qn, kn) * dm
    out = eg * _bnn(qn, s_in) + _bnn(qk, v_new)
    k_tail = kn * jnp.exp(gl - gc)
    s_out = s_in * jnp.exp(gl) + _btn(k_tail, v_new)
    y = out * lax.rsqrt(jnp.mean(out * out, axis=-1, keepdims=True) + NORM_EPS) * nw[None] * _silu(z)
    if t_saved is None:
        return y, s_out, t_inv
    return y, s_out


def _gdn_stack(refs, hb, d, h, first_head):
    q_ref, k_ref, v_ref, z_ref, b_ref, a_ref, alog_ref, dtb_ref = refs
    sls = [slice(i * d, (i + 1) * d) for i in range(hb)]
    heads = [first_head + i for i in range(hb)]
    wide = [jnp.stack([r[:, sl] for sl in sls]) for r in (q_ref, k_ref, v_ref, z_ref)]
    cols = [jnp.stack([_lane_pick(r[...], hd) for hd in heads]) for r in (b_ref, a_ref, alog_ref, dtb_ref)]
    return wide + cols


GDN_HEADS_PER_STEP = 8


def _gdn_chunks_fwd(sqkv, z, braw, araw, alog, dtb, nw, hb=GDN_HEADS_PER_STEP):
    t, w3 = sqkv.shape
    w = w3 // 3
    d = GDN_HEAD_DIM
    h = w // d
    hb = min(hb, h)
    hg = h // hb
    c = CHUNK
    nc = t // c

    def body(q_ref, k_ref, v_ref, z_ref, b_ref, a_ref, alog_ref, dtb_ref, nw_ref, y_ref, ssave_ref, tsave_ref,
             s_ref):
        @pl.when(pl.program_id(1) == 0)
        def _():
            s_ref[...] = jnp.zeros_like(s_ref)

        s_in = s_ref[...]
        ssave_ref[...] = s_in
        args = _gdn_stack((q_ref, k_ref, v_ref, z_ref, b_ref, a_ref, alog_ref, dtb_ref), hb, d, h,
                          pl.program_id(0) * hb)
        y, s_out, t_inv = _gdn_heads(*args, nw_ref[...], s_in)
        for i in range(hb):
            y_ref[:, i * d:(i + 1) * d] = y[i]
        s_ref[...] = s_out
        tsave_ref[...] = t_inv

    blk = (c, hb * d)
    z = _win(z)
    zb = z.off // (hb * d)
    assert z.off % (hb * d) == 0
    in_specs = [pl.BlockSpec(blk, lambda g, n: (n, g)), pl.BlockSpec(blk, lambda g, n: (n, hg + g)),
                pl.BlockSpec(blk, lambda g, n: (n, 2 * hg + g)), pl.BlockSpec(blk, lambda g, n: (n, zb + g)),
                pl.BlockSpec((c, h), lambda g, n: (n, 0)), pl.BlockSpec((c, h), lambda g, n: (n, 0)),
                pl.BlockSpec((1, h), lambda g, n: (0, 0)), pl.BlockSpec((1, h), lambda g, n: (0, 0)),
                pl.BlockSpec((1, d), lambda g, n: (0, 0))]
    out_shape = [_sds((t, w), F32), _sds((hg, nc, hb, d, d), F32), _sds((hg, nc, hb, c, c), F32)]
    out_specs = [pl.BlockSpec(blk, lambda g, n: (n, g)),
                 pl.BlockSpec((None, None, hb, d, d), lambda g, n: (g, n, 0, 0, 0)),
                 pl.BlockSpec((None, None, hb, c, c), lambda g, n: (g, n, 0, 0, 0))]
    return _call(body, name="gdn_chunks_fwd", out_shape=out_shape, grid=(hg, nc), in_specs=in_specs,
                 out_specs=out_specs, scratch_shapes=[pltpu.VMEM((hb, d, d), F32)],
                 semantics=("parallel", "arbitrary"))(sqkv, sqkv, sqkv, z.arr, braw, araw, alog, dtb, nw)


def _gdn_chunks_bwd(sqkv, z, braw, araw, alog, dtb, nw, ssave, tsave, dy, hb=GDN_HEADS_PER_STEP):
    t, w3 = sqkv.shape
    w = w3 // 3
    d = GDN_HEAD_DIM
    h = w // d
    hb = min(hb, h)
    hg = h // hb
    c = CHUNK
    nc = t // c

    def body(q_ref, k_ref, v_ref, z_ref, b_ref, a_ref, alog_ref, dtb_ref, nw_ref, ssave_ref, tsave_ref, dy_ref,
             dq_ref, dk_ref, dv_ref, dz_ref, db_ref, da_ref, dalog_ref, ddtb_ref, dnw_ref, ds_ref):
        first = jnp.logical_and(pl.program_id(0) == 0, pl.program_id(1) == 0)

        @pl.when(pl.program_id(1) == 0)
        def _():
            ds_ref[...] = jnp.zeros_like(ds_ref)

        @pl.when(first)
        def _():
            dalog_ref[...] = jnp.zeros_like(dalog_ref)
            ddtb_ref[...] = jnp.zeros_like(ddtb_ref)
            dnw_ref[...] = jnp.zeros_like(dnw_ref)

        lane_h = lax.broadcasted_iota(jnp.int32, (1, h), 1)
        db_acc = jnp.zeros((c, h), F32)
        da_acc = jnp.zeros((c, h), F32)
        args = _gdn_stack((q_ref, k_ref, v_ref, z_ref, b_ref, a_ref, alog_ref, dtb_ref), hb, d, h,
                          pl.program_id(0) * hb)
        t_saved = tsave_ref[...]
        _, vjp = jax.vjp(lambda *a: _gdn_heads(*a, t_saved=t_saved), *args, nw_ref[...], ssave_ref[...])
        dyb = jnp.stack([dy_ref[:, i * d:(i + 1) * d] for i in range(hb)])
        dq, dk, dv, dz, db, da, dalog, ddtb, dnw, ds_in = vjp((dyb, ds_ref[...]))
        for i in range(hb):
            sl = slice(i * d, (i + 1) * d)
            dq_ref[:, sl] = dq[i]
            dk_ref[:, sl] = dk[i]
            dv_ref[:, sl] = dv[i]
            dz_ref[:, sl] = dz[i].astype(dz_ref.dtype)
            onehot = (lane_h == pl.program_id(0) * hb + i).astype(F32)
            db_acc = db_acc + db[i] * onehot
            da_acc = da_acc + da[i] * onehot
            dalog_ref[...] += dalog[i] * onehot
            ddtb_ref[...] += ddtb[i] * onehot
        dnw_ref[...] += dnw
        ds_ref[...] = ds_in
        db_ref[...] = db_acc
        da_ref[...] = da_acc

    blk = (c, hb * d)
    rev = lambda n: nc - 1 - n
    z = _win(z)
    zb = z.off // (hb * d)
    assert z.off % (hb * d) == 0
    in_specs = [pl.BlockSpec(blk, lambda g, n: (rev(n), g)), pl.BlockSpec(blk, lambda g, n: (rev(n), hg + g)),
                pl.BlockSpec(blk, lambda g, n: (rev(n), 2 * hg + g)), pl.BlockSpec(blk, lambda g, n: (rev(n), zb + g)),
                pl.BlockSpec((c, h), lambda g, n: (rev(n), 0)), pl.BlockSpec((c, h), lambda g, n: (rev(n), 0)),
                pl.BlockSpec((1, h), lambda g, n: (0, 0)), pl.BlockSpec((1, h), lambda g, n: (0, 0)),
                pl.BlockSpec((1, d), lambda g, n: (0, 0)),
                pl.BlockSpec((None, None, hb, d, d), lambda g, n: (g, rev(n), 0, 0, 0)),
                pl.BlockSpec((None, None, hb, c, c), lambda g, n: (g, rev(n), 0, 0, 0)),
                pl.BlockSpec(blk, lambda g, n: (rev(n), g))]
    out_shape = [_sds((t, w), F32), _sds((t, w), F32), _sds((t, w), F32), _sds((t, w), BF16),
                 _sds((hg, t, h), F32), _sds((hg, t, h), F32), _sds((1, h), F32), _sds((1, h), F32), _sds((1, d), F32)]
    out_specs = [pl.BlockSpec(blk, lambda g, n: (rev(n), g))] * 4
    out_specs += [pl.BlockSpec((None, c, h), lambda g, n: (g, rev(n), 0))] * 2
    out_specs += [pl.BlockSpec((1, h), lambda g, n: (0, 0)), pl.BlockSpec((1, h), lambda g, n: (0, 0)),
                  pl.BlockSpec((1, d), lambda g, n: (0, 0))]
    return _call(body, name="gdn_chunks_bwd", out_shape=out_shape, grid=(hg, nc), in_specs=in_specs,
                 out_specs=out_specs, scratch_shapes=[pltpu.VMEM((hb, d, d), F32)],
                 semantics=("arbitrary", "arbitrary"))(sqkv, sqkv, sqkv, z.arr, braw, araw, alog, dtb, nw, ssave, tsave,
                                                       dy)


def _m2_groups(xs, z, bm, cm, dtraws, alogs, dtbs, dsks, nw, st):
    g, c, gw = xs.shape
    rep = len(dtraws)
    causal, _, lower, upper_t, _ = _chunk_masks(c)
    lower_b = jnp.broadcast_to(lower[None], (g, c, c))
    lane_head = lax.broadcasted_iota(jnp.int32, (1, 1, gw), 2) // M2_HEAD_DIM

    def expand(cols):
        res = jnp.broadcast_to(cols[-1], (g, cols[-1].shape[1], gw))
        for i in reversed(range(rep - 1)):
            res = jnp.where(lane_head == i, cols[i], res)
        return res

    dts = [_softplus(dtraws[i] + dtbs[i]) for i in range(rep)]
    adts = [-jnp.exp(alogs[i]) * dts[i] for i in range(rep)]
    dt_l, adt_l, dsk_l = expand(dts), expand(adts), expand(dsks)
    xdt = xs * dt_l
    acum = _bnn(lower_b, adt_l, HI)
    alast = jnp.sum(adt_l, axis=1, keepdims=True)
    scores = _bnt(cm, bm)
    y = jnp.exp(acum) * _bnn(cm, st) + dsk_l * xs
    for i in range(rep):
        seg = jnp.where(causal[None], jnp.exp(_bnn(lower_b, adts[i] * upper_t[None], HI)), 0.0)
        y = y + _bnn(scores * seg, jnp.where(lane_head == i, xdt, 0.0))
    st_out = st * jnp.exp(alast) + _btn(bm, xdt * jnp.exp(alast - acum))
    y2 = y * _silu(z)
    out = y2 * lax.rsqrt(jnp.mean(y2 * y2, axis=-1, keepdims=True) + NORM_EPS) * nw
    return out, st_out


def _m2_dims(sxbc, z):
    t = sxbc.shape[0]
    w2 = z.shape[1]
    g = M2_GROUPS
    n = M2_STATE
    assert sxbc.shape[1] == w2 + 2 * g * n
    gw = w2 // g
    return t, w2, g, n, gw, gw // M2_HEAD_DIM, t // CHUNK


def _m2_args(refs, g, n, gw, rep, st):
    sx_ref, z_ref, dt_ref, alog_ref, dtb_ref, dsk_ref, nw_ref = refs
    w2 = g * gw
    xs = jnp.stack([sx_ref[:, i * gw:(i + 1) * gw] for i in range(g)])
    z = jnp.stack([z_ref[:, i * gw:(i + 1) * gw] for i in range(g)])
    bm = jnp.stack([sx_ref[:, w2 + i * n:w2 + (i + 1) * n] for i in range(g)])
    cm = jnp.stack([sx_ref[:, w2 + (g + i) * n:w2 + (g + i + 1) * n] for i in range(g)])
    nw = jnp.stack([nw_ref[:, i * gw:(i + 1) * gw] for i in range(g)])

    def cols(ref):
        blk = ref[...]
        return [jnp.stack([_lane_pick(blk, gi * rep + i) for gi in range(g)]) for i in range(rep)]

    return xs, z, bm, cm, cols(dt_ref), cols(alog_ref), cols(dtb_ref), cols(dsk_ref), nw, st


def _m2_chunks_fwd(sxbc, z, dtraw, alog, dtb, dsk, nw):
    t, w2, g, n, gw, rep, nc = _m2_dims(sxbc, z)
    hm = dtraw.shape[1]
    c = CHUNK
    wx = sxbc.shape[1]

    def body(sx_ref, z_ref, dt_ref, alog_ref, dtb_ref, dsk_ref, nw_ref, y_ref, ssave_ref, st_ref):
        @pl.when(pl.program_id(0) == 0)
        def _():
            st_ref[...] = jnp.zeros_like(st_ref)

        st = st_ref[...]
        ssave_ref[...] = st
        y, st_out = _m2_groups(*_m2_args((sx_ref, z_ref, dt_ref, alog_ref, dtb_ref, dsk_ref, nw_ref), g, n, gw, rep, st))
        for i in range(g):
            y_ref[:, i * gw:(i + 1) * gw] = y[i]
        st_ref[...] = st_out

    z = _win(z)
    zb = z.off // w2
    assert z.off % w2 == 0
    in_specs = [pl.BlockSpec((c, wx), lambda k: (k, 0)), pl.BlockSpec((c, w2), lambda k: (k, zb)),
                pl.BlockSpec((c, hm), lambda k: (k, 0)),
                pl.BlockSpec((1, hm), lambda k: (0, 0)), pl.BlockSpec((1, hm), lambda k: (0, 0)),
                pl.BlockSpec((1, hm), lambda k: (0, 0)), pl.BlockSpec((1, w2), lambda k: (0, 0))]
    out_shape = [_sds((t, w2), F32), _sds((nc, g, n, gw), F32)]
    out_specs = [pl.BlockSpec((c, w2), lambda k: (k, 0)), pl.BlockSpec((None, g, n, gw), lambda k: (k, 0, 0, 0))]
    return _call(body, name="m2_chunks_fwd", out_shape=out_shape, grid=(nc,), in_specs=in_specs,
                 out_specs=out_specs, scratch_shapes=[pltpu.VMEM((g, n, gw), F32)],
                 semantics=("arbitrary",))(sxbc, z.arr, dtraw, alog, dtb, dsk, nw)


def _m2_chunks_bwd(sxbc, z, dtraw, alog, dtb, dsk, nw, ssave, dy):
    t, w2, g, n, gw, rep, nc = _m2_dims(sxbc, z)
    hm = dtraw.shape[1]
    c = CHUNK

    wx = sxbc.shape[1]

    def body(sx_ref, z_ref, dt_ref, alog_ref, dtb_ref, dsk_ref, nw_ref, ssave_ref, dy_ref,
             dsx_ref, dz_ref, ddt_ref, dalog_ref, ddtb_ref, ddsk_ref, dnw_ref, dst_ref):
        @pl.when(pl.program_id(0) == 0)
        def _():
            dst_ref[...] = jnp.zeros_like(dst_ref)
            dnw_ref[...] = jnp.zeros_like(dnw_ref)
            dalog_ref[...] = jnp.zeros_like(dalog_ref)
            ddtb_ref[...] = jnp.zeros_like(ddtb_ref)
            ddsk_ref[...] = jnp.zeros_like(ddsk_ref)

        args = _m2_args((sx_ref, z_ref, dt_ref, alog_ref, dtb_ref, dsk_ref, nw_ref), g, n, gw, rep, ssave_ref[...])
        _, vjp = jax.vjp(_m2_groups, *args)
        dyb = jnp.stack([dy_ref[:, i * gw:(i + 1) * gw] for i in range(g)])
        dxs, dz, dbm, dcm, ddts, dalogs, ddtbs, ddsks, dnw, dst = vjp((dyb, dst_ref[...]))
        dst_ref[...] = dst
        lane_h = lax.broadcasted_iota(jnp.int32, (1, hm), 1)
        ddt = jnp.zeros((c, hm), F32)
        for gi in range(g):
            dsx_ref[:, gi * gw:(gi + 1) * gw] = dxs[gi]
            dsx_ref[:, w2 + gi * n:w2 + (gi + 1) * n] = dbm[gi]
            dsx_ref[:, w2 + (g + gi) * n:w2 + (g + gi + 1) * n] = dcm[gi]
            dz_ref[:, gi * gw:(gi + 1) * gw] = dz[gi].astype(dz_ref.dtype)
            dnw_ref[:, gi * gw:(gi + 1) * gw] += dnw[gi]
            for i in range(rep):
                onehot = (lane_h == gi * rep + i).astype(F32)
                ddt = ddt + ddts[i][gi] * onehot
                dalog_ref[...] += dalogs[i][gi] * onehot
                ddtb_ref[...] += ddtbs[i][gi] * onehot
                ddsk_ref[...] += ddsks[i][gi] * onehot
        ddt_ref[...] = ddt

    rev = lambda k: nc - 1 - k
    z = _win(z)
    zb = z.off // w2
    assert z.off % w2 == 0
    in_specs = [pl.BlockSpec((c, wx), lambda k: (rev(k), 0)), pl.BlockSpec((c, w2), lambda k: (rev(k), zb)),
                pl.BlockSpec((c, hm), lambda k: (rev(k), 0)),
                pl.BlockSpec((1, hm), lambda k: (0, 0)), pl.BlockSpec((1, hm), lambda k: (0, 0)),
                pl.BlockSpec((1, hm), lambda k: (0, 0)), pl.BlockSpec((1, w2), lambda k: (0, 0)),
                pl.BlockSpec((None, g, n, gw), lambda k: (rev(k), 0, 0, 0)),
                pl.BlockSpec((c, w2), lambda k: (rev(k), 0))]
    out_shape = [_sds((t, wx), F32), _sds((t, w2), BF16), _sds((t, hm), F32), _sds((1, hm), F32), _sds((1, hm), F32),
                 _sds((1, hm), F32), _sds((1, w2), F32)]
    out_specs = [pl.BlockSpec((c, wx), lambda k: (rev(k), 0)), pl.BlockSpec((c, w2), lambda k: (rev(k), 0)),
                 pl.BlockSpec((c, hm), lambda k: (rev(k), 0)),
                 pl.BlockSpec((1, hm), lambda k: (0, 0)), pl.BlockSpec((1, hm), lambda k: (0, 0)),
                 pl.BlockSpec((1, hm), lambda k: (0, 0)), pl.BlockSpec((1, w2), lambda k: (0, 0))]
    return _call(body, name="m2_chunks_bwd", out_shape=out_shape, grid=(nc,), in_specs=in_specs,
                 out_specs=out_specs, scratch_shapes=[pltpu.VMEM((g, n, gw), F32)],
                 semantics=("arbitrary",))(sxbc, z.arr, dtraw, alog, dtb, dsk, nw, ssave, dy)


def _s5_scan(bu, a_l, cw, reverse, name, tb=256):
    t, wtot = bu.shape
    jn = wtot // cw
    half = cw // 2
    tb = _tile(t, tb, SUBLANES)
    nb = t // tb

    def body(bu_ref, a_ref, s_ref, st_ref):
        @pl.when(pl.program_id(1) == 0)
        def _():
            st_ref[...] = jnp.zeros_like(st_ref)

        are = a_ref[:, :half]
        aim = -a_ref[:, half:] if reverse else a_ref[:, half:]

        def step(k, carry):
            sre, sim = carry
            r = tb - 1 - k if reverse else k
            nre = are * sre - aim * sim + bu_ref[pl.ds(r, 1), pl.ds(0, half)]
            nim = are * sim + aim * sre + bu_ref[pl.ds(r, 1), pl.ds(half, half)]
            s_ref[pl.ds(r, 1), pl.ds(0, half)] = nre
            s_ref[pl.ds(r, 1), pl.ds(half, half)] = nim
            return nre, nim

        sre, sim = lax.fori_loop(0, tb, step, (st_ref[:, :half], st_ref[:, half:]), unroll=8)
        st_ref[:, :half] = sre
        st_ref[:, half:] = sim

    rb = (lambda i: nb - 1 - i) if reverse else (lambda i: i)
    return _call(body, name=name, out_shape=_sds((t, wtot), F32), grid=(jn, nb),
                 in_specs=[pl.BlockSpec((tb, cw), lambda j, i: (rb(i), j)), pl.BlockSpec((1, cw), lambda j, i: (0, j))],
                 out_specs=pl.BlockSpec((tb, cw), lambda j, i: (rb(i), j)),
                 scratch_shapes=[pltpu.VMEM((1, cw), F32)], semantics=("parallel", "arbitrary"))(bu, a_l)


def _s5_da(ds, s, cw, tb=256):
    t, wtot = ds.shape
    jn = wtot // cw
    half = cw // 2
    tb = _tile(t, tb, SUBLANES)
    nb = t // tb
    hb = tb // SUBLANES

    def body(ds_ref, s_ref, halo_ref, o_ref):
        i = pl.program_id(1)

        @pl.when(i == 0)
        def _():
            o_ref[...] = jnp.zeros_like(o_ref)

        prev = jnp.where(i == 0, 0.0, halo_ref[SUBLANES - 1:SUBLANES, :])
        row = lax.broadcasted_iota(jnp.int32, (tb, cw), 0)
        sh = jnp.where(row == 0, prev, pltpu.roll(s_ref[...], 1, 0))
        d = ds_ref[...]
        dre, dim, sre, sim = d[:, :half], d[:, half:], sh[:, :half], sh[:, half:]
        o_ref[:, :half] += jnp.sum(dre * sre + dim * sim, axis=0, keepdims=True)
        o_ref[:, half:] += jnp.sum(dim * sre - dre * sim, axis=0, keepdims=True)

    return _call(body, name="s5_da", out_shape=_sds((1, wtot), F32), grid=(jn, nb),
                 in_specs=[pl.BlockSpec((tb, cw), lambda j, i: (i, j)), pl.BlockSpec((tb, cw), lambda j, i: (i, j)),
                           pl.BlockSpec((SUBLANES, cw), lambda j, i: (jnp.maximum(i * hb - 1, 0), j))],
                 out_specs=pl.BlockSpec((1, cw), lambda j, i: (0, j)),
                 semantics=("parallel", "arbitrary"))(ds, s, s)


def _conv_rows(t, tb):
    tb = _tile(t, tb, SUBLANES * 2)
    return tb, t // tb, tb // SUBLANES


def _shift_down(x, above, s):
    n = x.shape[0]
    y = pltpu.roll(x, s, 0)
    row = lax.broadcasted_iota(jnp.int32, above.shape, 0)
    head = jnp.where(row < s, pltpu.roll(above, s, 0), y[:SUBLANES])
    return head if n == SUBLANES else jnp.concatenate([head, y[SUBLANES:]], axis=0)


def _shift_up(x, below, s):
    n = x.shape[0]
    y = pltpu.roll(x, n - s, 0)
    row = lax.broadcasted_iota(jnp.int32, below.shape, 0)
    tail = jnp.where(row >= SUBLANES - s, pltpu.roll(below, SUBLANES - s, 0), y[n - SUBLANES:])
    return tail if n == SUBLANES else jnp.concatenate([y[:n - SUBLANES], tail], axis=0)


def _conv_taps(x, above, w_ref, b_ref):
    xs = [x] + [_shift_down(x, above, s) for s in range(1, CONV_K)]
    c = b_ref[...] + w_ref[CONV_K - 1:CONV_K, :] * x
    for s in range(1, CONV_K):
        c = c + w_ref[CONV_K - 1 - s:CONV_K - s, :] * xs[s]
    return c, xs


CONV_COL_TILE = 1024


def _conv_specs(x, tb):
    x = _win(x)
    t, cwid = x.shape
    tb, nb, hb = _conv_rows(t, tb)
    ct = _col_tile([x], CONV_COL_TILE)
    base = x.off // ct
    blk_x = pl.BlockSpec((tb, ct), lambda j, i: (i, base + j))
    prev_x = pl.BlockSpec((SUBLANES, ct), lambda j, i: (jnp.maximum(i * hb - 1, 0), base + j))
    next_x = pl.BlockSpec((SUBLANES, ct), lambda j, i: (jnp.minimum((i + 1) * hb, nb * hb - 1), base + j))
    blk = pl.BlockSpec((tb, ct), lambda j, i: (i, j))
    nxt = pl.BlockSpec((SUBLANES, ct), lambda j, i: (jnp.minimum((i + 1) * hb, nb * hb - 1), j))
    taps = pl.BlockSpec((CONV_K, ct), lambda j, i: (0, j))
    bias = pl.BlockSpec((1, ct), lambda j, i: (0, j))
    return x, tb, nb, cwid // ct, dict(blk_x=blk_x, prev_x=prev_x, next_x=next_x, blk=blk, nxt=nxt, taps=taps, bias=bias)


def _conv_fwd(x, w, b, name, tb=256):
    x, tb, nb, ncol, sp = _conv_specs(x, tb)
    t, cwid = x.shape

    def body(x_ref, halo_ref, w_ref, b_ref, o_ref):
        above = jnp.where(pl.program_id(1) == 0, 0.0, halo_ref[...])
        o_ref[...] = _silu(_conv_taps(x_ref[...], above, w_ref, b_ref)[0])

    return _call(body, name=name, out_shape=_sds((t, cwid), F32), grid=(ncol, nb),
                 in_specs=[sp['blk_x'], sp['prev_x'], sp['taps'], sp['bias']], out_specs=sp['blk'],
                 semantics=("parallel", "parallel"))(x.arr, x.arr, w, b)


def _dsilu(c):
    sg = _sigmoid(c)
    return sg * (1.0 + c * (1.0 - sg))


def _conv_bwd(x, w, b, ds, name, tb=256):
    x, tb, nb, ncol, sp = _conv_specs(x, tb)
    t, cwid = x.shape

    def body(x_ref, halo_ref, xn_ref, ds_ref, dsn_ref, w_ref, b_ref, dx_ref, dw_ref, db_ref):
        i = pl.program_id(1)

        @pl.when(i == 0)
        def _():
            dw_ref[...] = jnp.zeros_like(dw_ref)
            db_ref[...] = jnp.zeros_like(db_ref)

        x = x_ref[...]
        above = jnp.where(i == 0, 0.0, halo_ref[...])
        c, xs = _conv_taps(x, above, w_ref, b_ref)
        dc = ds_ref[...] * _dsilu(c)
        cn, _ = _conv_taps(xn_ref[...], x[tb - SUBLANES:], w_ref, b_ref)
        dcn = jnp.where(i == nb - 1, 0.0, dsn_ref[...] * _dsilu(cn))
        dx = w_ref[CONV_K - 1:CONV_K, :] * dc
        for s in range(1, CONV_K):
            dx = dx + w_ref[CONV_K - 1 - s:CONV_K - s, :] * _shift_up(dc, dcn, s)
        dx_ref[...] = dx.astype(dx_ref.dtype)
        for s in range(CONV_K):
            dw_ref[CONV_K - 1 - s:CONV_K - s, :] += jnp.sum(dc * xs[s], axis=0, keepdims=True)
        db_ref[...] += jnp.sum(dc, axis=0, keepdims=True)

    return _call(body, name=name, out_shape=[_sds((t, cwid), BF16), _sds((CONV_K, cwid), F32), _sds((1, cwid), F32)],
                 grid=(ncol, nb),
                 in_specs=[sp['blk_x'], sp['prev_x'], sp['next_x'], sp['blk'], sp['nxt'], sp['taps'], sp['bias']],
                 out_specs=[sp['blk'], sp['taps'], sp['bias']],
                 semantics=("parallel", "arbitrary"))(x.arr, x.arr, x.arr, ds, ds, w, b)


def _f_rms(x, w):
    return (x * lax.rsqrt(jnp.mean(x * x, axis=-1, keepdims=True) + NORM_EPS) * w,)


def _f_s5_post1(ymm, u, d_l):
    return (_gelu_tanh(ymm + d_l * u),)


def _f_s5_post2(yg, tt, gate, b):
    return (yg * _sigmoid(tt + b) * _silu(gate),)


def _f_merge(ma, mb, mc, pa, pb, pc):
    return (_sigmoid(ma) * pa + _sigmoid(mb) * pb + _sigmoid(mc) * pc,)


def _loss_and_grad(x, tgt, fw, tb=256):
    t, dm = x.shape
    tb = _tile(t, tb, SUBLANES * 2)

    def f(xb, wb, tb_):
        y = _f_rms(xb, wb)[0]
        e = y - tb_
        return 0.5 * jnp.sum(jnp.mean(e * e, axis=-1, keepdims=True), axis=0, keepdims=True)

    def body(x_ref, t_ref, w_ref, loss_ref, dx_ref, dw_ref):
        @pl.when(pl.program_id(0) == 0)
        def _():
            loss_ref[...] = jnp.zeros_like(loss_ref)
            dw_ref[...] = jnp.zeros_like(dw_ref)

        tgt_b = t_ref[...]
        val, vjp = jax.vjp(lambda a, b: f(a, b, tgt_b), x_ref[...], w_ref[...])
        dxb, dwb = vjp(jnp.ones((1, 1), F32))
        loss_ref[...] += val
        dx_ref[...] = dxb
        dw_ref[...] += dwb

    return _call(body, name="loss_and_grad", out_shape=[_sds((1, 1), F32), _sds((t, dm), F32), _sds((1, dm), F32)],
                 grid=(t // tb,),
                 in_specs=[pl.BlockSpec((tb, dm), lambda i: (i, 0)), pl.BlockSpec((tb, dm), lambda i: (i, 0)),
                           pl.BlockSpec((1, dm), lambda i: (0, 0))],
                 out_specs=[pl.BlockSpec((1, 1), lambda i: (0, 0)), pl.BlockSpec((tb, dm), lambda i: (i, 0)),
                            pl.BlockSpec((1, dm), lambda i: (0, 0))],
                 semantics=("arbitrary",))(x, tgt, fw)


FLAT_W = 1024


def _sum_parts(parts, name, tb=256):
    n, r, wd = parts.shape
    tb = _tile(r, tb, SUBLANES)

    def body(p_ref, o_ref):
        acc = p_ref[0]
        for k in range(1, n):
            acc = acc + p_ref[k]
        o_ref[...] = acc

    return _call(body, name=name, out_shape=_sds((r, wd), F32), grid=(r // tb,),
                 in_specs=[pl.BlockSpec((n, tb, wd), lambda i: (0, i, 0))],
                 out_specs=pl.BlockSpec((tb, wd), lambda i: (i, 0)), semantics=("parallel",))(parts)


BLOCK_BYTES = 1 << 20


def _rows_per_block(r, wd):
    return _tile(r, max(SUBLANES * 2, BLOCK_BYTES // (4 * wd) // (SUBLANES * 2) * (SUBLANES * 2)), SUBLANES * 2)


def _add_my_half(g4, recv, c_idx, name):
    p, _, r, wd = g4.shape
    tb = _rows_per_block(r, wd)

    def body(c_ref, g_ref, r_ref, o_ref, ob_ref):
        s = g_ref[...] + r_ref[...]
        o_ref[...] = s
        ob_ref[...] = s.astype(BF16)

    spec = pl.BlockSpec((None, tb, wd), lambda j, i, c_ref: (j, i, 0))
    return _call(body, name=name, out_shape=[_sds((p, r, wd), F32), _sds((p, r, wd), BF16)], grid=(p, r // tb),
                 in_specs=[pl.BlockSpec((None, None, tb, wd), lambda j, i, c_ref: (j, c_ref[0], i, 0)), spec],
                 out_specs=[spec, spec], semantics=("parallel", "parallel"), num_scalar_prefetch=1)(c_idx, g4, recv)


def _sum_chips(own, got, me_idx, name):
    p, r, wd = own.shape
    tb = _rows_per_block(r, wd)

    def body(me_ref, own_ref, got_ref, o_ref):
        me = me_ref[0]
        acc = None
        for k in range(p):
            part = jnp.where(me == k, own_ref[...], got_ref[k].astype(F32))
            acc = part if acc is None else acc + part
        o_ref[...] = acc

    return _call(body, name=name, out_shape=_sds((r, wd), F32), grid=(r // tb,),
                 in_specs=[pl.BlockSpec((None, tb, wd), lambda i, me_ref: (me_ref[0], i, 0)),
                           pl.BlockSpec((p, tb, wd), lambda i, me_ref: (0, i, 0))],
                 out_specs=pl.BlockSpec((tb, wd), lambda i, me_ref: (i, 0)),
                 semantics=("parallel",), num_scalar_prefetch=1)(me_idx, own, got)


def _adamw(w, g, m, v, name):
    r, wd = w.shape
    tb = _rows_per_block(r, wd)

    def body(w_ref, g_ref, m_ref, v_ref, d_ref, nm_ref, nv_ref):
        gg = g_ref[...]
        nm = ADAM_B1 * m_ref[...] + (1.0 - ADAM_B1) * gg
        nv = ADAM_B2 * v_ref[...] + (1.0 - ADAM_B2) * (gg * gg)
        m_hat = nm / (1.0 - ADAM_B1 ** ADAM_STEP)
        v_hat = nv / (1.0 - ADAM_B2 ** ADAM_STEP)
        d_ref[...] = -ADAM_LR * (m_hat / (jnp.sqrt(v_hat) + ADAM_EPS) + ADAM_WD * w_ref[...])
        nm_ref[...] = nm
        nv_ref[...] = nv

    spec = pl.BlockSpec((tb, wd), lambda i: (i, 0))
    return _call(body, name=name, out_shape=[_sds((r, wd), F32)] * 3, grid=(r // tb,), in_specs=[spec] * 4,
                 out_specs=[spec] * 3, semantics=("parallel",))(w, g, m, v)


def _here():
    return lax.axis_index("x"), lax.axis_index("y"), lax.axis_index("c")


def _comm_call(body, name, out_shape, n_sems, operands):
    anyspec = pl.BlockSpec(memory_space=pl.ANY)
    outs = out_shape if isinstance(out_shape, (list, tuple)) else [out_shape]
    return _call(body, name=name, out_shape=out_shape, in_specs=[anyspec] * len(operands),
                 out_specs=[anyspec] * len(outs) if isinstance(out_shape, (list, tuple)) else anyspec,
                 scratch_shapes=[pltpu.SemaphoreType.DMA((n_sems,)), pltpu.SemaphoreType.DMA((n_sems,)),
                                 pltpu.SemaphoreType.DMA(())])(*operands)


def _gather_chips(x, name):
    def body(x_ref, o_ref, send_sems, recv_sems, local_sem):
        xi, yi, ci = _here()
        chips = [(1 - xi, yi), (xi, 1 - yi), (1 - xi, 1 - yi)]
        mine = pltpu.make_async_copy(x_ref, o_ref.at[2 * xi + yi], local_sem)
        mine.start()

        def copy(k, slot, to):
            return pltpu.make_async_remote_copy(src_ref=x_ref, dst_ref=o_ref.at[slot], send_sem=send_sems.at[k],
                                                recv_sem=recv_sems.at[k], device_id=to, device_id_type=MESH)

        sends = [copy(k, 2 * xi + yi, (px, py, ci)) for k, (px, py) in enumerate(chips)]
        for cp in sends:
            cp.start()
        for k, (px, py) in enumerate(chips):
            copy(k, 2 * px + py, (px, py, ci)).wait_recv()
        for cp in sends:
            cp.wait_send()
        mine.wait()

    return _comm_call(body, name, _sds((4,) + x.shape, x.dtype), 3, (x,))


def _gather_all(x, name):
    def body(x_ref, o_ref, send_sems, recv_sems, local_sem):
        xi, yi, ci = _here()
        me = 4 * xi + 2 * yi + ci
        flips = [(fx, fy, fc) for fx in (0, 1) for fy in (0, 1) for fc in (0, 1)][1:]
        peers = [((1 - xi) if fx else xi, (1 - yi) if fy else yi, (1 - ci) if fc else ci) for fx, fy, fc in flips]
        mine = pltpu.make_async_copy(x_ref, o_ref.at[me], local_sem)
        mine.start()

        def copy(k, slot, to):
            return pltpu.make_async_remote_copy(src_ref=x_ref, dst_ref=o_ref.at[slot], send_sem=send_sems.at[k],
                                                recv_sem=recv_sems.at[k], device_id=to, device_id_type=MESH)

        sends = [copy(k, me, p) for k, p in enumerate(peers)]
        for cp in sends:
            cp.start()
        for k, (px, py, pc) in enumerate(peers):
            copy(k, 4 * px + 2 * py + pc, (px, py, pc)).wait_recv()
        for cp in sends:
            cp.wait_send()
        mine.wait()

    return _comm_call(body, name, _sds((8,) + x.shape, x.dtype), 7, (x,))


def _multi_comm_call(body, name, out_shapes, n_sems, operands):
    anyspec = pl.BlockSpec(memory_space=pl.ANY)
    nin = len(operands)

    def flat_body(*refs):
        body(refs[:nin], refs[nin:nin + len(out_shapes)], refs[-2], refs[-1])

    return _call(flat_body, name=name, out_shape=list(out_shapes), in_specs=[anyspec] * nin,
                 out_specs=[anyspec] * len(out_shapes),
                 scratch_shapes=[pltpu.SemaphoreType.DMA((n_sems,)), pltpu.SemaphoreType.DMA((n_sems,))])(*operands)


def _remote(src, dst, send_sems, recv_sems, k, to):
    return pltpu.make_async_remote_copy(src_ref=src, dst_ref=dst, send_sem=send_sems.at[k], recv_sem=recv_sems.at[k],
                                        device_id=to, device_id_type=MESH)


def _gather_chips_split(xs, name):
    nw = len(xs)

    def body(x_refs, o_refs, send_sems, recv_sems):
        xi, yi, ci = _here()
        me = 2 * xi + yi
        sib = (xi, yi, 1 - ci)
        chips = [(1 - xi, yi), (xi, 1 - yi), (1 - xi, 1 - yi)]
        sends = []
        for i in range(nw):
            for k, (px, py) in enumerate(chips):
                sends.append(_remote(x_refs[i].at[ci], o_refs[i].at[me, ci], send_sems, recv_sems, 3 * i + k,
                                     (px, py, ci)))
        for cp in sends:
            cp.start()
        passed = []
        for i in range(nw):
            for k, (px, py) in enumerate(chips):
                landed = o_refs[i].at[2 * px + py, ci]
                _remote(landed, landed, send_sems, recv_sems, 3 * i + k, (px, py, ci)).wait_recv()
                fwd = _remote(landed, landed, send_sems, recv_sems, 3 * (nw + i) + k, sib)
                fwd.start()
                passed.append(fwd)
        for i in range(nw):
            for k, (px, py) in enumerate(chips):
                other = o_refs[i].at[2 * px + py, 1 - ci]
                _remote(other, other, send_sems, recv_sems, 3 * (nw + i) + k, sib).wait_recv()
        for cp in sends + passed:
            cp.wait_send()

    return _multi_comm_call(body, name, [_sds((4,) + x.shape, x.dtype) for x in xs], 6 * nw, xs)


def _swap_sibling_half(gs):
    def body(g_refs, o_refs, send_sems, recv_sems):
        xi, yi, ci = _here()
        cps = [_remote(g.at[:, 1 - ci], o, send_sems, recv_sems, i, (xi, yi, 1 - ci))
               for i, (g, o) in enumerate(zip(g_refs, o_refs))]
        for cp in cps:
            cp.start()
        for cp in cps:
            cp.wait()

    return _multi_comm_call(body, "swap_sibling_half", [_sds((g.shape[0],) + g.shape[2:], g.dtype) for g in gs],
                            len(gs), gs)


def _scatter_chips(gps):
    nw = len(gps)

    def body(g_refs, o_refs, send_sems, recv_sems):
        xi, yi, ci = _here()
        me = 2 * xi + yi
        chips = [(1 - xi, yi), (xi, 1 - yi), (1 - xi, 1 - yi)]
        sends = [_remote(g_refs[i].at[2 * px + py], o_refs[i].at[me], send_sems, recv_sems, 3 * i + k, (px, py, ci))
                 for i in range(nw) for k, (px, py) in enumerate(chips)]
        for cp in sends:
            cp.start()
        for i in range(nw):
            for k, (px, py) in enumerate(chips):
                slot = o_refs[i].at[2 * px + py]
                _remote(slot, slot, send_sems, recv_sems, 3 * i + k, (px, py, ci)).wait_recv()
        for cp in sends:
            cp.wait_send()

    return _multi_comm_call(body, "scatter_chips", [_sds(g.shape, g.dtype) for g in gps], 3 * nw, gps)


def _share_sibling(rs):
    def body(r_refs, o_refs, send_sems, recv_sems):
        xi, yi, ci = _here()
        cps = [_remote(r, o, send_sems, recv_sems, i, (xi, yi, 1 - ci)) for i, (r, o) in enumerate(zip(r_refs, o_refs))]
        for cp in cps:
            cp.start()
        for cp in cps:
            cp.wait()

    return _multi_comm_call(body, "share_sibling", [_sds(r.shape, r.dtype) for r in rs], len(rs), rs)


def _flat_pack(arrs, dtype, row_mult):
    flat = jnp.concatenate([a.astype(dtype).reshape(-1) for a in arrs])
    unit = FLAT_W * row_mult
    npad = -(-flat.shape[0] // unit) * unit
    return jnp.pad(flat, (0, npad - flat.shape[0])).reshape(npad // FLAT_W, FLAT_W)


def _flat_unpack(flat2d, shapes):
    flat = flat2d.reshape(-1)
    outs, off = [], 0
    for s in shapes:
        size = int(np.prod(s))
        outs.append(flat[off:off + size].reshape(s))
        off += size
    return outs


def _split_cols(a, widths):
    outs, off = [], 0
    for wd in widths:
        outs.append(lax.slice_in_dim(a, off, off + wd, axis=1))
        off += wd
    return outs


def _s5_params(lam_re, lam_im, log_step, b_re, b_im, c_re, c_im, d_skip):
    g, p = lam_re.shape
    hs = b_re.shape[2]
    gt = S5_GROUP_TILE
    jn = g // gt
    lam_re = jnp.minimum(lam_re, -1e-4)
    step = jnp.exp(log_step)[:, None]
    mag = jnp.exp(lam_re * step)
    ab_re = mag * jnp.cos(lam_im * step)
    ab_im = mag * jnp.sin(lam_im * step)
    den = lam_re * lam_re + lam_im * lam_im
    f_re = ((ab_re - 1.0) * lam_re + ab_im * lam_im) / den
    f_im = (ab_im * lam_re - (ab_re - 1.0) * lam_im) / den
    bb_re = f_re[..., None] * b_re - f_im[..., None] * b_im
    bb_im = f_re[..., None] * b_im + f_im[..., None] * b_re
    a_l = jnp.concatenate([ab_re.reshape(jn, gt * p), ab_im.reshape(jn, gt * p)], axis=1).reshape(1, jn * 2 * gt * p)
    eye = jnp.eye(gt, dtype=F32)

    def blockdiag(m):
        return jnp.einsum('jahp,ab->jahbp', m.reshape(jn, gt, hs, p), eye).reshape(jn, gt * hs, gt * p)

    b_blk = jnp.concatenate([blockdiag(bb_re.transpose(0, 2, 1)), blockdiag(bb_im.transpose(0, 2, 1))], axis=2)
    c_blk = jnp.concatenate([blockdiag(c_re), blockdiag(-c_im)], axis=2)
    return a_l, b_blk, c_blk, d_skip.reshape(1, g * hs)


def _layer_dims(p):
    d_model = p['w_out'].shape[1]
    wa = p['proj_a'].shape[0]
    h = p['gdn_a_log'].shape[0]
    wb = p['proj_b'].shape[0]
    wc = p['proj_c'].shape[0]
    hm = p['m2_a_log'].shape[0]
    cdim = p['m2_conv_w'].shape[1]
    width = dict(zip(PROJ_ORDER, (3 * wa, wa, h, h, wb, wb, wc, cdim, hm, d_model, d_model, d_model)))
    n_in = sum(width.values())
    return width, n_in, -(-n_in // LANES) * LANES


PROJ_ORDER = ('qkv', 'az', 'braw', 'araw', 'su', 'sgate', 'cz', 'cxbc', 'cdt', 'ma', 'mb', 'mc')
WORK_ORDER = ('qkv', 'az', 'cz', 'su', 'sgate', 'cxbc', 'ma', 'mb', 'mc', 'braw', 'araw', 'cdt')
ROW_COL_TILE = 512


def _reorder_cols(a, width, src_order, dst_order, n_out):
    off, o = {}, 0
    for n in src_order:
        off[n] = o
        o += width[n]
    parts = [lax.slice_in_dim(a, off[n], off[n] + width[n], axis=a.ndim - 1) for n in dst_order]
    used = sum(width[n] for n in dst_order)
    if n_out > used:
        parts.append(jnp.zeros(a.shape[:-1] + (n_out - used,), a.dtype))
    return jnp.concatenate(parts, axis=a.ndim - 1)


def _layer_fwd(x, p):
    width, n_in, n_pad = _layer_dims(p)
    sv = {'x': x}
    h = _rowwise(_f_rms, [x], [p['norm_w'][None]], [x.shape[1]], [BF16], "rms_fwd")[0]
    w_in = _reorder_cols(p['w_in'], width, PROJ_ORDER, WORK_ORDER, n_pad)
    proj = _matmul(h, w_in, 'nn', F32, "in_proj", tn=IN_PROJ_TILE)
    wins, off = {}, 0
    for n in WORK_ORDER:
        wins[n] = Win(proj, off, width[n])
        off += width[n]
    qkv, az, cz, su, sgate, cxbc, ma, mb, mc = (wins[n] for n in WORK_ORDER[:9])
    braw, araw, cdt = (lax.slice_in_dim(proj, wins[n].off, wins[n].off + width[n], axis=1) for n in WORK_ORDER[9:])
    sv.update(h=h, w_in=w_in, qkv=qkv, az=az, braw=braw, araw=araw, su=su, sgate=sgate, cz=cz, cxbc=cxbc, cdt=cdt,
              ma=ma, mb=mb, mc=mc)
    gb0 = jnp.zeros((1, qkv.shape[1]), F32)
    sqkv = _conv_fwd(qkv, p['gdn_conv_w'], gb0, "gdn_conv_fwd")
    ya, ssa, tsa = _gdn_chunks_fwd(sqkv, az, braw, araw, p['gdn_a_log'][None], p['gdn_dt_bias'][None],
                                   p['gdn_norm_w'][None])
    sv.update(sqkv=sqkv, ssa=ssa, tsa=tsa, ya=ya)
    s5_in = tuple(p[k] for k in ('s5_lam_re', 's5_lam_im', 's5_log_step', 's5_b_re', 's5_b_im', 's5_c_re', 's5_c_im',
                                 's5_d'))
    (a_l, b_blk, c_blk, d_l), s5_vjp = jax.vjp(_s5_params, *s5_in)
    cw = b_blk.shape[2]
    bu = _matmul_bd(su, b_blk, 'nn', F32, "s5_bu")
    s = _s5_scan(bu, a_l, cw, False, "s5_scan_fwd")
    ymm = _matmul_bd(s, c_blk, 'nt', F32, "s5_out")
    yg = _rowwise(_f_s5_post1, [ymm, su], [d_l], [su.shape[1]], [F32], "s5_post1_fwd", col_tile=ROW_COL_TILE)[0]
    tt = _matmul(yg, p['s5_glu_w'], 'nn', F32, "s5_glu")
    yb = _rowwise(_f_s5_post2, [yg, tt, sgate], [p['s5_glu_b'][None]], [su.shape[1]], [F32], "s5_post2_fwd",
                  col_tile=ROW_COL_TILE)[0]
    sv.update(a_l=a_l, b_blk=b_blk, c_blk=c_blk, d_l=d_l, s5_vjp=s5_vjp, s=s, ymm=ymm, yg=yg, tt=tt, yb=yb, cw=cw)
    sxbc = _conv_fwd(cxbc, p['m2_conv_w'], p['m2_conv_b'][None], "m2_conv_fwd")
    yc, ssc = _m2_chunks_fwd(sxbc, cz, cdt, p['m2_a_log'][None], p['m2_dt_bias'][None], p['m2_d'][None],
                             p['m2_norm_w'][None])
    sv.update(sxbc=sxbc, ssc=ssc, yc=yc)
    pa = _matmul(ya, p['proj_a'], 'nn', F32, "proj_a")
    pb = _matmul(yb, p['proj_b'], 'nn', F32, "proj_b")
    pc = _matmul(yc, p['proj_c'], 'nn', F32, "proj_c")
    merged = _rowwise(_f_merge, [ma, mb, mc, pa, pb, pc], [], [x.shape[1]], [BF16], "merge_fwd",
                      col_tile=ROW_COL_TILE)[0]
    x_next = _matmul(merged, p['w_out'], 'nn', F32, "out_proj", add=x)
    sv.update(pa=pa, pb=pb, pc=pc, merged=merged)
    return x_next, sv


def _layer_bwd(dx_out, p, sv):
    width, n_in, n_pad = _layer_dims(p)
    g = {}
    dmerged = _matmul(dx_out, p['w_out'], 'nt', F32, "out_proj_dx")
    g['w_out'] = _matmul(sv['merged'], dx_out, 'tn', F32, "out_proj_dw")
    dma, dmb, dmc, dpa, dpb, dpc = _rowwise_bwd(
        _f_merge, [sv['ma'], sv['mb'], sv['mc'], sv['pa'], sv['pb'], sv['pc']], [], [dmerged], [BF16] * 6,
        "merge_bwd", col_tile=ROW_COL_TILE)
    dya = _matmul(dpa, p['proj_a'], 'nt', F32, "proj_a_dx")
    dyb = _matmul(dpb, p['proj_b'], 'nt', F32, "proj_b_dx")
    dyc = _matmul(dpc, p['proj_c'], 'nt', F32, "proj_c_dx")
    g['proj_a'] = _matmul(sv['ya'], dpa, 'tn', F32, "proj_a_dw")
    g['proj_b'] = _matmul(sv['yb'], dpb, 'tn', F32, "proj_b_dw")
    g['proj_c'] = _matmul(sv['yc'], dpc, 'tn', F32, "proj_c_dw")
    alog, dtb, gnw = p['gdn_a_log'][None], p['gdn_dt_bias'][None], p['gdn_norm_w'][None]
    dq, dk, dv, daz, db3, da3, dalog, ddtb, dgnw = _gdn_chunks_bwd(sv['sqkv'], sv['az'], sv['braw'], sv['araw'], alog,
                                                                   dtb, gnw, sv['ssa'], sv['tsa'], dya)
    gb0 = jnp.zeros((1, sv['qkv'].shape[1]), F32)
    dqkv, g['gdn_conv_w'], _ = _conv_bwd(sv['qkv'], p['gdn_conv_w'], gb0, jnp.concatenate([dq, dk, dv], axis=1),
                                         "gdn_conv_bwd")
    dbraw, daraw = jnp.sum(db3, axis=0), jnp.sum(da3, axis=0)
    g.update(gdn_a_log=dalog[0], gdn_dt_bias=ddtb[0], gdn_norm_w=dgnw[0])
    dsx, dcz, dcdt, dmalog, dmdtb, dmdsk, dmnw = _m2_chunks_bwd(
        sv['sxbc'], sv['cz'], sv['cdt'], p['m2_a_log'][None], p['m2_dt_bias'][None], p['m2_d'][None],
        p['m2_norm_w'][None], sv['ssc'], dyc)
    dcxbc, g['m2_conv_w'], dconvb = _conv_bwd(sv['cxbc'], p['m2_conv_w'], p['m2_conv_b'][None], dsx, "m2_conv_bwd")
    g.update(m2_conv_b=dconvb[0], m2_a_log=dmalog[0], m2_dt_bias=dmdtb[0], m2_d=dmdsk[0], m2_norm_w=dmnw[0])
    dyg1, dtt, dsgate, dglub = _rowwise_bwd(_f_s5_post2, [sv['yg'], sv['tt'], sv['sgate']], [p['s5_glu_b'][None]],
                                            [dyb], [F32, BF16, BF16], "s5_post2_bwd", col_tile=ROW_COL_TILE)
    dyg = _matmul(dtt, p['s5_glu_w'], 'nt', F32, "s5_glu_dx", add=dyg1)
    g['s5_glu_w'] = _matmul(sv['yg'], dtt, 'tn', F32, "s5_glu_dw")
    g['s5_glu_b'] = dglub[0]
    dymm, dsu1, dd_l = _rowwise_bwd(_f_s5_post1, [sv['ymm'], sv['su']], [sv['d_l']], [dyg], [BF16, F32],
                                    "s5_post1_bwd", col_tile=ROW_COL_TILE)
    gy = _matmul_bd(dymm, sv['c_blk'], 'nn', F32, "s5_out_dx")
    ds = _s5_scan(gy, sv['a_l'], sv['cw'], True, "s5_scan_bwd")
    da_l = _s5_da(ds, sv['s'], sv['cw'])
    dsu = _matmul_bd(ds, sv['b_blk'], 'nt', BF16, "s5_bu_dx", add=dsu1)
    ka = sv['b_blk'].shape[1]
    db_blk = _matmul_bd(sv['su'], ds, ('tn', ka, sv['cw']), F32, "s5_bu_dw")
    dc_blk = _matmul_bd(dymm, sv['s'], ('tn', ka, sv['cw']), F32, "s5_out_dw")
    for k, v in zip(('s5_lam_re', 's5_lam_im', 's5_log_step', 's5_b_re', 's5_b_im', 's5_c_re', 's5_c_im', 's5_d'),
                    sv['s5_vjp']((da_l, db_blk, dc_blk, dd_l))):
        g[k] = v
    dproj = jnp.concatenate([dqkv, daz, dcz, dsu, dsgate, dcxbc, dma, dmb, dmc, dbraw.astype(BF16), daraw.astype(BF16),
                             dcdt.astype(BF16), jnp.zeros((dqkv.shape[0], n_pad - n_in), BF16)], axis=1)
    dh = _matmul(dproj, sv['w_in'], 'nt', F32, "in_proj_dx", tm=2048, tk=IN_PROJ_TILE)
    g['w_in'] = _reorder_cols(_matmul(sv['h'], dproj, 'tn', F32, "in_proj_dw", tn=IN_PROJ_TILE), width, WORK_ORDER,
                              PROJ_ORDER, n_in)
    dx, dnw = _rowwise_bwd(_f_rms, [sv['x']], [p['norm_w'][None]], [dh], [F32], "rms_bwd", addend=dx_out)
    g['norm_w'] = dnw[0]
    return dx, g


INPUT_NAMES = (['x'] + WEIGHT_NAMES + ['loss_target'] + ['m_' + n for n in WEIGHT_NAMES]
               + ['v_' + n for n in WEIGHT_NAMES])


def _step(d):
    xi, yi, ci = _here()
    me = 2 * xi + yi
    depth = d['norm_w'].shape[0]
    big, ssm = list(BIG), list(SHARDED_SMALL)
    nsh = 4
    full, gathered = {}, {}
    halves = [d[n].astype(BF16).reshape(2, -1, d[n].shape[-1]) for n in big]
    for n, hv, got in zip(big, halves, _gather_chips_split(halves, "gather_weights")):
        gathered[n] = lax.dynamic_update_slice(got, hv[None], (me, 0, 0, 0)).reshape((nsh,) + d[n].shape)
    cg = _gather_chips(_flat_pack([d[n] for n in ssm], F32, 8), "gather_conv_weights")
    parts = [_flat_unpack(cg[j], [d[n].shape for n in ssm]) for j in range(nsh)]
    for i, n in enumerate(ssm):
        full[n] = jnp.concatenate([parts[j][i] for j in range(nsh)], axis=SHARDED_SMALL[n])
    layer_names = [n for n in WEIGHT_NAMES if n != 'final_norm_w']

    def layer_params(l):
        p = {n: (full[n][l] if n in full else d[n][l]) for n in layer_names if n not in BIG}
        for n in big:
            p[n] = jnp.concatenate([gathered[n][j, l] for j in range(nsh)], axis=BIG[n] - 1)
        return p

    x = d['x'][0]
    saved = []
    for l in range(depth):
        x, sv = _layer_fwd(x, layer_params(l))
        saved.append(sv)
    loss11, dx, dfw = _loss_and_grad(x, d['loss_target'][0], d['final_norm_w'][None])
    loss = lax.psum(loss11[0, 0], ("x", "y", "c"))
    grads = [None] * depth
    for l in reversed(range(depth)):
        dx, grads[l] = _layer_bwd(dx, layer_params(l), saved[l])
    gfull = {n: jnp.stack([grads[l][n] for l in range(depth)]) for n in layer_names if n not in BIG}
    gfull['final_norm_w'] = dfw[0]
    def shard(a, axis, j):
        wd = a.shape[axis] // nsh
        return lax.slice_in_dim(a, j * wd, (j + 1) * wd, axis=axis)

    c_idx = jnp.reshape(ci, (1,)).astype(jnp.int32)
    me_idx = jnp.reshape(me, (1,)).astype(jnp.int32)
    g4 = [jnp.stack([jnp.stack([shard(grads[l][n], BIG[n] - 1, j) for l in range(depth)]) for j in range(nsh)])
          .reshape(nsh, 2, -1, d[n].shape[-1]) for n in big]
    pairs = [_add_my_half(g, r, c_idx, "add_my_half_" + n) for n, g, r in zip(big, g4, _swap_sibling_half(g4))]
    got = _scatter_chips([pb for _, pb in pairs])
    mine = [_sum_chips(pf, gt, me_idx, "sum_chips_" + n) for n, (pf, _), gt in zip(big, pairs, got)]
    theirs = _share_sibling(mine)
    out = {}
    for n, mn, th in zip(big, mine, theirs):
        both = jnp.where(ci == 0, jnp.stack([mn, th]), jnp.stack([th, mn]))
        w2, m2, v2 = (d[pre + n].reshape(both.shape[0] * both.shape[1], both.shape[2]) for pre in ('', 'm_', 'v_'))
        g2 = both.reshape(w2.shape)
        dl, nm, nv = _adamw(w2, g2, m2, v2, "adamw_" + n)
        for key, arr in (('grad_', g2), ('delta_', dl), ('new_m_', nm), ('new_v_', nv)):
            out[key + n] = arr.reshape(d[n].shape)
    small = [n for n in WEIGHT_NAMES if n not in BIG]
    sshapes = [gfull[n].shape for n in small]
    gsm = _sum_parts(_gather_all(_flat_pack([gfull[n] for n in small], F32, 8), "gather_small_grads"), "sum_devices")
    gs = dict(zip(small, _flat_unpack(gsm, sshapes)))
    for n in ssm:
        wd = d[n].shape[SHARDED_SMALL[n]]
        gs[n] = lax.dynamic_slice_in_dim(gs[n], me * wd, wd, axis=SHARDED_SMALL[n])
    lshapes = [d[n].shape for n in small]
    wps, gps, mps, vps = (_flat_pack(arrs, F32, 16) for arrs in (
        [d[n] for n in small], [gs[n] for n in small], [d['m_' + n] for n in small], [d['v_' + n] for n in small]))
    dl, nm, nv = _adamw(wps, gps, mps, vps, "adamw_small")
    for key, arr in (('grad_', gps), ('delta_', dl), ('new_m_', nm), ('new_v_', nv)):
        for n, a in zip(small, _flat_unpack(arr, lshapes)):
            out[key + n] = a
    res = [loss, dx[None]]
    for key in ('grad_', 'delta_', 'new_m_', 'new_v_'):
        res += [out[key + n] for n in WEIGHT_NAMES]
    return tuple(res)


def kernel(x, norm_w, w_in, gdn_conv_w, gdn_a_log, gdn_dt_bias, gdn_norm_w, s5_lam_re, s5_lam_im, s5_log_step, s5_b_re, s5_b_im, s5_c_re, s5_c_im, s5_d, s5_glu_w, s5_glu_b, m2_conv_w, m2_conv_b, m2_a_log, m2_dt_bias, m2_d, m2_norm_w, proj_a, proj_b, proj_c, w_out, final_norm_w, loss_target, m_norm_w, m_w_in, m_gdn_conv_w, m_gdn_a_log, m_gdn_dt_bias, m_gdn_norm_w, m_s5_lam_re, m_s5_lam_im, m_s5_log_step, m_s5_b_re, m_s5_b_im, m_s5_c_re, m_s5_c_im, m_s5_d, m_s5_glu_w, m_s5_glu_b, m_m2_conv_w, m_m2_conv_b, m_m2_a_log, m_m2_dt_bias, m_m2_d, m_m2_norm_w, m_proj_a, m_proj_b, m_proj_c, m_w_out, m_final_norm_w, v_norm_w, v_w_in, v_gdn_conv_w, v_gdn_a_log, v_gdn_dt_bias, v_gdn_norm_w, v_s5_lam_re, v_s5_lam_im, v_s5_log_step, v_s5_b_re, v_s5_b_im, v_s5_c_re, v_s5_c_im, v_s5_d, v_s5_glu_w, v_s5_glu_b, v_m2_conv_w, v_m2_conv_b, v_m2_a_log, v_m2_dt_bias, v_m2_d, v_m2_norm_w, v_proj_a, v_proj_b, v_proj_c, v_w_out, v_final_norm_w):
    args = (x, norm_w, w_in, gdn_conv_w, gdn_a_log, gdn_dt_bias, gdn_norm_w, s5_lam_re, s5_lam_im, s5_log_step, s5_b_re, s5_b_im, s5_c_re, s5_c_im, s5_d, s5_glu_w, s5_glu_b, m2_conv_w, m2_conv_b, m2_a_log, m2_dt_bias, m2_d, m2_norm_w, proj_a, proj_b, proj_c, w_out, final_norm_w, loss_target, m_norm_w, m_w_in, m_gdn_conv_w, m_gdn_a_log, m_gdn_dt_bias, m_gdn_norm_w, m_s5_lam_re, m_s5_lam_im, m_s5_log_step, m_s5_b_re, m_s5_b_im, m_s5_c_re, m_s5_c_im, m_s5_d, m_s5_glu_w, m_s5_glu_b, m_m2_conv_w, m_m2_conv_b, m_m2_a_log, m_m2_dt_bias, m_m2_d, m_m2_norm_w, m_proj_a, m_proj_b, m_proj_c, m_w_out, m_final_norm_w, v_norm_w, v_w_in, v_gdn_conv_w, v_gdn_a_log, v_gdn_dt_bias, v_gdn_norm_w, v_s5_lam_re, v_s5_lam_im, v_s5_log_step, v_s5_b_re, v_s5_b_im, v_s5_c_re, v_s5_c_im, v_s5_d, v_s5_glu_w, v_s5_glu_b, v_m2_conv_w, v_m2_conv_b, v_m2_a_log, v_m2_dt_bias, v_m2_d, v_m2_norm_w, v_proj_a, v_proj_b, v_proj_c, v_w_out, v_final_norm_w)
    return _step(dict(zip(INPUT_NAMES, args)))
```

```python
import functools
import math
from typing import NamedTuple

import jax
import jax.numpy as jnp
import numpy as np
from jax import lax
from jax.experimental import pallas as pl
from jax.experimental.pallas import tpu as pltpu

F32 = jnp.float32
BF16 = jnp.bfloat16
HI = lax.Precision.HIGH
MESH = pl.DeviceIdType.MESH

CHUNK = 64
CONV_K = 4
NORM_EPS = 1e-6
GDN_HEAD_DIM = 128
M2_HEAD_DIM = 64
M2_STATE = 128
M2_GROUPS = 4
S5_GROUP_TILE = 8
ADAM_LR = 0.001
ADAM_B1 = 0.9
ADAM_B2 = 0.999
ADAM_EPS = 1e-08
ADAM_WD = 0.01
ADAM_STEP = 10
LANES = 128
SUBLANES = 8
VMEM_LIMIT_BYTES = 56 * 1024 * 1024

WEIGHT_NAMES = ['norm_w', 'w_in', 'gdn_conv_w', 'gdn_a_log', 'gdn_dt_bias', 'gdn_norm_w', 's5_lam_re', 's5_lam_im',
                's5_log_step', 's5_b_re', 's5_b_im', 's5_c_re', 's5_c_im', 's5_d', 's5_glu_w', 's5_glu_b',
                'm2_conv_w', 'm2_conv_b', 'm2_a_log', 'm2_dt_bias', 'm2_d', 'm2_norm_w', 'proj_a', 'proj_b',
                'proj_c', 'w_out', 'final_norm_w']
BIG = {'w_in': 2, 'proj_a': 2, 'proj_b': 2, 'proj_c': 2, 'w_out': 1, 's5_glu_w': 1}
SHARDED_SMALL = {'gdn_conv_w': 2, 'm2_conv_w': 2}


def _call(body, *, name, out_shape, grid=None, in_specs=None, out_specs=None, scratch_shapes=(), semantics=None,
          num_scalar_prefetch=None):
    params = dict(vmem_limit_bytes=VMEM_LIMIT_BYTES)
    if semantics is not None:
        params['dimension_semantics'] = semantics
    kw = dict(name=name, out_shape=out_shape, compiler_params=pltpu.CompilerParams(**params))
    if num_scalar_prefetch is not None:
        kw['grid_spec'] = pltpu.PrefetchScalarGridSpec(num_scalar_prefetch=num_scalar_prefetch, grid=grid,
                                                       in_specs=in_specs, out_specs=out_specs,
                                                       scratch_shapes=scratch_shapes)
    else:
        if grid is not None:
            kw['grid'] = grid
        if in_specs is not None:
            kw['in_specs'] = in_specs
        if out_specs is not None:
            kw['out_specs'] = out_specs
        if scratch_shapes:
            kw['scratch_shapes'] = scratch_shapes
    return pl.pallas_call(body, **kw)


def _tile(n, target, unit):
    if n <= target:
        return n
    t = (target // unit) * unit
    while t >= unit:
        if n % t == 0:
            return t
        t -= unit
    raise ValueError(f"no tile for {n} (unit {unit}, target {target})")


def _sds(shape, dtype):
    return jax.ShapeDtypeStruct(tuple(shape), dtype)


def _sigmoid(x):
    return jax.nn.sigmoid(x)


def _silu(x):
    return x * jax.nn.sigmoid(x)


def _softplus(x):
    return jnp.maximum(x, 0.0) + jnp.log(1.0 + jnp.exp(-jnp.abs(x)))


def _gelu_tanh(x):
    return 0.5 * x * (1.0 + jnp.tanh(math.sqrt(2.0 / math.pi) * (x + 0.044715 * (x * x * x))))


def _dot(a, b, dims, prec=None):
    return lax.dot_general(a, b, (dims, ((), ())), precision=prec, preferred_element_type=F32)


def _nn(a, b, prec=None):
    return _dot(a, b, ((1,), (0,)), prec)


def _nt(a, b, prec=None):
    return _dot(a, b, ((1,), (1,)), prec)


def _tn(a, b, prec=None):
    return _dot(a, b, ((0,), (0,)), prec)


IN_PROJ_TILE = 1664
MATMUL_TILES = {'nn': (1024, 1024, 2048), 'nt': (1024, 1024, 2048), 'tn': (1024, 1024, 2048)}


def _matmul(a, b, mode, out_dtype, name, tm=None, tn=None, tk=None, add=None):
    tm, tn, tk = (t if t is not None else dflt for t, dflt in zip((tm, tn, tk), MATMUL_TILES[mode]))
    if mode == 'nn':
        (m, k), (k2, n) = a.shape, b.shape
    elif mode == 'nt':
        (m, k), (n, k2) = a.shape, b.shape
    else:
        (k, m), (k2, n) = a.shape, b.shape
    assert k == k2, (a.shape, b.shape, mode)
    tm = _tile(m, tm, LANES if mode == 'tn' else SUBLANES)
    tn = _tile(n, tn, LANES)
    tk = _tile(k, tk, LANES if mode != 'tn' else SUBLANES * 2)
    nk = k // tk
    dims = {'nn': ((1,), (0,)), 'nt': ((1,), (1,)), 'tn': ((0,), (0,))}[mode]

    def body(*refs):
        a_ref, b_ref = refs[:2]
        o_ref, acc_ref = refs[-2:]
        kk = pl.program_id(2)

        @pl.when(kk == 0)
        def _():
            acc_ref[...] = jnp.zeros_like(acc_ref)

        acc_ref[...] += _dot(a_ref[...].astype(BF16), b_ref[...].astype(BF16), dims)

        @pl.when(kk == nk - 1)
        def _():
            res = acc_ref[...]
            if add is not None:
                res = res + refs[2][...].astype(F32)
            o_ref[...] = res.astype(o_ref.dtype)

    a_spec = pl.BlockSpec((tk, tm), lambda i, j, kk: (kk, i)) if mode == 'tn' else pl.BlockSpec((tm, tk), lambda i, j, kk: (i, kk))
    b_spec = pl.BlockSpec((tn, tk), lambda i, j, kk: (j, kk)) if mode == 'nt' else pl.BlockSpec((tk, tn), lambda i, j, kk: (kk, j))
    o_spec = pl.BlockSpec((tm, tn), lambda i, j, kk: (i, j))
    ops = (a, b) if add is None else (a, b, add)
    return _call(body, name=name, out_shape=_sds((m, n), out_dtype), grid=(m // tm, n // tn, nk),
                 in_specs=[a_spec, b_spec] + ([] if add is None else [o_spec]), out_specs=o_spec,
                 scratch_shapes=[pltpu.VMEM((tm, tn), F32)], semantics=("parallel", "parallel", "arbitrary"))(*ops)


class Win(NamedTuple):
    arr: jax.Array
    off: int
    width: int

    @property
    def shape(self):
        return (self.arr.shape[0], self.width)


def _win(x):
    return x if isinstance(x, Win) else Win(x, 0, x.shape[1])


def _col_tile(wins, target):
    ct = (min(target, min(w.width for w in wins)) // LANES) * LANES
    while ct > LANES and any(w.width % ct or w.off % ct for w in wins):
        ct -= LANES
    assert all(w.width % ct == 0 and w.off % ct == 0 for w in wins), [(w.off, w.width) for w in wins]
    return ct


def _wspec(rows, ct, w, row_first=True):
    base = w.off // ct
    if row_first:
        return pl.BlockSpec((rows, ct), lambda i, j: (i, base + j))
    return pl.BlockSpec((rows, ct), lambda j, i: (i, base + j))


def _matmul_bd(a, b, mode, out_dtype, name, tm=1024, add=None):
    a = _win(a)
    if mode in ('nn', 'nt'):
        t = a.shape[0]
        jn, ka, nb = b.shape
        tm = _tile(t, tm, SUBLANES)
        win, wout = (ka, nb) if mode == 'nn' else (nb, ka)
        assert a.shape[1] == jn * win and a.off % win == 0
        abase = a.off // win

        def body_add(a_ref, b_ref, c_ref, o_ref):
            o_ref[...] = (_nt(a_ref[...].astype(BF16), b_ref[...].astype(BF16)) + c_ref[...].astype(F32)).astype(o_ref.dtype)

        if add is not None:
            assert mode == 'nt'
            return _call(body_add, name=name, out_shape=_sds((t, jn * wout), out_dtype), grid=(t // tm, jn),
                         in_specs=[pl.BlockSpec((tm, win), lambda i, j: (i, abase + j)),
                                   pl.BlockSpec((None, ka, nb), lambda i, j: (j, 0, 0)),
                                   pl.BlockSpec((tm, wout), lambda i, j: (i, j))],
                         out_specs=pl.BlockSpec((tm, wout), lambda i, j: (i, j)),
                         semantics=("parallel", "parallel"))(a.arr, b, add)

        def body(a_ref, b_ref, o_ref):
            if mode == 'nn':
                o_ref[...] = _nn(a_ref[...].astype(BF16), b_ref[...].astype(BF16)).astype(o_ref.dtype)
            else:
                o_ref[...] = _nt(a_ref[...].astype(BF16), b_ref[...].astype(BF16)).astype(o_ref.dtype)

        return _call(body, name=name, out_shape=_sds((t, jn * wout), out_dtype), grid=(t // tm, jn),
                     in_specs=[pl.BlockSpec((tm, win), lambda i, j: (i, abase + j)),
                               pl.BlockSpec((None, ka, nb), lambda i, j: (j, 0, 0))],
                     out_specs=pl.BlockSpec((tm, wout), lambda i, j: (i, j)),
                     semantics=("parallel", "parallel"))(a.arr, b)
    t = a.shape[0]
    ka, nb = mode[1], mode[2]
    jn = a.shape[1] // ka
    assert b.shape[1] == jn * nb and a.off % ka == 0
    abase = a.off // ka
    tk = _tile(t, tm, SUBLANES * 2)
    nk = t // tk

    def body_tn(a_ref, b_ref, o_ref):
        @pl.when(pl.program_id(1) == 0)
        def _():
            o_ref[...] = jnp.zeros_like(o_ref)

        o_ref[...] += _tn(a_ref[...].astype(BF16), b_ref[...].astype(BF16))

    return _call(body_tn, name=name, out_shape=_sds((jn, ka, nb), F32), grid=(jn, nk),
                 in_specs=[pl.BlockSpec((tk, ka), lambda j, kk: (kk, abase + j)),
                           pl.BlockSpec((tk, nb), lambda j, kk: (kk, j))],
                 out_specs=pl.BlockSpec((None, ka, nb), lambda j, kk: (j, 0, 0)),
                 semantics=("parallel", "arbitrary"))(a.arr, b)


def _concat_cols(parts, name, tb=256):
    t = parts[0].shape[0]
    tb = _tile(t, tb, SUBLANES * 2)
    widths = [p.shape[1] for p in parts]
    assert all(w % LANES == 0 for w in widths)

    def body(*refs):
        o_ref, off = refs[-1], 0
        for r, w in zip(refs[:-1], widths):
            o_ref[:, off:off + w] = r[...]
            off += w

    return _call(body, name=name, out_shape=_sds((t, sum(widths)), parts[0].dtype), grid=(t // tb,),
                 in_specs=[pl.BlockSpec((tb, w), lambda i: (i, 0)) for w in widths],
                 out_specs=pl.BlockSpec((tb, sum(widths)), lambda i: (i, 0)), semantics=("parallel",))(*parts)


def _rowwise_tiles(rows, tb, col_tile):
    rows = [_win(r) for r in rows]
    t = rows[0].shape[0]
    tb = _tile(t, tb, SUBLANES * 2)
    if col_tile is None:
        assert all(r.off % r.width == 0 for r in rows)
        return rows, tb, None, 1
    ct = _col_tile(rows, col_tile)
    tb = _tile(t, max(tb, BLOCK_BYTES // (4 * ct)), SUBLANES * 2)
    return rows, tb, ct, rows[0].width // ct


def _rowwise(fn, rows, params, out_widths, out_dtypes, name, tb=256, col_tile=None):
    rows, tb, ct, ncol = _rowwise_tiles(rows, tb, col_tile)
    t = rows[0].shape[0]
    nr, npar = len(rows), len(params)

    def body(*refs):
        ins = [r[...].astype(F32) for r in refs[:nr + npar]]
        outs = fn(*ins)
        for o_ref, o in zip(refs[nr + npar:], outs):
            o_ref[...] = o.astype(o_ref.dtype)

    in_specs = [_wspec(tb, ct or r.width, r) for r in rows]
    in_specs += [pl.BlockSpec((1, ct or p.shape[1]), lambda i, j: (0, j)) for p in params]
    out_shape = [_sds((t, w), d) for w, d in zip(out_widths, out_dtypes)]
    out_specs = [pl.BlockSpec((tb, ct or w), lambda i, j: (i, j)) for w in out_widths]
    return _call(body, name=name, out_shape=out_shape, grid=(t // tb, ncol), in_specs=in_specs, out_specs=out_specs,
                 semantics=("parallel", "parallel"))(*[r.arr for r in rows], *params)


def _rowwise_bwd(fn, rows, params, cts, row_grad_dtypes, name, tb=256, addend=None, col_tile=None):
    rows, tb, ct, ncol = _rowwise_tiles(rows, tb, col_tile)
    t = rows[0].shape[0]
    nr, npar, nc = len(rows), len(params), len(cts)
    keep = [i for i, d in enumerate(row_grad_dtypes) if d is not None]
    nadd = 0 if addend is None else 1

    def body(*refs):
        ins = [r[...].astype(F32) for r in refs[:nr + npar]]
        ct = [r[...].astype(F32) for r in refs[nr + npar:nr + npar + nc]]
        _, vjp = jax.vjp(fn, *ins)
        grads = vjp(tuple(ct))
        out_refs = refs[nr + npar + nc + nadd:]
        for o_ref, i in zip(out_refs[:len(keep)], keep):
            g = grads[i]
            if nadd and i == 0:
                g = g + refs[nr + npar + nc][...].astype(F32)
            o_ref[...] = g.astype(o_ref.dtype)

        @pl.when(pl.program_id(1) == 0)
        def _():
            for o_ref in out_refs[len(keep):]:
                o_ref[...] = jnp.zeros_like(o_ref)

        for o_ref, g in zip(out_refs[len(keep):], grads[nr:]):
            o_ref[...] += g

    def plain(w):
        return pl.BlockSpec((tb, ct or w), lambda j, i: (i, j))

    in_specs = [_wspec(tb, ct or r.width, r, row_first=False) for r in rows]
    in_specs += [pl.BlockSpec((1, ct or p.shape[1]), lambda j, i: (0, j)) for p in params]
    in_specs += [plain(c.shape[1]) for c in cts]
    extra = []
    if nadd:
        in_specs += [plain(addend.shape[1])]
        extra = [addend]
    out_shape = [_sds(rows[i].shape, row_grad_dtypes[i]) for i in keep] + [_sds(p.shape, F32) for p in params]
    out_specs = [plain(rows[i].width) for i in keep]
    out_specs += [pl.BlockSpec((1, ct or p.shape[1]), lambda j, i: (0, j)) for p in params]
    return _call(body, name=name, out_shape=out_shape, grid=(ncol, t // tb), in_specs=in_specs, out_specs=out_specs,
                 semantics=("parallel", "arbitrary"))(*[r.arr for r in rows], *params, *cts, *extra)


def _chunk_masks(c):
    row = lax.broadcasted_iota(jnp.int32, (c, c), 0)
    col = lax.broadcasted_iota(jnp.int32, (c, c), 1)
    causal = row >= col
    strict = row > col
    return causal, strict, causal.astype(F32), (row > col).astype(F32), (row == col).astype(F32)


def _lane_pick(blk, idx):
    lane = lax.broadcasted_iota(jnp.int32, blk.shape, 1)
    return jnp.sum(jnp.where(lane == idx, blk, 0.0), axis=1, keepdims=True)


def _bdot(a, b, ca, cb, prec=None):
    return lax.dot_general(a, b, (((ca,), (cb,)), ((0,), (0,))), precision=prec, preferred_element_type=F32)


def _bnn(a, b, prec=None):
    return _bdot(a, b, 2, 1, prec)


def _bnt(a, b, prec=None):
    return _bdot(a, b, 2, 2, prec)


def _btn(a, b, prec=None):
    return _bdot(a, b, 1, 1, prec)


def _unit_lower_inverse(a_mat, eye):
    x = -a_mat
    t_inv = eye + x
    p = x
    for _ in range(int(math.log2(a_mat.shape[-1])) - 1):
        p = _bnn(p, p, HI)
        t_inv = t_inv + _bnn(t_inv, p, HI)
    return t_inv


@jax.custom_vjp
def _saved_inverse(a_mat, t_saved):
    return t_saved


def _saved_inverse_fwd(a_mat, t_saved):
    return t_saved, t_saved


def _saved_inverse_bwd(t_inv, ct):
    return -_bnt(_btn(t_inv, ct, HI), t_inv, HI), jnp.zeros_like(t_inv)


_saved_inverse.defvjp(_saved_inverse_fwd, _saved_inverse_bwd)


def _gdn_heads(q, k, v, z, braw, araw, alog, dtb, nw, s_in, t_saved=None):
    b, c, d = q.shape
    causal, strict, lower, upper_t, eye = _chunk_masks(c)
    lower_b = jnp.broadcast_to(lower[None], (b, c, c))
    qn = q * lax.rsqrt(jnp.sum(q * q, axis=-1, keepdims=True) + NORM_EPS) * (d ** -0.5)
    kn = k * lax.rsqrt(jnp.sum(k * k, axis=-1, keepdims=True) + NORM_EPS)
    beta = _sigmoid(braw)
    g = -jnp.exp(alog) * _softplus(araw + dtb)
    dlog = _bnn(lower_b, g * upper_t[None], HI)
    dm = jnp.where(causal[None], jnp.exp(dlog), 0.0)
    g_lanes = jnp.broadcast_to(g, (b, c, d))
    gc = _bnn(lower_b, g_lanes, HI)
    gl = jnp.sum(g_lanes, axis=1, keepdims=True)
    eg = jnp.exp(gc)
    kb = kn * beta
    a_mat = jnp.where(strict[None], _bnt(kb, kn) * dm, 0.0)
    t_inv = _unit_lower_inverse(a_mat, eye[None]) if t_saved is None else _saved_inverse(a_mat, t_saved)
    r = beta * (v - eg * _bnn(kn, s_in))
    v_new = _bnn(t_inv, r)
    qk = _bnt(qn, kn) * dm
    out = eg * _bnn(qn, s_in) + _bnn(qk, v_new)
    k_tail = kn * jnp.exp(gl - gc)
    s_out = s_in * jnp.exp(gl) + _btn(k_tail, v_new)
    y = out * lax.rsqrt(jnp.mean(out * out, axis=-1, keepdims=True) + NORM_EPS) * nw[None] * _silu(z)
    if t_saved is None:
        return y, s_out, t_inv
    return y, s_out


def _gdn_stack(refs, hb, d, h, first_head):
    q_ref, k_ref, v_ref, z_ref, b_ref, a_ref, alog_ref, dtb_ref = refs
    sls = [slice(i * d, (i + 1) * d) for i in range(hb)]
    heads = [first_head + i for i in range(hb)]
    wide = [jnp.stack([r[:, sl] for sl in sls]) for r in (q_ref, k_ref, v_ref, z_ref)]
    cols = [jnp.stack([_lane_pick(r[...], hd) for hd in heads]) for r in (b_ref, a_ref, alog_ref, dtb_ref)]
    return wide + cols


GDN_HEADS_PER_STEP = 8


def _gdn_chunks_fwd(sqkv, z, braw, araw, alog, dtb, nw, hb=GDN_HEADS_PER_STEP):
    t, w3 = sqkv.shape
    w = w3 // 3
    d = GDN_HEAD_DIM
    h = w // d
    hb = min(hb, h)
    hg = h // hb
    c = CHUNK
    nc = t // c

    def body(q_ref, k_ref, v_ref, z_ref, b_ref, a_ref, alog_ref, dtb_ref, nw_ref, y_ref, ssave_ref, tsave_ref,
             s_ref):
        @pl.when(pl.program_id(1) == 0)
        def _():
            s_ref[...] = jnp.zeros_like(s_ref)

        s_in = s_ref[...]
        ssave_ref[...] = s_in
        args = _gdn_stack((q_ref, k_ref, v_ref, z_ref, b_ref, a_ref, alog_ref, dtb_ref), hb, d, h,
                          pl.program_id(0) * hb)
        y, s_out, t_inv = _gdn_heads(*args, nw_ref[...], s_in)
        for i in range(hb):
            y_ref[:, i * d:(i + 1) * d] = y[i]
        s_ref[...] = s_out
        tsave_ref[...] = t_inv

    blk = (c, hb * d)
    z = _win(z)
    zb = z.off // (hb * d)
    assert z.off % (hb * d) == 0
    in_specs = [pl.BlockSpec(blk, lambda g, n: (n, g)), pl.BlockSpec(blk, lambda g, n: (n, hg + g)),
                pl.BlockSpec(blk, lambda g, n: (n, 2 * hg + g)), pl.BlockSpec(blk, lambda g, n: (n, zb + g)),
                pl.BlockSpec((c, h), lambda g, n: (n, 0)), pl.BlockSpec((c, h), lambda g, n: (n, 0)),
                pl.BlockSpec((1, h), lambda g, n: (0, 0)), pl.BlockSpec((1, h), lambda g, n: (0, 0)),
                pl.BlockSpec((1, d), lambda g, n: (0, 0))]
    out_shape = [_sds((t, w), F32), _sds((hg, nc, hb, d, d), F32), _sds((hg, nc, hb, c, c), F32)]
    out_specs = [pl.BlockSpec(blk, lambda g, n: (n, g)),
                 pl.BlockSpec((None, None, hb, d, d), lambda g, n: (g, n, 0, 0, 0)),
                 pl.BlockSpec((None, None, hb, c, c), lambda g, n: (g, n, 0, 0, 0))]
    return _call(body, name="gdn_chunks_fwd", out_shape=out_shape, grid=(hg, nc), in_specs=in_specs,
                 out_specs=out_specs, scratch_shapes=[pltpu.VMEM((hb, d, d), F32)],
                 semantics=("parallel", "arbitrary"))(sqkv, sqkv, sqkv, z.arr, braw, araw, alog, dtb, nw)


def _gdn_chunks_bwd(sqkv, z, braw, araw, alog, dtb, nw, ssave, tsave, dy, hb=GDN_HEADS_PER_STEP):
    t, w3 = sqkv.shape
    w = w3 // 3
    d = GDN_HEAD_DIM
    h = w // d
    hb = min(hb, h)
    hg = h // hb
    c = CHUNK
    nc = t // c

    def body(q_ref, k_ref, v_ref, z_ref, b_ref, a_ref, alog_ref, dtb_ref, nw_ref, ssave_ref, tsave_ref, dy_ref,
             dsq_ref, dz_ref, db_ref, da_ref, dalog_ref, ddtb_ref, dnw_ref, ds_ref):
        first = jnp.logical_and(pl.program_id(0) == 0, pl.program_id(1) == 0)

        @pl.when(pl.program_id(1) == 0)
        def _():
            ds_ref[...] = jnp.zeros_like(ds_ref)

        @pl.when(first)
        def _():
            dalog_ref[...] = jnp.zeros_like(dalog_ref)
            ddtb_ref[...] = jnp.zeros_like(ddtb_ref)
            dnw_ref[...] = jnp.zeros_like(dnw_ref)

        lane_h = lax.broadcasted_iota(jnp.int32, (1, h), 1)
        db_acc = jnp.zeros((c, h), F32)
        da_acc = jnp.zeros((c, h), F32)
        args = _gdn_stack((q_ref, k_ref, v_ref, z_ref, b_ref, a_ref, alog_ref, dtb_ref), hb, d, h,
                          pl.program_id(0) * hb)
        t_saved = tsave_ref[...]
        _, vjp = jax.vjp(lambda *a: _gdn_heads(*a, t_saved=t_saved), *args, nw_ref[...], ssave_ref[...])
        dyb = jnp.stack([dy_ref[:, i * d:(i + 1) * d] for i in range(hb)])
        dq, dk, dv, dz, db, da, dalog, ddtb, dnw, ds_in = vjp((dyb, ds_ref[...]))
        for i in range(hb):
            sl = slice(i * d, (i + 1) * d)
            dsq_ref[:, i * d:(i + 1) * d] = dq[i]
            dsq_ref[:, w + i * d:w + (i + 1) * d] = dk[i]
            dsq_ref[:, 2 * w + i * d:2 * w + (i + 1) * d] = dv[i]
            dz_ref[:, sl] = dz[i].astype(dz_ref.dtype)
            onehot = (lane_h == pl.program_id(0) * hb + i).astype(F32)
            db_acc = db_acc + db[i] * onehot
            da_acc = da_acc + da[i] * onehot
            dalog_ref[...] += dalog[i] * onehot
            ddtb_ref[...] += ddtb[i] * onehot
        dnw_ref[...] += dnw
        ds_ref[...] = ds_in
        db_ref[...] = db_acc
        da_ref[...] = da_acc

    blk = (c, hb * d)
    rev = lambda n: nc - 1 - n
    z = _win(z)
    zb = z.off // (hb * d)
    assert z.off % (hb * d) == 0
    in_specs = [pl.BlockSpec(blk, lambda g, n: (rev(n), g)), pl.BlockSpec(blk, lambda g, n: (rev(n), hg + g)),
                pl.BlockSpec(blk, lambda g, n: (rev(n), 2 * hg + g)), pl.BlockSpec(blk, lambda g, n: (rev(n), zb + g)),
                pl.BlockSpec((c, h), lambda g, n: (rev(n), 0)), pl.BlockSpec((c, h), lambda g, n: (rev(n), 0)),
                pl.BlockSpec((1, h), lambda g, n: (0, 0)), pl.BlockSpec((1, h), lambda g, n: (0, 0)),
                pl.BlockSpec((1, d), lambda g, n: (0, 0)),
                pl.BlockSpec((None, None, hb, d, d), lambda g, n: (g, rev(n), 0, 0, 0)),
                pl.BlockSpec((None, None, hb, c, c), lambda g, n: (g, rev(n), 0, 0, 0)),
                pl.BlockSpec(blk, lambda g, n: (rev(n), g))]
    assert hg == 1
    out_shape = [_sds((t, w3), F32), _sds((t, w), BF16),
                 _sds((hg, t, h), F32), _sds((hg, t, h), F32), _sds((1, h), F32), _sds((1, h), F32), _sds((1, d), F32)]
    out_specs = [pl.BlockSpec((c, w3), lambda g, n: (rev(n), 0)), pl.BlockSpec(blk, lambda g, n: (rev(n), g))]
    out_specs += [pl.BlockSpec((None, c, h), lambda g, n: (g, rev(n), 0))] * 2
    out_specs += [pl.BlockSpec((1, h), lambda g, n: (0, 0)), pl.BlockSpec((1, h), lambda g, n: (0, 0)),
                  pl.BlockSpec((1, d), lambda g, n: (0, 0))]
    return _call(body, name="gdn_chunks_bwd", out_shape=out_shape, grid=(hg, nc), in_specs=in_specs,
                 out_specs=out_specs, scratch_shapes=[pltpu.VMEM((hb, d, d), F32)],
                 semantics=("arbitrary", "arbitrary"))(sqkv, sqkv, sqkv, z.arr, braw, araw, alog, dtb, nw, ssave, tsave,
                                                       dy)


def _m2_groups(xs, z, bm, cm, dtraws, alogs, dtbs, dsks, nw, st):
    g, c, gw = xs.shape
    rep = len(dtraws)
    causal, _, lower, upper_t, _ = _chunk_masks(c)
    lower_b = jnp.broadcast_to(lower[None], (g, c, c))
    lane_head = lax.broadcasted_iota(jnp.int32, (1, 1, gw), 2) // M2_HEAD_DIM

    def expand(cols):
        res = jnp.broadcast_to(cols[-1], (g, cols[-1].shape[1], gw))
        for i in reversed(range(rep - 1)):
            res = jnp.where(lane_head == i, cols[i], res)
        return res

    dts = [_softplus(dtraws[i] + dtbs[i]) for i in range(rep)]
    adts = [-jnp.exp(alogs[i]) * dts[i] for i in range(rep)]
    dt_l, adt_l, dsk_l = expand(dts), expand(adts), expand(dsks)
    xdt = xs * dt_l
    acum = _bnn(lower_b, adt_l, HI)
    alast = jnp.sum(adt_l, axis=1, keepdims=True)
    scores = _bnt(cm, bm)
    y = jnp.exp(acum) * _bnn(cm, st) + dsk_l * xs
    for i in range(rep):
        seg = jnp.where(causal[None], jnp.exp(_bnn(lower_b, adts[i] * upper_t[None], HI)), 0.0)
        y = y + _bnn(scores * seg, jnp.where(lane_head == i, xdt, 0.0))
    st_out = st * jnp.exp(alast) + _btn(bm, xdt * jnp.exp(alast - acum))
    y2 = y * _silu(z)
    out = y2 * lax.rsqrt(jnp.mean(y2 * y2, axis=-1, keepdims=True) + NORM_EPS) * nw
    return out, st_out


def _m2_dims(sxbc, z):
    t = sxbc.shape[0]
    w2 = z.shape[1]
    g = M2_GROUPS
    n = M2_STATE
    assert sxbc.shape[1] == w2 + 2 * g * n
    gw = w2 // g
    return t, w2, g, n, gw, gw // M2_HEAD_DIM, t // CHUNK


def _m2_args(refs, g, n, gw, rep, st):
    sx_ref, z_ref, dt_ref, alog_ref, dtb_ref, dsk_ref, nw_ref = refs
    w2 = g * gw
    xs = jnp.stack([sx_ref[:, i * gw:(i + 1) * gw] for i in range(g)])
    z = jnp.stack([z_ref[:, i * gw:(i + 1) * gw] for i in range(g)])
    bm = jnp.stack([sx_ref[:, w2 + i * n:w2 + (i + 1) * n] for i in range(g)])
    cm = jnp.stack([sx_ref[:, w2 + (g + i) * n:w2 + (g + i + 1) * n] for i in range(g)])
    nw = jnp.stack([nw_ref[:, i * gw:(i + 1) * gw] for i in range(g)])

    def cols(ref):
        blk = ref[...]
        return [jnp.stack([_lane_pick(blk, gi * rep + i) for gi in range(g)]) for i in range(rep)]

    return xs, z, bm, cm, cols(dt_ref), cols(alog_ref), cols(dtb_ref), cols(dsk_ref), nw, st


def _m2_chunks_fwd(sxbc, z, dtraw, alog, dtb, dsk, nw):
    t, w2, g, n, gw, rep, nc = _m2_dims(sxbc, z)
    hm = dtraw.shape[1]
    c = CHUNK
    wx = sxbc.shape[1]

    def body(sx_ref, z_ref, dt_ref, alog_ref, dtb_ref, dsk_ref, nw_ref, y_ref, ssave_ref, st_ref):
        @pl.when(pl.program_id(0) == 0)
        def _():
            st_ref[...] = jnp.zeros_like(st_ref)

        st = st_ref[...]
        ssave_ref[...] = st
        y, st_out = _m2_groups(*_m2_args((sx_ref, z_ref, dt_ref, alog_ref, dtb_ref, dsk_ref, nw_ref), g, n, gw, rep, st))
        for i in range(g):
            y_ref[:, i * gw:(i + 1) * gw] = y[i]
        st_ref[...] = st_out

    z = _win(z)
    zb = z.off // w2
    assert z.off % w2 == 0
    in_specs = [pl.BlockSpec((c, wx), lambda k: (k, 0)), pl.BlockSpec((c, w2), lambda k: (k, zb)),
                pl.BlockSpec((c, hm), lambda k: (k, 0)),
                pl.BlockSpec((1, hm), lambda k: (0, 0)), pl.BlockSpec((1, hm), lambda k: (0, 0)),
                pl.BlockSpec((1, hm), lambda k: (0, 0)), pl.BlockSpec((1, w2), lambda k: (0, 0))]
    out_shape = [_sds((t, w2), F32), _sds((nc, g, n, gw), F32)]
    out_specs = [pl.BlockSpec((c, w2), lambda k: (k, 0)), pl.BlockSpec((None, g, n, gw), lambda k: (k, 0, 0, 0))]
    return _call(body, name="m2_chunks_fwd", out_shape=out_shape, grid=(nc,), in_specs=in_specs,
                 out_specs=out_specs, scratch_shapes=[pltpu.VMEM((g, n, gw), F32)],
                 semantics=("arbitrary",))(sxbc, z.arr, dtraw, alog, dtb, dsk, nw)


def _m2_chunks_bwd(sxbc, z, dtraw, alog, dtb, dsk, nw, ssave, dy):
    t, w2, g, n, gw, rep, nc = _m2_dims(sxbc, z)
    hm = dtraw.shape[1]
    c = CHUNK

    wx = sxbc.shape[1]

    def body(sx_ref, z_ref, dt_ref, alog_ref, dtb_ref, dsk_ref, nw_ref, ssave_ref, dy_ref,
             dsx_ref, dz_ref, ddt_ref, dalog_ref, ddtb_ref, ddsk_ref, dnw_ref, dst_ref):
        @pl.when(pl.program_id(0) == 0)
        def _():
            dst_ref[...] = jnp.zeros_like(dst_ref)
            dnw_ref[...] = jnp.zeros_like(dnw_ref)
            dalog_ref[...] = jnp.zeros_like(dalog_ref)
            ddtb_ref[...] = jnp.zeros_like(ddtb_ref)
            ddsk_ref[...] = jnp.zeros_like(ddsk_ref)

        args = _m2_args((sx_ref, z_ref, dt_ref, alog_ref, dtb_ref, dsk_ref, nw_ref), g, n, gw, rep, ssave_ref[...])
        _, vjp = jax.vjp(_m2_groups, *args)
        dyb = jnp.stack([dy_ref[:, i * gw:(i + 1) * gw] for i in range(g)])
        dxs, dz, dbm, dcm, ddts, dalogs, ddtbs, ddsks, dnw, dst = vjp((dyb, dst_ref[...]))
        dst_ref[...] = dst
        lane_h = lax.broadcasted_iota(jnp.int32, (1, hm), 1)
        ddt = jnp.zeros((c, hm), F32)
        for gi in range(g):
            dsx_ref[:, gi * gw:(gi + 1) * gw] = dxs[gi]
            dsx_ref[:, w2 + gi * n:w2 + (gi + 1) * n] = dbm[gi]
            dsx_ref[:, w2 + (g + gi) * n:w2 + (g + gi + 1) * n] = dcm[gi]
            dz_ref[:, gi * gw:(gi + 1) * gw] = dz[gi].astype(dz_ref.dtype)
            dnw_ref[:, gi * gw:(gi + 1) * gw] += dnw[gi]
            for i in range(rep):
                onehot = (lane_h == gi * rep + i).astype(F32)
                ddt = ddt + ddts[i][gi] * onehot
                dalog_ref[...] += dalogs[i][gi] * onehot
                ddtb_ref[...] += ddtbs[i][gi] * onehot
                ddsk_ref[...] += ddsks[i][gi] * onehot
        ddt_ref[...] = ddt

    rev = lambda k: nc - 1 - k
    z = _win(z)
    zb = z.off // w2
    assert z.off % w2 == 0
    in_specs = [pl.BlockSpec((c, wx), lambda k: (rev(k), 0)), pl.BlockSpec((c, w2), lambda k: (rev(k), zb)),
                pl.BlockSpec((c, hm), lambda k: (rev(k), 0)),
                pl.BlockSpec((1, hm), lambda k: (0, 0)), pl.BlockSpec((1, hm), lambda k: (0, 0)),
                pl.BlockSpec((1, hm), lambda k: (0, 0)), pl.BlockSpec((1, w2), lambda k: (0, 0)),
                pl.BlockSpec((None, g, n, gw), lambda k: (rev(k), 0, 0, 0)),
                pl.BlockSpec((c, w2), lambda k: (rev(k), 0))]
    out_shape = [_sds((t, wx), F32), _sds((t, w2), BF16), _sds((t, hm), F32), _sds((1, hm), F32), _sds((1, hm), F32),
                 _sds((1, hm), F32), _sds((1, w2), F32)]
    out_specs = [pl.BlockSpec((c, wx), lambda k: (rev(k), 0)), pl.BlockSpec((c, w2), lambda k: (rev(k), 0)),
                 pl.BlockSpec((c, hm), lambda k: (rev(k), 0)),
                 pl.BlockSpec((1, hm), lambda k: (0, 0)), pl.BlockSpec((1, hm), lambda k: (0, 0)),
                 pl.BlockSpec((1, hm), lambda k: (0, 0)), pl.BlockSpec((1, w2), lambda k: (0, 0))]
    return _call(body, name="m2_chunks_bwd", out_shape=out_shape, grid=(nc,), in_specs=in_specs,
                 out_specs=out_specs, scratch_shapes=[pltpu.VMEM((g, n, gw), F32)],
                 semantics=("arbitrary",))(sxbc, z.arr, dtraw, alog, dtb, dsk, nw, ssave, dy)


def _s5_scan(bu, a_l, cw, reverse, name, tb=256):
    t, wtot = bu.shape
    jn = wtot // cw
    half = cw // 2
    tb = _tile(t, tb, SUBLANES)
    nb = t // tb

    def body(bu_ref, a_ref, s_ref, st_ref):
        @pl.when(pl.program_id(1) == 0)
        def _():
            st_ref[...] = jnp.zeros_like(st_ref)

        are = a_ref[:, :half]
        aim = -a_ref[:, half:] if reverse else a_ref[:, half:]

        def step(k, carry):
            sre, sim = carry
            r = tb - 1 - k if reverse else k
            nre = are * sre - aim * sim + bu_ref[pl.ds(r, 1), pl.ds(0, half)]
            nim = are * sim + aim * sre + bu_ref[pl.ds(r, 1), pl.ds(half, half)]
            s_ref[pl.ds(r, 1), pl.ds(0, half)] = nre
            s_ref[pl.ds(r, 1), pl.ds(half, half)] = nim
            return nre, nim

        sre, sim = lax.fori_loop(0, tb, step, (st_ref[:, :half], st_ref[:, half:]), unroll=8)
        st_ref[:, :half] = sre
        st_ref[:, half:] = sim

    rb = (lambda i: nb - 1 - i) if reverse else (lambda i: i)
    return _call(body, name=name, out_shape=_sds((t, wtot), F32), grid=(jn, nb),
                 in_specs=[pl.BlockSpec((tb, cw), lambda j, i: (rb(i), j)), pl.BlockSpec((1, cw), lambda j, i: (0, j))],
                 out_specs=pl.BlockSpec((tb, cw), lambda j, i: (rb(i), j)),
                 scratch_shapes=[pltpu.VMEM((1, cw), F32)], semantics=("parallel", "arbitrary"))(bu, a_l)


def _s5_da(ds, s, cw, tb=256):
    t, wtot = ds.shape
    jn = wtot // cw
    half = cw // 2
    tb = _tile(t, tb, SUBLANES)
    nb = t // tb
    hb = tb // SUBLANES

    def body(ds_ref, s_ref, halo_ref, o_ref):
        i = pl.program_id(1)

        @pl.when(i == 0)
        def _():
            o_ref[...] = jnp.zeros_like(o_ref)

        prev = jnp.where(i == 0, 0.0, halo_ref[SUBLANES - 1:SUBLANES, :])
        row = lax.broadcasted_iota(jnp.int32, (tb, cw), 0)
        sh = jnp.where(row == 0, prev, pltpu.roll(s_ref[...], 1, 0))
        d = ds_ref[...]
        dre, dim, sre, sim = d[:, :half], d[:, half:], sh[:, :half], sh[:, half:]
        o_ref[:, :half] += jnp.sum(dre * sre + dim * sim, axis=0, keepdims=True)
        o_ref[:, half:] += jnp.sum(dim * sre - dre * sim, axis=0, keepdims=True)

    return _call(body, name="s5_da", out_shape=_sds((1, wtot), F32), grid=(jn, nb),
                 in_specs=[pl.BlockSpec((tb, cw), lambda j, i: (i, j)), pl.BlockSpec((tb, cw), lambda j, i: (i, j)),
                           pl.BlockSpec((SUBLANES, cw), lambda j, i: (jnp.maximum(i * hb - 1, 0), j))],
                 out_specs=pl.BlockSpec((1, cw), lambda j, i: (0, j)),
                 semantics=("parallel", "arbitrary"))(ds, s, s)


def _conv_rows(t, tb):
    tb = _tile(t, tb, SUBLANES * 2)
    return tb, t // tb, tb // SUBLANES


def _shift_down(x, above, s):
    n = x.shape[0]
    y = pltpu.roll(x, s, 0)
    row = lax.broadcasted_iota(jnp.int32, above.shape, 0)
    head = jnp.where(row < s, pltpu.roll(above, s, 0), y[:SUBLANES])
    return head if n == SUBLANES else jnp.concatenate([head, y[SUBLANES:]], axis=0)


def _shift_up(x, below, s):
    n = x.shape[0]
    y = pltpu.roll(x, n - s, 0)
    row = lax.broadcasted_iota(jnp.int32, below.shape, 0)
    tail = jnp.where(row >= SUBLANES - s, pltpu.roll(below, SUBLANES - s, 0), y[n - SUBLANES:])
    return tail if n == SUBLANES else jnp.concatenate([y[:n - SUBLANES], tail], axis=0)


def _conv_taps(x, above, w_ref, b_ref):
    xs = [x] + [_shift_down(x, above, s) for s in range(1, CONV_K)]
    c = b_ref[...] + w_ref[CONV_K - 1:CONV_K, :] * x
    for s in range(1, CONV_K):
        c = c + w_ref[CONV_K - 1 - s:CONV_K - s, :] * xs[s]
    return c, xs


CONV_COL_TILE = 1024


def _conv_specs(x, tb):
    x = _win(x)
    t, cwid = x.shape
    ct = _col_tile([x], CONV_COL_TILE)
    tb, nb, hb = _conv_rows(t, max(tb, BLOCK_BYTES // (4 * ct)))
    base = x.off // ct
    blk_x = pl.BlockSpec((tb, ct), lambda j, i: (i, base + j))
    prev_x = pl.BlockSpec((SUBLANES, ct), lambda j, i: (jnp.maximum(i * hb - 1, 0), base + j))
    next_x = pl.BlockSpec((SUBLANES, ct), lambda j, i: (jnp.minimum((i + 1) * hb, nb * hb - 1), base + j))
    blk = pl.BlockSpec((tb, ct), lambda j, i: (i, j))
    nxt = pl.BlockSpec((SUBLANES, ct), lambda j, i: (jnp.minimum((i + 1) * hb, nb * hb - 1), j))
    taps = pl.BlockSpec((CONV_K, ct), lambda j, i: (0, j))
    bias = pl.BlockSpec((1, ct), lambda j, i: (0, j))
    return x, tb, nb, cwid // ct, dict(blk_x=blk_x, prev_x=prev_x, next_x=next_x, blk=blk, nxt=nxt, taps=taps, bias=bias)


def _conv_fwd(x, w, b, name, tb=256):
    x, tb, nb, ncol, sp = _conv_specs(x, tb)
    t, cwid = x.shape

    def body(x_ref, halo_ref, w_ref, b_ref, o_ref):
        above = jnp.where(pl.program_id(1) == 0, 0.0, halo_ref[...])
        o_ref[...] = _silu(_conv_taps(x_ref[...], above, w_ref, b_ref)[0])

    return _call(body, name=name, out_shape=_sds((t, cwid), F32), grid=(ncol, nb),
                 in_specs=[sp['blk_x'], sp['prev_x'], sp['taps'], sp['bias']], out_specs=sp['blk'],
                 semantics=("parallel", "parallel"))(x.arr, x.arr, w, b)


def _dsilu(c):
    sg = _sigmoid(c)
    return sg * (1.0 + c * (1.0 - sg))


def _conv_bwd(x, w, b, ds, name, tb=256):
    x, tb, nb, ncol, sp = _conv_specs(x, tb)
    t, cwid = x.shape

    def body(x_ref, halo_ref, xn_ref, ds_ref, dsn_ref, w_ref, b_ref, dx_ref, dw_ref, db_ref):
        i = pl.program_id(1)

        @pl.when(i == 0)
        def _():
            dw_ref[...] = jnp.zeros_like(dw_ref)
            db_ref[...] = jnp.zeros_like(db_ref)

        x = x_ref[...]
        above = jnp.where(i == 0, 0.0, halo_ref[...])
        c, xs = _conv_taps(x, above, w_ref, b_ref)
        dc = ds_ref[...] * _dsilu(c)
        cn, _ = _conv_taps(xn_ref[...], x[tb - SUBLANES:], w_ref, b_ref)
        dcn = jnp.where(i == nb - 1, 0.0, dsn_ref[...] * _dsilu(cn))
        dx = w_ref[CONV_K - 1:CONV_K, :] * dc
        for s in range(1, CONV_K):
            dx = dx + w_ref[CONV_K - 1 - s:CONV_K - s, :] * _shift_up(dc, dcn, s)
        dx_ref[...] = dx.astype(dx_ref.dtype)
        for s in range(CONV_K):
            dw_ref[CONV_K - 1 - s:CONV_K - s, :] += jnp.sum(dc * xs[s], axis=0, keepdims=True)
        db_ref[...] += jnp.sum(dc, axis=0, keepdims=True)

    return _call(body, name=name, out_shape=[_sds((t, cwid), BF16), _sds((CONV_K, cwid), F32), _sds((1, cwid), F32)],
                 grid=(ncol, nb),
                 in_specs=[sp['blk_x'], sp['prev_x'], sp['next_x'], sp['blk'], sp['nxt'], sp['taps'], sp['bias']],
                 out_specs=[sp['blk'], sp['taps'], sp['bias']],
                 semantics=("parallel", "arbitrary"))(x.arr, x.arr, x.arr, ds, ds, w, b)


def _f_rms(x, w):
    return (x * lax.rsqrt(jnp.mean(x * x, axis=-1, keepdims=True) + NORM_EPS) * w,)


def _f_s5_post1(ymm, u, d_l):
    return (_gelu_tanh(ymm + d_l * u),)


def _f_s5_post2(yg, tt, gate, b):
    return (yg * _sigmoid(tt + b) * _silu(gate),)


def _f_merge(ma, mb, mc, pa, pb, pc):
    return (_sigmoid(ma) * pa + _sigmoid(mb) * pb + _sigmoid(mc) * pc,)


def _loss_and_grad(x, tgt, fw, tb=256):
    t, dm = x.shape
    tb = _tile(t, tb, SUBLANES * 2)

    def f(xb, wb, tb_):
        y = _f_rms(xb, wb)[0]
        e = y - tb_
        return 0.5 * jnp.sum(jnp.mean(e * e, axis=-1, keepdims=True), axis=0, keepdims=True)

    def body(x_ref, t_ref, w_ref, loss_ref, dx_ref, dw_ref):
        @pl.when(pl.program_id(0) == 0)
        def _():
            loss_ref[...] = jnp.zeros_like(loss_ref)
            dw_ref[...] = jnp.zeros_like(dw_ref)

        tgt_b = t_ref[...]
        val, vjp = jax.vjp(lambda a, b: f(a, b, tgt_b), x_ref[...], w_ref[...])
        dxb, dwb = vjp(jnp.ones((1, 1), F32))
        loss_ref[...] += val
        dx_ref[...] = dxb
        dw_ref[...] += dwb

    return _call(body, name="loss_and_grad", out_shape=[_sds((1, 1), F32), _sds((t, dm), F32), _sds((1, dm), F32)],
                 grid=(t // tb,),
                 in_specs=[pl.BlockSpec((tb, dm), lambda i: (i, 0)), pl.BlockSpec((tb, dm), lambda i: (i, 0)),
                           pl.BlockSpec((1, dm), lambda i: (0, 0))],
                 out_specs=[pl.BlockSpec((1, 1), lambda i: (0, 0)), pl.BlockSpec((tb, dm), lambda i: (i, 0)),
                            pl.BlockSpec((1, dm), lambda i: (0, 0))],
                 semantics=("arbitrary",))(x, tgt, fw)


FLAT_W = 1024


def _sum_parts(parts, name, tb=256):
    n, r, wd = parts.shape
    tb = _tile(r, tb, SUBLANES)

    def body(p_ref, o_ref):
        acc = p_ref[0]
        for k in range(1, n):
            acc = acc + p_ref[k]
        o_ref[...] = acc

    return _call(body, name=name, out_shape=_sds((r, wd), F32), grid=(r // tb,),
                 in_specs=[pl.BlockSpec((n, tb, wd), lambda i: (0, i, 0))],
                 out_specs=pl.BlockSpec((tb, wd), lambda i: (i, 0)), semantics=("parallel",))(parts)


BLOCK_BYTES = 1 << 20


def _rows_per_block(r, wd):
    return _tile(r, max(SUBLANES * 2, BLOCK_BYTES // (4 * wd) // (SUBLANES * 2) * (SUBLANES * 2)), SUBLANES * 2)


def _add_my_half(g4, recv, c_idx, name):
    p, _, r, wd = g4.shape
    tb = _rows_per_block(r, wd)

    def body(c_ref, g_ref, r_ref, o_ref, ob_ref):
        s = g_ref[...] + r_ref[...]
        o_ref[...] = s
        ob_ref[...] = s.astype(BF16)

    spec = pl.BlockSpec((None, tb, wd), lambda j, i, c_ref: (j, i, 0))
    return _call(body, name=name, out_shape=[_sds((p, r, wd), F32), _sds((p, r, wd), BF16)], grid=(p, r // tb),
                 in_specs=[pl.BlockSpec((None, None, tb, wd), lambda j, i, c_ref: (j, c_ref[0], i, 0)), spec],
                 out_specs=[spec, spec], semantics=("parallel", "parallel"), num_scalar_prefetch=1)(c_idx, g4, recv)


def _sum_chips(own, got, me_idx, name):
    p, r, wd = own.shape
    tb = _rows_per_block(r, wd)

    def body(me_ref, own_ref, got_ref, o_ref):
        me = me_ref[0]
        acc = None
        for k in range(p):
            part = jnp.where(me == k, own_ref[...], got_ref[k].astype(F32))
            acc = part if acc is None else acc + part
        o_ref[...] = acc

    return _call(body, name=name, out_shape=_sds((r, wd), F32), grid=(r // tb,),
                 in_specs=[pl.BlockSpec((None, tb, wd), lambda i, me_ref: (me_ref[0], i, 0)),
                           pl.BlockSpec((p, tb, wd), lambda i, me_ref: (0, i, 0))],
                 out_specs=pl.BlockSpec((tb, wd), lambda i, me_ref: (i, 0)),
                 semantics=("parallel",), num_scalar_prefetch=1)(me_idx, own, got)


def _adamw(w, g, m, v, name):
    r, wd = w.shape
    tb = _rows_per_block(r, wd)

    def body(w_ref, g_ref, m_ref, v_ref, d_ref, nm_ref, nv_ref):
        gg = g_ref[...]
        nm = ADAM_B1 * m_ref[...] + (1.0 - ADAM_B1) * gg
        nv = ADAM_B2 * v_ref[...] + (1.0 - ADAM_B2) * (gg * gg)
        m_hat = nm / (1.0 - ADAM_B1 ** ADAM_STEP)
        v_hat = nv / (1.0 - ADAM_B2 ** ADAM_STEP)
        d_ref[...] = -ADAM_LR * (m_hat / (jnp.sqrt(v_hat) + ADAM_EPS) + ADAM_WD * w_ref[...])
        nm_ref[...] = nm
        nv_ref[...] = nv

    spec = pl.BlockSpec((tb, wd), lambda i: (i, 0))
    return _call(body, name=name, out_shape=[_sds((r, wd), F32)] * 3, grid=(r // tb,), in_specs=[spec] * 4,
                 out_specs=[spec] * 3, semantics=("parallel",))(w, g, m, v)


def _here():
    return lax.axis_index("x"), lax.axis_index("y"), lax.axis_index("c")


def _comm_call(body, name, out_shape, n_sems, operands):
    anyspec = pl.BlockSpec(memory_space=pl.ANY)
    outs = out_shape if isinstance(out_shape, (list, tuple)) else [out_shape]
    return _call(body, name=name, out_shape=out_shape, in_specs=[anyspec] * len(operands),
                 out_specs=[anyspec] * len(outs) if isinstance(out_shape, (list, tuple)) else anyspec,
                 scratch_shapes=[pltpu.SemaphoreType.DMA((n_sems,)), pltpu.SemaphoreType.DMA((n_sems,)),
                                 pltpu.SemaphoreType.DMA(())])(*operands)


def _gather_chips(x, name):
    def body(x_ref, o_ref, send_sems, recv_sems, local_sem):
        xi, yi, ci = _here()
        chips = [(1 - xi, yi), (xi, 1 - yi), (1 - xi, 1 - yi)]
        mine = pltpu.make_async_copy(x_ref, o_ref.at[2 * xi + yi], local_sem)
        mine.start()

        def copy(k, slot, to):
            return pltpu.make_async_remote_copy(src_ref=x_ref, dst_ref=o_ref.at[slot], send_sem=send_sems.at[k],
                                                recv_sem=recv_sems.at[k], device_id=to, device_id_type=MESH)

        sends = [copy(k, 2 * xi + yi, (px, py, ci)) for k, (px, py) in enumerate(chips)]
        for cp in sends:
            cp.start()
        for k, (px, py) in enumerate(chips):
            copy(k, 2 * px + py, (px, py, ci)).wait_recv()
        for cp in sends:
            cp.wait_send()
        mine.wait()

    return _comm_call(body, name, _sds((4,) + x.shape, x.dtype), 3, (x,))


def _gather_all(x, name):
    def body(x_ref, o_ref, send_sems, recv_sems, local_sem):
        xi, yi, ci = _here()
        me = 4 * xi + 2 * yi + ci
        flips = [(fx, fy, fc) for fx in (0, 1) for fy in (0, 1) for fc in (0, 1)][1:]
        peers = [((1 - xi) if fx else xi, (1 - yi) if fy else yi, (1 - ci) if fc else ci) for fx, fy, fc in flips]
        mine = pltpu.make_async_copy(x_ref, o_ref.at[me], local_sem)
        mine.start()

        def copy(k, slot, to):
            return pltpu.make_async_remote_copy(src_ref=x_ref, dst_ref=o_ref.at[slot], send_sem=send_sems.at[k],
                                                recv_sem=recv_sems.at[k], device_id=to, device_id_type=MESH)

        sends = [copy(k, me, p) for k, p in enumerate(peers)]
        for cp in sends:
            cp.start()
        for k, (px, py, pc) in enumerate(peers):
            copy(k, 4 * px + 2 * py + pc, (px, py, pc)).wait_recv()
        for cp in sends:
            cp.wait_send()
        mine.wait()

    return _comm_call(body, name, _sds((8,) + x.shape, x.dtype), 7, (x,))


def _multi_comm_call(body, name, out_shapes, n_sems, operands):
    anyspec = pl.BlockSpec(memory_space=pl.ANY)
    nin = len(operands)

    def flat_body(*refs):
        body(refs[:nin], refs[nin:nin + len(out_shapes)], refs[-2], refs[-1])

    return _call(flat_body, name=name, out_shape=list(out_shapes), in_specs=[anyspec] * nin,
                 out_specs=[anyspec] * len(out_shapes),
                 scratch_shapes=[pltpu.SemaphoreType.DMA((n_sems,)), pltpu.SemaphoreType.DMA((n_sems,))])(*operands)


def _remote(src, dst, send_sems, recv_sems, k, to):
    return pltpu.make_async_remote_copy(src_ref=src, dst_ref=dst, send_sem=send_sems.at[k], recv_sem=recv_sems.at[k],
                                        device_id=to, device_id_type=MESH)


def _gather_chips_split(xs, name):
    nw = len(xs)

    def body(x_refs, o_refs, send_sems, recv_sems):
        xi, yi, ci = _here()
        me = 2 * xi + yi
        sib = (xi, yi, 1 - ci)
        chips = [(1 - xi, yi), (xi, 1 - yi), (1 - xi, 1 - yi)]
        sends = []
        for i in range(nw):
            for k, (px, py) in enumerate(chips):
                sends.append(_remote(x_refs[i].at[ci], o_refs[i].at[me, ci], send_sems, recv_sems, 3 * i + k,
                                     (px, py, ci)))
        for cp in sends:
            cp.start()
        passed = []
        for i in range(nw):
            for k, (px, py) in enumerate(chips):
                landed = o_refs[i].at[2 * px + py, ci]
                _remote(landed, landed, send_sems, recv_sems, 3 * i + k, (px, py, ci)).wait_recv()
                fwd = _remote(landed, landed, send_sems, recv_sems, 3 * (nw + i) + k, sib)
                fwd.start()
                passed.append(fwd)
        for i in range(nw):
            for k, (px, py) in enumerate(chips):
                other = o_refs[i].at[2 * px + py, 1 - ci]
                _remote(other, other, send_sems, recv_sems, 3 * (nw + i) + k, sib).wait_recv()
        for cp in sends + passed:
            cp.wait_send()

    return _multi_comm_call(body, name, [_sds((4,) + x.shape, x.dtype) for x in xs], 6 * nw, xs)


def _swap_sibling_half(gs):
    def body(g_refs, o_refs, send_sems, recv_sems):
        xi, yi, ci = _here()
        cps = [_remote(g.at[:, 1 - ci], o, send_sems, recv_sems, i, (xi, yi, 1 - ci))
               for i, (g, o) in enumerate(zip(g_refs, o_refs))]
        for cp in cps:
            cp.start()
        for cp in cps:
            cp.wait()

    return _multi_comm_call(body, "swap_sibling_half", [_sds((g.shape[0],) + g.shape[2:], g.dtype) for g in gs],
                            len(gs), gs)


def _scatter_chips(gps):
    nw = len(gps)

    def body(g_refs, o_refs, send_sems, recv_sems):
        xi, yi, ci = _here()
        me = 2 * xi + yi
        chips = [(1 - xi, yi), (xi, 1 - yi), (1 - xi, 1 - yi)]
        sends = [_remote(g_refs[i].at[2 * px + py], o_refs[i].at[me], send_sems, recv_sems, 3 * i + k, (px, py, ci))
                 for i in range(nw) for k, (px, py) in enumerate(chips)]
        for cp in sends:
            cp.start()
        for i in range(nw):
            for k, (px, py) in enumerate(chips):
                slot = o_refs[i].at[2 * px + py]
                _remote(slot, slot, send_sems, recv_sems, 3 * i + k, (px, py, ci)).wait_recv()
        for cp in sends:
            cp.wait_send()

    return _multi_comm_call(body, "scatter_chips", [_sds(g.shape, g.dtype) for g in gps], 3 * nw, gps)


def _share_sibling(rs):
    def body(r_refs, o_refs, send_sems, recv_sems):
        xi, yi, ci = _here()
        cps = [_remote(r, o, send_sems, recv_sems, i, (xi, yi, 1 - ci)) for i, (r, o) in enumerate(zip(r_refs, o_refs))]
        for cp in cps:
            cp.start()
        for cp in cps:
            cp.wait()

    return _multi_comm_call(body, "share_sibling", [_sds(r.shape, r.dtype) for r in rs], len(rs), rs)


def _flat_pack(arrs, dtype, row_mult):
    flat = jnp.concatenate([a.astype(dtype).reshape(-1) for a in arrs])
    unit = FLAT_W * row_mult
    npad = -(-flat.shape[0] // unit) * unit
    return jnp.pad(flat, (0, npad - flat.shape[0])).reshape(npad // FLAT_W, FLAT_W)


def _flat_unpack(flat2d, shapes):
    flat = flat2d.reshape(-1)
    outs, off = [], 0
    for s in shapes:
        size = int(np.prod(s))
        outs.append(flat[off:off + size].reshape(s))
        off += size
    return outs


def _split_cols(a, widths):
    outs, off = [], 0
    for wd in widths:
        outs.append(lax.slice_in_dim(a, off, off + wd, axis=1))
        off += wd
    return outs


def _s5_params(lam_re, lam_im, log_step, b_re, b_im, c_re, c_im, d_skip):
    g, p = lam_re.shape
    hs = b_re.shape[2]
    gt = S5_GROUP_TILE
    jn = g // gt
    lam_re = jnp.minimum(lam_re, -1e-4)
    step = jnp.exp(log_step)[:, None]
    mag = jnp.exp(lam_re * step)
    ab_re = mag * jnp.cos(lam_im * step)
    ab_im = mag * jnp.sin(lam_im * step)
    den = lam_re * lam_re + lam_im * lam_im
    f_re = ((ab_re - 1.0) * lam_re + ab_im * lam_im) / den
    f_im = (ab_im * lam_re - (ab_re - 1.0) * lam_im) / den
    bb_re = f_re[..., None] * b_re - f_im[..., None] * b_im
    bb_im = f_re[..., None] * b_im + f_im[..., None] * b_re
    a_l = jnp.concatenate([ab_re.reshape(jn, gt * p), ab_im.reshape(jn, gt * p)], axis=1).reshape(1, jn * 2 * gt * p)
    eye = jnp.eye(gt, dtype=F32)

    def blockdiag(m):
        return jnp.einsum('jahp,ab->jahbp', m.reshape(jn, gt, hs, p), eye).reshape(jn, gt * hs, gt * p)

    b_blk = jnp.concatenate([blockdiag(bb_re.transpose(0, 2, 1)), blockdiag(bb_im.transpose(0, 2, 1))], axis=2)
    c_blk = jnp.concatenate([blockdiag(c_re), blockdiag(-c_im)], axis=2)
    return a_l, b_blk, c_blk, d_skip.reshape(1, g * hs)


def _layer_dims(p):
    d_model = p['w_out'].shape[1]
    wa = p['proj_a'].shape[0]
    h = p['gdn_a_log'].shape[0]
    wb = p['proj_b'].shape[0]
    wc = p['proj_c'].shape[0]
    hm = p['m2_a_log'].shape[0]
    cdim = p['m2_conv_w'].shape[1]
    width = dict(zip(PROJ_ORDER, (3 * wa, wa, h, h, wb, wb, wc, cdim, hm, d_model, d_model, d_model)))
    n_in = sum(width.values())
    return width, n_in, -(-n_in // LANES) * LANES


PROJ_ORDER = ('qkv', 'az', 'braw', 'araw', 'su', 'sgate', 'cz', 'cxbc', 'cdt', 'ma', 'mb', 'mc')
WORK_ORDER = ('qkv', 'az', 'cz', 'su', 'sgate', 'cxbc', 'ma', 'mb', 'mc', 'braw', 'araw', 'cdt')
ROW_COL_TILE = 512


def _reorder_cols(a, width, src_order, dst_order, n_out):
    off, o = {}, 0
    for n in src_order:
        off[n] = o
        o += width[n]
    parts = [lax.slice_in_dim(a, off[n], off[n] + width[n], axis=a.ndim - 1) for n in dst_order]
    used = sum(width[n] for n in dst_order)
    if n_out > used:
        parts.append(jnp.zeros(a.shape[:-1] + (n_out - used,), a.dtype))
    return jnp.concatenate(parts, axis=a.ndim - 1)


def _layer_fwd(x, p):
    width, n_in, n_pad = _layer_dims(p)
    sv = {'x': x}
    h = _rowwise(_f_rms, [x], [p['norm_w'][None]], [x.shape[1]], [BF16], "rms_fwd")[0]
    w_in = _reorder_cols(p['w_in'], width, PROJ_ORDER, WORK_ORDER, n_pad)
    proj = _matmul(h, w_in, 'nn', F32, "in_proj", tn=IN_PROJ_TILE)
    wins, off = {}, 0
    for n in WORK_ORDER:
        wins[n] = Win(proj, off, width[n])
        off += width[n]
    qkv, az, cz, su, sgate, cxbc, ma, mb, mc = (wins[n] for n in WORK_ORDER[:9])
    braw, araw, cdt = (lax.slice_in_dim(proj, wins[n].off, wins[n].off + width[n], axis=1) for n in WORK_ORDER[9:])
    sv.update(h=h, w_in=w_in, qkv=qkv, az=az, braw=braw, araw=araw, su=su, sgate=sgate, cz=cz, cxbc=cxbc, cdt=cdt,
              ma=ma, mb=mb, mc=mc)
    gb0 = jnp.zeros((1, qkv.shape[1]), F32)
    sqkv = _conv_fwd(qkv, p['gdn_conv_w'], gb0, "gdn_conv_fwd")
    ya, ssa, tsa = _gdn_chunks_fwd(sqkv, az, braw, araw, p['gdn_a_log'][None], p['gdn_dt_bias'][None],
                                   p['gdn_norm_w'][None])
    sv.update(sqkv=sqkv, ssa=ssa, tsa=tsa, ya=ya)
    s5_in = tuple(p[k] for k in ('s5_lam_re', 's5_lam_im', 's5_log_step', 's5_b_re', 's5_b_im', 's5_c_re', 's5_c_im',
                                 's5_d'))
    (a_l, b_blk, c_blk, d_l), s5_vjp = jax.vjp(_s5_params, *s5_in)
    cw = b_blk.shape[2]
    bu = _matmul_bd(su, b_blk, 'nn', F32, "s5_bu")
    s = _s5_scan(bu, a_l, cw, False, "s5_scan_fwd")
    ymm = _matmul_bd(s, c_blk, 'nt', F32, "s5_out")
    yg = _rowwise(_f_s5_post1, [ymm, su], [d_l], [su.shape[1]], [F32], "s5_post1_fwd", col_tile=ROW_COL_TILE)[0]
    tt = _matmul(yg, p['s5_glu_w'], 'nn', F32, "s5_glu")
    yb = _rowwise(_f_s5_post2, [yg, tt, sgate], [p['s5_glu_b'][None]], [su.shape[1]], [F32], "s5_post2_fwd",
                  col_tile=ROW_COL_TILE)[0]
    sv.update(a_l=a_l, b_blk=b_blk, c_blk=c_blk, d_l=d_l, s5_vjp=s5_vjp, s=s, ymm=ymm, yg=yg, tt=tt, yb=yb, cw=cw)
    sxbc = _conv_fwd(cxbc, p['m2_conv_w'], p['m2_conv_b'][None], "m2_conv_fwd")
    yc, ssc = _m2_chunks_fwd(sxbc, cz, cdt, p['m2_a_log'][None], p['m2_dt_bias'][None], p['m2_d'][None],
                             p['m2_norm_w'][None])
    sv.update(sxbc=sxbc, ssc=ssc, yc=yc)
    pa = _matmul(ya, p['proj_a'], 'nn', F32, "proj_a")
    pb = _matmul(yb, p['proj_b'], 'nn', F32, "proj_b")
    pc = _matmul(yc, p['proj_c'], 'nn', F32, "proj_c")
    merged = _rowwise(_f_merge, [ma, mb, mc, pa, pb, pc], [], [x.shape[1]], [BF16], "merge_fwd",
                      col_tile=ROW_COL_TILE)[0]
    x_next = _matmul(merged, p['w_out'], 'nn', F32, "out_proj", add=x)
    sv.update(pa=pa, pb=pb, pc=pc, merged=merged)
    return x_next, sv


def _layer_bwd(dx_out, p, sv):
    width, n_in, n_pad = _layer_dims(p)
    g = {}
    dmerged = _matmul(dx_out, p['w_out'], 'nt', F32, "out_proj_dx")
    g['w_out'] = _matmul(sv['merged'], dx_out, 'tn', F32, "out_proj_dw")
    dma, dmb, dmc, dpa, dpb, dpc = _rowwise_bwd(
        _f_merge, [sv['ma'], sv['mb'], sv['mc'], sv['pa'], sv['pb'], sv['pc']], [], [dmerged], [BF16] * 6,
        "merge_bwd", col_tile=ROW_COL_TILE)
    dya = _matmul(dpa, p['proj_a'], 'nt', F32, "proj_a_dx")
    dyb = _matmul(dpb, p['proj_b'], 'nt', F32, "proj_b_dx")
    dyc = _matmul(dpc, p['proj_c'], 'nt', F32, "proj_c_dx")
    g['proj_a'] = _matmul(sv['ya'], dpa, 'tn', F32, "proj_a_dw")
    g['proj_b'] = _matmul(sv['yb'], dpb, 'tn', F32, "proj_b_dw")
    g['proj_c'] = _matmul(sv['yc'], dpc, 'tn', F32, "proj_c_dw")
    alog, dtb, gnw = p['gdn_a_log'][None], p['gdn_dt_bias'][None], p['gdn_norm_w'][None]
    dsq, daz, db3, da3, dalog, ddtb, dgnw = _gdn_chunks_bwd(sv['sqkv'], sv['az'], sv['braw'], sv['araw'], alog, dtb, gnw,
                                                            sv['ssa'], sv['tsa'], dya)
    gb0 = jnp.zeros((1, sv['qkv'].shape[1]), F32)
    dqkv, g['gdn_conv_w'], _ = _conv_bwd(sv['qkv'], p['gdn_conv_w'], gb0, dsq, "gdn_conv_bwd")
    dbraw, daraw = jnp.sum(db3, axis=0), jnp.sum(da3, axis=0)
    g.update(gdn_a_log=dalog[0], gdn_dt_bias=ddtb[0], gdn_norm_w=dgnw[0])
    dsx, dcz, dcdt, dmalog, dmdtb, dmdsk, dmnw = _m2_chunks_bwd(
        sv['sxbc'], sv['cz'], sv['cdt'], p['m2_a_log'][None], p['m2_dt_bias'][None], p['m2_d'][None],
        p['m2_norm_w'][None], sv['ssc'], dyc)
    dcxbc, g['m2_conv_w'], dconvb = _conv_bwd(sv['cxbc'], p['m2_conv_w'], p['m2_conv_b'][None], dsx, "m2_conv_bwd")
    g.update(m2_conv_b=dconvb[0], m2_a_log=dmalog[0], m2_dt_bias=dmdtb[0], m2_d=dmdsk[0], m2_norm_w=dmnw[0])
    dyg1, dtt, dsgate, dglub = _rowwise_bwd(_f_s5_post2, [sv['yg'], sv['tt'], sv['sgate']], [p['s5_glu_b'][None]],
                                            [dyb], [F32, BF16, BF16], "s5_post2_bwd", col_tile=ROW_COL_TILE)
    dyg = _matmul(dtt, p['s5_glu_w'], 'nt', F32, "s5_glu_dx", add=dyg1)
    g['s5_glu_w'] = _matmul(sv['yg'], dtt, 'tn', F32, "s5_glu_dw")
    g['s5_glu_b'] = dglub[0]
    dymm, dsu1, dd_l = _rowwise_bwd(_f_s5_post1, [sv['ymm'], sv['su']], [sv['d_l']], [dyg], [BF16, F32],
                                    "s5_post1_bwd", col_tile=ROW_COL_TILE)
    gy = _matmul_bd(dymm, sv['c_blk'], 'nn', F32, "s5_out_dx")
    ds = _s5_scan(gy, sv['a_l'], sv['cw'], True, "s5_scan_bwd")
    da_l = _s5_da(ds, sv['s'], sv['cw'])
    dsu = _matmul_bd(ds, sv['b_blk'], 'nt', BF16, "s5_bu_dx", add=dsu1)
    ka = sv['b_blk'].shape[1]
    db_blk = _matmul_bd(sv['su'], ds, ('tn', ka, sv['cw']), F32, "s5_bu_dw")
    dc_blk = _matmul_bd(dymm, sv['s'], ('tn', ka, sv['cw']), F32, "s5_out_dw")
    for k, v in zip(('s5_lam_re', 's5_lam_im', 's5_log_step', 's5_b_re', 's5_b_im', 's5_c_re', 's5_c_im', 's5_d'),
                    sv['s5_vjp']((da_l, db_blk, dc_blk, dd_l))):
        g[k] = v
    small = jnp.concatenate([dbraw, daraw, dcdt], axis=1).astype(BF16)
    small = jnp.pad(small, ((0, 0), (0, n_pad - n_in + sum(width[n] for n in WORK_ORDER[9:]) - small.shape[1])))
    dproj = _concat_cols([dqkv, daz, dcz, dsu, dsgate, dcxbc, dma, dmb, dmc, small], "concat_dproj")
    dh = _matmul(dproj, sv['w_in'], 'nt', F32, "in_proj_dx", tm=2048, tk=IN_PROJ_TILE)
    g['w_in'] = _reorder_cols(_matmul(sv['h'], dproj, 'tn', F32, "in_proj_dw", tn=IN_PROJ_TILE), width, WORK_ORDER,
                              PROJ_ORDER, n_in)
    dx, dnw = _rowwise_bwd(_f_rms, [sv['x']], [p['norm_w'][None]], [dh], [F32], "rms_bwd", addend=dx_out)
    g['norm_w'] = dnw[0]
    return dx, g


INPUT_NAMES = (['x'] + WEIGHT_NAMES + ['loss_target'] + ['m_' + n for n in WEIGHT_NAMES]
               + ['v_' + n for n in WEIGHT_NAMES])


def _step(d):
    xi, yi, ci = _here()
    me = 2 * xi + yi
    depth = d['norm_w'].shape[0]
    big, ssm = list(BIG), list(SHARDED_SMALL)
    nsh = 4
    full, gathered = {}, {}
    halves = [d[n].astype(BF16).reshape(2, -1, d[n].shape[-1]) for n in big]
    for n, hv, got in zip(big, halves, _gather_chips_split(halves, "gather_weights")):
        gathered[n] = lax.dynamic_update_slice(got, hv[None], (me, 0, 0, 0)).reshape((nsh,) + d[n].shape)
    cg = _gather_chips(_flat_pack([d[n] for n in ssm], F32, 8), "gather_conv_weights")
    parts = [_flat_unpack(cg[j], [d[n].shape for n in ssm]) for j in range(nsh)]
    for i, n in enumerate(ssm):
        full[n] = jnp.concatenate([parts[j][i] for j in range(nsh)], axis=SHARDED_SMALL[n])
    layer_names = [n for n in WEIGHT_NAMES if n != 'final_norm_w']

    def layer_params(l):
        p = {n: (full[n][l] if n in full else d[n][l]) for n in layer_names if n not in BIG}
        for n in big:
            p[n] = jnp.concatenate([gathered[n][j, l] for j in range(nsh)], axis=BIG[n] - 1)
        return p

    x = d['x'][0]
    saved = []
    for l in range(depth):
        x, sv = _layer_fwd(x, layer_params(l))
        saved.append(sv)
    loss11, dx, dfw = _loss_and_grad(x, d['loss_target'][0], d['final_norm_w'][None])
    loss = lax.psum(loss11[0, 0], ("x", "y", "c"))
    grads = [None] * depth
    for l in reversed(range(depth)):
        dx, grads[l] = _layer_bwd(dx, layer_params(l), saved[l])
    gfull = {n: jnp.stack([grads[l][n] for l in range(depth)]) for n in layer_names if n not in BIG}
    gfull['final_norm_w'] = dfw[0]
    def shard(a, axis, j):
        wd = a.shape[axis] // nsh
        return lax.slice_in_dim(a, j * wd, (j + 1) * wd, axis=axis)

    c_idx = jnp.reshape(ci, (1,)).astype(jnp.int32)
    me_idx = jnp.reshape(me, (1,)).astype(jnp.int32)
    g4 = [jnp.stack([jnp.stack([shard(grads[l][n], BIG[n] - 1, j) for l in range(depth)]) for j in range(nsh)])
          .reshape(nsh, 2, -1, d[n].shape[-1]) for n in big]
    pairs = [_add_my_half(g, r, c_idx, "add_my_half_" + n) for n, g, r in zip(big, g4, _swap_sibling_half(g4))]
    got = _scatter_chips([pb for _, pb in pairs])
    mine = [_sum_chips(pf, gt, me_idx, "sum_chips_" + n) for n, (pf, _), gt in zip(big, pairs, got)]
    theirs = _share_sibling(mine)
    out = {}
    for n, mn, th in zip(big, mine, theirs):
        both = jnp.where(ci == 0, jnp.stack([mn, th]), jnp.stack([th, mn]))
        w2, m2, v2 = (d[pre + n].reshape(both.shape[0] * both.shape[1], both.shape[2]) for pre in ('', 'm_', 'v_'))
        g2 = both.reshape(w2.shape)
        dl, nm, nv = _adamw(w2, g2, m2, v2, "adamw_" + n)
        for key, arr in (('grad_', g2), ('delta_', dl), ('new_m_', nm), ('new_v_', nv)):
            out[key + n] = arr.reshape(d[n].shape)
    small = [n for n in WEIGHT_NAMES if n not in BIG]
    sshapes = [gfull[n].shape for n in small]
    gsm = _sum_parts(_gather_all(_flat_pack([gfull[n] for n in small], F32, 8), "gather_small_grads"), "sum_devices")
    gs = dict(zip(small, _flat_unpack(gsm, sshapes)))
    for n in ssm:
        wd = d[n].shape[SHARDED_SMALL[n]]
        gs[n] = lax.dynamic_slice_in_dim(gs[n], me * wd, wd, axis=SHARDED_SMALL[n])
    lshapes = [d[n].shape for n in small]
    wps, gps, mps, vps = (_flat_pack(arrs, F32, 16) for arrs in (
        [d[n] for n in small], [gs[n] for n in small], [d['m_' + n] for n in small], [d['v_' + n] for n in small]))
    dl, nm, nv = _adamw(wps, gps, mps, vps, "adamw_small")
    for key, arr in (('grad_', gps), ('delta_', dl), ('new_m_', nm), ('new_v_', nv)):
        for n, a in zip(small, _flat_unpack(arr, lshapes)):
            out[key + n] = a
    res = [loss, dx[None]]
    for key in ('grad_', 'delta_', 'new_m_', 'new_v_'):
        res += [out[key + n] for n in WEIGHT_NAMES]
    return tuple(res)


def kernel(x, norm_w, w_in, gdn_conv_w, gdn_a_log, gdn_dt_bias, gdn_norm_w, s5_lam_re, s5_lam_im, s5_log_step, s5_b_re, s5_b_im, s5_c_re, s5_c_im, s5_d, s5_glu_w, s5_glu_b, m2_conv_w, m2_conv_b, m2_a_log, m2_dt_bias, m2_d, m2_norm_w, proj_a, proj_b, proj_c, w_out, final_norm_w, loss_target, m_norm_w, m_w_in, m_gdn_conv_w, m_gdn_a_log, m_gdn_dt_bias, m_gdn_norm_w, m_s5_lam_re, m_s5_lam_im, m_s5_log_step, m_s5_b_re, m_s5_b_im, m_s5_c_re, m_s5_c_im, m_s5_d, m_s5_glu_w, m_s5_glu_b, m_m2_conv_w, m_m2_conv_b, m_m2_a_log, m_m2_dt_bias, m_m2_d, m_m2_norm_w, m_proj_a, m_proj_b, m_proj_c, m_w_out, m_final_norm_w, v_norm_w, v_w_in, v_gdn_conv_w, v_gdn_a_log, v_gdn_dt_bias, v_gdn_norm_w, v_s5_lam_re, v_s5_lam_im, v_s5_log_step, v_s5_b_re, v_s5_b_im, v_s5_c_re, v_s5_c_im, v_s5_d, v_s5_glu_w, v_s5_glu_b, v_m2_conv_w, v_m2_conv_b, v_m2_a_log, v_m2_dt_bias, v_m2_d, v_m2_norm_w, v_proj_a, v_proj_b, v_proj_c, v_w_out, v_final_norm_w):
    args = (x, norm_w, w_in, gdn_conv_w, gdn_a_log, gdn_dt_bias, gdn_norm_w, s5_lam_re, s5_lam_im, s5_log_step, s5_b_re, s5_b_im, s5_c_re, s5_c_im, s5_d, s5_glu_w, s5_glu_b, m2_conv_w, m2_conv_b, m2_a_log, m2_dt_bias, m2_d, m2_norm_w, proj_a, proj_b, proj_c, w_out, final_norm_w, loss_target, m_norm_w, m_w_in, m_gdn_conv_w, m_gdn_a_log, m_gdn_dt_bias, m_gdn_norm_w, m_s5_lam_re, m_s5_lam_im, m_s5_log_step, m_s5_b_re, m_s5_b_im, m_s5_c_re, m_s5_c_im, m_s5_d, m_s5_glu_w, m_s5_glu_b, m_m2_conv_w, m_m2_conv_b, m_m2_a_log, m_m2_dt_bias, m_m2_d, m_m2_norm_w, m_proj_a, m_proj_b, m_proj_c, m_w_out, m_final_norm_w, v_norm_w, v_w_in, v_gdn_conv_w, v_gdn_a_log, v_gdn_dt_bias, v_gdn_norm_w, v_s5_lam_re, v_s5_lam_im, v_s5_log_step, v_s5_b_re, v_s5_b_im, v_s5_c_re, v_s5_c_im, v_s5_d, v_s5_glu_w, v_s5_glu_b, v_m2_conv_w, v_m2_conv_b, v_m2_a_log, v_m2_dt_bias, v_m2_d, v_m2_norm_w, v_proj_a, v_proj_b, v_proj_c, v_w_out, v_final_norm_w)
    return _step(dict(zip(INPUT_NAMES, args)))
```

```python
import functools
import math
from typing import NamedTuple

import jax
import jax.numpy as jnp
import numpy as np
from jax import lax
from jax.experimental import pallas as pl
from jax.experimental.pallas import tpu as pltpu

F32 = jnp.float32
BF16 = jnp.bfloat16
HI = lax.Precision.HIGH
MESH = pl.DeviceIdType.MESH

CHUNK = 64
CONV_K = 4
NORM_EPS = 1e-6
GDN_HEAD_DIM = 128
M2_HEAD_DIM = 64
M2_STATE = 128
M2_GROUPS = 4
S5_GROUP_TILE = 8
ADAM_LR = 0.001
ADAM_B1 = 0.9
ADAM_B2 = 0.999
ADAM_EPS = 1e-08
ADAM_WD = 0.01
ADAM_STEP = 10
LANES = 128
SUBLANES = 8
VMEM_LIMIT_BYTES = 56 * 1024 * 1024

WEIGHT_NAMES = ['norm_w', 'w_in', 'gdn_conv_w', 'gdn_a_log', 'gdn_dt_bias', 'gdn_norm_w', 's5_lam_re', 's5_lam_im',
                's5_log_step', 's5_b_re', 's5_b_im', 's5_c_re', 's5_c_im', 's5_d', 's5_glu_w', 's5_glu_b',
                'm2_conv_w', 'm2_conv_b', 'm2_a_log', 'm2_dt_bias', 'm2_d', 'm2_norm_w', 'proj_a', 'proj_b',
                'proj_c', 'w_out', 'final_norm_w']
BIG = {'w_in': 2, 'proj_a': 2, 'proj_b': 2, 'proj_c': 2, 'w_out': 1, 's5_glu_w': 1}
SHARDED_SMALL = {'gdn_conv_w': 2, 'm2_conv_w': 2}


def _call(body, *, name, out_shape, grid=None, in_specs=None, out_specs=None, scratch_shapes=(), semantics=None,
          num_scalar_prefetch=None):
    params = dict(vmem_limit_bytes=VMEM_LIMIT_BYTES)
    if semantics is not None:
        params['dimension_semantics'] = semantics
    kw = dict(name=name, out_shape=out_shape, compiler_params=pltpu.CompilerParams(**params))
    if num_scalar_prefetch is not None:
        kw['grid_spec'] = pltpu.PrefetchScalarGridSpec(num_scalar_prefetch=num_scalar_prefetch, grid=grid,
                                                       in_specs=in_specs, out_specs=out_specs,
                                                       scratch_shapes=scratch_shapes)
    else:
        if grid is not None:
            kw['grid'] = grid
        if in_specs is not None:
            kw['in_specs'] = in_specs
        if out_specs is not None:
            kw['out_specs'] = out_specs
        if scratch_shapes:
            kw['scratch_shapes'] = scratch_shapes
    return pl.pallas_call(body, **kw)


def _tile(n, target, unit):
    if n <= target:
        return n
    t = (target // unit) * unit
    while t >= unit:
        if n % t == 0:
            return t
        t -= unit
    raise ValueError(f"no tile for {n} (unit {unit}, target {target})")


def _sds(shape, dtype):
    return jax.ShapeDtypeStruct(tuple(shape), dtype)


def _sigmoid(x):
    return jax.nn.sigmoid(x)


def _silu(x):
    return x * jax.nn.sigmoid(x)


def _softplus(x):
    return jnp.maximum(x, 0.0) + jnp.log(1.0 + jnp.exp(-jnp.abs(x)))


def _gelu_tanh(x):
    return 0.5 * x * (1.0 + jnp.tanh(math.sqrt(2.0 / math.pi) * (x + 0.044715 * (x * x * x))))


def _dot(a, b, dims, prec=None):
    return lax.dot_general(a, b, (dims, ((), ())), precision=prec, preferred_element_type=F32)


def _nn(a, b, prec=None):
    return _dot(a, b, ((1,), (0,)), prec)


def _nt(a, b, prec=None):
    return _dot(a, b, ((1,), (1,)), prec)


def _tn(a, b, prec=None):
    return _dot(a, b, ((0,), (0,)), prec)


IN_PROJ_TILE = 1664
MATMUL_TILES = {'nn': (1024, 1024, 2048), 'nt': (1024, 1024, 2048), 'tn': (1024, 1024, 2048)}


def _matmul(a, b, mode, out_dtype, name, tm=None, tn=None, tk=None, add=None):
    tm, tn, tk = (t if t is not None else dflt for t, dflt in zip((tm, tn, tk), MATMUL_TILES[mode]))
    if mode == 'nn':
        (m, k), (k2, n) = a.shape, b.shape
    elif mode == 'nt':
        (m, k), (n, k2) = a.shape, b.shape
    else:
        (k, m), (k2, n) = a.shape, b.shape
    assert k == k2, (a.shape, b.shape, mode)
    tm = _tile(m, tm, LANES if mode == 'tn' else SUBLANES)
    tn = _tile(n, tn, LANES)
    tk = _tile(k, tk, LANES if mode != 'tn' else SUBLANES * 2)
    nk = k // tk
    dims = {'nn': ((1,), (0,)), 'nt': ((1,), (1,)), 'tn': ((0,), (0,))}[mode]

    def body(*refs):
        a_ref, b_ref = refs[:2]
        o_ref, acc_ref = refs[-2:]
        kk = pl.program_id(2)

        @pl.when(kk == 0)
        def _():
            acc_ref[...] = jnp.zeros_like(acc_ref)

        acc_ref[...] += _dot(a_ref[...].astype(BF16), b_ref[...].astype(BF16), dims)

        @pl.when(kk == nk - 1)
        def _():
            res = acc_ref[...]
            if add is not None:
                res = res + refs[2][...].astype(F32)
            o_ref[...] = res.astype(o_ref.dtype)

    a_spec = pl.BlockSpec((tk, tm), lambda i, j, kk: (kk, i)) if mode == 'tn' else pl.BlockSpec((tm, tk), lambda i, j, kk: (i, kk))
    b_spec = pl.BlockSpec((tn, tk), lambda i, j, kk: (j, kk)) if mode == 'nt' else pl.BlockSpec((tk, tn), lambda i, j, kk: (kk, j))
    o_spec = pl.BlockSpec((tm, tn), lambda i, j, kk: (i, j))
    ops = (a, b) if add is None else (a, b, add)
    return _call(body, name=name, out_shape=_sds((m, n), out_dtype), grid=(m // tm, n // tn, nk),
                 in_specs=[a_spec, b_spec] + ([] if add is None else [o_spec]), out_specs=o_spec,
                 scratch_shapes=[pltpu.VMEM((tm, tn), F32)], semantics=("parallel", "parallel", "arbitrary"))(*ops)


class Win(NamedTuple):
    arr: jax.Array
    off: int
    width: int

    @property
    def shape(self):
        return (self.arr.shape[0], self.width)


def _win(x):
    return x if isinstance(x, Win) else Win(x, 0, x.shape[1])


def _col_tile(wins, target):
    ct = (min(target, min(w.width for w in wins)) // LANES) * LANES
    while ct > LANES and any(w.width % ct or w.off % ct for w in wins):
        ct -= LANES
    assert all(w.width % ct == 0 and w.off % ct == 0 for w in wins), [(w.off, w.width) for w in wins]
    return ct


def _wspec(rows, ct, w, row_first=True):
    base = w.off // ct
    if row_first:
        return pl.BlockSpec((rows, ct), lambda i, j: (i, base + j))
    return pl.BlockSpec((rows, ct), lambda j, i: (i, base + j))


def _bd_expand(a, b, name, tm=1024):
    a = _win(a)
    t = a.shape[0]
    jn, ka, nb = b.shape
    r = nb // LANES
    tm = _tile(t, tm, SUBLANES)
    assert a.shape[1] == jn * ka and a.off % ka == 0
    abase = a.off // ka

    def body(a_ref, b_ref, o_ref):
        o_ref[...] = _nn(a_ref[...].astype(BF16), b_ref[...].astype(BF16)).reshape(tm, r, LANES)

    return _call(body, name=name, out_shape=_sds((t, jn * r, LANES), F32), grid=(t // tm, jn),
                 in_specs=[pl.BlockSpec((tm, ka), lambda i, j: (i, abase + j)),
                           pl.BlockSpec((None, ka, nb), lambda i, j: (j, 0, 0))],
                 out_specs=pl.BlockSpec((tm, r, LANES), lambda i, j: (i, j, 0)),
                 semantics=("parallel", "parallel"))(a.arr, b)


def _bd_reduce(a3, b, out_dtype, name, tm=1024, add=None):
    t = a3.shape[0]
    jn, ka, nb = b.shape
    r = nb // LANES
    tm = _tile(t, tm, SUBLANES)

    def body(*refs):
        a_ref, b_ref, o_ref = refs[0], refs[1], refs[-1]
        res = _nt(a_ref[...].reshape(tm, nb).astype(BF16), b_ref[...].astype(BF16))
        if add is not None:
            res = res + refs[2][...].astype(F32)
        o_ref[...] = res.astype(o_ref.dtype)

    o_spec = pl.BlockSpec((tm, ka), lambda i, j: (i, j))
    ops = (a3, b) if add is None else (a3, b, add)
    return _call(body, name=name, out_shape=_sds((t, jn * ka), out_dtype), grid=(t // tm, jn),
                 in_specs=[pl.BlockSpec((tm, r, LANES), lambda i, j: (i, j, 0)),
                           pl.BlockSpec((None, ka, nb), lambda i, j: (j, 0, 0))] + ([] if add is None else [o_spec]),
                 out_specs=o_spec, semantics=("parallel", "parallel"))(*ops)


def _bd_outer(a, b3, ka, name, tk=1024):
    a = _win(a)
    t = a.shape[0]
    jn = a.shape[1] // ka
    r = b3.shape[1] // jn
    nb = r * LANES
    assert a.off % ka == 0
    abase = a.off // ka
    tk = _tile(t, tk, SUBLANES * 2)

    def body(a_ref, b_ref, o_ref):
        @pl.when(pl.program_id(1) == 0)
        def _():
            o_ref[...] = jnp.zeros_like(o_ref)

        o_ref[...] += _tn(a_ref[...].astype(BF16), b_ref[...].reshape(tk, nb).astype(BF16))

    return _call(body, name=name, out_shape=_sds((jn, ka, nb), F32), grid=(jn, t // tk),
                 in_specs=[pl.BlockSpec((tk, ka), lambda j, kk: (kk, abase + j)),
                           pl.BlockSpec((tk, r, LANES), lambda j, kk: (kk, j, 0))],
                 out_specs=pl.BlockSpec((None, ka, nb), lambda j, kk: (j, 0, 0)),
                 semantics=("parallel", "arbitrary"))(a.arr, b3)


def _concat_cols(parts, name, tb=256):
    t = parts[0].shape[0]
    tb = _tile(t, tb, SUBLANES * 2)
    widths = [p.shape[1] for p in parts]
    assert all(w % LANES == 0 for w in widths)

    def body(*refs):
        o_ref, off = refs[-1], 0
        for r, w in zip(refs[:-1], widths):
            o_ref[:, off:off + w] = r[...]
            off += w

    return _call(body, name=name, out_shape=_sds((t, sum(widths)), parts[0].dtype), grid=(t // tb,),
                 in_specs=[pl.BlockSpec((tb, w), lambda i: (i, 0)) for w in widths],
                 out_specs=pl.BlockSpec((tb, sum(widths)), lambda i: (i, 0)), semantics=("parallel",))(*parts)


def _rowwise_tiles(rows, tb, col_tile):
    rows = [_win(r) for r in rows]
    t = rows[0].shape[0]
    tb = _tile(t, tb, SUBLANES * 2)
    if col_tile is None:
        assert all(r.off % r.width == 0 for r in rows)
        return rows, tb, None, 1
    ct = _col_tile(rows, col_tile)
    tb = _tile(t, max(tb, BLOCK_BYTES // (4 * ct)), SUBLANES * 2)
    return rows, tb, ct, rows[0].width // ct


def _rowwise(fn, rows, params, out_widths, out_dtypes, name, tb=256, col_tile=None):
    rows, tb, ct, ncol = _rowwise_tiles(rows, tb, col_tile)
    t = rows[0].shape[0]
    nr, npar = len(rows), len(params)

    def body(*refs):
        ins = [r[...].astype(F32) for r in refs[:nr + npar]]
        outs = fn(*ins)
        for o_ref, o in zip(refs[nr + npar:], outs):
            o_ref[...] = o.astype(o_ref.dtype)

    in_specs = [_wspec(tb, ct or r.width, r) for r in rows]
    in_specs += [pl.BlockSpec((1, ct or p.shape[1]), lambda i, j: (0, j)) for p in params]
    out_shape = [_sds((t, w), d) for w, d in zip(out_widths, out_dtypes)]
    out_specs = [pl.BlockSpec((tb, ct or w), lambda i, j: (i, j)) for w in out_widths]
    return _call(body, name=name, out_shape=out_shape, grid=(t // tb, ncol), in_specs=in_specs, out_specs=out_specs,
                 semantics=("parallel", "parallel"))(*[r.arr for r in rows], *params)


def _rowwise_bwd(fn, rows, params, cts, row_grad_dtypes, name, tb=256, addend=None, col_tile=None):
    rows, tb, ct, ncol = _rowwise_tiles(rows, tb, col_tile)
    t = rows[0].shape[0]
    nr, npar, nc = len(rows), len(params), len(cts)
    keep = [i for i, d in enumerate(row_grad_dtypes) if d is not None]
    nadd = 0 if addend is None else 1

    def body(*refs):
        ins = [r[...].astype(F32) for r in refs[:nr + npar]]
        ct = [r[...].astype(F32) for r in refs[nr + npar:nr + npar + nc]]
        _, vjp = jax.vjp(fn, *ins)
        grads = vjp(tuple(ct))
        out_refs = refs[nr + npar + nc + nadd:]
        for o_ref, i in zip(out_refs[:len(keep)], keep):
            g = grads[i]
            if nadd and i == 0:
                g = g + refs[nr + npar + nc][...].astype(F32)
            o_ref[...] = g.astype(o_ref.dtype)

        @pl.when(pl.program_id(1) == 0)
        def _():
            for o_ref in out_refs[len(keep):]:
                o_ref[...] = jnp.zeros_like(o_ref)

        for o_ref, g in zip(out_refs[len(keep):], grads[nr:]):
            o_ref[...] += g

    def plain(w):
        return pl.BlockSpec((tb, ct or w), lambda j, i: (i, j))

    in_specs = [_wspec(tb, ct or r.width, r, row_first=False) for r in rows]
    in_specs += [pl.BlockSpec((1, ct or p.shape[1]), lambda j, i: (0, j)) for p in params]
    in_specs += [plain(c.shape[1]) for c in cts]
    extra = []
    if nadd:
        in_specs += [plain(addend.shape[1])]
        extra = [addend]
    out_shape = [_sds(rows[i].shape, row_grad_dtypes[i]) for i in keep] + [_sds(p.shape, F32) for p in params]
    out_specs = [plain(rows[i].width) for i in keep]
    out_specs += [pl.BlockSpec((1, ct or p.shape[1]), lambda j, i: (0, j)) for p in params]
    return _call(body, name=name, out_shape=out_shape, grid=(ncol, t // tb), in_specs=in_specs, out_specs=out_specs,
                 semantics=("parallel", "arbitrary"))(*[r.arr for r in rows], *params, *cts, *extra)


def _chunk_masks(c):
    row = lax.broadcasted_iota(jnp.int32, (c, c), 0)
    col = lax.broadcasted_iota(jnp.int32, (c, c), 1)
    causal = row >= col
    strict = row > col
    return causal, strict, causal.astype(F32), (row > col).astype(F32), (row == col).astype(F32)


def _lane_pick(blk, idx):
    lane = lax.broadcasted_iota(jnp.int32, blk.shape, 1)
    return jnp.sum(jnp.where(lane == idx, blk, 0.0), axis=1, keepdims=True)


def _bdot(a, b, ca, cb, prec=None):
    return lax.dot_general(a, b, (((ca,), (cb,)), ((0,), (0,))), precision=prec, preferred_element_type=F32)


def _bnn(a, b, prec=None):
    return _bdot(a, b, 2, 1, prec)


def _bnt(a, b, prec=None):
    return _bdot(a, b, 2, 2, prec)


def _btn(a, b, prec=None):
    return _bdot(a, b, 1, 1, prec)


def _unit_lower_inverse(a_mat, eye):
    x = -a_mat
    t_inv = eye + x
    p = x
    for _ in range(int(math.log2(a_mat.shape[-1])) - 1):
        p = _bnn(p, p, HI)
        t_inv = t_inv + _bnn(t_inv, p, HI)
    return t_inv


@jax.custom_vjp
def _saved_inverse(a_mat, t_saved):
    return t_saved


def _saved_inverse_fwd(a_mat, t_saved):
    return t_saved, t_saved


def _saved_inverse_bwd(t_inv, ct):
    return -_bnt(_btn(t_inv, ct, HI), t_inv, HI), jnp.zeros_like(t_inv)


_saved_inverse.defvjp(_saved_inverse_fwd, _saved_inverse_bwd)


def _gdn_heads(q, k, v, z, braw, araw, alog, dtb, nw, s_in, t_saved=None):
    b, c, d = q.shape
    causal, strict, lower, upper_t, eye = _chunk_masks(c)
    lower_b = jnp.broadcast_to(lower[None], (b, c, c))
    qn = q * lax.rsqrt(jnp.sum(q * q, axis=-1, keepdims=True) + NORM_EPS) * (d ** -0.5)
    kn = k * lax.rsqrt(jnp.sum(k * k, axis=-1, keepdims=True) + NORM_EPS)
    beta = _sigmoid(braw)
    g = -jnp.exp(alog) * _softplus(araw + dtb)
    dlog = _bnn(lower_b, g * upper_t[None], HI)
    dm = jnp.where(causal[None], jnp.exp(dlog), 0.0)
    g_lanes = jnp.broadcast_to(g, (b, c, d))
    gc = _bnn(lower_b, g_lanes, HI)
    gl = jnp.sum(g_lanes, axis=1, keepdims=True)
    eg = jnp.exp(gc)
    kb = kn * beta
    a_mat = jnp.where(strict[None], _bnt(kb, kn) * dm, 0.0)
    t_inv = _unit_lower_inverse(a_mat, eye[None]) if t_saved is None else _saved_inverse(a_mat, t_saved)
    r = beta * (v - eg * _bnn(kn, s_in))
    v_new = _bnn(t_inv, r)
    qk = _bnt(qn, kn) * dm
    out = eg * _bnn(qn, s_in) + _bnn(qk, v_new)
    k_tail = kn * jnp.exp(gl - gc)
    s_out = s_in * jnp.exp(gl) + _btn(k_tail, v_new)
    y = out * lax.rsqrt(jnp.mean(out * out, axis=-1, keepdims=True) + NORM_EPS) * nw[None] * _silu(z)
    if t_saved is None:
        return y, s_out, t_inv
    return y, s_out


def _gdn_stack(refs, hb, d, h, first_head):
    q_ref, k_ref, v_ref, z_ref, b_ref, a_ref, alog_ref, dtb_ref = refs
    sls = [slice(i * d, (i + 1) * d) for i in range(hb)]
    heads = [first_head + i for i in range(hb)]
    wide = [jnp.stack([r[:, sl] for sl in sls]) for r in (q_ref, k_ref, v_ref, z_ref)]
    cols = [jnp.stack([_lane_pick(r[...], hd) for hd in heads]) for r in (b_ref, a_ref, alog_ref, dtb_ref)]
    return wide + cols


GDN_HEADS_PER_STEP = 8


def _gdn_chunks_fwd(sqkv, z, braw, araw, alog, dtb, nw, hb=GDN_HEADS_PER_STEP):
    t, w3 = sqkv.shape
    w = w3 // 3
    d = GDN_HEAD_DIM
    h = w // d
    hb = min(hb, h)
    hg = h // hb
    c = CHUNK
    nc = t // c

    def body(q_ref, k_ref, v_ref, z_ref, b_ref, a_ref, alog_ref, dtb_ref, nw_ref, y_ref, ssave_ref, tsave_ref,
             s_ref):
        @pl.when(pl.program_id(1) == 0)
        def _():
            s_ref[...] = jnp.zeros_like(s_ref)

        s_in = s_ref[...]
        ssave_ref[...] = s_in
        args = _gdn_stack((q_ref, k_ref, v_ref, z_ref, b_ref, a_ref, alog_ref, dtb_ref), hb, d, h,
                          pl.program_id(0) * hb)
        y, s_out, t_inv = _gdn_heads(*args, nw_ref[...], s_in)
        for i in range(hb):
            y_ref[:, i * d:(i + 1) * d] = y[i]
        s_ref[...] = s_out
        tsave_ref[...] = t_inv

    blk = (c, hb * d)
    z = _win(z)
    zb = z.off // (hb * d)
    assert z.off % (hb * d) == 0
    in_specs = [pl.BlockSpec(blk, lambda g, n: (n, g)), pl.BlockSpec(blk, lambda g, n: (n, hg + g)),
                pl.BlockSpec(blk, lambda g, n: (n, 2 * hg + g)), pl.BlockSpec(blk, lambda g, n: (n, zb + g)),
                pl.BlockSpec((c, h), lambda g, n: (n, 0)), pl.BlockSpec((c, h), lambda g, n: (n, 0)),
                pl.BlockSpec((1, h), lambda g, n: (0, 0)), pl.BlockSpec((1, h), lambda g, n: (0, 0)),
                pl.BlockSpec((1, d), lambda g, n: (0, 0))]
    out_shape = [_sds((t, w), F32), _sds((hg, nc, hb, d, d), F32), _sds((hg, nc, hb, c, c), F32)]
    out_specs = [pl.BlockSpec(blk, lambda g, n: (n, g)),
                 pl.BlockSpec((None, None, hb, d, d), lambda g, n: (g, n, 0, 0, 0)),
                 pl.BlockSpec((None, None, hb, c, c), lambda g, n: (g, n, 0, 0, 0))]
    return _call(body, name="gdn_chunks_fwd", out_shape=out_shape, grid=(hg, nc), in_specs=in_specs,
                 out_specs=out_specs, scratch_shapes=[pltpu.VMEM((hb, d, d), F32)],
                 semantics=("parallel", "arbitrary"))(sqkv, sqkv, sqkv, z.arr, braw, araw, alog, dtb, nw)


def _gdn_chunks_bwd(sqkv, z, braw, araw, alog, dtb, nw, ssave, tsave, dy, hb=GDN_HEADS_PER_STEP):
    t, w3 = sqkv.shape
    w = w3 // 3
    d = GDN_HEAD_DIM
    h = w // d
    hb = min(hb, h)
    hg = h // hb
    c = CHUNK
    nc = t // c

    def body(q_ref, k_ref, v_ref, z_ref, b_ref, a_ref, alog_ref, dtb_ref, nw_ref, ssave_ref, tsave_ref, dy_ref,
             dsq_ref, dz_ref, db_ref, da_ref, dalog_ref, ddtb_ref, dnw_ref, ds_ref):
        first = jnp.logical_and(pl.program_id(0) == 0, pl.program_id(1) == 0)

        @pl.when(pl.program_id(1) == 0)
        def _():
            ds_ref[...] = jnp.zeros_like(ds_ref)

        @pl.when(first)
        def _():
            dalog_ref[...] = jnp.zeros_like(dalog_ref)
            ddtb_ref[...] = jnp.zeros_like(ddtb_ref)
            dnw_ref[...] = jnp.zeros_like(dnw_ref)

        lane_h = lax.broadcasted_iota(jnp.int32, (1, h), 1)
        db_acc = jnp.zeros((c, h), F32)
        da_acc = jnp.zeros((c, h), F32)
        args = _gdn_stack((q_ref, k_ref, v_ref, z_ref, b_ref, a_ref, alog_ref, dtb_ref), hb, d, h,
                          pl.program_id(0) * hb)
        t_saved = tsave_ref[...]
        _, vjp = jax.vjp(lambda *a: _gdn_heads(*a, t_saved=t_saved), *args, nw_ref[...], ssave_ref[...])
        dyb = jnp.stack([dy_ref[:, i * d:(i + 1) * d] for i in range(hb)])
        dq, dk, dv, dz, db, da, dalog, ddtb, dnw, ds_in = vjp((dyb, ds_ref[...]))
        for i in range(hb):
            sl = slice(i * d, (i + 1) * d)
            dsq_ref[:, i * d:(i + 1) * d] = dq[i]
            dsq_ref[:, w + i * d:w + (i + 1) * d] = dk[i]
            dsq_ref[:, 2 * w + i * d:2 * w + (i + 1) * d] = dv[i]
            dz_ref[:, sl] = dz[i].astype(dz_ref.dtype)
            onehot = (lane_h == pl.program_id(0) * hb + i).astype(F32)
            db_acc = db_acc + db[i] * onehot
            da_acc = da_acc + da[i] * onehot
            dalog_ref[...] += dalog[i] * onehot
            ddtb_ref[...] += ddtb[i] * onehot
        dnw_ref[...] += dnw
        ds_ref[...] = ds_in
        db_ref[...] = db_acc
        da_ref[...] = da_acc

    blk = (c, hb * d)
    rev = lambda n: nc - 1 - n
    z = _win(z)
    zb = z.off // (hb * d)
    assert z.off % (hb * d) == 0
    in_specs = [pl.BlockSpec(blk, lambda g, n: (rev(n), g)), pl.BlockSpec(blk, lambda g, n: (rev(n), hg + g)),
                pl.BlockSpec(blk, lambda g, n: (rev(n), 2 * hg + g)), pl.BlockSpec(blk, lambda g, n: (rev(n), zb + g)),
                pl.BlockSpec((c, h), lambda g, n: (rev(n), 0)), pl.BlockSpec((c, h), lambda g, n: (rev(n), 0)),
                pl.BlockSpec((1, h), lambda g, n: (0, 0)), pl.BlockSpec((1, h), lambda g, n: (0, 0)),
                pl.BlockSpec((1, d), lambda g, n: (0, 0)),
                pl.BlockSpec((None, None, hb, d, d), lambda g, n: (g, rev(n), 0, 0, 0)),
                pl.BlockSpec((None, None, hb, c, c), lambda g, n: (g, rev(n), 0, 0, 0)),
                pl.BlockSpec(blk, lambda g, n: (rev(n), g))]
    assert hg == 1
    out_shape = [_sds((t, w3), F32), _sds((t, w), BF16),
                 _sds((hg, t, h), F32), _sds((hg, t, h), F32), _sds((1, h), F32), _sds((1, h), F32), _sds((1, d), F32)]
    out_specs = [pl.BlockSpec((c, w3), lambda g, n: (rev(n), 0)), pl.BlockSpec(blk, lambda g, n: (rev(n), g))]
    out_specs += [pl.BlockSpec((None, c, h), lambda g, n: (g, rev(n), 0))] * 2
    out_specs += [pl.BlockSpec((1, h), lambda g, n: (0, 0)), pl.BlockSpec((1, h), lambda g, n: (0, 0)),
                  pl.BlockSpec((1, d), lambda g, n: (0, 0))]
    return _call(body, name="gdn_chunks_bwd", out_shape=out_shape, grid=(hg, nc), in_specs=in_specs,
                 out_specs=out_specs, scratch_shapes=[pltpu.VMEM((hb, d, d), F32)],
                 semantics=("arbitrary", "arbitrary"))(sqkv, sqkv, sqkv, z.arr, braw, araw, alog, dtb, nw, ssave, tsave,
                                                       dy)


def _m2_groups(xs, z, bm, cm, dtraws, alogs, dtbs, dsks, nw, st):
    g, c, gw = xs.shape
    rep = len(dtraws)
    causal, _, lower, upper_t, _ = _chunk_masks(c)
    lower_b = jnp.broadcast_to(lower[None], (g, c, c))
    lane_head = lax.broadcasted_iota(jnp.int32, (1, 1, gw), 2) // M2_HEAD_DIM

    def expand(cols):
        res = jnp.broadcast_to(cols[-1], (g, cols[-1].shape[1], gw))
        for i in reversed(range(rep - 1)):
            res = jnp.where(lane_head == i, cols[i], res)
        return res

    dts = [_softplus(dtraws[i] + dtbs[i]) for i in range(rep)]
    adts = [-jnp.exp(alogs[i]) * dts[i] for i in range(rep)]
    dt_l, adt_l, dsk_l = expand(dts), expand(adts), expand(dsks)
    xdt = xs * dt_l
    acum = _bnn(lower_b, adt_l, HI)
    alast = jnp.sum(adt_l, axis=1, keepdims=True)
    scores = _bnt(cm, bm)
    y = jnp.exp(acum) * _bnn(cm, st) + dsk_l * xs
    for i in range(rep):
        seg = jnp.where(causal[None], jnp.exp(_bnn(lower_b, adts[i] * upper_t[None], HI)), 0.0)
        y = y + _bnn(scores * seg, jnp.where(lane_head == i, xdt, 0.0))
    st_out = st * jnp.exp(alast) + _btn(bm, xdt * jnp.exp(alast - acum))
    y2 = y * _silu(z)
    out = y2 * lax.rsqrt(jnp.mean(y2 * y2, axis=-1, keepdims=True) + NORM_EPS) * nw
    return out, st_out


def _m2_dims(sxbc, z):
    t = sxbc.shape[0]
    w2 = z.shape[1]
    g = M2_GROUPS
    n = M2_STATE
    assert sxbc.shape[1] == w2 + 2 * g * n
    gw = w2 // g
    return t, w2, g, n, gw, gw // M2_HEAD_DIM, t // CHUNK


def _m2_args(refs, g, n, gw, rep, st):
    sx_ref, z_ref, dt_ref, alog_ref, dtb_ref, dsk_ref, nw_ref = refs
    w2 = g * gw
    xs = jnp.stack([sx_ref[:, i * gw:(i + 1) * gw] for i in range(g)])
    z = jnp.stack([z_ref[:, i * gw:(i + 1) * gw] for i in range(g)])
    bm = jnp.stack([sx_ref[:, w2 + i * n:w2 + (i + 1) * n] for i in range(g)])
    cm = jnp.stack([sx_ref[:, w2 + (g + i) * n:w2 + (g + i + 1) * n] for i in range(g)])
    nw = jnp.stack([nw_ref[:, i * gw:(i + 1) * gw] for i in range(g)])

    def cols(ref):
        blk = ref[...]
        return [jnp.stack([_lane_pick(blk, gi * rep + i) for gi in range(g)]) for i in range(rep)]

    return xs, z, bm, cm, cols(dt_ref), cols(alog_ref), cols(dtb_ref), cols(dsk_ref), nw, st


def _m2_chunks_fwd(sxbc, z, dtraw, alog, dtb, dsk, nw):
    t, w2, g, n, gw, rep, nc = _m2_dims(sxbc, z)
    hm = dtraw.shape[1]
    c = CHUNK
    wx = sxbc.shape[1]

    def body(sx_ref, z_ref, dt_ref, alog_ref, dtb_ref, dsk_ref, nw_ref, y_ref, ssave_ref, st_ref):
        @pl.when(pl.program_id(0) == 0)
        def _():
            st_ref[...] = jnp.zeros_like(st_ref)

        st = st_ref[...]
        ssave_ref[...] = st
        y, st_out = _m2_groups(*_m2_args((sx_ref, z_ref, dt_ref, alog_ref, dtb_ref, dsk_ref, nw_ref), g, n, gw, rep, st))
        for i in range(g):
            y_ref[:, i * gw:(i + 1) * gw] = y[i]
        st_ref[...] = st_out

    z = _win(z)
    zb = z.off // w2
    assert z.off % w2 == 0
    in_specs = [pl.BlockSpec((c, wx), lambda k: (k, 0)), pl.BlockSpec((c, w2), lambda k: (k, zb)),
                pl.BlockSpec((c, hm), lambda k: (k, 0)),
                pl.BlockSpec((1, hm), lambda k: (0, 0)), pl.BlockSpec((1, hm), lambda k: (0, 0)),
                pl.BlockSpec((1, hm), lambda k: (0, 0)), pl.BlockSpec((1, w2), lambda k: (0, 0))]
    out_shape = [_sds((t, w2), F32), _sds((nc, g, n, gw), F32)]
    out_specs = [pl.BlockSpec((c, w2), lambda k: (k, 0)), pl.BlockSpec((None, g, n, gw), lambda k: (k, 0, 0, 0))]
    return _call(body, name="m2_chunks_fwd", out_shape=out_shape, grid=(nc,), in_specs=in_specs,
                 out_specs=out_specs, scratch_shapes=[pltpu.VMEM((g, n, gw), F32)],
                 semantics=("arbitrary",))(sxbc, z.arr, dtraw, alog, dtb, dsk, nw)


def _m2_chunks_bwd(sxbc, z, dtraw, alog, dtb, dsk, nw, ssave, dy):
    t, w2, g, n, gw, rep, nc = _m2_dims(sxbc, z)
    hm = dtraw.shape[1]
    c = CHUNK

    wx = sxbc.shape[1]

    def body(sx_ref, z_ref, dt_ref, alog_ref, dtb_ref, dsk_ref, nw_ref, ssave_ref, dy_ref,
             dsx_ref, dz_ref, ddt_ref, dalog_ref, ddtb_ref, ddsk_ref, dnw_ref, dst_ref):
        @pl.when(pl.program_id(0) == 0)
        def _():
            dst_ref[...] = jnp.zeros_like(dst_ref)
            dnw_ref[...] = jnp.zeros_like(dnw_ref)
            dalog_ref[...] = jnp.zeros_like(dalog_ref)
            ddtb_ref[...] = jnp.zeros_like(ddtb_ref)
            ddsk_ref[...] = jnp.zeros_like(ddsk_ref)

        args = _m2_args((sx_ref, z_ref, dt_ref, alog_ref, dtb_ref, dsk_ref, nw_ref), g, n, gw, rep, ssave_ref[...])
        _, vjp = jax.vjp(_m2_groups, *args)
        dyb = jnp.stack([dy_ref[:, i * gw:(i + 1) * gw] for i in range(g)])
        dxs, dz, dbm, dcm, ddts, dalogs, ddtbs, ddsks, dnw, dst = vjp((dyb, dst_ref[...]))
        dst_ref[...] = dst
        lane_h = lax.broadcasted_iota(jnp.int32, (1, hm), 1)
        ddt = jnp.zeros((c, hm), F32)
        for gi in range(g):
            dsx_ref[:, gi * gw:(gi + 1) * gw] = dxs[gi]
            dsx_ref[:, w2 + gi * n:w2 + (gi + 1) * n] = dbm[gi]
            dsx_ref[:, w2 + (g + gi) * n:w2 + (g + gi + 1) * n] = dcm[gi]
            dz_ref[:, gi * gw:(gi + 1) * gw] = dz[gi].astype(dz_ref.dtype)
            dnw_ref[:, gi * gw:(gi + 1) * gw] += dnw[gi]
            for i in range(rep):
                onehot = (lane_h == gi * rep + i).astype(F32)
                ddt = ddt + ddts[i][gi] * onehot
                dalog_ref[...] += dalogs[i][gi] * onehot
                ddtb_ref[...] += ddtbs[i][gi] * onehot
                ddsk_ref[...] += ddsks[i][gi] * onehot
        ddt_ref[...] = ddt

    rev = lambda k: nc - 1 - k
    z = _win(z)
    zb = z.off // w2
    assert z.off % w2 == 0
    in_specs = [pl.BlockSpec((c, wx), lambda k: (rev(k), 0)), pl.BlockSpec((c, w2), lambda k: (rev(k), zb)),
                pl.BlockSpec((c, hm), lambda k: (rev(k), 0)),
                pl.BlockSpec((1, hm), lambda k: (0, 0)), pl.BlockSpec((1, hm), lambda k: (0, 0)),
                pl.BlockSpec((1, hm), lambda k: (0, 0)), pl.BlockSpec((1, w2), lambda k: (0, 0)),
                pl.BlockSpec((None, g, n, gw), lambda k: (rev(k), 0, 0, 0)),
                pl.BlockSpec((c, w2), lambda k: (rev(k), 0))]
    out_shape = [_sds((t, wx), F32), _sds((t, w2), BF16), _sds((t, hm), F32), _sds((1, hm), F32), _sds((1, hm), F32),
                 _sds((1, hm), F32), _sds((1, w2), F32)]
    out_specs = [pl.BlockSpec((c, wx), lambda k: (rev(k), 0)), pl.BlockSpec((c, w2), lambda k: (rev(k), 0)),
                 pl.BlockSpec((c, hm), lambda k: (rev(k), 0)),
                 pl.BlockSpec((1, hm), lambda k: (0, 0)), pl.BlockSpec((1, hm), lambda k: (0, 0)),
                 pl.BlockSpec((1, hm), lambda k: (0, 0)), pl.BlockSpec((1, w2), lambda k: (0, 0))]
    return _call(body, name="m2_chunks_bwd", out_shape=out_shape, grid=(nc,), in_specs=in_specs,
                 out_specs=out_specs, scratch_shapes=[pltpu.VMEM((g, n, gw), F32)],
                 semantics=("arbitrary",))(sxbc, z.arr, dtraw, alog, dtb, dsk, nw, ssave, dy)


def _s5_scan(bu3, aa3, bb3, reverse, name, tb=256):
    t, rtot, _ = bu3.shape
    jn = rtot // SUBLANES
    tb = _tile(t, tb, SUBLANES)
    nb = t // tb

    def body(bu_ref, aa_ref, bb_ref, s_ref, st_ref):
        @pl.when(pl.program_id(0) == 0)
        def _():
            st_ref[...] = jnp.zeros_like(st_ref)

        tiles = [slice(j * SUBLANES, (j + 1) * SUBLANES) for j in range(jn)]
        aa = [aa_ref[tl, :] for tl in tiles]
        bb = [bb_ref[tl, :] for tl in tiles]

        def step(k, carry):
            r = tb - 1 - k if reverse else k
            new = []
            for j, tl in enumerate(tiles):
                sj = aa[j] * carry[j] + bb[j] * pltpu.roll(carry[j], SUBLANES // 2, 0) + bu_ref[r, tl, :]
                s_ref[r, tl, :] = sj
                new.append(sj)
            return tuple(new)

        last = lax.fori_loop(0, tb, step, tuple(st_ref[tl, :] for tl in tiles), unroll=8)
        for j, tl in enumerate(tiles):
            st_ref[tl, :] = last[j]

    rb = (lambda i: nb - 1 - i) if reverse else (lambda i: i)
    return _call(body, name=name, out_shape=_sds(bu3.shape, F32), grid=(nb,),
                 in_specs=[pl.BlockSpec((tb, rtot, LANES), lambda i: (rb(i), 0, 0)),
                           pl.BlockSpec((rtot, LANES), lambda i: (0, 0)), pl.BlockSpec((rtot, LANES), lambda i: (0, 0))],
                 out_specs=pl.BlockSpec((tb, rtot, LANES), lambda i: (rb(i), 0, 0)),
                 scratch_shapes=[pltpu.VMEM((rtot, LANES), F32)], semantics=("arbitrary",))(bu3, aa3, bb3)


def _s5_da(ds3, s3, tb=256):
    t, rtot, _ = ds3.shape
    tb = _tile(t, tb, SUBLANES)
    nb = t // tb

    def body(ds_ref, s_ref, halo_ref, p_ref, q_ref):
        i = pl.program_id(0)

        @pl.when(i == 0)
        def _():
            p_ref[...] = jnp.zeros_like(p_ref)
            q_ref[...] = jnp.zeros_like(q_ref)

        for j in range(rtot // SUBLANES):
            tl = slice(j * SUBLANES, (j + 1) * SUBLANES)
            prev = jnp.where(i == 0, 0.0, halo_ref[:, tl, :])
            sh = jnp.concatenate([prev, s_ref[0:tb - 1, tl, :]], axis=0)
            d = ds_ref[:, tl, :]
            p_ref[tl, :] += jnp.sum(d * sh, axis=0)
            q_ref[tl, :] += jnp.sum(d * pltpu.roll(sh, SUBLANES // 2, 1), axis=0)

    blk = pl.BlockSpec((tb, rtot, LANES), lambda i: (i, 0, 0))
    acc = pl.BlockSpec((rtot, LANES), lambda i: (0, 0))
    return _call(body, name="s5_da", out_shape=[_sds((rtot, LANES), F32)] * 2, grid=(nb,),
                 in_specs=[blk, blk, pl.BlockSpec((1, rtot, LANES), lambda i: (jnp.maximum(i * tb - 1, 0), 0, 0))],
                 out_specs=[acc, acc], semantics=("arbitrary",))(ds3, s3, s3)


def _conv_rows(t, tb):
    tb = _tile(t, tb, SUBLANES * 2)
    return tb, t // tb, tb // SUBLANES


def _shift_down(x, above, s):
    n = x.shape[0]
    y = pltpu.roll(x, s, 0)
    row = lax.broadcasted_iota(jnp.int32, above.shape, 0)
    head = jnp.where(row < s, pltpu.roll(above, s, 0), y[:SUBLANES])
    return head if n == SUBLANES else jnp.concatenate([head, y[SUBLANES:]], axis=0)


def _shift_up(x, below, s):
    n = x.shape[0]
    y = pltpu.roll(x, n - s, 0)
    row = lax.broadcasted_iota(jnp.int32, below.shape, 0)
    tail = jnp.where(row >= SUBLANES - s, pltpu.roll(below, SUBLANES - s, 0), y[n - SUBLANES:])
    return tail if n == SUBLANES else jnp.concatenate([y[:n - SUBLANES], tail], axis=0)


def _conv_taps(x, above, w_ref, b_ref):
    xs = [x] + [_shift_down(x, above, s) for s in range(1, CONV_K)]
    c = b_ref[...] + w_ref[CONV_K - 1:CONV_K, :] * x
    for s in range(1, CONV_K):
        c = c + w_ref[CONV_K - 1 - s:CONV_K - s, :] * xs[s]
    return c, xs


CONV_COL_TILE = 1024


def _conv_specs(x, tb):
    x = _win(x)
    t, cwid = x.shape
    ct = _col_tile([x], CONV_COL_TILE)
    tb, nb, hb = _conv_rows(t, max(tb, BLOCK_BYTES // (4 * ct)))
    base = x.off // ct
    blk_x = pl.BlockSpec((tb, ct), lambda j, i: (i, base + j))
    prev_x = pl.BlockSpec((SUBLANES, ct), lambda j, i: (jnp.maximum(i * hb - 1, 0), base + j))
    next_x = pl.BlockSpec((SUBLANES, ct), lambda j, i: (jnp.minimum((i + 1) * hb, nb * hb - 1), base + j))
    blk = pl.BlockSpec((tb, ct), lambda j, i: (i, j))
    nxt = pl.BlockSpec((SUBLANES, ct), lambda j, i: (jnp.minimum((i + 1) * hb, nb * hb - 1), j))
    taps = pl.BlockSpec((CONV_K, ct), lambda j, i: (0, j))
    bias = pl.BlockSpec((1, ct), lambda j, i: (0, j))
    return x, tb, nb, cwid // ct, dict(blk_x=blk_x, prev_x=prev_x, next_x=next_x, blk=blk, nxt=nxt, taps=taps, bias=bias)


def _conv_fwd(x, w, b, name, tb=256):
    x, tb, nb, ncol, sp = _conv_specs(x, tb)
    t, cwid = x.shape

    def body(x_ref, halo_ref, w_ref, b_ref, o_ref):
        above = jnp.where(pl.program_id(1) == 0, 0.0, halo_ref[...])
        o_ref[...] = _silu(_conv_taps(x_ref[...], above, w_ref, b_ref)[0])

    return _call(body, name=name, out_shape=_sds((t, cwid), F32), grid=(ncol, nb),
                 in_specs=[sp['blk_x'], sp['prev_x'], sp['taps'], sp['bias']], out_specs=sp['blk'],
                 semantics=("parallel", "parallel"))(x.arr, x.arr, w, b)


def _dsilu(c):
    sg = _sigmoid(c)
    return sg * (1.0 + c * (1.0 - sg))


def _conv_bwd(x, w, b, ds, name, tb=256):
    x, tb, nb, ncol, sp = _conv_specs(x, tb)
    t, cwid = x.shape

    def body(x_ref, halo_ref, xn_ref, ds_ref, dsn_ref, w_ref, b_ref, dx_ref, dw_ref, db_ref):
        i = pl.program_id(1)

        @pl.when(i == 0)
        def _():
            dw_ref[...] = jnp.zeros_like(dw_ref)
            db_ref[...] = jnp.zeros_like(db_ref)

        x = x_ref[...]
        above = jnp.where(i == 0, 0.0, halo_ref[...])
        c, xs = _conv_taps(x, above, w_ref, b_ref)
        dc = ds_ref[...] * _dsilu(c)
        cn, _ = _conv_taps(xn_ref[...], x[tb - SUBLANES:], w_ref, b_ref)
        dcn = jnp.where(i == nb - 1, 0.0, dsn_ref[...] * _dsilu(cn))
        dx = w_ref[CONV_K - 1:CONV_K, :] * dc
        for s in range(1, CONV_K):
            dx = dx + w_ref[CONV_K - 1 - s:CONV_K - s, :] * _shift_up(dc, dcn, s)
        dx_ref[...] = dx.astype(dx_ref.dtype)
        for s in range(CONV_K):
            dw_ref[CONV_K - 1 - s:CONV_K - s, :] += jnp.sum(dc * xs[s], axis=0, keepdims=True)
        db_ref[...] += jnp.sum(dc, axis=0, keepdims=True)

    return _call(body, name=name, out_shape=[_sds((t, cwid), BF16), _sds((CONV_K, cwid), F32), _sds((1, cwid), F32)],
                 grid=(ncol, nb),
                 in_specs=[sp['blk_x'], sp['prev_x'], sp['next_x'], sp['blk'], sp['nxt'], sp['taps'], sp['bias']],
                 out_specs=[sp['blk'], sp['taps'], sp['bias']],
                 semantics=("parallel", "arbitrary"))(x.arr, x.arr, x.arr, ds, ds, w, b)


def _f_rms(x, w):
    return (x * lax.rsqrt(jnp.mean(x * x, axis=-1, keepdims=True) + NORM_EPS) * w,)


def _f_s5_post1(ymm, u, d_l):
    return (_gelu_tanh(ymm + d_l * u),)


def _f_s5_post2(yg, tt, gate, b):
    return (yg * _sigmoid(tt + b) * _silu(gate),)


def _f_merge(ma, mb, mc, pa, pb, pc):
    return (_sigmoid(ma) * pa + _sigmoid(mb) * pb + _sigmoid(mc) * pc,)


def _loss_and_grad(x, tgt, fw, tb=256):
    t, dm = x.shape
    tb = _tile(t, tb, SUBLANES * 2)

    def f(xb, wb, tb_):
        y = _f_rms(xb, wb)[0]
        e = y - tb_
        return 0.5 * jnp.sum(jnp.mean(e * e, axis=-1, keepdims=True), axis=0, keepdims=True)

    def body(x_ref, t_ref, w_ref, loss_ref, dx_ref, dw_ref):
        @pl.when(pl.program_id(0) == 0)
        def _():
            loss_ref[...] = jnp.zeros_like(loss_ref)
            dw_ref[...] = jnp.zeros_like(dw_ref)

        tgt_b = t_ref[...]
        val, vjp = jax.vjp(lambda a, b: f(a, b, tgt_b), x_ref[...], w_ref[...])
        dxb, dwb = vjp(jnp.ones((1, 1), F32))
        loss_ref[...] += val
        dx_ref[...] = dxb
        dw_ref[...] += dwb

    return _call(body, name="loss_and_grad", out_shape=[_sds((1, 1), F32), _sds((t, dm), F32), _sds((1, dm), F32)],
                 grid=(t // tb,),
                 in_specs=[pl.BlockSpec((tb, dm), lambda i: (i, 0)), pl.BlockSpec((tb, dm), lambda i: (i, 0)),
                           pl.BlockSpec((1, dm), lambda i: (0, 0))],
                 out_specs=[pl.BlockSpec((1, 1), lambda i: (0, 0)), pl.BlockSpec((tb, dm), lambda i: (i, 0)),
                            pl.BlockSpec((1, dm), lambda i: (0, 0))],
                 semantics=("arbitrary",))(x, tgt, fw)


FLAT_W = 1024


def _sum_parts(parts, name, tb=256):
    n, r, wd = parts.shape
    tb = _tile(r, tb, SUBLANES)

    def body(p_ref, o_ref):
        acc = p_ref[0]
        for k in range(1, n):
            acc = acc + p_ref[k]
        o_ref[...] = acc

    return _call(body, name=name, out_shape=_sds((r, wd), F32), grid=(r // tb,),
                 in_specs=[pl.BlockSpec((n, tb, wd), lambda i: (0, i, 0))],
                 out_specs=pl.BlockSpec((tb, wd), lambda i: (i, 0)), semantics=("parallel",))(parts)


BLOCK_BYTES = 1 << 20


def _rows_per_block(r, wd):
    return _tile(r, max(SUBLANES * 2, BLOCK_BYTES // (4 * wd) // (SUBLANES * 2) * (SUBLANES * 2)), SUBLANES * 2)


def _add_my_half(g4, recv, c_idx, name):
    p, _, r, wd = g4.shape
    tb = _rows_per_block(r, wd)

    def body(c_ref, g_ref, r_ref, o_ref, ob_ref):
        s = g_ref[...] + r_ref[...]
        o_ref[...] = s
        ob_ref[...] = s.astype(BF16)

    spec = pl.BlockSpec((None, tb, wd), lambda j, i, c_ref: (j, i, 0))
    return _call(body, name=name, out_shape=[_sds((p, r, wd), F32), _sds((p, r, wd), BF16)], grid=(p, r // tb),
                 in_specs=[pl.BlockSpec((None, None, tb, wd), lambda j, i, c_ref: (j, c_ref[0], i, 0)), spec],
                 out_specs=[spec, spec], semantics=("parallel", "parallel"), num_scalar_prefetch=1)(c_idx, g4, recv)


def _sum_chips(own, got, me_idx, name):
    p, r, wd = own.shape
    tb = _rows_per_block(r, wd)

    def body(me_ref, own_ref, got_ref, o_ref):
        me = me_ref[0]
        acc = None
        for k in range(p):
            part = jnp.where(me == k, own_ref[...], got_ref[k].astype(F32))
            acc = part if acc is None else acc + part
        o_ref[...] = acc

    return _call(body, name=name, out_shape=_sds((r, wd), F32), grid=(r // tb,),
                 in_specs=[pl.BlockSpec((None, tb, wd), lambda i, me_ref: (me_ref[0], i, 0)),
                           pl.BlockSpec((p, tb, wd), lambda i, me_ref: (0, i, 0))],
                 out_specs=pl.BlockSpec((tb, wd), lambda i, me_ref: (i, 0)),
                 semantics=("parallel",), num_scalar_prefetch=1)(me_idx, own, got)


def _adamw(w, g, m, v, name):
    r, wd = w.shape
    tb = _rows_per_block(r, wd)

    def body(w_ref, g_ref, m_ref, v_ref, d_ref, nm_ref, nv_ref):
        gg = g_ref[...]
        nm = ADAM_B1 * m_ref[...] + (1.0 - ADAM_B1) * gg
        nv = ADAM_B2 * v_ref[...] + (1.0 - ADAM_B2) * (gg * gg)
        m_hat = nm / (1.0 - ADAM_B1 ** ADAM_STEP)
        v_hat = nv / (1.0 - ADAM_B2 ** ADAM_STEP)
        d_ref[...] = -ADAM_LR * (m_hat / (jnp.sqrt(v_hat) + ADAM_EPS) + ADAM_WD * w_ref[...])
        nm_ref[...] = nm
        nv_ref[...] = nv

    spec = pl.BlockSpec((tb, wd), lambda i: (i, 0))
    return _call(body, name=name, out_shape=[_sds((r, wd), F32)] * 3, grid=(r // tb,), in_specs=[spec] * 4,
                 out_specs=[spec] * 3, semantics=("parallel",))(w, g, m, v)


def _here():
    return lax.axis_index("x"), lax.axis_index("y"), lax.axis_index("c")


def _comm_call(body, name, out_shape, n_sems, operands):
    anyspec = pl.BlockSpec(memory_space=pl.ANY)
    outs = out_shape if isinstance(out_shape, (list, tuple)) else [out_shape]
    return _call(body, name=name, out_shape=out_shape, in_specs=[anyspec] * len(operands),
                 out_specs=[anyspec] * len(outs) if isinstance(out_shape, (list, tuple)) else anyspec,
                 scratch_shapes=[pltpu.SemaphoreType.DMA((n_sems,)), pltpu.SemaphoreType.DMA((n_sems,)),
                                 pltpu.SemaphoreType.DMA(())])(*operands)


def _gather_chips(x, name):
    def body(x_ref, o_ref, send_sems, recv_sems, local_sem):
        xi, yi, ci = _here()
        chips = [(1 - xi, yi), (xi, 1 - yi), (1 - xi, 1 - yi)]
        mine = pltpu.make_async_copy(x_ref, o_ref.at[2 * xi + yi], local_sem)
        mine.start()

        def copy(k, slot, to):
            return pltpu.make_async_remote_copy(src_ref=x_ref, dst_ref=o_ref.at[slot], send_sem=send_sems.at[k],
                                                recv_sem=recv_sems.at[k], device_id=to, device_id_type=MESH)

        sends = [copy(k, 2 * xi + yi, (px, py, ci)) for k, (px, py) in enumerate(chips)]
        for cp in sends:
            cp.start()
        for k, (px, py) in enumerate(chips):
            copy(k, 2 * px + py, (px, py, ci)).wait_recv()
        for cp in sends:
            cp.wait_send()
        mine.wait()

    return _comm_call(body, name, _sds((4,) + x.shape, x.dtype), 3, (x,))


def _gather_all(x, name):
    def body(x_ref, o_ref, send_sems, recv_sems, local_sem):
        xi, yi, ci = _here()
        me = 4 * xi + 2 * yi + ci
        flips = [(fx, fy, fc) for fx in (0, 1) for fy in (0, 1) for fc in (0, 1)][1:]
        peers = [((1 - xi) if fx else xi, (1 - yi) if fy else yi, (1 - ci) if fc else ci) for fx, fy, fc in flips]
        mine = pltpu.make_async_copy(x_ref, o_ref.at[me], local_sem)
        mine.start()

        def copy(k, slot, to):
            return pltpu.make_async_remote_copy(src_ref=x_ref, dst_ref=o_ref.at[slot], send_sem=send_sems.at[k],
                                                recv_sem=recv_sems.at[k], device_id=to, device_id_type=MESH)

        sends = [copy(k, me, p) for k, p in enumerate(peers)]
        for cp in sends:
            cp.start()
        for k, (px, py, pc) in enumerate(peers):
            copy(k, 4 * px + 2 * py + pc, (px, py, pc)).wait_recv()
        for cp in sends:
            cp.wait_send()
        mine.wait()

    return _comm_call(body, name, _sds((8,) + x.shape, x.dtype), 7, (x,))


def _multi_comm_call(body, name, out_shapes, n_sems, operands):
    anyspec = pl.BlockSpec(memory_space=pl.ANY)
    nin = len(operands)

    def flat_body(*refs):
        body(refs[:nin], refs[nin:nin + len(out_shapes)], refs[-2], refs[-1])

    return _call(flat_body, name=name, out_shape=list(out_shapes), in_specs=[anyspec] * nin,
                 out_specs=[anyspec] * len(out_shapes),
                 scratch_shapes=[pltpu.SemaphoreType.DMA((n_sems,)), pltpu.SemaphoreType.DMA((n_sems,))])(*operands)


def _remote(src, dst, send_sems, recv_sems, k, to):
    return pltpu.make_async_remote_copy(src_ref=src, dst_ref=dst, send_sem=send_sems.at[k], recv_sem=recv_sems.at[k],
                                        device_id=to, device_id_type=MESH)


def _gather_chips_split(xs, name):
    nw = len(xs)

    def body(x_refs, o_refs, send_sems, recv_sems):
        xi, yi, ci = _here()
        me = 2 * xi + yi
        sib = (xi, yi, 1 - ci)
        chips = [(1 - xi, yi), (xi, 1 - yi), (1 - xi, 1 - yi)]
        sends = []
        for i in range(nw):
            for k, (px, py) in enumerate(chips):
                sends.append(_remote(x_refs[i].at[ci], o_refs[i].at[me, ci], send_sems, recv_sems, 3 * i + k,
                                     (px, py, ci)))
        for cp in sends:
            cp.start()
        passed = []
        for i in range(nw):
            for k, (px, py) in enumerate(chips):
                landed = o_refs[i].at[2 * px + py, ci]
                _remote(landed, landed, send_sems, recv_sems, 3 * i + k, (px, py, ci)).wait_recv()
                fwd = _remote(landed, landed, send_sems, recv_sems, 3 * (nw + i) + k, sib)
                fwd.start()
                passed.append(fwd)
        for i in range(nw):
            for k, (px, py) in enumerate(chips):
                other = o_refs[i].at[2 * px + py, 1 - ci]
                _remote(other, other, send_sems, recv_sems, 3 * (nw + i) + k, sib).wait_recv()
        for cp in sends + passed:
            cp.wait_send()

    return _multi_comm_call(body, name, [_sds((4,) + x.shape, x.dtype) for x in xs], 6 * nw, xs)


def _swap_sibling_half(gs):
    def body(g_refs, o_refs, send_sems, recv_sems):
        xi, yi, ci = _here()
        cps = [_remote(g.at[:, 1 - ci], o, send_sems, recv_sems, i, (xi, yi, 1 - ci))
               for i, (g, o) in enumerate(zip(g_refs, o_refs))]
        for cp in cps:
            cp.start()
        for cp in cps:
            cp.wait()

    return _multi_comm_call(body, "swap_sibling_half", [_sds((g.shape[0],) + g.shape[2:], g.dtype) for g in gs],
                            len(gs), gs)


def _scatter_chips(gps):
    nw = len(gps)

    def body(g_refs, o_refs, send_sems, recv_sems):
        xi, yi, ci = _here()
        me = 2 * xi + yi
        chips = [(1 - xi, yi), (xi, 1 - yi), (1 - xi, 1 - yi)]
        sends = [_remote(g_refs[i].at[2 * px + py], o_refs[i].at[me], send_sems, recv_sems, 3 * i + k, (px, py, ci))
                 for i in range(nw) for k, (px, py) in enumerate(chips)]
        for cp in sends:
            cp.start()
        for i in range(nw):
            for k, (px, py) in enumerate(chips):
                slot = o_refs[i].at[2 * px + py]
                _remote(slot, slot, send_sems, recv_sems, 3 * i + k, (px, py, ci)).wait_recv()
        for cp in sends:
            cp.wait_send()

    return _multi_comm_call(body, "scatter_chips", [_sds(g.shape, g.dtype) for g in gps], 3 * nw, gps)


def _share_sibling(rs):
    def body(r_refs, o_refs, send_sems, recv_sems):
        xi, yi, ci = _here()
        cps = [_remote(r, o, send_sems, recv_sems, i, (xi, yi, 1 - ci)) for i, (r, o) in enumerate(zip(r_refs, o_refs))]
        for cp in cps:
            cp.start()
        for cp in cps:
            cp.wait()

    return _multi_comm_call(body, "share_sibling", [_sds(r.shape, r.dtype) for r in rs], len(rs), rs)


def _flat_pack(arrs, dtype, row_mult):
    flat = jnp.concatenate([a.astype(dtype).reshape(-1) for a in arrs])
    unit = FLAT_W * row_mult
    npad = -(-flat.shape[0] // unit) * unit
    return jnp.pad(flat, (0, npad - flat.shape[0])).reshape(npad // FLAT_W, FLAT_W)


def _flat_unpack(flat2d, shapes):
    flat = flat2d.reshape(-1)
    outs, off = [], 0
    for s in shapes:
        size = int(np.prod(s))
        outs.append(flat[off:off + size].reshape(s))
        off += size
    return outs


def _split_cols(a, widths):
    outs, off = [], 0
    for wd in widths:
        outs.append(lax.slice_in_dim(a, off, off + wd, axis=1))
        off += wd
    return outs


def _s5_params(lam_re, lam_im, log_step, b_re, b_im, c_re, c_im, d_skip):
    g, p = lam_re.shape
    hs = b_re.shape[2]
    gt = S5_GROUP_TILE
    jn = g // gt
    lam_re = jnp.minimum(lam_re, -1e-4)
    step = jnp.exp(log_step)[:, None]
    mag = jnp.exp(lam_re * step)
    ab_re = mag * jnp.cos(lam_im * step)
    ab_im = mag * jnp.sin(lam_im * step)
    den = lam_re * lam_re + lam_im * lam_im
    f_re = ((ab_re - 1.0) * lam_re + ab_im * lam_im) / den
    f_im = (ab_im * lam_re - (ab_re - 1.0) * lam_im) / den
    bb_re = f_re[..., None] * b_re - f_im[..., None] * b_im
    bb_im = f_re[..., None] * b_im + f_im[..., None] * b_re
    a_l = jnp.concatenate([ab_re.reshape(jn, gt * p), ab_im.reshape(jn, gt * p)], axis=1).reshape(1, jn * 2 * gt * p)
    eye = jnp.eye(gt, dtype=F32)

    def blockdiag(m):
        return jnp.einsum('jahp,ab->jahbp', m.reshape(jn, gt, hs, p), eye).reshape(jn, gt * hs, gt * p)

    b_blk = jnp.concatenate([blockdiag(bb_re.transpose(0, 2, 1)), blockdiag(bb_im.transpose(0, 2, 1))], axis=2)
    c_blk = jnp.concatenate([blockdiag(c_re), blockdiag(-c_im)], axis=2)
    return a_l, b_blk, c_blk, d_skip.reshape(1, g * hs)


def _s5_scan_consts(a_l, cw):
    jn, hr = a_l.shape[1] // cw, cw // 2 // LANES
    a4 = a_l.reshape(jn, 2, hr, LANES)
    are, aim = a4[:, 0], a4[:, 1]
    flat = lambda u, v: jnp.concatenate([u, v], axis=1).reshape(jn * 2 * hr, LANES)
    return flat(are, are), flat(-aim, aim), flat(aim, -aim)


def _s5_da_lanes(p, q, cw):
    jn, hr = p.shape[0] * LANES // cw, cw // 2 // LANES
    p4, q4 = p.reshape(jn, 2, hr, LANES), q.reshape(jn, 2, hr, LANES)
    da_re = (p4[:, 0] + p4[:, 1]).reshape(jn, cw // 2)
    da_im = (q4[:, 1] - q4[:, 0]).reshape(jn, cw // 2)
    return jnp.concatenate([da_re, da_im], axis=1).reshape(1, jn * cw)


def _layer_dims(p):
    d_model = p['w_out'].shape[1]
    wa = p['proj_a'].shape[0]
    h = p['gdn_a_log'].shape[0]
    wb = p['proj_b'].shape[0]
    wc = p['proj_c'].shape[0]
    hm = p['m2_a_log'].shape[0]
    cdim = p['m2_conv_w'].shape[1]
    width = dict(zip(PROJ_ORDER, (3 * wa, wa, h, h, wb, wb, wc, cdim, hm, d_model, d_model, d_model)))
    n_in = sum(width.values())
    return width, n_in, -(-n_in // LANES) * LANES


PROJ_ORDER = ('qkv', 'az', 'braw', 'araw', 'su', 'sgate', 'cz', 'cxbc', 'cdt', 'ma', 'mb', 'mc')
WORK_ORDER = ('qkv', 'az', 'cz', 'su', 'sgate', 'cxbc', 'ma', 'mb', 'mc', 'braw', 'araw', 'cdt')
ROW_COL_TILE = 512


def _reorder_cols(a, width, src_order, dst_order, n_out):
    off, o = {}, 0
    for n in src_order:
        off[n] = o
        o += width[n]
    parts = [lax.slice_in_dim(a, off[n], off[n] + width[n], axis=a.ndim - 1) for n in dst_order]
    used = sum(width[n] for n in dst_order)
    if n_out > used:
        parts.append(jnp.zeros(a.shape[:-1] + (n_out - used,), a.dtype))
    return jnp.concatenate(parts, axis=a.ndim - 1)


def _layer_fwd(x, p):
    width, n_in, n_pad = _layer_dims(p)
    sv = {'x': x}
    h = _rowwise(_f_rms, [x], [p['norm_w'][None]], [x.shape[1]], [BF16], "rms_fwd")[0]
    w_in = _reorder_cols(p['w_in'], width, PROJ_ORDER, WORK_ORDER, n_pad)
    proj = _matmul(h, w_in, 'nn', F32, "in_proj", tn=IN_PROJ_TILE)
    wins, off = {}, 0
    for n in WORK_ORDER:
        wins[n] = Win(proj, off, width[n])
        off += width[n]
    qkv, az, cz, su, sgate, cxbc, ma, mb, mc = (wins[n] for n in WORK_ORDER[:9])
    braw, araw, cdt = (lax.slice_in_dim(proj, wins[n].off, wins[n].off + width[n], axis=1) for n in WORK_ORDER[9:])
    sv.update(h=h, w_in=w_in, qkv=qkv, az=az, braw=braw, araw=araw, su=su, sgate=sgate, cz=cz, cxbc=cxbc, cdt=cdt,
              ma=ma, mb=mb, mc=mc)
    gb0 = jnp.zeros((1, qkv.shape[1]), F32)
    sqkv = _conv_fwd(qkv, p['gdn_conv_w'], gb0, "gdn_conv_fwd")
    ya, ssa, tsa = _gdn_chunks_fwd(sqkv, az, braw, araw, p['gdn_a_log'][None], p['gdn_dt_bias'][None],
                                   p['gdn_norm_w'][None])
    sv.update(sqkv=sqkv, ssa=ssa, tsa=tsa, ya=ya)
    s5_in = tuple(p[k] for k in ('s5_lam_re', 's5_lam_im', 's5_log_step', 's5_b_re', 's5_b_im', 's5_c_re', 's5_c_im',
                                 's5_d'))
    (a_l, b_blk, c_blk, d_l), s5_vjp = jax.vjp(_s5_params, *s5_in)
    cw = b_blk.shape[2]
    aa3, bb3, bb3_conj = _s5_scan_consts(a_l, cw)
    bu = _bd_expand(su, b_blk, "s5_bu")
    s = _s5_scan(bu, aa3, bb3, False, "s5_scan_fwd")
    ymm = _bd_reduce(s, c_blk, F32, "s5_out")
    yg = _rowwise(_f_s5_post1, [ymm, su], [d_l], [su.shape[1]], [F32], "s5_post1_fwd", col_tile=ROW_COL_TILE)[0]
    tt = _matmul(yg, p['s5_glu_w'], 'nn', F32, "s5_glu")
    yb = _rowwise(_f_s5_post2, [yg, tt, sgate], [p['s5_glu_b'][None]], [su.shape[1]], [F32], "s5_post2_fwd",
                  col_tile=ROW_COL_TILE)[0]
    sv.update(aa3=aa3, bb3_conj=bb3_conj, b_blk=b_blk, c_blk=c_blk, d_l=d_l, s5_vjp=s5_vjp, s=s, ymm=ymm, yg=yg, tt=tt,
              yb=yb, cw=cw)
    sxbc = _conv_fwd(cxbc, p['m2_conv_w'], p['m2_conv_b'][None], "m2_conv_fwd")
    yc, ssc = _m2_chunks_fwd(sxbc, cz, cdt, p['m2_a_log'][None], p['m2_dt_bias'][None], p['m2_d'][None],
                             p['m2_norm_w'][None])
    sv.update(sxbc=sxbc, ssc=ssc, yc=yc)
    pa = _matmul(ya, p['proj_a'], 'nn', F32, "proj_a")
    pb = _matmul(yb, p['proj_b'], 'nn', F32, "proj_b")
    pc = _matmul(yc, p['proj_c'], 'nn', F32, "proj_c")
    merged = _rowwise(_f_merge, [ma, mb, mc, pa, pb, pc], [], [x.shape[1]], [BF16], "merge_fwd",
                      col_tile=ROW_COL_TILE)[0]
    x_next = _matmul(merged, p['w_out'], 'nn', F32, "out_proj", add=x)
    sv.update(pa=pa, pb=pb, pc=pc, merged=merged)
    return x_next, sv


def _layer_bwd(dx_out, p, sv):
    width, n_in, n_pad = _layer_dims(p)
    g = {}
    dmerged = _matmul(dx_out, p['w_out'], 'nt', F32, "out_proj_dx")
    g['w_out'] = _matmul(sv['merged'], dx_out, 'tn', F32, "out_proj_dw")
    dma, dmb, dmc, dpa, dpb, dpc = _rowwise_bwd(
        _f_merge, [sv['ma'], sv['mb'], sv['mc'], sv['pa'], sv['pb'], sv['pc']], [], [dmerged], [BF16] * 6,
        "merge_bwd", col_tile=ROW_COL_TILE)
    dya = _matmul(dpa, p['proj_a'], 'nt', F32, "proj_a_dx")
    dyb = _matmul(dpb, p['proj_b'], 'nt', F32, "proj_b_dx")
    dyc = _matmul(dpc, p['proj_c'], 'nt', F32, "proj_c_dx")
    g['proj_a'] = _matmul(sv['ya'], dpa, 'tn', F32, "proj_a_dw")
    g['proj_b'] = _matmul(sv['yb'], dpb, 'tn', F32, "proj_b_dw")
    g['proj_c'] = _matmul(sv['yc'], dpc, 'tn', F32, "proj_c_dw")
    alog, dtb, gnw = p['gdn_a_log'][None], p['gdn_dt_bias'][None], p['gdn_norm_w'][None]
    dsq, daz, db3, da3, dalog, ddtb, dgnw = _gdn_chunks_bwd(sv['sqkv'], sv['az'], sv['braw'], sv['araw'], alog, dtb, gnw,
                                                            sv['ssa'], sv['tsa'], dya)
    gb0 = jnp.zeros((1, sv['qkv'].shape[1]), F32)
    dqkv, g['gdn_conv_w'], _ = _conv_bwd(sv['qkv'], p['gdn_conv_w'], gb0, dsq, "gdn_conv_bwd")
    dbraw, daraw = jnp.sum(db3, axis=0), jnp.sum(da3, axis=0)
    g.update(gdn_a_log=dalog[0], gdn_dt_bias=ddtb[0], gdn_norm_w=dgnw[0])
    dsx, dcz, dcdt, dmalog, dmdtb, dmdsk, dmnw = _m2_chunks_bwd(
        sv['sxbc'], sv['cz'], sv['cdt'], p['m2_a_log'][None], p['m2_dt_bias'][None], p['m2_d'][None],
        p['m2_norm_w'][None], sv['ssc'], dyc)
    dcxbc, g['m2_conv_w'], dconvb = _conv_bwd(sv['cxbc'], p['m2_conv_w'], p['m2_conv_b'][None], dsx, "m2_conv_bwd")
    g.update(m2_conv_b=dconvb[0], m2_a_log=dmalog[0], m2_dt_bias=dmdtb[0], m2_d=dmdsk[0], m2_norm_w=dmnw[0])
    dyg1, dtt, dsgate, dglub = _rowwise_bwd(_f_s5_post2, [sv['yg'], sv['tt'], sv['sgate']], [p['s5_glu_b'][None]],
                                            [dyb], [F32, BF16, BF16], "s5_post2_bwd", col_tile=ROW_COL_TILE)
    dyg = _matmul(dtt, p['s5_glu_w'], 'nt', F32, "s5_glu_dx", add=dyg1)
    g['s5_glu_w'] = _matmul(sv['yg'], dtt, 'tn', F32, "s5_glu_dw")
    g['s5_glu_b'] = dglub[0]
    dymm, dsu1, dd_l = _rowwise_bwd(_f_s5_post1, [sv['ymm'], sv['su']], [sv['d_l']], [dyg], [BF16, F32],
                                    "s5_post1_bwd", col_tile=ROW_COL_TILE)
    gy = _bd_expand(dymm, sv['c_blk'], "s5_out_dx")
    ds = _s5_scan(gy, sv['aa3'], sv['bb3_conj'], True, "s5_scan_bwd")
    da_l = _s5_da_lanes(*_s5_da(ds, sv['s']), sv['cw'])
    dsu = _bd_reduce(ds, sv['b_blk'], BF16, "s5_bu_dx", add=dsu1)
    ka = sv['b_blk'].shape[1]
    db_blk = _bd_outer(sv['su'], ds, ka, "s5_bu_dw")
    dc_blk = _bd_outer(dymm, sv['s'], ka, "s5_out_dw")
    for k, v in zip(('s5_lam_re', 's5_lam_im', 's5_log_step', 's5_b_re', 's5_b_im', 's5_c_re', 's5_c_im', 's5_d'),
                    sv['s5_vjp']((da_l, db_blk, dc_blk, dd_l))):
        g[k] = v
    small = jnp.concatenate([dbraw, daraw, dcdt], axis=1).astype(BF16)
    small = jnp.pad(small, ((0, 0), (0, n_pad - n_in + sum(width[n] for n in WORK_ORDER[9:]) - small.shape[1])))
    dproj = _concat_cols([dqkv, daz, dcz, dsu, dsgate, dcxbc, dma, dmb, dmc, small], "concat_dproj")
    dh = _matmul(dproj, sv['w_in'], 'nt', F32, "in_proj_dx", tm=2048, tk=IN_PROJ_TILE)
    g['w_in'] = _reorder_cols(_matmul(sv['h'], dproj, 'tn', F32, "in_proj_dw", tn=IN_PROJ_TILE), width, WORK_ORDER,
                              PROJ_ORDER, n_in)
    dx, dnw = _rowwise_bwd(_f_rms, [sv['x']], [p['norm_w'][None]], [dh], [F32], "rms_bwd", addend=dx_out)
    g['norm_w'] = dnw[0]
    return dx, g


INPUT_NAMES = (['x'] + WEIGHT_NAMES + ['loss_target'] + ['m_' + n for n in WEIGHT_NAMES]
               + ['v_' + n for n in WEIGHT_NAMES])


def _step(d):
    xi, yi, ci = _here()
    me = 2 * xi + yi
    depth = d['norm_w'].shape[0]
    big, ssm = list(BIG), list(SHARDED_SMALL)
    nsh = 4
    full, gathered = {}, {}
    halves = [d[n].astype(BF16).reshape(2, -1, d[n].shape[-1]) for n in big]
    for n, hv, got in zip(big, halves, _gather_chips_split(halves, "gather_weights")):
        gathered[n] = lax.dynamic_update_slice(got, hv[None], (me, 0, 0, 0)).reshape((nsh,) + d[n].shape)
    cg = _gather_chips(_flat_pack([d[n] for n in ssm], F32, 8), "gather_conv_weights")
    parts = [_flat_unpack(cg[j], [d[n].shape for n in ssm]) for j in range(nsh)]
    for i, n in enumerate(ssm):
        full[n] = jnp.concatenate([parts[j][i] for j in range(nsh)], axis=SHARDED_SMALL[n])
    layer_names = [n for n in WEIGHT_NAMES if n != 'final_norm_w']

    def layer_params(l):
        p = {n: (full[n][l] if n in full else d[n][l]) for n in layer_names if n not in BIG}
        for n in big:
            p[n] = jnp.concatenate([gathered[n][j, l] for j in range(nsh)], axis=BIG[n] - 1)
        return p

    x = d['x'][0]
    saved = []
    for l in range(depth):
        x, sv = _layer_fwd(x, layer_params(l))
        saved.append(sv)
    loss11, dx, dfw = _loss_and_grad(x, d['loss_target'][0], d['final_norm_w'][None])
    loss = lax.psum(loss11[0, 0], ("x", "y", "c"))
    grads = [None] * depth
    for l in reversed(range(depth)):
        dx, grads[l] = _layer_bwd(dx, layer_params(l), saved[l])
    gfull = {n: jnp.stack([grads[l][n] for l in range(depth)]) for n in layer_names if n not in BIG}
    gfull['final_norm_w'] = dfw[0]
    def shard(a, axis, j):
        wd = a.shape[axis] // nsh
        return lax.slice_in_dim(a, j * wd, (j + 1) * wd, axis=axis)

    c_idx = jnp.reshape(ci, (1,)).astype(jnp.int32)
    me_idx = jnp.reshape(me, (1,)).astype(jnp.int32)
    g4 = [jnp.stack([jnp.stack([shard(grads[l][n], BIG[n] - 1, j) for l in range(depth)]) for j in range(nsh)])
          .reshape(nsh, 2, -1, d[n].shape[-1]) for n in big]
    pairs = [_add_my_half(g, r, c_idx, "add_my_half_" + n) for n, g, r in zip(big, g4, _swap_sibling_half(g4))]
    got = _scatter_chips([pb for _, pb in pairs])
    mine = [_sum_chips(pf, gt, me_idx, "sum_chips_" + n) for n, (pf, _), gt in zip(big, pairs, got)]
    theirs = _share_sibling(mine)
    out = {}
    for n, mn, th in zip(big, mine, theirs):
        both = jnp.where(ci == 0, jnp.stack([mn, th]), jnp.stack([th, mn]))
        w2, m2, v2 = (d[pre + n].reshape(both.shape[0] * both.shape[1], both.shape[2]) for pre in ('', 'm_', 'v_'))
        g2 = both.reshape(w2.shape)
        dl, nm, nv = _adamw(w2, g2, m2, v2, "adamw_" + n)
        for key, arr in (('grad_', g2), ('delta_', dl), ('new_m_', nm), ('new_v_', nv)):
            out[key + n] = arr.reshape(d[n].shape)
    small = [n for n in WEIGHT_NAMES if n not in BIG]
    sshapes = [gfull[n].shape for n in small]
    gsm = _sum_parts(_gather_all(_flat_pack([gfull[n] for n in small], F32, 8), "gather_small_grads"), "sum_devices")
    gs = dict(zip(small, _flat_unpack(gsm, sshapes)))
    for n in ssm:
        wd = d[n].shape[SHARDED_SMALL[n]]
        gs[n] = lax.dynamic_slice_in_dim(gs[n], me * wd, wd, axis=SHARDED_SMALL[n])
    lshapes = [d[n].shape for n in small]
    wps, gps, mps, vps = (_flat_pack(arrs, F32, 16) for arrs in (
        [d[n] for n in small], [gs[n] for n in small], [d['m_' + n] for n in small], [d['v_' + n] for n in small]))
    dl, nm, nv = _adamw(wps, gps, mps, vps, "adamw_small")
    for key, arr in (('grad_', gps), ('delta_', dl), ('new_m_', nm), ('new_v_', nv)):
        for n, a in zip(small, _flat_unpack(arr, lshapes)):
            out[key + n] = a
    res = [loss, dx[None]]
    for key in ('grad_', 'delta_', 'new_m_', 'new_v_'):
        res += [out[key + n] for n in WEIGHT_NAMES]
    return tuple(res)


def kernel(x, norm_w, w_in, gdn_conv_w, gdn_a_log, gdn_dt_bias, gdn_norm_w, s5_lam_re, s5_lam_im, s5_log_step, s5_b_re, s5_b_im, s5_c_re, s5_c_im, s5_d, s5_glu_w, s5_glu_b, m2_conv_w, m2_conv_b, m2_a_log, m2_dt_bias, m2_d, m2_norm_w, proj_a, proj_b, proj_c, w_out, final_norm_w, loss_target, m_norm_w, m_w_in, m_gdn_conv_w, m_gdn_a_log, m_gdn_dt_bias, m_gdn_norm_w, m_s5_lam_re, m_s5_lam_im, m_s5_log_step, m_s5_b_re, m_s5_b_im, m_s5_c_re, m_s5_c_im, m_s5_d, m_s5_glu_w, m_s5_glu_b, m_m2_conv_w, m_m2_conv_b, m_m2_a_log, m_m2_dt_bias, m_m2_d, m_m2_norm_w, m_proj_a, m_proj_b, m_proj_c, m_w_out, m_final_norm_w, v_norm_w, v_w_in, v_gdn_conv_w, v_gdn_a_log, v_gdn_dt_bias, v_gdn_norm_w, v_s5_lam_re, v_s5_lam_im, v_s5_log_step, v_s5_b_re, v_s5_b_im, v_s5_c_re, v_s5_c_im, v_s5_d, v_s5_glu_w, v_s5_glu_b, v_m2_conv_w, v_m2_conv_b, v_m2_a_log, v_m2_dt_bias, v_m2_d, v_m2_norm_w, v_proj_a, v_proj_b, v_proj_c, v_w_out, v_final_norm_w):
    args = (x, norm_w, w_in, gdn_conv_w, gdn_a_log, gdn_dt_bias, gdn_norm_w, s5_lam_re, s5_lam_im, s5_log_step, s5_b_re, s5_b_im, s5_c_re, s5_c_im, s5_d, s5_glu_w, s5_glu_b, m2_conv_w, m2_conv_b, m2_a_log, m2_dt_bias, m2_d, m2_norm_w, proj_a, proj_b, proj_c, w_out, final_norm_w, loss_target, m_norm_w, m_w_in, m_gdn_conv_w, m_gdn_a_log, m_gdn_dt_bias, m_gdn_norm_w, m_s5_lam_re, m_s5_lam_im, m_s5_log_step, m_s5_b_re, m_s5_b_im, m_s5_c_re, m_s5_c_im, m_s5_d, m_s5_glu_w, m_s5_glu_b, m_m2_conv_w, m_m2_conv_b, m_m2_a_log, m_m2_dt_bias, m_m2_d, m_m2_norm_w, m_proj_a, m_proj_b, m_proj_c, m_w_out, m_final_norm_w, v_norm_w, v_w_in, v_gdn_conv_w, v_gdn_a_log, v_gdn_dt_bias, v_gdn_norm_w, v_s5_lam_re, v_s5_lam_im, v_s5_log_step, v_s5_b_re, v_s5_b_im, v_s5_c_re, v_s5_c_im, v_s5_d, v_s5_glu_w, v_s5_glu_b, v_m2_conv_w, v_m2_conv_b, v_m2_a_log, v_m2_dt_bias, v_m2_d, v_m2_norm_w, v_proj_a, v_proj_b, v_proj_c, v_w_out, v_final_norm_w)
    return _step(dict(zip(INPUT_NAMES, args)))
```

```python
import functools
import math
from typing import NamedTuple

import jax
import jax.numpy as jnp
import numpy as np
from jax import lax
from jax.experimental import pallas as pl
from jax.experimental.pallas import tpu as pltpu
from jax.experimental.pallas import tpu_sc as plsc

F32 = jnp.float32
BF16 = jnp.bfloat16
HI = lax.Precision.HIGH
MESH = pl.DeviceIdType.MESH

CHUNK = 64
CONV_K = 4
NORM_EPS = 1e-6
GDN_HEAD_DIM = 128
M2_HEAD_DIM = 64
M2_STATE = 128
M2_GROUPS = 4
S5_GROUP_TILE = 8
ADAM_LR = 0.001
ADAM_B1 = 0.9
ADAM_B2 = 0.999
ADAM_EPS = 1e-08
ADAM_WD = 0.01
ADAM_STEP = 10
LANES = 128
SUBLANES = 8
VMEM_LIMIT_BYTES = 56 * 1024 * 1024

WEIGHT_NAMES = ['norm_w', 'w_in', 'gdn_conv_w', 'gdn_a_log', 'gdn_dt_bias', 'gdn_norm_w', 's5_lam_re', 's5_lam_im',
                's5_log_step', 's5_b_re', 's5_b_im', 's5_c_re', 's5_c_im', 's5_d', 's5_glu_w', 's5_glu_b',
                'm2_conv_w', 'm2_conv_b', 'm2_a_log', 'm2_dt_bias', 'm2_d', 'm2_norm_w', 'proj_a', 'proj_b',
                'proj_c', 'w_out', 'final_norm_w']
BIG = {'w_in': 2, 'proj_a': 2, 'proj_b': 2, 'proj_c': 2, 'w_out': 1, 's5_glu_w': 1}
SHARDED_SMALL = {'gdn_conv_w': 2, 'm2_conv_w': 2}


def _call(body, *, name, out_shape, grid=None, in_specs=None, out_specs=None, scratch_shapes=(), semantics=None,
          num_scalar_prefetch=None):
    params = dict(vmem_limit_bytes=VMEM_LIMIT_BYTES)
    if semantics is not None:
        params['dimension_semantics'] = semantics
    kw = dict(name=name, out_shape=out_shape, compiler_params=pltpu.CompilerParams(**params))
    if num_scalar_prefetch is not None:
        kw['grid_spec'] = pltpu.PrefetchScalarGridSpec(num_scalar_prefetch=num_scalar_prefetch, grid=grid,
                                                       in_specs=in_specs, out_specs=out_specs,
                                                       scratch_shapes=scratch_shapes)
    else:
        if grid is not None:
            kw['grid'] = grid
        if in_specs is not None:
            kw['in_specs'] = in_specs
        if out_specs is not None:
            kw['out_specs'] = out_specs
        if scratch_shapes:
            kw['scratch_shapes'] = scratch_shapes
    return pl.pallas_call(body, **kw)


def _tile(n, target, unit):
    if n <= target:
        return n
    t = (target // unit) * unit
    while t >= unit:
        if n % t == 0:
            return t
        t -= unit
    raise ValueError(f"no tile for {n} (unit {unit}, target {target})")


def _sds(shape, dtype):
    return jax.ShapeDtypeStruct(tuple(shape), dtype)


def _sigmoid(x):
    return jax.nn.sigmoid(x)


def _silu(x):
    return x * jax.nn.sigmoid(x)


def _softplus(x):
    return jnp.maximum(x, 0.0) + jnp.log(1.0 + jnp.exp(-jnp.abs(x)))


def _gelu_tanh(x):
    return 0.5 * x * (1.0 + jnp.tanh(math.sqrt(2.0 / math.pi) * (x + 0.044715 * (x * x * x))))


def _dot(a, b, dims, prec=None):
    return lax.dot_general(a, b, (dims, ((), ())), precision=prec, preferred_element_type=F32)


def _nn(a, b, prec=None):
    return _dot(a, b, ((1,), (0,)), prec)


def _nt(a, b, prec=None):
    return _dot(a, b, ((1,), (1,)), prec)


def _tn(a, b, prec=None):
    return _dot(a, b, ((0,), (0,)), prec)


IN_PROJ_TILE = 1664
MATMUL_TILES = {'nn': (1024, 1024, 2048), 'nt': (1024, 1024, 2048), 'tn': (1024, 1024, 2048)}


def _matmul(a, b, mode, out_dtype, name, tm=None, tn=None, tk=None, add=None):
    tm, tn, tk = (t if t is not None else dflt for t, dflt in zip((tm, tn, tk), MATMUL_TILES[mode]))
    if mode == 'nn':
        (m, k), (k2, n) = a.shape, b.shape
    elif mode == 'nt':
        (m, k), (n, k2) = a.shape, b.shape
    else:
        (k, m), (k2, n) = a.shape, b.shape
    assert k == k2, (a.shape, b.shape, mode)
    tm = _tile(m, tm, LANES if mode == 'tn' else SUBLANES)
    tn = _tile(n, tn, LANES)
    tk = _tile(k, tk, LANES if mode != 'tn' else SUBLANES * 2)
    nk = k // tk
    dims = {'nn': ((1,), (0,)), 'nt': ((1,), (1,)), 'tn': ((0,), (0,))}[mode]

    def body(*refs):
        a_ref, b_ref = refs[:2]
        o_ref, acc_ref = refs[-2:]
        kk = pl.program_id(2)

        @pl.when(kk == 0)
        def _():
            acc_ref[...] = jnp.zeros_like(acc_ref)

        acc_ref[...] += _dot(a_ref[...].astype(BF16), b_ref[...].astype(BF16), dims)

        @pl.when(kk == nk - 1)
        def _():
            res = acc_ref[...]
            if add is not None:
                res = res + refs[2][...].astype(F32)
            o_ref[...] = res.astype(o_ref.dtype)

    a_spec = pl.BlockSpec((tk, tm), lambda i, j, kk: (kk, i)) if mode == 'tn' else pl.BlockSpec((tm, tk), lambda i, j, kk: (i, kk))
    b_spec = pl.BlockSpec((tn, tk), lambda i, j, kk: (j, kk)) if mode == 'nt' else pl.BlockSpec((tk, tn), lambda i, j, kk: (kk, j))
    o_spec = pl.BlockSpec((tm, tn), lambda i, j, kk: (i, j))
    ops = (a, b) if add is None else (a, b, add)
    return _call(body, name=name, out_shape=_sds((m, n), out_dtype), grid=(m // tm, n // tn, nk),
                 in_specs=[a_spec, b_spec] + ([] if add is None else [o_spec]), out_specs=o_spec,
                 scratch_shapes=[pltpu.VMEM((tm, tn), F32)], semantics=("parallel", "parallel", "arbitrary"))(*ops)


class Win(NamedTuple):
    arr: jax.Array
    off: int
    width: int

    @property
    def shape(self):
        return (self.arr.shape[0], self.width)


def _win(x):
    return x if isinstance(x, Win) else Win(x, 0, x.shape[1])


def _col_tile(wins, target):
    ct = (min(target, min(w.width for w in wins)) // LANES) * LANES
    while ct > LANES and any(w.width % ct or w.off % ct for w in wins):
        ct -= LANES
    assert all(w.width % ct == 0 and w.off % ct == 0 for w in wins), [(w.off, w.width) for w in wins]
    return ct


def _wspec(rows, ct, w, row_first=True):
    base = w.off // ct
    if row_first:
        return pl.BlockSpec((rows, ct), lambda i, j: (i, base + j))
    return pl.BlockSpec((rows, ct), lambda j, i: (i, base + j))


def _bd_expand(a, b, name, tm=1024):
    a = _win(a)
    t = a.shape[0]
    jn, ka, nb = b.shape
    r = nb // LANES
    tm = _tile(t, tm, SUBLANES)
    assert a.shape[1] == jn * ka and a.off % ka == 0
    abase = a.off // ka

    def body(a_ref, b_ref, o_ref):
        o_ref[...] = _nn(a_ref[...].astype(BF16), b_ref[...].astype(BF16)).reshape(tm, r, LANES)

    return _call(body, name=name, out_shape=_sds((t, jn * r, LANES), F32), grid=(t // tm, jn),
                 in_specs=[pl.BlockSpec((tm, ka), lambda i, j: (i, abase + j)),
                           pl.BlockSpec((None, ka, nb), lambda i, j: (j, 0, 0))],
                 out_specs=pl.BlockSpec((tm, r, LANES), lambda i, j: (i, j, 0)),
                 semantics=("parallel", "parallel"))(a.arr, b)


def _bd_reduce(a3, b, out_dtype, name, tm=1024, add=None):
    t = a3.shape[0]
    jn, ka, nb = b.shape
    r = nb // LANES
    tm = _tile(t, tm, SUBLANES)

    def body(*refs):
        a_ref, b_ref, o_ref = refs[0], refs[1], refs[-1]
        res = _nt(a_ref[...].reshape(tm, nb).astype(BF16), b_ref[...].astype(BF16))
        if add is not None:
            res = res + refs[2][...].astype(F32)
        o_ref[...] = res.astype(o_ref.dtype)

    o_spec = pl.BlockSpec((tm, ka), lambda i, j: (i, j))
    ops = (a3, b) if add is None else (a3, b, add)
    return _call(body, name=name, out_shape=_sds((t, jn * ka), out_dtype), grid=(t // tm, jn),
                 in_specs=[pl.BlockSpec((tm, r, LANES), lambda i, j: (i, j, 0)),
                           pl.BlockSpec((None, ka, nb), lambda i, j: (j, 0, 0))] + ([] if add is None else [o_spec]),
                 out_specs=o_spec, semantics=("parallel", "parallel"))(*ops)


def _bd_outer(a, b3, ka, name, tk=1024):
    a = _win(a)
    t = a.shape[0]
    jn = a.shape[1] // ka
    r = b3.shape[1] // jn
    nb = r * LANES
    assert a.off % ka == 0
    abase = a.off // ka
    tk = _tile(t, tk, SUBLANES * 2)

    def body(a_ref, b_ref, o_ref):
        @pl.when(pl.program_id(1) == 0)
        def _():
            o_ref[...] = jnp.zeros_like(o_ref)

        o_ref[...] += _tn(a_ref[...].astype(BF16), b_ref[...].reshape(tk, nb).astype(BF16))

    return _call(body, name=name, out_shape=_sds((jn, ka, nb), F32), grid=(jn, t // tk),
                 in_specs=[pl.BlockSpec((tk, ka), lambda j, kk: (kk, abase + j)),
                           pl.BlockSpec((tk, r, LANES), lambda j, kk: (kk, j, 0))],
                 out_specs=pl.BlockSpec((None, ka, nb), lambda j, kk: (j, 0, 0)),
                 semantics=("parallel", "arbitrary"))(a.arr, b3)


def _concat_cols(parts, name, tb=256):
    t = parts[0].shape[0]
    tb = _tile(t, tb, SUBLANES * 2)
    widths = [p.shape[1] for p in parts]
    assert all(w % LANES == 0 for w in widths)

    def body(*refs):
        o_ref, off = refs[-1], 0
        for r, w in zip(refs[:-1], widths):
            o_ref[:, off:off + w] = r[...]
            off += w

    return _call(body, name=name, out_shape=_sds((t, sum(widths)), parts[0].dtype), grid=(t // tb,),
                 in_specs=[pl.BlockSpec((tb, w), lambda i: (i, 0)) for w in widths],
                 out_specs=pl.BlockSpec((tb, sum(widths)), lambda i: (i, 0)), semantics=("parallel",))(*parts)


def _rowwise_tiles(rows, tb, col_tile):
    rows = [_win(r) for r in rows]
    t = rows[0].shape[0]
    tb = _tile(t, tb, SUBLANES * 2)
    if col_tile is None:
        assert all(r.off % r.width == 0 for r in rows)
        return rows, tb, None, 1
    ct = _col_tile(rows, col_tile)
    tb = _tile(t, max(tb, BLOCK_BYTES // (4 * ct)), SUBLANES * 2)
    return rows, tb, ct, rows[0].width // ct


def _rowwise(fn, rows, params, out_widths, out_dtypes, name, tb=256, col_tile=None):
    rows, tb, ct, ncol = _rowwise_tiles(rows, tb, col_tile)
    t = rows[0].shape[0]
    nr, npar = len(rows), len(params)

    def body(*refs):
        ins = [r[...].astype(F32) for r in refs[:nr + npar]]
        outs = fn(*ins)
        for o_ref, o in zip(refs[nr + npar:], outs):
            o_ref[...] = o.astype(o_ref.dtype)

    in_specs = [_wspec(tb, ct or r.width, r) for r in rows]
    in_specs += [pl.BlockSpec((1, ct or p.shape[1]), lambda i, j: (0, j)) for p in params]
    out_shape = [_sds((t, w), d) for w, d in zip(out_widths, out_dtypes)]
    out_specs = [pl.BlockSpec((tb, ct or w), lambda i, j: (i, j)) for w in out_widths]
    return _call(body, name=name, out_shape=out_shape, grid=(t // tb, ncol), in_specs=in_specs, out_specs=out_specs,
                 semantics=("parallel", "parallel"))(*[r.arr for r in rows], *params)


def _rowwise_bwd(fn, rows, params, cts, row_grad_dtypes, name, tb=256, addend=None, col_tile=None):
    rows, tb, ct, ncol = _rowwise_tiles(rows, tb, col_tile)
    t = rows[0].shape[0]
    nr, npar, nc = len(rows), len(params), len(cts)
    keep = [i for i, d in enumerate(row_grad_dtypes) if d is not None]
    nadd = 0 if addend is None else 1

    def body(*refs):
        ins = [r[...].astype(F32) for r in refs[:nr + npar]]
        ct = [r[...].astype(F32) for r in refs[nr + npar:nr + npar + nc]]
        _, vjp = jax.vjp(fn, *ins)
        grads = vjp(tuple(ct))
        out_refs = refs[nr + npar + nc + nadd:]
        for o_ref, i in zip(out_refs[:len(keep)], keep):
            g = grads[i]
            if nadd and i == 0:
                g = g + refs[nr + npar + nc][...].astype(F32)
            o_ref[...] = g.astype(o_ref.dtype)

        @pl.when(pl.program_id(1) == 0)
        def _():
            for o_ref in out_refs[len(keep):]:
                o_ref[...] = jnp.zeros_like(o_ref)

        for o_ref, g in zip(out_refs[len(keep):], grads[nr:]):
            o_ref[...] += g

    def plain(w):
        return pl.BlockSpec((tb, ct or w), lambda j, i: (i, j))

    in_specs = [_wspec(tb, ct or r.width, r, row_first=False) for r in rows]
    in_specs += [pl.BlockSpec((1, ct or p.shape[1]), lambda j, i: (0, j)) for p in params]
    in_specs += [plain(c.shape[1]) for c in cts]
    extra = []
    if nadd:
        in_specs += [plain(addend.shape[1])]
        extra = [addend]
    out_shape = [_sds(rows[i].shape, row_grad_dtypes[i]) for i in keep] + [_sds(p.shape, F32) for p in params]
    out_specs = [plain(rows[i].width) for i in keep]
    out_specs += [pl.BlockSpec((1, ct or p.shape[1]), lambda j, i: (0, j)) for p in params]
    return _call(body, name=name, out_shape=out_shape, grid=(ncol, t // tb), in_specs=in_specs, out_specs=out_specs,
                 semantics=("parallel", "arbitrary"))(*[r.arr for r in rows], *params, *cts, *extra)


def _chunk_masks(c):
    row = lax.broadcasted_iota(jnp.int32, (c, c), 0)
    col = lax.broadcasted_iota(jnp.int32, (c, c), 1)
    causal = row >= col
    strict = row > col
    return causal, strict, causal.astype(F32), (row > col).astype(F32), (row == col).astype(F32)


def _lane_pick(blk, idx):
    lane = lax.broadcasted_iota(jnp.int32, blk.shape, 1)
    return jnp.sum(jnp.where(lane == idx, blk, 0.0), axis=1, keepdims=True)


def _bdot(a, b, ca, cb, prec=None):
    return lax.dot_general(a, b, (((ca,), (cb,)), ((0,), (0,))), precision=prec, preferred_element_type=F32)


def _bnn(a, b, prec=None):
    return _bdot(a, b, 2, 1, prec)


def _bnt(a, b, prec=None):
    return _bdot(a, b, 2, 2, prec)


def _btn(a, b, prec=None):
    return _bdot(a, b, 1, 1, prec)


def _unit_lower_inverse(a_mat, eye):
    x = -a_mat
    t_inv = eye + x
    p = x
    for _ in range(int(math.log2(a_mat.shape[-1])) - 1):
        p = _bnn(p, p, HI)
        t_inv = t_inv + _bnn(t_inv, p, HI)
    return t_inv


@jax.custom_vjp
def _saved_inverse(a_mat, t_saved):
    return t_saved


def _saved_inverse_fwd(a_mat, t_saved):
    return t_saved, t_saved


def _saved_inverse_bwd(t_inv, ct):
    return -_bnt(_btn(t_inv, ct, HI), t_inv, HI), jnp.zeros_like(t_inv)


_saved_inverse.defvjp(_saved_inverse_fwd, _saved_inverse_bwd)


def _gdn_heads(q, k, v, z, braw, araw, alog, dtb, nw, s_in, t_saved=None):
    b, c, d = q.shape
    causal, strict, lower, upper_t, eye = _chunk_masks(c)
    lower_b = jnp.broadcast_to(lower[None], (b, c, c))
    qn = q * lax.rsqrt(jnp.sum(q * q, axis=-1, keepdims=True) + NORM_EPS) * (d ** -0.5)
    kn = k * lax.rsqrt(jnp.sum(k * k, axis=-1, keepdims=True) + NORM_EPS)
    beta = _sigmoid(braw)
    g = -jnp.exp(alog) * _softplus(araw + dtb)
    dlog = _bnn(lower_b, g * upper_t[None], HI)
    dm = jnp.where(causal[None], jnp.exp(dlog), 0.0)
    g_lanes = jnp.broadcast_to(g, (b, c, d))
    gc = _bnn(lower_b, g_lanes, HI)
    gl = jnp.sum(g_lanes, axis=1, keepdims=True)
    eg = jnp.exp(gc)
    kb = kn * beta
    a_mat = jnp.where(strict[None], _bnt(kb, kn) * dm, 0.0)
    t_inv = _unit_lower_inverse(a_mat, eye[None]) if t_saved is None else _saved_inverse(a_mat, t_saved)
    r = beta * (v - eg * _bnn(kn, s_in))
    v_new = _bnn(t_inv, r)
    qk = _bnt(qn, kn) * dm
    out = eg * _bnn(qn, s_in) + _bnn(qk, v_new)
    k_tail = kn * jnp.exp(gl - gc)
    s_out = s_in * jnp.exp(gl) + _btn(k_tail, v_new)
    y = out * lax.rsqrt(jnp.mean(out * out, axis=-1, keepdims=True) + NORM_EPS) * nw[None] * _silu(z)
    if t_saved is None:
        return y, s_out, t_inv
    return y, s_out


def _gdn_stack(refs, hb, d, h, first_head):
    q_ref, k_ref, v_ref, z_ref, b_ref, a_ref, alog_ref, dtb_ref = refs
    sls = [slice(i * d, (i + 1) * d) for i in range(hb)]
    heads = [first_head + i for i in range(hb)]
    wide = [jnp.stack([r[:, sl] for sl in sls]) for r in (q_ref, k_ref, v_ref, z_ref)]
    cols = [jnp.stack([_lane_pick(r[...], hd) for hd in heads]) for r in (b_ref, a_ref, alog_ref, dtb_ref)]
    return wide + cols


GDN_HEADS_PER_STEP = 8


def _gdn_chunks_fwd(sqkv, z, braw, araw, alog, dtb, nw, hb=GDN_HEADS_PER_STEP):
    t, w3 = sqkv.shape
    w = w3 // 3
    d = GDN_HEAD_DIM
    h = w // d
    hb = min(hb, h)
    hg = h // hb
    c = CHUNK
    nc = t // c

    def body(q_ref, k_ref, v_ref, z_ref, b_ref, a_ref, alog_ref, dtb_ref, nw_ref, y_ref, ssave_ref, tsave_ref,
             s_ref):
        @pl.when(pl.program_id(1) == 0)
        def _():
            s_ref[...] = jnp.zeros_like(s_ref)

        s_in = s_ref[...]
        ssave_ref[...] = s_in
        args = _gdn_stack((q_ref, k_ref, v_ref, z_ref, b_ref, a_ref, alog_ref, dtb_ref), hb, d, h,
                          pl.program_id(0) * hb)
        y, s_out, t_inv = _gdn_heads(*args, nw_ref[...], s_in)
        for i in range(hb):
            y_ref[:, i * d:(i + 1) * d] = y[i]
        s_ref[...] = s_out
        tsave_ref[...] = t_inv

    blk = (c, hb * d)
    z = _win(z)
    zb = z.off // (hb * d)
    assert z.off % (hb * d) == 0
    in_specs = [pl.BlockSpec(blk, lambda g, n: (n, g)), pl.BlockSpec(blk, lambda g, n: (n, hg + g)),
                pl.BlockSpec(blk, lambda g, n: (n, 2 * hg + g)), pl.BlockSpec(blk, lambda g, n: (n, zb + g)),
                pl.BlockSpec((c, h), lambda g, n: (n, 0)), pl.BlockSpec((c, h), lambda g, n: (n, 0)),
                pl.BlockSpec((1, h), lambda g, n: (0, 0)), pl.BlockSpec((1, h), lambda g, n: (0, 0)),
                pl.BlockSpec((1, d), lambda g, n: (0, 0))]
    out_shape = [_sds((t, w), F32), _sds((hg, nc, hb, d, d), F32), _sds((hg, nc, hb, c, c), F32)]
    out_specs = [pl.BlockSpec(blk, lambda g, n: (n, g)),
                 pl.BlockSpec((None, None, hb, d, d), lambda g, n: (g, n, 0, 0, 0)),
                 pl.BlockSpec((None, None, hb, c, c), lambda g, n: (g, n, 0, 0, 0))]
    return _call(body, name="gdn_chunks_fwd", out_shape=out_shape, grid=(hg, nc), in_specs=in_specs,
                 out_specs=out_specs, scratch_shapes=[pltpu.VMEM((hb, d, d), F32)],
                 semantics=("parallel", "arbitrary"))(sqkv, sqkv, sqkv, z.arr, braw, araw, alog, dtb, nw)


def _gdn_chunks_bwd(sqkv, z, braw, araw, alog, dtb, nw, ssave, tsave, dy, hb=GDN_HEADS_PER_STEP):
    t, w3 = sqkv.shape
    w = w3 // 3
    d = GDN_HEAD_DIM
    h = w // d
    hb = min(hb, h)
    hg = h // hb
    c = CHUNK
    nc = t // c

    def body(q_ref, k_ref, v_ref, z_ref, b_ref, a_ref, alog_ref, dtb_ref, nw_ref, ssave_ref, tsave_ref, dy_ref,
             dsq_ref, dz_ref, db_ref, da_ref, dalog_ref, ddtb_ref, dnw_ref, ds_ref):
        first = jnp.logical_and(pl.program_id(0) == 0, pl.program_id(1) == 0)

        @pl.when(pl.program_id(1) == 0)
        def _():
            ds_ref[...] = jnp.zeros_like(ds_ref)

        @pl.when(first)
        def _():
            dalog_ref[...] = jnp.zeros_like(dalog_ref)
            ddtb_ref[...] = jnp.zeros_like(ddtb_ref)
            dnw_ref[...] = jnp.zeros_like(dnw_ref)

        lane_h = lax.broadcasted_iota(jnp.int32, (1, h), 1)
        db_acc = jnp.zeros((c, h), F32)
        da_acc = jnp.zeros((c, h), F32)
        args = _gdn_stack((q_ref, k_ref, v_ref, z_ref, b_ref, a_ref, alog_ref, dtb_ref), hb, d, h,
                          pl.program_id(0) * hb)
        t_saved = tsave_ref[...]
        _, vjp = jax.vjp(lambda *a: _gdn_heads(*a, t_saved=t_saved), *args, nw_ref[...], ssave_ref[...])
        dyb = jnp.stack([dy_ref[:, i * d:(i + 1) * d] for i in range(hb)])
        dq, dk, dv, dz, db, da, dalog, ddtb, dnw, ds_in = vjp((dyb, ds_ref[...]))
        for i in range(hb):
            sl = slice(i * d, (i + 1) * d)
            dsq_ref[:, i * d:(i + 1) * d] = dq[i]
            dsq_ref[:, w + i * d:w + (i + 1) * d] = dk[i]
            dsq_ref[:, 2 * w + i * d:2 * w + (i + 1) * d] = dv[i]
            dz_ref[:, sl] = dz[i].astype(dz_ref.dtype)
            onehot = (lane_h == pl.program_id(0) * hb + i).astype(F32)
            db_acc = db_acc + db[i] * onehot
            da_acc = da_acc + da[i] * onehot
            dalog_ref[...] += dalog[i] * onehot
            ddtb_ref[...] += ddtb[i] * onehot
        dnw_ref[...] += dnw
        ds_ref[...] = ds_in
        db_ref[...] = db_acc
        da_ref[...] = da_acc

    blk = (c, hb * d)
    rev = lambda n: nc - 1 - n
    z = _win(z)
    zb = z.off // (hb * d)
    assert z.off % (hb * d) == 0
    in_specs = [pl.BlockSpec(blk, lambda g, n: (rev(n), g)), pl.BlockSpec(blk, lambda g, n: (rev(n), hg + g)),
                pl.BlockSpec(blk, lambda g, n: (rev(n), 2 * hg + g)), pl.BlockSpec(blk, lambda g, n: (rev(n), zb + g)),
                pl.BlockSpec((c, h), lambda g, n: (rev(n), 0)), pl.BlockSpec((c, h), lambda g, n: (rev(n), 0)),
                pl.BlockSpec((1, h), lambda g, n: (0, 0)), pl.BlockSpec((1, h), lambda g, n: (0, 0)),
                pl.BlockSpec((1, d), lambda g, n: (0, 0)),
                pl.BlockSpec((None, None, hb, d, d), lambda g, n: (g, rev(n), 0, 0, 0)),
                pl.BlockSpec((None, None, hb, c, c), lambda g, n: (g, rev(n), 0, 0, 0)),
                pl.BlockSpec(blk, lambda g, n: (rev(n), g))]
    assert hg == 1
    out_shape = [_sds((t, w3), F32), _sds((t, w), BF16),
                 _sds((hg, t, h), F32), _sds((hg, t, h), F32), _sds((1, h), F32), _sds((1, h), F32), _sds((1, d), F32)]
    out_specs = [pl.BlockSpec((c, w3), lambda g, n: (rev(n), 0)), pl.BlockSpec(blk, lambda g, n: (rev(n), g))]
    out_specs += [pl.BlockSpec((None, c, h), lambda g, n: (g, rev(n), 0))] * 2
    out_specs += [pl.BlockSpec((1, h), lambda g, n: (0, 0)), pl.BlockSpec((1, h), lambda g, n: (0, 0)),
                  pl.BlockSpec((1, d), lambda g, n: (0, 0))]
    return _call(body, name="gdn_chunks_bwd", out_shape=out_shape, grid=(hg, nc), in_specs=in_specs,
                 out_specs=out_specs, scratch_shapes=[pltpu.VMEM((hb, d, d), F32)],
                 semantics=("arbitrary", "arbitrary"))(sqkv, sqkv, sqkv, z.arr, braw, araw, alog, dtb, nw, ssave, tsave,
                                                       dy)


def _m2_groups(xs, z, bm, cm, dtraws, alogs, dtbs, dsks, nw, st):
    g, c, gw = xs.shape
    rep = len(dtraws)
    causal, _, lower, upper_t, _ = _chunk_masks(c)
    lower_b = jnp.broadcast_to(lower[None], (g, c, c))
    lane_head = lax.broadcasted_iota(jnp.int32, (1, 1, gw), 2) // M2_HEAD_DIM

    def expand(cols):
        res = jnp.broadcast_to(cols[-1], (g, cols[-1].shape[1], gw))
        for i in reversed(range(rep - 1)):
            res = jnp.where(lane_head == i, cols[i], res)
        return res

    dts = [_softplus(dtraws[i] + dtbs[i]) for i in range(rep)]
    adts = [-jnp.exp(alogs[i]) * dts[i] for i in range(rep)]
    dt_l, adt_l, dsk_l = expand(dts), expand(adts), expand(dsks)
    xdt = xs * dt_l
    acum = _bnn(lower_b, adt_l, HI)
    alast = jnp.sum(adt_l, axis=1, keepdims=True)
    scores = _bnt(cm, bm)
    y = jnp.exp(acum) * _bnn(cm, st) + dsk_l * xs
    for i in range(rep):
        seg = jnp.where(causal[None], jnp.exp(_bnn(lower_b, adts[i] * upper_t[None], HI)), 0.0)
        y = y + _bnn(scores * seg, jnp.where(lane_head == i, xdt, 0.0))
    st_out = st * jnp.exp(alast) + _btn(bm, xdt * jnp.exp(alast - acum))
    y2 = y * _silu(z)
    out = y2 * lax.rsqrt(jnp.mean(y2 * y2, axis=-1, keepdims=True) + NORM_EPS) * nw
    return out, st_out


def _m2_dims(sxbc, z):
    t = sxbc.shape[0]
    w2 = z.shape[1]
    g = M2_GROUPS
    n = M2_STATE
    assert sxbc.shape[1] == w2 + 2 * g * n
    gw = w2 // g
    return t, w2, g, n, gw, gw // M2_HEAD_DIM, t // CHUNK


def _m2_args(refs, g, n, gw, rep, st):
    sx_ref, z_ref, dt_ref, alog_ref, dtb_ref, dsk_ref, nw_ref = refs
    w2 = g * gw
    xs = jnp.stack([sx_ref[:, i * gw:(i + 1) * gw] for i in range(g)])
    z = jnp.stack([z_ref[:, i * gw:(i + 1) * gw] for i in range(g)])
    bm = jnp.stack([sx_ref[:, w2 + i * n:w2 + (i + 1) * n] for i in range(g)])
    cm = jnp.stack([sx_ref[:, w2 + (g + i) * n:w2 + (g + i + 1) * n] for i in range(g)])
    nw = jnp.stack([nw_ref[:, i * gw:(i + 1) * gw] for i in range(g)])

    def cols(ref):
        blk = ref[...]
        return [jnp.stack([_lane_pick(blk, gi * rep + i) for gi in range(g)]) for i in range(rep)]

    return xs, z, bm, cm, cols(dt_ref), cols(alog_ref), cols(dtb_ref), cols(dsk_ref), nw, st


def _m2_chunks_fwd(sxbc, z, dtraw, alog, dtb, dsk, nw):
    t, w2, g, n, gw, rep, nc = _m2_dims(sxbc, z)
    hm = dtraw.shape[1]
    c = CHUNK
    wx = sxbc.shape[1]

    def body(sx_ref, z_ref, dt_ref, alog_ref, dtb_ref, dsk_ref, nw_ref, y_ref, ssave_ref, st_ref):
        @pl.when(pl.program_id(0) == 0)
        def _():
            st_ref[...] = jnp.zeros_like(st_ref)

        st = st_ref[...]
        ssave_ref[...] = st
        y, st_out = _m2_groups(*_m2_args((sx_ref, z_ref, dt_ref, alog_ref, dtb_ref, dsk_ref, nw_ref), g, n, gw, rep, st))
        for i in range(g):
            y_ref[:, i * gw:(i + 1) * gw] = y[i]
        st_ref[...] = st_out

    z = _win(z)
    zb = z.off // w2
    assert z.off % w2 == 0
    in_specs = [pl.BlockSpec((c, wx), lambda k: (k, 0)), pl.BlockSpec((c, w2), lambda k: (k, zb)),
                pl.BlockSpec((c, hm), lambda k: (k, 0)),
                pl.BlockSpec((1, hm), lambda k: (0, 0)), pl.BlockSpec((1, hm), lambda k: (0, 0)),
                pl.BlockSpec((1, hm), lambda k: (0, 0)), pl.BlockSpec((1, w2), lambda k: (0, 0))]
    out_shape = [_sds((t, w2), F32), _sds((nc, g, n, gw), F32)]
    out_specs = [pl.BlockSpec((c, w2), lambda k: (k, 0)), pl.BlockSpec((None, g, n, gw), lambda k: (k, 0, 0, 0))]
    return _call(body, name="m2_chunks_fwd", out_shape=out_shape, grid=(nc,), in_specs=in_specs,
                 out_specs=out_specs, scratch_shapes=[pltpu.VMEM((g, n, gw), F32)],
                 semantics=("arbitrary",))(sxbc, z.arr, dtraw, alog, dtb, dsk, nw)


def _m2_chunks_bwd(sxbc, z, dtraw, alog, dtb, dsk, nw, ssave, dy):
    t, w2, g, n, gw, rep, nc = _m2_dims(sxbc, z)
    hm = dtraw.shape[1]
    c = CHUNK

    wx = sxbc.shape[1]

    def body(sx_ref, z_ref, dt_ref, alog_ref, dtb_ref, dsk_ref, nw_ref, ssave_ref, dy_ref,
             dsx_ref, dz_ref, ddt_ref, dalog_ref, ddtb_ref, ddsk_ref, dnw_ref, dst_ref):
        @pl.when(pl.program_id(0) == 0)
        def _():
            dst_ref[...] = jnp.zeros_like(dst_ref)
            dnw_ref[...] = jnp.zeros_like(dnw_ref)
            dalog_ref[...] = jnp.zeros_like(dalog_ref)
            ddtb_ref[...] = jnp.zeros_like(ddtb_ref)
            ddsk_ref[...] = jnp.zeros_like(ddsk_ref)

        args = _m2_args((sx_ref, z_ref, dt_ref, alog_ref, dtb_ref, dsk_ref, nw_ref), g, n, gw, rep, ssave_ref[...])
        _, vjp = jax.vjp(_m2_groups, *args)
        dyb = jnp.stack([dy_ref[:, i * gw:(i + 1) * gw] for i in range(g)])
        dxs, dz, dbm, dcm, ddts, dalogs, ddtbs, ddsks, dnw, dst = vjp((dyb, dst_ref[...]))
        dst_ref[...] = dst
        lane_h = lax.broadcasted_iota(jnp.int32, (1, hm), 1)
        ddt = jnp.zeros((c, hm), F32)
        for gi in range(g):
            dsx_ref[:, gi * gw:(gi + 1) * gw] = dxs[gi]
            dsx_ref[:, w2 + gi * n:w2 + (gi + 1) * n] = dbm[gi]
            dsx_ref[:, w2 + (g + gi) * n:w2 + (g + gi + 1) * n] = dcm[gi]
            dz_ref[:, gi * gw:(gi + 1) * gw] = dz[gi].astype(dz_ref.dtype)
            dnw_ref[:, gi * gw:(gi + 1) * gw] += dnw[gi]
            for i in range(rep):
                onehot = (lane_h == gi * rep + i).astype(F32)
                ddt = ddt + ddts[i][gi] * onehot
                dalog_ref[...] += dalogs[i][gi] * onehot
                ddtb_ref[...] += ddtbs[i][gi] * onehot
                ddsk_ref[...] += ddsks[i][gi] * onehot
        ddt_ref[...] = ddt

    rev = lambda k: nc - 1 - k
    z = _win(z)
    zb = z.off // w2
    assert z.off % w2 == 0
    in_specs = [pl.BlockSpec((c, wx), lambda k: (rev(k), 0)), pl.BlockSpec((c, w2), lambda k: (rev(k), zb)),
                pl.BlockSpec((c, hm), lambda k: (rev(k), 0)),
                pl.BlockSpec((1, hm), lambda k: (0, 0)), pl.BlockSpec((1, hm), lambda k: (0, 0)),
                pl.BlockSpec((1, hm), lambda k: (0, 0)), pl.BlockSpec((1, w2), lambda k: (0, 0)),
                pl.BlockSpec((None, g, n, gw), lambda k: (rev(k), 0, 0, 0)),
                pl.BlockSpec((c, w2), lambda k: (rev(k), 0))]
    out_shape = [_sds((t, wx), F32), _sds((t, w2), BF16), _sds((t, hm), F32), _sds((1, hm), F32), _sds((1, hm), F32),
                 _sds((1, hm), F32), _sds((1, w2), F32)]
    out_specs = [pl.BlockSpec((c, wx), lambda k: (rev(k), 0)), pl.BlockSpec((c, w2), lambda k: (rev(k), 0)),
                 pl.BlockSpec((c, hm), lambda k: (rev(k), 0)),
                 pl.BlockSpec((1, hm), lambda k: (0, 0)), pl.BlockSpec((1, hm), lambda k: (0, 0)),
                 pl.BlockSpec((1, hm), lambda k: (0, 0)), pl.BlockSpec((1, w2), lambda k: (0, 0))]
    return _call(body, name="m2_chunks_bwd", out_shape=out_shape, grid=(nc,), in_specs=in_specs,
                 out_specs=out_specs, scratch_shapes=[pltpu.VMEM((g, n, gw), F32)],
                 semantics=("arbitrary",))(sxbc, z.arr, dtraw, alog, dtb, dsk, nw, ssave, dy)


def _s5_scan(bu3, aa3, bb3, reverse, name, tb=256):
    t, rtot, _ = bu3.shape
    jn = rtot // SUBLANES
    tb = _tile(t, tb, SUBLANES)
    nb = t // tb

    def body(bu_ref, aa_ref, bb_ref, s_ref, st_ref):
        @pl.when(pl.program_id(0) == 0)
        def _():
            st_ref[...] = jnp.zeros_like(st_ref)

        tiles = [slice(j * SUBLANES, (j + 1) * SUBLANES) for j in range(jn)]
        aa = [aa_ref[tl, :] for tl in tiles]
        bb = [bb_ref[tl, :] for tl in tiles]

        def step(k, carry):
            r = tb - 1 - k if reverse else k
            new = []
            for j, tl in enumerate(tiles):
                sj, wj = carry[2 * j], carry[2 * j + 1]
                xj = bu_ref[r, tl, :]
                nj = aa[j] * sj + bb[j] * wj + xj
                s_ref[r, tl, :] = nj
                new += [nj, aa[j] * wj - bb[j] * sj + pltpu.roll(xj, SUBLANES // 2, 0)]
            return tuple(new)

        init = []
        for tl in tiles:
            init += [st_ref[tl, :], pltpu.roll(st_ref[tl, :], SUBLANES // 2, 0)]
        last = lax.fori_loop(0, tb, step, tuple(init), unroll=8)
        for j, tl in enumerate(tiles):
            st_ref[tl, :] = last[2 * j]

    rb = (lambda i: nb - 1 - i) if reverse else (lambda i: i)
    return _call(body, name=name, out_shape=_sds(bu3.shape, F32), grid=(nb,),
                 in_specs=[pl.BlockSpec((tb, rtot, LANES), lambda i: (rb(i), 0, 0)),
                           pl.BlockSpec((rtot, LANES), lambda i: (0, 0)), pl.BlockSpec((rtot, LANES), lambda i: (0, 0))],
                 out_specs=pl.BlockSpec((tb, rtot, LANES), lambda i: (rb(i), 0, 0)),
                 scratch_shapes=[pltpu.VMEM((rtot, LANES), F32)], semantics=("arbitrary",))(bu3, aa3, bb3)


def _s5_da(ds3, s3, tb=256):
    t, rtot, _ = ds3.shape
    tb = _tile(t, tb, SUBLANES)
    nb = t // tb

    def body(ds_ref, s_ref, halo_ref, p_ref, q_ref):
        i = pl.program_id(0)

        @pl.when(i == 0)
        def _():
            p_ref[...] = jnp.zeros_like(p_ref)
            q_ref[...] = jnp.zeros_like(q_ref)

        for j in range(rtot // SUBLANES):
            tl = slice(j * SUBLANES, (j + 1) * SUBLANES)
            prev = jnp.where(i == 0, 0.0, halo_ref[:, tl, :])
            sh = jnp.concatenate([prev, s_ref[0:tb - 1, tl, :]], axis=0)
            d = ds_ref[:, tl, :]
            p_ref[tl, :] += jnp.sum(d * sh, axis=0)
            q_ref[tl, :] += jnp.sum(d * pltpu.roll(sh, SUBLANES // 2, 1), axis=0)

    blk = pl.BlockSpec((tb, rtot, LANES), lambda i: (i, 0, 0))
    acc = pl.BlockSpec((rtot, LANES), lambda i: (0, 0))
    return _call(body, name="s5_da", out_shape=[_sds((rtot, LANES), F32)] * 2, grid=(nb,),
                 in_specs=[blk, blk, pl.BlockSpec((1, rtot, LANES), lambda i: (jnp.maximum(i * tb - 1, 0), 0, 0))],
                 out_specs=[acc, acc], semantics=("arbitrary",))(ds3, s3, s3)


def _conv_rows(t, tb):
    tb = _tile(t, tb, SUBLANES * 2)
    return tb, t // tb, tb // SUBLANES


def _shift_down(x, above, s):
    n = x.shape[0]
    y = pltpu.roll(x, s, 0)
    row = lax.broadcasted_iota(jnp.int32, above.shape, 0)
    head = jnp.where(row < s, pltpu.roll(above, s, 0), y[:SUBLANES])
    return head if n == SUBLANES else jnp.concatenate([head, y[SUBLANES:]], axis=0)


def _shift_up(x, below, s):
    n = x.shape[0]
    y = pltpu.roll(x, n - s, 0)
    row = lax.broadcasted_iota(jnp.int32, below.shape, 0)
    tail = jnp.where(row >= SUBLANES - s, pltpu.roll(below, SUBLANES - s, 0), y[n - SUBLANES:])
    return tail if n == SUBLANES else jnp.concatenate([y[:n - SUBLANES], tail], axis=0)


def _conv_taps(x, above, w_ref, b_ref):
    xs = [x] + [_shift_down(x, above, s) for s in range(1, CONV_K)]
    c = b_ref[...] + w_ref[CONV_K - 1:CONV_K, :] * x
    for s in range(1, CONV_K):
        c = c + w_ref[CONV_K - 1 - s:CONV_K - s, :] * xs[s]
    return c, xs


CONV_COL_TILE = 1024


def _conv_specs(x, tb):
    x = _win(x)
    t, cwid = x.shape
    ct = _col_tile([x], CONV_COL_TILE)
    tb, nb, hb = _conv_rows(t, max(tb, BLOCK_BYTES // (4 * ct)))
    base = x.off // ct
    blk_x = pl.BlockSpec((tb, ct), lambda j, i: (i, base + j))
    prev_x = pl.BlockSpec((SUBLANES, ct), lambda j, i: (jnp.maximum(i * hb - 1, 0), base + j))
    next_x = pl.BlockSpec((SUBLANES, ct), lambda j, i: (jnp.minimum((i + 1) * hb, nb * hb - 1), base + j))
    blk = pl.BlockSpec((tb, ct), lambda j, i: (i, j))
    nxt = pl.BlockSpec((SUBLANES, ct), lambda j, i: (jnp.minimum((i + 1) * hb, nb * hb - 1), j))
    taps = pl.BlockSpec((CONV_K, ct), lambda j, i: (0, j))
    bias = pl.BlockSpec((1, ct), lambda j, i: (0, j))
    return x, tb, nb, cwid // ct, dict(blk_x=blk_x, prev_x=prev_x, next_x=next_x, blk=blk, nxt=nxt, taps=taps, bias=bias)


def _conv_fwd(x, w, b, name, tb=256):
    x, tb, nb, ncol, sp = _conv_specs(x, tb)
    t, cwid = x.shape

    def body(x_ref, halo_ref, w_ref, b_ref, o_ref):
        above = jnp.where(pl.program_id(1) == 0, 0.0, halo_ref[...])
        o_ref[...] = _silu(_conv_taps(x_ref[...], above, w_ref, b_ref)[0])

    return _call(body, name=name, out_shape=_sds((t, cwid), F32), grid=(ncol, nb),
                 in_specs=[sp['blk_x'], sp['prev_x'], sp['taps'], sp['bias']], out_specs=sp['blk'],
                 semantics=("parallel", "parallel"))(x.arr, x.arr, w, b)


def _dsilu(c):
    sg = _sigmoid(c)
    return sg * (1.0 + c * (1.0 - sg))


def _conv_bwd(x, w, b, ds, name, tb=256):
    x, tb, nb, ncol, sp = _conv_specs(x, tb)
    t, cwid = x.shape

    def body(x_ref, halo_ref, xn_ref, ds_ref, dsn_ref, w_ref, b_ref, dx_ref, dw_ref, db_ref):
        i = pl.program_id(1)

        @pl.when(i == 0)
        def _():
            dw_ref[...] = jnp.zeros_like(dw_ref)
            db_ref[...] = jnp.zeros_like(db_ref)

        x = x_ref[...]
        above = jnp.where(i == 0, 0.0, halo_ref[...])
        c, xs = _conv_taps(x, above, w_ref, b_ref)
        dc = ds_ref[...] * _dsilu(c)
        cn, _ = _conv_taps(xn_ref[...], x[tb - SUBLANES:], w_ref, b_ref)
        dcn = jnp.where(i == nb - 1, 0.0, dsn_ref[...] * _dsilu(cn))
        dx = w_ref[CONV_K - 1:CONV_K, :] * dc
        for s in range(1, CONV_K):
            dx = dx + w_ref[CONV_K - 1 - s:CONV_K - s, :] * _shift_up(dc, dcn, s)
        dx_ref[...] = dx.astype(dx_ref.dtype)
        for s in range(CONV_K):
            dw_ref[CONV_K - 1 - s:CONV_K - s, :] += jnp.sum(dc * xs[s], axis=0, keepdims=True)
        db_ref[...] += jnp.sum(dc, axis=0, keepdims=True)

    return _call(body, name=name, out_shape=[_sds((t, cwid), BF16), _sds((CONV_K, cwid), F32), _sds((1, cwid), F32)],
                 grid=(ncol, nb),
                 in_specs=[sp['blk_x'], sp['prev_x'], sp['next_x'], sp['blk'], sp['nxt'], sp['taps'], sp['bias']],
                 out_specs=[sp['blk'], sp['taps'], sp['bias']],
                 semantics=("parallel", "arbitrary"))(x.arr, x.arr, x.arr, ds, ds, w, b)


def _f_rms(x, w):
    return (x * lax.rsqrt(jnp.mean(x * x, axis=-1, keepdims=True) + NORM_EPS) * w,)


def _f_s5_post1(ymm, u, d_l):
    return (_gelu_tanh(ymm + d_l * u),)


def _f_s5_post2(yg, tt, gate, b):
    return (yg * _sigmoid(tt + b) * _silu(gate),)


def _f_merge(ma, mb, mc, pa, pb, pc):
    return (_sigmoid(ma) * pa + _sigmoid(mb) * pb + _sigmoid(mc) * pc,)


def _loss_and_grad(x, tgt, fw, tb=256):
    t, dm = x.shape
    tb = _tile(t, tb, SUBLANES * 2)

    def f(xb, wb, tb_):
        y = _f_rms(xb, wb)[0]
        e = y - tb_
        return 0.5 * jnp.sum(jnp.mean(e * e, axis=-1, keepdims=True), axis=0, keepdims=True)

    def body(x_ref, t_ref, w_ref, loss_ref, dx_ref, dw_ref):
        @pl.when(pl.program_id(0) == 0)
        def _():
            loss_ref[...] = jnp.zeros_like(loss_ref)
            dw_ref[...] = jnp.zeros_like(dw_ref)

        tgt_b = t_ref[...]
        val, vjp = jax.vjp(lambda a, b: f(a, b, tgt_b), x_ref[...], w_ref[...])
        dxb, dwb = vjp(jnp.ones((1, 1), F32))
        loss_ref[...] += val
        dx_ref[...] = dxb
        dw_ref[...] += dwb

    return _call(body, name="loss_and_grad", out_shape=[_sds((1, 1), F32), _sds((t, dm), F32), _sds((1, dm), F32)],
                 grid=(t // tb,),
                 in_specs=[pl.BlockSpec((tb, dm), lambda i: (i, 0)), pl.BlockSpec((tb, dm), lambda i: (i, 0)),
                           pl.BlockSpec((1, dm), lambda i: (0, 0))],
                 out_specs=[pl.BlockSpec((1, 1), lambda i: (0, 0)), pl.BlockSpec((tb, dm), lambda i: (i, 0)),
                            pl.BlockSpec((1, dm), lambda i: (0, 0))],
                 semantics=("arbitrary",))(x, tgt, fw)


FLAT_W = 1024


def _sum_parts(parts, name, tb=256):
    n, r, wd = parts.shape
    tb = _tile(r, tb, SUBLANES)

    def body(p_ref, o_ref):
        acc = p_ref[0]
        for k in range(1, n):
            acc = acc + p_ref[k]
        o_ref[...] = acc

    return _call(body, name=name, out_shape=_sds((r, wd), F32), grid=(r // tb,),
                 in_specs=[pl.BlockSpec((n, tb, wd), lambda i: (0, i, 0))],
                 out_specs=pl.BlockSpec((tb, wd), lambda i: (i, 0)), semantics=("parallel",))(parts)


BLOCK_BYTES = 1 << 20


def _rows_per_block(r, wd):
    return _tile(r, max(SUBLANES * 2, BLOCK_BYTES // (4 * wd) // (SUBLANES * 2) * (SUBLANES * 2)), SUBLANES * 2)


def _add_my_half(g4, recv, c_idx, name):
    p, _, r, wd = g4.shape
    tb = _rows_per_block(r, wd)

    def body(c_ref, g_ref, r_ref, o_ref, ob_ref):
        s = g_ref[...] + r_ref[...]
        o_ref[...] = s
        ob_ref[...] = s.astype(BF16)

    spec = pl.BlockSpec((None, tb, wd), lambda j, i, c_ref: (j, i, 0))
    return _call(body, name=name, out_shape=[_sds((p, r, wd), F32), _sds((p, r, wd), BF16)], grid=(p, r // tb),
                 in_specs=[pl.BlockSpec((None, None, tb, wd), lambda j, i, c_ref: (j, c_ref[0], i, 0)), spec],
                 out_specs=[spec, spec], semantics=("parallel", "parallel"), num_scalar_prefetch=1)(c_idx, g4, recv)


def _sum_chips(own, got, me_idx, name):
    p, r, wd = own.shape
    tb = _rows_per_block(r, wd)

    def body(me_ref, own_ref, got_ref, o_ref):
        me = me_ref[0]
        acc = None
        for k in range(p):
            part = jnp.where(me == k, own_ref[...], got_ref[k].astype(F32))
            acc = part if acc is None else acc + part
        o_ref[...] = acc

    return _call(body, name=name, out_shape=_sds((r, wd), F32), grid=(r // tb,),
                 in_specs=[pl.BlockSpec((None, tb, wd), lambda i, me_ref: (me_ref[0], i, 0)),
                           pl.BlockSpec((p, tb, wd), lambda i, me_ref: (0, i, 0))],
                 out_specs=pl.BlockSpec((tb, wd), lambda i, me_ref: (i, 0)),
                 semantics=("parallel",), num_scalar_prefetch=1)(me_idx, own, got)


def _adamw(w, g, m, v, name):
    r, wd = w.shape
    tb = _rows_per_block(r, wd)

    def body(w_ref, g_ref, m_ref, v_ref, d_ref, nm_ref, nv_ref):
        gg = g_ref[...]
        nm = ADAM_B1 * m_ref[...] + (1.0 - ADAM_B1) * gg
        nv = ADAM_B2 * v_ref[...] + (1.0 - ADAM_B2) * (gg * gg)
        m_hat = nm / (1.0 - ADAM_B1 ** ADAM_STEP)
        v_hat = nv / (1.0 - ADAM_B2 ** ADAM_STEP)
        d_ref[...] = -ADAM_LR * (m_hat / (jnp.sqrt(v_hat) + ADAM_EPS) + ADAM_WD * w_ref[...])
        nm_ref[...] = nm
        nv_ref[...] = nv

    spec = pl.BlockSpec((tb, wd), lambda i: (i, 0))
    return _call(body, name=name, out_shape=[_sds((r, wd), F32)] * 3, grid=(r // tb,), in_specs=[spec] * 4,
                 out_specs=[spec] * 3, semantics=("parallel",))(w, g, m, v)


def _here():
    return lax.axis_index("x"), lax.axis_index("y"), lax.axis_index("c")


def _comm_call(body, name, out_shape, n_sems, operands):
    anyspec = pl.BlockSpec(memory_space=pl.ANY)
    outs = out_shape if isinstance(out_shape, (list, tuple)) else [out_shape]
    return _call(body, name=name, out_shape=out_shape, in_specs=[anyspec] * len(operands),
                 out_specs=[anyspec] * len(outs) if isinstance(out_shape, (list, tuple)) else anyspec,
                 scratch_shapes=[pltpu.SemaphoreType.DMA((n_sems,)), pltpu.SemaphoreType.DMA((n_sems,)),
                                 pltpu.SemaphoreType.DMA(())])(*operands)


def _gather_chips(x, name):
    def body(x_ref, o_ref, send_sems, recv_sems, local_sem):
        xi, yi, ci = _here()
        chips = [(1 - xi, yi), (xi, 1 - yi), (1 - xi, 1 - yi)]
        mine = pltpu.make_async_copy(x_ref, o_ref.at[2 * xi + yi], local_sem)
        mine.start()

        def copy(k, slot, to):
            return pltpu.make_async_remote_copy(src_ref=x_ref, dst_ref=o_ref.at[slot], send_sem=send_sems.at[k],
                                                recv_sem=recv_sems.at[k], device_id=to, device_id_type=MESH)

        sends = [copy(k, 2 * xi + yi, (px, py, ci)) for k, (px, py) in enumerate(chips)]
        for cp in sends:
            cp.start()
        for k, (px, py) in enumerate(chips):
            copy(k, 2 * px + py, (px, py, ci)).wait_recv()
        for cp in sends:
            cp.wait_send()
        mine.wait()

    return _comm_call(body, name, _sds((4,) + x.shape, x.dtype), 3, (x,))


def _gather_all(x, name):
    def body(x_ref, o_ref, send_sems, recv_sems, local_sem):
        xi, yi, ci = _here()
        me = 4 * xi + 2 * yi + ci
        flips = [(fx, fy, fc) for fx in (0, 1) for fy in (0, 1) for fc in (0, 1)][1:]
        peers = [((1 - xi) if fx else xi, (1 - yi) if fy else yi, (1 - ci) if fc else ci) for fx, fy, fc in flips]
        mine = pltpu.make_async_copy(x_ref, o_ref.at[me], local_sem)
        mine.start()

        def copy(k, slot, to):
            return pltpu.make_async_remote_copy(src_ref=x_ref, dst_ref=o_ref.at[slot], send_sem=send_sems.at[k],
                                                recv_sem=recv_sems.at[k], device_id=to, device_id_type=MESH)

        sends = [copy(k, me, p) for k, p in enumerate(peers)]
        for cp in sends:
            cp.start()
        for k, (px, py, pc) in enumerate(peers):
            copy(k, 4 * px + 2 * py + pc, (px, py, pc)).wait_recv()
        for cp in sends:
            cp.wait_send()
        mine.wait()

    return _comm_call(body, name, _sds((8,) + x.shape, x.dtype), 7, (x,))


def _multi_comm_call(body, name, out_shapes, n_sems, operands):
    anyspec = pl.BlockSpec(memory_space=pl.ANY)
    nin = len(operands)

    def flat_body(*refs):
        body(refs[:nin], refs[nin:nin + len(out_shapes)], refs[-2], refs[-1])

    return _call(flat_body, name=name, out_shape=list(out_shapes), in_specs=[anyspec] * nin,
                 out_specs=[anyspec] * len(out_shapes),
                 scratch_shapes=[pltpu.SemaphoreType.DMA((n_sems,)), pltpu.SemaphoreType.DMA((n_sems,))])(*operands)


def _remote(src, dst, send_sems, recv_sems, k, to):
    return pltpu.make_async_remote_copy(src_ref=src, dst_ref=dst, send_sem=send_sems.at[k], recv_sem=recv_sems.at[k],
                                        device_id=to, device_id_type=MESH)


def _gather_chips_split(xs, name):
    nw = len(xs)

    def body(x_refs, o_refs, send_sems, recv_sems):
        xi, yi, ci = _here()
        me = 2 * xi + yi
        sib = (xi, yi, 1 - ci)
        chips = [(1 - xi, yi), (xi, 1 - yi), (1 - xi, 1 - yi)]
        sends = []
        for i in range(nw):
            for k, (px, py) in enumerate(chips):
                sends.append(_remote(x_refs[i].at[ci], o_refs[i].at[me, ci], send_sems, recv_sems, 3 * i + k,
                                     (px, py, ci)))
        for cp in sends:
            cp.start()
        passed = []
        for i in range(nw):
            for k, (px, py) in enumerate(chips):
                landed = o_refs[i].at[2 * px + py, ci]
                _remote(landed, landed, send_sems, recv_sems, 3 * i + k, (px, py, ci)).wait_recv()
                fwd = _remote(landed, landed, send_sems, recv_sems, 3 * (nw + i) + k, sib)
                fwd.start()
                passed.append(fwd)
        for i in range(nw):
            for k, (px, py) in enumerate(chips):
                other = o_refs[i].at[2 * px + py, 1 - ci]
                _remote(other, other, send_sems, recv_sems, 3 * (nw + i) + k, sib).wait_recv()
        for cp in sends + passed:
            cp.wait_send()

    return _multi_comm_call(body, name, [_sds((4,) + x.shape, x.dtype) for x in xs], 6 * nw, xs)


GATHER_COLLECTIVE_ID = 1


def _gather_chips_split_async(xs, name, collective_id):
    nw = len(xs)
    x_refs = [jax.new_ref(x, memory_space=pltpu.MemorySpace.HBM) for x in xs]
    o_refs = [jax.empty_ref(_sds((4,) + x.shape, x.dtype), memory_space=pltpu.MemorySpace.HBM) for x in xs]

    @pl.kernel(mesh=plsc.ScalarSubcoreMesh(axis_name="sequencer", num_cores=1), name=name,
               scratch_types=(pltpu.SemaphoreType.DMA((6 * nw,)), pltpu.SemaphoreType.DMA((6 * nw,))),
               compiler_params=pltpu.CompilerParams(collective_id=collective_id))
    def launch(send_sems, recv_sems):
        xi, yi, ci = _here()
        me = 2 * xi + yi
        sib = (xi, yi, 1 - ci)
        chips = [(1 - xi, yi), (xi, 1 - yi), (1 - xi, 1 - yi)]
        barrier = pltpu.get_barrier_semaphore()
        for peer in [sib] + [(px, py, ci) for px, py in chips]:
            pl.semaphore_signal(barrier, inc=1, device_id=peer, device_id_type=MESH)
        pl.semaphore_wait(barrier, 4)
        sends = []
        for i in range(nw):
            for k, (px, py) in enumerate(chips):
                sends.append(_remote(x_refs[i].at[ci], o_refs[i].at[me, ci], send_sems, recv_sems, 3 * i + k,
                                     (px, py, ci)))
        for cp in sends:
            cp.start()
        passed = []
        for i in range(nw):
            for k, (px, py) in enumerate(chips):
                landed = o_refs[i].at[2 * px + py, ci]
                _remote(landed, landed, send_sems, recv_sems, 3 * i + k, (px, py, ci)).wait_recv()
                fwd = _remote(landed, landed, send_sems, recv_sems, 3 * (nw + i) + k, sib)
                fwd.start()
                passed.append(fwd)
        for i in range(nw):
            for k, (px, py) in enumerate(chips):
                other = o_refs[i].at[2 * px + py, 1 - ci]
                _remote(other, other, send_sems, recv_sems, 3 * (nw + i) + k, sib).wait_recv()
        for cp in sends + passed:
            cp.wait_send()

    launch()
    return [o[...] for o in o_refs]


def _swap_sibling_half(gs):
    def body(g_refs, o_refs, send_sems, recv_sems):
        xi, yi, ci = _here()
        cps = [_remote(g.at[:, 1 - ci], o, send_sems, recv_sems, i, (xi, yi, 1 - ci))
               for i, (g, o) in enumerate(zip(g_refs, o_refs))]
        for cp in cps:
            cp.start()
        for cp in cps:
            cp.wait()

    return _multi_comm_call(body, "swap_sibling_half", [_sds((g.shape[0],) + g.shape[2:], g.dtype) for g in gs],
                            len(gs), gs)


def _scatter_chips(gps):
    nw = len(gps)

    def body(g_refs, o_refs, send_sems, recv_sems):
        xi, yi, ci = _here()
        me = 2 * xi + yi
        chips = [(1 - xi, yi), (xi, 1 - yi), (1 - xi, 1 - yi)]
        sends = [_remote(g_refs[i].at[2 * px + py], o_refs[i].at[me], send_sems, recv_sems, 3 * i + k, (px, py, ci))
                 for i in range(nw) for k, (px, py) in enumerate(chips)]
        for cp in sends:
            cp.start()
        for i in range(nw):
            for k, (px, py) in enumerate(chips):
                slot = o_refs[i].at[2 * px + py]
                _remote(slot, slot, send_sems, recv_sems, 3 * i + k, (px, py, ci)).wait_recv()
        for cp in sends:
            cp.wait_send()

    return _multi_comm_call(body, "scatter_chips", [_sds(g.shape, g.dtype) for g in gps], 3 * nw, gps)


def _share_sibling(rs):
    def body(r_refs, o_refs, send_sems, recv_sems):
        xi, yi, ci = _here()
        cps = [_remote(r, o, send_sems, recv_sems, i, (xi, yi, 1 - ci)) for i, (r, o) in enumerate(zip(r_refs, o_refs))]
        for cp in cps:
            cp.start()
        for cp in cps:
            cp.wait()

    return _multi_comm_call(body, "share_sibling", [_sds(r.shape, r.dtype) for r in rs], len(rs), rs)


def _flat_pack(arrs, dtype, row_mult):
    flat = jnp.concatenate([a.astype(dtype).reshape(-1) for a in arrs])
    unit = FLAT_W * row_mult
    npad = -(-flat.shape[0] // unit) * unit
    return jnp.pad(flat, (0, npad - flat.shape[0])).reshape(npad // FLAT_W, FLAT_W)


def _flat_unpack(flat2d, shapes):
    flat = flat2d.reshape(-1)
    outs, off = [], 0
    for s in shapes:
        size = int(np.prod(s))
        outs.append(flat[off:off + size].reshape(s))
        off += size
    return outs


def _split_cols(a, widths):
    outs, off = [], 0
    for wd in widths:
        outs.append(lax.slice_in_dim(a, off, off + wd, axis=1))
        off += wd
    return outs


def _s5_params(lam_re, lam_im, log_step, b_re, b_im, c_re, c_im, d_skip):
    g, p = lam_re.shape
    hs = b_re.shape[2]
    gt = S5_GROUP_TILE
    jn = g // gt
    lam_re = jnp.minimum(lam_re, -1e-4)
    step = jnp.exp(log_step)[:, None]
    mag = jnp.exp(lam_re * step)
    ab_re = mag * jnp.cos(lam_im * step)
    ab_im = mag * jnp.sin(lam_im * step)
    den = lam_re * lam_re + lam_im * lam_im
    f_re = ((ab_re - 1.0) * lam_re + ab_im * lam_im) / den
    f_im = (ab_im * lam_re - (ab_re - 1.0) * lam_im) / den
    bb_re = f_re[..., None] * b_re - f_im[..., None] * b_im
    bb_im = f_re[..., None] * b_im + f_im[..., None] * b_re
    a_l = jnp.concatenate([ab_re.reshape(jn, gt * p), ab_im.reshape(jn, gt * p)], axis=1).reshape(1, jn * 2 * gt * p)
    eye = jnp.eye(gt, dtype=F32)

    def blockdiag(m):
        return jnp.einsum('jahp,ab->jahbp', m.reshape(jn, gt, hs, p), eye).reshape(jn, gt * hs, gt * p)

    b_blk = jnp.concatenate([blockdiag(bb_re.transpose(0, 2, 1)), blockdiag(bb_im.transpose(0, 2, 1))], axis=2)
    c_blk = jnp.concatenate([blockdiag(c_re), blockdiag(-c_im)], axis=2)
    return a_l, b_blk, c_blk, d_skip.reshape(1, g * hs)


def _s5_scan_consts(a_l, cw):
    jn, hr = a_l.shape[1] // cw, cw // 2 // LANES
    a4 = a_l.reshape(jn, 2, hr, LANES)
    are, aim = a4[:, 0], a4[:, 1]
    flat = lambda u, v: jnp.concatenate([u, v], axis=1).reshape(jn * 2 * hr, LANES)
    return flat(are, are), flat(-aim, aim), flat(aim, -aim)


def _s5_da_lanes(p, q, cw):
    jn, hr = p.shape[0] * LANES // cw, cw // 2 // LANES
    p4, q4 = p.reshape(jn, 2, hr, LANES), q.reshape(jn, 2, hr, LANES)
    da_re = (p4[:, 0] + p4[:, 1]).reshape(jn, cw // 2)
    da_im = (q4[:, 1] - q4[:, 0]).reshape(jn, cw // 2)
    return jnp.concatenate([da_re, da_im], axis=1).reshape(1, jn * cw)


def _layer_dims(p):
    d_model = p['w_out'].shape[1]
    wa = p['proj_a'].shape[0]
    h = p['gdn_a_log'].shape[0]
    wb = p['proj_b'].shape[0]
    wc = p['proj_c'].shape[0]
    hm = p['m2_a_log'].shape[0]
    cdim = p['m2_conv_w'].shape[1]
    width = dict(zip(PROJ_ORDER, (3 * wa, wa, h, h, wb, wb, wc, cdim, hm, d_model, d_model, d_model)))
    n_in = sum(width.values())
    return width, n_in, -(-n_in // LANES) * LANES


PROJ_ORDER = ('qkv', 'az', 'braw', 'araw', 'su', 'sgate', 'cz', 'cxbc', 'cdt', 'ma', 'mb', 'mc')
WORK_ORDER = ('qkv', 'az', 'cz', 'su', 'sgate', 'cxbc', 'ma', 'mb', 'mc', 'braw', 'araw', 'cdt')
ROW_COL_TILE = 512


def _reorder_cols(a, width, src_order, dst_order, n_out):
    off, o = {}, 0
    for n in src_order:
        off[n] = o
        o += width[n]
    parts = [lax.slice_in_dim(a, off[n], off[n] + width[n], axis=a.ndim - 1) for n in dst_order]
    used = sum(width[n] for n in dst_order)
    if n_out > used:
        parts.append(jnp.zeros(a.shape[:-1] + (n_out - used,), a.dtype))
    return jnp.concatenate(parts, axis=a.ndim - 1)


def _layer_fwd(x, p):
    width, n_in, n_pad = _layer_dims(p)
    sv = {'x': x}
    h = _rowwise(_f_rms, [x], [p['norm_w'][None]], [x.shape[1]], [BF16], "rms_fwd")[0]
    w_in = _reorder_cols(p['w_in'], width, PROJ_ORDER, WORK_ORDER, n_pad)
    proj = _matmul(h, w_in, 'nn', F32, "in_proj", tn=IN_PROJ_TILE)
    wins, off = {}, 0
    for n in WORK_ORDER:
        wins[n] = Win(proj, off, width[n])
        off += width[n]
    qkv, az, cz, su, sgate, cxbc, ma, mb, mc = (wins[n] for n in WORK_ORDER[:9])
    braw, araw, cdt = (lax.slice_in_dim(proj, wins[n].off, wins[n].off + width[n], axis=1) for n in WORK_ORDER[9:])
    sv.update(h=h, w_in=w_in, qkv=qkv, az=az, braw=braw, araw=araw, su=su, sgate=sgate, cz=cz, cxbc=cxbc, cdt=cdt,
              ma=ma, mb=mb, mc=mc)
    gb0 = jnp.zeros((1, qkv.shape[1]), F32)
    sqkv = _conv_fwd(qkv, p['gdn_conv_w'], gb0, "gdn_conv_fwd")
    ya, ssa, tsa = _gdn_chunks_fwd(sqkv, az, braw, araw, p['gdn_a_log'][None], p['gdn_dt_bias'][None],
                                   p['gdn_norm_w'][None])
    sv.update(sqkv=sqkv, ssa=ssa, tsa=tsa, ya=ya)
    s5_in = tuple(p[k] for k in ('s5_lam_re', 's5_lam_im', 's5_log_step', 's5_b_re', 's5_b_im', 's5_c_re', 's5_c_im',
                                 's5_d'))
    (a_l, b_blk, c_blk, d_l), s5_vjp = jax.vjp(_s5_params, *s5_in)
    cw = b_blk.shape[2]
    aa3, bb3, bb3_conj = _s5_scan_consts(a_l, cw)
    bu = _bd_expand(su, b_blk, "s5_bu")
    s = _s5_scan(bu, aa3, bb3, False, "s5_scan_fwd")
    ymm = _bd_reduce(s, c_blk, F32, "s5_out")
    yg = _rowwise(_f_s5_post1, [ymm, su], [d_l], [su.shape[1]], [F32], "s5_post1_fwd", col_tile=ROW_COL_TILE)[0]
    tt = _matmul(yg, p['s5_glu_w'], 'nn', F32, "s5_glu")
    yb = _rowwise(_f_s5_post2, [yg, tt, sgate], [p['s5_glu_b'][None]], [su.shape[1]], [F32], "s5_post2_fwd",
                  col_tile=ROW_COL_TILE)[0]
    sv.update(aa3=aa3, bb3_conj=bb3_conj, b_blk=b_blk, c_blk=c_blk, d_l=d_l, s5_vjp=s5_vjp, s=s, ymm=ymm, yg=yg, tt=tt,
              yb=yb, cw=cw)
    sxbc = _conv_fwd(cxbc, p['m2_conv_w'], p['m2_conv_b'][None], "m2_conv_fwd")
    yc, ssc = _m2_chunks_fwd(sxbc, cz, cdt, p['m2_a_log'][None], p['m2_dt_bias'][None], p['m2_d'][None],
                             p['m2_norm_w'][None])
    sv.update(sxbc=sxbc, ssc=ssc, yc=yc)
    pa = _matmul(ya, p['proj_a'], 'nn', F32, "proj_a")
    pb = _matmul(yb, p['proj_b'], 'nn', F32, "proj_b")
    pc = _matmul(yc, p['proj_c'], 'nn', F32, "proj_c")
    merged = _rowwise(_f_merge, [ma, mb, mc, pa, pb, pc], [], [x.shape[1]], [BF16], "merge_fwd",
                      col_tile=ROW_COL_TILE)[0]
    x_next = _matmul(merged, p['w_out'], 'nn', F32, "out_proj", add=x)
    sv.update(pa=pa, pb=pb, pc=pc, merged=merged)
    return x_next, sv


def _layer_bwd(dx_out, p, sv):
    width, n_in, n_pad = _layer_dims(p)
    g = {}
    dmerged = _matmul(dx_out, p['w_out'], 'nt', F32, "out_proj_dx")
    g['w_out'] = _matmul(sv['merged'], dx_out, 'tn', F32, "out_proj_dw")
    dma, dmb, dmc, dpa, dpb, dpc = _rowwise_bwd(
        _f_merge, [sv['ma'], sv['mb'], sv['mc'], sv['pa'], sv['pb'], sv['pc']], [], [dmerged], [BF16] * 6,
        "merge_bwd", col_tile=ROW_COL_TILE)
    dya = _matmul(dpa, p['proj_a'], 'nt', F32, "proj_a_dx")
    dyb = _matmul(dpb, p['proj_b'], 'nt', F32, "proj_b_dx")
    dyc = _matmul(dpc, p['proj_c'], 'nt', F32, "proj_c_dx")
    g['proj_a'] = _matmul(sv['ya'], dpa, 'tn', F32, "proj_a_dw")
    g['proj_b'] = _matmul(sv['yb'], dpb, 'tn', F32, "proj_b_dw")
    g['proj_c'] = _matmul(sv['yc'], dpc, 'tn', F32, "proj_c_dw")
    alog, dtb, gnw = p['gdn_a_log'][None], p['gdn_dt_bias'][None], p['gdn_norm_w'][None]
    dsq, daz, db3, da3, dalog, ddtb, dgnw = _gdn_chunks_bwd(sv['sqkv'], sv['az'], sv['braw'], sv['araw'], alog, dtb, gnw,
                                                            sv['ssa'], sv['tsa'], dya)
    gb0 = jnp.zeros((1, sv['qkv'].shape[1]), F32)
    dqkv, g['gdn_conv_w'], _ = _conv_bwd(sv['qkv'], p['gdn_conv_w'], gb0, dsq, "gdn_conv_bwd")
    dbraw, daraw = jnp.sum(db3, axis=0), jnp.sum(da3, axis=0)
    g.update(gdn_a_log=dalog[0], gdn_dt_bias=ddtb[0], gdn_norm_w=dgnw[0])
    dsx, dcz, dcdt, dmalog, dmdtb, dmdsk, dmnw = _m2_chunks_bwd(
        sv['sxbc'], sv['cz'], sv['cdt'], p['m2_a_log'][None], p['m2_dt_bias'][None], p['m2_d'][None],
        p['m2_norm_w'][None], sv['ssc'], dyc)
    dcxbc, g['m2_conv_w'], dconvb = _conv_bwd(sv['cxbc'], p['m2_conv_w'], p['m2_conv_b'][None], dsx, "m2_conv_bwd")
    g.update(m2_conv_b=dconvb[0], m2_a_log=dmalog[0], m2_dt_bias=dmdtb[0], m2_d=dmdsk[0], m2_norm_w=dmnw[0])
    dyg1, dtt, dsgate, dglub = _rowwise_bwd(_f_s5_post2, [sv['yg'], sv['tt'], sv['sgate']], [p['s5_glu_b'][None]],
                                            [dyb], [F32, BF16, BF16], "s5_post2_bwd", col_tile=ROW_COL_TILE)
    dyg = _matmul(dtt, p['s5_glu_w'], 'nt', F32, "s5_glu_dx", add=dyg1)
    g['s5_glu_w'] = _matmul(sv['yg'], dtt, 'tn', F32, "s5_glu_dw")
    g['s5_glu_b'] = dglub[0]
    dymm, dsu1, dd_l = _rowwise_bwd(_f_s5_post1, [sv['ymm'], sv['su']], [sv['d_l']], [dyg], [BF16, F32],
                                    "s5_post1_bwd", col_tile=ROW_COL_TILE)
    gy = _bd_expand(dymm, sv['c_blk'], "s5_out_dx")
    ds = _s5_scan(gy, sv['aa3'], sv['bb3_conj'], True, "s5_scan_bwd")
    da_l = _s5_da_lanes(*_s5_da(ds, sv['s']), sv['cw'])
    dsu = _bd_reduce(ds, sv['b_blk'], BF16, "s5_bu_dx", add=dsu1)
    ka = sv['b_blk'].shape[1]
    db_blk = _bd_outer(sv['su'], ds, ka, "s5_bu_dw")
    dc_blk = _bd_outer(dymm, sv['s'], ka, "s5_out_dw")
    for k, v in zip(('s5_lam_re', 's5_lam_im', 's5_log_step', 's5_b_re', 's5_b_im', 's5_c_re', 's5_c_im', 's5_d'),
                    sv['s5_vjp']((da_l, db_blk, dc_blk, dd_l))):
        g[k] = v
    small = jnp.concatenate([dbraw, daraw, dcdt], axis=1).astype(BF16)
    small = jnp.pad(small, ((0, 0), (0, n_pad - n_in + sum(width[n] for n in WORK_ORDER[9:]) - small.shape[1])))
    dproj = _concat_cols([dqkv, daz, dcz, dsu, dsgate, dcxbc, dma, dmb, dmc, small], "concat_dproj")
    dh = _matmul(dproj, sv['w_in'], 'nt', F32, "in_proj_dx", tm=2048, tk=IN_PROJ_TILE)
    g['w_in'] = _reorder_cols(_matmul(sv['h'], dproj, 'tn', F32, "in_proj_dw", tn=IN_PROJ_TILE), width, WORK_ORDER,
                              PROJ_ORDER, n_in)
    dx, dnw = _rowwise_bwd(_f_rms, [sv['x']], [p['norm_w'][None]], [dh], [F32], "rms_bwd", addend=dx_out)
    g['norm_w'] = dnw[0]
    return dx, g


INPUT_NAMES = (['x'] + WEIGHT_NAMES + ['loss_target'] + ['m_' + n for n in WEIGHT_NAMES]
               + ['v_' + n for n in WEIGHT_NAMES])


def _step(d):
    xi, yi, ci = _here()
    me = 2 * xi + yi
    depth = d['norm_w'].shape[0]
    big, ssm = list(BIG), list(SHARDED_SMALL)
    nsh = 4
    full, gathered = {}, {}
    for first, stop in ((0, 1), (1, depth)):
        halves = [d[n][first:stop].astype(BF16).reshape(2, -1, d[n].shape[-1]) for n in big]
        gots = (_gather_chips_split(halves, "gather_weights") if first == 0
                else _gather_chips_split_async(halves, "gather_weights_async", GATHER_COLLECTIVE_ID))
        for n, hv, got in zip(big, halves, gots):
            got = lax.dynamic_update_slice(got, hv[None], (me, 0, 0, 0))
            gathered[n, first] = got.reshape((nsh, stop - first) + d[n].shape[1:])
    cg = _gather_chips(_flat_pack([d[n] for n in ssm], F32, 8), "gather_conv_weights")
    parts = [_flat_unpack(cg[j], [d[n].shape for n in ssm]) for j in range(nsh)]
    for i, n in enumerate(ssm):
        full[n] = jnp.concatenate([parts[j][i] for j in range(nsh)], axis=SHARDED_SMALL[n])
    layer_names = [n for n in WEIGHT_NAMES if n != 'final_norm_w']

    def layer_params(l):
        p = {n: (full[n][l] if n in full else d[n][l]) for n in layer_names if n not in BIG}
        for n in big:
            got = gathered[n, 0][:, 0] if l == 0 else gathered[n, 1][:, l - 1]
            p[n] = jnp.concatenate([got[j] for j in range(nsh)], axis=BIG[n] - 1)
        return p

    x = d['x'][0]
    saved, params = [], []
    for l in range(depth):
        params.append(layer_params(l))
        x, sv = _layer_fwd(x, params[l])
        saved.append(sv)
    loss11, dx, dfw = _loss_and_grad(x, d['loss_target'][0], d['final_norm_w'][None])
    loss = lax.psum(loss11[0, 0], ("x", "y", "c"))
    grads = [None] * depth
    for l in reversed(range(depth)):
        dx, grads[l] = _layer_bwd(dx, params[l], saved[l])
    gfull = {n: jnp.stack([grads[l][n] for l in range(depth)]) for n in layer_names if n not in BIG}
    gfull['final_norm_w'] = dfw[0]
    def shard(a, axis, j):
        wd = a.shape[axis] // nsh
        return lax.slice_in_dim(a, j * wd, (j + 1) * wd, axis=axis)

    c_idx = jnp.reshape(ci, (1,)).astype(jnp.int32)
    me_idx = jnp.reshape(me, (1,)).astype(jnp.int32)
    g4 = [jnp.stack([jnp.stack([shard(grads[l][n], BIG[n] - 1, j) for l in range(depth)]) for j in range(nsh)])
          .reshape(nsh, 2, -1, d[n].shape[-1]) for n in big]
    pairs = [_add_my_half(g, r, c_idx, "add_my_half_" + n) for n, g, r in zip(big, g4, _swap_sibling_half(g4))]
    got = _scatter_chips([pb for _, pb in pairs])
    mine = [_sum_chips(pf, gt, me_idx, "sum_chips_" + n) for n, (pf, _), gt in zip(big, pairs, got)]
    theirs = _share_sibling(mine)
    out = {}
    for n, mn, th in zip(big, mine, theirs):
        both = jnp.where(ci == 0, jnp.stack([mn, th]), jnp.stack([th, mn]))
        w2, m2, v2 = (d[pre + n].reshape(both.shape[0] * both.shape[1], both.shape[2]) for pre in ('', 'm_', 'v_'))
        g2 = both.reshape(w2.shape)
        dl, nm, nv = _adamw(w2, g2, m2, v2, "adamw_" + n)
        for key, arr in (('grad_', g2), ('delta_', dl), ('new_m_', nm), ('new_v_', nv)):
            out[key + n] = arr.reshape(d[n].shape)
    small = [n for n in WEIGHT_NAMES if n not in BIG]
    sshapes = [gfull[n].shape for n in small]
    gsm = _sum_parts(_gather_all(_flat_pack([gfull[n] for n in small], F32, 8), "gather_small_grads"), "sum_devices")
    gs = dict(zip(small, _flat_unpack(gsm, sshapes)))
    for n in ssm:
        wd = d[n].shape[SHARDED_SMALL[n]]
        gs[n] = lax.dynamic_slice_in_dim(gs[n], me * wd, wd, axis=SHARDED_SMALL[n])
    lshapes = [d[n].shape for n in small]
    wps, gps, mps, vps = (_flat_pack(arrs, F32, 16) for arrs in (
        [d[n] for n in small], [gs[n] for n in small], [d['m_' + n] for n in small], [d['v_' + n] for n in small]))
    dl, nm, nv = _adamw(wps, gps, mps, vps, "adamw_small")
    for key, arr in (('grad_', gps), ('delta_', dl), ('new_m_', nm), ('new_v_', nv)):
        for n, a in zip(small, _flat_unpack(arr, lshapes)):
            out[key + n] = a
    res = [loss, dx[None]]
    for key in ('grad_', 'delta_', 'new_m_', 'new_v_'):
        res += [out[key + n] for n in WEIGHT_NAMES]
    return tuple(res)


def kernel(x, norm_w, w_in, gdn_conv_w, gdn_a_log, gdn_dt_bias, gdn_norm_w, s5_lam_re, s5_lam_im, s5_log_step, s5_b_re, s5_b_im, s5_c_re, s5_c_im, s5_d, s5_glu_w, s5_glu_b, m2_conv_w, m2_conv_b, m2_a_log, m2_dt_bias, m2_d, m2_norm_w, proj_a, proj_b, proj_c, w_out, final_norm_w, loss_target, m_norm_w, m_w_in, m_gdn_conv_w, m_gdn_a_log, m_gdn_dt_bias, m_gdn_norm_w, m_s5_lam_re, m_s5_lam_im, m_s5_log_step, m_s5_b_re, m_s5_b_im, m_s5_c_re, m_s5_c_im, m_s5_d, m_s5_glu_w, m_s5_glu_b, m_m2_conv_w, m_m2_conv_b, m_m2_a_log, m_m2_dt_bias, m_m2_d, m_m2_norm_w, m_proj_a, m_proj_b, m_proj_c, m_w_out, m_final_norm_w, v_norm_w, v_w_in, v_gdn_conv_w, v_gdn_a_log, v_gdn_dt_bias, v_gdn_norm_w, v_s5_lam_re, v_s5_lam_im, v_s5_log_step, v_s5_b_re, v_s5_b_im, v_s5_c_re, v_s5_c_im, v_s5_d, v_s5_glu_w, v_s5_glu_b, v_m2_conv_w, v_m2_conv_b, v_m2_a_log, v_m2_dt_bias, v_m2_d, v_m2_norm_w, v_proj_a, v_proj_b, v_proj_c, v_w_out, v_final_norm_w):
    args = (x, norm_w, w_in, gdn_conv_w, gdn_a_log, gdn_dt_bias, gdn_norm_w, s5_lam_re, s5_lam_im, s5_log_step, s5_b_re, s5_b_im, s5_c_re, s5_c_im, s5_d, s5_glu_w, s5_glu_b, m2_conv_w, m2_conv_b, m2_a_log, m2_dt_bias, m2_d, m2_norm_w, proj_a, proj_b, proj_c, w_out, final_norm_w, loss_target, m_norm_w, m_w_in, m_gdn_conv_w, m_gdn_a_log, m_gdn_dt_bias, m_gdn_norm_w, m_s5_lam_re, m_s5_lam_im, m_s5_log_step, m_s5_b_re, m_s5_b_im, m_s5_c_re, m_s5_c_im, m_s5_d, m_s5_glu_w, m_s5_glu_b, m_m2_conv_w, m_m2_conv_b, m_m2_a_log, m_m2_dt_bias, m_m2_d, m_m2_norm_w, m_proj_a, m_proj_b, m_proj_c, m_w_out, m_final_norm_w, v_norm_w, v_w_in, v_gdn_conv_w, v_gdn_a_log, v_gdn_dt_bias, v_gdn_norm_w, v_s5_lam_re, v_s5_lam_im, v_s5_log_step, v_s5_b_re, v_s5_b_im, v_s5_c_re, v_s5_c_im, v_s5_d, v_s5_glu_w, v_s5_glu_b, v_m2_conv_w, v_m2_conv_b, v_m2_a_log, v_m2_dt_bias, v_m2_d, v_m2_norm_w, v_proj_a, v_proj_b, v_proj_c, v_w_out, v_final_norm_w)
    return _step(dict(zip(INPUT_NAMES, args)))
```

```python
import functools
import math
from typing import NamedTuple

import jax
import jax.numpy as jnp
import numpy as np
from jax import lax
from jax.experimental import pallas as pl
from jax.experimental.pallas import tpu as pltpu
from jax.experimental.pallas import tpu_sc as plsc

F32 = jnp.float32
BF16 = jnp.bfloat16
HI = lax.Precision.HIGH
MESH = pl.DeviceIdType.MESH

CHUNK = 64
CONV_K = 4
NORM_EPS = 1e-6
GDN_HEAD_DIM = 128
M2_HEAD_DIM = 64
M2_STATE = 128
M2_GROUPS = 4
S5_GROUP_TILE = 8
ADAM_LR = 0.001
ADAM_B1 = 0.9
ADAM_B2 = 0.999
ADAM_EPS = 1e-08
ADAM_WD = 0.01
ADAM_STEP = 10
LANES = 128
SUBLANES = 8
VMEM_LIMIT_BYTES = 56 * 1024 * 1024

WEIGHT_NAMES = ['norm_w', 'w_in', 'gdn_conv_w', 'gdn_a_log', 'gdn_dt_bias', 'gdn_norm_w', 's5_lam_re', 's5_lam_im',
                's5_log_step', 's5_b_re', 's5_b_im', 's5_c_re', 's5_c_im', 's5_d', 's5_glu_w', 's5_glu_b',
                'm2_conv_w', 'm2_conv_b', 'm2_a_log', 'm2_dt_bias', 'm2_d', 'm2_norm_w', 'proj_a', 'proj_b',
                'proj_c', 'w_out', 'final_norm_w']
BIG = {'w_in': 2, 'proj_a': 2, 'proj_b': 2, 'proj_c': 2, 'w_out': 1, 's5_glu_w': 1}
SHARDED_SMALL = {'gdn_conv_w': 2, 'm2_conv_w': 2}


def _call(body, *, name, out_shape, grid=None, in_specs=None, out_specs=None, scratch_shapes=(), semantics=None,
          num_scalar_prefetch=None):
    params = dict(vmem_limit_bytes=VMEM_LIMIT_BYTES)
    if semantics is not None:
        params['dimension_semantics'] = semantics
    kw = dict(name=name, out_shape=out_shape, compiler_params=pltpu.CompilerParams(**params))
    if num_scalar_prefetch is not None:
        kw['grid_spec'] = pltpu.PrefetchScalarGridSpec(num_scalar_prefetch=num_scalar_prefetch, grid=grid,
                                                       in_specs=in_specs, out_specs=out_specs,
                                                       scratch_shapes=scratch_shapes)
    else:
        if grid is not None:
            kw['grid'] = grid
        if in_specs is not None:
            kw['in_specs'] = in_specs
        if out_specs is not None:
            kw['out_specs'] = out_specs
        if scratch_shapes:
            kw['scratch_shapes'] = scratch_shapes
    return pl.pallas_call(body, **kw)


def _tile(n, target, unit):
    if n <= target:
        return n
    t = (target // unit) * unit
    while t >= unit:
        if n % t == 0:
            return t
        t -= unit
    raise ValueError(f"no tile for {n} (unit {unit}, target {target})")


def _sds(shape, dtype):
    return jax.ShapeDtypeStruct(tuple(shape), dtype)


def _sigmoid(x):
    return jax.nn.sigmoid(x)


def _silu(x):
    return x * jax.nn.sigmoid(x)


def _softplus(x):
    return jnp.maximum(x, 0.0) + jnp.log(1.0 + jnp.exp(-jnp.abs(x)))


def _gelu_tanh(x):
    return 0.5 * x * (1.0 + jnp.tanh(math.sqrt(2.0 / math.pi) * (x + 0.044715 * (x * x * x))))


def _dot(a, b, dims, prec=None):
    return lax.dot_general(a, b, (dims, ((), ())), precision=prec, preferred_element_type=F32)


def _nn(a, b, prec=None):
    return _dot(a, b, ((1,), (0,)), prec)


def _nt(a, b, prec=None):
    return _dot(a, b, ((1,), (1,)), prec)


def _tn(a, b, prec=None):
    return _dot(a, b, ((0,), (0,)), prec)


IN_PROJ_TILE = 1664
MATMUL_TILES = {'nn': (1024, 1024, 2048), 'nt': (1024, 1024, 2048), 'tn': (1024, 1024, 2048)}


def _matmul(a, b, mode, out_dtype, name, tm=None, tn=None, tk=None, add=None):
    tm, tn, tk = (t if t is not None else dflt for t, dflt in zip((tm, tn, tk), MATMUL_TILES[mode]))
    if mode == 'nn':
        (m, k), (k2, n) = a.shape, b.shape
    elif mode == 'nt':
        (m, k), (n, k2) = a.shape, b.shape
    else:
        (k, m), (k2, n) = a.shape, b.shape
    assert k == k2, (a.shape, b.shape, mode)
    tm = _tile(m, tm, LANES if mode == 'tn' else SUBLANES)
    tn = _tile(n, tn, LANES)
    tk = _tile(k, tk, LANES if mode != 'tn' else SUBLANES * 2)
    nk = k // tk
    dims = {'nn': ((1,), (0,)), 'nt': ((1,), (1,)), 'tn': ((0,), (0,))}[mode]

    def body(*refs):
        a_ref, b_ref = refs[:2]
        o_ref, acc_ref = refs[-2:]
        kk = pl.program_id(2)

        @pl.when(kk == 0)
        def _():
            acc_ref[...] = jnp.zeros_like(acc_ref)

        acc_ref[...] += _dot(a_ref[...].astype(BF16), b_ref[...].astype(BF16), dims)

        @pl.when(kk == nk - 1)
        def _():
            res = acc_ref[...]
            if add is not None:
                res = res + refs[2][...].astype(F32)
            o_ref[...] = res.astype(o_ref.dtype)

    a_spec = pl.BlockSpec((tk, tm), lambda i, j, kk: (kk, i)) if mode == 'tn' else pl.BlockSpec((tm, tk), lambda i, j, kk: (i, kk))
    b_spec = pl.BlockSpec((tn, tk), lambda i, j, kk: (j, kk)) if mode == 'nt' else pl.BlockSpec((tk, tn), lambda i, j, kk: (kk, j))
    o_spec = pl.BlockSpec((tm, tn), lambda i, j, kk: (i, j))
    ops = (a, b) if add is None else (a, b, add)
    return _call(body, name=name, out_shape=_sds((m, n), out_dtype), grid=(m // tm, n // tn, nk),
                 in_specs=[a_spec, b_spec] + ([] if add is None else [o_spec]), out_specs=o_spec,
                 scratch_shapes=[pltpu.VMEM((tm, tn), F32)], semantics=("parallel", "parallel", "arbitrary"))(*ops)


class Win(NamedTuple):
    arr: jax.Array
    off: int
    width: int

    @property
    def shape(self):
        return (self.arr.shape[0], self.width)


def _win(x):
    return x if isinstance(x, Win) else Win(x, 0, x.shape[1])


def _col_tile(wins, target):
    ct = (min(target, min(w.width for w in wins)) // LANES) * LANES
    while ct > LANES and any(w.width % ct or w.off % ct for w in wins):
        ct -= LANES
    assert all(w.width % ct == 0 and w.off % ct == 0 for w in wins), [(w.off, w.width) for w in wins]
    return ct


def _wspec(rows, ct, w, row_first=True):
    base = w.off // ct
    if row_first:
        return pl.BlockSpec((rows, ct), lambda i, j: (i, base + j))
    return pl.BlockSpec((rows, ct), lambda j, i: (i, base + j))


def _bd_expand(a, b, name, tm=1024):
    a = _win(a)
    t = a.shape[0]
    jn, ka, nb = b.shape
    r = nb // LANES
    tm = _tile(t, tm, SUBLANES)
    assert a.shape[1] == jn * ka and a.off % ka == 0
    abase = a.off // ka

    def body(a_ref, b_ref, o_ref):
        o_ref[...] = _nn(a_ref[...].astype(BF16), b_ref[...].astype(BF16)).reshape(tm, r, LANES)

    return _call(body, name=name, out_shape=_sds((t, jn * r, LANES), F32), grid=(t // tm, jn),
                 in_specs=[pl.BlockSpec((tm, ka), lambda i, j: (i, abase + j)),
                           pl.BlockSpec((None, ka, nb), lambda i, j: (j, 0, 0))],
                 out_specs=pl.BlockSpec((tm, r, LANES), lambda i, j: (i, j, 0)),
                 semantics=("parallel", "parallel"))(a.arr, b)


def _bd_reduce(a3, b, out_dtype, name, tm=1024, add=None):
    t = a3.shape[0]
    jn, ka, nb = b.shape
    r = nb // LANES
    tm = _tile(t, tm, SUBLANES)

    def body(*refs):
        a_ref, b_ref, o_ref = refs[0], refs[1], refs[-1]
        res = _nt(a_ref[...].reshape(tm, nb).astype(BF16), b_ref[...].astype(BF16))
        if add is not None:
            res = res + refs[2][...].astype(F32)
        o_ref[...] = res.astype(o_ref.dtype)

    o_spec = pl.BlockSpec((tm, ka), lambda i, j: (i, j))
    ops = (a3, b) if add is None else (a3, b, add)
    return _call(body, name=name, out_shape=_sds((t, jn * ka), out_dtype), grid=(t // tm, jn),
                 in_specs=[pl.BlockSpec((tm, r, LANES), lambda i, j: (i, j, 0)),
                           pl.BlockSpec((None, ka, nb), lambda i, j: (j, 0, 0))] + ([] if add is None else [o_spec]),
                 out_specs=o_spec, semantics=("parallel", "parallel"))(*ops)


def _bd_outer(a, b3, ka, name, tk=1024):
    a = _win(a)
    t = a.shape[0]
    jn = a.shape[1] // ka
    r = b3.shape[1] // jn
    nb = r * LANES
    assert a.off % ka == 0
    abase = a.off // ka
    tk = _tile(t, tk, SUBLANES * 2)

    def body(a_ref, b_ref, o_ref):
        @pl.when(pl.program_id(1) == 0)
        def _():
            o_ref[...] = jnp.zeros_like(o_ref)

        o_ref[...] += _tn(a_ref[...].astype(BF16), b_ref[...].reshape(tk, nb).astype(BF16))

    return _call(body, name=name, out_shape=_sds((jn, ka, nb), F32), grid=(jn, t // tk),
                 in_specs=[pl.BlockSpec((tk, ka), lambda j, kk: (kk, abase + j)),
                           pl.BlockSpec((tk, r, LANES), lambda j, kk: (kk, j, 0))],
                 out_specs=pl.BlockSpec((None, ka, nb), lambda j, kk: (j, 0, 0)),
                 semantics=("parallel", "arbitrary"))(a.arr, b3)


def _concat_cols(parts, name, tb=256):
    t = parts[0].shape[0]
    tb = _tile(t, tb, SUBLANES * 2)
    widths = [p.shape[1] for p in parts]
    assert all(w % LANES == 0 for w in widths)

    def body(*refs):
        o_ref, off = refs[-1], 0
        for r, w in zip(refs[:-1], widths):
            o_ref[:, off:off + w] = r[...]
            off += w

    return _call(body, name=name, out_shape=_sds((t, sum(widths)), parts[0].dtype), grid=(t // tb,),
                 in_specs=[pl.BlockSpec((tb, w), lambda i: (i, 0)) for w in widths],
                 out_specs=pl.BlockSpec((tb, sum(widths)), lambda i: (i, 0)), semantics=("parallel",))(*parts)


def _rowwise_tiles(rows, tb, col_tile):
    rows = [_win(r) for r in rows]
    t = rows[0].shape[0]
    tb = _tile(t, tb, SUBLANES * 2)
    if col_tile is None:
        assert all(r.off % r.width == 0 for r in rows)
        return rows, tb, None, 1
    ct = _col_tile(rows, col_tile)
    tb = _tile(t, max(tb, BLOCK_BYTES // (4 * ct)), SUBLANES * 2)
    return rows, tb, ct, rows[0].width // ct


def _rowwise(fn, rows, params, out_widths, out_dtypes, name, tb=256, col_tile=None):
    rows, tb, ct, ncol = _rowwise_tiles(rows, tb, col_tile)
    t = rows[0].shape[0]
    nr, npar = len(rows), len(params)

    def body(*refs):
        ins = [r[...].astype(F32) for r in refs[:nr + npar]]
        outs = fn(*ins)
        for o_ref, o in zip(refs[nr + npar:], outs):
            o_ref[...] = o.astype(o_ref.dtype)

    in_specs = [_wspec(tb, ct or r.width, r) for r in rows]
    in_specs += [pl.BlockSpec((1, ct or p.shape[1]), lambda i, j: (0, j)) for p in params]
    out_shape = [_sds((t, w), d) for w, d in zip(out_widths, out_dtypes)]
    out_specs = [pl.BlockSpec((tb, ct or w), lambda i, j: (i, j)) for w in out_widths]
    return _call(body, name=name, out_shape=out_shape, grid=(t // tb, ncol), in_specs=in_specs, out_specs=out_specs,
                 semantics=("parallel", "parallel"))(*[r.arr for r in rows], *params)


def _rowwise_bwd(fn, rows, params, cts, row_grad_dtypes, name, tb=256, addend=None, col_tile=None):
    rows, tb, ct, ncol = _rowwise_tiles(rows, tb, col_tile)
    t = rows[0].shape[0]
    nr, npar, nc = len(rows), len(params), len(cts)
    keep = [i for i, d in enumerate(row_grad_dtypes) if d is not None]
    nadd = 0 if addend is None else 1

    def body(*refs):
        ins = [r[...].astype(F32) for r in refs[:nr + npar]]
        ct = [r[...].astype(F32) for r in refs[nr + npar:nr + npar + nc]]
        _, vjp = jax.vjp(fn, *ins)
        grads = vjp(tuple(ct))
        out_refs = refs[nr + npar + nc + nadd:]
        for o_ref, i in zip(out_refs[:len(keep)], keep):
            g = grads[i]
            if nadd and i == 0:
                g = g + refs[nr + npar + nc][...].astype(F32)
            o_ref[...] = g.astype(o_ref.dtype)

        @pl.when(pl.program_id(1) == 0)
        def _():
            for o_ref in out_refs[len(keep):]:
                o_ref[...] = jnp.zeros_like(o_ref)

        for o_ref, g in zip(out_refs[len(keep):], grads[nr:]):
            o_ref[...] += g

    def plain(w):
        return pl.BlockSpec((tb, ct or w), lambda j, i: (i, j))

    in_specs = [_wspec(tb, ct or r.width, r, row_first=False) for r in rows]
    in_specs += [pl.BlockSpec((1, ct or p.shape[1]), lambda j, i: (0, j)) for p in params]
    in_specs += [plain(c.shape[1]) for c in cts]
    extra = []
    if nadd:
        in_specs += [plain(addend.shape[1])]
        extra = [addend]
    out_shape = [_sds(rows[i].shape, row_grad_dtypes[i]) for i in keep] + [_sds(p.shape, F32) for p in params]
    out_specs = [plain(rows[i].width) for i in keep]
    out_specs += [pl.BlockSpec((1, ct or p.shape[1]), lambda j, i: (0, j)) for p in params]
    return _call(body, name=name, out_shape=out_shape, grid=(ncol, t // tb), in_specs=in_specs, out_specs=out_specs,
                 semantics=("parallel", "arbitrary"))(*[r.arr for r in rows], *params, *cts, *extra)


def _chunk_masks(c):
    row = lax.broadcasted_iota(jnp.int32, (c, c), 0)
    col = lax.broadcasted_iota(jnp.int32, (c, c), 1)
    causal = row >= col
    strict = row > col
    return causal, strict, causal.astype(F32), (row > col).astype(F32), (row == col).astype(F32)


def _lane_pick(blk, idx):
    lane = lax.broadcasted_iota(jnp.int32, blk.shape, 1)
    return jnp.sum(jnp.where(lane == idx, blk, 0.0), axis=1, keepdims=True)


def _bdot(a, b, ca, cb, prec=None):
    return lax.dot_general(a, b, (((ca,), (cb,)), ((0,), (0,))), precision=prec, preferred_element_type=F32)


def _bnn(a, b, prec=None):
    return _bdot(a, b, 2, 1, prec)


def _bnt(a, b, prec=None):
    return _bdot(a, b, 2, 2, prec)


def _btn(a, b, prec=None):
    return _bdot(a, b, 1, 1, prec)


def _unit_lower_inverse(a_mat, eye):
    x = -a_mat
    t_inv = eye + x
    p = x
    for _ in range(int(math.log2(a_mat.shape[-1])) - 1):
        p = _bnn(p, p, HI)
        t_inv = t_inv + _bnn(t_inv, p, HI)
    return t_inv


@jax.custom_vjp
def _saved_inverse(a_mat, t_saved):
    return t_saved


def _saved_inverse_fwd(a_mat, t_saved):
    return t_saved, t_saved


def _saved_inverse_bwd(t_inv, ct):
    return -_bnt(_btn(t_inv, ct, HI), t_inv, HI), jnp.zeros_like(t_inv)


_saved_inverse.defvjp(_saved_inverse_fwd, _saved_inverse_bwd)


def _gdn_heads(q, k, v, z, braw, araw, alog, dtb, nw, s_in, t_saved=None):
    b, c, d = q.shape
    causal, strict, lower, upper_t, eye = _chunk_masks(c)
    lower_b = jnp.broadcast_to(lower[None], (b, c, c))
    qn = q * lax.rsqrt(jnp.sum(q * q, axis=-1, keepdims=True) + NORM_EPS) * (d ** -0.5)
    kn = k * lax.rsqrt(jnp.sum(k * k, axis=-1, keepdims=True) + NORM_EPS)
    beta = _sigmoid(braw)
    g = -jnp.exp(alog) * _softplus(araw + dtb)
    dlog = _bnn(lower_b, g * upper_t[None], HI)
    dm = jnp.where(causal[None], jnp.exp(dlog), 0.0)
    g_lanes = jnp.broadcast_to(g, (b, c, d))
    gc = _bnn(lower_b, g_lanes, HI)
    gl = jnp.sum(g_lanes, axis=1, keepdims=True)
    eg = jnp.exp(gc)
    kb = kn * beta
    a_mat = jnp.where(strict[None], _bnt(kb, kn) * dm, 0.0)
    t_inv = _unit_lower_inverse(a_mat, eye[None]) if t_saved is None else _saved_inverse(a_mat, t_saved)
    r = beta * (v - eg * _bnn(kn, s_in))
    v_new = _bnn(t_inv, r)
    qk = _bnt(qn, kn) * dm
    out = eg * _bnn(qn, s_in) + _bnn(qk, v_new)
    k_tail = kn * jnp.exp(gl - gc)
    s_out = s_in * jnp.exp(gl) + _btn(k_tail, v_new)
    y = out * lax.rsqrt(jnp.mean(out * out, axis=-1, keepdims=True) + NORM_EPS) * nw[None] * _silu(z)
    if t_saved is None:
        return y, s_out, t_inv
    return y, s_out


def _gdn_stack(refs, hb, d, h, first_head):
    q_ref, k_ref, v_ref, z_ref, b_ref, a_ref, alog_ref, dtb_ref = refs
    sls = [slice(i * d, (i + 1) * d) for i in range(hb)]
    heads = [first_head + i for i in range(hb)]
    wide = [jnp.stack([r[:, sl] for sl in sls]) for r in (q_ref, k_ref, v_ref, z_ref)]
    cols = [jnp.stack([_lane_pick(r[...], hd) for hd in heads]) for r in (b_ref, a_ref, alog_ref, dtb_ref)]
    return wide + cols


GDN_HEADS_PER_STEP = 8


def _gdn_chunks_fwd(sqkv, z, braw, araw, alog, dtb, nw, hb=GDN_HEADS_PER_STEP):
    t, w3 = sqkv.shape
    w = w3 // 3
    d = GDN_HEAD_DIM
    h = w // d
    hb = min(hb, h)
    hg = h // hb
    c = CHUNK
    nc = t // c

    def body(q_ref, k_ref, v_ref, z_ref, b_ref, a_ref, alog_ref, dtb_ref, nw_ref, y_ref, ssave_ref, tsave_ref,
             s_ref):
        @pl.when(pl.program_id(1) == 0)
        def _():
            s_ref[...] = jnp.zeros_like(s_ref)

        s_in = s_ref[...]
        ssave_ref[...] = s_in
        args = _gdn_stack((q_ref, k_ref, v_ref, z_ref, b_ref, a_ref, alog_ref, dtb_ref), hb, d, h,
                          pl.program_id(0) * hb)
        y, s_out, t_inv = _gdn_heads(*args, nw_ref[...], s_in)
        for i in range(hb):
            y_ref[:, i * d:(i + 1) * d] = y[i]
        s_ref[...] = s_out
        tsave_ref[...] = t_inv

    blk = (c, hb * d)
    z = _win(z)
    zb = z.off // (hb * d)
    assert z.off % (hb * d) == 0
    in_specs = [pl.BlockSpec(blk, lambda g, n: (n, g)), pl.BlockSpec(blk, lambda g, n: (n, hg + g)),
                pl.BlockSpec(blk, lambda g, n: (n, 2 * hg + g)), pl.BlockSpec(blk, lambda g, n: (n, zb + g)),
                pl.BlockSpec((c, h), lambda g, n: (n, 0)), pl.BlockSpec((c, h), lambda g, n: (n, 0)),
                pl.BlockSpec((1, h), lambda g, n: (0, 0)), pl.BlockSpec((1, h), lambda g, n: (0, 0)),
                pl.BlockSpec((1, d), lambda g, n: (0, 0))]
    out_shape = [_sds((t, w), F32), _sds((hg, nc, hb, d, d), F32), _sds((hg, nc, hb, c, c), F32)]
    out_specs = [pl.BlockSpec(blk, lambda g, n: (n, g)),
                 pl.BlockSpec((None, None, hb, d, d), lambda g, n: (g, n, 0, 0, 0)),
                 pl.BlockSpec((None, None, hb, c, c), lambda g, n: (g, n, 0, 0, 0))]
    return _call(body, name="gdn_chunks_fwd", out_shape=out_shape, grid=(hg, nc), in_specs=in_specs,
                 out_specs=out_specs, scratch_shapes=[pltpu.VMEM((hb, d, d), F32)],
                 semantics=("parallel", "arbitrary"))(sqkv, sqkv, sqkv, z.arr, braw, araw, alog, dtb, nw)


def _gdn_chunks_bwd(sqkv, z, braw, araw, alog, dtb, nw, ssave, tsave, dy, hb=GDN_HEADS_PER_STEP):
    t, w3 = sqkv.shape
    w = w3 // 3
    d = GDN_HEAD_DIM
    h = w // d
    hb = min(hb, h)
    hg = h // hb
    c = CHUNK
    nc = t // c

    def body(q_ref, k_ref, v_ref, z_ref, b_ref, a_ref, alog_ref, dtb_ref, nw_ref, ssave_ref, tsave_ref, dy_ref,
             dsq_ref, dz_ref, db_ref, da_ref, dalog_ref, ddtb_ref, dnw_ref, ds_ref):
        first = jnp.logical_and(pl.program_id(0) == 0, pl.program_id(1) == 0)

        @pl.when(pl.program_id(1) == 0)
        def _():
            ds_ref[...] = jnp.zeros_like(ds_ref)

        @pl.when(first)
        def _():
            dalog_ref[...] = jnp.zeros_like(dalog_ref)
            ddtb_ref[...] = jnp.zeros_like(ddtb_ref)
            dnw_ref[...] = jnp.zeros_like(dnw_ref)

        lane_h = lax.broadcasted_iota(jnp.int32, (1, h), 1)
        db_acc = jnp.zeros((c, h), F32)
        da_acc = jnp.zeros((c, h), F32)
        args = _gdn_stack((q_ref, k_ref, v_ref, z_ref, b_ref, a_ref, alog_ref, dtb_ref), hb, d, h,
                          pl.program_id(0) * hb)
        t_saved = tsave_ref[...]
        _, vjp = jax.vjp(lambda *a: _gdn_heads(*a, t_saved=t_saved), *args, nw_ref[...], ssave_ref[...])
        dyb = jnp.stack([dy_ref[:, i * d:(i + 1) * d] for i in range(hb)])
        dq, dk, dv, dz, db, da, dalog, ddtb, dnw, ds_in = vjp((dyb, ds_ref[...]))
        for i in range(hb):
            sl = slice(i * d, (i + 1) * d)
            dsq_ref[:, i * d:(i + 1) * d] = dq[i]
            dsq_ref[:, w + i * d:w + (i + 1) * d] = dk[i]
            dsq_ref[:, 2 * w + i * d:2 * w + (i + 1) * d] = dv[i]
            dz_ref[:, sl] = dz[i].astype(dz_ref.dtype)
            onehot = (lane_h == pl.program_id(0) * hb + i).astype(F32)
            db_acc = db_acc + db[i] * onehot
            da_acc = da_acc + da[i] * onehot
            dalog_ref[...] += dalog[i] * onehot
            ddtb_ref[...] += ddtb[i] * onehot
        dnw_ref[...] += dnw
        ds_ref[...] = ds_in
        db_ref[...] = db_acc
        da_ref[...] = da_acc

    blk = (c, hb * d)
    rev = lambda n: nc - 1 - n
    z = _win(z)
    zb = z.off // (hb * d)
    assert z.off % (hb * d) == 0
    in_specs = [pl.BlockSpec(blk, lambda g, n: (rev(n), g)), pl.BlockSpec(blk, lambda g, n: (rev(n), hg + g)),
                pl.BlockSpec(blk, lambda g, n: (rev(n), 2 * hg + g)), pl.BlockSpec(blk, lambda g, n: (rev(n), zb + g)),
                pl.BlockSpec((c, h), lambda g, n: (rev(n), 0)), pl.BlockSpec((c, h), lambda g, n: (rev(n), 0)),
                pl.BlockSpec((1, h), lambda g, n: (0, 0)), pl.BlockSpec((1, h), lambda g, n: (0, 0)),
                pl.BlockSpec((1, d), lambda g, n: (0, 0)),
                pl.BlockSpec((None, None, hb, d, d), lambda g, n: (g, rev(n), 0, 0, 0)),
                pl.BlockSpec((None, None, hb, c, c), lambda g, n: (g, rev(n), 0, 0, 0)),
                pl.BlockSpec(blk, lambda g, n: (rev(n), g))]
    assert hg == 1
    out_shape = [_sds((t, w3), F32), _sds((t, w), BF16),
                 _sds((hg, t, h), F32), _sds((hg, t, h), F32), _sds((1, h), F32), _sds((1, h), F32), _sds((1, d), F32)]
    out_specs = [pl.BlockSpec((c, w3), lambda g, n: (rev(n), 0)), pl.BlockSpec(blk, lambda g, n: (rev(n), g))]
    out_specs += [pl.BlockSpec((None, c, h), lambda g, n: (g, rev(n), 0))] * 2
    out_specs += [pl.BlockSpec((1, h), lambda g, n: (0, 0)), pl.BlockSpec((1, h), lambda g, n: (0, 0)),
                  pl.BlockSpec((1, d), lambda g, n: (0, 0))]
    return _call(body, name="gdn_chunks_bwd", out_shape=out_shape, grid=(hg, nc), in_specs=in_specs,
                 out_specs=out_specs, scratch_shapes=[pltpu.VMEM((hb, d, d), F32)],
                 semantics=("arbitrary", "arbitrary"))(sqkv, sqkv, sqkv, z.arr, braw, araw, alog, dtb, nw, ssave, tsave,
                                                       dy)


def _m2_groups(xs, z, bm, cm, dtraws, alogs, dtbs, dsks, nw, st):
    g, c, gw = xs.shape
    rep = len(dtraws)
    causal, _, lower, upper_t, _ = _chunk_masks(c)
    lower_b = jnp.broadcast_to(lower[None], (g, c, c))
    lane_head = lax.broadcasted_iota(jnp.int32, (1, 1, gw), 2) // M2_HEAD_DIM

    def expand(cols):
        res = jnp.broadcast_to(cols[-1], (g, cols[-1].shape[1], gw))
        for i in reversed(range(rep - 1)):
            res = jnp.where(lane_head == i, cols[i], res)
        return res

    dts = [_softplus(dtraws[i] + dtbs[i]) for i in range(rep)]
    adts = [-jnp.exp(alogs[i]) * dts[i] for i in range(rep)]
    dt_l, adt_l, dsk_l = expand(dts), expand(adts), expand(dsks)
    xdt = xs * dt_l
    acum = _bnn(lower_b, adt_l, HI)
    alast = jnp.sum(adt_l, axis=1, keepdims=True)
    scores = _bnt(cm, bm)
    y = jnp.exp(acum) * _bnn(cm, st) + dsk_l * xs
    for i in range(rep):
        seg = jnp.where(causal[None], jnp.exp(_bnn(lower_b, adts[i] * upper_t[None], HI)), 0.0)
        y = y + _bnn(scores * seg, jnp.where(lane_head == i, xdt, 0.0))
    st_out = st * jnp.exp(alast) + _btn(bm, xdt * jnp.exp(alast - acum))
    y2 = y * _silu(z)
    out = y2 * lax.rsqrt(jnp.mean(y2 * y2, axis=-1, keepdims=True) + NORM_EPS) * nw
    return out, st_out


def _m2_dims(sxbc, z):
    t = sxbc.shape[0]
    w2 = z.shape[1]
    g = M2_GROUPS
    n = M2_STATE
    assert sxbc.shape[1] == w2 + 2 * g * n
    gw = w2 // g
    return t, w2, g, n, gw, gw // M2_HEAD_DIM, t // CHUNK


def _m2_args(refs, g, n, gw, rep, st):
    sx_ref, z_ref, dt_ref, alog_ref, dtb_ref, dsk_ref, nw_ref = refs
    w2 = g * gw
    xs = jnp.stack([sx_ref[:, i * gw:(i + 1) * gw] for i in range(g)])
    z = jnp.stack([z_ref[:, i * gw:(i + 1) * gw] for i in range(g)])
    bm = jnp.stack([sx_ref[:, w2 + i * n:w2 + (i + 1) * n] for i in range(g)])
    cm = jnp.stack([sx_ref[:, w2 + (g + i) * n:w2 + (g + i + 1) * n] for i in range(g)])
    nw = jnp.stack([nw_ref[:, i * gw:(i + 1) * gw] for i in range(g)])

    def cols(ref):
        blk = ref[...]
        return [jnp.stack([_lane_pick(blk, gi * rep + i) for gi in range(g)]) for i in range(rep)]

    return xs, z, bm, cm, cols(dt_ref), cols(alog_ref), cols(dtb_ref), cols(dsk_ref), nw, st


def _m2_chunks_fwd(sxbc, z, dtraw, alog, dtb, dsk, nw):
    t, w2, g, n, gw, rep, nc = _m2_dims(sxbc, z)
    hm = dtraw.shape[1]
    c = CHUNK
    wx = sxbc.shape[1]

    def body(sx_ref, z_ref, dt_ref, alog_ref, dtb_ref, dsk_ref, nw_ref, y_ref, ssave_ref, st_ref):
        @pl.when(pl.program_id(0) == 0)
        def _():
            st_ref[...] = jnp.zeros_like(st_ref)

        st = st_ref[...]
        ssave_ref[...] = st
        y, st_out = _m2_groups(*_m2_args((sx_ref, z_ref, dt_ref, alog_ref, dtb_ref, dsk_ref, nw_ref), g, n, gw, rep, st))
        for i in range(g):
            y_ref[:, i * gw:(i + 1) * gw] = y[i]
        st_ref[...] = st_out

    z = _win(z)
    zb = z.off // w2
    assert z.off % w2 == 0
    in_specs = [pl.BlockSpec((c, wx), lambda k: (k, 0)), pl.BlockSpec((c, w2), lambda k: (k, zb)),
                pl.BlockSpec((c, hm), lambda k: (k, 0)),
                pl.BlockSpec((1, hm), lambda k: (0, 0)), pl.BlockSpec((1, hm), lambda k: (0, 0)),
                pl.BlockSpec((1, hm), lambda k: (0, 0)), pl.BlockSpec((1, w2), lambda k: (0, 0))]
    out_shape = [_sds((t, w2), F32), _sds((nc, g, n, gw), F32)]
    out_specs = [pl.BlockSpec((c, w2), lambda k: (k, 0)), pl.BlockSpec((None, g, n, gw), lambda k: (k, 0, 0, 0))]
    return _call(body, name="m2_chunks_fwd", out_shape=out_shape, grid=(nc,), in_specs=in_specs,
                 out_specs=out_specs, scratch_shapes=[pltpu.VMEM((g, n, gw), F32)],
                 semantics=("arbitrary",))(sxbc, z.arr, dtraw, alog, dtb, dsk, nw)


def _m2_chunks_bwd(sxbc, z, dtraw, alog, dtb, dsk, nw, ssave, dy):
    t, w2, g, n, gw, rep, nc = _m2_dims(sxbc, z)
    hm = dtraw.shape[1]
    c = CHUNK

    wx = sxbc.shape[1]

    def body(sx_ref, z_ref, dt_ref, alog_ref, dtb_ref, dsk_ref, nw_ref, ssave_ref, dy_ref,
             dsx_ref, dz_ref, ddt_ref, dalog_ref, ddtb_ref, ddsk_ref, dnw_ref, dst_ref):
        @pl.when(pl.program_id(0) == 0)
        def _():
            dst_ref[...] = jnp.zeros_like(dst_ref)
            dnw_ref[...] = jnp.zeros_like(dnw_ref)
            dalog_ref[...] = jnp.zeros_like(dalog_ref)
            ddtb_ref[...] = jnp.zeros_like(ddtb_ref)
            ddsk_ref[...] = jnp.zeros_like(ddsk_ref)

        args = _m2_args((sx_ref, z_ref, dt_ref, alog_ref, dtb_ref, dsk_ref, nw_ref), g, n, gw, rep, ssave_ref[...])
        _, vjp = jax.vjp(_m2_groups, *args)
        dyb = jnp.stack([dy_ref[:, i * gw:(i + 1) * gw] for i in range(g)])
        dxs, dz, dbm, dcm, ddts, dalogs, ddtbs, ddsks, dnw, dst = vjp((dyb, dst_ref[...]))
        dst_ref[...] = dst
        lane_h = lax.broadcasted_iota(jnp.int32, (1, hm), 1)
        ddt = jnp.zeros((c, hm), F32)
        for gi in range(g):
            dsx_ref[:, gi * gw:(gi + 1) * gw] = dxs[gi]
            dsx_ref[:, w2 + gi * n:w2 + (gi + 1) * n] = dbm[gi]
            dsx_ref[:, w2 + (g + gi) * n:w2 + (g + gi + 1) * n] = dcm[gi]
            dz_ref[:, gi * gw:(gi + 1) * gw] = dz[gi].astype(dz_ref.dtype)
            dnw_ref[:, gi * gw:(gi + 1) * gw] += dnw[gi]
            for i in range(rep):
                onehot = (lane_h == gi * rep + i).astype(F32)
                ddt = ddt + ddts[i][gi] * onehot
                dalog_ref[...] += dalogs[i][gi] * onehot
                ddtb_ref[...] += ddtbs[i][gi] * onehot
                ddsk_ref[...] += ddsks[i][gi] * onehot
        ddt_ref[...] = ddt

    rev = lambda k: nc - 1 - k
    z = _win(z)
    zb = z.off // w2
    assert z.off % w2 == 0
    in_specs = [pl.BlockSpec((c, wx), lambda k: (rev(k), 0)), pl.BlockSpec((c, w2), lambda k: (rev(k), zb)),
                pl.BlockSpec((c, hm), lambda k: (rev(k), 0)),
                pl.BlockSpec((1, hm), lambda k: (0, 0)), pl.BlockSpec((1, hm), lambda k: (0, 0)),
                pl.BlockSpec((1, hm), lambda k: (0, 0)), pl.BlockSpec((1, w2), lambda k: (0, 0)),
                pl.BlockSpec((None, g, n, gw), lambda k: (rev(k), 0, 0, 0)),
                pl.BlockSpec((c, w2), lambda k: (rev(k), 0))]
    out_shape = [_sds((t, wx), F32), _sds((t, w2), BF16), _sds((t, hm), F32), _sds((1, hm), F32), _sds((1, hm), F32),
                 _sds((1, hm), F32), _sds((1, w2), F32)]
    out_specs = [pl.BlockSpec((c, wx), lambda k: (rev(k), 0)), pl.BlockSpec((c, w2), lambda k: (rev(k), 0)),
                 pl.BlockSpec((c, hm), lambda k: (rev(k), 0)),
                 pl.BlockSpec((1, hm), lambda k: (0, 0)), pl.BlockSpec((1, hm), lambda k: (0, 0)),
                 pl.BlockSpec((1, hm), lambda k: (0, 0)), pl.BlockSpec((1, w2), lambda k: (0, 0))]
    return _call(body, name="m2_chunks_bwd", out_shape=out_shape, grid=(nc,), in_specs=in_specs,
                 out_specs=out_specs, scratch_shapes=[pltpu.VMEM((g, n, gw), F32)],
                 semantics=("arbitrary",))(sxbc, z.arr, dtraw, alog, dtb, dsk, nw, ssave, dy)


def _s5_scan(bu3, aa3, bb3, reverse, name, tb=256):
    t, rtot, _ = bu3.shape
    jn = rtot // SUBLANES
    tb = _tile(t, tb, SUBLANES)
    nb = t // tb

    def body(bu_ref, aa_ref, bb_ref, s_ref, st_ref):
        @pl.when(pl.program_id(0) == 0)
        def _():
            st_ref[...] = jnp.zeros_like(st_ref)

        tiles = [slice(j * SUBLANES, (j + 1) * SUBLANES) for j in range(jn)]
        aa = [aa_ref[tl, :] for tl in tiles]
        bb = [bb_ref[tl, :] for tl in tiles]

        def step(k, carry):
            r = tb - 1 - k if reverse else k
            new = []
            for j, tl in enumerate(tiles):
                sj, wj = carry[2 * j], carry[2 * j + 1]
                xj = bu_ref[r, tl, :]
                nj = aa[j] * sj + bb[j] * wj + xj
                s_ref[r, tl, :] = nj
                new += [nj, aa[j] * wj - bb[j] * sj + pltpu.roll(xj, SUBLANES // 2, 0)]
            return tuple(new)

        init = []
        for tl in tiles:
            init += [st_ref[tl, :], pltpu.roll(st_ref[tl, :], SUBLANES // 2, 0)]
        last = lax.fori_loop(0, tb, step, tuple(init), unroll=8)
        for j, tl in enumerate(tiles):
            st_ref[tl, :] = last[2 * j]

    rb = (lambda i: nb - 1 - i) if reverse else (lambda i: i)
    return _call(body, name=name, out_shape=_sds(bu3.shape, F32), grid=(nb,),
                 in_specs=[pl.BlockSpec((tb, rtot, LANES), lambda i: (rb(i), 0, 0)),
                           pl.BlockSpec((rtot, LANES), lambda i: (0, 0)), pl.BlockSpec((rtot, LANES), lambda i: (0, 0))],
                 out_specs=pl.BlockSpec((tb, rtot, LANES), lambda i: (rb(i), 0, 0)),
                 scratch_shapes=[pltpu.VMEM((rtot, LANES), F32)], semantics=("arbitrary",))(bu3, aa3, bb3)


def _s5_da(ds3, s3, tb=256):
    t, rtot, _ = ds3.shape
    tb = _tile(t, tb, SUBLANES)
    nb = t // tb

    def body(ds_ref, s_ref, halo_ref, p_ref, q_ref):
        i = pl.program_id(0)

        @pl.when(i == 0)
        def _():
            p_ref[...] = jnp.zeros_like(p_ref)
            q_ref[...] = jnp.zeros_like(q_ref)

        for j in range(rtot // SUBLANES):
            tl = slice(j * SUBLANES, (j + 1) * SUBLANES)
            prev = jnp.where(i == 0, 0.0, halo_ref[:, tl, :])
            sh = jnp.concatenate([prev, s_ref[0:tb - 1, tl, :]], axis=0)
            d = ds_ref[:, tl, :]
            p_ref[tl, :] += jnp.sum(d * sh, axis=0)
            q_ref[tl, :] += jnp.sum(d * pltpu.roll(sh, SUBLANES // 2, 1), axis=0)

    blk = pl.BlockSpec((tb, rtot, LANES), lambda i: (i, 0, 0))
    acc = pl.BlockSpec((rtot, LANES), lambda i: (0, 0))
    return _call(body, name="s5_da", out_shape=[_sds((rtot, LANES), F32)] * 2, grid=(nb,),
                 in_specs=[blk, blk, pl.BlockSpec((1, rtot, LANES), lambda i: (jnp.maximum(i * tb - 1, 0), 0, 0))],
                 out_specs=[acc, acc], semantics=("arbitrary",))(ds3, s3, s3)


def _conv_rows(t, tb):
    tb = _tile(t, tb, SUBLANES * 2)
    return tb, t // tb, tb // SUBLANES


def _shift_down(x, above, s):
    n = x.shape[0]
    y = pltpu.roll(x, s, 0)
    row = lax.broadcasted_iota(jnp.int32, above.shape, 0)
    head = jnp.where(row < s, pltpu.roll(above, s, 0), y[:SUBLANES])
    return head if n == SUBLANES else jnp.concatenate([head, y[SUBLANES:]], axis=0)


def _shift_up(x, below, s):
    n = x.shape[0]
    y = pltpu.roll(x, n - s, 0)
    row = lax.broadcasted_iota(jnp.int32, below.shape, 0)
    tail = jnp.where(row >= SUBLANES - s, pltpu.roll(below, SUBLANES - s, 0), y[n - SUBLANES:])
    return tail if n == SUBLANES else jnp.concatenate([y[:n - SUBLANES], tail], axis=0)


def _conv_taps(x, above, w_ref, b_ref):
    xs = [x] + [_shift_down(x, above, s) for s in range(1, CONV_K)]
    c = b_ref[...] + w_ref[CONV_K - 1:CONV_K, :] * x
    for s in range(1, CONV_K):
        c = c + w_ref[CONV_K - 1 - s:CONV_K - s, :] * xs[s]
    return c, xs


CONV_COL_TILE = 1024


def _conv_specs(x, tb):
    x = _win(x)
    t, cwid = x.shape
    ct = _col_tile([x], CONV_COL_TILE)
    tb, nb, hb = _conv_rows(t, max(tb, BLOCK_BYTES // (4 * ct)))
    base = x.off // ct
    blk_x = pl.BlockSpec((tb, ct), lambda j, i: (i, base + j))
    prev_x = pl.BlockSpec((SUBLANES, ct), lambda j, i: (jnp.maximum(i * hb - 1, 0), base + j))
    next_x = pl.BlockSpec((SUBLANES, ct), lambda j, i: (jnp.minimum((i + 1) * hb, nb * hb - 1), base + j))
    blk = pl.BlockSpec((tb, ct), lambda j, i: (i, j))
    nxt = pl.BlockSpec((SUBLANES, ct), lambda j, i: (jnp.minimum((i + 1) * hb, nb * hb - 1), j))
    taps = pl.BlockSpec((CONV_K, ct), lambda j, i: (0, j))
    bias = pl.BlockSpec((1, ct), lambda j, i: (0, j))
    return x, tb, nb, cwid // ct, dict(blk_x=blk_x, prev_x=prev_x, next_x=next_x, blk=blk, nxt=nxt, taps=taps, bias=bias)


def _conv_fwd(x, w, b, name, tb=256):
    x, tb, nb, ncol, sp = _conv_specs(x, tb)
    t, cwid = x.shape

    def body(x_ref, halo_ref, w_ref, b_ref, o_ref):
        above = jnp.where(pl.program_id(1) == 0, 0.0, halo_ref[...])
        o_ref[...] = _silu(_conv_taps(x_ref[...], above, w_ref, b_ref)[0])

    return _call(body, name=name, out_shape=_sds((t, cwid), F32), grid=(ncol, nb),
                 in_specs=[sp['blk_x'], sp['prev_x'], sp['taps'], sp['bias']], out_specs=sp['blk'],
                 semantics=("parallel", "parallel"))(x.arr, x.arr, w, b)


def _dsilu(c):
    sg = _sigmoid(c)
    return sg * (1.0 + c * (1.0 - sg))


def _conv_bwd(x, w, b, ds, name, tb=256):
    x, tb, nb, ncol, sp = _conv_specs(x, tb)
    t, cwid = x.shape

    def body(x_ref, halo_ref, xn_ref, ds_ref, dsn_ref, w_ref, b_ref, dx_ref, dw_ref, db_ref):
        i = pl.program_id(1)

        @pl.when(i == 0)
        def _():
            dw_ref[...] = jnp.zeros_like(dw_ref)
            db_ref[...] = jnp.zeros_like(db_ref)

        x = x_ref[...]
        above = jnp.where(i == 0, 0.0, halo_ref[...])
        c, xs = _conv_taps(x, above, w_ref, b_ref)
        dc = ds_ref[...] * _dsilu(c)
        cn, _ = _conv_taps(xn_ref[...], x[tb - SUBLANES:], w_ref, b_ref)
        dcn = jnp.where(i == nb - 1, 0.0, dsn_ref[...] * _dsilu(cn))
        dx = w_ref[CONV_K - 1:CONV_K, :] * dc
        for s in range(1, CONV_K):
            dx = dx + w_ref[CONV_K - 1 - s:CONV_K - s, :] * _shift_up(dc, dcn, s)
        dx_ref[...] = dx.astype(dx_ref.dtype)
        for s in range(CONV_K):
            dw_ref[CONV_K - 1 - s:CONV_K - s, :] += jnp.sum(dc * xs[s], axis=0, keepdims=True)
        db_ref[...] += jnp.sum(dc, axis=0, keepdims=True)

    return _call(body, name=name, out_shape=[_sds((t, cwid), BF16), _sds((CONV_K, cwid), F32), _sds((1, cwid), F32)],
                 grid=(ncol, nb),
                 in_specs=[sp['blk_x'], sp['prev_x'], sp['next_x'], sp['blk'], sp['nxt'], sp['taps'], sp['bias']],
                 out_specs=[sp['blk'], sp['taps'], sp['bias']],
                 semantics=("parallel", "arbitrary"))(x.arr, x.arr, x.arr, ds, ds, w, b)


def _f_rms(x, w):
    return (x * lax.rsqrt(jnp.mean(x * x, axis=-1, keepdims=True) + NORM_EPS) * w,)


def _f_s5_post1(ymm, u, d_l):
    return (_gelu_tanh(ymm + d_l * u),)


def _f_s5_post2(yg, tt, gate, b):
    return (yg * _sigmoid(tt + b) * _silu(gate),)


def _f_merge(ma, mb, mc, pa, pb, pc):
    return (_sigmoid(ma) * pa + _sigmoid(mb) * pb + _sigmoid(mc) * pc,)


def _loss_and_grad(x, tgt, fw, tb=256):
    t, dm = x.shape
    tb = _tile(t, tb, SUBLANES * 2)

    def f(xb, wb, tb_):
        y = _f_rms(xb, wb)[0]
        e = y - tb_
        return 0.5 * jnp.sum(jnp.mean(e * e, axis=-1, keepdims=True), axis=0, keepdims=True)

    def body(x_ref, t_ref, w_ref, loss_ref, dx_ref, dw_ref):
        @pl.when(pl.program_id(0) == 0)
        def _():
            loss_ref[...] = jnp.zeros_like(loss_ref)
            dw_ref[...] = jnp.zeros_like(dw_ref)

        tgt_b = t_ref[...]
        val, vjp = jax.vjp(lambda a, b: f(a, b, tgt_b), x_ref[...], w_ref[...])
        dxb, dwb = vjp(jnp.ones((1, 1), F32))
        loss_ref[...] += val
        dx_ref[...] = dxb
        dw_ref[...] += dwb

    return _call(body, name="loss_and_grad", out_shape=[_sds((1, 1), F32), _sds((t, dm), F32), _sds((1, dm), F32)],
                 grid=(t // tb,),
                 in_specs=[pl.BlockSpec((tb, dm), lambda i: (i, 0)), pl.BlockSpec((tb, dm), lambda i: (i, 0)),
                           pl.BlockSpec((1, dm), lambda i: (0, 0))],
                 out_specs=[pl.BlockSpec((1, 1), lambda i: (0, 0)), pl.BlockSpec((tb, dm), lambda i: (i, 0)),
                            pl.BlockSpec((1, dm), lambda i: (0, 0))],
                 semantics=("arbitrary",))(x, tgt, fw)


FLAT_W = 1024


def _sum_parts(parts, name, tb=256):
    n, r, wd = parts.shape
    tb = _tile(r, tb, SUBLANES)

    def body(p_ref, o_ref):
        acc = p_ref[0]
        for k in range(1, n):
            acc = acc + p_ref[k]
        o_ref[...] = acc

    return _call(body, name=name, out_shape=_sds((r, wd), F32), grid=(r // tb,),
                 in_specs=[pl.BlockSpec((n, tb, wd), lambda i: (0, i, 0))],
                 out_specs=pl.BlockSpec((tb, wd), lambda i: (i, 0)), semantics=("parallel",))(parts)


BLOCK_BYTES = 1 << 20


def _rows_per_block(r, wd):
    return _tile(r, max(SUBLANES * 2, BLOCK_BYTES // (4 * wd) // (SUBLANES * 2) * (SUBLANES * 2)), SUBLANES * 2)


def _add_my_half(g4, recv, c_idx, name):
    p, _, r, wd = g4.shape
    tb = _rows_per_block(r, wd)

    def body(c_ref, g_ref, r_ref, o_ref, ob_ref):
        s = g_ref[...] + r_ref[...]
        o_ref[...] = s
        ob_ref[...] = s.astype(BF16)

    spec = pl.BlockSpec((None, tb, wd), lambda j, i, c_ref: (j, i, 0))
    return _call(body, name=name, out_shape=[_sds((p, r, wd), F32), _sds((p, r, wd), BF16)], grid=(p, r // tb),
                 in_specs=[pl.BlockSpec((None, None, tb, wd), lambda j, i, c_ref: (j, c_ref[0], i, 0)), spec],
                 out_specs=[spec, spec], semantics=("parallel", "parallel"), num_scalar_prefetch=1)(c_idx, g4, recv)


def _sum_chips(own, got, me_idx, name):
    p, r, wd = own.shape
    tb = _rows_per_block(r, wd)

    def body(me_ref, own_ref, got_ref, o_ref):
        me = me_ref[0]
        acc = None
        for k in range(p):
            part = jnp.where(me == k, own_ref[...], got_ref[k].astype(F32))
            acc = part if acc is None else acc + part
        o_ref[...] = acc

    return _call(body, name=name, out_shape=_sds((r, wd), F32), grid=(r // tb,),
                 in_specs=[pl.BlockSpec((None, tb, wd), lambda i, me_ref: (me_ref[0], i, 0)),
                           pl.BlockSpec((p, tb, wd), lambda i, me_ref: (0, i, 0))],
                 out_specs=pl.BlockSpec((tb, wd), lambda i, me_ref: (i, 0)),
                 semantics=("parallel",), num_scalar_prefetch=1)(me_idx, own, got)


def _adamw(w, g, m, v, name):
    r, wd = w.shape
    tb = _rows_per_block(r, wd)

    def body(w_ref, g_ref, m_ref, v_ref, d_ref, nm_ref, nv_ref):
        gg = g_ref[...]
        nm = ADAM_B1 * m_ref[...] + (1.0 - ADAM_B1) * gg
        nv = ADAM_B2 * v_ref[...] + (1.0 - ADAM_B2) * (gg * gg)
        m_hat = nm / (1.0 - ADAM_B1 ** ADAM_STEP)
        v_hat = nv / (1.0 - ADAM_B2 ** ADAM_STEP)
        d_ref[...] = -ADAM_LR * (m_hat / (jnp.sqrt(v_hat) + ADAM_EPS) + ADAM_WD * w_ref[...])
        nm_ref[...] = nm
        nv_ref[...] = nv

    spec = pl.BlockSpec((tb, wd), lambda i: (i, 0))
    return _call(body, name=name, out_shape=[_sds((r, wd), F32)] * 3, grid=(r // tb,), in_specs=[spec] * 4,
                 out_specs=[spec] * 3, semantics=("parallel",))(w, g, m, v)


def _here():
    return lax.axis_index("x"), lax.axis_index("y"), lax.axis_index("c")


def _comm_call(body, name, out_shape, n_sems, operands):
    anyspec = pl.BlockSpec(memory_space=pl.ANY)
    outs = out_shape if isinstance(out_shape, (list, tuple)) else [out_shape]
    return _call(body, name=name, out_shape=out_shape, in_specs=[anyspec] * len(operands),
                 out_specs=[anyspec] * len(outs) if isinstance(out_shape, (list, tuple)) else anyspec,
                 scratch_shapes=[pltpu.SemaphoreType.DMA((n_sems,)), pltpu.SemaphoreType.DMA((n_sems,)),
                                 pltpu.SemaphoreType.DMA(())])(*operands)


def _gather_chips(x, name):
    def body(x_ref, o_ref, send_sems, recv_sems, local_sem):
        xi, yi, ci = _here()
        chips = [(1 - xi, yi), (xi, 1 - yi), (1 - xi, 1 - yi)]
        mine = pltpu.make_async_copy(x_ref, o_ref.at[2 * xi + yi], local_sem)
        mine.start()

        def copy(k, slot, to):
            return pltpu.make_async_remote_copy(src_ref=x_ref, dst_ref=o_ref.at[slot], send_sem=send_sems.at[k],
                                                recv_sem=recv_sems.at[k], device_id=to, device_id_type=MESH)

        sends = [copy(k, 2 * xi + yi, (px, py, ci)) for k, (px, py) in enumerate(chips)]
        for cp in sends:
            cp.start()
        for k, (px, py) in enumerate(chips):
            copy(k, 2 * px + py, (px, py, ci)).wait_recv()
        for cp in sends:
            cp.wait_send()
        mine.wait()

    return _comm_call(body, name, _sds((4,) + x.shape, x.dtype), 3, (x,))


def _gather_all(x, name):
    def body(x_ref, o_ref, send_sems, recv_sems, local_sem):
        xi, yi, ci = _here()
        me = 4 * xi + 2 * yi + ci
        flips = [(fx, fy, fc) for fx in (0, 1) for fy in (0, 1) for fc in (0, 1)][1:]
        peers = [((1 - xi) if fx else xi, (1 - yi) if fy else yi, (1 - ci) if fc else ci) for fx, fy, fc in flips]
        mine = pltpu.make_async_copy(x_ref, o_ref.at[me], local_sem)
        mine.start()

        def copy(k, slot, to):
            return pltpu.make_async_remote_copy(src_ref=x_ref, dst_ref=o_ref.at[slot], send_sem=send_sems.at[k],
                                                recv_sem=recv_sems.at[k], device_id=to, device_id_type=MESH)

        sends = [copy(k, me, p) for k, p in enumerate(peers)]
        for cp in sends:
            cp.start()
        for k, (px, py, pc) in enumerate(peers):
            copy(k, 4 * px + 2 * py + pc, (px, py, pc)).wait_recv()
        for cp in sends:
            cp.wait_send()
        mine.wait()

    return _comm_call(body, name, _sds((8,) + x.shape, x.dtype), 7, (x,))


def _multi_comm_call(body, name, out_shapes, n_sems, operands):
    anyspec = pl.BlockSpec(memory_space=pl.ANY)
    nin = len(operands)

    def flat_body(*refs):
        body(refs[:nin], refs[nin:nin + len(out_shapes)], refs[-2], refs[-1])

    return _call(flat_body, name=name, out_shape=list(out_shapes), in_specs=[anyspec] * nin,
                 out_specs=[anyspec] * len(out_shapes),
                 scratch_shapes=[pltpu.SemaphoreType.DMA((n_sems,)), pltpu.SemaphoreType.DMA((n_sems,))])(*operands)


def _remote(src, dst, send_sems, recv_sems, k, to):
    return pltpu.make_async_remote_copy(src_ref=src, dst_ref=dst, send_sem=send_sems.at[k], recv_sem=recv_sems.at[k],
                                        device_id=to, device_id_type=MESH)


def _gather_chips_split(xs, name):
    nw = len(xs)

    def body(x_refs, o_refs, send_sems, recv_sems):
        xi, yi, ci = _here()
        me = 2 * xi + yi
        sib = (xi, yi, 1 - ci)
        chips = [(1 - xi, yi), (xi, 1 - yi), (1 - xi, 1 - yi)]
        sends = []
        for i in range(nw):
            for k, (px, py) in enumerate(chips):
                sends.append(_remote(x_refs[i].at[ci], o_refs[i].at[me, ci], send_sems, recv_sems, 3 * i + k,
                                     (px, py, ci)))
        for cp in sends:
            cp.start()
        passed = []
        for i in range(nw):
            for k, (px, py) in enumerate(chips):
                landed = o_refs[i].at[2 * px + py, ci]
                _remote(landed, landed, send_sems, recv_sems, 3 * i + k, (px, py, ci)).wait_recv()
                fwd = _remote(landed, landed, send_sems, recv_sems, 3 * (nw + i) + k, sib)
                fwd.start()
                passed.append(fwd)
        for i in range(nw):
            for k, (px, py) in enumerate(chips):
                other = o_refs[i].at[2 * px + py, 1 - ci]
                _remote(other, other, send_sems, recv_sems, 3 * (nw + i) + k, sib).wait_recv()
        for cp in sends + passed:
            cp.wait_send()

    return _multi_comm_call(body, name, [_sds((4,) + x.shape, x.dtype) for x in xs], 6 * nw, xs)


GATHER_COLLECTIVE_ID = 1


def _gather_chips_split_async(xs, name, collective_id):
    nw = len(xs)
    x_refs = [jax.new_ref(x, memory_space=pltpu.MemorySpace.HBM) for x in xs]
    o_refs = [jax.empty_ref(_sds((4,) + x.shape, x.dtype), memory_space=pltpu.MemorySpace.HBM) for x in xs]

    @pl.kernel(mesh=plsc.ScalarSubcoreMesh(axis_name="sequencer", num_cores=1), name=name,
               scratch_types=(pltpu.SemaphoreType.DMA((6 * nw,)), pltpu.SemaphoreType.DMA((6 * nw,))),
               compiler_params=pltpu.CompilerParams(collective_id=collective_id))
    def launch(send_sems, recv_sems):
        xi, yi, ci = _here()
        me = 2 * xi + yi
        sib = (xi, yi, 1 - ci)
        chips = [(1 - xi, yi), (xi, 1 - yi), (1 - xi, 1 - yi)]
        barrier = pltpu.get_barrier_semaphore()
        for peer in [sib] + [(px, py, ci) for px, py in chips]:
            pl.semaphore_signal(barrier, inc=1, device_id=peer, device_id_type=MESH)
        pl.semaphore_wait(barrier, 4)
        sends = []
        for i in range(nw):
            for k, (px, py) in enumerate(chips):
                sends.append(_remote(x_refs[i].at[ci], o_refs[i].at[me, ci], send_sems, recv_sems, 3 * i + k,
                                     (px, py, ci)))
        for cp in sends:
            cp.start()
        passed = []
        for i in range(nw):
            for k, (px, py) in enumerate(chips):
                landed = o_refs[i].at[2 * px + py, ci]
                _remote(landed, landed, send_sems, recv_sems, 3 * i + k, (px, py, ci)).wait_recv()
                fwd = _remote(landed, landed, send_sems, recv_sems, 3 * (nw + i) + k, sib)
                fwd.start()
                passed.append(fwd)
        for i in range(nw):
            for k, (px, py) in enumerate(chips):
                other = o_refs[i].at[2 * px + py, 1 - ci]
                _remote(other, other, send_sems, recv_sems, 3 * (nw + i) + k, sib).wait_recv()
        for cp in sends + passed:
            cp.wait_send()

    launch()
    return [o[...] for o in o_refs]


def _swap_sibling_half(gs):
    def body(g_refs, o_refs, send_sems, recv_sems):
        xi, yi, ci = _here()
        cps = [_remote(g.at[:, 1 - ci], o, send_sems, recv_sems, i, (xi, yi, 1 - ci))
               for i, (g, o) in enumerate(zip(g_refs, o_refs))]
        for cp in cps:
            cp.start()
        for cp in cps:
            cp.wait()

    return _multi_comm_call(body, "swap_sibling_half", [_sds((g.shape[0],) + g.shape[2:], g.dtype) for g in gs],
                            len(gs), gs)


def _scatter_chips(gps):
    nw = len(gps)

    def body(g_refs, o_refs, send_sems, recv_sems):
        xi, yi, ci = _here()
        me = 2 * xi + yi
        chips = [(1 - xi, yi), (xi, 1 - yi), (1 - xi, 1 - yi)]
        sends = [_remote(g_refs[i].at[2 * px + py], o_refs[i].at[me], send_sems, recv_sems, 3 * i + k, (px, py, ci))
                 for i in range(nw) for k, (px, py) in enumerate(chips)]
        for cp in sends:
            cp.start()
        for i in range(nw):
            for k, (px, py) in enumerate(chips):
                slot = o_refs[i].at[2 * px + py]
                _remote(slot, slot, send_sems, recv_sems, 3 * i + k, (px, py, ci)).wait_recv()
        for cp in sends:
            cp.wait_send()

    return _multi_comm_call(body, "scatter_chips", [_sds(g.shape, g.dtype) for g in gps], 3 * nw, gps)


SCATTER_COLLECTIVE_ID = 2


def _scatter_chips_async(gps, name, collective_id):
    nw = len(gps)
    g_refs = [jax.new_ref(g, memory_space=pltpu.MemorySpace.HBM) for g in gps]
    o_refs = [jax.empty_ref(_sds(g.shape, g.dtype), memory_space=pltpu.MemorySpace.HBM) for g in gps]

    @pl.kernel(mesh=plsc.ScalarSubcoreMesh(axis_name="sequencer", num_cores=1), name=name,
               scratch_types=(pltpu.SemaphoreType.DMA((3 * nw,)), pltpu.SemaphoreType.DMA((3 * nw,))),
               compiler_params=pltpu.CompilerParams(collective_id=collective_id))
    def launch(send_sems, recv_sems):
        xi, yi, ci = _here()
        me = 2 * xi + yi
        chips = [(1 - xi, yi), (xi, 1 - yi), (1 - xi, 1 - yi)]
        barrier = pltpu.get_barrier_semaphore()
        for px, py in chips:
            pl.semaphore_signal(barrier, inc=1, device_id=(px, py, ci), device_id_type=MESH)
        pl.semaphore_wait(barrier, 3)
        sends = [_remote(g_refs[i].at[2 * px + py], o_refs[i].at[me], send_sems, recv_sems, 3 * i + k, (px, py, ci))
                 for i in range(nw) for k, (px, py) in enumerate(chips)]
        for cp in sends:
            cp.start()
        for i in range(nw):
            for k, (px, py) in enumerate(chips):
                slot = o_refs[i].at[2 * px + py]
                _remote(slot, slot, send_sems, recv_sems, 3 * i + k, (px, py, ci)).wait_recv()
        for cp in sends:
            cp.wait_send()

    launch()
    return [o[...] for o in o_refs]


def _share_sibling(rs):
    def body(r_refs, o_refs, send_sems, recv_sems):
        xi, yi, ci = _here()
        cps = [_remote(r, o, send_sems, recv_sems, i, (xi, yi, 1 - ci)) for i, (r, o) in enumerate(zip(r_refs, o_refs))]
        for cp in cps:
            cp.start()
        for cp in cps:
            cp.wait()

    return _multi_comm_call(body, "share_sibling", [_sds(r.shape, r.dtype) for r in rs], len(rs), rs)


def _flat_pack(arrs, dtype, row_mult):
    flat = jnp.concatenate([a.astype(dtype).reshape(-1) for a in arrs])
    unit = FLAT_W * row_mult
    npad = -(-flat.shape[0] // unit) * unit
    return jnp.pad(flat, (0, npad - flat.shape[0])).reshape(npad // FLAT_W, FLAT_W)


def _flat_unpack(flat2d, shapes):
    flat = flat2d.reshape(-1)
    outs, off = [], 0
    for s in shapes:
        size = int(np.prod(s))
        outs.append(flat[off:off + size].reshape(s))
        off += size
    return outs


def _split_cols(a, widths):
    outs, off = [], 0
    for wd in widths:
        outs.append(lax.slice_in_dim(a, off, off + wd, axis=1))
        off += wd
    return outs


def _s5_params(lam_re, lam_im, log_step, b_re, b_im, c_re, c_im, d_skip):
    g, p = lam_re.shape
    hs = b_re.shape[2]
    gt = S5_GROUP_TILE
    jn = g // gt
    lam_re = jnp.minimum(lam_re, -1e-4)
    step = jnp.exp(log_step)[:, None]
    mag = jnp.exp(lam_re * step)
    ab_re = mag * jnp.cos(lam_im * step)
    ab_im = mag * jnp.sin(lam_im * step)
    den = lam_re * lam_re + lam_im * lam_im
    f_re = ((ab_re - 1.0) * lam_re + ab_im * lam_im) / den
    f_im = (ab_im * lam_re - (ab_re - 1.0) * lam_im) / den
    bb_re = f_re[..., None] * b_re - f_im[..., None] * b_im
    bb_im = f_re[..., None] * b_im + f_im[..., None] * b_re
    a_l = jnp.concatenate([ab_re.reshape(jn, gt * p), ab_im.reshape(jn, gt * p)], axis=1).reshape(1, jn * 2 * gt * p)
    eye = jnp.eye(gt, dtype=F32)

    def blockdiag(m):
        return jnp.einsum('jahp,ab->jahbp', m.reshape(jn, gt, hs, p), eye).reshape(jn, gt * hs, gt * p)

    b_blk = jnp.concatenate([blockdiag(bb_re.transpose(0, 2, 1)), blockdiag(bb_im.transpose(0, 2, 1))], axis=2)
    c_blk = jnp.concatenate([blockdiag(c_re), blockdiag(-c_im)], axis=2)
    return a_l, b_blk, c_blk, d_skip.reshape(1, g * hs)


def _s5_scan_consts(a_l, cw):
    jn, hr = a_l.shape[1] // cw, cw // 2 // LANES
    a4 = a_l.reshape(jn, 2, hr, LANES)
    are, aim = a4[:, 0], a4[:, 1]
    flat = lambda u, v: jnp.concatenate([u, v], axis=1).reshape(jn * 2 * hr, LANES)
    return flat(are, are), flat(-aim, aim), flat(aim, -aim)


def _s5_da_lanes(p, q, cw):
    jn, hr = p.shape[0] * LANES // cw, cw // 2 // LANES
    p4, q4 = p.reshape(jn, 2, hr, LANES), q.reshape(jn, 2, hr, LANES)
    da_re = (p4[:, 0] + p4[:, 1]).reshape(jn, cw // 2)
    da_im = (q4[:, 1] - q4[:, 0]).reshape(jn, cw // 2)
    return jnp.concatenate([da_re, da_im], axis=1).reshape(1, jn * cw)


def _layer_dims(p):
    d_model = p['w_out'].shape[1]
    wa = p['proj_a'].shape[0]
    h = p['gdn_a_log'].shape[0]
    wb = p['proj_b'].shape[0]
    wc = p['proj_c'].shape[0]
    hm = p['m2_a_log'].shape[0]
    cdim = p['m2_conv_w'].shape[1]
    width = dict(zip(PROJ_ORDER, (3 * wa, wa, h, h, wb, wb, wc, cdim, hm, d_model, d_model, d_model)))
    n_in = sum(width.values())
    return width, n_in, -(-n_in // LANES) * LANES


PROJ_ORDER = ('qkv', 'az', 'braw', 'araw', 'su', 'sgate', 'cz', 'cxbc', 'cdt', 'ma', 'mb', 'mc')
WORK_ORDER = ('qkv', 'az', 'cz', 'su', 'sgate', 'cxbc', 'ma', 'mb', 'mc', 'braw', 'araw', 'cdt')
ROW_COL_TILE = 512


def _reorder_cols(a, width, src_order, dst_order, n_out):
    off, o = {}, 0
    for n in src_order:
        off[n] = o
        o += width[n]
    parts = [lax.slice_in_dim(a, off[n], off[n] + width[n], axis=a.ndim - 1) for n in dst_order]
    used = sum(width[n] for n in dst_order)
    if n_out > used:
        parts.append(jnp.zeros(a.shape[:-1] + (n_out - used,), a.dtype))
    return jnp.concatenate(parts, axis=a.ndim - 1)


def _layer_fwd(x, p):
    width, n_in, n_pad = _layer_dims(p)
    sv = {'x': x}
    h = _rowwise(_f_rms, [x], [p['norm_w'][None]], [x.shape[1]], [BF16], "rms_fwd")[0]
    w_in = _reorder_cols(p['w_in'], width, PROJ_ORDER, WORK_ORDER, n_pad)
    proj = _matmul(h, w_in, 'nn', F32, "in_proj", tn=IN_PROJ_TILE)
    wins, off = {}, 0
    for n in WORK_ORDER:
        wins[n] = Win(proj, off, width[n])
        off += width[n]
    qkv, az, cz, su, sgate, cxbc, ma, mb, mc = (wins[n] for n in WORK_ORDER[:9])
    braw, araw, cdt = (lax.slice_in_dim(proj, wins[n].off, wins[n].off + width[n], axis=1) for n in WORK_ORDER[9:])
    sv.update(h=h, w_in=w_in, qkv=qkv, az=az, braw=braw, araw=araw, su=su, sgate=sgate, cz=cz, cxbc=cxbc, cdt=cdt,
              ma=ma, mb=mb, mc=mc)
    gb0 = jnp.zeros((1, qkv.shape[1]), F32)
    sqkv = _conv_fwd(qkv, p['gdn_conv_w'], gb0, "gdn_conv_fwd")
    ya, ssa, tsa = _gdn_chunks_fwd(sqkv, az, braw, araw, p['gdn_a_log'][None], p['gdn_dt_bias'][None],
                                   p['gdn_norm_w'][None])
    sv.update(sqkv=sqkv, ssa=ssa, tsa=tsa, ya=ya)
    s5_in = tuple(p[k] for k in ('s5_lam_re', 's5_lam_im', 's5_log_step', 's5_b_re', 's5_b_im', 's5_c_re', 's5_c_im',
                                 's5_d'))
    (a_l, b_blk, c_blk, d_l), s5_vjp = jax.vjp(_s5_params, *s5_in)
    cw = b_blk.shape[2]
    aa3, bb3, bb3_conj = _s5_scan_consts(a_l, cw)
    bu = _bd_expand(su, b_blk, "s5_bu")
    s = _s5_scan(bu, aa3, bb3, False, "s5_scan_fwd")
    ymm = _bd_reduce(s, c_blk, F32, "s5_out")
    yg = _rowwise(_f_s5_post1, [ymm, su], [d_l], [su.shape[1]], [F32], "s5_post1_fwd", col_tile=ROW_COL_TILE)[0]
    tt = _matmul(yg, p['s5_glu_w'], 'nn', F32, "s5_glu")
    yb = _rowwise(_f_s5_post2, [yg, tt, sgate], [p['s5_glu_b'][None]], [su.shape[1]], [F32], "s5_post2_fwd",
                  col_tile=ROW_COL_TILE)[0]
    sv.update(aa3=aa3, bb3_conj=bb3_conj, b_blk=b_blk, c_blk=c_blk, d_l=d_l, s5_vjp=s5_vjp, s=s, ymm=ymm, yg=yg, tt=tt,
              yb=yb, cw=cw)
    sxbc = _conv_fwd(cxbc, p['m2_conv_w'], p['m2_conv_b'][None], "m2_conv_fwd")
    yc, ssc = _m2_chunks_fwd(sxbc, cz, cdt, p['m2_a_log'][None], p['m2_dt_bias'][None], p['m2_d'][None],
                             p['m2_norm_w'][None])
    sv.update(sxbc=sxbc, ssc=ssc, yc=yc)
    pa = _matmul(ya, p['proj_a'], 'nn', F32, "proj_a")
    pb = _matmul(yb, p['proj_b'], 'nn', F32, "proj_b")
    pc = _matmul(yc, p['proj_c'], 'nn', F32, "proj_c")
    merged = _rowwise(_f_merge, [ma, mb, mc, pa, pb, pc], [], [x.shape[1]], [BF16], "merge_fwd",
                      col_tile=ROW_COL_TILE)[0]
    x_next = _matmul(merged, p['w_out'], 'nn', F32, "out_proj", add=x)
    sv.update(pa=pa, pb=pb, pc=pc, merged=merged)
    return x_next, sv


def _layer_bwd(dx_out, p, sv):
    width, n_in, n_pad = _layer_dims(p)
    g = {}
    dmerged = _matmul(dx_out, p['w_out'], 'nt', F32, "out_proj_dx")
    g['w_out'] = _matmul(sv['merged'], dx_out, 'tn', F32, "out_proj_dw")
    dma, dmb, dmc, dpa, dpb, dpc = _rowwise_bwd(
        _f_merge, [sv['ma'], sv['mb'], sv['mc'], sv['pa'], sv['pb'], sv['pc']], [], [dmerged], [BF16] * 6,
        "merge_bwd", col_tile=ROW_COL_TILE)
    dya = _matmul(dpa, p['proj_a'], 'nt', F32, "proj_a_dx")
    dyb = _matmul(dpb, p['proj_b'], 'nt', F32, "proj_b_dx")
    dyc = _matmul(dpc, p['proj_c'], 'nt', F32, "proj_c_dx")
    g['proj_a'] = _matmul(sv['ya'], dpa, 'tn', F32, "proj_a_dw")
    g['proj_b'] = _matmul(sv['yb'], dpb, 'tn', F32, "proj_b_dw")
    g['proj_c'] = _matmul(sv['yc'], dpc, 'tn', F32, "proj_c_dw")
    alog, dtb, gnw = p['gdn_a_log'][None], p['gdn_dt_bias'][None], p['gdn_norm_w'][None]
    dsq, daz, db3, da3, dalog, ddtb, dgnw = _gdn_chunks_bwd(sv['sqkv'], sv['az'], sv['braw'], sv['araw'], alog, dtb, gnw,
                                                            sv['ssa'], sv['tsa'], dya)
    gb0 = jnp.zeros((1, sv['qkv'].shape[1]), F32)
    dqkv, g['gdn_conv_w'], _ = _conv_bwd(sv['qkv'], p['gdn_conv_w'], gb0, dsq, "gdn_conv_bwd")
    dbraw, daraw = jnp.sum(db3, axis=0), jnp.sum(da3, axis=0)
    g.update(gdn_a_log=dalog[0], gdn_dt_bias=ddtb[0], gdn_norm_w=dgnw[0])
    dsx, dcz, dcdt, dmalog, dmdtb, dmdsk, dmnw = _m2_chunks_bwd(
        sv['sxbc'], sv['cz'], sv['cdt'], p['m2_a_log'][None], p['m2_dt_bias'][None], p['m2_d'][None],
        p['m2_norm_w'][None], sv['ssc'], dyc)
    dcxbc, g['m2_conv_w'], dconvb = _conv_bwd(sv['cxbc'], p['m2_conv_w'], p['m2_conv_b'][None], dsx, "m2_conv_bwd")
    g.update(m2_conv_b=dconvb[0], m2_a_log=dmalog[0], m2_dt_bias=dmdtb[0], m2_d=dmdsk[0], m2_norm_w=dmnw[0])
    dyg1, dtt, dsgate, dglub = _rowwise_bwd(_f_s5_post2, [sv['yg'], sv['tt'], sv['sgate']], [p['s5_glu_b'][None]],
                                            [dyb], [F32, BF16, BF16], "s5_post2_bwd", col_tile=ROW_COL_TILE)
    dyg = _matmul(dtt, p['s5_glu_w'], 'nt', F32, "s5_glu_dx", add=dyg1)
    g['s5_glu_w'] = _matmul(sv['yg'], dtt, 'tn', F32, "s5_glu_dw")
    g['s5_glu_b'] = dglub[0]
    dymm, dsu1, dd_l = _rowwise_bwd(_f_s5_post1, [sv['ymm'], sv['su']], [sv['d_l']], [dyg], [BF16, F32],
                                    "s5_post1_bwd", col_tile=ROW_COL_TILE)
    gy = _bd_expand(dymm, sv['c_blk'], "s5_out_dx")
    ds = _s5_scan(gy, sv['aa3'], sv['bb3_conj'], True, "s5_scan_bwd")
    da_l = _s5_da_lanes(*_s5_da(ds, sv['s']), sv['cw'])
    dsu = _bd_reduce(ds, sv['b_blk'], BF16, "s5_bu_dx", add=dsu1)
    ka = sv['b_blk'].shape[1]
    db_blk = _bd_outer(sv['su'], ds, ka, "s5_bu_dw")
    dc_blk = _bd_outer(dymm, sv['s'], ka, "s5_out_dw")
    for k, v in zip(('s5_lam_re', 's5_lam_im', 's5_log_step', 's5_b_re', 's5_b_im', 's5_c_re', 's5_c_im', 's5_d'),
                    sv['s5_vjp']((da_l, db_blk, dc_blk, dd_l))):
        g[k] = v
    small = jnp.concatenate([dbraw, daraw, dcdt], axis=1).astype(BF16)
    small = jnp.pad(small, ((0, 0), (0, n_pad - n_in + sum(width[n] for n in WORK_ORDER[9:]) - small.shape[1])))
    dproj = _concat_cols([dqkv, daz, dcz, dsu, dsgate, dcxbc, dma, dmb, dmc, small], "concat_dproj")
    dh = _matmul(dproj, sv['w_in'], 'nt', F32, "in_proj_dx", tm=2048, tk=IN_PROJ_TILE)
    g['w_in'] = _reorder_cols(_matmul(sv['h'], dproj, 'tn', F32, "in_proj_dw", tn=IN_PROJ_TILE), width, WORK_ORDER,
                              PROJ_ORDER, n_in)
    dx, dnw = _rowwise_bwd(_f_rms, [sv['x']], [p['norm_w'][None]], [dh], [F32], "rms_bwd", addend=dx_out)
    g['norm_w'] = dnw[0]
    return dx, g


INPUT_NAMES = (['x'] + WEIGHT_NAMES + ['loss_target'] + ['m_' + n for n in WEIGHT_NAMES]
               + ['v_' + n for n in WEIGHT_NAMES])


def _step(d):
    xi, yi, ci = _here()
    me = 2 * xi + yi
    depth = d['norm_w'].shape[0]
    big, ssm = list(BIG), list(SHARDED_SMALL)
    nsh = 4
    full, gathered = {}, {}
    gots = None
    for first, stop in ((0, 1), (1, depth)):
        halves = [d[n][first:stop].astype(BF16).reshape(2, -1, d[n].shape[-1]) for n in big]
        if first == 0:
            gots = _gather_chips_split(halves, "gather_weights")
        else:
            halves, gots = lax.optimization_barrier((halves, gots))
            gots = _gather_chips_split_async(halves, "gather_weights_async", GATHER_COLLECTIVE_ID)
        for n, hv, got in zip(big, halves, gots):
            got = lax.dynamic_update_slice(got, hv[None], (me, 0, 0, 0))
            gathered[n, first] = got.reshape((nsh, stop - first) + d[n].shape[1:])
    cg = _gather_chips(_flat_pack([d[n] for n in ssm], F32, 8), "gather_conv_weights")
    parts = [_flat_unpack(cg[j], [d[n].shape for n in ssm]) for j in range(nsh)]
    for i, n in enumerate(ssm):
        full[n] = jnp.concatenate([parts[j][i] for j in range(nsh)], axis=SHARDED_SMALL[n])
    layer_names = [n for n in WEIGHT_NAMES if n != 'final_norm_w']

    def layer_params(l):
        p = {n: (full[n][l] if n in full else d[n][l]) for n in layer_names if n not in BIG}
        for n in big:
            got = gathered[n, 0][:, 0] if l == 0 else gathered[n, 1][:, l - 1]
            p[n] = jnp.concatenate([got[j] for j in range(nsh)], axis=BIG[n] - 1)
        return p

    x = d['x'][0]
    saved, params = [], []
    for l in range(depth):
        params.append(layer_params(l))
        x, sv = _layer_fwd(x, params[l])
        saved.append(sv)
    loss11, dx, dfw = _loss_and_grad(x, d['loss_target'][0], d['final_norm_w'][None])
    loss = lax.psum(loss11[0, 0], ("x", "y", "c"))
    def shard(a, axis, j):
        wd = a.shape[axis] // nsh
        return lax.slice_in_dim(a, j * wd, (j + 1) * wd, axis=axis)

    c_idx = jnp.reshape(ci, (1,)).astype(jnp.int32)
    me_idx = jnp.reshape(me, (1,)).astype(jnp.int32)
    grads, own, got = [None] * depth, [None] * depth, [None] * depth
    for l in reversed(range(depth)):
        dx, grads[l] = _layer_bwd(dx, params[l], saved[l])
        g4 = [jnp.stack([shard(grads[l][n], BIG[n] - 1, j) for j in range(nsh)]).reshape(nsh, 2, -1, d[n].shape[-1])
              for n in big]
        pairs = [_add_my_half(g, r, c_idx, "add_my_half_" + n) for n, g, r in zip(big, g4, _swap_sibling_half(g4))]
        own[l] = [pf for pf, _ in pairs]
        wire = [pb for _, pb in pairs]
        got[l] = (_scatter_chips(wire) if l == 0
                  else _scatter_chips_async(wire, "scatter_chips_async_%d" % l, SCATTER_COLLECTIVE_ID + l))
    gfull = {n: jnp.stack([grads[l][n] for l in range(depth)]) for n in layer_names if n not in BIG}
    gfull['final_norm_w'] = dfw[0]
    got, dx = lax.optimization_barrier((got, dx))
    mine = [_sum_chips(pf, gt, me_idx, "sum_chips_" + n) for l in range(depth)
            for n, pf, gt in zip(big, own[l], got[l])]
    theirs = _share_sibling(mine)
    out = {}
    for i, n in enumerate(big):
        layers = []
        for l in range(depth):
            mn, th = mine[l * len(big) + i], theirs[l * len(big) + i]
            layers.append(jnp.where(ci == 0, jnp.stack([mn, th]), jnp.stack([th, mn])))
        g2 = jnp.stack(layers).reshape(-1, d[n].shape[-1])
        w2, m2, v2 = (d[pre + n].reshape(g2.shape) for pre in ('', 'm_', 'v_'))
        dl, nm, nv = _adamw(w2, g2, m2, v2, "adamw_" + n)
        for key, arr in (('grad_', g2), ('delta_', dl), ('new_m_', nm), ('new_v_', nv)):
            out[key + n] = arr.reshape(d[n].shape)
    small = [n for n in WEIGHT_NAMES if n not in BIG]
    sshapes = [gfull[n].shape for n in small]
    gsm = _sum_parts(_gather_all(_flat_pack([gfull[n] for n in small], F32, 8), "gather_small_grads"), "sum_devices")
    gs = dict(zip(small, _flat_unpack(gsm, sshapes)))
    for n in ssm:
        wd = d[n].shape[SHARDED_SMALL[n]]
        gs[n] = lax.dynamic_slice_in_dim(gs[n], me * wd, wd, axis=SHARDED_SMALL[n])
    lshapes = [d[n].shape for n in small]
    wps, gps, mps, vps = (_flat_pack(arrs, F32, 16) for arrs in (
        [d[n] for n in small], [gs[n] for n in small], [d['m_' + n] for n in small], [d['v_' + n] for n in small]))
    dl, nm, nv = _adamw(wps, gps, mps, vps, "adamw_small")
    for key, arr in (('grad_', gps), ('delta_', dl), ('new_m_', nm), ('new_v_', nv)):
        for n, a in zip(small, _flat_unpack(arr, lshapes)):
            out[key + n] = a
    res = [loss, dx[None]]
    for key in ('grad_', 'delta_', 'new_m_', 'new_v_'):
        res += [out[key + n] for n in WEIGHT_NAMES]
    return tuple(res)


def kernel(x, norm_w, w_in, gdn_conv_w, gdn_a_log, gdn_dt_bias, gdn_norm_w, s5_lam_re, s5_lam_im, s5_log_step, s5_b_re, s5_b_im, s5_c_re, s5_c_im, s5_d, s5_glu_w, s5_glu_b, m2_conv_w, m2_conv_b, m2_a_log, m2_dt_bias, m2_d, m2_norm_w, proj_a, proj_b, proj_c, w_out, final_norm_w, loss_target, m_norm_w, m_w_in, m_gdn_conv_w, m_gdn_a_log, m_gdn_dt_bias, m_gdn_norm_w, m_s5_lam_re, m_s5_lam_im, m_s5_log_step, m_s5_b_re, m_s5_b_im, m_s5_c_re, m_s5_c_im, m_s5_d, m_s5_glu_w, m_s5_glu_b, m_m2_conv_w, m_m2_conv_b, m_m2_a_log, m_m2_dt_bias, m_m2_d, m_m2_norm_w, m_proj_a, m_proj_b, m_proj_c, m_w_out, m_final_norm_w, v_norm_w, v_w_in, v_gdn_conv_w, v_gdn_a_log, v_gdn_dt_bias, v_gdn_norm_w, v_s5_lam_re, v_s5_lam_im, v_s5_log_step, v_s5_b_re, v_s5_b_im, v_s5_c_re, v_s5_c_im, v_s5_d, v_s5_glu_w, v_s5_glu_b, v_m2_conv_w, v_m2_conv_b, v_m2_a_log, v_m2_dt_bias, v_m2_d, v_m2_norm_w, v_proj_a, v_proj_b, v_proj_c, v_w_out, v_final_norm_w):
    args = (x, norm_w, w_in, gdn_conv_w, gdn_a_log, gdn_dt_bias, gdn_norm_w, s5_lam_re, s5_lam_im, s5_log_step, s5_b_re, s5_b_im, s5_c_re, s5_c_im, s5_d, s5_glu_w, s5_glu_b, m2_conv_w, m2_conv_b, m2_a_log, m2_dt_bias, m2_d, m2_norm_w, proj_a, proj_b, proj_c, w_out, final_norm_w, loss_target, m_norm_w, m_w_in, m_gdn_conv_w, m_gdn_a_log, m_gdn_dt_bias, m_gdn_norm_w, m_s5_lam_re, m_s5_lam_im, m_s5_log_step, m_s5_b_re, m_s5_b_im, m_s5_c_re, m_s5_c_im, m_s5_d, m_s5_glu_w, m_s5_glu_b, m_m2_conv_w, m_m2_conv_b, m_m2_a_log, m_m2_dt_bias, m_m2_d, m_m2_norm_w, m_proj_a, m_proj_b, m_proj_c, m_w_out, m_final_norm_w, v_norm_w, v_w_in, v_gdn_conv_w, v_gdn_a_log, v_gdn_dt_bias, v_gdn_norm_w, v_s5_lam_re, v_s5_lam_im, v_s5_log_step, v_s5_b_re, v_s5_b_im, v_s5_c_re, v_s5_c_im, v_s5_d, v_s5_glu_w, v_s5_glu_b, v_m2_conv_w, v_m2_conv_b, v_m2_a_log, v_m2_dt_bias, v_m2_d, v_m2_norm_w, v_proj_a, v_proj_b, v_proj_c, v_w_out, v_final_norm_w)
    return _step(dict(zip(INPUT_NAMES, args)))
```

```python
import functools
import math
from typing import NamedTuple

import jax
import jax.numpy as jnp
import numpy as np
from jax import lax
from jax.experimental import pallas as pl
from jax.experimental.pallas import tpu as pltpu
from jax.experimental.pallas import tpu_sc as plsc

F32 = jnp.float32
BF16 = jnp.bfloat16
HI = lax.Precision.HIGH
MESH = pl.DeviceIdType.MESH

CHUNK = 64
CONV_K = 4
NORM_EPS = 1e-6
GDN_HEAD_DIM = 128
M2_HEAD_DIM = 64
M2_STATE = 128
M2_GROUPS = 4
S5_GROUP_TILE = 8
ADAM_LR = 0.001
ADAM_B1 = 0.9
ADAM_B2 = 0.999
ADAM_EPS = 1e-08
ADAM_WD = 0.01
ADAM_STEP = 10
LANES = 128
SUBLANES = 8
VMEM_LIMIT_BYTES = 56 * 1024 * 1024

WEIGHT_NAMES = ['norm_w', 'w_in', 'gdn_conv_w', 'gdn_a_log', 'gdn_dt_bias', 'gdn_norm_w', 's5_lam_re', 's5_lam_im',
                's5_log_step', 's5_b_re', 's5_b_im', 's5_c_re', 's5_c_im', 's5_d', 's5_glu_w', 's5_glu_b',
                'm2_conv_w', 'm2_conv_b', 'm2_a_log', 'm2_dt_bias', 'm2_d', 'm2_norm_w', 'proj_a', 'proj_b',
                'proj_c', 'w_out', 'final_norm_w']
BIG = {'w_in': 2, 'proj_a': 2, 'proj_b': 2, 'proj_c': 2, 'w_out': 1, 's5_glu_w': 1}
SHARDED_SMALL = {'gdn_conv_w': 2, 'm2_conv_w': 2}


def _call(body, *, name, out_shape, grid=None, in_specs=None, out_specs=None, scratch_shapes=(), semantics=None,
          num_scalar_prefetch=None):
    params = dict(vmem_limit_bytes=VMEM_LIMIT_BYTES)
    if semantics is not None:
        params['dimension_semantics'] = semantics
    kw = dict(name=name, out_shape=out_shape, compiler_params=pltpu.CompilerParams(**params))
    if num_scalar_prefetch is not None:
        kw['grid_spec'] = pltpu.PrefetchScalarGridSpec(num_scalar_prefetch=num_scalar_prefetch, grid=grid,
                                                       in_specs=in_specs, out_specs=out_specs,
                                                       scratch_shapes=scratch_shapes)
    else:
        if grid is not None:
            kw['grid'] = grid
        if in_specs is not None:
            kw['in_specs'] = in_specs
        if out_specs is not None:
            kw['out_specs'] = out_specs
        if scratch_shapes:
            kw['scratch_shapes'] = scratch_shapes
    return pl.pallas_call(body, **kw)


def _tile(n, target, unit):
    if n <= target:
        return n
    t = (target // unit) * unit
    while t >= unit:
        if n % t == 0:
            return t
        t -= unit
    raise ValueError(f"no tile for {n} (unit {unit}, target {target})")


def _sds(shape, dtype):
    return jax.ShapeDtypeStruct(tuple(shape), dtype)


def _sigmoid(x):
    return jax.nn.sigmoid(x)


def _silu(x):
    return x * jax.nn.sigmoid(x)


def _softplus(x):
    return jnp.maximum(x, 0.0) + jnp.log(1.0 + jnp.exp(-jnp.abs(x)))


def _gelu_tanh(x):
    return 0.5 * x * (1.0 + jnp.tanh(math.sqrt(2.0 / math.pi) * (x + 0.044715 * (x * x * x))))


def _dot(a, b, dims, prec=None):
    return lax.dot_general(a, b, (dims, ((), ())), precision=prec, preferred_element_type=F32)


def _nn(a, b, prec=None):
    return _dot(a, b, ((1,), (0,)), prec)


def _nt(a, b, prec=None):
    return _dot(a, b, ((1,), (1,)), prec)


def _tn(a, b, prec=None):
    return _dot(a, b, ((0,), (0,)), prec)


IN_PROJ_TILE = 1664
MATMUL_TILES = {'nn': (1024, 1024, 2048), 'nt': (1024, 1024, 2048), 'tn': (1024, 1024, 2048)}


def _matmul(a, b, mode, out_dtype, name, tm=None, tn=None, tk=None, add=None):
    tm, tn, tk = (t if t is not None else dflt for t, dflt in zip((tm, tn, tk), MATMUL_TILES[mode]))
    if mode == 'nn':
        (m, k), (k2, n) = a.shape, b.shape
    elif mode == 'nt':
        (m, k), (n, k2) = a.shape, b.shape
    else:
        (k, m), (k2, n) = a.shape, b.shape
    assert k == k2, (a.shape, b.shape, mode)
    tm = _tile(m, tm, LANES if mode == 'tn' else SUBLANES)
    tn = _tile(n, tn, LANES)
    tk = _tile(k, tk, LANES if mode != 'tn' else SUBLANES * 2)
    nk = k // tk
    dims = {'nn': ((1,), (0,)), 'nt': ((1,), (1,)), 'tn': ((0,), (0,))}[mode]

    def body(*refs):
        a_ref, b_ref = refs[:2]
        o_ref, acc_ref = refs[-2:]
        kk = pl.program_id(2)

        @pl.when(kk == 0)
        def _():
            acc_ref[...] = jnp.zeros_like(acc_ref)

        acc_ref[...] += _dot(a_ref[...].astype(BF16), b_ref[...].astype(BF16), dims)

        @pl.when(kk == nk - 1)
        def _():
            res = acc_ref[...]
            if add is not None:
                res = res + refs[2][...].astype(F32)
            o_ref[...] = res.astype(o_ref.dtype)

    a_spec = pl.BlockSpec((tk, tm), lambda i, j, kk: (kk, i)) if mode == 'tn' else pl.BlockSpec((tm, tk), lambda i, j, kk: (i, kk))
    b_spec = pl.BlockSpec((tn, tk), lambda i, j, kk: (j, kk)) if mode == 'nt' else pl.BlockSpec((tk, tn), lambda i, j, kk: (kk, j))
    o_spec = pl.BlockSpec((tm, tn), lambda i, j, kk: (i, j))
    ops = (a, b) if add is None else (a, b, add)
    return _call(body, name=name, out_shape=_sds((m, n), out_dtype), grid=(m // tm, n // tn, nk),
                 in_specs=[a_spec, b_spec] + ([] if add is None else [o_spec]), out_specs=o_spec,
                 scratch_shapes=[pltpu.VMEM((tm, tn), F32)], semantics=("parallel", "parallel", "arbitrary"))(*ops)


class Win(NamedTuple):
    arr: jax.Array
    off: int
    width: int

    @property
    def shape(self):
        return (self.arr.shape[0], self.width)


def _win(x):
    return x if isinstance(x, Win) else Win(x, 0, x.shape[1])


def _col_tile(wins, target):
    ct = (min(target, min(w.width for w in wins)) // LANES) * LANES
    while ct > LANES and any(w.width % ct or w.off % ct for w in wins):
        ct -= LANES
    assert all(w.width % ct == 0 and w.off % ct == 0 for w in wins), [(w.off, w.width) for w in wins]
    return ct


def _wspec(rows, ct, w, row_first=True):
    base = w.off // ct
    if row_first:
        return pl.BlockSpec((rows, ct), lambda i, j: (i, base + j))
    return pl.BlockSpec((rows, ct), lambda j, i: (i, base + j))


def _bd_expand(a, b, name, tm=1024):
    a = _win(a)
    t = a.shape[0]
    jn, ka, nb = b.shape
    r = nb // LANES
    tm = _tile(t, tm, SUBLANES)
    assert a.shape[1] == jn * ka and a.off % ka == 0
    abase = a.off // ka

    def body(a_ref, b_ref, o_ref):
        o_ref[...] = _nn(a_ref[...].astype(BF16), b_ref[...].astype(BF16)).reshape(tm, r, LANES)

    return _call(body, name=name, out_shape=_sds((t, jn * r, LANES), F32), grid=(t // tm, jn),
                 in_specs=[pl.BlockSpec((tm, ka), lambda i, j: (i, abase + j)),
                           pl.BlockSpec((None, ka, nb), lambda i, j: (j, 0, 0))],
                 out_specs=pl.BlockSpec((tm, r, LANES), lambda i, j: (i, j, 0)),
                 semantics=("parallel", "parallel"))(a.arr, b)


def _bd_reduce(a3, b, out_dtype, name, tm=1024, add=None):
    t = a3.shape[0]
    jn, ka, nb = b.shape
    r = nb // LANES
    tm = _tile(t, tm, SUBLANES)

    def body(*refs):
        a_ref, b_ref, o_ref = refs[0], refs[1], refs[-1]
        res = _nt(a_ref[...].reshape(tm, nb).astype(BF16), b_ref[...].astype(BF16))
        if add is not None:
            res = res + refs[2][...].astype(F32)
        o_ref[...] = res.astype(o_ref.dtype)

    o_spec = pl.BlockSpec((tm, ka), lambda i, j: (i, j))
    ops = (a3, b) if add is None else (a3, b, add)
    return _call(body, name=name, out_shape=_sds((t, jn * ka), out_dtype), grid=(t // tm, jn),
                 in_specs=[pl.BlockSpec((tm, r, LANES), lambda i, j: (i, j, 0)),
                           pl.BlockSpec((None, ka, nb), lambda i, j: (j, 0, 0))] + ([] if add is None else [o_spec]),
                 out_specs=o_spec, semantics=("parallel", "parallel"))(*ops)


def _bd_outer(a, b3, ka, name, tk=1024):
    a = _win(a)
    t = a.shape[0]
    jn = a.shape[1] // ka
    r = b3.shape[1] // jn
    nb = r * LANES
    assert a.off % ka == 0
    abase = a.off // ka
    tk = _tile(t, tk, SUBLANES * 2)

    def body(a_ref, b_ref, o_ref):
        @pl.when(pl.program_id(1) == 0)
        def _():
            o_ref[...] = jnp.zeros_like(o_ref)

        o_ref[...] += _tn(a_ref[...].astype(BF16), b_ref[...].reshape(tk, nb).astype(BF16))

    return _call(body, name=name, out_shape=_sds((jn, ka, nb), F32), grid=(jn, t // tk),
                 in_specs=[pl.BlockSpec((tk, ka), lambda j, kk: (kk, abase + j)),
                           pl.BlockSpec((tk, r, LANES), lambda j, kk: (kk, j, 0))],
                 out_specs=pl.BlockSpec((None, ka, nb), lambda j, kk: (j, 0, 0)),
                 semantics=("parallel", "arbitrary"))(a.arr, b3)


def _concat_cols(parts, name, tb=256):
    t = parts[0].shape[0]
    tb = _tile(t, tb, SUBLANES * 2)
    widths = [p.shape[1] for p in parts]
    assert all(w % LANES == 0 for w in widths)

    def body(*refs):
        o_ref, off = refs[-1], 0
        for r, w in zip(refs[:-1], widths):
            o_ref[:, off:off + w] = r[...]
            off += w

    return _call(body, name=name, out_shape=_sds((t, sum(widths)), parts[0].dtype), grid=(t // tb,),
                 in_specs=[pl.BlockSpec((tb, w), lambda i: (i, 0)) for w in widths],
                 out_specs=pl.BlockSpec((tb, sum(widths)), lambda i: (i, 0)), semantics=("parallel",))(*parts)


def _rowwise_tiles(rows, tb, col_tile):
    rows = [_win(r) for r in rows]
    t = rows[0].shape[0]
    tb = _tile(t, tb, SUBLANES * 2)
    if col_tile is None:
        assert all(r.off % r.width == 0 for r in rows)
        return rows, tb, None, 1
    ct = _col_tile(rows, col_tile)
    tb = _tile(t, max(tb, BLOCK_BYTES // (4 * ct)), SUBLANES * 2)
    return rows, tb, ct, rows[0].width // ct


def _rowwise(fn, rows, params, out_widths, out_dtypes, name, tb=256, col_tile=None):
    rows, tb, ct, ncol = _rowwise_tiles(rows, tb, col_tile)
    t = rows[0].shape[0]
    nr, npar = len(rows), len(params)

    def body(*refs):
        ins = [r[...].astype(F32) for r in refs[:nr + npar]]
        outs = fn(*ins)
        for o_ref, o in zip(refs[nr + npar:], outs):
            o_ref[...] = o.astype(o_ref.dtype)

    in_specs = [_wspec(tb, ct or r.width, r) for r in rows]
    in_specs += [pl.BlockSpec((1, ct or p.shape[1]), lambda i, j: (0, j)) for p in params]
    out_shape = [_sds((t, w), d) for w, d in zip(out_widths, out_dtypes)]
    out_specs = [pl.BlockSpec((tb, ct or w), lambda i, j: (i, j)) for w in out_widths]
    return _call(body, name=name, out_shape=out_shape, grid=(t // tb, ncol), in_specs=in_specs, out_specs=out_specs,
                 semantics=("parallel", "parallel"))(*[r.arr for r in rows], *params)


def _rowwise_bwd(fn, rows, params, cts, row_grad_dtypes, name, tb=256, addend=None, col_tile=None):
    rows, tb, ct, ncol = _rowwise_tiles(rows, tb, col_tile)
    t = rows[0].shape[0]
    nr, npar, nc = len(rows), len(params), len(cts)
    keep = [i for i, d in enumerate(row_grad_dtypes) if d is not None]
    nadd = 0 if addend is None else 1

    def body(*refs):
        ins = [r[...].astype(F32) for r in refs[:nr + npar]]
        ct = [r[...].astype(F32) for r in refs[nr + npar:nr + npar + nc]]
        _, vjp = jax.vjp(fn, *ins)
        grads = vjp(tuple(ct))
        out_refs = refs[nr + npar + nc + nadd:]
        for o_ref, i in zip(out_refs[:len(keep)], keep):
            g = grads[i]
            if nadd and i == 0:
                g = g + refs[nr + npar + nc][...].astype(F32)
            o_ref[...] = g.astype(o_ref.dtype)

        @pl.when(pl.program_id(1) == 0)
        def _():
            for o_ref in out_refs[len(keep):]:
                o_ref[...] = jnp.zeros_like(o_ref)

        for o_ref, g in zip(out_refs[len(keep):], grads[nr:]):
            o_ref[...] += g

    def plain(w):
        return pl.BlockSpec((tb, ct or w), lambda j, i: (i, j))

    in_specs = [_wspec(tb, ct or r.width, r, row_first=False) for r in rows]
    in_specs += [pl.BlockSpec((1, ct or p.shape[1]), lambda j, i: (0, j)) for p in params]
    in_specs += [plain(c.shape[1]) for c in cts]
    extra = []
    if nadd:
        in_specs += [plain(addend.shape[1])]
        extra = [addend]
    out_shape = [_sds(rows[i].shape, row_grad_dtypes[i]) for i in keep] + [_sds(p.shape, F32) for p in params]
    out_specs = [plain(rows[i].width) for i in keep]
    out_specs += [pl.BlockSpec((1, ct or p.shape[1]), lambda j, i: (0, j)) for p in params]
    return _call(body, name=name, out_shape=out_shape, grid=(ncol, t // tb), in_specs=in_specs, out_specs=out_specs,
                 semantics=("parallel", "arbitrary"))(*[r.arr for r in rows], *params, *cts, *extra)


def _chunk_masks(c):
    row = lax.broadcasted_iota(jnp.int32, (c, c), 0)
    col = lax.broadcasted_iota(jnp.int32, (c, c), 1)
    causal = row >= col
    strict = row > col
    return causal, strict, causal.astype(F32), (row > col).astype(F32), (row == col).astype(F32)


def _lane_pick(blk, idx):
    lane = lax.broadcasted_iota(jnp.int32, blk.shape, 1)
    return jnp.sum(jnp.where(lane == idx, blk, 0.0), axis=1, keepdims=True)


def _bdot(a, b, ca, cb, prec=None):
    return lax.dot_general(a, b, (((ca,), (cb,)), ((0,), (0,))), precision=prec, preferred_element_type=F32)


def _bnn(a, b, prec=None):
    return _bdot(a, b, 2, 1, prec)


def _bnt(a, b, prec=None):
    return _bdot(a, b, 2, 2, prec)


def _btn(a, b, prec=None):
    return _bdot(a, b, 1, 1, prec)


def _unit_lower_inverse(a_mat, eye):
    x = -a_mat
    t_inv = eye + x
    p = x
    for _ in range(int(math.log2(a_mat.shape[-1])) - 1):
        p = _bnn(p, p, HI)
        t_inv = t_inv + _bnn(t_inv, p, HI)
    return t_inv


@jax.custom_vjp
def _saved_inverse(a_mat, t_saved):
    return t_saved


def _saved_inverse_fwd(a_mat, t_saved):
    return t_saved, t_saved


def _saved_inverse_bwd(t_inv, ct):
    return -_bnt(_btn(t_inv, ct, HI), t_inv, HI), jnp.zeros_like(t_inv)


_saved_inverse.defvjp(_saved_inverse_fwd, _saved_inverse_bwd)


def _gdn_heads(q, k, v, z, braw, araw, alog, dtb, nw, s_in, t_saved=None):
    b, c, d = q.shape
    causal, strict, lower, upper_t, eye = _chunk_masks(c)
    lower_b = jnp.broadcast_to(lower[None], (b, c, c))
    qn = q * lax.rsqrt(jnp.sum(q * q, axis=-1, keepdims=True) + NORM_EPS) * (d ** -0.5)
    kn = k * lax.rsqrt(jnp.sum(k * k, axis=-1, keepdims=True) + NORM_EPS)
    beta = _sigmoid(braw)
    g = -jnp.exp(alog) * _softplus(araw + dtb)
    dlog = _bnn(lower_b, g * upper_t[None], HI)
    dm = jnp.where(causal[None], jnp.exp(dlog), 0.0)
    g_lanes = jnp.broadcast_to(g, (b, c, d))
    gc = _bnn(lower_b, g_lanes, HI)
    gl = jnp.sum(g_lanes, axis=1, keepdims=True)
    eg = jnp.exp(gc)
    kb = kn * beta
    a_mat = jnp.where(strict[None], _bnt(kb, kn) * dm, 0.0)
    t_inv = _unit_lower_inverse(a_mat, eye[None]) if t_saved is None else _saved_inverse(a_mat, t_saved)
    r = beta * (v - eg * _bnn(kn, s_in))
    v_new = _bnn(t_inv, r)
    qk = _bnt(qn, kn) * dm
    out = eg * _bnn(qn, s_in) + _bnn(qk, v_new)
    k_tail = kn * jnp.exp(gl - gc)
    s_out = s_in * jnp.exp(gl) + _btn(k_tail, v_new)
    y = out * lax.rsqrt(jnp.mean(out * out, axis=-1, keepdims=True) + NORM_EPS) * nw[None] * _silu(z)
    if t_saved is None:
        return y, s_out, t_inv
    return y, s_out


def _gdn_stack(refs, hb, d, h, first_head):
    q_ref, k_ref, v_ref, z_ref, b_ref, a_ref, alog_ref, dtb_ref = refs
    sls = [slice(i * d, (i + 1) * d) for i in range(hb)]
    heads = [first_head + i for i in range(hb)]
    wide = [jnp.stack([r[:, sl] for sl in sls]) for r in (q_ref, k_ref, v_ref, z_ref)]
    cols = [jnp.stack([_lane_pick(r[...], hd) for hd in heads]) for r in (b_ref, a_ref, alog_ref, dtb_ref)]
    return wide + cols


GDN_HEADS_PER_STEP = 8


def _gdn_chunks_fwd(sqkv, z, braw, araw, alog, dtb, nw, hb=GDN_HEADS_PER_STEP):
    t, w3 = sqkv.shape
    w = w3 // 3
    d = GDN_HEAD_DIM
    h = w // d
    hb = min(hb, h)
    hg = h // hb
    c = CHUNK
    nc = t // c

    def body(q_ref, k_ref, v_ref, z_ref, b_ref, a_ref, alog_ref, dtb_ref, nw_ref, y_ref, ssave_ref, tsave_ref,
             s_ref):
        @pl.when(pl.program_id(1) == 0)
        def _():
            s_ref[...] = jnp.zeros_like(s_ref)

        s_in = s_ref[...]
        ssave_ref[...] = s_in
        args = _gdn_stack((q_ref, k_ref, v_ref, z_ref, b_ref, a_ref, alog_ref, dtb_ref), hb, d, h,
                          pl.program_id(0) * hb)
        y, s_out, t_inv = _gdn_heads(*args, nw_ref[...], s_in)
        for i in range(hb):
            y_ref[:, i * d:(i + 1) * d] = y[i]
        s_ref[...] = s_out
        tsave_ref[...] = t_inv

    blk = (c, hb * d)
    z = _win(z)
    zb = z.off // (hb * d)
    assert z.off % (hb * d) == 0
    in_specs = [pl.BlockSpec(blk, lambda g, n: (n, g)), pl.BlockSpec(blk, lambda g, n: (n, hg + g)),
                pl.BlockSpec(blk, lambda g, n: (n, 2 * hg + g)), pl.BlockSpec(blk, lambda g, n: (n, zb + g)),
                pl.BlockSpec((c, h), lambda g, n: (n, 0)), pl.BlockSpec((c, h), lambda g, n: (n, 0)),
                pl.BlockSpec((1, h), lambda g, n: (0, 0)), pl.BlockSpec((1, h), lambda g, n: (0, 0)),
                pl.BlockSpec((1, d), lambda g, n: (0, 0))]
    out_shape = [_sds((t, w), F32), _sds((hg, nc, hb, d, d), F32), _sds((hg, nc, hb, c, c), F32)]
    out_specs = [pl.BlockSpec(blk, lambda g, n: (n, g)),
                 pl.BlockSpec((None, None, hb, d, d), lambda g, n: (g, n, 0, 0, 0)),
                 pl.BlockSpec((None, None, hb, c, c), lambda g, n: (g, n, 0, 0, 0))]
    return _call(body, name="gdn_chunks_fwd", out_shape=out_shape, grid=(hg, nc), in_specs=in_specs,
                 out_specs=out_specs, scratch_shapes=[pltpu.VMEM((hb, d, d), F32)],
                 semantics=("parallel", "arbitrary"))(sqkv, sqkv, sqkv, z.arr, braw, araw, alog, dtb, nw)


def _gdn_chunks_bwd(sqkv, z, braw, araw, alog, dtb, nw, ssave, tsave, dy, hb=GDN_HEADS_PER_STEP):
    t, w3 = sqkv.shape
    w = w3 // 3
    d = GDN_HEAD_DIM
    h = w // d
    hb = min(hb, h)
    hg = h // hb
    c = CHUNK
    nc = t // c

    def body(q_ref, k_ref, v_ref, z_ref, b_ref, a_ref, alog_ref, dtb_ref, nw_ref, ssave_ref, tsave_ref, dy_ref,
             dsq_ref, dz_ref, db_ref, da_ref, dalog_ref, ddtb_ref, dnw_ref, ds_ref):
        first = jnp.logical_and(pl.program_id(0) == 0, pl.program_id(1) == 0)

        @pl.when(pl.program_id(1) == 0)
        def _():
            ds_ref[...] = jnp.zeros_like(ds_ref)

        @pl.when(first)
        def _():
            dalog_ref[...] = jnp.zeros_like(dalog_ref)
            ddtb_ref[...] = jnp.zeros_like(ddtb_ref)
            dnw_ref[...] = jnp.zeros_like(dnw_ref)

        lane_h = lax.broadcasted_iota(jnp.int32, (1, h), 1)
        db_acc = jnp.zeros((c, h), F32)
        da_acc = jnp.zeros((c, h), F32)
        args = _gdn_stack((q_ref, k_ref, v_ref, z_ref, b_ref, a_ref, alog_ref, dtb_ref), hb, d, h,
                          pl.program_id(0) * hb)
        t_saved = tsave_ref[...]
        _, vjp = jax.vjp(lambda *a: _gdn_heads(*a, t_saved=t_saved), *args, nw_ref[...], ssave_ref[...])
        dyb = jnp.stack([dy_ref[:, i * d:(i + 1) * d] for i in range(hb)])
        dq, dk, dv, dz, db, da, dalog, ddtb, dnw, ds_in = vjp((dyb, ds_ref[...]))
        for i in range(hb):
            sl = slice(i * d, (i + 1) * d)
            dsq_ref[:, i * d:(i + 1) * d] = dq[i]
            dsq_ref[:, w + i * d:w + (i + 1) * d] = dk[i]
            dsq_ref[:, 2 * w + i * d:2 * w + (i + 1) * d] = dv[i]
            dz_ref[:, sl] = dz[i].astype(dz_ref.dtype)
            onehot = (lane_h == pl.program_id(0) * hb + i).astype(F32)
            db_acc = db_acc + db[i] * onehot
            da_acc = da_acc + da[i] * onehot
            dalog_ref[...] += dalog[i] * onehot
            ddtb_ref[...] += ddtb[i] * onehot
        dnw_ref[...] += dnw
        ds_ref[...] = ds_in
        db_ref[...] = db_acc
        da_ref[...] = da_acc

    blk = (c, hb * d)
    rev = lambda n: nc - 1 - n
    z = _win(z)
    zb = z.off // (hb * d)
    assert z.off % (hb * d) == 0
    in_specs = [pl.BlockSpec(blk, lambda g, n: (rev(n), g)), pl.BlockSpec(blk, lambda g, n: (rev(n), hg + g)),
                pl.BlockSpec(blk, lambda g, n: (rev(n), 2 * hg + g)), pl.BlockSpec(blk, lambda g, n: (rev(n), zb + g)),
                pl.BlockSpec((c, h), lambda g, n: (rev(n), 0)), pl.BlockSpec((c, h), lambda g, n: (rev(n), 0)),
                pl.BlockSpec((1, h), lambda g, n: (0, 0)), pl.BlockSpec((1, h), lambda g, n: (0, 0)),
                pl.BlockSpec((1, d), lambda g, n: (0, 0)),
                pl.BlockSpec((None, None, hb, d, d), lambda g, n: (g, rev(n), 0, 0, 0)),
                pl.BlockSpec((None, None, hb, c, c), lambda g, n: (g, rev(n), 0, 0, 0)),
                pl.BlockSpec(blk, lambda g, n: (rev(n), g))]
    assert hg == 1
    out_shape = [_sds((t, w3), F32), _sds((t, w), BF16),
                 _sds((hg, t, h), F32), _sds((hg, t, h), F32), _sds((1, h), F32), _sds((1, h), F32), _sds((1, d), F32)]
    out_specs = [pl.BlockSpec((c, w3), lambda g, n: (rev(n), 0)), pl.BlockSpec(blk, lambda g, n: (rev(n), g))]
    out_specs += [pl.BlockSpec((None, c, h), lambda g, n: (g, rev(n), 0))] * 2
    out_specs += [pl.BlockSpec((1, h), lambda g, n: (0, 0)), pl.BlockSpec((1, h), lambda g, n: (0, 0)),
                  pl.BlockSpec((1, d), lambda g, n: (0, 0))]
    return _call(body, name="gdn_chunks_bwd", out_shape=out_shape, grid=(hg, nc), in_specs=in_specs,
                 out_specs=out_specs, scratch_shapes=[pltpu.VMEM((hb, d, d), F32)],
                 semantics=("arbitrary", "arbitrary"))(sqkv, sqkv, sqkv, z.arr, braw, araw, alog, dtb, nw, ssave, tsave,
                                                       dy)


def _m2_groups(xs, z, bm, cm, dtraws, alogs, dtbs, dsks, nw, st):
    g, c, gw = xs.shape
    rep = len(dtraws)
    causal, _, lower, upper_t, _ = _chunk_masks(c)
    lower_b = jnp.broadcast_to(lower[None], (g, c, c))
    lane_head = lax.broadcasted_iota(jnp.int32, (1, 1, gw), 2) // M2_HEAD_DIM

    def expand(cols):
        res = jnp.broadcast_to(cols[-1], (g, cols[-1].shape[1], gw))
        for i in reversed(range(rep - 1)):
            res = jnp.where(lane_head == i, cols[i], res)
        return res

    dts = [_softplus(dtraws[i] + dtbs[i]) for i in range(rep)]
    adts = [-jnp.exp(alogs[i]) * dts[i] for i in range(rep)]
    dt_l, adt_l, dsk_l = expand(dts), expand(adts), expand(dsks)
    xdt = xs * dt_l
    acum = _bnn(lower_b, adt_l, HI)
    alast = jnp.sum(adt_l, axis=1, keepdims=True)
    scores = _bnt(cm, bm)
    y = jnp.exp(acum) * _bnn(cm, st) + dsk_l * xs
    for i in range(rep):
        seg = jnp.where(causal[None], jnp.exp(_bnn(lower_b, adts[i] * upper_t[None], HI)), 0.0)
        y = y + _bnn(scores * seg, jnp.where(lane_head == i, xdt, 0.0))
    st_out = st * jnp.exp(alast) + _btn(bm, xdt * jnp.exp(alast - acum))
    y2 = y * _silu(z)
    out = y2 * lax.rsqrt(jnp.mean(y2 * y2, axis=-1, keepdims=True) + NORM_EPS) * nw
    return out, st_out


def _m2_dims(sxbc, z):
    t = sxbc.shape[0]
    w2 = z.shape[1]
    g = M2_GROUPS
    n = M2_STATE
    assert sxbc.shape[1] == w2 + 2 * g * n
    gw = w2 // g
    return t, w2, g, n, gw, gw // M2_HEAD_DIM, t // CHUNK


def _m2_args(refs, g, n, gw, rep, st):
    sx_ref, z_ref, dt_ref, alog_ref, dtb_ref, dsk_ref, nw_ref = refs
    w2 = g * gw
    xs = jnp.stack([sx_ref[:, i * gw:(i + 1) * gw] for i in range(g)])
    z = jnp.stack([z_ref[:, i * gw:(i + 1) * gw] for i in range(g)])
    bm = jnp.stack([sx_ref[:, w2 + i * n:w2 + (i + 1) * n] for i in range(g)])
    cm = jnp.stack([sx_ref[:, w2 + (g + i) * n:w2 + (g + i + 1) * n] for i in range(g)])
    nw = jnp.stack([nw_ref[:, i * gw:(i + 1) * gw] for i in range(g)])

    def cols(ref):
        blk = ref[...]
        return [jnp.stack([_lane_pick(blk, gi * rep + i) for gi in range(g)]) for i in range(rep)]

    return xs, z, bm, cm, cols(dt_ref), cols(alog_ref), cols(dtb_ref), cols(dsk_ref), nw, st


def _m2_chunks_fwd(sxbc, z, dtraw, alog, dtb, dsk, nw):
    t, w2, g, n, gw, rep, nc = _m2_dims(sxbc, z)
    hm = dtraw.shape[1]
    c = CHUNK
    wx = sxbc.shape[1]

    def body(sx_ref, z_ref, dt_ref, alog_ref, dtb_ref, dsk_ref, nw_ref, y_ref, ssave_ref, st_ref):
        @pl.when(pl.program_id(0) == 0)
        def _():
            st_ref[...] = jnp.zeros_like(st_ref)

        st = st_ref[...]
        ssave_ref[...] = st
        y, st_out = _m2_groups(*_m2_args((sx_ref, z_ref, dt_ref, alog_ref, dtb_ref, dsk_ref, nw_ref), g, n, gw, rep, st))
        for i in range(g):
            y_ref[:, i * gw:(i + 1) * gw] = y[i]
        st_ref[...] = st_out

    z = _win(z)
    zb = z.off // w2
    assert z.off % w2 == 0
    in_specs = [pl.BlockSpec((c, wx), lambda k: (k, 0)), pl.BlockSpec((c, w2), lambda k: (k, zb)),
                pl.BlockSpec((c, hm), lambda k: (k, 0)),
                pl.BlockSpec((1, hm), lambda k: (0, 0)), pl.BlockSpec((1, hm), lambda k: (0, 0)),
                pl.BlockSpec((1, hm), lambda k: (0, 0)), pl.BlockSpec((1, w2), lambda k: (0, 0))]
    out_shape = [_sds((t, w2), F32), _sds((nc, g, n, gw), F32)]
    out_specs = [pl.BlockSpec((c, w2), lambda k: (k, 0)), pl.BlockSpec((None, g, n, gw), lambda k: (k, 0, 0, 0))]
    return _call(body, name="m2_chunks_fwd", out_shape=out_shape, grid=(nc,), in_specs=in_specs,
                 out_specs=out_specs, scratch_shapes=[pltpu.VMEM((g, n, gw), F32)],
                 semantics=("arbitrary",))(sxbc, z.arr, dtraw, alog, dtb, dsk, nw)


def _m2_chunks_bwd(sxbc, z, dtraw, alog, dtb, dsk, nw, ssave, dy):
    t, w2, g, n, gw, rep, nc = _m2_dims(sxbc, z)
    hm = dtraw.shape[1]
    c = CHUNK

    wx = sxbc.shape[1]

    def body(sx_ref, z_ref, dt_ref, alog_ref, dtb_ref, dsk_ref, nw_ref, ssave_ref, dy_ref,
             dsx_ref, dz_ref, ddt_ref, dalog_ref, ddtb_ref, ddsk_ref, dnw_ref, dst_ref):
        @pl.when(pl.program_id(0) == 0)
        def _():
            dst_ref[...] = jnp.zeros_like(dst_ref)
            dnw_ref[...] = jnp.zeros_like(dnw_ref)
            dalog_ref[...] = jnp.zeros_like(dalog_ref)
            ddtb_ref[...] = jnp.zeros_like(ddtb_ref)
            ddsk_ref[...] = jnp.zeros_like(ddsk_ref)

        args = _m2_args((sx_ref, z_ref, dt_ref, alog_ref, dtb_ref, dsk_ref, nw_ref), g, n, gw, rep, ssave_ref[...])
        _, vjp = jax.vjp(_m2_groups, *args)
        dyb = jnp.stack([dy_ref[:, i * gw:(i + 1) * gw] for i in range(g)])
        dxs, dz, dbm, dcm, ddts, dalogs, ddtbs, ddsks, dnw, dst = vjp((dyb, dst_ref[...]))
        dst_ref[...] = dst
        lane_h = lax.broadcasted_iota(jnp.int32, (1, hm), 1)
        ddt = jnp.zeros((c, hm), F32)
        for gi in range(g):
            dsx_ref[:, gi * gw:(gi + 1) * gw] = dxs[gi]
            dsx_ref[:, w2 + gi * n:w2 + (gi + 1) * n] = dbm[gi]
            dsx_ref[:, w2 + (g + gi) * n:w2 + (g + gi + 1) * n] = dcm[gi]
            dz_ref[:, gi * gw:(gi + 1) * gw] = dz[gi].astype(dz_ref.dtype)
            dnw_ref[:, gi * gw:(gi + 1) * gw] += dnw[gi]
            for i in range(rep):
                onehot = (lane_h == gi * rep + i).astype(F32)
                ddt = ddt + ddts[i][gi] * onehot
                dalog_ref[...] += dalogs[i][gi] * onehot
                ddtb_ref[...] += ddtbs[i][gi] * onehot
                ddsk_ref[...] += ddsks[i][gi] * onehot
        ddt_ref[...] = ddt

    rev = lambda k: nc - 1 - k
    z = _win(z)
    zb = z.off // w2
    assert z.off % w2 == 0
    in_specs = [pl.BlockSpec((c, wx), lambda k: (rev(k), 0)), pl.BlockSpec((c, w2), lambda k: (rev(k), zb)),
                pl.BlockSpec((c, hm), lambda k: (rev(k), 0)),
                pl.BlockSpec((1, hm), lambda k: (0, 0)), pl.BlockSpec((1, hm), lambda k: (0, 0)),
                pl.BlockSpec((1, hm), lambda k: (0, 0)), pl.BlockSpec((1, w2), lambda k: (0, 0)),
                pl.BlockSpec((None, g, n, gw), lambda k: (rev(k), 0, 0, 0)),
                pl.BlockSpec((c, w2), lambda k: (rev(k), 0))]
    out_shape = [_sds((t, wx), F32), _sds((t, w2), BF16), _sds((t, hm), F32), _sds((1, hm), F32), _sds((1, hm), F32),
                 _sds((1, hm), F32), _sds((1, w2), F32)]
    out_specs = [pl.BlockSpec((c, wx), lambda k: (rev(k), 0)), pl.BlockSpec((c, w2), lambda k: (rev(k), 0)),
                 pl.BlockSpec((c, hm), lambda k: (rev(k), 0)),
                 pl.BlockSpec((1, hm), lambda k: (0, 0)), pl.BlockSpec((1, hm), lambda k: (0, 0)),
                 pl.BlockSpec((1, hm), lambda k: (0, 0)), pl.BlockSpec((1, w2), lambda k: (0, 0))]
    return _call(body, name="m2_chunks_bwd", out_shape=out_shape, grid=(nc,), in_specs=in_specs,
                 out_specs=out_specs, scratch_shapes=[pltpu.VMEM((g, n, gw), F32)],
                 semantics=("arbitrary",))(sxbc, z.arr, dtraw, alog, dtb, dsk, nw, ssave, dy)


def _s5_scan(bu3, aa3, bb3, reverse, name, tb=256):
    t, rtot, _ = bu3.shape
    jn = rtot // SUBLANES
    tb = _tile(t, tb, SUBLANES)
    nb = t // tb

    def body(bu_ref, aa_ref, bb_ref, s_ref, st_ref):
        @pl.when(pl.program_id(0) == 0)
        def _():
            st_ref[...] = jnp.zeros_like(st_ref)

        tiles = [slice(j * SUBLANES, (j + 1) * SUBLANES) for j in range(jn)]
        aa = [aa_ref[tl, :] for tl in tiles]
        bb = [bb_ref[tl, :] for tl in tiles]

        def step(k, carry):
            r = tb - 1 - k if reverse else k
            new = []
            for j, tl in enumerate(tiles):
                sj, wj = carry[2 * j], carry[2 * j + 1]
                xj = bu_ref[r, tl, :]
                nj = aa[j] * sj + bb[j] * wj + xj
                s_ref[r, tl, :] = nj
                new += [nj, aa[j] * wj - bb[j] * sj + pltpu.roll(xj, SUBLANES // 2, 0)]
            return tuple(new)

        init = []
        for tl in tiles:
            init += [st_ref[tl, :], pltpu.roll(st_ref[tl, :], SUBLANES // 2, 0)]
        last = lax.fori_loop(0, tb, step, tuple(init), unroll=8)
        for j, tl in enumerate(tiles):
            st_ref[tl, :] = last[2 * j]

    rb = (lambda i: nb - 1 - i) if reverse else (lambda i: i)
    return _call(body, name=name, out_shape=_sds(bu3.shape, F32), grid=(nb,),
                 in_specs=[pl.BlockSpec((tb, rtot, LANES), lambda i: (rb(i), 0, 0)),
                           pl.BlockSpec((rtot, LANES), lambda i: (0, 0)), pl.BlockSpec((rtot, LANES), lambda i: (0, 0))],
                 out_specs=pl.BlockSpec((tb, rtot, LANES), lambda i: (rb(i), 0, 0)),
                 scratch_shapes=[pltpu.VMEM((rtot, LANES), F32)], semantics=("arbitrary",))(bu3, aa3, bb3)


def _s5_da(ds3, s3, tb=256):
    t, rtot, _ = ds3.shape
    tb = _tile(t, tb, SUBLANES)
    nb = t // tb

    def body(ds_ref, s_ref, halo_ref, p_ref, q_ref):
        i = pl.program_id(0)

        @pl.when(i == 0)
        def _():
            p_ref[...] = jnp.zeros_like(p_ref)
            q_ref[...] = jnp.zeros_like(q_ref)

        for j in range(rtot // SUBLANES):
            tl = slice(j * SUBLANES, (j + 1) * SUBLANES)
            prev = jnp.where(i == 0, 0.0, halo_ref[:, tl, :])
            sh = jnp.concatenate([prev, s_ref[0:tb - 1, tl, :]], axis=0)
            d = ds_ref[:, tl, :]
            p_ref[tl, :] += jnp.sum(d * sh, axis=0)
            q_ref[tl, :] += jnp.sum(d * pltpu.roll(sh, SUBLANES // 2, 1), axis=0)

    blk = pl.BlockSpec((tb, rtot, LANES), lambda i: (i, 0, 0))
    acc = pl.BlockSpec((rtot, LANES), lambda i: (0, 0))
    return _call(body, name="s5_da", out_shape=[_sds((rtot, LANES), F32)] * 2, grid=(nb,),
                 in_specs=[blk, blk, pl.BlockSpec((1, rtot, LANES), lambda i: (jnp.maximum(i * tb - 1, 0), 0, 0))],
                 out_specs=[acc, acc], semantics=("arbitrary",))(ds3, s3, s3)


def _conv_rows(t, tb):
    tb = _tile(t, tb, SUBLANES * 2)
    return tb, t // tb, tb // SUBLANES


def _shift_down(x, above, s):
    n = x.shape[0]
    y = pltpu.roll(x, s, 0)
    row = lax.broadcasted_iota(jnp.int32, above.shape, 0)
    head = jnp.where(row < s, pltpu.roll(above, s, 0), y[:SUBLANES])
    return head if n == SUBLANES else jnp.concatenate([head, y[SUBLANES:]], axis=0)


def _shift_up(x, below, s):
    n = x.shape[0]
    y = pltpu.roll(x, n - s, 0)
    row = lax.broadcasted_iota(jnp.int32, below.shape, 0)
    tail = jnp.where(row >= SUBLANES - s, pltpu.roll(below, SUBLANES - s, 0), y[n - SUBLANES:])
    return tail if n == SUBLANES else jnp.concatenate([y[:n - SUBLANES], tail], axis=0)


def _conv_taps(x, above, w_ref, b_ref):
    xs = [x] + [_shift_down(x, above, s) for s in range(1, CONV_K)]
    c = b_ref[...] + w_ref[CONV_K - 1:CONV_K, :] * x
    for s in range(1, CONV_K):
        c = c + w_ref[CONV_K - 1 - s:CONV_K - s, :] * xs[s]
    return c, xs


CONV_COL_TILE = 1024


def _conv_specs(x, tb):
    x = _win(x)
    t, cwid = x.shape
    ct = _col_tile([x], CONV_COL_TILE)
    tb, nb, hb = _conv_rows(t, max(tb, BLOCK_BYTES // (4 * ct)))
    base = x.off // ct
    blk_x = pl.BlockSpec((tb, ct), lambda j, i: (i, base + j))
    prev_x = pl.BlockSpec((SUBLANES, ct), lambda j, i: (jnp.maximum(i * hb - 1, 0), base + j))
    next_x = pl.BlockSpec((SUBLANES, ct), lambda j, i: (jnp.minimum((i + 1) * hb, nb * hb - 1), base + j))
    blk = pl.BlockSpec((tb, ct), lambda j, i: (i, j))
    nxt = pl.BlockSpec((SUBLANES, ct), lambda j, i: (jnp.minimum((i + 1) * hb, nb * hb - 1), j))
    taps = pl.BlockSpec((CONV_K, ct), lambda j, i: (0, j))
    bias = pl.BlockSpec((1, ct), lambda j, i: (0, j))
    return x, tb, nb, cwid // ct, dict(blk_x=blk_x, prev_x=prev_x, next_x=next_x, blk=blk, nxt=nxt, taps=taps, bias=bias)


def _conv_fwd(x, w, b, name, tb=256):
    x, tb, nb, ncol, sp = _conv_specs(x, tb)
    t, cwid = x.shape

    def body(x_ref, halo_ref, w_ref, b_ref, o_ref):
        above = jnp.where(pl.program_id(1) == 0, 0.0, halo_ref[...])
        o_ref[...] = _silu(_conv_taps(x_ref[...], above, w_ref, b_ref)[0])

    return _call(body, name=name, out_shape=_sds((t, cwid), F32), grid=(ncol, nb),
                 in_specs=[sp['blk_x'], sp['prev_x'], sp['taps'], sp['bias']], out_specs=sp['blk'],
                 semantics=("parallel", "parallel"))(x.arr, x.arr, w, b)


def _dsilu(c):
    sg = _sigmoid(c)
    return sg * (1.0 + c * (1.0 - sg))


def _conv_bwd(x, w, b, ds, name, tb=256):
    x, tb, nb, ncol, sp = _conv_specs(x, tb)
    t, cwid = x.shape

    def body(x_ref, halo_ref, xn_ref, ds_ref, dsn_ref, w_ref, b_ref, dx_ref, dw_ref, db_ref):
        i = pl.program_id(1)

        @pl.when(i == 0)
        def _():
            dw_ref[...] = jnp.zeros_like(dw_ref)
            db_ref[...] = jnp.zeros_like(db_ref)

        x = x_ref[...]
        above = jnp.where(i == 0, 0.0, halo_ref[...])
        c, xs = _conv_taps(x, above, w_ref, b_ref)
        dc = ds_ref[...] * _dsilu(c)
        cn, _ = _conv_taps(xn_ref[...], x[tb - SUBLANES:], w_ref, b_ref)
        dcn = jnp.where(i == nb - 1, 0.0, dsn_ref[...] * _dsilu(cn))
        dx = w_ref[CONV_K - 1:CONV_K, :] * dc
        for s in range(1, CONV_K):
            dx = dx + w_ref[CONV_K - 1 - s:CONV_K - s, :] * _shift_up(dc, dcn, s)
        dx_ref[...] = dx.astype(dx_ref.dtype)
        for s in range(CONV_K):
            dw_ref[CONV_K - 1 - s:CONV_K - s, :] += jnp.sum(dc * xs[s], axis=0, keepdims=True)
        db_ref[...] += jnp.sum(dc, axis=0, keepdims=True)

    return _call(body, name=name, out_shape=[_sds((t, cwid), BF16), _sds((CONV_K, cwid), F32), _sds((1, cwid), F32)],
                 grid=(ncol, nb),
                 in_specs=[sp['blk_x'], sp['prev_x'], sp['next_x'], sp['blk'], sp['nxt'], sp['taps'], sp['bias']],
                 out_specs=[sp['blk'], sp['taps'], sp['bias']],
                 semantics=("parallel", "arbitrary"))(x.arr, x.arr, x.arr, ds, ds, w, b)


def _f_rms(x, w):
    return (x * lax.rsqrt(jnp.mean(x * x, axis=-1, keepdims=True) + NORM_EPS) * w,)


def _f_s5_post1(ymm, u, d_l):
    return (_gelu_tanh(ymm + d_l * u),)


def _f_s5_post2(yg, tt, gate, b):
    return (yg * _sigmoid(tt + b) * _silu(gate),)


def _f_merge(ma, mb, mc, pa, pb, pc):
    return (_sigmoid(ma) * pa + _sigmoid(mb) * pb + _sigmoid(mc) * pc,)


def _loss_and_grad(x, tgt, fw, tb=256):
    t, dm = x.shape
    tb = _tile(t, tb, SUBLANES * 2)

    def f(xb, wb, tb_):
        y = _f_rms(xb, wb)[0]
        e = y - tb_
        return 0.5 * jnp.sum(jnp.mean(e * e, axis=-1, keepdims=True), axis=0, keepdims=True)

    def body(x_ref, t_ref, w_ref, loss_ref, dx_ref, dw_ref):
        @pl.when(pl.program_id(0) == 0)
        def _():
            loss_ref[...] = jnp.zeros_like(loss_ref)
            dw_ref[...] = jnp.zeros_like(dw_ref)

        tgt_b = t_ref[...]
        val, vjp = jax.vjp(lambda a, b: f(a, b, tgt_b), x_ref[...], w_ref[...])
        dxb, dwb = vjp(jnp.ones((1, 1), F32))
        loss_ref[...] += val
        dx_ref[...] = dxb
        dw_ref[...] += dwb

    return _call(body, name="loss_and_grad", out_shape=[_sds((1, 1), F32), _sds((t, dm), F32), _sds((1, dm), F32)],
                 grid=(t // tb,),
                 in_specs=[pl.BlockSpec((tb, dm), lambda i: (i, 0)), pl.BlockSpec((tb, dm), lambda i: (i, 0)),
                           pl.BlockSpec((1, dm), lambda i: (0, 0))],
                 out_specs=[pl.BlockSpec((1, 1), lambda i: (0, 0)), pl.BlockSpec((tb, dm), lambda i: (i, 0)),
                            pl.BlockSpec((1, dm), lambda i: (0, 0))],
                 semantics=("arbitrary",))(x, tgt, fw)


FLAT_W = 1024


def _sum_parts(parts, name, tb=256):
    n, r, wd = parts.shape
    tb = _tile(r, tb, SUBLANES)

    def body(p_ref, o_ref):
        acc = p_ref[0]
        for k in range(1, n):
            acc = acc + p_ref[k]
        o_ref[...] = acc

    return _call(body, name=name, out_shape=_sds((r, wd), F32), grid=(r // tb,),
                 in_specs=[pl.BlockSpec((n, tb, wd), lambda i: (0, i, 0))],
                 out_specs=pl.BlockSpec((tb, wd), lambda i: (i, 0)), semantics=("parallel",))(parts)


BLOCK_BYTES = 1 << 20


def _rows_per_block(r, wd):
    return _tile(r, max(SUBLANES * 2, BLOCK_BYTES // (4 * wd) // (SUBLANES * 2) * (SUBLANES * 2)), SUBLANES * 2)


def _add_my_half(g4, recv, c_idx, name):
    p, _, r, wd = g4.shape
    tb = _rows_per_block(r, wd)

    def body(c_ref, g_ref, r_ref, o_ref, ob_ref):
        s = g_ref[...] + r_ref[...]
        o_ref[...] = s
        ob_ref[...] = s.astype(BF16)

    spec = pl.BlockSpec((None, tb, wd), lambda j, i, c_ref: (j, i, 0))
    return _call(body, name=name, out_shape=[_sds((p, r, wd), F32), _sds((p, r, wd), BF16)], grid=(p, r // tb),
                 in_specs=[pl.BlockSpec((None, None, tb, wd), lambda j, i, c_ref: (j, c_ref[0], i, 0)), spec],
                 out_specs=[spec, spec], semantics=("parallel", "parallel"), num_scalar_prefetch=1)(c_idx, g4, recv)


def _sum_chips(own, got, me_idx, name):
    p, r, wd = own.shape
    tb = _rows_per_block(r, wd)

    def body(me_ref, own_ref, got_ref, o_ref):
        me = me_ref[0]
        acc = None
        for k in range(p):
            part = jnp.where(me == k, own_ref[...], got_ref[k].astype(F32))
            acc = part if acc is None else acc + part
        o_ref[...] = acc

    return _call(body, name=name, out_shape=_sds((r, wd), F32), grid=(r // tb,),
                 in_specs=[pl.BlockSpec((None, tb, wd), lambda i, me_ref: (me_ref[0], i, 0)),
                           pl.BlockSpec((p, tb, wd), lambda i, me_ref: (0, i, 0))],
                 out_specs=pl.BlockSpec((tb, wd), lambda i, me_ref: (i, 0)),
                 semantics=("parallel",), num_scalar_prefetch=1)(me_idx, own, got)


def _adamw(w, g, m, v, name):
    r, wd = w.shape
    tb = _rows_per_block(r, wd)

    def body(w_ref, g_ref, m_ref, v_ref, d_ref, nm_ref, nv_ref):
        gg = g_ref[...]
        nm = ADAM_B1 * m_ref[...] + (1.0 - ADAM_B1) * gg
        nv = ADAM_B2 * v_ref[...] + (1.0 - ADAM_B2) * (gg * gg)
        m_hat = nm / (1.0 - ADAM_B1 ** ADAM_STEP)
        v_hat = nv / (1.0 - ADAM_B2 ** ADAM_STEP)
        d_ref[...] = -ADAM_LR * (m_hat / (jnp.sqrt(v_hat) + ADAM_EPS) + ADAM_WD * w_ref[...])
        nm_ref[...] = nm
        nv_ref[...] = nv

    spec = pl.BlockSpec((tb, wd), lambda i: (i, 0))
    return _call(body, name=name, out_shape=[_sds((r, wd), F32)] * 3, grid=(r // tb,), in_specs=[spec] * 4,
                 out_specs=[spec] * 3, semantics=("parallel",))(w, g, m, v)


def _here():
    return lax.axis_index("x"), lax.axis_index("y"), lax.axis_index("c")


def _comm_call(body, name, out_shape, n_sems, operands):
    anyspec = pl.BlockSpec(memory_space=pl.ANY)
    outs = out_shape if isinstance(out_shape, (list, tuple)) else [out_shape]
    return _call(body, name=name, out_shape=out_shape, in_specs=[anyspec] * len(operands),
                 out_specs=[anyspec] * len(outs) if isinstance(out_shape, (list, tuple)) else anyspec,
                 scratch_shapes=[pltpu.SemaphoreType.DMA((n_sems,)), pltpu.SemaphoreType.DMA((n_sems,)),
                                 pltpu.SemaphoreType.DMA(())])(*operands)


def _gather_chips(x, name):
    def body(x_ref, o_ref, send_sems, recv_sems, local_sem):
        xi, yi, ci = _here()
        chips = [(1 - xi, yi), (xi, 1 - yi), (1 - xi, 1 - yi)]
        mine = pltpu.make_async_copy(x_ref, o_ref.at[2 * xi + yi], local_sem)
        mine.start()

        def copy(k, slot, to):
            return pltpu.make_async_remote_copy(src_ref=x_ref, dst_ref=o_ref.at[slot], send_sem=send_sems.at[k],
                                                recv_sem=recv_sems.at[k], device_id=to, device_id_type=MESH)

        sends = [copy(k, 2 * xi + yi, (px, py, ci)) for k, (px, py) in enumerate(chips)]
        for cp in sends:
            cp.start()
        for k, (px, py) in enumerate(chips):
            copy(k, 2 * px + py, (px, py, ci)).wait_recv()
        for cp in sends:
            cp.wait_send()
        mine.wait()

    return _comm_call(body, name, _sds((4,) + x.shape, x.dtype), 3, (x,))


def _gather_all(x, name):
    def body(x_ref, o_ref, send_sems, recv_sems, local_sem):
        xi, yi, ci = _here()
        me = 4 * xi + 2 * yi + ci
        flips = [(fx, fy, fc) for fx in (0, 1) for fy in (0, 1) for fc in (0, 1)][1:]
        peers = [((1 - xi) if fx else xi, (1 - yi) if fy else yi, (1 - ci) if fc else ci) for fx, fy, fc in flips]
        mine = pltpu.make_async_copy(x_ref, o_ref.at[me], local_sem)
        mine.start()

        def copy(k, slot, to):
            return pltpu.make_async_remote_copy(src_ref=x_ref, dst_ref=o_ref.at[slot], send_sem=send_sems.at[k],
                                                recv_sem=recv_sems.at[k], device_id=to, device_id_type=MESH)

        sends = [copy(k, me, p) for k, p in enumerate(peers)]
        for cp in sends:
            cp.start()
        for k, (px, py, pc) in enumerate(peers):
            copy(k, 4 * px + 2 * py + pc, (px, py, pc)).wait_recv()
        for cp in sends:
            cp.wait_send()
        mine.wait()

    return _comm_call(body, name, _sds((8,) + x.shape, x.dtype), 7, (x,))


def _multi_comm_call(body, name, out_shapes, n_sems, operands):
    anyspec = pl.BlockSpec(memory_space=pl.ANY)
    nin = len(operands)

    def flat_body(*refs):
        body(refs[:nin], refs[nin:nin + len(out_shapes)], refs[-2], refs[-1])

    return _call(flat_body, name=name, out_shape=list(out_shapes), in_specs=[anyspec] * nin,
                 out_specs=[anyspec] * len(out_shapes),
                 scratch_shapes=[pltpu.SemaphoreType.DMA((n_sems,)), pltpu.SemaphoreType.DMA((n_sems,))])(*operands)


def _remote(src, dst, send_sems, recv_sems, k, to):
    return pltpu.make_async_remote_copy(src_ref=src, dst_ref=dst, send_sem=send_sems.at[k], recv_sem=recv_sems.at[k],
                                        device_id=to, device_id_type=MESH)


def _gather_chips_split(xs, name):
    nw = len(xs)

    def body(x_refs, o_refs, send_sems, recv_sems):
        xi, yi, ci = _here()
        me = 2 * xi + yi
        sib = (xi, yi, 1 - ci)
        chips = [(1 - xi, yi), (xi, 1 - yi), (1 - xi, 1 - yi)]
        sends = []
        for i in range(nw):
            for k, (px, py) in enumerate(chips):
                sends.append(_remote(x_refs[i].at[ci], o_refs[i].at[me, ci], send_sems, recv_sems, 3 * i + k,
                                     (px, py, ci)))
        for cp in sends:
            cp.start()
        passed = []
        for i in range(nw):
            for k, (px, py) in enumerate(chips):
                landed = o_refs[i].at[2 * px + py, ci]
                _remote(landed, landed, send_sems, recv_sems, 3 * i + k, (px, py, ci)).wait_recv()
                fwd = _remote(landed, landed, send_sems, recv_sems, 3 * (nw + i) + k, sib)
                fwd.start()
                passed.append(fwd)
        for i in range(nw):
            for k, (px, py) in enumerate(chips):
                other = o_refs[i].at[2 * px + py, 1 - ci]
                _remote(other, other, send_sems, recv_sems, 3 * (nw + i) + k, sib).wait_recv()
        for cp in sends + passed:
            cp.wait_send()

    return _multi_comm_call(body, name, [_sds((4,) + x.shape, x.dtype) for x in xs], 6 * nw, xs)


GATHER_COLLECTIVE_ID = 1


def _gather_chips_split_async(xs, name, collective_id):
    nw = len(xs)
    x_refs = [jax.new_ref(x, memory_space=pltpu.MemorySpace.HBM) for x in xs]
    o_refs = [jax.empty_ref(_sds((4,) + x.shape, x.dtype), memory_space=pltpu.MemorySpace.HBM) for x in xs]

    @pl.kernel(mesh=plsc.ScalarSubcoreMesh(axis_name="sequencer", num_cores=1), name=name,
               scratch_types=(pltpu.SemaphoreType.DMA((6 * nw,)), pltpu.SemaphoreType.DMA((6 * nw,))),
               compiler_params=pltpu.CompilerParams(collective_id=collective_id))
    def launch(send_sems, recv_sems):
        xi, yi, ci = _here()
        me = 2 * xi + yi
        sib = (xi, yi, 1 - ci)
        chips = [(1 - xi, yi), (xi, 1 - yi), (1 - xi, 1 - yi)]
        barrier = pltpu.get_barrier_semaphore()
        for peer in [sib] + [(px, py, ci) for px, py in chips]:
            pl.semaphore_signal(barrier, inc=1, device_id=peer, device_id_type=MESH)
        pl.semaphore_wait(barrier, 4)
        sends = []
        for i in range(nw):
            for k, (px, py) in enumerate(chips):
                sends.append(_remote(x_refs[i].at[ci], o_refs[i].at[me, ci], send_sems, recv_sems, 3 * i + k,
                                     (px, py, ci)))
        for cp in sends:
            cp.start()
        passed = []
        for i in range(nw):
            for k, (px, py) in enumerate(chips):
                landed = o_refs[i].at[2 * px + py, ci]
                _remote(landed, landed, send_sems, recv_sems, 3 * i + k, (px, py, ci)).wait_recv()
                fwd = _remote(landed, landed, send_sems, recv_sems, 3 * (nw + i) + k, sib)
                fwd.start()
                passed.append(fwd)
        for i in range(nw):
            for k, (px, py) in enumerate(chips):
                other = o_refs[i].at[2 * px + py, 1 - ci]
                _remote(other, other, send_sems, recv_sems, 3 * (nw + i) + k, sib).wait_recv()
        for cp in sends + passed:
            cp.wait_send()

    launch()
    return [o[...] for o in o_refs]


def _swap_sibling_half(gs):
    def body(g_refs, o_refs, send_sems, recv_sems):
        xi, yi, ci = _here()
        cps = [_remote(g.at[:, 1 - ci], o, send_sems, recv_sems, i, (xi, yi, 1 - ci))
               for i, (g, o) in enumerate(zip(g_refs, o_refs))]
        for cp in cps:
            cp.start()
        for cp in cps:
            cp.wait()

    return _multi_comm_call(body, "swap_sibling_half", [_sds((g.shape[0],) + g.shape[2:], g.dtype) for g in gs],
                            len(gs), gs)


SCATTER_COLLECTIVE_ID = 2


def _scatter_chips_async(gps, name, collective_id):
    nw = len(gps)
    g_refs = [jax.new_ref(g, memory_space=pltpu.MemorySpace.HBM) for g in gps]
    o_refs = [jax.empty_ref(_sds(g.shape, g.dtype), memory_space=pltpu.MemorySpace.HBM) for g in gps]

    @pl.kernel(mesh=plsc.ScalarSubcoreMesh(axis_name="sequencer", num_cores=1), name=name,
               scratch_types=(pltpu.SemaphoreType.DMA((3 * nw,)), pltpu.SemaphoreType.DMA((3 * nw,))),
               compiler_params=pltpu.CompilerParams(collective_id=collective_id))
    def launch(send_sems, recv_sems):
        xi, yi, ci = _here()
        me = 2 * xi + yi
        chips = [(1 - xi, yi), (xi, 1 - yi), (1 - xi, 1 - yi)]
        barrier = pltpu.get_barrier_semaphore()
        for px, py in chips:
            pl.semaphore_signal(barrier, inc=1, device_id=(px, py, ci), device_id_type=MESH)
        pl.semaphore_wait(barrier, 3)
        sends = [_remote(g_refs[i].at[2 * px + py], o_refs[i].at[me], send_sems, recv_sems, 3 * i + k, (px, py, ci))
                 for i in range(nw) for k, (px, py) in enumerate(chips)]
        for cp in sends:
            cp.start()
        for i in range(nw):
            for k, (px, py) in enumerate(chips):
                slot = o_refs[i].at[2 * px + py]
                _remote(slot, slot, send_sems, recv_sems, 3 * i + k, (px, py, ci)).wait_recv()
        for cp in sends:
            cp.wait_send()

    launch()
    return [o[...] for o in o_refs]


def _share_sibling(rs):
    def body(r_refs, o_refs, send_sems, recv_sems):
        xi, yi, ci = _here()
        cps = [_remote(r, o, send_sems, recv_sems, i, (xi, yi, 1 - ci)) for i, (r, o) in enumerate(zip(r_refs, o_refs))]
        for cp in cps:
            cp.start()
        for cp in cps:
            cp.wait()

    return _multi_comm_call(body, "share_sibling", [_sds(r.shape, r.dtype) for r in rs], len(rs), rs)


def _flat_pack(arrs, dtype, row_mult):
    flat = jnp.concatenate([a.astype(dtype).reshape(-1) for a in arrs])
    unit = FLAT_W * row_mult
    npad = -(-flat.shape[0] // unit) * unit
    return jnp.pad(flat, (0, npad - flat.shape[0])).reshape(npad // FLAT_W, FLAT_W)


def _flat_unpack(flat2d, shapes):
    flat = flat2d.reshape(-1)
    outs, off = [], 0
    for s in shapes:
        size = int(np.prod(s))
        outs.append(flat[off:off + size].reshape(s))
        off += size
    return outs


def _split_cols(a, widths):
    outs, off = [], 0
    for wd in widths:
        outs.append(lax.slice_in_dim(a, off, off + wd, axis=1))
        off += wd
    return outs


def _s5_params(lam_re, lam_im, log_step, b_re, b_im, c_re, c_im, d_skip):
    g, p = lam_re.shape
    hs = b_re.shape[2]
    gt = S5_GROUP_TILE
    jn = g // gt
    lam_re = jnp.minimum(lam_re, -1e-4)
    step = jnp.exp(log_step)[:, None]
    mag = jnp.exp(lam_re * step)
    ab_re = mag * jnp.cos(lam_im * step)
    ab_im = mag * jnp.sin(lam_im * step)
    den = lam_re * lam_re + lam_im * lam_im
    f_re = ((ab_re - 1.0) * lam_re + ab_im * lam_im) / den
    f_im = (ab_im * lam_re - (ab_re - 1.0) * lam_im) / den
    bb_re = f_re[..., None] * b_re - f_im[..., None] * b_im
    bb_im = f_re[..., None] * b_im + f_im[..., None] * b_re
    a_l = jnp.concatenate([ab_re.reshape(jn, gt * p), ab_im.reshape(jn, gt * p)], axis=1).reshape(1, jn * 2 * gt * p)
    eye = jnp.eye(gt, dtype=F32)

    def blockdiag(m):
        return jnp.einsum('jahp,ab->jahbp', m.reshape(jn, gt, hs, p), eye).reshape(jn, gt * hs, gt * p)

    b_blk = jnp.concatenate([blockdiag(bb_re.transpose(0, 2, 1)), blockdiag(bb_im.transpose(0, 2, 1))], axis=2)
    c_blk = jnp.concatenate([blockdiag(c_re), blockdiag(-c_im)], axis=2)
    return a_l, b_blk, c_blk, d_skip.reshape(1, g * hs)


def _s5_scan_consts(a_l, cw):
    jn, hr = a_l.shape[1] // cw, cw // 2 // LANES
    a4 = a_l.reshape(jn, 2, hr, LANES)
    are, aim = a4[:, 0], a4[:, 1]
    flat = lambda u, v: jnp.concatenate([u, v], axis=1).reshape(jn * 2 * hr, LANES)
    return flat(are, are), flat(-aim, aim), flat(aim, -aim)


def _s5_da_lanes(p, q, cw):
    jn, hr = p.shape[0] * LANES // cw, cw // 2 // LANES
    p4, q4 = p.reshape(jn, 2, hr, LANES), q.reshape(jn, 2, hr, LANES)
    da_re = (p4[:, 0] + p4[:, 1]).reshape(jn, cw // 2)
    da_im = (q4[:, 1] - q4[:, 0]).reshape(jn, cw // 2)
    return jnp.concatenate([da_re, da_im], axis=1).reshape(1, jn * cw)


def _layer_dims(p):
    d_model = p['w_out'].shape[1]
    wa = p['proj_a'].shape[0]
    h = p['gdn_a_log'].shape[0]
    wb = p['proj_b'].shape[0]
    wc = p['proj_c'].shape[0]
    hm = p['m2_a_log'].shape[0]
    cdim = p['m2_conv_w'].shape[1]
    width = dict(zip(PROJ_ORDER, (3 * wa, wa, h, h, wb, wb, wc, cdim, hm, d_model, d_model, d_model)))
    n_in = sum(width.values())
    return width, n_in, -(-n_in // LANES) * LANES


PROJ_ORDER = ('qkv', 'az', 'braw', 'araw', 'su', 'sgate', 'cz', 'cxbc', 'cdt', 'ma', 'mb', 'mc')
WORK_ORDER = ('qkv', 'az', 'cz', 'su', 'sgate', 'cxbc', 'ma', 'mb', 'mc', 'braw', 'araw', 'cdt')
ROW_COL_TILE = 512


def _offsets(width, order):
    off, o = {}, 0
    for n in order:
        off[n] = o
        o += width[n]
    return off


def _virtual_cols(parts, lo, hi):
    wd = parts[0].shape[-1]
    out = []
    while lo < hi:
        j = lo // wd
        stop = min(hi, (j + 1) * wd)
        out.append(lax.slice_in_dim(parts[j], lo - j * wd, stop - j * wd, axis=parts[j].ndim - 1))
        lo = stop
    return out


def _work_order_weight(shards, width, n_pad):
    off = _offsets(width, PROJ_ORDER)
    parts = [s for n in WORK_ORDER for s in _virtual_cols(shards, off[n], off[n] + width[n])]
    used = sum(width.values())
    parts.append(jnp.zeros(shards[0].shape[:-1] + (n_pad - used,), shards[0].dtype))
    return jnp.concatenate(parts, axis=-1)


def _weight_order_shard(gw, width, lo, hi):
    src, dst = _offsets(width, WORK_ORDER), _offsets(width, PROJ_ORDER)
    parts = []
    for n in PROJ_ORDER:
        a, b = max(lo, dst[n]), min(hi, dst[n] + width[n])
        if a < b:
            parts.append(lax.slice_in_dim(gw, src[n] + a - dst[n], src[n] + b - dst[n], axis=gw.ndim - 1))
    return jnp.concatenate(parts, axis=-1)


def _layer_fwd(x, p):
    width, n_in, n_pad = _layer_dims(p)
    sv = {'x': x}
    h = _rowwise(_f_rms, [x], [p['norm_w'][None]], [x.shape[1]], [BF16], "rms_fwd")[0]
    w_in = _work_order_weight(p['w_in_shards'], width, n_pad)
    proj = _matmul(h, w_in, 'nn', F32, "in_proj", tn=IN_PROJ_TILE)
    wins, off = {}, 0
    for n in WORK_ORDER:
        wins[n] = Win(proj, off, width[n])
        off += width[n]
    qkv, az, cz, su, sgate, cxbc, ma, mb, mc = (wins[n] for n in WORK_ORDER[:9])
    braw, araw, cdt = (lax.slice_in_dim(proj, wins[n].off, wins[n].off + width[n], axis=1) for n in WORK_ORDER[9:])
    sv.update(h=h, w_in=w_in, qkv=qkv, az=az, braw=braw, araw=araw, su=su, sgate=sgate, cz=cz, cxbc=cxbc, cdt=cdt,
              ma=ma, mb=mb, mc=mc)
    gb0 = jnp.zeros((1, qkv.shape[1]), F32)
    sqkv = _conv_fwd(qkv, p['gdn_conv_w'], gb0, "gdn_conv_fwd")
    ya, ssa, tsa = _gdn_chunks_fwd(sqkv, az, braw, araw, p['gdn_a_log'][None], p['gdn_dt_bias'][None],
                                   p['gdn_norm_w'][None])
    sv.update(sqkv=sqkv, ssa=ssa, tsa=tsa, ya=ya)
    s5_in = tuple(p[k] for k in ('s5_lam_re', 's5_lam_im', 's5_log_step', 's5_b_re', 's5_b_im', 's5_c_re', 's5_c_im',
                                 's5_d'))
    (a_l, b_blk, c_blk, d_l), s5_vjp = jax.vjp(_s5_params, *s5_in)
    cw = b_blk.shape[2]
    aa3, bb3, bb3_conj = _s5_scan_consts(a_l, cw)
    bu = _bd_expand(su, b_blk, "s5_bu")
    s = _s5_scan(bu, aa3, bb3, False, "s5_scan_fwd")
    ymm = _bd_reduce(s, c_blk, F32, "s5_out")
    yg = _rowwise(_f_s5_post1, [ymm, su], [d_l], [su.shape[1]], [F32], "s5_post1_fwd", col_tile=ROW_COL_TILE)[0]
    tt = _matmul(yg, p['s5_glu_w'], 'nn', F32, "s5_glu")
    yb = _rowwise(_f_s5_post2, [yg, tt, sgate], [p['s5_glu_b'][None]], [su.shape[1]], [F32], "s5_post2_fwd",
                  col_tile=ROW_COL_TILE)[0]
    sv.update(aa3=aa3, bb3_conj=bb3_conj, b_blk=b_blk, c_blk=c_blk, d_l=d_l, s5_vjp=s5_vjp, s=s, ymm=ymm, yg=yg, tt=tt,
              yb=yb, cw=cw)
    sxbc = _conv_fwd(cxbc, p['m2_conv_w'], p['m2_conv_b'][None], "m2_conv_fwd")
    yc, ssc = _m2_chunks_fwd(sxbc, cz, cdt, p['m2_a_log'][None], p['m2_dt_bias'][None], p['m2_d'][None],
                             p['m2_norm_w'][None])
    sv.update(sxbc=sxbc, ssc=ssc, yc=yc)
    pa = _matmul(ya, p['proj_a'], 'nn', F32, "proj_a")
    pb = _matmul(yb, p['proj_b'], 'nn', F32, "proj_b")
    pc = _matmul(yc, p['proj_c'], 'nn', F32, "proj_c")
    merged = _rowwise(_f_merge, [ma, mb, mc, pa, pb, pc], [], [x.shape[1]], [BF16], "merge_fwd",
                      col_tile=ROW_COL_TILE)[0]
    x_next = _matmul(merged, p['w_out'], 'nn', F32, "out_proj", add=x)
    sv.update(pa=pa, pb=pb, pc=pc, merged=merged)
    return x_next, sv


def _layer_bwd(dx_out, p, sv):
    width, n_in, n_pad = _layer_dims(p)
    g = {}
    dmerged = _matmul(dx_out, p['w_out'], 'nt', F32, "out_proj_dx")
    g['w_out'] = _matmul(sv['merged'], dx_out, 'tn', F32, "out_proj_dw")
    dma, dmb, dmc, dpa, dpb, dpc = _rowwise_bwd(
        _f_merge, [sv['ma'], sv['mb'], sv['mc'], sv['pa'], sv['pb'], sv['pc']], [], [dmerged], [BF16] * 6,
        "merge_bwd", col_tile=ROW_COL_TILE)
    dya = _matmul(dpa, p['proj_a'], 'nt', F32, "proj_a_dx")
    dyb = _matmul(dpb, p['proj_b'], 'nt', F32, "proj_b_dx")
    dyc = _matmul(dpc, p['proj_c'], 'nt', F32, "proj_c_dx")
    g['proj_a'] = _matmul(sv['ya'], dpa, 'tn', F32, "proj_a_dw")
    g['proj_b'] = _matmul(sv['yb'], dpb, 'tn', F32, "proj_b_dw")
    g['proj_c'] = _matmul(sv['yc'], dpc, 'tn', F32, "proj_c_dw")
    alog, dtb, gnw = p['gdn_a_log'][None], p['gdn_dt_bias'][None], p['gdn_norm_w'][None]
    dsq, daz, db3, da3, dalog, ddtb, dgnw = _gdn_chunks_bwd(sv['sqkv'], sv['az'], sv['braw'], sv['araw'], alog, dtb, gnw,
                                                            sv['ssa'], sv['tsa'], dya)
    gb0 = jnp.zeros((1, sv['qkv'].shape[1]), F32)
    dqkv, g['gdn_conv_w'], _ = _conv_bwd(sv['qkv'], p['gdn_conv_w'], gb0, dsq, "gdn_conv_bwd")
    dbraw, daraw = jnp.sum(db3, axis=0), jnp.sum(da3, axis=0)
    g.update(gdn_a_log=dalog[0], gdn_dt_bias=ddtb[0], gdn_norm_w=dgnw[0])
    dsx, dcz, dcdt, dmalog, dmdtb, dmdsk, dmnw = _m2_chunks_bwd(
        sv['sxbc'], sv['cz'], sv['cdt'], p['m2_a_log'][None], p['m2_dt_bias'][None], p['m2_d'][None],
        p['m2_norm_w'][None], sv['ssc'], dyc)
    dcxbc, g['m2_conv_w'], dconvb = _conv_bwd(sv['cxbc'], p['m2_conv_w'], p['m2_conv_b'][None], dsx, "m2_conv_bwd")
    g.update(m2_conv_b=dconvb[0], m2_a_log=dmalog[0], m2_dt_bias=dmdtb[0], m2_d=dmdsk[0], m2_norm_w=dmnw[0])
    dyg1, dtt, dsgate, dglub = _rowwise_bwd(_f_s5_post2, [sv['yg'], sv['tt'], sv['sgate']], [p['s5_glu_b'][None]],
                                            [dyb], [F32, BF16, BF16], "s5_post2_bwd", col_tile=ROW_COL_TILE)
    dyg = _matmul(dtt, p['s5_glu_w'], 'nt', F32, "s5_glu_dx", add=dyg1)
    g['s5_glu_w'] = _matmul(sv['yg'], dtt, 'tn', F32, "s5_glu_dw")
    g['s5_glu_b'] = dglub[0]
    dymm, dsu1, dd_l = _rowwise_bwd(_f_s5_post1, [sv['ymm'], sv['su']], [sv['d_l']], [dyg], [BF16, F32],
                                    "s5_post1_bwd", col_tile=ROW_COL_TILE)
    gy = _bd_expand(dymm, sv['c_blk'], "s5_out_dx")
    ds = _s5_scan(gy, sv['aa3'], sv['bb3_conj'], True, "s5_scan_bwd")
    da_l = _s5_da_lanes(*_s5_da(ds, sv['s']), sv['cw'])
    dsu = _bd_reduce(ds, sv['b_blk'], BF16, "s5_bu_dx", add=dsu1)
    ka = sv['b_blk'].shape[1]
    db_blk = _bd_outer(sv['su'], ds, ka, "s5_bu_dw")
    dc_blk = _bd_outer(dymm, sv['s'], ka, "s5_out_dw")
    for k, v in zip(('s5_lam_re', 's5_lam_im', 's5_log_step', 's5_b_re', 's5_b_im', 's5_c_re', 's5_c_im', 's5_d'),
                    sv['s5_vjp']((da_l, db_blk, dc_blk, dd_l))):
        g[k] = v
    small = jnp.concatenate([dbraw, daraw, dcdt], axis=1).astype(BF16)
    small = jnp.pad(small, ((0, 0), (0, n_pad - n_in + sum(width[n] for n in WORK_ORDER[9:]) - small.shape[1])))
    dproj = _concat_cols([dqkv, daz, dcz, dsu, dsgate, dcxbc, dma, dmb, dmc, small], "concat_dproj")
    dh = _matmul(dproj, sv['w_in'], 'nt', F32, "in_proj_dx", tm=2048, tk=IN_PROJ_TILE)
    g['w_in'] = _matmul(sv['h'], dproj, 'tn', F32, "in_proj_dw", tn=IN_PROJ_TILE)
    dx, dnw = _rowwise_bwd(_f_rms, [sv['x']], [p['norm_w'][None]], [dh], [F32], "rms_bwd", addend=dx_out)
    g['norm_w'] = dnw[0]
    return dx, g


INPUT_NAMES = (['x'] + WEIGHT_NAMES + ['loss_target'] + ['m_' + n for n in WEIGHT_NAMES]
               + ['v_' + n for n in WEIGHT_NAMES])


def _step(d):
    xi, yi, ci = _here()
    me = 2 * xi + yi
    depth = d['norm_w'].shape[0]
    big, ssm = list(BIG), list(SHARDED_SMALL)
    nsh = 4
    full, gathered = {}, {}
    gots = None
    for first, stop in ((0, 1), (1, depth)):
        halves = [d[n][first:stop].astype(BF16).reshape(2, -1, d[n].shape[-1]) for n in big]
        if first == 0:
            gots = _gather_chips_split(halves, "gather_weights")
        else:
            halves, gots = lax.optimization_barrier((halves, gots))
            gots = _gather_chips_split_async(halves, "gather_weights_async", GATHER_COLLECTIVE_ID)
        for n, hv, got in zip(big, halves, gots):
            got = lax.dynamic_update_slice(got, hv[None], (me, 0, 0, 0))
            gathered[n, first] = got.reshape((nsh, stop - first) + d[n].shape[1:])
    cg = _gather_chips(_flat_pack([d[n] for n in ssm], F32, 8), "gather_conv_weights")
    parts = [_flat_unpack(cg[j], [d[n].shape for n in ssm]) for j in range(nsh)]
    for i, n in enumerate(ssm):
        full[n] = jnp.concatenate([parts[j][i] for j in range(nsh)], axis=SHARDED_SMALL[n])
    layer_names = [n for n in WEIGHT_NAMES if n != 'final_norm_w']

    def layer_params(l):
        p = {n: (full[n][l] if n in full else d[n][l]) for n in layer_names if n not in BIG}
        for n in big:
            got = gathered[n, 0][:, 0] if l == 0 else gathered[n, 1][:, l - 1]
            if n == 'w_in':
                p['w_in_shards'] = [got[j] for j in range(nsh)]
            else:
                p[n] = jnp.concatenate([got[j] for j in range(nsh)], axis=BIG[n] - 1)
        return p

    x = d['x'][0]
    saved, params = [], []
    for l in range(depth):
        params.append(layer_params(l))
        x, sv = _layer_fwd(x, params[l])
        saved.append(sv)
    loss11, dx, dfw = _loss_and_grad(x, d['loss_target'][0], d['final_norm_w'][None])
    loss = lax.psum(loss11[0, 0], ("x", "y", "c"))
    width_in = _layer_dims(params[0])[0]

    def shard(a, n, j):
        wd = d[n].shape[BIG[n]]
        if n == 'w_in':
            return _weight_order_shard(a, width_in, j * wd, (j + 1) * wd)
        return lax.slice_in_dim(a, j * wd, (j + 1) * wd, axis=BIG[n] - 1)

    c_idx = jnp.reshape(ci, (1,)).astype(jnp.int32)
    me_idx = jnp.reshape(me, (1,)).astype(jnp.int32)
    grads, own, got = [None] * depth, [None] * depth, [None] * depth
    for l in reversed(range(depth)):
        dx, grads[l] = _layer_bwd(dx, params[l], saved[l])
        g4 = [jnp.stack([shard(grads[l][n], n, j) for j in range(nsh)]).reshape(nsh, 2, -1, d[n].shape[-1])
              for n in big]
        pairs = [_add_my_half(g, r, c_idx, "add_my_half_" + n) for n, g, r in zip(big, g4, _swap_sibling_half(g4))]
        own[l] = [pf for pf, _ in pairs]
        got[l] = _scatter_chips_async([pb for _, pb in pairs], "scatter_chips_async_%d" % l, SCATTER_COLLECTIVE_ID + l)
    gfull = {n: jnp.stack([grads[l][n] for l in range(depth)]) for n in layer_names if n not in BIG}
    gfull['final_norm_w'] = dfw[0]
    out = {}
    small = [n for n in WEIGHT_NAMES if n not in BIG]
    sshapes = [gfull[n].shape for n in small]
    gsm = _sum_parts(_gather_all(_flat_pack([gfull[n] for n in small], F32, 8), "gather_small_grads"), "sum_devices")
    gs = dict(zip(small, _flat_unpack(gsm, sshapes)))
    for n in ssm:
        wd = d[n].shape[SHARDED_SMALL[n]]
        gs[n] = lax.dynamic_slice_in_dim(gs[n], me * wd, wd, axis=SHARDED_SMALL[n])
    lshapes = [d[n].shape for n in small]
    wps, gps, mps, vps = (_flat_pack(arrs, F32, 16) for arrs in (
        [d[n] for n in small], [gs[n] for n in small], [d['m_' + n] for n in small], [d['v_' + n] for n in small]))
    dl, nm, nv = _adamw(wps, gps, mps, vps, "adamw_small")
    for key, arr in (('grad_', gps), ('delta_', dl), ('new_m_', nm), ('new_v_', nv)):
        for n, a in zip(small, _flat_unpack(arr, lshapes)):
            out[key + n] = a
    got[1:], dx = lax.optimization_barrier((got[1:], dx))
    mine = [None] * (depth * len(big))
    for l in list(range(1, depth)) + [0]:
        if l == 0:
            got[0], _ = lax.optimization_barrier((got[0], (dl, mine[len(big):])))
        for i, (n, pf, gt) in enumerate(zip(big, own[l], got[l])):
            mine[l * len(big) + i] = _sum_chips(pf, gt, me_idx, "sum_chips_" + n)
    theirs = _share_sibling(mine)
    for i, n in enumerate(big):
        layers = []
        for l in range(depth):
            mn, th = mine[l * len(big) + i], theirs[l * len(big) + i]
            layers.append(jnp.where(ci == 0, jnp.stack([mn, th]), jnp.stack([th, mn])))
        g2 = jnp.stack(layers).reshape(-1, d[n].shape[-1])
        w2, m2, v2 = (d[pre + n].reshape(g2.shape) for pre in ('', 'm_', 'v_'))
        dl, nm, nv = _adamw(w2, g2, m2, v2, "adamw_" + n)
        for key, arr in (('grad_', g2), ('delta_', dl), ('new_m_', nm), ('new_v_', nv)):
            out[key + n] = arr.reshape(d[n].shape)
    res = [loss, dx[None]]
    for key in ('grad_', 'delta_', 'new_m_', 'new_v_'):
        res += [out[key + n] for n in WEIGHT_NAMES]
    return tuple(res)


def kernel(x, norm_w, w_in, gdn_conv_w, gdn_a_log, gdn_dt_bias, gdn_norm_w, s5_lam_re, s5_lam_im, s5_log_step, s5_b_re, s5_b_im, s5_c_re, s5_c_im, s5_d, s5_glu_w, s5_glu_b, m2_conv_w, m2_conv_b, m2_a_log, m2_dt_bias, m2_d, m2_norm_w, proj_a, proj_b, proj_c, w_out, final_norm_w, loss_target, m_norm_w, m_w_in, m_gdn_conv_w, m_gdn_a_log, m_gdn_dt_bias, m_gdn_norm_w, m_s5_lam_re, m_s5_lam_im, m_s5_log_step, m_s5_b_re, m_s5_b_im, m_s5_c_re, m_s5_c_im, m_s5_d, m_s5_glu_w, m_s5_glu_b, m_m2_conv_w, m_m2_conv_b, m_m2_a_log, m_m2_dt_bias, m_m2_d, m_m2_norm_w, m_proj_a, m_proj_b, m_proj_c, m_w_out, m_final_norm_w, v_norm_w, v_w_in, v_gdn_conv_w, v_gdn_a_log, v_gdn_dt_bias, v_gdn_norm_w, v_s5_lam_re, v_s5_lam_im, v_s5_log_step, v_s5_b_re, v_s5_b_im, v_s5_c_re, v_s5_c_im, v_s5_d, v_s5_glu_w, v_s5_glu_b, v_m2_conv_w, v_m2_conv_b, v_m2_a_log, v_m2_dt_bias, v_m2_d, v_m2_norm_w, v_proj_a, v_proj_b, v_proj_c, v_w_out, v_final_norm_w):
    args = (x, norm_w, w_in, gdn_conv_w, gdn_a_log, gdn_dt_bias, gdn_norm_w, s5_lam_re, s5_lam_im, s5_log_step, s5_b_re, s5_b_im, s5_c_re, s5_c_im, s5_d, s5_glu_w, s5_glu_b, m2_conv_w, m2_conv_b, m2_a_log, m2_dt_bias, m2_d, m2_norm_w, proj_a, proj_b, proj_c, w_out, final_norm_w, loss_target, m_norm_w, m_w_in, m_gdn_conv_w, m_gdn_a_log, m_gdn_dt_bias, m_gdn_norm_w, m_s5_lam_re, m_s5_lam_im, m_s5_log_step, m_s5_b_re, m_s5_b_im, m_s5_c_re, m_s5_c_im, m_s5_d, m_s5_glu_w, m_s5_glu_b, m_m2_conv_w, m_m2_conv_b, m_m2_a_log, m_m2_dt_bias, m_m2_d, m_m2_norm_w, m_proj_a, m_proj_b, m_proj_c, m_w_out, m_final_norm_w, v_norm_w, v_w_in, v_gdn_conv_w, v_gdn_a_log, v_gdn_dt_bias, v_gdn_norm_w, v_s5_lam_re, v_s5_lam_im, v_s5_log_step, v_s5_b_re, v_s5_b_im, v_s5_c_re, v_s5_c_im, v_s5_d, v_s5_glu_w, v_s5_glu_b, v_m2_conv_w, v_m2_conv_b, v_m2_a_log, v_m2_dt_bias, v_m2_d, v_m2_norm_w, v_proj_a, v_proj_b, v_proj_c, v_w_out, v_final_norm_w)
    return _step(dict(zip(INPUT_NAMES, args)))
```

```python
import math
from typing import NamedTuple

import jax
import jax.numpy as jnp
import numpy as np
from jax import lax
from jax.experimental import pallas as pl
from jax.experimental.pallas import tpu as pltpu
from jax.experimental.pallas import tpu_sc as plsc

F32 = jnp.float32
BF16 = jnp.bfloat16
HI = lax.Precision.HIGH
MESH = pl.DeviceIdType.MESH

CHUNK = 64
CONV_K = 4
NORM_EPS = 1e-6
GDN_HEAD_DIM = 128
M2_HEAD_DIM = 64
M2_STATE = 128
M2_GROUPS = 4
S5_GROUP_TILE = 8
ADAM_LR = 0.001
ADAM_B1 = 0.9
ADAM_B2 = 0.999
ADAM_EPS = 1e-08
ADAM_WD = 0.01
ADAM_STEP = 10
LANES = 128
SUBLANES = 8
VMEM_LIMIT_BYTES = 56 * 1024 * 1024

WEIGHT_NAMES = ['norm_w', 'w_in', 'gdn_conv_w', 'gdn_a_log', 'gdn_dt_bias', 'gdn_norm_w', 's5_lam_re', 's5_lam_im',
                's5_log_step', 's5_b_re', 's5_b_im', 's5_c_re', 's5_c_im', 's5_d', 's5_glu_w', 's5_glu_b',
                'm2_conv_w', 'm2_conv_b', 'm2_a_log', 'm2_dt_bias', 'm2_d', 'm2_norm_w', 'proj_a', 'proj_b',
                'proj_c', 'w_out', 'final_norm_w']
BIG = {'w_in': 2, 'proj_a': 2, 'proj_b': 2, 'proj_c': 2, 'w_out': 1, 's5_glu_w': 1}
SHARDED_SMALL = {'gdn_conv_w': 2, 'm2_conv_w': 2}


def _call(body, *, name, out_shape, grid=None, in_specs=None, out_specs=None, scratch_shapes=(), semantics=None,
          num_scalar_prefetch=None):
    params = dict(vmem_limit_bytes=VMEM_LIMIT_BYTES)
    if semantics is not None:
        params['dimension_semantics'] = semantics
    kw = dict(name=name, out_shape=out_shape, compiler_params=pltpu.CompilerParams(**params))
    if num_scalar_prefetch is not None:
        kw['grid_spec'] = pltpu.PrefetchScalarGridSpec(num_scalar_prefetch=num_scalar_prefetch, grid=grid,
                                                       in_specs=in_specs, out_specs=out_specs,
                                                       scratch_shapes=scratch_shapes)
    else:
        if grid is not None:
            kw['grid'] = grid
        if in_specs is not None:
            kw['in_specs'] = in_specs
        if out_specs is not None:
            kw['out_specs'] = out_specs
        if scratch_shapes:
            kw['scratch_shapes'] = scratch_shapes
    return pl.pallas_call(body, **kw)


def _tile(n, target, unit):
    if n <= target:
        return n
    t = (target // unit) * unit
    while t >= unit:
        if n % t == 0:
            return t
        t -= unit
    raise ValueError(f"no tile for {n} (unit {unit}, target {target})")


def _sds(shape, dtype):
    return jax.ShapeDtypeStruct(tuple(shape), dtype)


def _sigmoid(x):
    return jax.nn.sigmoid(x)


def _silu(x):
    return x * jax.nn.sigmoid(x)


def _softplus(x):
    return jnp.maximum(x, 0.0) + jnp.log(1.0 + jnp.exp(-jnp.abs(x)))


def _gelu_tanh(x):
    return 0.5 * x * (1.0 + jnp.tanh(math.sqrt(2.0 / math.pi) * (x + 0.044715 * (x * x * x))))


def _dot(a, b, dims, prec=None):
    return lax.dot_general(a, b, (dims, ((), ())), precision=prec, preferred_element_type=F32)


def _nn(a, b, prec=None):
    return _dot(a, b, ((1,), (0,)), prec)


def _nt(a, b, prec=None):
    return _dot(a, b, ((1,), (1,)), prec)


def _tn(a, b, prec=None):
    return _dot(a, b, ((0,), (0,)), prec)


IN_PROJ_TILE = 1664
MATMUL_TILES = {'nn': (1024, 1024, 2048), 'nt': (1024, 1024, 2048), 'tn': (1024, 1024, 2048)}


def _matmul(a, b, mode, out_dtype, name, tm=None, tn=None, tk=None, add=None):
    tm, tn, tk = (t if t is not None else dflt for t, dflt in zip((tm, tn, tk), MATMUL_TILES[mode]))
    if mode == 'nn':
        (m, k), (k2, n) = a.shape, b.shape
    elif mode == 'nt':
        (m, k), (n, k2) = a.shape, b.shape
    else:
        (k, m), (k2, n) = a.shape, b.shape
    assert k == k2, (a.shape, b.shape, mode)
    tm = _tile(m, tm, LANES if mode == 'tn' else SUBLANES)
    tn = _tile(n, tn, LANES)
    tk = _tile(k, tk, LANES if mode != 'tn' else SUBLANES * 2)
    nk = k // tk
    dims = {'nn': ((1,), (0,)), 'nt': ((1,), (1,)), 'tn': ((0,), (0,))}[mode]

    def body(*refs):
        a_ref, b_ref = refs[:2]
        o_ref, acc_ref = refs[-2:]
        kk = pl.program_id(2)

        @pl.when(kk == 0)
        def _():
            acc_ref[...] = jnp.zeros_like(acc_ref)

        acc_ref[...] += _dot(a_ref[...].astype(BF16), b_ref[...].astype(BF16), dims)

        @pl.when(kk == nk - 1)
        def _():
            res = acc_ref[...]
            if add is not None:
                res = res + refs[2][...].astype(F32)
            o_ref[...] = res.astype(o_ref.dtype)

    a_spec = pl.BlockSpec((tk, tm), lambda i, j, kk: (kk, i)) if mode == 'tn' else pl.BlockSpec((tm, tk), lambda i, j, kk: (i, kk))
    b_spec = pl.BlockSpec((tn, tk), lambda i, j, kk: (j, kk)) if mode == 'nt' else pl.BlockSpec((tk, tn), lambda i, j, kk: (kk, j))
    o_spec = pl.BlockSpec((tm, tn), lambda i, j, kk: (i, j))
    ops = (a, b) if add is None else (a, b, add)
    return _call(body, name=name, out_shape=_sds((m, n), out_dtype), grid=(m // tm, n // tn, nk),
                 in_specs=[a_spec, b_spec] + ([] if add is None else [o_spec]), out_specs=o_spec,
                 scratch_shapes=[pltpu.VMEM((tm, tn), F32)], semantics=("parallel", "parallel", "arbitrary"))(*ops)


class Win(NamedTuple):
    arr: jax.Array
    off: int
    width: int

    @property
    def shape(self):
        return (self.arr.shape[0], self.width)


def _win(x):
    return x if isinstance(x, Win) else Win(x, 0, x.shape[1])


def _col_tile(wins, target):
    ct = (min(target, min(w.width for w in wins)) // LANES) * LANES
    while ct > LANES and any(w.width % ct or w.off % ct for w in wins):
        ct -= LANES
    assert all(w.width % ct == 0 and w.off % ct == 0 for w in wins), [(w.off, w.width) for w in wins]
    return ct


def _wspec(rows, ct, w, row_first=True):
    base = w.off // ct
    if row_first:
        return pl.BlockSpec((rows, ct), lambda i, j: (i, base + j))
    return pl.BlockSpec((rows, ct), lambda j, i: (i, base + j))


def _bd_expand(a, b, name, tm=1024):
    a = _win(a)
    t = a.shape[0]
    jn, ka, nb = b.shape
    r = nb // LANES
    tm = _tile(t, tm, SUBLANES)
    assert a.shape[1] == jn * ka and a.off % ka == 0
    abase = a.off // ka

    def body(a_ref, b_ref, o_ref):
        o_ref[...] = _nn(a_ref[...].astype(BF16), b_ref[...].astype(BF16)).reshape(tm, r, LANES)

    return _call(body, name=name, out_shape=_sds((t, jn * r, LANES), F32), grid=(t // tm, jn),
                 in_specs=[pl.BlockSpec((tm, ka), lambda i, j: (i, abase + j)),
                           pl.BlockSpec((None, ka, nb), lambda i, j: (j, 0, 0))],
                 out_specs=pl.BlockSpec((tm, r, LANES), lambda i, j: (i, j, 0)),
                 semantics=("parallel", "parallel"))(a.arr, b)


def _bd_reduce(a3, b, out_dtype, name, tm=1024, add=None):
    t = a3.shape[0]
    jn, ka, nb = b.shape
    r = nb // LANES
    tm = _tile(t, tm, SUBLANES)

    def body(*refs):
        a_ref, b_ref, o_ref = refs[0], refs[1], refs[-1]
        res = _nt(a_ref[...].reshape(tm, nb).astype(BF16), b_ref[...].astype(BF16))
        if add is not None:
            res = res + refs[2][...].astype(F32)
        o_ref[...] = res.astype(o_ref.dtype)

    o_spec = pl.BlockSpec((tm, ka), lambda i, j: (i, j))
    ops = (a3, b) if add is None else (a3, b, add)
    return _call(body, name=name, out_shape=_sds((t, jn * ka), out_dtype), grid=(t // tm, jn),
                 in_specs=[pl.BlockSpec((tm, r, LANES), lambda i, j: (i, j, 0)),
                           pl.BlockSpec((None, ka, nb), lambda i, j: (j, 0, 0))] + ([] if add is None else [o_spec]),
                 out_specs=o_spec, semantics=("parallel", "parallel"))(*ops)


def _bd_outer(a, b3, ka, name, tk=1024):
    a = _win(a)
    t = a.shape[0]
    jn = a.shape[1] // ka
    r = b3.shape[1] // jn
    nb = r * LANES
    assert a.off % ka == 0
    abase = a.off // ka
    tk = _tile(t, tk, SUBLANES * 2)

    def body(a_ref, b_ref, o_ref):
        @pl.when(pl.program_id(1) == 0)
        def _():
            o_ref[...] = jnp.zeros_like(o_ref)

        o_ref[...] += _tn(a_ref[...].astype(BF16), b_ref[...].reshape(tk, nb).astype(BF16))

    return _call(body, name=name, out_shape=_sds((jn, ka, nb), F32), grid=(jn, t // tk),
                 in_specs=[pl.BlockSpec((tk, ka), lambda j, kk: (kk, abase + j)),
                           pl.BlockSpec((tk, r, LANES), lambda j, kk: (kk, j, 0))],
                 out_specs=pl.BlockSpec((None, ka, nb), lambda j, kk: (j, 0, 0)),
                 semantics=("parallel", "arbitrary"))(a.arr, b3)


def _concat_cols(parts, name, tb=256):
    t = parts[0].shape[0]
    tb = _tile(t, tb, SUBLANES * 2)
    widths = [p.shape[1] for p in parts]
    assert all(w % LANES == 0 for w in widths)

    def body(*refs):
        o_ref, off = refs[-1], 0
        for r, w in zip(refs[:-1], widths):
            o_ref[:, off:off + w] = r[...]
            off += w

    return _call(body, name=name, out_shape=_sds((t, sum(widths)), parts[0].dtype), grid=(t // tb,),
                 in_specs=[pl.BlockSpec((tb, w), lambda i: (i, 0)) for w in widths],
                 out_specs=pl.BlockSpec((tb, sum(widths)), lambda i: (i, 0)), semantics=("parallel",))(*parts)


def _rowwise_tiles(rows, tb, col_tile):
    rows = [_win(r) for r in rows]
    t = rows[0].shape[0]
    tb = _tile(t, tb, SUBLANES * 2)
    if col_tile is None:
        assert all(r.off % r.width == 0 for r in rows)
        return rows, tb, None, 1
    ct = _col_tile(rows, col_tile)
    tb = _tile(t, max(tb, BLOCK_BYTES // (4 * ct)), SUBLANES * 2)
    return rows, tb, ct, rows[0].width // ct


def _rowwise(fn, rows, params, out_widths, out_dtypes, name, tb=256, col_tile=None):
    rows, tb, ct, ncol = _rowwise_tiles(rows, tb, col_tile)
    t = rows[0].shape[0]
    nr, npar = len(rows), len(params)

    def body(*refs):
        ins = [r[...].astype(F32) for r in refs[:nr + npar]]
        outs = fn(*ins)
        for o_ref, o in zip(refs[nr + npar:], outs):
            o_ref[...] = o.astype(o_ref.dtype)

    in_specs = [_wspec(tb, ct or r.width, r) for r in rows]
    in_specs += [pl.BlockSpec((1, ct or p.shape[1]), lambda i, j: (0, j)) for p in params]
    out_shape = [_sds((t, w), d) for w, d in zip(out_widths, out_dtypes)]
    out_specs = [pl.BlockSpec((tb, ct or w), lambda i, j: (i, j)) for w in out_widths]
    return _call(body, name=name, out_shape=out_shape, grid=(t // tb, ncol), in_specs=in_specs, out_specs=out_specs,
                 semantics=("parallel", "parallel"))(*[r.arr for r in rows], *params)


def _rowwise_bwd(fn, rows, params, cts, row_grad_dtypes, name, tb=256, addend=None, col_tile=None):
    rows, tb, ct, ncol = _rowwise_tiles(rows, tb, col_tile)
    t = rows[0].shape[0]
    nr, npar, nc = len(rows), len(params), len(cts)
    keep = [i for i, d in enumerate(row_grad_dtypes) if d is not None]
    nadd = 0 if addend is None else 1

    def body(*refs):
        ins = [r[...].astype(F32) for r in refs[:nr + npar]]
        ct = [r[...].astype(F32) for r in refs[nr + npar:nr + npar + nc]]
        _, vjp = jax.vjp(fn, *ins)
        grads = vjp(tuple(ct))
        out_refs = refs[nr + npar + nc + nadd:]
        for o_ref, i in zip(out_refs[:len(keep)], keep):
            g = grads[i]
            if nadd and i == 0:
                g = g + refs[nr + npar + nc][...].astype(F32)
            o_ref[...] = g.astype(o_ref.dtype)

        @pl.when(pl.program_id(1) == 0)
        def _():
            for o_ref in out_refs[len(keep):]:
                o_ref[...] = jnp.zeros_like(o_ref)

        for o_ref, g in zip(out_refs[len(keep):], grads[nr:]):
            o_ref[...] += g

    def plain(w):
        return pl.BlockSpec((tb, ct or w), lambda j, i: (i, j))

    in_specs = [_wspec(tb, ct or r.width, r, row_first=False) for r in rows]
    in_specs += [pl.BlockSpec((1, ct or p.shape[1]), lambda j, i: (0, j)) for p in params]
    in_specs += [plain(c.shape[1]) for c in cts]
    extra = []
    if nadd:
        in_specs += [plain(addend.shape[1])]
        extra = [addend]
    out_shape = [_sds(rows[i].shape, row_grad_dtypes[i]) for i in keep] + [_sds(p.shape, F32) for p in params]
    out_specs = [plain(rows[i].width) for i in keep]
    out_specs += [pl.BlockSpec((1, ct or p.shape[1]), lambda j, i: (0, j)) for p in params]
    return _call(body, name=name, out_shape=out_shape, grid=(ncol, t // tb), in_specs=in_specs, out_specs=out_specs,
                 semantics=("parallel", "arbitrary"))(*[r.arr for r in rows], *params, *cts, *extra)


def _chunk_masks(c):
    row = lax.broadcasted_iota(jnp.int32, (c, c), 0)
    col = lax.broadcasted_iota(jnp.int32, (c, c), 1)
    causal = row >= col
    strict = row > col
    return causal, strict, causal.astype(F32), (row > col).astype(F32), (row == col).astype(F32)


def _lane_pick(blk, idx):
    lane = lax.broadcasted_iota(jnp.int32, blk.shape, 1)
    return jnp.sum(jnp.where(lane == idx, blk, 0.0), axis=1, keepdims=True)


def _bdot(a, b, ca, cb, prec=None):
    return lax.dot_general(a, b, (((ca,), (cb,)), ((0,), (0,))), precision=prec, preferred_element_type=F32)


def _bnn(a, b, prec=None):
    return _bdot(a, b, 2, 1, prec)


def _bnt(a, b, prec=None):
    return _bdot(a, b, 2, 2, prec)


def _btn(a, b, prec=None):
    return _bdot(a, b, 1, 1, prec)


def _unit_lower_inverse(a_mat, eye):
    x = -a_mat
    t_inv = eye + x
    p = x
    for _ in range(int(math.log2(a_mat.shape[-1])) - 1):
        p = _bnn(p, p, HI)
        t_inv = t_inv + _bnn(t_inv, p, HI)
    return t_inv


@jax.custom_vjp
def _saved_inverse(a_mat, t_saved):
    return t_saved


def _saved_inverse_fwd(a_mat, t_saved):
    return t_saved, t_saved


def _saved_inverse_bwd(t_inv, ct):
    return -_bnt(_btn(t_inv, ct, HI), t_inv, HI), jnp.zeros_like(t_inv)


_saved_inverse.defvjp(_saved_inverse_fwd, _saved_inverse_bwd)


def _gdn_heads(q, k, v, z, braw, araw, alog, dtb, nw, s_in, t_saved=None):
    b, c, d = q.shape
    causal, strict, lower, upper_t, eye = _chunk_masks(c)
    lower_b = jnp.broadcast_to(lower[None], (b, c, c))
    qn = q * lax.rsqrt(jnp.sum(q * q, axis=-1, keepdims=True) + NORM_EPS) * (d ** -0.5)
    kn = k * lax.rsqrt(jnp.sum(k * k, axis=-1, keepdims=True) + NORM_EPS)
    beta = _sigmoid(braw)
    g = -jnp.exp(alog) * _softplus(araw + dtb)
    dlog = _bnn(lower_b, g * upper_t[None], HI)
    dm = jnp.where(causal[None], jnp.exp(dlog), 0.0)
    g_lanes = jnp.broadcast_to(g, (b, c, d))
    gc = _bnn(lower_b, g_lanes, HI)
    gl = jnp.sum(g_lanes, axis=1, keepdims=True)
    eg = jnp.exp(gc)
    kb = kn * beta
    a_mat = jnp.where(strict[None], _bnt(kb, kn) * dm, 0.0)
    t_inv = _unit_lower_inverse(a_mat, eye[None]) if t_saved is None else _saved_inverse(a_mat, t_saved)
    r = beta * (v - eg * _bnn(kn, s_in))
    v_new = _bnn(t_inv, r)
    qk = _bnt(qn, kn) * dm
    out = eg * _bnn(qn, s_in) + _bnn(qk, v_new)
    k_tail = kn * jnp.exp(gl - gc)
    s_out = s_in * jnp.exp(gl) + _btn(k_tail, v_new)
    y = out * lax.rsqrt(jnp.mean(out * out, axis=-1, keepdims=True) + NORM_EPS) * nw[None] * _silu(z)
    if t_saved is None:
        return y, s_out, t_inv
    return y, s_out


def _gdn_stack(refs, hb, d, h, first_head):
    q_ref, k_ref, v_ref, z_ref, b_ref, a_ref, alog_ref, dtb_ref = refs
    sls = [slice(i * d, (i + 1) * d) for i in range(hb)]
    heads = [first_head + i for i in range(hb)]
    wide = [jnp.stack([r[:, sl] for sl in sls]) for r in (q_ref, k_ref, v_ref, z_ref)]
    cols = [jnp.stack([_lane_pick(r[...], hd) for hd in heads]) for r in (b_ref, a_ref, alog_ref, dtb_ref)]
    return wide + cols


GDN_HEADS_PER_STEP = 8


def _gdn_chunks_fwd(sqkv, z, braw, araw, alog, dtb, nw, hb=GDN_HEADS_PER_STEP):
    t, w3 = sqkv.shape
    w = w3 // 3
    d = GDN_HEAD_DIM
    h = w // d
    hb = min(hb, h)
    hg = h // hb
    c = CHUNK
    nc = t // c

    def body(q_ref, k_ref, v_ref, z_ref, b_ref, a_ref, alog_ref, dtb_ref, nw_ref, y_ref, ssave_ref, tsave_ref,
             s_ref):
        @pl.when(pl.program_id(1) == 0)
        def _():
            s_ref[...] = jnp.zeros_like(s_ref)

        s_in = s_ref[...]
        ssave_ref[...] = s_in
        args = _gdn_stack((q_ref, k_ref, v_ref, z_ref, b_ref, a_ref, alog_ref, dtb_ref), hb, d, h,
                          pl.program_id(0) * hb)
        y, s_out, t_inv = _gdn_heads(*args, nw_ref[...], s_in)
        for i in range(hb):
            y_ref[:, i * d:(i + 1) * d] = y[i]
        s_ref[...] = s_out
        tsave_ref[...] = t_inv

    blk = (c, hb * d)
    z = _win(z)
    zb = z.off // (hb * d)
    assert z.off % (hb * d) == 0
    in_specs = [pl.BlockSpec(blk, lambda g, n: (n, g)), pl.BlockSpec(blk, lambda g, n: (n, hg + g)),
                pl.BlockSpec(blk, lambda g, n: (n, 2 * hg + g)), pl.BlockSpec(blk, lambda g, n: (n, zb + g)),
                pl.BlockSpec((c, h), lambda g, n: (n, 0)), pl.BlockSpec((c, h), lambda g, n: (n, 0)),
                pl.BlockSpec((1, h), lambda g, n: (0, 0)), pl.BlockSpec((1, h), lambda g, n: (0, 0)),
                pl.BlockSpec((1, d), lambda g, n: (0, 0))]
    out_shape = [_sds((t, w), F32), _sds((hg, nc, hb, d, d), F32), _sds((hg, nc, hb, c, c), F32)]
    out_specs = [pl.BlockSpec(blk, lambda g, n: (n, g)),
                 pl.BlockSpec((None, None, hb, d, d), lambda g, n: (g, n, 0, 0, 0)),
                 pl.BlockSpec((None, None, hb, c, c), lambda g, n: (g, n, 0, 0, 0))]
    return _call(body, name="gdn_chunks_fwd", out_shape=out_shape, grid=(hg, nc), in_specs=in_specs,
                 out_specs=out_specs, scratch_shapes=[pltpu.VMEM((hb, d, d), F32)],
                 semantics=("parallel", "arbitrary"))(sqkv, sqkv, sqkv, z.arr, braw, araw, alog, dtb, nw)


def _gdn_chunks_bwd(sqkv, z, braw, araw, alog, dtb, nw, ssave, tsave, dy, hb=GDN_HEADS_PER_STEP):
    t, w3 = sqkv.shape
    w = w3 // 3
    d = GDN_HEAD_DIM
    h = w // d
    hb = min(hb, h)
    hg = h // hb
    c = CHUNK
    nc = t // c

    def body(q_ref, k_ref, v_ref, z_ref, b_ref, a_ref, alog_ref, dtb_ref, nw_ref, ssave_ref, tsave_ref, dy_ref,
             dsq_ref, dz_ref, db_ref, da_ref, dalog_ref, ddtb_ref, dnw_ref, ds_ref):
        first = jnp.logical_and(pl.program_id(0) == 0, pl.program_id(1) == 0)

        @pl.when(pl.program_id(1) == 0)
        def _():
            ds_ref[...] = jnp.zeros_like(ds_ref)

        @pl.when(first)
        def _():
            dalog_ref[...] = jnp.zeros_like(dalog_ref)
            ddtb_ref[...] = jnp.zeros_like(ddtb_ref)
            dnw_ref[...] = jnp.zeros_like(dnw_ref)

        lane_h = lax.broadcasted_iota(jnp.int32, (1, h), 1)
        db_acc = jnp.zeros((c, h), F32)
        da_acc = jnp.zeros((c, h), F32)
        args = _gdn_stack((q_ref, k_ref, v_ref, z_ref, b_ref, a_ref, alog_ref, dtb_ref), hb, d, h,
                          pl.program_id(0) * hb)
        t_saved = tsave_ref[...]
        _, vjp = jax.vjp(lambda *a: _gdn_heads(*a, t_saved=t_saved), *args, nw_ref[...], ssave_ref[...])
        dyb = jnp.stack([dy_ref[:, i * d:(i + 1) * d] for i in range(hb)])
        dq, dk, dv, dz, db, da, dalog, ddtb, dnw, ds_in = vjp((dyb, ds_ref[...]))
        for i in range(hb):
            sl = slice(i * d, (i + 1) * d)
            dsq_ref[:, i * d:(i + 1) * d] = dq[i]
            dsq_ref[:, w + i * d:w + (i + 1) * d] = dk[i]
            dsq_ref[:, 2 * w + i * d:2 * w + (i + 1) * d] = dv[i]
            dz_ref[:, sl] = dz[i].astype(dz_ref.dtype)
            onehot = (lane_h == pl.program_id(0) * hb + i).astype(F32)
            db_acc = db_acc + db[i] * onehot
            da_acc = da_acc + da[i] * onehot
            dalog_ref[...] += dalog[i] * onehot
            ddtb_ref[...] += ddtb[i] * onehot
        dnw_ref[...] += dnw
        ds_ref[...] = ds_in
        db_ref[...] = db_acc
        da_ref[...] = da_acc

    blk = (c, hb * d)
    rev = lambda n: nc - 1 - n
    z = _win(z)
    zb = z.off // (hb * d)
    assert z.off % (hb * d) == 0
    in_specs = [pl.BlockSpec(blk, lambda g, n: (rev(n), g)), pl.BlockSpec(blk, lambda g, n: (rev(n), hg + g)),
                pl.BlockSpec(blk, lambda g, n: (rev(n), 2 * hg + g)), pl.BlockSpec(blk, lambda g, n: (rev(n), zb + g)),
                pl.BlockSpec((c, h), lambda g, n: (rev(n), 0)), pl.BlockSpec((c, h), lambda g, n: (rev(n), 0)),
                pl.BlockSpec((1, h), lambda g, n: (0, 0)), pl.BlockSpec((1, h), lambda g, n: (0, 0)),
                pl.BlockSpec((1, d), lambda g, n: (0, 0)),
                pl.BlockSpec((None, None, hb, d, d), lambda g, n: (g, rev(n), 0, 0, 0)),
                pl.BlockSpec((None, None, hb, c, c), lambda g, n: (g, rev(n), 0, 0, 0)),
                pl.BlockSpec(blk, lambda g, n: (rev(n), g))]
    assert hg == 1
    out_shape = [_sds((t, w3), F32), _sds((t, w), BF16),
                 _sds((hg, t, h), F32), _sds((hg, t, h), F32), _sds((1, h), F32), _sds((1, h), F32), _sds((1, d), F32)]
    out_specs = [pl.BlockSpec((c, w3), lambda g, n: (rev(n), 0)), pl.BlockSpec(blk, lambda g, n: (rev(n), g))]
    out_specs += [pl.BlockSpec((None, c, h), lambda g, n: (g, rev(n), 0))] * 2
    out_specs += [pl.BlockSpec((1, h), lambda g, n: (0, 0)), pl.BlockSpec((1, h), lambda g, n: (0, 0)),
                  pl.BlockSpec((1, d), lambda g, n: (0, 0))]
    return _call(body, name="gdn_chunks_bwd", out_shape=out_shape, grid=(hg, nc), in_specs=in_specs,
                 out_specs=out_specs, scratch_shapes=[pltpu.VMEM((hb, d, d), F32)],
                 semantics=("arbitrary", "arbitrary"))(sqkv, sqkv, sqkv, z.arr, braw, araw, alog, dtb, nw, ssave, tsave,
                                                       dy)


def _m2_groups(xs, z, bm, cm, dtraws, alogs, dtbs, dsks, nw, st):
    g, c, gw = xs.shape
    rep = len(dtraws)
    causal, _, lower, upper_t, _ = _chunk_masks(c)
    lower_b = jnp.broadcast_to(lower[None], (g, c, c))
    lane_head = lax.broadcasted_iota(jnp.int32, (1, 1, gw), 2) // M2_HEAD_DIM

    def expand(cols):
        res = jnp.broadcast_to(cols[-1], (g, cols[-1].shape[1], gw))
        for i in reversed(range(rep - 1)):
            res = jnp.where(lane_head == i, cols[i], res)
        return res

    dts = [_softplus(dtraws[i] + dtbs[i]) for i in range(rep)]
    adts = [-jnp.exp(alogs[i]) * dts[i] for i in range(rep)]
    dt_l, adt_l, dsk_l = expand(dts), expand(adts), expand(dsks)
    xdt = xs * dt_l
    acum = _bnn(lower_b, adt_l, HI)
    alast = jnp.sum(adt_l, axis=1, keepdims=True)
    scores = _bnt(cm, bm)
    y = jnp.exp(acum) * _bnn(cm, st) + dsk_l * xs
    for i in range(rep):
        seg = jnp.where(causal[None], jnp.exp(_bnn(lower_b, adts[i] * upper_t[None], HI)), 0.0)
        y = y + _bnn(scores * seg, jnp.where(lane_head == i, xdt, 0.0))
    st_out = st * jnp.exp(alast) + _btn(bm, xdt * jnp.exp(alast - acum))
    y2 = y * _silu(z)
    out = y2 * lax.rsqrt(jnp.mean(y2 * y2, axis=-1, keepdims=True) + NORM_EPS) * nw
    return out, st_out


def _m2_dims(sxbc, z):
    t = sxbc.shape[0]
    w2 = z.shape[1]
    g = M2_GROUPS
    n = M2_STATE
    assert sxbc.shape[1] == w2 + 2 * g * n
    gw = w2 // g
    return t, w2, g, n, gw, gw // M2_HEAD_DIM, t // CHUNK


def _m2_args(refs, g, n, gw, rep, st):
    sx_ref, z_ref, dt_ref, alog_ref, dtb_ref, dsk_ref, nw_ref = refs
    w2 = g * gw
    xs = jnp.stack([sx_ref[:, i * gw:(i + 1) * gw] for i in range(g)])
    z = jnp.stack([z_ref[:, i * gw:(i + 1) * gw] for i in range(g)])
    bm = jnp.stack([sx_ref[:, w2 + i * n:w2 + (i + 1) * n] for i in range(g)])
    cm = jnp.stack([sx_ref[:, w2 + (g + i) * n:w2 + (g + i + 1) * n] for i in range(g)])
    nw = jnp.stack([nw_ref[:, i * gw:(i + 1) * gw] for i in range(g)])

    def cols(ref):
        blk = ref[...]
        return [jnp.stack([_lane_pick(blk, gi * rep + i) for gi in range(g)]) for i in range(rep)]

    return xs, z, bm, cm, cols(dt_ref), cols(alog_ref), cols(dtb_ref), cols(dsk_ref), nw, st


def _m2_chunks_fwd(sxbc, z, dtraw, alog, dtb, dsk, nw):
    t, w2, g, n, gw, rep, nc = _m2_dims(sxbc, z)
    hm = dtraw.shape[1]
    c = CHUNK
    wx = sxbc.shape[1]

    def body(sx_ref, z_ref, dt_ref, alog_ref, dtb_ref, dsk_ref, nw_ref, y_ref, ssave_ref, st_ref):
        @pl.when(pl.program_id(0) == 0)
        def _():
            st_ref[...] = jnp.zeros_like(st_ref)

        st = st_ref[...]
        ssave_ref[...] = st
        y, st_out = _m2_groups(*_m2_args((sx_ref, z_ref, dt_ref, alog_ref, dtb_ref, dsk_ref, nw_ref), g, n, gw, rep, st))
        for i in range(g):
            y_ref[:, i * gw:(i + 1) * gw] = y[i]
        st_ref[...] = st_out

    z = _win(z)
    zb = z.off // w2
    assert z.off % w2 == 0
    in_specs = [pl.BlockSpec((c, wx), lambda k: (k, 0)), pl.BlockSpec((c, w2), lambda k: (k, zb)),
                pl.BlockSpec((c, hm), lambda k: (k, 0)),
                pl.BlockSpec((1, hm), lambda k: (0, 0)), pl.BlockSpec((1, hm), lambda k: (0, 0)),
                pl.BlockSpec((1, hm), lambda k: (0, 0)), pl.BlockSpec((1, w2), lambda k: (0, 0))]
    out_shape = [_sds((t, w2), F32), _sds((nc, g, n, gw), F32)]
    out_specs = [pl.BlockSpec((c, w2), lambda k: (k, 0)), pl.BlockSpec((None, g, n, gw), lambda k: (k, 0, 0, 0))]
    return _call(body, name="m2_chunks_fwd", out_shape=out_shape, grid=(nc,), in_specs=in_specs,
                 out_specs=out_specs, scratch_shapes=[pltpu.VMEM((g, n, gw), F32)],
                 semantics=("arbitrary",))(sxbc, z.arr, dtraw, alog, dtb, dsk, nw)


def _m2_chunks_bwd(sxbc, z, dtraw, alog, dtb, dsk, nw, ssave, dy):
    t, w2, g, n, gw, rep, nc = _m2_dims(sxbc, z)
    hm = dtraw.shape[1]
    c = CHUNK

    wx = sxbc.shape[1]

    def body(sx_ref, z_ref, dt_ref, alog_ref, dtb_ref, dsk_ref, nw_ref, ssave_ref, dy_ref,
             dsx_ref, dz_ref, ddt_ref, dalog_ref, ddtb_ref, ddsk_ref, dnw_ref, dst_ref):
        @pl.when(pl.program_id(0) == 0)
        def _():
            dst_ref[...] = jnp.zeros_like(dst_ref)
            dnw_ref[...] = jnp.zeros_like(dnw_ref)
            dalog_ref[...] = jnp.zeros_like(dalog_ref)
            ddtb_ref[...] = jnp.zeros_like(ddtb_ref)
            ddsk_ref[...] = jnp.zeros_like(ddsk_ref)

        args = _m2_args((sx_ref, z_ref, dt_ref, alog_ref, dtb_ref, dsk_ref, nw_ref), g, n, gw, rep, ssave_ref[...])
        _, vjp = jax.vjp(_m2_groups, *args)
        dyb = jnp.stack([dy_ref[:, i * gw:(i + 1) * gw] for i in range(g)])
        dxs, dz, dbm, dcm, ddts, dalogs, ddtbs, ddsks, dnw, dst = vjp((dyb, dst_ref[...]))
        dst_ref[...] = dst
        lane_h = lax.broadcasted_iota(jnp.int32, (1, hm), 1)
        ddt = jnp.zeros((c, hm), F32)
        for gi in range(g):
            dsx_ref[:, gi * gw:(gi + 1) * gw] = dxs[gi]
            dsx_ref[:, w2 + gi * n:w2 + (gi + 1) * n] = dbm[gi]
            dsx_ref[:, w2 + (g + gi) * n:w2 + (g + gi + 1) * n] = dcm[gi]
            dz_ref[:, gi * gw:(gi + 1) * gw] = dz[gi].astype(dz_ref.dtype)
            dnw_ref[:, gi * gw:(gi + 1) * gw] += dnw[gi]
            for i in range(rep):
                onehot = (lane_h == gi * rep + i).astype(F32)
                ddt = ddt + ddts[i][gi] * onehot
                dalog_ref[...] += dalogs[i][gi] * onehot
                ddtb_ref[...] += ddtbs[i][gi] * onehot
                ddsk_ref[...] += ddsks[i][gi] * onehot
        ddt_ref[...] = ddt

    rev = lambda k: nc - 1 - k
    z = _win(z)
    zb = z.off // w2
    assert z.off % w2 == 0
    in_specs = [pl.BlockSpec((c, wx), lambda k: (rev(k), 0)), pl.BlockSpec((c, w2), lambda k: (rev(k), zb)),
                pl.BlockSpec((c, hm), lambda k: (rev(k), 0)),
                pl.BlockSpec((1, hm), lambda k: (0, 0)), pl.BlockSpec((1, hm), lambda k: (0, 0)),
                pl.BlockSpec((1, hm), lambda k: (0, 0)), pl.BlockSpec((1, w2), lambda k: (0, 0)),
                pl.BlockSpec((None, g, n, gw), lambda k: (rev(k), 0, 0, 0)),
                pl.BlockSpec((c, w2), lambda k: (rev(k), 0))]
    out_shape = [_sds((t, wx), F32), _sds((t, w2), BF16), _sds((t, hm), F32), _sds((1, hm), F32), _sds((1, hm), F32),
                 _sds((1, hm), F32), _sds((1, w2), F32)]
    out_specs = [pl.BlockSpec((c, wx), lambda k: (rev(k), 0)), pl.BlockSpec((c, w2), lambda k: (rev(k), 0)),
                 pl.BlockSpec((c, hm), lambda k: (rev(k), 0)),
                 pl.BlockSpec((1, hm), lambda k: (0, 0)), pl.BlockSpec((1, hm), lambda k: (0, 0)),
                 pl.BlockSpec((1, hm), lambda k: (0, 0)), pl.BlockSpec((1, w2), lambda k: (0, 0))]
    return _call(body, name="m2_chunks_bwd", out_shape=out_shape, grid=(nc,), in_specs=in_specs,
                 out_specs=out_specs, scratch_shapes=[pltpu.VMEM((g, n, gw), F32)],
                 semantics=("arbitrary",))(sxbc, z.arr, dtraw, alog, dtb, dsk, nw, ssave, dy)


def _s5_scan(bu3, aa3, bb3, reverse, name, tb=256):
    t, rtot, _ = bu3.shape
    jn = rtot // SUBLANES
    tb = _tile(t, tb, SUBLANES)
    nb = t // tb

    def body(bu_ref, aa_ref, bb_ref, s_ref, st_ref):
        @pl.when(pl.program_id(0) == 0)
        def _():
            st_ref[...] = jnp.zeros_like(st_ref)

        tiles = [slice(j * SUBLANES, (j + 1) * SUBLANES) for j in range(jn)]
        aa = [aa_ref[tl, :] for tl in tiles]
        bb = [bb_ref[tl, :] for tl in tiles]

        def step(k, carry):
            r = tb - 1 - k if reverse else k
            new = []
            for j, tl in enumerate(tiles):
                sj, wj = carry[2 * j], carry[2 * j + 1]
                xj = bu_ref[r, tl, :]
                nj = aa[j] * sj + bb[j] * wj + xj
                s_ref[r, tl, :] = nj
                new += [nj, aa[j] * wj - bb[j] * sj + pltpu.roll(xj, SUBLANES // 2, 0)]
            return tuple(new)

        init = []
        for tl in tiles:
            init += [st_ref[tl, :], pltpu.roll(st_ref[tl, :], SUBLANES // 2, 0)]
        last = lax.fori_loop(0, tb, step, tuple(init), unroll=8)
        for j, tl in enumerate(tiles):
            st_ref[tl, :] = last[2 * j]

    rb = (lambda i: nb - 1 - i) if reverse else (lambda i: i)
    return _call(body, name=name, out_shape=_sds(bu3.shape, F32), grid=(nb,),
                 in_specs=[pl.BlockSpec((tb, rtot, LANES), lambda i: (rb(i), 0, 0)),
                           pl.BlockSpec((rtot, LANES), lambda i: (0, 0)), pl.BlockSpec((rtot, LANES), lambda i: (0, 0))],
                 out_specs=pl.BlockSpec((tb, rtot, LANES), lambda i: (rb(i), 0, 0)),
                 scratch_shapes=[pltpu.VMEM((rtot, LANES), F32)], semantics=("arbitrary",))(bu3, aa3, bb3)


def _s5_da(ds3, s3, tb=256):
    t, rtot, _ = ds3.shape
    tb = _tile(t, tb, SUBLANES)
    nb = t // tb

    def body(ds_ref, s_ref, halo_ref, p_ref, q_ref):
        i = pl.program_id(0)

        @pl.when(i == 0)
        def _():
            p_ref[...] = jnp.zeros_like(p_ref)
            q_ref[...] = jnp.zeros_like(q_ref)

        for j in range(rtot // SUBLANES):
            tl = slice(j * SUBLANES, (j + 1) * SUBLANES)
            prev = jnp.where(i == 0, 0.0, halo_ref[:, tl, :])
            sh = jnp.concatenate([prev, s_ref[0:tb - 1, tl, :]], axis=0)
            d = ds_ref[:, tl, :]
            p_ref[tl, :] += jnp.sum(d * sh, axis=0)
            q_ref[tl, :] += jnp.sum(d * pltpu.roll(sh, SUBLANES // 2, 1), axis=0)

    blk = pl.BlockSpec((tb, rtot, LANES), lambda i: (i, 0, 0))
    acc = pl.BlockSpec((rtot, LANES), lambda i: (0, 0))
    return _call(body, name="s5_da", out_shape=[_sds((rtot, LANES), F32)] * 2, grid=(nb,),
                 in_specs=[blk, blk, pl.BlockSpec((1, rtot, LANES), lambda i: (jnp.maximum(i * tb - 1, 0), 0, 0))],
                 out_specs=[acc, acc], semantics=("arbitrary",))(ds3, s3, s3)


def _conv_rows(t, tb):
    tb = _tile(t, tb, SUBLANES * 2)
    return tb, t // tb, tb // SUBLANES


def _shift_down(x, above, s):
    n = x.shape[0]
    y = pltpu.roll(x, s, 0)
    row = lax.broadcasted_iota(jnp.int32, above.shape, 0)
    head = jnp.where(row < s, pltpu.roll(above, s, 0), y[:SUBLANES])
    return head if n == SUBLANES else jnp.concatenate([head, y[SUBLANES:]], axis=0)


def _shift_up(x, below, s):
    n = x.shape[0]
    y = pltpu.roll(x, n - s, 0)
    row = lax.broadcasted_iota(jnp.int32, below.shape, 0)
    tail = jnp.where(row >= SUBLANES - s, pltpu.roll(below, SUBLANES - s, 0), y[n - SUBLANES:])
    return tail if n == SUBLANES else jnp.concatenate([y[:n - SUBLANES], tail], axis=0)


def _conv_taps(x, above, w_ref, b_ref):
    xs = [x] + [_shift_down(x, above, s) for s in range(1, CONV_K)]
    c = b_ref[...] + w_ref[CONV_K - 1:CONV_K, :] * x
    for s in range(1, CONV_K):
        c = c + w_ref[CONV_K - 1 - s:CONV_K - s, :] * xs[s]
    return c, xs


CONV_COL_TILE = 1024


def _conv_specs(x, tb):
    x = _win(x)
    t, cwid = x.shape
    ct = _col_tile([x], CONV_COL_TILE)
    tb, nb, hb = _conv_rows(t, max(tb, BLOCK_BYTES // (4 * ct)))
    base = x.off // ct
    blk_x = pl.BlockSpec((tb, ct), lambda j, i: (i, base + j))
    prev_x = pl.BlockSpec((SUBLANES, ct), lambda j, i: (jnp.maximum(i * hb - 1, 0), base + j))
    next_x = pl.BlockSpec((SUBLANES, ct), lambda j, i: (jnp.minimum((i + 1) * hb, nb * hb - 1), base + j))
    blk = pl.BlockSpec((tb, ct), lambda j, i: (i, j))
    nxt = pl.BlockSpec((SUBLANES, ct), lambda j, i: (jnp.minimum((i + 1) * hb, nb * hb - 1), j))
    taps = pl.BlockSpec((CONV_K, ct), lambda j, i: (0, j))
    bias = pl.BlockSpec((1, ct), lambda j, i: (0, j))
    return x, tb, nb, cwid // ct, dict(blk_x=blk_x, prev_x=prev_x, next_x=next_x, blk=blk, nxt=nxt, taps=taps, bias=bias)


def _conv_fwd(x, w, b, name, tb=256):
    x, tb, nb, ncol, sp = _conv_specs(x, tb)
    t, cwid = x.shape

    def body(x_ref, halo_ref, w_ref, b_ref, o_ref):
        above = jnp.where(pl.program_id(1) == 0, 0.0, halo_ref[...])
        o_ref[...] = _silu(_conv_taps(x_ref[...], above, w_ref, b_ref)[0])

    return _call(body, name=name, out_shape=_sds((t, cwid), F32), grid=(ncol, nb),
                 in_specs=[sp['blk_x'], sp['prev_x'], sp['taps'], sp['bias']], out_specs=sp['blk'],
                 semantics=("parallel", "parallel"))(x.arr, x.arr, w, b)


def _dsilu(c):
    sg = _sigmoid(c)
    return sg * (1.0 + c * (1.0 - sg))


def _conv_bwd(x, w, b, ds, name, tb=256):
    x, tb, nb, ncol, sp = _conv_specs(x, tb)
    t, cwid = x.shape

    def body(x_ref, halo_ref, xn_ref, ds_ref, dsn_ref, w_ref, b_ref, dx_ref, dw_ref, db_ref):
        i = pl.program_id(1)

        @pl.when(i == 0)
        def _():
            dw_ref[...] = jnp.zeros_like(dw_ref)
            db_ref[...] = jnp.zeros_like(db_ref)

        x = x_ref[...]
        above = jnp.where(i == 0, 0.0, halo_ref[...])
        c, xs = _conv_taps(x, above, w_ref, b_ref)
        dc = ds_ref[...] * _dsilu(c)
        cn, _ = _conv_taps(xn_ref[...], x[tb - SUBLANES:], w_ref, b_ref)
        dcn = jnp.where(i == nb - 1, 0.0, dsn_ref[...] * _dsilu(cn))
        dx = w_ref[CONV_K - 1:CONV_K, :] * dc
        for s in range(1, CONV_K):
            dx = dx + w_ref[CONV_K - 1 - s:CONV_K - s, :] * _shift_up(dc, dcn, s)
        dx_ref[...] = dx.astype(dx_ref.dtype)
        for s in range(CONV_K):
            dw_ref[CONV_K - 1 - s:CONV_K - s, :] += jnp.sum(dc * xs[s], axis=0, keepdims=True)
        db_ref[...] += jnp.sum(dc, axis=0, keepdims=True)

    return _call(body, name=name, out_shape=[_sds((t, cwid), BF16), _sds((CONV_K, cwid), F32), _sds((1, cwid), F32)],
                 grid=(ncol, nb),
                 in_specs=[sp['blk_x'], sp['prev_x'], sp['next_x'], sp['blk'], sp['nxt'], sp['taps'], sp['bias']],
                 out_specs=[sp['blk'], sp['taps'], sp['bias']],
                 semantics=("parallel", "arbitrary"))(x.arr, x.arr, x.arr, ds, ds, w, b)


def _f_rms(x, w):
    return (x * lax.rsqrt(jnp.mean(x * x, axis=-1, keepdims=True) + NORM_EPS) * w,)


def _f_s5_post1(ymm, u, d_l):
    return (_gelu_tanh(ymm + d_l * u),)


def _f_s5_post2(yg, tt, gate, b):
    return (yg * _sigmoid(tt + b) * _silu(gate),)


def _f_merge(ma, mb, mc, pa, pb, pc):
    return (_sigmoid(ma) * pa + _sigmoid(mb) * pb + _sigmoid(mc) * pc,)


def _loss_and_grad(x, tgt, fw, tb=256):
    t, dm = x.shape
    tb = _tile(t, tb, SUBLANES * 2)

    def f(xb, wb, tb_):
        y = _f_rms(xb, wb)[0]
        e = y - tb_
        return 0.5 * jnp.sum(jnp.mean(e * e, axis=-1, keepdims=True), axis=0, keepdims=True)

    def body(x_ref, t_ref, w_ref, loss_ref, dx_ref, dw_ref):
        @pl.when(pl.program_id(0) == 0)
        def _():
            loss_ref[...] = jnp.zeros_like(loss_ref)
            dw_ref[...] = jnp.zeros_like(dw_ref)

        tgt_b = t_ref[...]
        val, vjp = jax.vjp(lambda a, b: f(a, b, tgt_b), x_ref[...], w_ref[...])
        dxb, dwb = vjp(jnp.ones((1, 1), F32))
        loss_ref[...] += val
        dx_ref[...] = dxb
        dw_ref[...] += dwb

    return _call(body, name="loss_and_grad", out_shape=[_sds((1, 1), F32), _sds((t, dm), F32), _sds((1, dm), F32)],
                 grid=(t // tb,),
                 in_specs=[pl.BlockSpec((tb, dm), lambda i: (i, 0)), pl.BlockSpec((tb, dm), lambda i: (i, 0)),
                           pl.BlockSpec((1, dm), lambda i: (0, 0))],
                 out_specs=[pl.BlockSpec((1, 1), lambda i: (0, 0)), pl.BlockSpec((tb, dm), lambda i: (i, 0)),
                            pl.BlockSpec((1, dm), lambda i: (0, 0))],
                 semantics=("arbitrary",))(x, tgt, fw)


FLAT_W = 1024


def _sum_parts(parts, name, tb=256):
    n, r, wd = parts.shape
    tb = _tile(r, tb, SUBLANES)

    def body(p_ref, o_ref):
        acc = p_ref[0]
        for k in range(1, n):
            acc = acc + p_ref[k]
        o_ref[...] = acc

    return _call(body, name=name, out_shape=_sds((r, wd), F32), grid=(r // tb,),
                 in_specs=[pl.BlockSpec((n, tb, wd), lambda i: (0, i, 0))],
                 out_specs=pl.BlockSpec((tb, wd), lambda i: (i, 0)), semantics=("parallel",))(parts)


BLOCK_BYTES = 1 << 20


def _rows_per_block(r, wd):
    return _tile(r, max(SUBLANES * 2, BLOCK_BYTES // (4 * wd) // (SUBLANES * 2) * (SUBLANES * 2)), SUBLANES * 2)


def _add_my_half(g4, recv, c_idx, name):
    p, _, r, wd = g4.shape
    tb = _rows_per_block(r, wd)

    def body(c_ref, g_ref, r_ref, o_ref, ob_ref):
        s = g_ref[...] + r_ref[...]
        o_ref[...] = s
        ob_ref[...] = s.astype(BF16)

    spec = pl.BlockSpec((None, tb, wd), lambda j, i, c_ref: (j, i, 0))
    return _call(body, name=name, out_shape=[_sds((p, r, wd), F32), _sds((p, r, wd), BF16)], grid=(p, r // tb),
                 in_specs=[pl.BlockSpec((None, None, tb, wd), lambda j, i, c_ref: (j, c_ref[0], i, 0)), spec],
                 out_specs=[spec, spec], semantics=("parallel", "parallel"), num_scalar_prefetch=1)(c_idx, g4, recv)


def _sum_chips(own, got, me_idx, name):
    p, r, wd = own.shape
    tb = _rows_per_block(r, wd)

    def body(me_ref, own_ref, got_ref, o_ref):
        me = me_ref[0]
        acc = None
        for k in range(p):
            part = jnp.where(me == k, own_ref[...], got_ref[k].astype(F32))
            acc = part if acc is None else acc + part
        o_ref[...] = acc

    return _call(body, name=name, out_shape=_sds((r, wd), F32), grid=(r // tb,),
                 in_specs=[pl.BlockSpec((None, tb, wd), lambda i, me_ref: (me_ref[0], i, 0)),
                           pl.BlockSpec((p, tb, wd), lambda i, me_ref: (0, i, 0))],
                 out_specs=pl.BlockSpec((tb, wd), lambda i, me_ref: (i, 0)),
                 semantics=("parallel",), num_scalar_prefetch=1)(me_idx, own, got)


def _adamw(w, g, m, v, name):
    r, wd = w.shape
    tb = _rows_per_block(r, wd)

    def body(w_ref, g_ref, m_ref, v_ref, d_ref, nm_ref, nv_ref):
        gg = g_ref[...]
        nm = ADAM_B1 * m_ref[...] + (1.0 - ADAM_B1) * gg
        nv = ADAM_B2 * v_ref[...] + (1.0 - ADAM_B2) * (gg * gg)
        m_hat = nm / (1.0 - ADAM_B1 ** ADAM_STEP)
        v_hat = nv / (1.0 - ADAM_B2 ** ADAM_STEP)
        d_ref[...] = -ADAM_LR * (m_hat / (jnp.sqrt(v_hat) + ADAM_EPS) + ADAM_WD * w_ref[...])
        nm_ref[...] = nm
        nv_ref[...] = nv

    spec = pl.BlockSpec((tb, wd), lambda i: (i, 0))
    return _call(body, name=name, out_shape=[_sds((r, wd), F32)] * 3, grid=(r // tb,), in_specs=[spec] * 4,
                 out_specs=[spec] * 3, semantics=("parallel",))(w, g, m, v)


def _here():
    return lax.axis_index("x"), lax.axis_index("y"), lax.axis_index("c")


def _comm_call(body, name, out_shape, n_sems, operands):
    anyspec = pl.BlockSpec(memory_space=pl.ANY)
    outs = out_shape if isinstance(out_shape, (list, tuple)) else [out_shape]
    return _call(body, name=name, out_shape=out_shape, in_specs=[anyspec] * len(operands),
                 out_specs=[anyspec] * len(outs) if isinstance(out_shape, (list, tuple)) else anyspec,
                 scratch_shapes=[pltpu.SemaphoreType.DMA((n_sems,)), pltpu.SemaphoreType.DMA((n_sems,)),
                                 pltpu.SemaphoreType.DMA(())])(*operands)


def _gather_chips(x, name):
    def body(x_ref, o_ref, send_sems, recv_sems, local_sem):
        xi, yi, ci = _here()
        chips = [(1 - xi, yi), (xi, 1 - yi), (1 - xi, 1 - yi)]
        mine = pltpu.make_async_copy(x_ref, o_ref.at[2 * xi + yi], local_sem)
        mine.start()

        def copy(k, slot, to):
            return pltpu.make_async_remote_copy(src_ref=x_ref, dst_ref=o_ref.at[slot], send_sem=send_sems.at[k],
                                                recv_sem=recv_sems.at[k], device_id=to, device_id_type=MESH)

        sends = [copy(k, 2 * xi + yi, (px, py, ci)) for k, (px, py) in enumerate(chips)]
        for cp in sends:
            cp.start()
        for k, (px, py) in enumerate(chips):
            copy(k, 2 * px + py, (px, py, ci)).wait_recv()
        for cp in sends:
            cp.wait_send()
        mine.wait()

    return _comm_call(body, name, _sds((4,) + x.shape, x.dtype), 3, (x,))


def _gather_all(x, name):
    def body(x_ref, o_ref, send_sems, recv_sems, local_sem):
        xi, yi, ci = _here()
        me = 4 * xi + 2 * yi + ci
        flips = [(fx, fy, fc) for fx in (0, 1) for fy in (0, 1) for fc in (0, 1)][1:]
        peers = [((1 - xi) if fx else xi, (1 - yi) if fy else yi, (1 - ci) if fc else ci) for fx, fy, fc in flips]
        mine = pltpu.make_async_copy(x_ref, o_ref.at[me], local_sem)
        mine.start()

        def copy(k, slot, to):
            return pltpu.make_async_remote_copy(src_ref=x_ref, dst_ref=o_ref.at[slot], send_sem=send_sems.at[k],
                                                recv_sem=recv_sems.at[k], device_id=to, device_id_type=MESH)

        sends = [copy(k, me, p) for k, p in enumerate(peers)]
        for cp in sends:
            cp.start()
        for k, (px, py, pc) in enumerate(peers):
            copy(k, 4 * px + 2 * py + pc, (px, py, pc)).wait_recv()
        for cp in sends:
            cp.wait_send()
        mine.wait()

    return _comm_call(body, name, _sds((8,) + x.shape, x.dtype), 7, (x,))


def _multi_comm_call(body, name, out_shapes, n_sems, operands):
    anyspec = pl.BlockSpec(memory_space=pl.ANY)
    nin = len(operands)

    def flat_body(*refs):
        body(refs[:nin], refs[nin:nin + len(out_shapes)], refs[-2], refs[-1])

    return _call(flat_body, name=name, out_shape=list(out_shapes), in_specs=[anyspec] * nin,
                 out_specs=[anyspec] * len(out_shapes),
                 scratch_shapes=[pltpu.SemaphoreType.DMA((n_sems,)), pltpu.SemaphoreType.DMA((n_sems,))])(*operands)


def _remote(src, dst, send_sems, recv_sems, k, to):
    return pltpu.make_async_remote_copy(src_ref=src, dst_ref=dst, send_sem=send_sems.at[k], recv_sem=recv_sems.at[k],
                                        device_id=to, device_id_type=MESH)


def _gather_chips_split(xs, name):
    nw = len(xs)

    def body(x_refs, o_refs, send_sems, recv_sems):
        xi, yi, ci = _here()
        me = 2 * xi + yi
        sib = (xi, yi, 1 - ci)
        chips = [(1 - xi, yi), (xi, 1 - yi), (1 - xi, 1 - yi)]
        sends = []
        for i in range(nw):
            for k, (px, py) in enumerate(chips):
                sends.append(_remote(x_refs[i].at[ci], o_refs[i].at[me, ci], send_sems, recv_sems, 3 * i + k,
                                     (px, py, ci)))
        for cp in sends:
            cp.start()
        passed = []
        for i in range(nw):
            for k, (px, py) in enumerate(chips):
                landed = o_refs[i].at[2 * px + py, ci]
                _remote(landed, landed, send_sems, recv_sems, 3 * i + k, (px, py, ci)).wait_recv()
                fwd = _remote(landed, landed, send_sems, recv_sems, 3 * (nw + i) + k, sib)
                fwd.start()
                passed.append(fwd)
        for i in range(nw):
            for k, (px, py) in enumerate(chips):
                other = o_refs[i].at[2 * px + py, 1 - ci]
                _remote(other, other, send_sems, recv_sems, 3 * (nw + i) + k, sib).wait_recv()
        for cp in sends + passed:
            cp.wait_send()

    return _multi_comm_call(body, name, [_sds((4,) + x.shape, x.dtype) for x in xs], 6 * nw, xs)


GATHER_COLLECTIVE_ID = 1


def _gather_chips_split_async(xs, name, collective_id):
    nw = len(xs)
    x_refs = [jax.new_ref(x, memory_space=pltpu.MemorySpace.HBM) for x in xs]
    o_refs = [jax.empty_ref(_sds((4,) + x.shape, x.dtype), memory_space=pltpu.MemorySpace.HBM) for x in xs]

    @pl.kernel(mesh=plsc.ScalarSubcoreMesh(axis_name="sequencer", num_cores=1), name=name,
               scratch_types=(pltpu.SemaphoreType.DMA((6 * nw,)), pltpu.SemaphoreType.DMA((6 * nw,))),
               compiler_params=pltpu.CompilerParams(collective_id=collective_id))
    def launch(send_sems, recv_sems):
        xi, yi, ci = _here()
        me = 2 * xi + yi
        sib = (xi, yi, 1 - ci)
        chips = [(1 - xi, yi), (xi, 1 - yi), (1 - xi, 1 - yi)]
        barrier = pltpu.get_barrier_semaphore()
        for peer in [sib] + [(px, py, ci) for px, py in chips]:
            pl.semaphore_signal(barrier, inc=1, device_id=peer, device_id_type=MESH)
        pl.semaphore_wait(barrier, 4)
        sends = []
        for i in range(nw):
            for k, (px, py) in enumerate(chips):
                sends.append(_remote(x_refs[i].at[ci], o_refs[i].at[me, ci], send_sems, recv_sems, 3 * i + k,
                                     (px, py, ci)))
        for cp in sends:
            cp.start()
        passed = []
        for i in range(nw):
            for k, (px, py) in enumerate(chips):
                landed = o_refs[i].at[2 * px + py, ci]
                _remote(landed, landed, send_sems, recv_sems, 3 * i + k, (px, py, ci)).wait_recv()
                fwd = _remote(landed, landed, send_sems, recv_sems, 3 * (nw + i) + k, sib)
                fwd.start()
                passed.append(fwd)
        for i in range(nw):
            for k, (px, py) in enumerate(chips):
                other = o_refs[i].at[2 * px + py, 1 - ci]
                _remote(other, other, send_sems, recv_sems, 3 * (nw + i) + k, sib).wait_recv()
        for cp in sends + passed:
            cp.wait_send()

    launch()
    return [o[...] for o in o_refs]


def _swap_sibling_half(gs):
    def body(g_refs, o_refs, send_sems, recv_sems):
        xi, yi, ci = _here()
        cps = [_remote(g.at[:, 1 - ci], o, send_sems, recv_sems, i, (xi, yi, 1 - ci))
               for i, (g, o) in enumerate(zip(g_refs, o_refs))]
        for cp in cps:
            cp.start()
        for cp in cps:
            cp.wait()

    return _multi_comm_call(body, "swap_sibling_half", [_sds((g.shape[0],) + g.shape[2:], g.dtype) for g in gs],
                            len(gs), gs)


SCATTER_COLLECTIVE_ID = 2


def _scatter_chips_async(gps, name, collective_id):
    nw = len(gps)
    g_refs = [jax.new_ref(g, memory_space=pltpu.MemorySpace.HBM) for g in gps]
    o_refs = [jax.empty_ref(_sds(g.shape, g.dtype), memory_space=pltpu.MemorySpace.HBM) for g in gps]

    @pl.kernel(mesh=plsc.ScalarSubcoreMesh(axis_name="sequencer", num_cores=1), name=name,
               scratch_types=(pltpu.SemaphoreType.DMA((3 * nw,)), pltpu.SemaphoreType.DMA((3 * nw,))),
               compiler_params=pltpu.CompilerParams(collective_id=collective_id))
    def launch(send_sems, recv_sems):
        xi, yi, ci = _here()
        me = 2 * xi + yi
        chips = [(1 - xi, yi), (xi, 1 - yi), (1 - xi, 1 - yi)]
        barrier = pltpu.get_barrier_semaphore()
        for px, py in chips:
            pl.semaphore_signal(barrier, inc=1, device_id=(px, py, ci), device_id_type=MESH)
        pl.semaphore_wait(barrier, 3)
        sends = [_remote(g_refs[i].at[2 * px + py], o_refs[i].at[me], send_sems, recv_sems, 3 * i + k, (px, py, ci))
                 for i in range(nw) for k, (px, py) in enumerate(chips)]
        for cp in sends:
            cp.start()
        for i in range(nw):
            for k, (px, py) in enumerate(chips):
                slot = o_refs[i].at[2 * px + py]
                _remote(slot, slot, send_sems, recv_sems, 3 * i + k, (px, py, ci)).wait_recv()
        for cp in sends:
            cp.wait_send()

    launch()
    return [o[...] for o in o_refs]


def _share_sibling(rs):
    def body(r_refs, o_refs, send_sems, recv_sems):
        xi, yi, ci = _here()
        cps = [_remote(r, o, send_sems, recv_sems, i, (xi, yi, 1 - ci)) for i, (r, o) in enumerate(zip(r_refs, o_refs))]
        for cp in cps:
            cp.start()
        for cp in cps:
            cp.wait()

    return _multi_comm_call(body, "share_sibling", [_sds(r.shape, r.dtype) for r in rs], len(rs), rs)


def _flat_pack(arrs, dtype, row_mult):
    flat = jnp.concatenate([a.astype(dtype).reshape(-1) for a in arrs])
    unit = FLAT_W * row_mult
    npad = -(-flat.shape[0] // unit) * unit
    return jnp.pad(flat, (0, npad - flat.shape[0])).reshape(npad // FLAT_W, FLAT_W)


def _flat_unpack(flat2d, shapes):
    flat = flat2d.reshape(-1)
    outs, off = [], 0
    for s in shapes:
        size = int(np.prod(s))
        outs.append(flat[off:off + size].reshape(s))
        off += size
    return outs


def _s5_params(lam_re, lam_im, log_step, b_re, b_im, c_re, c_im, d_skip):
    g, p = lam_re.shape
    hs = b_re.shape[2]
    gt = S5_GROUP_TILE
    jn = g // gt
    lam_re = jnp.minimum(lam_re, -1e-4)
    step = jnp.exp(log_step)[:, None]
    mag = jnp.exp(lam_re * step)
    ab_re = mag * jnp.cos(lam_im * step)
    ab_im = mag * jnp.sin(lam_im * step)
    den = lam_re * lam_re + lam_im * lam_im
    f_re = ((ab_re - 1.0) * lam_re + ab_im * lam_im) / den
    f_im = (ab_im * lam_re - (ab_re - 1.0) * lam_im) / den
    bb_re = f_re[..., None] * b_re - f_im[..., None] * b_im
    bb_im = f_re[..., None] * b_im + f_im[..., None] * b_re
    a_l = jnp.concatenate([ab_re.reshape(jn, gt * p), ab_im.reshape(jn, gt * p)], axis=1).reshape(1, jn * 2 * gt * p)
    eye = jnp.eye(gt, dtype=F32)

    def blockdiag(m):
        return jnp.einsum('jahp,ab->jahbp', m.reshape(jn, gt, hs, p), eye).reshape(jn, gt * hs, gt * p)

    b_blk = jnp.concatenate([blockdiag(bb_re.transpose(0, 2, 1)), blockdiag(bb_im.transpose(0, 2, 1))], axis=2)
    c_blk = jnp.concatenate([blockdiag(c_re), blockdiag(-c_im)], axis=2)
    return a_l, b_blk, c_blk, d_skip.reshape(1, g * hs)


def _s5_scan_consts(a_l, cw):
    jn, hr = a_l.shape[1] // cw, cw // 2 // LANES
    a4 = a_l.reshape(jn, 2, hr, LANES)
    are, aim = a4[:, 0], a4[:, 1]
    flat = lambda u, v: jnp.concatenate([u, v], axis=1).reshape(jn * 2 * hr, LANES)
    return flat(are, are), flat(-aim, aim), flat(aim, -aim)


def _s5_da_lanes(p, q, cw):
    jn, hr = p.shape[0] * LANES // cw, cw // 2 // LANES
    p4, q4 = p.reshape(jn, 2, hr, LANES), q.reshape(jn, 2, hr, LANES)
    da_re = (p4[:, 0] + p4[:, 1]).reshape(jn, cw // 2)
    da_im = (q4[:, 1] - q4[:, 0]).reshape(jn, cw // 2)
    return jnp.concatenate([da_re, da_im], axis=1).reshape(1, jn * cw)


def _layer_dims(p):
    d_model = p['w_out'].shape[1]
    wa = p['proj_a'].shape[0]
    h = p['gdn_a_log'].shape[0]
    wb = p['proj_b'].shape[0]
    wc = p['proj_c'].shape[0]
    hm = p['m2_a_log'].shape[0]
    cdim = p['m2_conv_w'].shape[1]
    width = dict(zip(PROJ_ORDER, (3 * wa, wa, h, h, wb, wb, wc, cdim, hm, d_model, d_model, d_model)))
    n_in = sum(width.values())
    return width, n_in, -(-n_in // LANES) * LANES


PROJ_ORDER = ('qkv', 'az', 'braw', 'araw', 'su', 'sgate', 'cz', 'cxbc', 'cdt', 'ma', 'mb', 'mc')
WORK_ORDER = ('qkv', 'az', 'cz', 'su', 'sgate', 'cxbc', 'ma', 'mb', 'mc', 'braw', 'araw', 'cdt')
ROW_COL_TILE = 512


def _offsets(width, order):
    off, o = {}, 0
    for n in order:
        off[n] = o
        o += width[n]
    return off


def _virtual_cols(parts, lo, hi):
    wd = parts[0].shape[-1]
    out = []
    while lo < hi:
        j = lo // wd
        stop = min(hi, (j + 1) * wd)
        out.append(lax.slice_in_dim(parts[j], lo - j * wd, stop - j * wd, axis=parts[j].ndim - 1))
        lo = stop
    return out


def _work_order_weight(shards, width, n_pad):
    off = _offsets(width, PROJ_ORDER)
    parts = [s for n in WORK_ORDER for s in _virtual_cols(shards, off[n], off[n] + width[n])]
    used = sum(width.values())
    parts.append(jnp.zeros(shards[0].shape[:-1] + (n_pad - used,), shards[0].dtype))
    return jnp.concatenate(parts, axis=-1)


def _weight_order_shard(gw, width, lo, hi):
    src, dst = _offsets(width, WORK_ORDER), _offsets(width, PROJ_ORDER)
    parts = []
    for n in PROJ_ORDER:
        a, b = max(lo, dst[n]), min(hi, dst[n] + width[n])
        if a < b:
            parts.append(lax.slice_in_dim(gw, src[n] + a - dst[n], src[n] + b - dst[n], axis=gw.ndim - 1))
    return jnp.concatenate(parts, axis=-1)


def _layer_fwd(x, p):
    width, n_in, n_pad = _layer_dims(p)
    sv = {'x': x}
    h = _rowwise(_f_rms, [x], [p['norm_w'][None]], [x.shape[1]], [BF16], "rms_fwd")[0]
    w_in = _work_order_weight(p['w_in_shards'], width, n_pad)
    proj = _matmul(h, w_in, 'nn', F32, "in_proj", tn=IN_PROJ_TILE)
    wins, off = {}, 0
    for n in WORK_ORDER:
        wins[n] = Win(proj, off, width[n])
        off += width[n]
    qkv, az, cz, su, sgate, cxbc, ma, mb, mc = (wins[n] for n in WORK_ORDER[:9])
    braw, araw, cdt = (lax.slice_in_dim(proj, wins[n].off, wins[n].off + width[n], axis=1) for n in WORK_ORDER[9:])
    sv.update(h=h, w_in=w_in, qkv=qkv, az=az, braw=braw, araw=araw, su=su, sgate=sgate, cz=cz, cxbc=cxbc, cdt=cdt,
              ma=ma, mb=mb, mc=mc)
    gb0 = jnp.zeros((1, qkv.shape[1]), F32)
    sqkv = _conv_fwd(qkv, p['gdn_conv_w'], gb0, "gdn_conv_fwd")
    ya, ssa, tsa = _gdn_chunks_fwd(sqkv, az, braw, araw, p['gdn_a_log'][None], p['gdn_dt_bias'][None],
                                   p['gdn_norm_w'][None])
    sv.update(sqkv=sqkv, ssa=ssa, tsa=tsa, ya=ya)
    s5_in = tuple(p[k] for k in ('s5_lam_re', 's5_lam_im', 's5_log_step', 's5_b_re', 's5_b_im', 's5_c_re', 's5_c_im',
                                 's5_d'))
    (a_l, b_blk, c_blk, d_l), s5_vjp = jax.vjp(_s5_params, *s5_in)
    cw = b_blk.shape[2]
    aa3, bb3, bb3_conj = _s5_scan_consts(a_l, cw)
    bu = _bd_expand(su, b_blk, "s5_bu")
    s = _s5_scan(bu, aa3, bb3, False, "s5_scan_fwd")
    ymm = _bd_reduce(s, c_blk, F32, "s5_out")
    yg = _rowwise(_f_s5_post1, [ymm, su], [d_l], [su.shape[1]], [F32], "s5_post1_fwd", col_tile=ROW_COL_TILE)[0]
    tt = _matmul(yg, p['s5_glu_w'], 'nn', F32, "s5_glu")
    yb = _rowwise(_f_s5_post2, [yg, tt, sgate], [p['s5_glu_b'][None]], [su.shape[1]], [F32], "s5_post2_fwd",
                  col_tile=ROW_COL_TILE)[0]
    sv.update(aa3=aa3, bb3_conj=bb3_conj, b_blk=b_blk, c_blk=c_blk, d_l=d_l, s5_vjp=s5_vjp, s=s, ymm=ymm, yg=yg, tt=tt,
              yb=yb, cw=cw)
    sxbc = _conv_fwd(cxbc, p['m2_conv_w'], p['m2_conv_b'][None], "m2_conv_fwd")
    yc, ssc = _m2_chunks_fwd(sxbc, cz, cdt, p['m2_a_log'][None], p['m2_dt_bias'][None], p['m2_d'][None],
                             p['m2_norm_w'][None])
    sv.update(sxbc=sxbc, ssc=ssc, yc=yc)
    pa = _matmul(ya, p['proj_a'], 'nn', F32, "proj_a")
    pb = _matmul(yb, p['proj_b'], 'nn', F32, "proj_b")
    pc = _matmul(yc, p['proj_c'], 'nn', F32, "proj_c")
    merged = _rowwise(_f_merge, [ma, mb, mc, pa, pb, pc], [], [x.shape[1]], [BF16], "merge_fwd",
                      col_tile=ROW_COL_TILE)[0]
    x_next = _matmul(merged, p['w_out'], 'nn', F32, "out_proj", add=x)
    sv.update(pa=pa, pb=pb, pc=pc, merged=merged)
    return x_next, sv


def _layer_bwd(dx_out, p, sv):
    width, n_in, n_pad = _layer_dims(p)
    g = {}
    dmerged = _matmul(dx_out, p['w_out'], 'nt', F32, "out_proj_dx")
    g['w_out'] = _matmul(sv['merged'], dx_out, 'tn', F32, "out_proj_dw")
    dma, dmb, dmc, dpa, dpb, dpc = _rowwise_bwd(
        _f_merge, [sv['ma'], sv['mb'], sv['mc'], sv['pa'], sv['pb'], sv['pc']], [], [dmerged], [BF16] * 6,
        "merge_bwd", col_tile=ROW_COL_TILE)
    dya = _matmul(dpa, p['proj_a'], 'nt', F32, "proj_a_dx")
    dyb = _matmul(dpb, p['proj_b'], 'nt', F32, "proj_b_dx")
    dyc = _matmul(dpc, p['proj_c'], 'nt', F32, "proj_c_dx")
    g['proj_a'] = _matmul(sv['ya'], dpa, 'tn', F32, "proj_a_dw")
    g['proj_b'] = _matmul(sv['yb'], dpb, 'tn', F32, "proj_b_dw")
    g['proj_c'] = _matmul(sv['yc'], dpc, 'tn', F32, "proj_c_dw")
    alog, dtb, gnw = p['gdn_a_log'][None], p['gdn_dt_bias'][None], p['gdn_norm_w'][None]
    dsq, daz, db3, da3, dalog, ddtb, dgnw = _gdn_chunks_bwd(sv['sqkv'], sv['az'], sv['braw'], sv['araw'], alog, dtb, gnw,
                                                            sv['ssa'], sv['tsa'], dya)
    gb0 = jnp.zeros((1, sv['qkv'].shape[1]), F32)
    dqkv, g['gdn_conv_w'], _ = _conv_bwd(sv['qkv'], p['gdn_conv_w'], gb0, dsq, "gdn_conv_bwd")
    dbraw, daraw = jnp.sum(db3, axis=0), jnp.sum(da3, axis=0)
    g.update(gdn_a_log=dalog[0], gdn_dt_bias=ddtb[0], gdn_norm_w=dgnw[0])
    dsx, dcz, dcdt, dmalog, dmdtb, dmdsk, dmnw = _m2_chunks_bwd(
        sv['sxbc'], sv['cz'], sv['cdt'], p['m2_a_log'][None], p['m2_dt_bias'][None], p['m2_d'][None],
        p['m2_norm_w'][None], sv['ssc'], dyc)
    dcxbc, g['m2_conv_w'], dconvb = _conv_bwd(sv['cxbc'], p['m2_conv_w'], p['m2_conv_b'][None], dsx, "m2_conv_bwd")
    g.update(m2_conv_b=dconvb[0], m2_a_log=dmalog[0], m2_dt_bias=dmdtb[0], m2_d=dmdsk[0], m2_norm_w=dmnw[0])
    dyg1, dtt, dsgate, dglub = _rowwise_bwd(_f_s5_post2, [sv['yg'], sv['tt'], sv['sgate']], [p['s5_glu_b'][None]],
                                            [dyb], [F32, BF16, BF16], "s5_post2_bwd", col_tile=ROW_COL_TILE)
    dyg = _matmul(dtt, p['s5_glu_w'], 'nt', F32, "s5_glu_dx", add=dyg1)
    g['s5_glu_w'] = _matmul(sv['yg'], dtt, 'tn', F32, "s5_glu_dw")
    g['s5_glu_b'] = dglub[0]
    dymm, dsu1, dd_l = _rowwise_bwd(_f_s5_post1, [sv['ymm'], sv['su']], [sv['d_l']], [dyg], [BF16, F32],
                                    "s5_post1_bwd", col_tile=ROW_COL_TILE)
    gy = _bd_expand(dymm, sv['c_blk'], "s5_out_dx")
    ds = _s5_scan(gy, sv['aa3'], sv['bb3_conj'], True, "s5_scan_bwd")
    da_l = _s5_da_lanes(*_s5_da(ds, sv['s']), sv['cw'])
    dsu = _bd_reduce(ds, sv['b_blk'], BF16, "s5_bu_dx", add=dsu1)
    ka = sv['b_blk'].shape[1]
    db_blk = _bd_outer(sv['su'], ds, ka, "s5_bu_dw")
    dc_blk = _bd_outer(dymm, sv['s'], ka, "s5_out_dw")
    for k, v in zip(('s5_lam_re', 's5_lam_im', 's5_log_step', 's5_b_re', 's5_b_im', 's5_c_re', 's5_c_im', 's5_d'),
                    sv['s5_vjp']((da_l, db_blk, dc_blk, dd_l))):
        g[k] = v
    small = jnp.concatenate([dbraw, daraw, dcdt], axis=1).astype(BF16)
    small = jnp.pad(small, ((0, 0), (0, n_pad - n_in + sum(width[n] for n in WORK_ORDER[9:]) - small.shape[1])))
    dproj = _concat_cols([dqkv, daz, dcz, dsu, dsgate, dcxbc, dma, dmb, dmc, small], "concat_dproj")
    dh = _matmul(dproj, sv['w_in'], 'nt', F32, "in_proj_dx", tm=2048, tk=IN_PROJ_TILE)
    g['w_in'] = _matmul(sv['h'], dproj, 'tn', F32, "in_proj_dw", tn=IN_PROJ_TILE)
    dx, dnw = _rowwise_bwd(_f_rms, [sv['x']], [p['norm_w'][None]], [dh], [F32], "rms_bwd", addend=dx_out)
    g['norm_w'] = dnw[0]
    return dx, g


INPUT_NAMES = (['x'] + WEIGHT_NAMES + ['loss_target'] + ['m_' + n for n in WEIGHT_NAMES]
               + ['v_' + n for n in WEIGHT_NAMES])


def _step(d):
    xi, yi, ci = _here()
    me = 2 * xi + yi
    depth = d['norm_w'].shape[0]
    big, ssm = list(BIG), list(SHARDED_SMALL)
    nsh = 4
    full, gathered = {}, {}
    gots = None
    for first, stop in ((0, 1), (1, depth)):
        halves = [d[n][first:stop].astype(BF16).reshape(2, -1, d[n].shape[-1]) for n in big]
        if first == 0:
            gots = _gather_chips_split(halves, "gather_weights")
        else:
            halves, gots = lax.optimization_barrier((halves, gots))
            gots = _gather_chips_split_async(halves, "gather_weights_async", GATHER_COLLECTIVE_ID)
        for n, hv, got in zip(big, halves, gots):
            got = lax.dynamic_update_slice(got, hv[None], (me, 0, 0, 0))
            gathered[n, first] = got.reshape((nsh, stop - first) + d[n].shape[1:])
    cg = _gather_chips(_flat_pack([d[n] for n in ssm], F32, 8), "gather_conv_weights")
    parts = [_flat_unpack(cg[j], [d[n].shape for n in ssm]) for j in range(nsh)]
    for i, n in enumerate(ssm):
        full[n] = jnp.concatenate([parts[j][i] for j in range(nsh)], axis=SHARDED_SMALL[n])
    layer_names = [n for n in WEIGHT_NAMES if n != 'final_norm_w']

    def layer_params(l):
        p = {n: (full[n][l] if n in full else d[n][l]) for n in layer_names if n not in BIG}
        for n in big:
            got = gathered[n, 0][:, 0] if l == 0 else gathered[n, 1][:, l - 1]
            if n == 'w_in':
                p['w_in_shards'] = [got[j] for j in range(nsh)]
            else:
                p[n] = jnp.concatenate([got[j] for j in range(nsh)], axis=BIG[n] - 1)
        return p

    x = d['x'][0]
    saved, params = [], []
    for l in range(depth):
        params.append(layer_params(l))
        x, sv = _layer_fwd(x, params[l])
        saved.append(sv)
    loss11, dx, dfw = _loss_and_grad(x, d['loss_target'][0], d['final_norm_w'][None])
    loss = lax.psum(loss11[0, 0], ("x", "y", "c"))
    width_in = _layer_dims(params[0])[0]

    def shard(a, n, j):
        wd = d[n].shape[BIG[n]]
        if n == 'w_in':
            return _weight_order_shard(a, width_in, j * wd, (j + 1) * wd)
        return lax.slice_in_dim(a, j * wd, (j + 1) * wd, axis=BIG[n] - 1)

    c_idx = jnp.reshape(ci, (1,)).astype(jnp.int32)
    me_idx = jnp.reshape(me, (1,)).astype(jnp.int32)
    grads, own, got = [None] * depth, [None] * depth, [None] * depth
    mine = [None] * (depth * len(big))

    def chip_sums(l):
        for i, (n, pf, gt) in enumerate(zip(big, own[l], got[l])):
            mine[l * len(big) + i] = _sum_chips(pf, gt, me_idx, "sum_chips_" + n)

    for l in reversed(range(depth)):
        dx, grads[l] = _layer_bwd(dx, params[l], saved[l])
        g4 = [jnp.stack([shard(grads[l][n], n, j) for j in range(nsh)]).reshape(nsh, 2, -1, d[n].shape[-1])
              for n in big]
        pairs = [_add_my_half(g, r, c_idx, "add_my_half_" + n) for n, g, r in zip(big, g4, _swap_sibling_half(g4))]
        own[l] = [pf for pf, _ in pairs]
        got[l] = _scatter_chips_async([pb for _, pb in pairs], "scatter_chips_async_%d" % l, SCATTER_COLLECTIVE_ID + l)
        if l + 1 < depth:
            got[l + 1], dx = lax.optimization_barrier((got[l + 1], dx))
            chip_sums(l + 1)
    gfull = {n: jnp.stack([grads[l][n] for l in range(depth)]) for n in layer_names if n not in BIG}
    gfull['final_norm_w'] = dfw[0]
    out = {}
    small = [n for n in WEIGHT_NAMES if n not in BIG]
    sshapes = [gfull[n].shape for n in small]
    gsm = _sum_parts(_gather_all(_flat_pack([gfull[n] for n in small], F32, 8), "gather_small_grads"), "sum_devices")
    gs = dict(zip(small, _flat_unpack(gsm, sshapes)))
    for n in ssm:
        wd = d[n].shape[SHARDED_SMALL[n]]
        gs[n] = lax.dynamic_slice_in_dim(gs[n], me * wd, wd, axis=SHARDED_SMALL[n])
    lshapes = [d[n].shape for n in small]
    wps, gps, mps, vps = (_flat_pack(arrs, F32, 16) for arrs in (
        [d[n] for n in small], [gs[n] for n in small], [d['m_' + n] for n in small], [d['v_' + n] for n in small]))
    dl, nm, nv = _adamw(wps, gps, mps, vps, "adamw_small")
    for key, arr in (('grad_', gps), ('delta_', dl), ('new_m_', nm), ('new_v_', nv)):
        for n, a in zip(small, _flat_unpack(arr, lshapes)):
            out[key + n] = a
    got[0], _ = lax.optimization_barrier((got[0], dl))
    chip_sums(0)
    theirs = _share_sibling(mine)
    for i, n in enumerate(big):
        layers = []
        for l in range(depth):
            mn, th = mine[l * len(big) + i], theirs[l * len(big) + i]
            layers.append(jnp.where(ci == 0, jnp.stack([mn, th]), jnp.stack([th, mn])))
        g2 = jnp.stack(layers).reshape(-1, d[n].shape[-1])
        w2, m2, v2 = (d[pre + n].reshape(g2.shape) for pre in ('', 'm_', 'v_'))
        dl, nm, nv = _adamw(w2, g2, m2, v2, "adamw_" + n)
        for key, arr in (('grad_', g2), ('delta_', dl), ('new_m_', nm), ('new_v_', nv)):
            out[key + n] = arr.reshape(d[n].shape)
    res = [loss, dx[None]]
    for key in ('grad_', 'delta_', 'new_m_', 'new_v_'):
        res += [out[key + n] for n in WEIGHT_NAMES]
    return tuple(res)


def kernel(x, norm_w, w_in, gdn_conv_w, gdn_a_log, gdn_dt_bias, gdn_norm_w, s5_lam_re, s5_lam_im, s5_log_step, s5_b_re, s5_b_im, s5_c_re, s5_c_im, s5_d, s5_glu_w, s5_glu_b, m2_conv_w, m2_conv_b, m2_a_log, m2_dt_bias, m2_d, m2_norm_w, proj_a, proj_b, proj_c, w_out, final_norm_w, loss_target, m_norm_w, m_w_in, m_gdn_conv_w, m_gdn_a_log, m_gdn_dt_bias, m_gdn_norm_w, m_s5_lam_re, m_s5_lam_im, m_s5_log_step, m_s5_b_re, m_s5_b_im, m_s5_c_re, m_s5_c_im, m_s5_d, m_s5_glu_w, m_s5_glu_b, m_m2_conv_w, m_m2_conv_b, m_m2_a_log, m_m2_dt_bias, m_m2_d, m_m2_norm_w, m_proj_a, m_proj_b, m_proj_c, m_w_out, m_final_norm_w, v_norm_w, v_w_in, v_gdn_conv_w, v_gdn_a_log, v_gdn_dt_bias, v_gdn_norm_w, v_s5_lam_re, v_s5_lam_im, v_s5_log_step, v_s5_b_re, v_s5_b_im, v_s5_c_re, v_s5_c_im, v_s5_d, v_s5_glu_w, v_s5_glu_b, v_m2_conv_w, v_m2_conv_b, v_m2_a_log, v_m2_dt_bias, v_m2_d, v_m2_norm_w, v_proj_a, v_proj_b, v_proj_c, v_w_out, v_final_norm_w):
    args = (x, norm_w, w_in, gdn_conv_w, gdn_a_log, gdn_dt_bias, gdn_norm_w, s5_lam_re, s5_lam_im, s5_log_step, s5_b_re, s5_b_im, s5_c_re, s5_c_im, s5_d, s5_glu_w, s5_glu_b, m2_conv_w, m2_conv_b, m2_a_log, m2_dt_bias, m2_d, m2_norm_w, proj_a, proj_b, proj_c, w_out, final_norm_w, loss_target, m_norm_w, m_w_in, m_gdn_conv_w, m_gdn_a_log, m_gdn_dt_bias, m_gdn_norm_w, m_s5_lam_re, m_s5_lam_im, m_s5_log_step, m_s5_b_re, m_s5_b_im, m_s5_c_re, m_s5_c_im, m_s5_d, m_s5_glu_w, m_s5_glu_b, m_m2_conv_w, m_m2_conv_b, m_m2_a_log, m_m2_dt_bias, m_m2_d, m_m2_norm_w, m_proj_a, m_proj_b, m_proj_c, m_w_out, m_final_norm_w, v_norm_w, v_w_in, v_gdn_conv_w, v_gdn_a_log, v_gdn_dt_bias, v_gdn_norm_w, v_s5_lam_re, v_s5_lam_im, v_s5_log_step, v_s5_b_re, v_s5_b_im, v_s5_c_re, v_s5_c_im, v_s5_d, v_s5_glu_w, v_s5_glu_b, v_m2_conv_w, v_m2_conv_b, v_m2_a_log, v_m2_dt_bias, v_m2_d, v_m2_norm_w, v_proj_a, v_proj_b, v_proj_c, v_w_out, v_final_norm_w)
    return _step(dict(zip(INPUT_NAMES, args)))
```

```python
import math
from typing import NamedTuple

import jax
import jax.numpy as jnp
import numpy as np
from jax import lax
from jax.experimental import pallas as pl
from jax.experimental.pallas import tpu as pltpu
from jax.experimental.pallas import tpu_sc as plsc

F32 = jnp.float32
BF16 = jnp.bfloat16
HI = lax.Precision.HIGH
MESH = pl.DeviceIdType.MESH

CHUNK = 64
CONV_K = 4
NORM_EPS = 1e-6
GDN_HEAD_DIM = 128
M2_HEAD_DIM = 64
M2_STATE = 128
M2_GROUPS = 4
S5_GROUP_TILE = 8
ADAM_LR = 0.001
ADAM_B1 = 0.9
ADAM_B2 = 0.999
ADAM_EPS = 1e-08
ADAM_WD = 0.01
ADAM_STEP = 10
LANES = 128
SUBLANES = 8
VMEM_LIMIT_BYTES = 56 * 1024 * 1024

WEIGHT_NAMES = ['norm_w', 'w_in', 'gdn_conv_w', 'gdn_a_log', 'gdn_dt_bias', 'gdn_norm_w', 's5_lam_re', 's5_lam_im',
                's5_log_step', 's5_b_re', 's5_b_im', 's5_c_re', 's5_c_im', 's5_d', 's5_glu_w', 's5_glu_b',
                'm2_conv_w', 'm2_conv_b', 'm2_a_log', 'm2_dt_bias', 'm2_d', 'm2_norm_w', 'proj_a', 'proj_b',
                'proj_c', 'w_out', 'final_norm_w']
BIG = {'w_in': 2, 'proj_a': 2, 'proj_b': 2, 'proj_c': 2, 'w_out': 1, 's5_glu_w': 1}
SHARDED_SMALL = {'gdn_conv_w': 2, 'm2_conv_w': 2}


def _call(body, *, name, out_shape, grid=None, in_specs=None, out_specs=None, scratch_shapes=(), semantics=None,
          num_scalar_prefetch=None):
    params = dict(vmem_limit_bytes=VMEM_LIMIT_BYTES)
    if semantics is not None:
        params['dimension_semantics'] = semantics
    kw = dict(name=name, out_shape=out_shape, compiler_params=pltpu.CompilerParams(**params))
    if num_scalar_prefetch is not None:
        kw['grid_spec'] = pltpu.PrefetchScalarGridSpec(num_scalar_prefetch=num_scalar_prefetch, grid=grid,
                                                       in_specs=in_specs, out_specs=out_specs,
                                                       scratch_shapes=scratch_shapes)
    else:
        if grid is not None:
            kw['grid'] = grid
        if in_specs is not None:
            kw['in_specs'] = in_specs
        if out_specs is not None:
            kw['out_specs'] = out_specs
        if scratch_shapes:
            kw['scratch_shapes'] = scratch_shapes
    return pl.pallas_call(body, **kw)


def _tile(n, target, unit):
    if n <= target:
        return n
    t = (target // unit) * unit
    while t >= unit:
        if n % t == 0:
            return t
        t -= unit
    raise ValueError(f"no tile for {n} (unit {unit}, target {target})")


def _sds(shape, dtype):
    return jax.ShapeDtypeStruct(tuple(shape), dtype)


def _sigmoid(x):
    return jax.nn.sigmoid(x)


def _silu(x):
    return x * jax.nn.sigmoid(x)


def _softplus(x):
    return jnp.maximum(x, 0.0) + jnp.log(1.0 + jnp.exp(-jnp.abs(x)))


def _gelu_tanh(x):
    return 0.5 * x * (1.0 + jnp.tanh(math.sqrt(2.0 / math.pi) * (x + 0.044715 * (x * x * x))))


def _dot(a, b, dims, prec=None):
    return lax.dot_general(a, b, (dims, ((), ())), precision=prec, preferred_element_type=F32)


def _nn(a, b, prec=None):
    return _dot(a, b, ((1,), (0,)), prec)


def _nt(a, b, prec=None):
    return _dot(a, b, ((1,), (1,)), prec)


def _tn(a, b, prec=None):
    return _dot(a, b, ((0,), (0,)), prec)


IN_PROJ_TILE = 1664
MATMUL_TILES = {'nn': (1024, 1024, 2048), 'nt': (1024, 1024, 2048), 'tn': (1024, 1024, 2048)}


def _matmul(a, b, mode, out_dtype, name, tm=None, tn=None, tk=None, add=None):
    tm, tn, tk = (t if t is not None else dflt for t, dflt in zip((tm, tn, tk), MATMUL_TILES[mode]))
    if mode == 'nn':
        (m, k), (k2, n) = a.shape, b.shape
    elif mode == 'nt':
        (m, k), (n, k2) = a.shape, b.shape
    else:
        (k, m), (k2, n) = a.shape, b.shape
    assert k == k2, (a.shape, b.shape, mode)
    tm = _tile(m, tm, LANES if mode == 'tn' else SUBLANES)
    tn = _tile(n, tn, LANES)
    tk = _tile(k, tk, LANES if mode != 'tn' else SUBLANES * 2)
    nk = k // tk
    dims = {'nn': ((1,), (0,)), 'nt': ((1,), (1,)), 'tn': ((0,), (0,))}[mode]

    def body(*refs):
        a_ref, b_ref = refs[:2]
        o_ref, acc_ref = refs[-2:]
        kk = pl.program_id(2)

        @pl.when(kk == 0)
        def _():
            acc_ref[...] = jnp.zeros_like(acc_ref)

        acc_ref[...] += _dot(a_ref[...].astype(BF16), b_ref[...].astype(BF16), dims)

        @pl.when(kk == nk - 1)
        def _():
            res = acc_ref[...]
            if add is not None:
                res = res + refs[2][...].astype(F32)
            o_ref[...] = res.astype(o_ref.dtype)

    a_spec = pl.BlockSpec((tk, tm), lambda i, j, kk: (kk, i)) if mode == 'tn' else pl.BlockSpec((tm, tk), lambda i, j, kk: (i, kk))
    b_spec = pl.BlockSpec((tn, tk), lambda i, j, kk: (j, kk)) if mode == 'nt' else pl.BlockSpec((tk, tn), lambda i, j, kk: (kk, j))
    o_spec = pl.BlockSpec((tm, tn), lambda i, j, kk: (i, j))
    ops = (a, b) if add is None else (a, b, add)
    return _call(body, name=name, out_shape=_sds((m, n), out_dtype), grid=(m // tm, n // tn, nk),
                 in_specs=[a_spec, b_spec] + ([] if add is None else [o_spec]), out_specs=o_spec,
                 scratch_shapes=[pltpu.VMEM((tm, tn), F32)], semantics=("parallel", "parallel", "arbitrary"))(*ops)


class Win(NamedTuple):
    arr: jax.Array
    off: int
    width: int

    @property
    def shape(self):
        return (self.arr.shape[0], self.width)


def _win(x):
    return x if isinstance(x, Win) else Win(x, 0, x.shape[1])


def _col_tile(wins, target):
    ct = (min(target, min(w.width for w in wins)) // LANES) * LANES
    while ct > LANES and any(w.width % ct or w.off % ct for w in wins):
        ct -= LANES
    assert all(w.width % ct == 0 and w.off % ct == 0 for w in wins), [(w.off, w.width) for w in wins]
    return ct


def _wspec(rows, ct, w, row_first=True):
    base = w.off // ct
    if row_first:
        return pl.BlockSpec((rows, ct), lambda i, j: (i, base + j))
    return pl.BlockSpec((rows, ct), lambda j, i: (i, base + j))


def _bd_expand(a, b, name, tm=1024):
    a = _win(a)
    t = a.shape[0]
    jn, ka, nb = b.shape
    r = nb // LANES
    tm = _tile(t, tm, SUBLANES)
    assert a.shape[1] == jn * ka and a.off % ka == 0
    abase = a.off // ka

    def body(a_ref, b_ref, o_ref):
        o_ref[...] = _nn(a_ref[...].astype(BF16), b_ref[...].astype(BF16)).reshape(tm, r, LANES)

    return _call(body, name=name, out_shape=_sds((t, jn * r, LANES), F32), grid=(t // tm, jn),
                 in_specs=[pl.BlockSpec((tm, ka), lambda i, j: (i, abase + j)),
                           pl.BlockSpec((None, ka, nb), lambda i, j: (j, 0, 0))],
                 out_specs=pl.BlockSpec((tm, r, LANES), lambda i, j: (i, j, 0)),
                 semantics=("parallel", "parallel"))(a.arr, b)


def _bd_reduce(a3, b, out_dtype, name, tm=1024, add=None):
    t = a3.shape[0]
    jn, ka, nb = b.shape
    r = nb // LANES
    tm = _tile(t, tm, SUBLANES)

    def body(*refs):
        a_ref, b_ref, o_ref = refs[0], refs[1], refs[-1]
        res = _nt(a_ref[...].reshape(tm, nb).astype(BF16), b_ref[...].astype(BF16))
        if add is not None:
            res = res + refs[2][...].astype(F32)
        o_ref[...] = res.astype(o_ref.dtype)

    o_spec = pl.BlockSpec((tm, ka), lambda i, j: (i, j))
    ops = (a3, b) if add is None else (a3, b, add)
    return _call(body, name=name, out_shape=_sds((t, jn * ka), out_dtype), grid=(t // tm, jn),
                 in_specs=[pl.BlockSpec((tm, r, LANES), lambda i, j: (i, j, 0)),
                           pl.BlockSpec((None, ka, nb), lambda i, j: (j, 0, 0))] + ([] if add is None else [o_spec]),
                 out_specs=o_spec, semantics=("parallel", "parallel"))(*ops)


def _bd_outer(a, b3, ka, name, tk=1024):
    a = _win(a)
    t = a.shape[0]
    jn = a.shape[1] // ka
    r = b3.shape[1] // jn
    nb = r * LANES
    assert a.off % ka == 0
    abase = a.off // ka
    tk = _tile(t, tk, SUBLANES * 2)

    def body(a_ref, b_ref, o_ref):
        @pl.when(pl.program_id(1) == 0)
        def _():
            o_ref[...] = jnp.zeros_like(o_ref)

        o_ref[...] += _tn(a_ref[...].astype(BF16), b_ref[...].reshape(tk, nb).astype(BF16))

    return _call(body, name=name, out_shape=_sds((jn, ka, nb), F32), grid=(jn, t // tk),
                 in_specs=[pl.BlockSpec((tk, ka), lambda j, kk: (kk, abase + j)),
                           pl.BlockSpec((tk, r, LANES), lambda j, kk: (kk, j, 0))],
                 out_specs=pl.BlockSpec((None, ka, nb), lambda j, kk: (j, 0, 0)),
                 semantics=("parallel", "arbitrary"))(a.arr, b3)


def _concat_cols(parts, name, tb=256):
    t = parts[0].shape[0]
    tb = _tile(t, tb, SUBLANES * 2)
    widths = [p.shape[1] for p in parts]
    assert all(w % LANES == 0 for w in widths)

    def body(*refs):
        o_ref, off = refs[-1], 0
        for r, w in zip(refs[:-1], widths):
            o_ref[:, off:off + w] = r[...]
            off += w

    return _call(body, name=name, out_shape=_sds((t, sum(widths)), parts[0].dtype), grid=(t // tb,),
                 in_specs=[pl.BlockSpec((tb, w), lambda i: (i, 0)) for w in widths],
                 out_specs=pl.BlockSpec((tb, sum(widths)), lambda i: (i, 0)), semantics=("parallel",))(*parts)


def _rowwise_tiles(rows, tb, col_tile):
    rows = [_win(r) for r in rows]
    t = rows[0].shape[0]
    tb = _tile(t, tb, SUBLANES * 2)
    if col_tile is None:
        assert all(r.off % r.width == 0 for r in rows)
        return rows, tb, None, 1
    ct = _col_tile(rows, col_tile)
    tb = _tile(t, max(tb, BLOCK_BYTES // (4 * ct)), SUBLANES * 2)
    return rows, tb, ct, rows[0].width // ct


def _rowwise(fn, rows, params, out_widths, out_dtypes, name, tb=256, col_tile=None):
    rows, tb, ct, ncol = _rowwise_tiles(rows, tb, col_tile)
    t = rows[0].shape[0]
    nr, npar = len(rows), len(params)

    def body(*refs):
        ins = [r[...].astype(F32) for r in refs[:nr + npar]]
        outs = fn(*ins)
        for o_ref, o in zip(refs[nr + npar:], outs):
            o_ref[...] = o.astype(o_ref.dtype)

    in_specs = [_wspec(tb, ct or r.width, r) for r in rows]
    in_specs += [pl.BlockSpec((1, ct or p.shape[1]), lambda i, j: (0, j)) for p in params]
    out_shape = [_sds((t, w), d) for w, d in zip(out_widths, out_dtypes)]
    out_specs = [pl.BlockSpec((tb, ct or w), lambda i, j: (i, j)) for w in out_widths]
    return _call(body, name=name, out_shape=out_shape, grid=(t // tb, ncol), in_specs=in_specs, out_specs=out_specs,
                 semantics=("parallel", "parallel"))(*[r.arr for r in rows], *params)


def _rowwise_bwd(fn, rows, params, cts, row_grad_dtypes, name, tb=256, addend=None, col_tile=None):
    rows, tb, ct, ncol = _rowwise_tiles(rows, tb, col_tile)
    t = rows[0].shape[0]
    nr, npar, nc = len(rows), len(params), len(cts)
    keep = [i for i, d in enumerate(row_grad_dtypes) if d is not None]
    nadd = 0 if addend is None else 1

    def body(*refs):
        ins = [r[...].astype(F32) for r in refs[:nr + npar]]
        ct = [r[...].astype(F32) for r in refs[nr + npar:nr + npar + nc]]
        _, vjp = jax.vjp(fn, *ins)
        grads = vjp(tuple(ct))
        out_refs = refs[nr + npar + nc + nadd:]
        for o_ref, i in zip(out_refs[:len(keep)], keep):
            g = grads[i]
            if nadd and i == 0:
                g = g + refs[nr + npar + nc][...].astype(F32)
            o_ref[...] = g.astype(o_ref.dtype)

        @pl.when(pl.program_id(1) == 0)
        def _():
            for o_ref in out_refs[len(keep):]:
                o_ref[...] = jnp.zeros_like(o_ref)

        for o_ref, g in zip(out_refs[len(keep):], grads[nr:]):
            o_ref[...] += g

    def plain(w):
        return pl.BlockSpec((tb, ct or w), lambda j, i: (i, j))

    in_specs = [_wspec(tb, ct or r.width, r, row_first=False) for r in rows]
    in_specs += [pl.BlockSpec((1, ct or p.shape[1]), lambda j, i: (0, j)) for p in params]
    in_specs += [plain(c.shape[1]) for c in cts]
    extra = []
    if nadd:
        in_specs += [plain(addend.shape[1])]
        extra = [addend]
    out_shape = [_sds(rows[i].shape, row_grad_dtypes[i]) for i in keep] + [_sds(p.shape, F32) for p in params]
    out_specs = [plain(rows[i].width) for i in keep]
    out_specs += [pl.BlockSpec((1, ct or p.shape[1]), lambda j, i: (0, j)) for p in params]
    return _call(body, name=name, out_shape=out_shape, grid=(ncol, t // tb), in_specs=in_specs, out_specs=out_specs,
                 semantics=("parallel", "arbitrary"))(*[r.arr for r in rows], *params, *cts, *extra)


def _chunk_masks(c):
    row = lax.broadcasted_iota(jnp.int32, (c, c), 0)
    col = lax.broadcasted_iota(jnp.int32, (c, c), 1)
    causal = row >= col
    strict = row > col
    return causal, strict, causal.astype(F32), (row > col).astype(F32), (row == col).astype(F32)


def _lane_pick(blk, idx):
    lane = lax.broadcasted_iota(jnp.int32, blk.shape, 1)
    return jnp.sum(jnp.where(lane == idx, blk, 0.0), axis=1, keepdims=True)


def _bdot(a, b, ca, cb, prec=None):
    return lax.dot_general(a, b, (((ca,), (cb,)), ((0,), (0,))), precision=prec, preferred_element_type=F32)


def _bnn(a, b, prec=None):
    return _bdot(a, b, 2, 1, prec)


def _bnt(a, b, prec=None):
    return _bdot(a, b, 2, 2, prec)


def _btn(a, b, prec=None):
    return _bdot(a, b, 1, 1, prec)


def _unit_lower_inverse(a_mat, eye):
    x = -a_mat
    t_inv = eye + x
    p = x
    for _ in range(int(math.log2(a_mat.shape[-1])) - 1):
        p = _bnn(p, p, HI)
        t_inv = t_inv + _bnn(t_inv, p, HI)
    return t_inv


@jax.custom_vjp
def _saved_inverse(a_mat, t_saved):
    return t_saved


def _saved_inverse_fwd(a_mat, t_saved):
    return t_saved, t_saved


def _saved_inverse_bwd(t_inv, ct):
    return -_bnt(_btn(t_inv, ct, HI), t_inv, HI), jnp.zeros_like(t_inv)


_saved_inverse.defvjp(_saved_inverse_fwd, _saved_inverse_bwd)


def _gdn_heads(q, k, v, z, braw, araw, alog, dtb, nw, s_in, t_saved=None):
    b, c, d = q.shape
    causal, strict, lower, upper_t, eye = _chunk_masks(c)
    lower_b = jnp.broadcast_to(lower[None], (b, c, c))
    qn = q * lax.rsqrt(jnp.sum(q * q, axis=-1, keepdims=True) + NORM_EPS) * (d ** -0.5)
    kn = k * lax.rsqrt(jnp.sum(k * k, axis=-1, keepdims=True) + NORM_EPS)
    beta = _sigmoid(braw)
    g = -jnp.exp(alog) * _softplus(araw + dtb)
    dlog = _bnn(lower_b, g * upper_t[None], HI)
    dm = jnp.where(causal[None], jnp.exp(dlog), 0.0)
    g_lanes = jnp.broadcast_to(g, (b, c, d))
    gc = _bnn(lower_b, g_lanes, HI)
    gl = jnp.sum(g_lanes, axis=1, keepdims=True)
    eg = jnp.exp(gc)
    kb = kn * beta
    a_mat = jnp.where(strict[None], _bnt(kb, kn) * dm, 0.0)
    t_inv = _unit_lower_inverse(a_mat, eye[None]) if t_saved is None else _saved_inverse(a_mat, t_saved)
    r = beta * (v - eg * _bnn(kn, s_in))
    v_new = _bnn(t_inv, r)
    qk = _bnt(qn, kn) * dm
    out = eg * _bnn(qn, s_in) + _bnn(qk, v_new)
    k_tail = kn * jnp.exp(gl - gc)
    s_out = s_in * jnp.exp(gl) + _btn(k_tail, v_new)
    y = out * lax.rsqrt(jnp.mean(out * out, axis=-1, keepdims=True) + NORM_EPS) * nw[None] * _silu(z)
    if t_saved is None:
        return y, s_out, t_inv
    return y, s_out


def _gdn_stack(refs, hb, d, h, first_head):
    q_ref, k_ref, v_ref, z_ref, b_ref, a_ref, alog_ref, dtb_ref = refs
    sls = [slice(i * d, (i + 1) * d) for i in range(hb)]
    heads = [first_head + i for i in range(hb)]
    wide = [jnp.stack([r[:, sl] for sl in sls]) for r in (q_ref, k_ref, v_ref, z_ref)]
    cols = [jnp.stack([_lane_pick(r[...], hd) for hd in heads]) for r in (b_ref, a_ref, alog_ref, dtb_ref)]
    return wide + cols


GDN_HEADS_PER_STEP = 8


def _gdn_chunks_fwd(sqkv, z, braw, araw, alog, dtb, nw, hb=GDN_HEADS_PER_STEP):
    t, w3 = sqkv.shape
    w = w3 // 3
    d = GDN_HEAD_DIM
    h = w // d
    hb = min(hb, h)
    hg = h // hb
    c = CHUNK
    nc = t // c

    def body(q_ref, k_ref, v_ref, z_ref, b_ref, a_ref, alog_ref, dtb_ref, nw_ref, y_ref, ssave_ref, tsave_ref,
             s_ref):
        @pl.when(pl.program_id(1) == 0)
        def _():
            s_ref[...] = jnp.zeros_like(s_ref)

        s_in = s_ref[...]
        ssave_ref[...] = s_in
        args = _gdn_stack((q_ref, k_ref, v_ref, z_ref, b_ref, a_ref, alog_ref, dtb_ref), hb, d, h,
                          pl.program_id(0) * hb)
        y, s_out, t_inv = _gdn_heads(*args, nw_ref[...], s_in)
        for i in range(hb):
            y_ref[:, i * d:(i + 1) * d] = y[i]
        s_ref[...] = s_out
        tsave_ref[...] = t_inv

    blk = (c, hb * d)
    z = _win(z)
    zb = z.off // (hb * d)
    assert z.off % (hb * d) == 0
    in_specs = [pl.BlockSpec(blk, lambda g, n: (n, g)), pl.BlockSpec(blk, lambda g, n: (n, hg + g)),
                pl.BlockSpec(blk, lambda g, n: (n, 2 * hg + g)), pl.BlockSpec(blk, lambda g, n: (n, zb + g)),
                pl.BlockSpec((c, h), lambda g, n: (n, 0)), pl.BlockSpec((c, h), lambda g, n: (n, 0)),
                pl.BlockSpec((1, h), lambda g, n: (0, 0)), pl.BlockSpec((1, h), lambda g, n: (0, 0)),
                pl.BlockSpec((1, d), lambda g, n: (0, 0))]
    out_shape = [_sds((t, w), F32), _sds((hg, nc, hb, d, d), F32), _sds((hg, nc, hb, c, c), F32)]
    out_specs = [pl.BlockSpec(blk, lambda g, n: (n, g)),
                 pl.BlockSpec((None, None, hb, d, d), lambda g, n: (g, n, 0, 0, 0)),
                 pl.BlockSpec((None, None, hb, c, c), lambda g, n: (g, n, 0, 0, 0))]
    return _call(body, name="gdn_chunks_fwd", out_shape=out_shape, grid=(hg, nc), in_specs=in_specs,
                 out_specs=out_specs, scratch_shapes=[pltpu.VMEM((hb, d, d), F32)],
                 semantics=("parallel", "arbitrary"))(sqkv, sqkv, sqkv, z.arr, braw, araw, alog, dtb, nw)


def _gdn_chunks_bwd(sqkv, z, braw, araw, alog, dtb, nw, ssave, tsave, dy, hb=GDN_HEADS_PER_STEP):
    t, w3 = sqkv.shape
    w = w3 // 3
    d = GDN_HEAD_DIM
    h = w // d
    hb = min(hb, h)
    hg = h // hb
    c = CHUNK
    nc = t // c

    def body(q_ref, k_ref, v_ref, z_ref, b_ref, a_ref, alog_ref, dtb_ref, nw_ref, ssave_ref, tsave_ref, dy_ref,
             dsq_ref, dz_ref, db_ref, da_ref, dalog_ref, ddtb_ref, dnw_ref, ds_ref):
        first = jnp.logical_and(pl.program_id(0) == 0, pl.program_id(1) == 0)

        @pl.when(pl.program_id(1) == 0)
        def _():
            ds_ref[...] = jnp.zeros_like(ds_ref)

        @pl.when(first)
        def _():
            dalog_ref[...] = jnp.zeros_like(dalog_ref)
            ddtb_ref[...] = jnp.zeros_like(ddtb_ref)
            dnw_ref[...] = jnp.zeros_like(dnw_ref)

        lane_h = lax.broadcasted_iota(jnp.int32, (1, h), 1)
        db_acc = jnp.zeros((c, h), F32)
        da_acc = jnp.zeros((c, h), F32)
        args = _gdn_stack((q_ref, k_ref, v_ref, z_ref, b_ref, a_ref, alog_ref, dtb_ref), hb, d, h,
                          pl.program_id(0) * hb)
        t_saved = tsave_ref[...]
        _, vjp = jax.vjp(lambda *a: _gdn_heads(*a, t_saved=t_saved), *args, nw_ref[...], ssave_ref[...])
        dyb = jnp.stack([dy_ref[:, i * d:(i + 1) * d] for i in range(hb)])
        dq, dk, dv, dz, db, da, dalog, ddtb, dnw, ds_in = vjp((dyb, ds_ref[...]))
        for i in range(hb):
            sl = slice(i * d, (i + 1) * d)
            dsq_ref[:, i * d:(i + 1) * d] = dq[i]
            dsq_ref[:, w + i * d:w + (i + 1) * d] = dk[i]
            dsq_ref[:, 2 * w + i * d:2 * w + (i + 1) * d] = dv[i]
            dz_ref[:, sl] = dz[i].astype(dz_ref.dtype)
            onehot = (lane_h == pl.program_id(0) * hb + i).astype(F32)
            db_acc = db_acc + db[i] * onehot
            da_acc = da_acc + da[i] * onehot
            dalog_ref[...] += dalog[i] * onehot
            ddtb_ref[...] += ddtb[i] * onehot
        dnw_ref[...] += dnw
        ds_ref[...] = ds_in
        db_ref[...] = db_acc
        da_ref[...] = da_acc

    blk = (c, hb * d)
    rev = lambda n: nc - 1 - n
    z = _win(z)
    zb = z.off // (hb * d)
    assert z.off % (hb * d) == 0
    in_specs = [pl.BlockSpec(blk, lambda g, n: (rev(n), g)), pl.BlockSpec(blk, lambda g, n: (rev(n), hg + g)),
                pl.BlockSpec(blk, lambda g, n: (rev(n), 2 * hg + g)), pl.BlockSpec(blk, lambda g, n: (rev(n), zb + g)),
                pl.BlockSpec((c, h), lambda g, n: (rev(n), 0)), pl.BlockSpec((c, h), lambda g, n: (rev(n), 0)),
                pl.BlockSpec((1, h), lambda g, n: (0, 0)), pl.BlockSpec((1, h), lambda g, n: (0, 0)),
                pl.BlockSpec((1, d), lambda g, n: (0, 0)),
                pl.BlockSpec((None, None, hb, d, d), lambda g, n: (g, rev(n), 0, 0, 0)),
                pl.BlockSpec((None, None, hb, c, c), lambda g, n: (g, rev(n), 0, 0, 0)),
                pl.BlockSpec(blk, lambda g, n: (rev(n), g))]
    assert hg == 1
    out_shape = [_sds((t, w3), F32), _sds((t, w), BF16),
                 _sds((hg, t, h), F32), _sds((hg, t, h), F32), _sds((1, h), F32), _sds((1, h), F32), _sds((1, d), F32)]
    out_specs = [pl.BlockSpec((c, w3), lambda g, n: (rev(n), 0)), pl.BlockSpec(blk, lambda g, n: (rev(n), g))]
    out_specs += [pl.BlockSpec((None, c, h), lambda g, n: (g, rev(n), 0))] * 2
    out_specs += [pl.BlockSpec((1, h), lambda g, n: (0, 0)), pl.BlockSpec((1, h), lambda g, n: (0, 0)),
                  pl.BlockSpec((1, d), lambda g, n: (0, 0))]
    return _call(body, name="gdn_chunks_bwd", out_shape=out_shape, grid=(hg, nc), in_specs=in_specs,
                 out_specs=out_specs, scratch_shapes=[pltpu.VMEM((hb, d, d), F32)],
                 semantics=("arbitrary", "arbitrary"))(sqkv, sqkv, sqkv, z.arr, braw, araw, alog, dtb, nw, ssave, tsave,
                                                       dy)


def _m2_groups(xs, z, bm, cm, dtraws, alogs, dtbs, dsks, nw, st):
    g, c, gw = xs.shape
    rep = len(dtraws)
    causal, _, lower, upper_t, _ = _chunk_masks(c)
    lower_b = jnp.broadcast_to(lower[None], (g, c, c))
    lane_head = lax.broadcasted_iota(jnp.int32, (1, 1, gw), 2) // M2_HEAD_DIM

    def expand(cols):
        res = jnp.broadcast_to(cols[-1], (g, cols[-1].shape[1], gw))
        for i in reversed(range(rep - 1)):
            res = jnp.where(lane_head == i, cols[i], res)
        return res

    dts = [_softplus(dtraws[i] + dtbs[i]) for i in range(rep)]
    adts = [-jnp.exp(alogs[i]) * dts[i] for i in range(rep)]
    dt_l, adt_l, dsk_l = expand(dts), expand(adts), expand(dsks)
    xdt = xs * dt_l
    acum = _bnn(lower_b, adt_l, HI)
    alast = jnp.sum(adt_l, axis=1, keepdims=True)
    scores = _bnt(cm, bm)
    y = jnp.exp(acum) * _bnn(cm, st) + dsk_l * xs
    for i in range(rep):
        seg = jnp.where(causal[None], jnp.exp(_bnn(lower_b, adts[i] * upper_t[None], HI)), 0.0)
        y = y + _bnn(scores * seg, jnp.where(lane_head == i, xdt, 0.0))
    st_out = st * jnp.exp(alast) + _btn(bm, xdt * jnp.exp(alast - acum))
    y2 = y * _silu(z)
    out = y2 * lax.rsqrt(jnp.mean(y2 * y2, axis=-1, keepdims=True) + NORM_EPS) * nw
    return out, st_out


def _m2_dims(sxbc, z):
    t = sxbc.shape[0]
    w2 = z.shape[1]
    g = M2_GROUPS
    n = M2_STATE
    assert sxbc.shape[1] == w2 + 2 * g * n
    gw = w2 // g
    return t, w2, g, n, gw, gw // M2_HEAD_DIM, t // CHUNK


def _m2_args(refs, g, n, gw, rep, st):
    sx_ref, z_ref, dt_ref, alog_ref, dtb_ref, dsk_ref, nw_ref = refs
    w2 = g * gw
    xs = jnp.stack([sx_ref[:, i * gw:(i + 1) * gw] for i in range(g)])
    z = jnp.stack([z_ref[:, i * gw:(i + 1) * gw] for i in range(g)])
    bm = jnp.stack([sx_ref[:, w2 + i * n:w2 + (i + 1) * n] for i in range(g)])
    cm = jnp.stack([sx_ref[:, w2 + (g + i) * n:w2 + (g + i + 1) * n] for i in range(g)])
    nw = jnp.stack([nw_ref[:, i * gw:(i + 1) * gw] for i in range(g)])

    def cols(ref):
        blk = ref[...]
        return [jnp.stack([_lane_pick(blk, gi * rep + i) for gi in range(g)]) for i in range(rep)]

    return xs, z, bm, cm, cols(dt_ref), cols(alog_ref), cols(dtb_ref), cols(dsk_ref), nw, st


def _m2_chunks_fwd(sxbc, z, dtraw, alog, dtb, dsk, nw):
    t, w2, g, n, gw, rep, nc = _m2_dims(sxbc, z)
    hm = dtraw.shape[1]
    c = CHUNK
    wx = sxbc.shape[1]

    def body(sx_ref, z_ref, dt_ref, alog_ref, dtb_ref, dsk_ref, nw_ref, y_ref, ssave_ref, st_ref):
        @pl.when(pl.program_id(0) == 0)
        def _():
            st_ref[...] = jnp.zeros_like(st_ref)

        st = st_ref[...]
        ssave_ref[...] = st
        y, st_out = _m2_groups(*_m2_args((sx_ref, z_ref, dt_ref, alog_ref, dtb_ref, dsk_ref, nw_ref), g, n, gw, rep, st))
        for i in range(g):
            y_ref[:, i * gw:(i + 1) * gw] = y[i]
        st_ref[...] = st_out

    z = _win(z)
    zb = z.off // w2
    assert z.off % w2 == 0
    in_specs = [pl.BlockSpec((c, wx), lambda k: (k, 0)), pl.BlockSpec((c, w2), lambda k: (k, zb)),
                pl.BlockSpec((c, hm), lambda k: (k, 0)),
                pl.BlockSpec((1, hm), lambda k: (0, 0)), pl.BlockSpec((1, hm), lambda k: (0, 0)),
                pl.BlockSpec((1, hm), lambda k: (0, 0)), pl.BlockSpec((1, w2), lambda k: (0, 0))]
    out_shape = [_sds((t, w2), F32), _sds((nc, g, n, gw), F32)]
    out_specs = [pl.BlockSpec((c, w2), lambda k: (k, 0)), pl.BlockSpec((None, g, n, gw), lambda k: (k, 0, 0, 0))]
    return _call(body, name="m2_chunks_fwd", out_shape=out_shape, grid=(nc,), in_specs=in_specs,
                 out_specs=out_specs, scratch_shapes=[pltpu.VMEM((g, n, gw), F32)],
                 semantics=("arbitrary",))(sxbc, z.arr, dtraw, alog, dtb, dsk, nw)


def _m2_chunks_bwd(sxbc, z, dtraw, alog, dtb, dsk, nw, ssave, dy):
    t, w2, g, n, gw, rep, nc = _m2_dims(sxbc, z)
    hm = dtraw.shape[1]
    c = CHUNK

    wx = sxbc.shape[1]

    def body(sx_ref, z_ref, dt_ref, alog_ref, dtb_ref, dsk_ref, nw_ref, ssave_ref, dy_ref,
             dsx_ref, dz_ref, ddt_ref, dalog_ref, ddtb_ref, ddsk_ref, dnw_ref, dst_ref):
        @pl.when(pl.program_id(0) == 0)
        def _():
            dst_ref[...] = jnp.zeros_like(dst_ref)
            dnw_ref[...] = jnp.zeros_like(dnw_ref)
            dalog_ref[...] = jnp.zeros_like(dalog_ref)
            ddtb_ref[...] = jnp.zeros_like(ddtb_ref)
            ddsk_ref[...] = jnp.zeros_like(ddsk_ref)

        args = _m2_args((sx_ref, z_ref, dt_ref, alog_ref, dtb_ref, dsk_ref, nw_ref), g, n, gw, rep, ssave_ref[...])
        _, vjp = jax.vjp(_m2_groups, *args)
        dyb = jnp.stack([dy_ref[:, i * gw:(i + 1) * gw] for i in range(g)])
        dxs, dz, dbm, dcm, ddts, dalogs, ddtbs, ddsks, dnw, dst = vjp((dyb, dst_ref[...]))
        dst_ref[...] = dst
        lane_h = lax.broadcasted_iota(jnp.int32, (1, hm), 1)
        ddt = jnp.zeros((c, hm), F32)
        for gi in range(g):
            dsx_ref[:, gi * gw:(gi + 1) * gw] = dxs[gi]
            dsx_ref[:, w2 + gi * n:w2 + (gi + 1) * n] = dbm[gi]
            dsx_ref[:, w2 + (g + gi) * n:w2 + (g + gi + 1) * n] = dcm[gi]
            dz_ref[:, gi * gw:(gi + 1) * gw] = dz[gi].astype(dz_ref.dtype)
            dnw_ref[:, gi * gw:(gi + 1) * gw] += dnw[gi]
            for i in range(rep):
                onehot = (lane_h == gi * rep + i).astype(F32)
                ddt = ddt + ddts[i][gi] * onehot
                dalog_ref[...] += dalogs[i][gi] * onehot
                ddtb_ref[...] += ddtbs[i][gi] * onehot
                ddsk_ref[...] += ddsks[i][gi] * onehot
        ddt_ref[...] = ddt

    rev = lambda k: nc - 1 - k
    z = _win(z)
    zb = z.off // w2
    assert z.off % w2 == 0
    in_specs = [pl.BlockSpec((c, wx), lambda k: (rev(k), 0)), pl.BlockSpec((c, w2), lambda k: (rev(k), zb)),
                pl.BlockSpec((c, hm), lambda k: (rev(k), 0)),
                pl.BlockSpec((1, hm), lambda k: (0, 0)), pl.BlockSpec((1, hm), lambda k: (0, 0)),
                pl.BlockSpec((1, hm), lambda k: (0, 0)), pl.BlockSpec((1, w2), lambda k: (0, 0)),
                pl.BlockSpec((None, g, n, gw), lambda k: (rev(k), 0, 0, 0)),
                pl.BlockSpec((c, w2), lambda k: (rev(k), 0))]
    out_shape = [_sds((t, wx), F32), _sds((t, w2), BF16), _sds((t, hm), F32), _sds((1, hm), F32), _sds((1, hm), F32),
                 _sds((1, hm), F32), _sds((1, w2), F32)]
    out_specs = [pl.BlockSpec((c, wx), lambda k: (rev(k), 0)), pl.BlockSpec((c, w2), lambda k: (rev(k), 0)),
                 pl.BlockSpec((c, hm), lambda k: (rev(k), 0)),
                 pl.BlockSpec((1, hm), lambda k: (0, 0)), pl.BlockSpec((1, hm), lambda k: (0, 0)),
                 pl.BlockSpec((1, hm), lambda k: (0, 0)), pl.BlockSpec((1, w2), lambda k: (0, 0))]
    return _call(body, name="m2_chunks_bwd", out_shape=out_shape, grid=(nc,), in_specs=in_specs,
                 out_specs=out_specs, scratch_shapes=[pltpu.VMEM((g, n, gw), F32)],
                 semantics=("arbitrary",))(sxbc, z.arr, dtraw, alog, dtb, dsk, nw, ssave, dy)


def _s5_scan(bu3, aa3, bb3, reverse, name, tb=256):
    t, rtot, _ = bu3.shape
    jn = rtot // SUBLANES
    tb = _tile(t, tb, SUBLANES)
    nb = t // tb

    def body(bu_ref, aa_ref, bb_ref, s_ref, st_ref):
        @pl.when(pl.program_id(0) == 0)
        def _():
            st_ref[...] = jnp.zeros_like(st_ref)

        tiles = [slice(j * SUBLANES, (j + 1) * SUBLANES) for j in range(jn)]
        aa = [aa_ref[tl, :] for tl in tiles]
        bb = [bb_ref[tl, :] for tl in tiles]

        def step(k, carry):
            r = tb - 1 - k if reverse else k
            new = []
            for j, tl in enumerate(tiles):
                sj, wj = carry[2 * j], carry[2 * j + 1]
                xj = bu_ref[r, tl, :]
                nj = aa[j] * sj + bb[j] * wj + xj
                s_ref[r, tl, :] = nj
                new += [nj, aa[j] * wj - bb[j] * sj + pltpu.roll(xj, SUBLANES // 2, 0)]
            return tuple(new)

        init = []
        for tl in tiles:
            init += [st_ref[tl, :], pltpu.roll(st_ref[tl, :], SUBLANES // 2, 0)]
        last = lax.fori_loop(0, tb, step, tuple(init), unroll=8)
        for j, tl in enumerate(tiles):
            st_ref[tl, :] = last[2 * j]

    rb = (lambda i: nb - 1 - i) if reverse else (lambda i: i)
    return _call(body, name=name, out_shape=_sds(bu3.shape, F32), grid=(nb,),
                 in_specs=[pl.BlockSpec((tb, rtot, LANES), lambda i: (rb(i), 0, 0)),
                           pl.BlockSpec((rtot, LANES), lambda i: (0, 0)), pl.BlockSpec((rtot, LANES), lambda i: (0, 0))],
                 out_specs=pl.BlockSpec((tb, rtot, LANES), lambda i: (rb(i), 0, 0)),
                 scratch_shapes=[pltpu.VMEM((rtot, LANES), F32)], semantics=("arbitrary",))(bu3, aa3, bb3)


def _s5_da(ds3, s3, tb=256):
    t, rtot, _ = ds3.shape
    tb = _tile(t, tb, SUBLANES)
    nb = t // tb

    def body(ds_ref, s_ref, halo_ref, p_ref, q_ref):
        i = pl.program_id(0)

        @pl.when(i == 0)
        def _():
            p_ref[...] = jnp.zeros_like(p_ref)
            q_ref[...] = jnp.zeros_like(q_ref)

        for j in range(rtot // SUBLANES):
            tl = slice(j * SUBLANES, (j + 1) * SUBLANES)
            prev = jnp.where(i == 0, 0.0, halo_ref[:, tl, :])
            sh = jnp.concatenate([prev, s_ref[0:tb - 1, tl, :]], axis=0)
            d = ds_ref[:, tl, :]
            p_ref[tl, :] += jnp.sum(d * sh, axis=0)
            q_ref[tl, :] += jnp.sum(d * pltpu.roll(sh, SUBLANES // 2, 1), axis=0)

    blk = pl.BlockSpec((tb, rtot, LANES), lambda i: (i, 0, 0))
    acc = pl.BlockSpec((rtot, LANES), lambda i: (0, 0))
    return _call(body, name="s5_da", out_shape=[_sds((rtot, LANES), F32)] * 2, grid=(nb,),
                 in_specs=[blk, blk, pl.BlockSpec((1, rtot, LANES), lambda i: (jnp.maximum(i * tb - 1, 0), 0, 0))],
                 out_specs=[acc, acc], semantics=("arbitrary",))(ds3, s3, s3)


def _conv_rows(t, tb):
    tb = _tile(t, tb, SUBLANES * 2)
    return tb, t // tb, tb // SUBLANES


def _shift_down(x, above, s):
    n = x.shape[0]
    y = pltpu.roll(x, s, 0)
    row = lax.broadcasted_iota(jnp.int32, above.shape, 0)
    head = jnp.where(row < s, pltpu.roll(above, s, 0), y[:SUBLANES])
    return head if n == SUBLANES else jnp.concatenate([head, y[SUBLANES:]], axis=0)


def _shift_up(x, below, s):
    n = x.shape[0]
    y = pltpu.roll(x, n - s, 0)
    row = lax.broadcasted_iota(jnp.int32, below.shape, 0)
    tail = jnp.where(row >= SUBLANES - s, pltpu.roll(below, SUBLANES - s, 0), y[n - SUBLANES:])
    return tail if n == SUBLANES else jnp.concatenate([y[:n - SUBLANES], tail], axis=0)


def _conv_taps(x, above, w_ref, b_ref):
    xs = [x] + [_shift_down(x, above, s) for s in range(1, CONV_K)]
    c = b_ref[...] + w_ref[CONV_K - 1:CONV_K, :] * x
    for s in range(1, CONV_K):
        c = c + w_ref[CONV_K - 1 - s:CONV_K - s, :] * xs[s]
    return c, xs


CONV_COL_TILE = 1024


def _conv_specs(x, tb):
    x = _win(x)
    t, cwid = x.shape
    ct = _col_tile([x], CONV_COL_TILE)
    tb, nb, hb = _conv_rows(t, max(tb, BLOCK_BYTES // (4 * ct)))
    base = x.off // ct
    blk_x = pl.BlockSpec((tb, ct), lambda j, i: (i, base + j))
    prev_x = pl.BlockSpec((SUBLANES, ct), lambda j, i: (jnp.maximum(i * hb - 1, 0), base + j))
    next_x = pl.BlockSpec((SUBLANES, ct), lambda j, i: (jnp.minimum((i + 1) * hb, nb * hb - 1), base + j))
    blk = pl.BlockSpec((tb, ct), lambda j, i: (i, j))
    nxt = pl.BlockSpec((SUBLANES, ct), lambda j, i: (jnp.minimum((i + 1) * hb, nb * hb - 1), j))
    taps = pl.BlockSpec((CONV_K, ct), lambda j, i: (0, j))
    bias = pl.BlockSpec((1, ct), lambda j, i: (0, j))
    return x, tb, nb, cwid // ct, dict(blk_x=blk_x, prev_x=prev_x, next_x=next_x, blk=blk, nxt=nxt, taps=taps, bias=bias)


def _conv_fwd(x, w, b, name, tb=256):
    x, tb, nb, ncol, sp = _conv_specs(x, tb)
    t, cwid = x.shape

    def body(x_ref, halo_ref, w_ref, b_ref, o_ref):
        above = jnp.where(pl.program_id(1) == 0, 0.0, halo_ref[...])
        o_ref[...] = _silu(_conv_taps(x_ref[...], above, w_ref, b_ref)[0])

    return _call(body, name=name, out_shape=_sds((t, cwid), F32), grid=(ncol, nb),
                 in_specs=[sp['blk_x'], sp['prev_x'], sp['taps'], sp['bias']], out_specs=sp['blk'],
                 semantics=("parallel", "parallel"))(x.arr, x.arr, w, b)


def _dsilu(c):
    sg = _sigmoid(c)
    return sg * (1.0 + c * (1.0 - sg))


def _conv_bwd(x, w, b, ds, name, tb=256):
    x, tb, nb, ncol, sp = _conv_specs(x, tb)
    t, cwid = x.shape

    def body(x_ref, halo_ref, xn_ref, ds_ref, dsn_ref, w_ref, b_ref, dx_ref, dw_ref, db_ref):
        i = pl.program_id(1)

        @pl.when(i == 0)
        def _():
            dw_ref[...] = jnp.zeros_like(dw_ref)
            db_ref[...] = jnp.zeros_like(db_ref)

        x = x_ref[...]
        above = jnp.where(i == 0, 0.0, halo_ref[...])
        c, xs = _conv_taps(x, above, w_ref, b_ref)
        dc = ds_ref[...] * _dsilu(c)
        cn, _ = _conv_taps(xn_ref[...], x[tb - SUBLANES:], w_ref, b_ref)
        dcn = jnp.where(i == nb - 1, 0.0, dsn_ref[...] * _dsilu(cn))
        dx = w_ref[CONV_K - 1:CONV_K, :] * dc
        for s in range(1, CONV_K):
            dx = dx + w_ref[CONV_K - 1 - s:CONV_K - s, :] * _shift_up(dc, dcn, s)
        dx_ref[...] = dx.astype(dx_ref.dtype)
        for s in range(CONV_K):
            dw_ref[CONV_K - 1 - s:CONV_K - s, :] += jnp.sum(dc * xs[s], axis=0, keepdims=True)
        db_ref[...] += jnp.sum(dc, axis=0, keepdims=True)

    return _call(body, name=name, out_shape=[_sds((t, cwid), BF16), _sds((CONV_K, cwid), F32), _sds((1, cwid), F32)],
                 grid=(ncol, nb),
                 in_specs=[sp['blk_x'], sp['prev_x'], sp['next_x'], sp['blk'], sp['nxt'], sp['taps'], sp['bias']],
                 out_specs=[sp['blk'], sp['taps'], sp['bias']],
                 semantics=("parallel", "arbitrary"))(x.arr, x.arr, x.arr, ds, ds, w, b)


def _f_rms(x, w):
    return (x * lax.rsqrt(jnp.mean(x * x, axis=-1, keepdims=True) + NORM_EPS) * w,)


def _f_s5_post1(ymm, u, d_l):
    return (_gelu_tanh(ymm + d_l * u),)


def _f_s5_post2(yg, tt, gate, b):
    return (yg * _sigmoid(tt + b) * _silu(gate),)


def _f_merge(ma, mb, mc, pa, pb, pc):
    return (_sigmoid(ma) * pa + _sigmoid(mb) * pb + _sigmoid(mc) * pc,)


def _loss_and_grad(x, tgt, fw, tb=256):
    t, dm = x.shape
    tb = _tile(t, tb, SUBLANES * 2)

    def f(xb, wb, tb_):
        y = _f_rms(xb, wb)[0]
        e = y - tb_
        return 0.5 * jnp.sum(jnp.mean(e * e, axis=-1, keepdims=True), axis=0, keepdims=True)

    def body(x_ref, t_ref, w_ref, loss_ref, dx_ref, dw_ref):
        @pl.when(pl.program_id(0) == 0)
        def _():
            loss_ref[...] = jnp.zeros_like(loss_ref)
            dw_ref[...] = jnp.zeros_like(dw_ref)

        tgt_b = t_ref[...]
        val, vjp = jax.vjp(lambda a, b: f(a, b, tgt_b), x_ref[...], w_ref[...])
        dxb, dwb = vjp(jnp.ones((1, 1), F32))
        loss_ref[...] += val
        dx_ref[...] = dxb
        dw_ref[...] += dwb

    return _call(body, name="loss_and_grad", out_shape=[_sds((1, 1), F32), _sds((t, dm), F32), _sds((1, dm), F32)],
                 grid=(t // tb,),
                 in_specs=[pl.BlockSpec((tb, dm), lambda i: (i, 0)), pl.BlockSpec((tb, dm), lambda i: (i, 0)),
                           pl.BlockSpec((1, dm), lambda i: (0, 0))],
                 out_specs=[pl.BlockSpec((1, 1), lambda i: (0, 0)), pl.BlockSpec((tb, dm), lambda i: (i, 0)),
                            pl.BlockSpec((1, dm), lambda i: (0, 0))],
                 semantics=("arbitrary",))(x, tgt, fw)


FLAT_W = 1024


def _sum_parts(parts, name, tb=256):
    n, r, wd = parts.shape
    tb = _tile(r, tb, SUBLANES)

    def body(p_ref, o_ref):
        acc = p_ref[0]
        for k in range(1, n):
            acc = acc + p_ref[k]
        o_ref[...] = acc

    return _call(body, name=name, out_shape=_sds((r, wd), F32), grid=(r // tb,),
                 in_specs=[pl.BlockSpec((n, tb, wd), lambda i: (0, i, 0))],
                 out_specs=pl.BlockSpec((tb, wd), lambda i: (i, 0)), semantics=("parallel",))(parts)


BLOCK_BYTES = 1 << 20


def _rows_per_block(r, wd):
    return _tile(r, max(SUBLANES * 2, BLOCK_BYTES // (4 * wd) // (SUBLANES * 2) * (SUBLANES * 2)), SUBLANES * 2)


def _add_my_half(g4, recv, c_idx, name):
    p, _, r, wd = g4.shape
    tb = _rows_per_block(r, wd)

    def body(c_ref, g_ref, r_ref, o_ref, ob_ref):
        s = g_ref[...] + r_ref[...]
        o_ref[...] = s
        ob_ref[...] = s.astype(BF16)

    spec = pl.BlockSpec((None, tb, wd), lambda j, i, c_ref: (j, i, 0))
    return _call(body, name=name, out_shape=[_sds((p, r, wd), F32), _sds((p, r, wd), BF16)], grid=(p, r // tb),
                 in_specs=[pl.BlockSpec((None, None, tb, wd), lambda j, i, c_ref: (j, c_ref[0], i, 0)), spec],
                 out_specs=[spec, spec], semantics=("parallel", "parallel"), num_scalar_prefetch=1)(c_idx, g4, recv)


def _sum_chips(own, got, me_idx, name):
    p, r, wd = own.shape
    tb = _rows_per_block(r, wd)

    def body(me_ref, own_ref, got_ref, o_ref):
        me = me_ref[0]
        acc = None
        for k in range(p):
            part = jnp.where(me == k, own_ref[...], got_ref[k].astype(F32))
            acc = part if acc is None else acc + part
        o_ref[...] = acc

    return _call(body, name=name, out_shape=_sds((r, wd), F32), grid=(r // tb,),
                 in_specs=[pl.BlockSpec((None, tb, wd), lambda i, me_ref: (me_ref[0], i, 0)),
                           pl.BlockSpec((p, tb, wd), lambda i, me_ref: (0, i, 0))],
                 out_specs=pl.BlockSpec((tb, wd), lambda i, me_ref: (i, 0)),
                 semantics=("parallel",), num_scalar_prefetch=1)(me_idx, own, got)


def _adamw(w, g, m, v, name):
    r, wd = w.shape
    tb = _rows_per_block(r, wd)

    def body(w_ref, g_ref, m_ref, v_ref, d_ref, nm_ref, nv_ref):
        gg = g_ref[...]
        nm = ADAM_B1 * m_ref[...] + (1.0 - ADAM_B1) * gg
        nv = ADAM_B2 * v_ref[...] + (1.0 - ADAM_B2) * (gg * gg)
        m_hat = nm / (1.0 - ADAM_B1 ** ADAM_STEP)
        v_hat = nv / (1.0 - ADAM_B2 ** ADAM_STEP)
        d_ref[...] = -ADAM_LR * (m_hat / (jnp.sqrt(v_hat) + ADAM_EPS) + ADAM_WD * w_ref[...])
        nm_ref[...] = nm
        nv_ref[...] = nv

    spec = pl.BlockSpec((tb, wd), lambda i: (i, 0))
    return _call(body, name=name, out_shape=[_sds((r, wd), F32)] * 3, grid=(r // tb,), in_specs=[spec] * 4,
                 out_specs=[spec] * 3, semantics=("parallel",))(w, g, m, v)


def _here():
    return lax.axis_index("x"), lax.axis_index("y"), lax.axis_index("c")


def _comm_call(body, name, out_shape, n_sems, operands):
    anyspec = pl.BlockSpec(memory_space=pl.ANY)
    outs = out_shape if isinstance(out_shape, (list, tuple)) else [out_shape]
    return _call(body, name=name, out_shape=out_shape, in_specs=[anyspec] * len(operands),
                 out_specs=[anyspec] * len(outs) if isinstance(out_shape, (list, tuple)) else anyspec,
                 scratch_shapes=[pltpu.SemaphoreType.DMA((n_sems,)), pltpu.SemaphoreType.DMA((n_sems,)),
                                 pltpu.SemaphoreType.DMA(())])(*operands)


def _gather_chips(x, name):
    def body(x_ref, o_ref, send_sems, recv_sems, local_sem):
        xi, yi, ci = _here()
        chips = [(1 - xi, yi), (xi, 1 - yi), (1 - xi, 1 - yi)]
        mine = pltpu.make_async_copy(x_ref, o_ref.at[2 * xi + yi], local_sem)
        mine.start()

        def copy(k, slot, to):
            return pltpu.make_async_remote_copy(src_ref=x_ref, dst_ref=o_ref.at[slot], send_sem=send_sems.at[k],
                                                recv_sem=recv_sems.at[k], device_id=to, device_id_type=MESH)

        sends = [copy(k, 2 * xi + yi, (px, py, ci)) for k, (px, py) in enumerate(chips)]
        for cp in sends:
            cp.start()
        for k, (px, py) in enumerate(chips):
            copy(k, 2 * px + py, (px, py, ci)).wait_recv()
        for cp in sends:
            cp.wait_send()
        mine.wait()

    return _comm_call(body, name, _sds((4,) + x.shape, x.dtype), 3, (x,))


def _gather_all(x, name):
    def body(x_ref, o_ref, send_sems, recv_sems, local_sem):
        xi, yi, ci = _here()
        me = 4 * xi + 2 * yi + ci
        flips = [(fx, fy, fc) for fx in (0, 1) for fy in (0, 1) for fc in (0, 1)][1:]
        peers = [((1 - xi) if fx else xi, (1 - yi) if fy else yi, (1 - ci) if fc else ci) for fx, fy, fc in flips]
        mine = pltpu.make_async_copy(x_ref, o_ref.at[me], local_sem)
        mine.start()

        def copy(k, slot, to):
            return pltpu.make_async_remote_copy(src_ref=x_ref, dst_ref=o_ref.at[slot], send_sem=send_sems.at[k],
                                                recv_sem=recv_sems.at[k], device_id=to, device_id_type=MESH)

        sends = [copy(k, me, p) for k, p in enumerate(peers)]
        for cp in sends:
            cp.start()
        for k, (px, py, pc) in enumerate(peers):
            copy(k, 4 * px + 2 * py + pc, (px, py, pc)).wait_recv()
        for cp in sends:
            cp.wait_send()
        mine.wait()

    return _comm_call(body, name, _sds((8,) + x.shape, x.dtype), 7, (x,))


def _multi_comm_call(body, name, out_shapes, n_sems, operands):
    anyspec = pl.BlockSpec(memory_space=pl.ANY)
    nin = len(operands)

    def flat_body(*refs):
        body(refs[:nin], refs[nin:nin + len(out_shapes)], refs[-2], refs[-1])

    return _call(flat_body, name=name, out_shape=list(out_shapes), in_specs=[anyspec] * nin,
                 out_specs=[anyspec] * len(out_shapes),
                 scratch_shapes=[pltpu.SemaphoreType.DMA((n_sems,)), pltpu.SemaphoreType.DMA((n_sems,))])(*operands)


def _remote(src, dst, send_sems, recv_sems, k, to):
    return pltpu.make_async_remote_copy(src_ref=src, dst_ref=dst, send_sem=send_sems.at[k], recv_sem=recv_sems.at[k],
                                        device_id=to, device_id_type=MESH)


def _gather_chips_split(xs, name):
    nw = len(xs)

    def body(x_refs, o_refs, send_sems, recv_sems):
        xi, yi, ci = _here()
        me = 2 * xi + yi
        sib = (xi, yi, 1 - ci)
        chips = [(1 - xi, yi), (xi, 1 - yi), (1 - xi, 1 - yi)]
        sends = []
        for i in range(nw):
            for k, (px, py) in enumerate(chips):
                sends.append(_remote(x_refs[i].at[ci], o_refs[i].at[me, ci], send_sems, recv_sems, 3 * i + k,
                                     (px, py, ci)))
        for cp in sends:
            cp.start()
        passed = []
        for i in range(nw):
            for k, (px, py) in enumerate(chips):
                landed = o_refs[i].at[2 * px + py, ci]
                _remote(landed, landed, send_sems, recv_sems, 3 * i + k, (px, py, ci)).wait_recv()
                fwd = _remote(landed, landed, send_sems, recv_sems, 3 * (nw + i) + k, sib)
                fwd.start()
                passed.append(fwd)
        for i in range(nw):
            for k, (px, py) in enumerate(chips):
                other = o_refs[i].at[2 * px + py, 1 - ci]
                _remote(other, other, send_sems, recv_sems, 3 * (nw + i) + k, sib).wait_recv()
        for cp in sends + passed:
            cp.wait_send()

    return _multi_comm_call(body, name, [_sds((4,) + x.shape, x.dtype) for x in xs], 6 * nw, xs)


GATHER_COLLECTIVE_ID = 1


def _gather_chips_split_async(xs, name, collective_id):
    nw = len(xs)
    x_refs = [jax.new_ref(x, memory_space=pltpu.MemorySpace.HBM) for x in xs]
    o_refs = [jax.empty_ref(_sds((4,) + x.shape, x.dtype), memory_space=pltpu.MemorySpace.HBM) for x in xs]

    @pl.kernel(mesh=plsc.ScalarSubcoreMesh(axis_name="sequencer", num_cores=1), name=name,
               scratch_types=(pltpu.SemaphoreType.DMA((6 * nw,)), pltpu.SemaphoreType.DMA((6 * nw,))),
               compiler_params=pltpu.CompilerParams(collective_id=collective_id))
    def launch(send_sems, recv_sems):
        xi, yi, ci = _here()
        me = 2 * xi + yi
        sib = (xi, yi, 1 - ci)
        chips = [(1 - xi, yi), (xi, 1 - yi), (1 - xi, 1 - yi)]
        barrier = pltpu.get_barrier_semaphore()
        for peer in [sib] + [(px, py, ci) for px, py in chips]:
            pl.semaphore_signal(barrier, inc=1, device_id=peer, device_id_type=MESH)
        pl.semaphore_wait(barrier, 4)
        sends = []
        for i in range(nw):
            for k, (px, py) in enumerate(chips):
                sends.append(_remote(x_refs[i].at[ci], o_refs[i].at[me, ci], send_sems, recv_sems, 3 * i + k,
                                     (px, py, ci)))
        for cp in sends:
            cp.start()
        passed = []
        for i in range(nw):
            for k, (px, py) in enumerate(chips):
                landed = o_refs[i].at[2 * px + py, ci]
                _remote(landed, landed, send_sems, recv_sems, 3 * i + k, (px, py, ci)).wait_recv()
                fwd = _remote(landed, landed, send_sems, recv_sems, 3 * (nw + i) + k, sib)
                fwd.start()
                passed.append(fwd)
        for i in range(nw):
            for k, (px, py) in enumerate(chips):
                other = o_refs[i].at[2 * px + py, 1 - ci]
                _remote(other, other, send_sems, recv_sems, 3 * (nw + i) + k, sib).wait_recv()
        for cp in sends + passed:
            cp.wait_send()

    launch()
    return [o[...] for o in o_refs]


def _swap_sibling_half(gs):
    def body(g_refs, o_refs, send_sems, recv_sems):
        xi, yi, ci = _here()
        cps = [_remote(g.at[:, 1 - ci], o, send_sems, recv_sems, i, (xi, yi, 1 - ci))
               for i, (g, o) in enumerate(zip(g_refs, o_refs))]
        for cp in cps:
            cp.start()
        for cp in cps:
            cp.wait()

    return _multi_comm_call(body, "swap_sibling_half", [_sds((g.shape[0],) + g.shape[2:], g.dtype) for g in gs],
                            len(gs), gs)


SWAP_COLLECTIVE_ID = 6


def _swap_sibling_half_async(gs, name, collective_id):
    g_refs = [jax.new_ref(g, memory_space=pltpu.MemorySpace.HBM) for g in gs]
    o_refs = [jax.empty_ref(_sds((g.shape[0],) + g.shape[2:], g.dtype), memory_space=pltpu.MemorySpace.HBM) for g in gs]

    @pl.kernel(mesh=plsc.ScalarSubcoreMesh(axis_name="sequencer", num_cores=1), name=name,
               scratch_types=(pltpu.SemaphoreType.DMA((len(gs),)), pltpu.SemaphoreType.DMA((len(gs),))),
               compiler_params=pltpu.CompilerParams(collective_id=collective_id))
    def launch(send_sems, recv_sems):
        xi, yi, ci = _here()
        sib = (xi, yi, 1 - ci)
        barrier = pltpu.get_barrier_semaphore()
        pl.semaphore_signal(barrier, inc=1, device_id=sib, device_id_type=MESH)
        pl.semaphore_wait(barrier, 1)
        cps = [_remote(g.at[:, 1 - ci], o, send_sems, recv_sems, i, sib) for i, (g, o) in enumerate(zip(g_refs, o_refs))]
        for cp in cps:
            cp.start()
        for cp in cps:
            cp.wait()

    launch()
    return [o[...] for o in o_refs]


SCATTER_COLLECTIVE_ID = 2


def _scatter_chips_async(gps, name, collective_id):
    nw = len(gps)
    g_refs = [jax.new_ref(g, memory_space=pltpu.MemorySpace.HBM) for g in gps]
    o_refs = [jax.empty_ref(_sds(g.shape, g.dtype), memory_space=pltpu.MemorySpace.HBM) for g in gps]

    @pl.kernel(mesh=plsc.ScalarSubcoreMesh(axis_name="sequencer", num_cores=1), name=name,
               scratch_types=(pltpu.SemaphoreType.DMA((3 * nw,)), pltpu.SemaphoreType.DMA((3 * nw,))),
               compiler_params=pltpu.CompilerParams(collective_id=collective_id))
    def launch(send_sems, recv_sems):
        xi, yi, ci = _here()
        me = 2 * xi + yi
        chips = [(1 - xi, yi), (xi, 1 - yi), (1 - xi, 1 - yi)]
        barrier = pltpu.get_barrier_semaphore()
        for px, py in chips:
            pl.semaphore_signal(barrier, inc=1, device_id=(px, py, ci), device_id_type=MESH)
        pl.semaphore_wait(barrier, 3)
        sends = [_remote(g_refs[i].at[2 * px + py], o_refs[i].at[me], send_sems, recv_sems, 3 * i + k, (px, py, ci))
                 for i in range(nw) for k, (px, py) in enumerate(chips)]
        for cp in sends:
            cp.start()
        for i in range(nw):
            for k, (px, py) in enumerate(chips):
                slot = o_refs[i].at[2 * px + py]
                _remote(slot, slot, send_sems, recv_sems, 3 * i + k, (px, py, ci)).wait_recv()
        for cp in sends:
            cp.wait_send()

    launch()
    return [o[...] for o in o_refs]


def _share_sibling(rs):
    def body(r_refs, o_refs, send_sems, recv_sems):
        xi, yi, ci = _here()
        cps = [_remote(r, o, send_sems, recv_sems, i, (xi, yi, 1 - ci)) for i, (r, o) in enumerate(zip(r_refs, o_refs))]
        for cp in cps:
            cp.start()
        for cp in cps:
            cp.wait()

    return _multi_comm_call(body, "share_sibling", [_sds(r.shape, r.dtype) for r in rs], len(rs), rs)


def _flat_pack(arrs, dtype, row_mult):
    flat = jnp.concatenate([a.astype(dtype).reshape(-1) for a in arrs])
    unit = FLAT_W * row_mult
    npad = -(-flat.shape[0] // unit) * unit
    return jnp.pad(flat, (0, npad - flat.shape[0])).reshape(npad // FLAT_W, FLAT_W)


def _flat_unpack(flat2d, shapes):
    flat = flat2d.reshape(-1)
    outs, off = [], 0
    for s in shapes:
        size = int(np.prod(s))
        outs.append(flat[off:off + size].reshape(s))
        off += size
    return outs


def _s5_params(lam_re, lam_im, log_step, b_re, b_im, c_re, c_im, d_skip):
    g, p = lam_re.shape
    hs = b_re.shape[2]
    gt = S5_GROUP_TILE
    jn = g // gt
    lam_re = jnp.minimum(lam_re, -1e-4)
    step = jnp.exp(log_step)[:, None]
    mag = jnp.exp(lam_re * step)
    ab_re = mag * jnp.cos(lam_im * step)
    ab_im = mag * jnp.sin(lam_im * step)
    den = lam_re * lam_re + lam_im * lam_im
    f_re = ((ab_re - 1.0) * lam_re + ab_im * lam_im) / den
    f_im = (ab_im * lam_re - (ab_re - 1.0) * lam_im) / den
    bb_re = f_re[..., None] * b_re - f_im[..., None] * b_im
    bb_im = f_re[..., None] * b_im + f_im[..., None] * b_re
    a_l = jnp.concatenate([ab_re.reshape(jn, gt * p), ab_im.reshape(jn, gt * p)], axis=1).reshape(1, jn * 2 * gt * p)
    eye = jnp.eye(gt, dtype=F32)

    def blockdiag(m):
        return jnp.einsum('jahp,ab->jahbp', m.reshape(jn, gt, hs, p), eye).reshape(jn, gt * hs, gt * p)

    b_blk = jnp.concatenate([blockdiag(bb_re.transpose(0, 2, 1)), blockdiag(bb_im.transpose(0, 2, 1))], axis=2)
    c_blk = jnp.concatenate([blockdiag(c_re), blockdiag(-c_im)], axis=2)
    return a_l, b_blk, c_blk, d_skip.reshape(1, g * hs)


def _s5_scan_consts(a_l, cw):
    jn, hr = a_l.shape[1] // cw, cw // 2 // LANES
    a4 = a_l.reshape(jn, 2, hr, LANES)
    are, aim = a4[:, 0], a4[:, 1]
    flat = lambda u, v: jnp.concatenate([u, v], axis=1).reshape(jn * 2 * hr, LANES)
    return flat(are, are), flat(-aim, aim), flat(aim, -aim)


def _s5_da_lanes(p, q, cw):
    jn, hr = p.shape[0] * LANES // cw, cw // 2 // LANES
    p4, q4 = p.reshape(jn, 2, hr, LANES), q.reshape(jn, 2, hr, LANES)
    da_re = (p4[:, 0] + p4[:, 1]).reshape(jn, cw // 2)
    da_im = (q4[:, 1] - q4[:, 0]).reshape(jn, cw // 2)
    return jnp.concatenate([da_re, da_im], axis=1).reshape(1, jn * cw)


def _layer_dims(p):
    d_model = p['w_out'].shape[1]
    wa = p['proj_a'].shape[0]
    h = p['gdn_a_log'].shape[0]
    wb = p['proj_b'].shape[0]
    wc = p['proj_c'].shape[0]
    hm = p['m2_a_log'].shape[0]
    cdim = p['m2_conv_w'].shape[1]
    width = dict(zip(PROJ_ORDER, (3 * wa, wa, h, h, wb, wb, wc, cdim, hm, d_model, d_model, d_model)))
    n_in = sum(width.values())
    return width, n_in, -(-n_in // LANES) * LANES


PROJ_ORDER = ('qkv', 'az', 'braw', 'araw', 'su', 'sgate', 'cz', 'cxbc', 'cdt', 'ma', 'mb', 'mc')
WORK_ORDER = ('qkv', 'az', 'cz', 'su', 'sgate', 'cxbc', 'ma', 'mb', 'mc', 'braw', 'araw', 'cdt')
ROW_COL_TILE = 512


def _offsets(width, order):
    off, o = {}, 0
    for n in order:
        off[n] = o
        o += width[n]
    return off


def _virtual_cols(parts, lo, hi):
    wd = parts[0].shape[-1]
    out = []
    while lo < hi:
        j = lo // wd
        stop = min(hi, (j + 1) * wd)
        out.append(lax.slice_in_dim(parts[j], lo - j * wd, stop - j * wd, axis=parts[j].ndim - 1))
        lo = stop
    return out


def _work_order_weight(shards, width, n_pad):
    off = _offsets(width, PROJ_ORDER)
    parts = [s for n in WORK_ORDER for s in _virtual_cols(shards, off[n], off[n] + width[n])]
    used = sum(width.values())
    parts.append(jnp.zeros(shards[0].shape[:-1] + (n_pad - used,), shards[0].dtype))
    return jnp.concatenate(parts, axis=-1)


def _weight_order_shard(gw, width, lo, hi):
    src, dst = _offsets(width, WORK_ORDER), _offsets(width, PROJ_ORDER)
    parts = []
    for n in PROJ_ORDER:
        a, b = max(lo, dst[n]), min(hi, dst[n] + width[n])
        if a < b:
            parts.append(lax.slice_in_dim(gw, src[n] + a - dst[n], src[n] + b - dst[n], axis=gw.ndim - 1))
    return jnp.concatenate(parts, axis=-1)


def _layer_fwd(x, p):
    width, n_in, n_pad = _layer_dims(p)
    sv = {'x': x}
    h = _rowwise(_f_rms, [x], [p['norm_w'][None]], [x.shape[1]], [BF16], "rms_fwd")[0]
    w_in = _work_order_weight(p['w_in_shards'], width, n_pad)
    proj = _matmul(h, w_in, 'nn', F32, "in_proj", tn=IN_PROJ_TILE)
    wins, off = {}, 0
    for n in WORK_ORDER:
        wins[n] = Win(proj, off, width[n])
        off += width[n]
    qkv, az, cz, su, sgate, cxbc, ma, mb, mc = (wins[n] for n in WORK_ORDER[:9])
    braw, araw, cdt = (lax.slice_in_dim(proj, wins[n].off, wins[n].off + width[n], axis=1) for n in WORK_ORDER[9:])
    sv.update(h=h, w_in=w_in, qkv=qkv, az=az, braw=braw, araw=araw, su=su, sgate=sgate, cz=cz, cxbc=cxbc, cdt=cdt,
              ma=ma, mb=mb, mc=mc)
    gb0 = jnp.zeros((1, qkv.shape[1]), F32)
    sqkv = _conv_fwd(qkv, p['gdn_conv_w'], gb0, "gdn_conv_fwd")
    ya, ssa, tsa = _gdn_chunks_fwd(sqkv, az, braw, araw, p['gdn_a_log'][None], p['gdn_dt_bias'][None],
                                   p['gdn_norm_w'][None])
    sv.update(sqkv=sqkv, ssa=ssa, tsa=tsa, ya=ya)
    s5_in = tuple(p[k] for k in ('s5_lam_re', 's5_lam_im', 's5_log_step', 's5_b_re', 's5_b_im', 's5_c_re', 's5_c_im',
                                 's5_d'))
    (a_l, b_blk, c_blk, d_l), s5_vjp = jax.vjp(_s5_params, *s5_in)
    cw = b_blk.shape[2]
    aa3, bb3, bb3_conj = _s5_scan_consts(a_l, cw)
    bu = _bd_expand(su, b_blk, "s5_bu")
    s = _s5_scan(bu, aa3, bb3, False, "s5_scan_fwd")
    ymm = _bd_reduce(s, c_blk, F32, "s5_out")
    yg = _rowwise(_f_s5_post1, [ymm, su], [d_l], [su.shape[1]], [F32], "s5_post1_fwd", col_tile=ROW_COL_TILE)[0]
    tt = _matmul(yg, p['s5_glu_w'], 'nn', F32, "s5_glu")
    yb = _rowwise(_f_s5_post2, [yg, tt, sgate], [p['s5_glu_b'][None]], [su.shape[1]], [F32], "s5_post2_fwd",
                  col_tile=ROW_COL_TILE)[0]
    sv.update(aa3=aa3, bb3_conj=bb3_conj, b_blk=b_blk, c_blk=c_blk, d_l=d_l, s5_vjp=s5_vjp, s=s, ymm=ymm, yg=yg, tt=tt,
              yb=yb, cw=cw)
    sxbc = _conv_fwd(cxbc, p['m2_conv_w'], p['m2_conv_b'][None], "m2_conv_fwd")
    yc, ssc = _m2_chunks_fwd(sxbc, cz, cdt, p['m2_a_log'][None], p['m2_dt_bias'][None], p['m2_d'][None],
                             p['m2_norm_w'][None])
    sv.update(sxbc=sxbc, ssc=ssc, yc=yc)
    pa = _matmul(ya, p['proj_a'], 'nn', F32, "proj_a")
    pb = _matmul(yb, p['proj_b'], 'nn', F32, "proj_b")
    pc = _matmul(yc, p['proj_c'], 'nn', F32, "proj_c")
    merged = _rowwise(_f_merge, [ma, mb, mc, pa, pb, pc], [], [x.shape[1]], [BF16], "merge_fwd",
                      col_tile=ROW_COL_TILE)[0]
    x_next = _matmul(merged, p['w_out'], 'nn', F32, "out_proj", add=x)
    sv.update(pa=pa, pb=pb, pc=pc, merged=merged)
    return x_next, sv


def _layer_bwd(dx_out, p, sv):
    width, n_in, n_pad = _layer_dims(p)
    g = {}
    dmerged = _matmul(dx_out, p['w_out'], 'nt', F32, "out_proj_dx")
    g['w_out'] = _matmul(sv['merged'], dx_out, 'tn', F32, "out_proj_dw")
    dma, dmb, dmc, dpa, dpb, dpc = _rowwise_bwd(
        _f_merge, [sv['ma'], sv['mb'], sv['mc'], sv['pa'], sv['pb'], sv['pc']], [], [dmerged], [BF16] * 6,
        "merge_bwd", col_tile=ROW_COL_TILE)
    dya = _matmul(dpa, p['proj_a'], 'nt', F32, "proj_a_dx")
    dyb = _matmul(dpb, p['proj_b'], 'nt', F32, "proj_b_dx")
    dyc = _matmul(dpc, p['proj_c'], 'nt', F32, "proj_c_dx")
    g['proj_a'] = _matmul(sv['ya'], dpa, 'tn', F32, "proj_a_dw")
    g['proj_b'] = _matmul(sv['yb'], dpb, 'tn', F32, "proj_b_dw")
    g['proj_c'] = _matmul(sv['yc'], dpc, 'tn', F32, "proj_c_dw")
    alog, dtb, gnw = p['gdn_a_log'][None], p['gdn_dt_bias'][None], p['gdn_norm_w'][None]
    dsq, daz, db3, da3, dalog, ddtb, dgnw = _gdn_chunks_bwd(sv['sqkv'], sv['az'], sv['braw'], sv['araw'], alog, dtb, gnw,
                                                            sv['ssa'], sv['tsa'], dya)
    gb0 = jnp.zeros((1, sv['qkv'].shape[1]), F32)
    dqkv, g['gdn_conv_w'], _ = _conv_bwd(sv['qkv'], p['gdn_conv_w'], gb0, dsq, "gdn_conv_bwd")
    dbraw, daraw = jnp.sum(db3, axis=0), jnp.sum(da3, axis=0)
    g.update(gdn_a_log=dalog[0], gdn_dt_bias=ddtb[0], gdn_norm_w=dgnw[0])
    dsx, dcz, dcdt, dmalog, dmdtb, dmdsk, dmnw = _m2_chunks_bwd(
        sv['sxbc'], sv['cz'], sv['cdt'], p['m2_a_log'][None], p['m2_dt_bias'][None], p['m2_d'][None],
        p['m2_norm_w'][None], sv['ssc'], dyc)
    dcxbc, g['m2_conv_w'], dconvb = _conv_bwd(sv['cxbc'], p['m2_conv_w'], p['m2_conv_b'][None], dsx, "m2_conv_bwd")
    g.update(m2_conv_b=dconvb[0], m2_a_log=dmalog[0], m2_dt_bias=dmdtb[0], m2_d=dmdsk[0], m2_norm_w=dmnw[0])
    dyg1, dtt, dsgate, dglub = _rowwise_bwd(_f_s5_post2, [sv['yg'], sv['tt'], sv['sgate']], [p['s5_glu_b'][None]],
                                            [dyb], [F32, BF16, BF16], "s5_post2_bwd", col_tile=ROW_COL_TILE)
    dyg = _matmul(dtt, p['s5_glu_w'], 'nt', F32, "s5_glu_dx", add=dyg1)
    g['s5_glu_w'] = _matmul(sv['yg'], dtt, 'tn', F32, "s5_glu_dw")
    g['s5_glu_b'] = dglub[0]
    dymm, dsu1, dd_l = _rowwise_bwd(_f_s5_post1, [sv['ymm'], sv['su']], [sv['d_l']], [dyg], [BF16, F32],
                                    "s5_post1_bwd", col_tile=ROW_COL_TILE)
    gy = _bd_expand(dymm, sv['c_blk'], "s5_out_dx")
    ds = _s5_scan(gy, sv['aa3'], sv['bb3_conj'], True, "s5_scan_bwd")
    da_l = _s5_da_lanes(*_s5_da(ds, sv['s']), sv['cw'])
    dsu = _bd_reduce(ds, sv['b_blk'], BF16, "s5_bu_dx", add=dsu1)
    ka = sv['b_blk'].shape[1]
    db_blk = _bd_outer(sv['su'], ds, ka, "s5_bu_dw")
    dc_blk = _bd_outer(dymm, sv['s'], ka, "s5_out_dw")
    for k, v in zip(('s5_lam_re', 's5_lam_im', 's5_log_step', 's5_b_re', 's5_b_im', 's5_c_re', 's5_c_im', 's5_d'),
                    sv['s5_vjp']((da_l, db_blk, dc_blk, dd_l))):
        g[k] = v
    small = jnp.concatenate([dbraw, daraw, dcdt], axis=1).astype(BF16)
    small = jnp.pad(small, ((0, 0), (0, n_pad - n_in + sum(width[n] for n in WORK_ORDER[9:]) - small.shape[1])))
    dproj = _concat_cols([dqkv, daz, dcz, dsu, dsgate, dcxbc, dma, dmb, dmc, small], "concat_dproj")
    dh = _matmul(dproj, sv['w_in'], 'nt', F32, "in_proj_dx", tm=2048, tk=IN_PROJ_TILE)
    g['w_in'] = _matmul(sv['h'], dproj, 'tn', F32, "in_proj_dw", tn=IN_PROJ_TILE)
    dx, dnw = _rowwise_bwd(_f_rms, [sv['x']], [p['norm_w'][None]], [dh], [F32], "rms_bwd", addend=dx_out)
    g['norm_w'] = dnw[0]
    return dx, g


INPUT_NAMES = (['x'] + WEIGHT_NAMES + ['loss_target'] + ['m_' + n for n in WEIGHT_NAMES]
               + ['v_' + n for n in WEIGHT_NAMES])


def _step(d):
    xi, yi, ci = _here()
    me = 2 * xi + yi
    depth = d['norm_w'].shape[0]
    big, ssm = list(BIG), list(SHARDED_SMALL)
    nsh = 4
    full, gathered = {}, {}
    gots = None
    for first, stop in ((0, 1), (1, depth)):
        halves = [d[n][first:stop].astype(BF16).reshape(2, -1, d[n].shape[-1]) for n in big]
        if first == 0:
            gots = _gather_chips_split(halves, "gather_weights")
        else:
            halves, gots = lax.optimization_barrier((halves, gots))
            gots = _gather_chips_split_async(halves, "gather_weights_async", GATHER_COLLECTIVE_ID)
        for n, hv, got in zip(big, halves, gots):
            got = lax.dynamic_update_slice(got, hv[None], (me, 0, 0, 0))
            gathered[n, first] = got.reshape((nsh, stop - first) + d[n].shape[1:])
    cg = _gather_chips(_flat_pack([d[n] for n in ssm], F32, 8), "gather_conv_weights")
    parts = [_flat_unpack(cg[j], [d[n].shape for n in ssm]) for j in range(nsh)]
    for i, n in enumerate(ssm):
        full[n] = jnp.concatenate([parts[j][i] for j in range(nsh)], axis=SHARDED_SMALL[n])
    layer_names = [n for n in WEIGHT_NAMES if n != 'final_norm_w']

    def layer_params(l):
        p = {n: (full[n][l] if n in full else d[n][l]) for n in layer_names if n not in BIG}
        for n in big:
            got = gathered[n, 0][:, 0] if l == 0 else gathered[n, 1][:, l - 1]
            if n == 'w_in':
                p['w_in_shards'] = [got[j] for j in range(nsh)]
            else:
                p[n] = jnp.concatenate([got[j] for j in range(nsh)], axis=BIG[n] - 1)
        return p

    x = d['x'][0]
    saved, params = [], []
    for l in range(depth):
        params.append(layer_params(l))
        x, sv = _layer_fwd(x, params[l])
        saved.append(sv)
    loss11, dx, dfw = _loss_and_grad(x, d['loss_target'][0], d['final_norm_w'][None])
    loss = lax.psum(loss11[0, 0], ("x", "y", "c"))
    width_in = _layer_dims(params[0])[0]

    def shard(a, n, j):
        wd = d[n].shape[BIG[n]]
        if n == 'w_in':
            return _weight_order_shard(a, width_in, j * wd, (j + 1) * wd)
        return lax.slice_in_dim(a, j * wd, (j + 1) * wd, axis=BIG[n] - 1)

    c_idx = jnp.reshape(ci, (1,)).astype(jnp.int32)
    me_idx = jnp.reshape(me, (1,)).astype(jnp.int32)
    grads, own, got = [None] * depth, [None] * depth, [None] * depth
    def pair_and_scatter(l, g4, recv):
        pairs = [_add_my_half(g, r, c_idx, "add_my_half_" + n) for n, g, r in zip(big, g4, recv)]
        own[l] = [pf for pf, _ in pairs]
        got[l] = _scatter_chips_async([pb for _, pb in pairs], "scatter_chips_async_%d" % l, SCATTER_COLLECTIVE_ID + l)

    behind = None
    for l in reversed(range(depth)):
        dx, grads[l] = _layer_bwd(dx, params[l], saved[l])
        g4 = [jnp.stack([shard(grads[l][n], n, j) for j in range(nsh)]).reshape(nsh, 2, -1, d[n].shape[-1])
              for n in big]
        recv = _swap_sibling_half_async(g4, "swap_sibling_half_async_%d" % l, SWAP_COLLECTIVE_ID + l)
        if behind is not None:
            lb, g4b, recvb = behind
            recvb, dx = lax.optimization_barrier((recvb, dx))
            pair_and_scatter(lb, g4b, recvb)
        behind = (l, g4, recv)
    pair_and_scatter(*behind)
    gfull = {n: jnp.stack([grads[l][n] for l in range(depth)]) for n in layer_names if n not in BIG}
    gfull['final_norm_w'] = dfw[0]
    out = {}
    small = [n for n in WEIGHT_NAMES if n not in BIG]
    sshapes = [gfull[n].shape for n in small]
    gsm = _sum_parts(_gather_all(_flat_pack([gfull[n] for n in small], F32, 8), "gather_small_grads"), "sum_devices")
    gs = dict(zip(small, _flat_unpack(gsm, sshapes)))
    for n in ssm:
        wd = d[n].shape[SHARDED_SMALL[n]]
        gs[n] = lax.dynamic_slice_in_dim(gs[n], me * wd, wd, axis=SHARDED_SMALL[n])
    lshapes = [d[n].shape for n in small]
    wps, gps, mps, vps = (_flat_pack(arrs, F32, 16) for arrs in (
        [d[n] for n in small], [gs[n] for n in small], [d['m_' + n] for n in small], [d['v_' + n] for n in small]))
    dl, nm, nv = _adamw(wps, gps, mps, vps, "adamw_small")
    for key, arr in (('grad_', gps), ('delta_', dl), ('new_m_', nm), ('new_v_', nv)):
        for n, a in zip(small, _flat_unpack(arr, lshapes)):
            out[key + n] = a
    got[1:], dx = lax.optimization_barrier((got[1:], dx))
    mine = [None] * (depth * len(big))
    for l in list(range(1, depth)) + [0]:
        if l == 0:
            got[0], _ = lax.optimization_barrier((got[0], (dl, mine[len(big):])))
        for i, (n, pf, gt) in enumerate(zip(big, own[l], got[l])):
            mine[l * len(big) + i] = _sum_chips(pf, gt, me_idx, "sum_chips_" + n)
    theirs = _share_sibling(mine)
    for i, n in enumerate(big):
        layers = []
        for l in range(depth):
            mn, th = mine[l * len(big) + i], theirs[l * len(big) + i]
            layers.append(jnp.where(ci == 0, jnp.stack([mn, th]), jnp.stack([th, mn])))
        g2 = jnp.stack(layers).reshape(-1, d[n].shape[-1])
        w2, m2, v2 = (d[pre + n].reshape(g2.shape) for pre in ('', 'm_', 'v_'))
        dl, nm, nv = _adamw(w2, g2, m2, v2, "adamw_" + n)
        for key, arr in (('grad_', g2), ('delta_', dl), ('new_m_', nm), ('new_v_', nv)):
            out[key + n] = arr.reshape(d[n].shape)
    res = [loss, dx[None]]
    for key in ('grad_', 'delta_', 'new_m_', 'new_v_'):
        res += [out[key + n] for n in WEIGHT_NAMES]
    return tuple(res)


def kernel(x, norm_w, w_in, gdn_conv_w, gdn_a_log, gdn_dt_bias, gdn_norm_w, s5_lam_re, s5_lam_im, s5_log_step, s5_b_re, s5_b_im, s5_c_re, s5_c_im, s5_d, s5_glu_w, s5_glu_b, m2_conv_w, m2_conv_b, m2_a_log, m2_dt_bias, m2_d, m2_norm_w, proj_a, proj_b, proj_c, w_out, final_norm_w, loss_target, m_norm_w, m_w_in, m_gdn_conv_w, m_gdn_a_log, m_gdn_dt_bias, m_gdn_norm_w, m_s5_lam_re, m_s5_lam_im, m_s5_log_step, m_s5_b_re, m_s5_b_im, m_s5_c_re, m_s5_c_im, m_s5_d, m_s5_glu_w, m_s5_glu_b, m_m2_conv_w, m_m2_conv_b, m_m2_a_log, m_m2_dt_bias, m_m2_d, m_m2_norm_w, m_proj_a, m_proj_b, m_proj_c, m_w_out, m_final_norm_w, v_norm_w, v_w_in, v_gdn_conv_w, v_gdn_a_log, v_gdn_dt_bias, v_gdn_norm_w, v_s5_lam_re, v_s5_lam_im, v_s5_log_step, v_s5_b_re, v_s5_b_im, v_s5_c_re, v_s5_c_im, v_s5_d, v_s5_glu_w, v_s5_glu_b, v_m2_conv_w, v_m2_conv_b, v_m2_a_log, v_m2_dt_bias, v_m2_d, v_m2_norm_w, v_proj_a, v_proj_b, v_proj_c, v_w_out, v_final_norm_w):
    args = (x, norm_w, w_in, gdn_conv_w, gdn_a_log, gdn_dt_bias, gdn_norm_w, s5_lam_re, s5_lam_im, s5_log_step, s5_b_re, s5_b_im, s5_c_re, s5_c_im, s5_d, s5_glu_w, s5_glu_b, m2_conv_w, m2_conv_b, m2_a_log, m2_dt_bias, m2_d, m2_norm_w, proj_a, proj_b, proj_c, w_out, final_norm_w, loss_target, m_norm_w, m_w_in, m_gdn_conv_w, m_gdn_a_log, m_gdn_dt_bias, m_gdn_norm_w, m_s5_lam_re, m_s5_lam_im, m_s5_log_step, m_s5_b_re, m_s5_b_im, m_s5_c_re, m_s5_c_im, m_s5_d, m_s5_glu_w, m_s5_glu_b, m_m2_conv_w, m_m2_conv_b, m_m2_a_log, m_m2_dt_bias, m_m2_d, m_m2_norm_w, m_proj_a, m_proj_b, m_proj_c, m_w_out, m_final_norm_w, v_norm_w, v_w_in, v_gdn_conv_w, v_gdn_a_log, v_gdn_dt_bias, v_gdn_norm_w, v_s5_lam_re, v_s5_lam_im, v_s5_log_step, v_s5_b_re, v_s5_b_im, v_s5_c_re, v_s5_c_im, v_s5_d, v_s5_glu_w, v_s5_glu_b, v_m2_conv_w, v_m2_conv_b, v_m2_a_log, v_m2_dt_bias, v_m2_d, v_m2_norm_w, v_proj_a, v_proj_b, v_proj_c, v_w_out, v_final_norm_w)
    return _step(dict(zip(INPUT_NAMES, args)))
```
